```python
import jax, jax.numpy as jnp
from jax import lax
import numpy as np

D_MODEL = 2048
BATCH = 8
SEQ = 2048
DEPTH = 1

CHUNK = 64
HEAD_DIM = 128
N_HEADS_SB = 8
N_HEADS_CA = 8
W_SB = N_HEADS_SB * HEAD_DIM
W_CA = N_HEADS_CA * HEAD_DIM
LEFT_CHUNKS = 8
BAND = (LEFT_CHUNKS + 1) * CHUNK
REL_CLIP = 128
N_REL = REL_CLIP + CHUNK
Q_BLOCK = 128
D_FF = -(-8 * D_MODEL // (3 * 256)) * 256
D_PLE = 256
EPS = 1e-6
NEG = -1e30
IN_COLS = 3 * W_SB + 3 * W_CA + 2 * D_MODEL

kernel_name = "hybrid_stickbreak_chunkrel_block"


def rmsnorm(x, g):
    xf = x.astype(jnp.float32)
    y = xf * lax.rsqrt(jnp.mean(xf * xf, axis=-1, keepdims=True) + EPS)
    return (y * g.astype(jnp.float32)).astype(x.dtype)


def stick_breaking_attention(q, k, v):
    B, S, H, Dh = q.shape
    scale = Dh ** -0.5
    outs = []
    for qb in range(S // Q_BLOCK):
        t0 = qb * Q_BLOCK
        t1 = t0 + Q_BLOCK
        kb = k[:, :t1]
        vb = v[:, :t1]
        z = jnp.einsum('bqhd,bkhd->bhqk', q[:, t0:t1], kb).astype(jnp.float32) * scale
        past = jnp.arange(t1)[None, :] < jnp.arange(t0, t1)[:, None]
        log_keep = jnp.where(past, jax.nn.log_sigmoid(-z), 0.0)
        between = lax.cumsum(log_keep, axis=3, reverse=True) - log_keep
        a = jnp.where(past, jnp.exp(jax.nn.log_sigmoid(z) + between), 0.0)
        outs.append(jnp.einsum('bhqk,bkhd->bqhd', a.astype(v.dtype), vb))
    return jnp.concatenate(outs, axis=1)


def chunked_relpos_attention(q, k, v, rel_bias):
    B, S, H, Dh = q.shape
    nc = S // CHUNK
    pad = LEFT_CHUNKS * CHUNK
    scale = Dh ** -0.5
    kp = jnp.pad(k, ((0, 0), (pad, 0), (0, 0), (0, 0)))
    vp = jnp.pad(v, ((0, 0), (pad, 0), (0, 0), (0, 0)))
    qc = jnp.moveaxis(q.reshape(B, nc, CHUNK, H, Dh), 1, 0)
    s_loc = jnp.arange(BAND)[None, :]
    rel = s_loc - (jnp.arange(CHUNK)[:, None] + pad)
    rel_idx = jnp.clip(rel, -REL_CLIP, CHUNK - 1) + REL_CLIP
    bias = rel_bias.astype(jnp.float32)[:, rel_idx]

    def one_chunk(args):
        c, qblk = args
        start = c * CHUNK
        kb = lax.dynamic_slice_in_dim(kp, start, BAND, axis=1)
        vb = lax.dynamic_slice_in_dim(vp, start, BAND, axis=1)
        valid = (start + s_loc) >= pad
        z = jnp.einsum('bqhd,bkhd->bhqk', qblk, kb).astype(jnp.float32) * scale + bias
        w = jax.nn.softmax(jnp.where(valid, z, NEG), axis=-1)
        return jnp.einsum('bhqk,bkhd->bqhd', w.astype(v.dtype), vb)

    out = lax.map(one_chunk, (jnp.arange(nc), qc))
    return jnp.moveaxis(out, 0, 1).reshape(B, S, H, Dh)


def _fwd_setup_inputs(seed: int = 0) -> dict:
    key = jax.random.key(seed)
    ks = jax.random.split(key, 16)
    f32 = jnp.float32

    def w(k, shape, fan_in):
        return jax.random.normal(k, shape, f32) * fan_in ** -0.5

    def gain(k, shape):
        return 1.0 + 0.05 * jax.random.normal(k, shape, f32)

    return {
        "x": jax.random.normal(ks[0], (BATCH, SEQ, D_MODEL), f32),
        "p": jax.random.normal(ks[1], (DEPTH, BATCH, SEQ, D_PLE), f32),
        "w_in": w(ks[2], (DEPTH, D_MODEL, IN_COLS), D_MODEL),
        "w_sb_out": w(ks[3], (DEPTH, W_SB, D_MODEL), W_SB),
        "w_ca_out": w(ks[4], (DEPTH, W_CA, D_MODEL), W_CA),
        "w_mix_out": w(ks[5], (DEPTH, D_MODEL, D_MODEL), D_MODEL),
        "rel_bias": 0.3 * jax.random.normal(ks[6], (DEPTH, N_HEADS_CA, N_REL), f32),
        "g_mix": gain(ks[7], (DEPTH, D_MODEL)),
        "g_ffn": gain(ks[8], (DEPTH, D_MODEL)),
        "g_ple": gain(ks[9], (DEPTH, D_MODEL)),
        "g_final": gain(ks[10], (D_MODEL,)),
        "w_ffn_in": w(ks[11], (DEPTH, D_MODEL, 2 * D_FF), D_MODEL),
        "w_ffn_out": w(ks[12], (DEPTH, D_FF, D_MODEL), D_FF),
        "w_ple_in": w(ks[13], (DEPTH, D_PLE, D_MODEL), D_PLE),
        "w_ple_gate": w(ks[14], (DEPTH, D_MODEL, D_MODEL), D_MODEL),
    }


def _fwd_reference(x, p, w_in, w_sb_out, w_ca_out, w_mix_out, rel_bias, g_mix, g_ffn, g_ple, g_final,
              w_ffn_in, w_ffn_out, w_ple_in, w_ple_gate):
    B, S, _ = x.shape
    splits = [W_SB, 2 * W_SB, 3 * W_SB, 3 * W_SB + W_CA, 3 * W_SB + 2 * W_CA, 3 * W_SB + 3 * W_CA,
              3 * W_SB + 3 * W_CA + D_MODEL]
    for i in range(DEPTH):
        h = rmsnorm(x, g_mix[i])
        proj = h @ w_in[i]
        q_sb, k_sb, v_sb, q_ca, k_ca, v_ca, gate_sb, gate_ca = jnp.split(proj, splits, axis=-1)
        y_sb = stick_breaking_attention(q_sb.reshape(B, S, N_HEADS_SB, HEAD_DIM),
                                        k_sb.reshape(B, S, N_HEADS_SB, HEAD_DIM),
                                        v_sb.reshape(B, S, N_HEADS_SB, HEAD_DIM)).reshape(B, S, W_SB)
        y_ca = chunked_relpos_attention(q_ca.reshape(B, S, N_HEADS_CA, HEAD_DIM),
                                        k_ca.reshape(B, S, N_HEADS_CA, HEAD_DIM),
                                        v_ca.reshape(B, S, N_HEADS_CA, HEAD_DIM),
                                        rel_bias[i]).reshape(B, S, W_CA)
        merged = (jax.nn.sigmoid(gate_sb) * (y_sb @ w_sb_out[i])
                  + jax.nn.sigmoid(gate_ca) * (y_ca @ w_ca_out[i]))
        x = x + merged @ w_mix_out[i]
        h = rmsnorm(x, g_ffn[i])
        g_ff, u_ff = jnp.split(h @ w_ffn_in[i], 2, axis=-1)
        x = x + (jax.nn.silu(g_ff) * u_ff) @ w_ffn_out[i]
        h = rmsnorm(x, g_ple[i])
        x = x + jax.nn.sigmoid(h @ w_ple_gate[i]) * (p[i] @ w_ple_in[i])
    return rmsnorm(x, g_final)


import jax as _jax
import jax.numpy as _jnp

TWIN_FORMAT = 'train_step'
FWD_PARAMS = ['x', 'p', 'w_in', 'w_sb_out', 'w_ca_out', 'w_mix_out', 'rel_bias', 'g_mix', 'g_ffn', 'g_ple', 'g_final', 'w_ffn_in', 'w_ffn_out', 'w_ple_in', 'w_ple_gate']
TWIN_WEIGHTS = ['w_in', 'w_sb_out', 'w_ca_out', 'w_mix_out', 'rel_bias', 'g_mix', 'g_ffn', 'g_ple', 'g_final', 'w_ffn_in', 'w_ffn_out', 'w_ple_in', 'w_ple_gate']
TWIN_DIFF_INPUT = 'x'
TWIN_INPUTS = ['x', 'p', 'w_in', 'w_sb_out', 'w_ca_out', 'w_mix_out', 'rel_bias', 'g_mix', 'g_ffn', 'g_ple', 'g_final', 'w_ffn_in', 'w_ffn_out', 'w_ple_in', 'w_ple_gate', 'loss_target', 'm_w_in', 'm_w_sb_out', 'm_w_ca_out', 'm_w_mix_out', 'm_rel_bias', 'm_g_mix', 'm_g_ffn', 'm_g_ple', 'm_g_final', 'm_w_ffn_in', 'm_w_ffn_out', 'm_w_ple_in', 'm_w_ple_gate', 'v_w_in', 'v_w_sb_out', 'v_w_ca_out', 'v_w_mix_out', 'v_rel_bias', 'v_g_mix', 'v_g_ffn', 'v_g_ple', 'v_g_final', 'v_w_ffn_in', 'v_w_ffn_out', 'v_w_ple_in', 'v_w_ple_gate']
TWIN_OUTPUTS = ['loss', 'grad_x', 'grad_w_in', 'grad_w_sb_out', 'grad_w_ca_out', 'grad_w_mix_out', 'grad_rel_bias', 'grad_g_mix', 'grad_g_ffn', 'grad_g_ple', 'grad_g_final', 'grad_w_ffn_in', 'grad_w_ffn_out', 'grad_w_ple_in', 'grad_w_ple_gate', 'delta_w_in', 'delta_w_sb_out', 'delta_w_ca_out', 'delta_w_mix_out', 'delta_rel_bias', 'delta_g_mix', 'delta_g_ffn', 'delta_g_ple', 'delta_g_final', 'delta_w_ffn_in', 'delta_w_ffn_out', 'delta_w_ple_in', 'delta_w_ple_gate', 'new_m_w_in', 'new_m_w_sb_out', 'new_m_w_ca_out', 'new_m_w_mix_out', 'new_m_rel_bias', 'new_m_g_mix', 'new_m_g_ffn', 'new_m_g_ple', 'new_m_g_final', 'new_m_w_ffn_in', 'new_m_w_ffn_out', 'new_m_w_ple_in', 'new_m_w_ple_gate', 'new_v_w_in', 'new_v_w_sb_out', 'new_v_w_ca_out', 'new_v_w_mix_out', 'new_v_rel_bias', 'new_v_g_mix', 'new_v_g_ffn', 'new_v_g_ple', 'new_v_g_final', 'new_v_w_ffn_in', 'new_v_w_ffn_out', 'new_v_w_ple_in', 'new_v_w_ple_gate']
TWIN_LEAF_KINDS = {'loss': 'loss', 'grad_x': 'grad_x', 'grad_w_in': 'grad_w', 'grad_w_sb_out': 'grad_w', 'grad_w_ca_out': 'grad_w', 'grad_w_mix_out': 'grad_w', 'grad_rel_bias': 'grad_w', 'grad_g_mix': 'grad_w', 'grad_g_ffn': 'grad_w', 'grad_g_ple': 'grad_w', 'grad_g_final': 'grad_w', 'grad_w_ffn_in': 'grad_w', 'grad_w_ffn_out': 'grad_w', 'grad_w_ple_in': 'grad_w', 'grad_w_ple_gate': 'grad_w', 'delta_w_in': 'delta_w', 'delta_w_sb_out': 'delta_w', 'delta_w_ca_out': 'delta_w', 'delta_w_mix_out': 'delta_w', 'delta_rel_bias': 'delta_w', 'delta_g_mix': 'delta_w', 'delta_g_ffn': 'delta_w', 'delta_g_ple': 'delta_w', 'delta_g_final': 'delta_w', 'delta_w_ffn_in': 'delta_w', 'delta_w_ffn_out': 'delta_w', 'delta_w_ple_in': 'delta_w', 'delta_w_ple_gate': 'delta_w', 'new_m_w_in': 'new_m', 'new_m_w_sb_out': 'new_m', 'new_m_w_ca_out': 'new_m', 'new_m_w_mix_out': 'new_m', 'new_m_rel_bias': 'new_m', 'new_m_g_mix': 'new_m', 'new_m_g_ffn': 'new_m', 'new_m_g_ple': 'new_m', 'new_m_g_final': 'new_m', 'new_m_w_ffn_in': 'new_m', 'new_m_w_ffn_out': 'new_m', 'new_m_w_ple_in': 'new_m', 'new_m_w_ple_gate': 'new_m', 'new_v_w_in': 'new_v', 'new_v_w_sb_out': 'new_v', 'new_v_w_ca_out': 'new_v', 'new_v_w_mix_out': 'new_v', 'new_v_rel_bias': 'new_v', 'new_v_g_mix': 'new_v', 'new_v_g_ffn': 'new_v', 'new_v_g_ple': 'new_v', 'new_v_g_final': 'new_v', 'new_v_w_ffn_in': 'new_v', 'new_v_w_ffn_out': 'new_v', 'new_v_w_ple_in': 'new_v', 'new_v_w_ple_gate': 'new_v'}


def _forward(args):
    return _fwd_reference(*[args[k] for k in FWD_PARAMS])


def _output_shape():
    out = _jax.eval_shape(lambda: _forward(_fwd_setup_inputs(0)))
    return out.shape, out.dtype

N_MICROBATCH = 1
ADAM_LR = 0.001
ADAM_B1 = 0.9
ADAM_B2 = 0.999
ADAM_EPS = 1e-08
ADAM_WD = 0.01
ADAM_STEP = 10
PER_EXAMPLE_BATCH_AXIS = {'x': 0, 'p': 1, 'loss_target': 0}
SHARED_INPUTS = []
_WEIGHT_DTYPES = {'w_in': _jnp.float32, 'w_sb_out': _jnp.float32, 'w_ca_out': _jnp.float32, 'w_mix_out': _jnp.float32, 'rel_bias': _jnp.float32, 'g_mix': _jnp.float32, 'g_ffn': _jnp.float32, 'g_ple': _jnp.float32, 'g_final': _jnp.float32, 'w_ffn_in': _jnp.float32, 'w_ffn_out': _jnp.float32, 'w_ple_in': _jnp.float32, 'w_ple_gate': _jnp.float32}
MOMENT_SCALE = {'w_in': 1.270550e-02, 'w_sb_out': 2.184159e-02, 'w_ca_out': 5.065257e-03, 'w_mix_out': 2.204280e-02, 'rel_bias': 5.324104e-03, 'g_mix': 2.853047e-02, 'g_ffn': 4.319636e-02, 'g_ple': 1.008324e-02, 'g_final': 8.010373e+00, 'w_ffn_in': 1.765900e-02, 'w_ffn_out': 2.885114e-02, 'w_ple_in': 2.615895e-02, 'w_ple_gate': 1.002776e-02}


def _to_microbatches(a, axis):
    t = _jnp.moveaxis(a, axis, 0)
    t = t.reshape((N_MICROBATCH, t.shape[0] // N_MICROBATCH) + t.shape[1:])
    return _jnp.moveaxis(t, 1, axis + 1)


def setup_inputs(seed: int = 0) -> dict:
    inp = _fwd_setup_inputs(seed)
    key = _jax.random.fold_in(_jax.random.key(seed), 7919)
    shape, _ = _output_shape()
    out = dict(inp)
    out["loss_target"] = _jax.random.normal(_jax.random.fold_in(key, 0), shape, _jnp.float32)
    for i, name in enumerate(TWIN_WEIGHTS):
        w = inp[name].astype(_jnp.float32)
        if MOMENT_SCALE is None:
            s = _jnp.sqrt(_jnp.mean(_jnp.square(w)) + 1e-30)
        else:
            s = MOMENT_SCALE[name]
        km, kv = _jax.random.split(_jax.random.fold_in(key, i + 1))
        out[name] = w
        out["m_" + name] = s * _jax.random.normal(km, w.shape, _jnp.float32)
        out["v_" + name] = (s * s) * _jax.random.uniform(kv, w.shape, _jnp.float32, 0.5, 1.5)
    if N_MICROBATCH > 1:
        for name, axis in PER_EXAMPLE_BATCH_AXIS.items():
            out[name] = _to_microbatches(out[name], axis)
    return {'x': out['x'], 'p': out['p'], 'w_in': out['w_in'], 'w_sb_out': out['w_sb_out'], 'w_ca_out': out['w_ca_out'], 'w_mix_out': out['w_mix_out'], 'rel_bias': out['rel_bias'], 'g_mix': out['g_mix'], 'g_ffn': out['g_ffn'], 'g_ple': out['g_ple'], 'g_final': out['g_final'], 'w_ffn_in': out['w_ffn_in'], 'w_ffn_out': out['w_ffn_out'], 'w_ple_in': out['w_ple_in'], 'w_ple_gate': out['w_ple_gate'], 'loss_target': out['loss_target'], 'm_w_in': out['m_w_in'], 'm_w_sb_out': out['m_w_sb_out'], 'm_w_ca_out': out['m_w_ca_out'], 'm_w_mix_out': out['m_w_mix_out'], 'm_rel_bias': out['m_rel_bias'], 'm_g_mix': out['m_g_mix'], 'm_g_ffn': out['m_g_ffn'], 'm_g_ple': out['m_g_ple'], 'm_g_final': out['m_g_final'], 'm_w_ffn_in': out['m_w_ffn_in'], 'm_w_ffn_out': out['m_w_ffn_out'], 'm_w_ple_in': out['m_w_ple_in'], 'm_w_ple_gate': out['m_w_ple_gate'], 'v_w_in': out['v_w_in'], 'v_w_sb_out': out['v_w_sb_out'], 'v_w_ca_out': out['v_w_ca_out'], 'v_w_mix_out': out['v_w_mix_out'], 'v_rel_bias': out['v_rel_bias'], 'v_g_mix': out['v_g_mix'], 'v_g_ffn': out['v_g_ffn'], 'v_g_ple': out['v_g_ple'], 'v_g_final': out['v_g_final'], 'v_w_ffn_in': out['v_w_ffn_in'], 'v_w_ffn_out': out['v_w_ffn_out'], 'v_w_ple_in': out['v_w_ple_in'], 'v_w_ple_gate': out['v_w_ple_gate']}


def _loss(weights, diff, rest, loss_target):
    with _jax.named_scope("forward"):
        args = {**rest, TWIN_DIFF_INPUT: diff, **{k: w.astype(_WEIGHT_DTYPES[k]) for k, w in weights.items()}}
        y = _forward(args)
    with _jax.named_scope("loss_head"):
        err = _jnp.square(y.astype(_jnp.float32) - loss_target)
        return 0.5 * _jnp.sum(_jnp.mean(err, axis=-1)) if err.ndim else 0.5 * err


def _adamw(w, g, m, v):
    m = ADAM_B1 * m + (1.0 - ADAM_B1) * g
    v = ADAM_B2 * v + (1.0 - ADAM_B2) * _jnp.square(g)
    m_hat = m / (1.0 - ADAM_B1 ** ADAM_STEP)
    v_hat = v / (1.0 - ADAM_B2 ** ADAM_STEP)
    delta = -ADAM_LR * (m_hat / (_jnp.sqrt(v_hat) + ADAM_EPS) + ADAM_WD * w)
    return delta, m, v


def reference(x, p, w_in, w_sb_out, w_ca_out, w_mix_out, rel_bias, g_mix, g_ffn, g_ple, g_final, w_ffn_in, w_ffn_out, w_ple_in, w_ple_gate, loss_target, m_w_in, m_w_sb_out, m_w_ca_out, m_w_mix_out, m_rel_bias, m_g_mix, m_g_ffn, m_g_ple, m_g_final, m_w_ffn_in, m_w_ffn_out, m_w_ple_in, m_w_ple_gate, v_w_in, v_w_sb_out, v_w_ca_out, v_w_mix_out, v_rel_bias, v_g_mix, v_g_ffn, v_g_ple, v_g_final, v_w_ffn_in, v_w_ffn_out, v_w_ple_in, v_w_ple_gate):
    given = dict(x=x, p=p, w_in=w_in, w_sb_out=w_sb_out, w_ca_out=w_ca_out, w_mix_out=w_mix_out, rel_bias=rel_bias, g_mix=g_mix, g_ffn=g_ffn, g_ple=g_ple, g_final=g_final, w_ffn_in=w_ffn_in, w_ffn_out=w_ffn_out, w_ple_in=w_ple_in, w_ple_gate=w_ple_gate, loss_target=loss_target, m_w_in=m_w_in, m_w_sb_out=m_w_sb_out, m_w_ca_out=m_w_ca_out, m_w_mix_out=m_w_mix_out, m_rel_bias=m_rel_bias, m_g_mix=m_g_mix, m_g_ffn=m_g_ffn, m_g_ple=m_g_ple, m_g_final=m_g_final, m_w_ffn_in=m_w_ffn_in, m_w_ffn_out=m_w_ffn_out, m_w_ple_in=m_w_ple_in, m_w_ple_gate=m_w_ple_gate, v_w_in=v_w_in, v_w_sb_out=v_w_sb_out, v_w_ca_out=v_w_ca_out, v_w_mix_out=v_w_mix_out, v_rel_bias=v_rel_bias, v_g_mix=v_g_mix, v_g_ffn=v_g_ffn, v_g_ple=v_g_ple, v_g_final=v_g_final, v_w_ffn_in=v_w_ffn_in, v_w_ffn_out=v_w_ffn_out, v_w_ple_in=v_w_ple_in, v_w_ple_gate=v_w_ple_gate)
    weights = {n: given[n] for n in TWIN_WEIGHTS}
    shared = {n: given[n] for n in SHARED_INPUTS}
    per_example = {n: given[n] for n in ['x', 'p']}
    grad_fn = _jax.value_and_grad(_loss, argnums=(0, 1))

    def one_microbatch(ex, loss_target):
        ex = dict(ex)
        diff = ex.pop(TWIN_DIFF_INPUT)
        return grad_fn(weights, diff, {**shared, **ex}, loss_target)

    if N_MICROBATCH == 1:
        loss, (grad_w, grad_x) = one_microbatch(per_example, given["loss_target"])
    else:
        def body(carry, xs):
            loss_sum, grad_sum = carry
            l_k, (gw_k, gx_k) = one_microbatch(xs[0], xs[1])
            with _jax.named_scope("update"):
                return (loss_sum + l_k, _jax.tree.map(_jnp.add, grad_sum, gw_k)), gx_k

        init = (_jnp.zeros((), _jnp.float32), _jax.tree.map(_jnp.zeros_like, weights))
        (loss, grad_w), grad_x = _jax.lax.scan(body, init, (per_example, given["loss_target"]))
    with _jax.named_scope("update"):
        delta_w, new_m, new_v = {}, {}, {}
        for n in TWIN_WEIGHTS:
            delta_w[n], new_m[n], new_v[n] = _adamw(weights[n], grad_w[n], given["m_" + n], given["v_" + n])
    return (loss, grad_x, *[grad_w[n] for n in TWIN_WEIGHTS], *[delta_w[n] for n in TWIN_WEIGHTS],
            *[new_m[n] for n in TWIN_WEIGHTS], *[new_v[n] for n in TWIN_WEIGHTS])
```

```python
import functools

import jax
import jax.numpy as jnp
from jax import lax
from jax.experimental import pallas as pl
from jax.experimental.pallas import tpu as pltpu

F32, BF16 = jnp.float32, jnp.bfloat16

N_DEV = 8
HEAD_DIM = 128
CHUNK = 64
LEFT_CHUNKS = 8
REL_CLIP = 128
N_REL = REL_CLIP + CHUNK
PAIR = 2 * CHUNK
PBAND = (LEFT_CHUNKS + 2) * CHUNK
PAD = LEFT_CHUNKS * CHUNK
SB_BLOCK = 256
ROWS = 256
EPS = 1e-6
NEG = -1e30
SCALE = HEAD_DIM ** -0.5
VMEM_LIMIT_BYTES = 56 * 1024 * 1024

ADAM_LR, ADAM_B1, ADAM_B2, ADAM_EPS, ADAM_WD, ADAM_STEP = 0.001, 0.9, 0.999, 1e-08, 0.01, 10

NN = (((1,), (0,)), ((), ()))
NT = (((1,), (1,)), ((), ()))
TN = (((0,), (0,)), ((), ()))
MESH = pl.DeviceIdType.MESH


def _params(*sem):
    return pltpu.CompilerParams(dimension_semantics=sem or None, vmem_limit_bytes=VMEM_LIMIT_BYTES)


def _dot(a, b, dims=NN):
    return lax.dot_general(a, b, dims, preferred_element_type=F32)


def _mm(a, b, *, mode, tm, tn, tk, out_dtype, name, b_blocked=False, out_block=None, res=None):
    if mode == "nn":
        M, K = a.shape
        a_spec = pl.BlockSpec((tm, tk), lambda i, j, k: (i, k))
        if b_blocked:
            G, _, nb = b.shape
            N, per = G * nb, nb // tn
            assert nb % tn == 0
            b_spec = pl.BlockSpec((None, tk, tn), lambda i, j, k: (j // per, k, j % per))
        else:
            N = b.shape[1]
            b_spec = pl.BlockSpec((tk, tn), lambda i, j, k: (k, j))
        dims = NN
    elif mode == "nt":
        M, K = a.shape
        a_spec = pl.BlockSpec((tm, tk), lambda i, j, k: (i, k))
        if b_blocked:
            G, N, nb = b.shape
            per = nb // tk
            assert K == G * nb and nb % tk == 0
            b_spec = pl.BlockSpec((None, tn, tk), lambda i, j, k: (k // per, j, k % per))
        else:
            N = b.shape[0]
            b_spec = pl.BlockSpec((tn, tk), lambda i, j, k: (j, k))
        dims = NT
    else:
        K, M = a.shape
        N = b.shape[1]
        a_spec = pl.BlockSpec((tk, tm), lambda i, j, k: (k, i))
        b_spec = pl.BlockSpec((tk, tn), lambda i, j, k: (k, j))
        dims = TN
    assert M % tm == 0 and N % tn == 0 and K % tk == 0, (name, M, N, K, tm, tn, tk)
    nk = K // tk
    if out_block is None:
        out_shape = jax.ShapeDtypeStruct((M, N), out_dtype)
        o_spec = pl.BlockSpec((tm, tn), lambda i, j, k: (i, j))
    else:
        per_o = out_block // tn
        assert out_block % tn == 0
        out_shape = jax.ShapeDtypeStruct((N // out_block, M, out_block), out_dtype)
        o_spec = pl.BlockSpec((None, tm, tn), lambda i, j, k: (j // per_o, i, j % per_o))
    in_specs = [a_spec, b_spec]
    args = [a, b]
    if res is not None:
        in_specs.append(pl.BlockSpec((tm, tn), lambda i, j, k: (i, j)))
        args.append(res)

    def body(*refs):
        a_ref, b_ref = refs[0], refs[1]
        r_ref = refs[2] if res is not None else None
        o_ref = refs[3] if res is not None else refs[2]

        def finish(acc):
            if r_ref is not None:
                acc = acc + r_ref[...]
            o_ref[...] = acc.astype(o_ref.dtype)

        if nk == 1:
            finish(_dot(a_ref[...], b_ref[...], dims))
        else:
            acc_ref = refs[-1]
            k = pl.program_id(2)

            @pl.when(k == 0)
            def _():
                acc_ref[...] = jnp.zeros_like(acc_ref)

            acc_ref[...] += _dot(a_ref[...], b_ref[...], dims)

            @pl.when(k == nk - 1)
            def _():
                finish(acc_ref[...])

    return pl.pallas_call(
        body, grid=(M // tm, N // tn, nk), in_specs=in_specs, out_specs=o_spec, out_shape=out_shape,
        scratch_shapes=[] if nk == 1 else [pltpu.VMEM((tm, tn), F32)],
        compiler_params=_params("parallel", "parallel", "arbitrary"), name=name)(*args)


def _row_spec(d, col=0):
    return pl.BlockSpec((ROWS, d), lambda i: (i, col))


def _vec_spec(d):
    return pl.BlockSpec((1, d), lambda i: (0, 0))


def _rms(x):
    return lax.rsqrt(jnp.mean(x * x, axis=-1, keepdims=True) + EPS)


def _norm_fwd(x, g, name):
    T, D = x.shape

    def body(x_ref, g_ref, h_ref):
        xv = x_ref[...]
        h_ref[...] = (xv * _rms(xv) * g_ref[...]).astype(BF16)

    return pl.pallas_call(body, grid=(T // ROWS,), in_specs=[_row_spec(D), _vec_spec(D)], out_specs=_row_spec(D),
                          out_shape=jax.ShapeDtypeStruct((T, D), BF16), compiler_params=_params("parallel"), name=name)(x, g)


def _norm_bwd_math(dh, xv, gv):
    r = _rms(xv)
    xhat = xv * r
    dxhat = dh * gv
    dx = r * (dxhat - xhat * jnp.mean(dxhat * xhat, axis=-1, keepdims=True))
    dg = jnp.sum(dh * xhat, axis=0, keepdims=True)
    return dx, dg


def _norm_bwd(dh, x, g, dres, name):
    T, D = x.shape

    def body(dh_ref, x_ref, g_ref, dres_ref, dx_ref, dxb_ref, dg_ref):
        dx, dg = _norm_bwd_math(dh_ref[...], x_ref[...], g_ref[...])
        dx = dx + dres_ref[...]
        dx_ref[...] = dx
        dxb_ref[...] = dx.astype(BF16)

        @pl.when(pl.program_id(0) == 0)
        def _():
            dg_ref[...] = jnp.zeros_like(dg_ref)

        dg_ref[...] += dg

    return pl.pallas_call(
        body, grid=(T // ROWS,), in_specs=[_row_spec(D), _row_spec(D), _vec_spec(D), _row_spec(D)],
        out_specs=[_row_spec(D), _row_spec(D), _vec_spec(D)],
        out_shape=[jax.ShapeDtypeStruct((T, D), F32), jax.ShapeDtypeStruct((T, D), BF16), jax.ShapeDtypeStruct((1, D), F32)],
        compiler_params=_params("arbitrary"), name=name)(dh, x, g, dres)


def _merge_fwd(proj, a_sb, a_ca, D, gate_col):
    T = proj.shape[0]

    def body(gs_ref, gc_ref, a_ref, b_ref, o_ref):
        o_ref[...] = (jax.nn.sigmoid(gs_ref[...]) * a_ref[...] + jax.nn.sigmoid(gc_ref[...]) * b_ref[...]).astype(BF16)

    return pl.pallas_call(
        body, grid=(T // ROWS,), in_specs=[_row_spec(D, gate_col), _row_spec(D, gate_col + 1), _row_spec(D), _row_spec(D)],
        out_specs=_row_spec(D), out_shape=jax.ShapeDtypeStruct((T, D), BF16),
        compiler_params=_params("parallel"), name="merge_fwd")(proj, proj, a_sb, a_ca)


def _merge_bwd(dm, proj, a_sb, a_ca, D, gate_col):
    T = proj.shape[0]

    def body(dm_ref, gs_ref, gc_ref, a_ref, b_ref, da_ref, db_ref, dgs_ref, dgc_ref):
        dmv = dm_ref[...]
        ss, sc = jax.nn.sigmoid(gs_ref[...]), jax.nn.sigmoid(gc_ref[...])
        da_ref[...] = (dmv * ss).astype(BF16)
        db_ref[...] = (dmv * sc).astype(BF16)
        dgs_ref[...] = (dmv * a_ref[...] * ss * (1.0 - ss)).astype(BF16)
        dgc_ref[...] = (dmv * b_ref[...] * sc * (1.0 - sc)).astype(BF16)

    return pl.pallas_call(
        body, grid=(T // ROWS,),
        in_specs=[_row_spec(D), _row_spec(D, gate_col), _row_spec(D, gate_col + 1), _row_spec(D), _row_spec(D)],
        out_specs=[_row_spec(D)] * 4, out_shape=[jax.ShapeDtypeStruct((T, D), BF16)] * 4,
        compiler_params=_params("parallel"), name="merge_bwd")(dm, proj, proj, a_sb, a_ca)


def _swiglu_fwd(gu):
    T, F2 = gu.shape
    F = F2 // 2

    def body(g_ref, u_ref, o_ref):
        gv = g_ref[...]
        o_ref[...] = (gv * jax.nn.sigmoid(gv) * u_ref[...]).astype(BF16)

    return pl.pallas_call(body, grid=(T // ROWS,), in_specs=[_row_spec(F, 0), _row_spec(F, 1)], out_specs=_row_spec(F),
                          out_shape=jax.ShapeDtypeStruct((T, F), BF16), compiler_params=_params("parallel"), name="swiglu_fwd")(gu, gu)


def _swiglu_bwd(dact, gu):
    T, F2 = gu.shape
    F = F2 // 2

    def body(d_ref, g_ref, u_ref, o_ref):
        dv, gv, uv = d_ref[...], g_ref[...], u_ref[...]
        s = jax.nn.sigmoid(gv)
        o_ref[:, 0:F] = (dv * uv * s * (1.0 + gv * (1.0 - s))).astype(BF16)
        o_ref[:, F:F2] = (dv * gv * s).astype(BF16)

    return pl.pallas_call(body, grid=(T // ROWS,), in_specs=[_row_spec(F), _row_spec(F, 0), _row_spec(F, 1)], out_specs=_row_spec(F2),
                          out_shape=jax.ShapeDtypeStruct((T, F2), BF16), compiler_params=_params("parallel"), name="swiglu_bwd")(dact, gu, gu)


def _tail(x3, zg, pe, g_final, target):
    T, D = x3.shape

    def body(x3_ref, zg_ref, pe_ref, g_ref, t_ref, loss_ref, dx_ref, dpe_ref, dzg_ref, dg_ref):
        gate = jax.nn.sigmoid(zg_ref[...])
        pev = pe_ref[...]
        x4 = x3_ref[...] + gate * pev
        gv = g_ref[...]
        xhat = x4 * _rms(x4)
        err = xhat * gv - t_ref[...]
        part = 0.5 * jnp.sum(jnp.mean(err * err, axis=-1, keepdims=True), axis=0, keepdims=True)
        dx, dg = _norm_bwd_math(err * (1.0 / D), x4, gv)
        dx_ref[...] = dx
        dpe_ref[...] = (dx * gate).astype(BF16)
        dzg_ref[...] = (dx * pev * gate * (1.0 - gate)).astype(BF16)

        @pl.when(pl.program_id(0) == 0)
        def _():
            dg_ref[...] = jnp.zeros_like(dg_ref)
            loss_ref[...] = jnp.zeros_like(loss_ref)

        dg_ref[...] += dg
        loss_ref[...] += jnp.broadcast_to(part, loss_ref.shape)

    return pl.pallas_call(
        body, grid=(T // ROWS,), in_specs=[_row_spec(D), _row_spec(D), _row_spec(D), _vec_spec(D), _row_spec(D)],
        out_specs=[_vec_spec(128), _row_spec(D), _row_spec(D), _row_spec(D), _vec_spec(D)],
        out_shape=[jax.ShapeDtypeStruct((1, 128), F32), jax.ShapeDtypeStruct((T, D), F32), jax.ShapeDtypeStruct((T, D), BF16),
                   jax.ShapeDtypeStruct((T, D), BF16), jax.ShapeDtypeStruct((1, D), F32)],
        compiler_params=_params("arbitrary"), name="tail")(x3, zg, pe, g_final, target)


def _cast_bf16(x, name):
    R, C = x.shape
    rows = next(r for r in (ROWS, 128, 64, 32, 16) if R % r == 0)

    def body(x_ref, o_ref):
        o_ref[...] = x_ref[...].astype(BF16)

    spec = pl.BlockSpec((rows, C), lambda i: (i, 0))
    return pl.pallas_call(body, grid=(R // rows,), in_specs=[spec], out_specs=spec, out_shape=jax.ShapeDtypeStruct((R, C), BF16),
                          compiler_params=_params("parallel"), name=name)(x)


def _head_spec(T, col0):
    return pl.BlockSpec((T, HEAD_DIM), lambda h, *_: (0, col0 + h))


def _sb_tile(qv, k_ref, v_ref, qb, kb, c_lk):
    B = SB_BLOCK
    ks = pl.multiple_of(kb * B, B)
    kk = k_ref[pl.ds(ks, B), :].astype(BF16)
    vv = v_ref[pl.ds(ks, B), :].astype(BF16)
    z = _dot(qv, kk, NT) * SCALE
    t_idx = qb * B + lax.broadcasted_iota(jnp.int32, (B, B), 0)
    s_idx = kb * B + lax.broadcasted_iota(jnp.int32, (B, B), 1)
    past = s_idx < t_idx
    sp = jnp.log(1.0 + jnp.exp(-jnp.abs(z)))
    ls_pos = jnp.minimum(z, 0.0) - sp
    lk = jnp.where(past, jnp.minimum(-z, 0.0) - sp, 0.0)
    between = c_lk + _lane_scan(lk, right=True)
    a = jnp.where(past, jnp.exp(ls_pos + between), 0.0)
    return kk, vv, past, ls_pos, lk, a


def _lane_scan(x, right):
    n = x.shape[1]
    j = lax.broadcasted_iota(jnp.int32, (n, n), 0)
    s = lax.broadcasted_iota(jnp.int32, (n, n), 1)
    tri = jnp.where((j > s) if right else (j < s), 1.0, 0.0).astype(BF16)
    hi = x.astype(BF16)
    lo = (x - hi.astype(F32)).astype(BF16)
    return _dot(hi, tri) + _dot(lo, tri)


def _sb_fwd(proj, n_heads):
    T = proj.shape[0]
    B = SB_BLOCK

    def body(q_ref, k_ref, v_ref, y_ref):
        qb = pl.program_id(1)
        qv = q_ref[...].astype(BF16)

        def tile(i, carry):
            acc, c_lk = carry
            kb = qb - i
            _, vv, _, _, lk, a = _sb_tile(qv, k_ref, v_ref, qb, kb, c_lk)
            return acc + _dot(a.astype(BF16), vv), c_lk + jnp.sum(lk, axis=1, keepdims=True)

        acc, _ = lax.fori_loop(0, qb + 1, tile, (jnp.zeros((B, HEAD_DIM), F32), jnp.zeros((B, 1), F32)))
        y_ref[...] = acc.astype(BF16)

    blk = pl.BlockSpec((B, HEAD_DIM), lambda h, i: (i, h))
    return pl.pallas_call(
        body, grid=(n_heads, T // B),
        in_specs=[blk, _head_spec(T, n_heads), _head_spec(T, 2 * n_heads)], out_specs=blk,
        out_shape=jax.ShapeDtypeStruct((T, n_heads * HEAD_DIM), BF16),
        compiler_params=_params("parallel", "arbitrary"), name="sb_fwd")(proj, proj, proj)


def _sb_bwd(proj, dy, n_heads):
    T = proj.shape[0]
    B = SB_BLOCK

    def body(q_ref, k_ref, v_ref, dy_ref, dq_ref, dk_ref, dv_ref, g_s, sig_s):
        qb = pl.program_id(1)

        @pl.when(qb == 0)
        def _():
            dk_ref[...] = jnp.zeros_like(dk_ref)
            dv_ref[...] = jnp.zeros_like(dv_ref)

        qv = q_ref[...].astype(BF16)
        dyb = dy_ref[...].astype(BF16)

        def sweep(i, c_lk):
            kb = qb - i
            ks = pl.multiple_of(kb * B, B)
            _, vv, _, ls_pos, lk, a = _sb_tile(qv, k_ref, v_ref, qb, kb, c_lk)
            g_s[kb] = _dot(dyb, vv, NT) * a
            sig_s[kb] = jnp.exp(ls_pos)
            dv_ref[pl.ds(ks, B), :] += _dot(a.astype(BF16), dyb, TN)
            return c_lk + jnp.sum(lk, axis=1, keepdims=True)

        lax.fori_loop(0, qb + 1, sweep, jnp.zeros((B, 1), F32))

        def back(kb, carry):
            dq, c_g = carry
            ks = pl.multiple_of(kb * B, B)
            kk = k_ref[pl.ds(ks, B), :].astype(BF16)
            g, sig = g_s[kb], sig_s[kb]
            t_idx = qb * B + lax.broadcasted_iota(jnp.int32, (B, B), 0)
            s_idx = kb * B + lax.broadcasted_iota(jnp.int32, (B, B), 1)
            before = c_g + _lane_scan(g, right=False)
            dz = (jnp.where(s_idx < t_idx, g * (1.0 - sig) - before * sig, 0.0) * SCALE).astype(BF16)
            dk_ref[pl.ds(ks, B), :] += _dot(dz, qv, TN)
            return dq + _dot(dz, kk), c_g + jnp.sum(g, axis=1, keepdims=True)

        dq, _ = lax.fori_loop(0, qb + 1, back, (jnp.zeros((B, HEAD_DIM), F32), jnp.zeros((B, 1), F32)))
        dq_ref[...] = dq

    blk = pl.BlockSpec((B, HEAD_DIM), lambda h, i: (i, h))
    full = _head_spec(T, 0)
    shp = jax.ShapeDtypeStruct((T, n_heads * HEAD_DIM), F32)
    return pl.pallas_call(
        body, grid=(n_heads, T // B),
        in_specs=[blk, _head_spec(T, n_heads), _head_spec(T, 2 * n_heads), blk], out_specs=[blk, full, full],
        out_shape=[shp, shp, shp], scratch_shapes=[pltpu.VMEM((T // B, B, B), F32)] * 2,
        compiler_params=_params("parallel", "arbitrary"), name="sb_bwd")(proj, proj, proj, dy)


def _rel_index(cols, col0):
    i = lax.broadcasted_iota(jnp.int32, (PAIR, cols), 0)
    j = col0 + lax.broadcasted_iota(jnp.int32, (PAIR, cols), 1)
    return jnp.clip(j - i - PAD, -REL_CLIP, CHUNK - 1) + REL_CLIP


NEAR0 = PBAND - 2 * PAIR


def _bias_expand(rel_bias):
    H = rel_bias.shape[0]

    def body(rb_ref, o_ref):
        h = pl.program_id(0)
        o_ref[:, 0:NEAR0] = jnp.full((PAIR, NEAR0), rb_ref[h, 0], F32)
        idx = _rel_index(2 * PAIR, NEAR0)

        def step(r, acc):
            return jnp.where(idx == r, rb_ref[h, r], acc)

        o_ref[:, NEAR0:PBAND] = lax.fori_loop(0, N_REL, step, jnp.zeros((PAIR, 2 * PAIR), F32))

    return pl.pallas_call(
        body, grid=(H,), in_specs=[pl.BlockSpec(memory_space=pltpu.SMEM)],
        out_specs=pl.BlockSpec((None, PAIR, PBAND), lambda h: (h, 0, 0)),
        out_shape=jax.ShapeDtypeStruct((H, PAIR, PBAND), F32), compiler_params=_params("parallel"), name="bias_expand")(rel_bias)


def _bias_reduce(dbias):
    H = dbias.shape[0]

    def body(d_ref, o_ref):
        far = jnp.sum(jnp.sum(d_ref[:, 0:NEAR0], axis=1, keepdims=True), axis=0, keepdims=True)
        near = d_ref[:, NEAR0:PBAND]
        idx = _rel_index(2 * PAIR, NEAR0)
        lane = lax.broadcasted_iota(jnp.int32, (1, 2 * PAIR), 1)

        def step(r, acc):
            s = jnp.sum(jnp.sum(jnp.where(idx == r, near, 0.0), axis=1, keepdims=True), axis=0, keepdims=True)
            return acc + jnp.where(lane == r, s, 0.0)

        out = lax.fori_loop(0, N_REL, step, jnp.zeros((1, 2 * PAIR), F32))
        o_ref[...] = out + jnp.where(lane == 0, far, 0.0)

    return pl.pallas_call(
        body, grid=(H,), in_specs=[pl.BlockSpec((None, PAIR, PBAND), lambda h: (h, 0, 0))],
        out_specs=pl.BlockSpec((None, 1, 2 * PAIR), lambda h: (h, 0, 0)),
        out_shape=jax.ShapeDtypeStruct((H, 1, 2 * PAIR), F32), compiler_params=_params("parallel"), name="bias_reduce")(dbias)


def _ca_pair(pr, q_ref, kpad, vpad, bias):
    r0 = pl.multiple_of(pr * PAIR, PAIR)
    qp = q_ref[pl.ds(r0, PAIR), :].astype(BF16)
    kb = kpad[pl.ds(r0, PBAND), :]
    vb = vpad[pl.ds(r0, PBAND), :]
    i = lax.broadcasted_iota(jnp.int32, (PAIR, PBAND), 0)
    j = lax.broadcasted_iota(jnp.int32, (PAIR, PBAND), 1)
    qc, kc = i // CHUNK, j // CHUNK
    valid = (kc >= qc) & (kc <= qc + LEFT_CHUNKS) & (pr * PAIR + j >= PAD)
    z = jnp.where(valid, _dot(qp, kb, NT) * SCALE + bias, NEG)
    e = jnp.exp(z - jnp.max(z, axis=1, keepdims=True))
    return qp, kb, vb, e / jnp.sum(e, axis=1, keepdims=True)


def _ca_fill(k_ref, v_ref, kpad, vpad):
    T = k_ref.shape[0]
    kpad[0:PAD, :] = jnp.zeros((PAD, HEAD_DIM), BF16)
    vpad[0:PAD, :] = jnp.zeros((PAD, HEAD_DIM), BF16)
    kpad[PAD:PAD + T, :] = k_ref[...].astype(BF16)
    vpad[PAD:PAD + T, :] = v_ref[...].astype(BF16)


def _ca_fwd(proj, bias, n_heads, col0):
    T = proj.shape[0]

    def body(q_ref, k_ref, v_ref, b_ref, y_ref, kpad, vpad):
        _ca_fill(k_ref, v_ref, kpad, vpad)
        bias_v = b_ref[...]

        def pair(pr, _):
            _, _, vb, w = _ca_pair(pr, q_ref, kpad, vpad, bias_v)
            y_ref[pl.ds(pl.multiple_of(pr * PAIR, PAIR), PAIR), :] = _dot(w.astype(BF16), vb).astype(BF16)
            return 0

        lax.fori_loop(0, T // PAIR, pair, 0)

    return pl.pallas_call(
        body, grid=(n_heads,),
        in_specs=[_head_spec(T, col0), _head_spec(T, col0 + n_heads), _head_spec(T, col0 + 2 * n_heads),
                  pl.BlockSpec((None, PAIR, PBAND), lambda h: (h, 0, 0))],
        out_specs=_head_spec(T, 0), out_shape=jax.ShapeDtypeStruct((T, n_heads * HEAD_DIM), BF16),
        scratch_shapes=[pltpu.VMEM((PAD + T, HEAD_DIM), BF16)] * 2,
        compiler_params=_params("parallel"), name="ca_fwd")(proj, proj, proj, bias)


def _ca_bwd(proj, bias, dy, n_heads, col0):
    T = proj.shape[0]

    def body(q_ref, k_ref, v_ref, b_ref, dy_ref, dq_ref, dk_ref, dv_ref, db_ref, kpad, vpad, dkpad, dvpad):
        _ca_fill(k_ref, v_ref, kpad, vpad)
        dkpad[...] = jnp.zeros_like(dkpad)
        dvpad[...] = jnp.zeros_like(dvpad)
        db_ref[...] = jnp.zeros_like(db_ref)
        bias_v = b_ref[...]

        def pair(pr, _):
            r0 = pl.multiple_of(pr * PAIR, PAIR)
            qp, kb, vb, w = _ca_pair(pr, q_ref, kpad, vpad, bias_v)
            dyp = dy_ref[pl.ds(r0, PAIR), :].astype(BF16)
            dw = _dot(dyp, vb, NT)
            dz = w * (dw - jnp.sum(dw * w, axis=1, keepdims=True))
            db_ref[...] += dz
            dzs = (dz * SCALE).astype(BF16)
            dq_ref[pl.ds(r0, PAIR), :] = _dot(dzs, kb)
            dkpad[pl.ds(r0, PBAND), :] += _dot(dzs, qp, TN)
            dvpad[pl.ds(r0, PBAND), :] += _dot(w.astype(BF16), dyp, TN)
            return 0

        lax.fori_loop(0, T // PAIR, pair, 0)
        dk_ref[...] = dkpad[PAD:PAD + T, :]
        dv_ref[...] = dvpad[PAD:PAD + T, :]

    full = _head_spec(T, 0)
    bspec = pl.BlockSpec((None, PAIR, PBAND), lambda h: (h, 0, 0))
    shp = jax.ShapeDtypeStruct((T, n_heads * HEAD_DIM), F32)
    return pl.pallas_call(
        body, grid=(n_heads,),
        in_specs=[_head_spec(T, col0), _head_spec(T, col0 + n_heads), _head_spec(T, col0 + 2 * n_heads), bspec, full],
        out_specs=[full, full, full, bspec],
        out_shape=[shp, shp, shp, jax.ShapeDtypeStruct((n_heads, PAIR, PBAND), F32)],
        scratch_shapes=[pltpu.VMEM((PAD + T, HEAD_DIM), BF16)] * 2 + [pltpu.VMEM((PAD + T, HEAD_DIM), F32)] * 2,
        compiler_params=_params("parallel"), name="ca_bwd")(proj, proj, proj, bias, dy)


def _local_step(x, p, target, w, g):
    T, D = x.shape
    H = g["rel_bias"].shape[0]
    W = H * HEAD_DIM
    nb_in = w["w_in"].shape[2]
    nb_ff = w["w_ffn_in"].shape[2]
    nb_o = w["w_sb_out"].shape[2]
    tm = min(T, 1024)
    gate_col = 6 * W // D

    h1 = _norm_fwd(x, g["g_mix"], "norm1")
    proj = _mm(h1, w["w_in"], mode="nn", tm=tm, tn=nb_in, tk=D, out_dtype=F32, b_blocked=True, name="mm_in")
    y_sb = _sb_fwd(proj, H)
    bias = _bias_expand(g["rel_bias"])
    y_ca = _ca_fwd(proj, bias, H, 3 * H)
    a_sb = _mm(y_sb, w["w_sb_out"], mode="nn", tm=tm, tn=nb_o, tk=W, out_dtype=F32, b_blocked=True, name="mm_sb_out")
    a_ca = _mm(y_ca, w["w_ca_out"], mode="nn", tm=tm, tn=nb_o, tk=W, out_dtype=F32, b_blocked=True, name="mm_ca_out")
    merged = _merge_fwd(proj, a_sb, a_ca, D, gate_col)
    x2 = _mm(merged, w["w_mix_out"], mode="nn", tm=tm, tn=min(D, 1024), tk=D, out_dtype=F32, res=x, name="mm_mix")
    h2 = _norm_fwd(x2, g["g_ffn"], "norm2")
    gu = _mm(h2, w["w_ffn_in"], mode="nn", tm=tm, tn=nb_ff, tk=D, out_dtype=F32, b_blocked=True, name="mm_ffn_in")
    act = _swiglu_fwd(gu)
    F = act.shape[1]
    tkf = F // 2 if F % 256 == 0 else F
    x3 = _mm(act, w["w_ffn_out"], mode="nn", tm=tm, tn=min(D, 1024), tk=tkf, out_dtype=F32, res=x2, name="mm_ffn_out")
    h3 = _norm_fwd(x3, g["g_ple"], "norm3")
    zg = _mm(h3, w["w_ple_gate"], mode="nn", tm=tm, tn=min(D, 1024), tk=D, out_dtype=F32, name="mm_ple_gate")
    pb = _cast_bf16(p, "cast_p")
    P = p.shape[1]
    nb_p = w["w_ple_in"].shape[2]
    pe = _mm(pb, w["w_ple_in"], mode="nn", tm=tm, tn=nb_p, tk=P, out_dtype=F32, b_blocked=True, name="mm_ple_in")
    loss, dx4, dpe, dzg, dg_final = _tail(x3, zg, pe, g["g_final"], target)

    tw = min(D, 512)
    DW = BF16
    dw = {}
    dw["w_ple_in"] = _mm(pb, dpe, mode="tn", tm=P, tn=nb_p, tk=T, out_dtype=DW, out_block=nb_p, name="mm_d_ple_in")
    dw["w_ple_gate"] = _mm(h3, dzg, mode="tn", tm=tw, tn=min(D, 1024), tk=T, out_dtype=DW, name="mm_d_ple_gate")
    dh3 = _mm(dzg, w["w_ple_gate"], mode="nt", tm=tm, tn=min(D, 1024), tk=D, out_dtype=F32, name="mm_dh3")
    dx3, dx3b, dg_ple = _norm_bwd(dh3, x3, g["g_ple"], dx4, "norm3_bwd")
    dact = _mm(dx3b, w["w_ffn_out"], mode="nt", tm=tm, tn=tkf, tk=D, out_dtype=F32, name="mm_dact")
    dw["w_ffn_out"] = _mm(act, dx3b, mode="tn", tm=F // 4, tn=min(D, 1024), tk=T, out_dtype=DW, name="mm_d_ffn_out")
    dgu = _swiglu_bwd(dact, gu)
    dw["w_ffn_in"] = _mm(h2, dgu, mode="tn", tm=tw, tn=nb_ff, tk=T, out_dtype=DW, out_block=nb_ff, name="mm_d_ffn_in")
    dh2 = _mm(dgu, w["w_ffn_in"], mode="nt", tm=tm, tn=min(D, 1024), tk=nb_ff, out_dtype=F32, b_blocked=True, name="mm_dh2")
    dx2, dx2b, dg_ffn = _norm_bwd(dh2, x2, g["g_ffn"], dx3, "norm2_bwd")
    dmerged = _mm(dx2b, w["w_mix_out"], mode="nt", tm=tm, tn=min(D, 1024), tk=D, out_dtype=F32, name="mm_dmerged")
    dw["w_mix_out"] = _mm(merged, dx2b, mode="tn", tm=tw, tn=min(D, 1024), tk=T, out_dtype=DW, name="mm_d_mix")
    da_sb, da_ca, dgate_sb, dgate_ca = _merge_bwd(dmerged, proj, a_sb, a_ca, D, gate_col)
    dw["w_sb_out"] = _mm(y_sb, da_sb, mode="tn", tm=min(W, 512), tn=nb_o, tk=T, out_dtype=DW, out_block=nb_o, name="mm_d_sb_out")
    dw["w_ca_out"] = _mm(y_ca, da_ca, mode="tn", tm=min(W, 512), tn=nb_o, tk=T, out_dtype=DW, out_block=nb_o, name="mm_d_ca_out")
    dy_sb = _mm(da_sb, w["w_sb_out"], mode="nt", tm=tm, tn=W, tk=nb_o, out_dtype=F32, b_blocked=True, name="mm_dy_sb")
    dy_ca = _mm(da_ca, w["w_ca_out"], mode="nt", tm=tm, tn=W, tk=nb_o, out_dtype=F32, b_blocked=True, name="mm_dy_ca")
    dq_sb, dk_sb, dv_sb = _sb_bwd(proj, dy_sb, H)
    dq_ca, dk_ca, dv_ca, dbias = _ca_bwd(proj, bias, dy_ca, H, 3 * H)
    d_rel = _bias_reduce(dbias)[:, 0, :N_REL]
    dproj = jnp.concatenate([t.astype(BF16) for t in (dq_sb, dk_sb, dv_sb, dq_ca, dk_ca, dv_ca)] + [dgate_sb, dgate_ca], axis=1)
    dw["w_in"] = _mm(h1, dproj, mode="tn", tm=tw, tn=nb_in, tk=T, out_dtype=DW, out_block=nb_in, name="mm_d_in")
    dh1 = _mm(dproj, w["w_in"], mode="nt", tm=tm, tn=min(D, 1024), tk=nb_in, out_dtype=F32, b_blocked=True, name="mm_dh1")
    grad_x, _, dg_mix = _norm_bwd(dh1, x, g["g_mix"], dx2, "norm1_bwd")
    small = dict(g_mix=dg_mix, g_ffn=dg_ffn, g_ple=dg_ple, g_final=dg_final, rel_bias=d_rel)
    return loss, grad_x, dw, small


ANY = pl.BlockSpec(memory_space=pl.ANY)


def _position():
    x, y, c = lax.axis_index("x"), lax.axis_index("y"), lax.axis_index("c")
    return x, y, c


def _block_of(px, py, pc):
    return 4 * px + 2 * py + pc


def _flip(pos, k):
    x, y, c = pos
    return (1 - x if k & 4 else x, 1 - y if k & 2 else y, 1 - c if k & 1 else c)


def _all_gather(shards):
    n = len(shards)

    def body(*refs):
        ins, outs = refs[:n], refs[n:2 * n]
        send, recv, lsem = refs[2 * n:]
        x, y, c = _position()
        me, sibling = (x, y, c), (x, y, 1 - c)
        chips = [(1 - x, y), (x, 1 - y), (1 - x, 1 - y)]

        def copy(wi, k, block, to, src=None):
            rows = outs[wi].at[_block_of(*block)]
            return pltpu.make_async_remote_copy(
                src_ref=rows if src is None else src, dst_ref=rows, send_sem=send.at[wi * 7 + k], recv_sem=recv.at[wi * 7 + k],
                device_id=to, device_id_type=MESH)

        local = [pltpu.make_async_copy(ins[wi], outs[wi].at[_block_of(*me)], lsem.at[wi]) for wi in range(n)]
        for cp in local:
            cp.start()
        first = []
        for wi in range(n):
            first += [copy(wi, 1 + j, me, (*chip, c), src=ins[wi]) for j, chip in enumerate(chips)]
            first.append(copy(wi, 0, me, sibling, src=ins[wi]))
        for cp in first:
            cp.start()
        passed = []
        for wi in range(n):
            for j, chip in enumerate(chips):
                copy(wi, 1 + j, (*chip, c), me).wait_recv()
                fwd = copy(wi, 4 + j, (*chip, c), sibling)
                fwd.start()
                passed.append(fwd)
        for wi in range(n):
            copy(wi, 0, sibling, me).wait_recv()
            for j, chip in enumerate(chips):
                copy(wi, 4 + j, (*chip, 1 - c), me).wait_recv()
        for cp in first + passed:
            cp.wait_send()
        for cp in local:
            cp.wait()

    return pl.pallas_call(
        body, in_specs=[ANY] * n, out_specs=[ANY] * n,
        out_shape=[jax.ShapeDtypeStruct((N_DEV,) + s.shape, s.dtype) for s in shards],
        scratch_shapes=[pltpu.SemaphoreType.DMA((7 * n,)), pltpu.SemaphoreType.DMA((7 * n,)), pltpu.SemaphoreType.DMA((n,))],
        name="all_gather")(*shards)


def _grad_exchange(blocks):
    n = len(blocks)

    def body(*refs):
        ins, outs = refs[:n], refs[n:2 * n]
        send, recv, lsem = refs[2 * n:]
        me = _position()
        mine = _block_of(*me)
        local = [pltpu.make_async_copy(ins[wi].at[mine], outs[wi].at[mine], lsem.at[wi]) for wi in range(n)]
        for cp in local:
            cp.start()

        def copy(wi, k):
            peer = _flip(me, k)
            return pltpu.make_async_remote_copy(
                src_ref=ins[wi].at[_block_of(*peer)], dst_ref=outs[wi].at[mine],
                send_sem=send.at[wi * 7 + k - 1], recv_sem=recv.at[wi * 7 + k - 1], device_id=peer, device_id_type=MESH)

        def arrival(wi, k):
            peer = _flip(me, k)
            rows = outs[wi].at[_block_of(*peer)]
            return pltpu.make_async_remote_copy(
                src_ref=rows, dst_ref=rows, send_sem=send.at[wi * 7 + k - 1], recv_sem=recv.at[wi * 7 + k - 1],
                device_id=peer, device_id_type=MESH)

        sent = [copy(wi, k) for wi in range(n) for k in range(1, N_DEV)]
        for cp in sent:
            cp.start()
        for wi in range(n):
            for k in range(1, N_DEV):
                arrival(wi, k).wait_recv()
        for cp in sent:
            cp.wait_send()
        for cp in local:
            cp.wait()

    return pl.pallas_call(
        body, in_specs=[ANY] * n, out_specs=[ANY] * n,
        out_shape=[jax.ShapeDtypeStruct(b.shape, b.dtype) for b in blocks],
        scratch_shapes=[pltpu.SemaphoreType.DMA((7 * n,)), pltpu.SemaphoreType.DMA((7 * n,)), pltpu.SemaphoreType.DMA((n,))],
        name="grad_exchange")(*blocks)


def _adamw(w, g, m, v):
    m = ADAM_B1 * m + (1.0 - ADAM_B1) * g
    v = ADAM_B2 * v + (1.0 - ADAM_B2) * (g * g)
    m_hat = m / (1.0 - ADAM_B1 ** ADAM_STEP)
    v_hat = v / (1.0 - ADAM_B2 ** ADAM_STEP)
    delta = -ADAM_LR * (m_hat / (jnp.sqrt(v_hat) + ADAM_EPS) + ADAM_WD * w)
    return delta, m, v


def _reduce_adamw(parts, w, m, v, name):
    R, C = w.shape
    rt = 128 if R % 128 == 0 else 64
    assert R % rt == 0

    def body(p_ref, w_ref, m_ref, v_ref, g_out, d_out, m_out, v_out):
        g = p_ref[0].astype(F32)
        for s in range(1, N_DEV):
            g = g + p_ref[s].astype(F32)
        delta, m2, v2 = _adamw(w_ref[...], g, m_ref[...], v_ref[...])
        g_out[...] = g
        d_out[...] = delta
        m_out[...] = m2
        v_out[...] = v2

    spec = pl.BlockSpec((rt, C), lambda i: (i, 0))
    return pl.pallas_call(
        body, grid=(R // rt,), in_specs=[pl.BlockSpec((N_DEV, rt, C), lambda i: (0, i, 0)), spec, spec, spec],
        out_specs=[spec] * 4, out_shape=[jax.ShapeDtypeStruct((R, C), F32)] * 4,
        compiler_params=_params("parallel"), name=name)(parts, w, m, v)


def _small_step(part, w, m, v):
    R, C = part.shape

    def body(part_ref, w_ref, m_ref, v_ref, g_out, d_out, m_out, v_out, gath, send, recv):
        me = _position()
        gath[_block_of(*me)] = part_ref[...]

        def copy(k, slot):
            return pltpu.make_async_remote_copy(
                src_ref=part_ref, dst_ref=gath.at[slot], send_sem=send.at[k - 1], recv_sem=recv.at[k - 1],
                device_id=_flip(me, k), device_id_type=MESH)

        sent = [copy(k, _block_of(*me)) for k in range(1, N_DEV)]
        for cp in sent:
            cp.start()
        for k in range(1, N_DEV):
            copy(k, _block_of(*_flip(me, k))).wait_recv()
        for cp in sent:
            cp.wait_send()
        g = gath[0]
        for s in range(1, N_DEV):
            g = g + gath[s]
        delta, m2, v2 = _adamw(w_ref[...], g, m_ref[...], v_ref[...])
        g_out[...] = g
        d_out[...] = delta
        m_out[...] = m2
        v_out[...] = v2

    vm = pl.BlockSpec(memory_space=pltpu.VMEM)
    return pl.pallas_call(
        body, in_specs=[vm] * 4, out_specs=[vm] * 4, out_shape=[jax.ShapeDtypeStruct((R, C), F32)] * 4,
        scratch_shapes=[pltpu.VMEM((N_DEV, R, C), F32), pltpu.SemaphoreType.DMA((7,)), pltpu.SemaphoreType.DMA((7,))],
        name="small_step")(part, w, m, v)


COLUMN_SHARDED = ("w_in", "w_sb_out", "w_ca_out", "w_ffn_in", "w_ple_in")
ROW_SHARDED = ("w_mix_out", "w_ffn_out", "w_ple_gate")
BIG = COLUMN_SHARDED + ROW_SHARDED
SMALL = ("g_mix", "g_ffn", "g_ple", "g_final", "rel_bias")
WEIGHTS = ("w_in", "w_sb_out", "w_ca_out", "w_mix_out", "rel_bias", "g_mix", "g_ffn", "g_ple", "g_final",
           "w_ffn_in", "w_ffn_out", "w_ple_in", "w_ple_gate")


def _pack_small(t, D):
    rows = [t[n].reshape(1, D) for n in SMALL[:4]]
    rb = t["rel_bias"].reshape(1, -1)
    rows.append(jnp.pad(rb, ((0, 0), (0, D - rb.shape[1]))))
    return jnp.concatenate(rows + [jnp.zeros((8 - len(rows), D), F32)], axis=0)


def _unpack_small(a, like):
    out = {n: a[i].reshape(like[n].shape) for i, n in enumerate(SMALL[:4])}
    out["rel_bias"] = a[4, :like["rel_bias"].size].reshape(like["rel_bias"].shape)
    return out


def kernel(x, p, w_in, w_sb_out, w_ca_out, w_mix_out, rel_bias, g_mix, g_ffn, g_ple, g_final, w_ffn_in, w_ffn_out, w_ple_in, w_ple_gate, loss_target, m_w_in, m_w_sb_out, m_w_ca_out, m_w_mix_out, m_rel_bias, m_g_mix, m_g_ffn, m_g_ple, m_g_final, m_w_ffn_in, m_w_ffn_out, m_w_ple_in, m_w_ple_gate, v_w_in, v_w_sb_out, v_w_ca_out, v_w_mix_out, v_rel_bias, v_g_mix, v_g_ffn, v_g_ple, v_g_final, v_w_ffn_in, v_w_ffn_out, v_w_ple_in, v_w_ple_gate):
    wts = dict(w_in=w_in, w_sb_out=w_sb_out, w_ca_out=w_ca_out, w_mix_out=w_mix_out, rel_bias=rel_bias, g_mix=g_mix, g_ffn=g_ffn,
               g_ple=g_ple, g_final=g_final, w_ffn_in=w_ffn_in, w_ffn_out=w_ffn_out, w_ple_in=w_ple_in, w_ple_gate=w_ple_gate)
    mom = dict(w_in=m_w_in, w_sb_out=m_w_sb_out, w_ca_out=m_w_ca_out, w_mix_out=m_w_mix_out, rel_bias=m_rel_bias, g_mix=m_g_mix,
               g_ffn=m_g_ffn, g_ple=m_g_ple, g_final=m_g_final, w_ffn_in=m_w_ffn_in, w_ffn_out=m_w_ffn_out, w_ple_in=m_w_ple_in,
               w_ple_gate=m_w_ple_gate)
    var = dict(w_in=v_w_in, w_sb_out=v_w_sb_out, w_ca_out=v_w_ca_out, w_mix_out=v_w_mix_out, rel_bias=v_rel_bias, g_mix=v_g_mix,
               g_ffn=v_g_ffn, g_ple=v_g_ple, g_final=v_g_final, w_ffn_in=v_w_ffn_in, w_ffn_out=v_w_ffn_out, w_ple_in=v_w_ple_in,
               w_ple_gate=v_w_ple_gate)
    T, D = x.shape[1], x.shape[2]
    shard = {n: wts[n].reshape(wts[n].shape[-2:]) for n in BIG}

    gathered = dict(zip(BIG, _all_gather([_cast_bf16(shard[n], "cast_" + n) for n in BIG])))
    w = {n: gathered[n] for n in COLUMN_SHARDED}
    for n in ROW_SHARDED:
        w[n] = gathered[n].reshape(-1, gathered[n].shape[-1])
    g = dict(g_mix=g_mix.reshape(1, D), g_ffn=g_ffn.reshape(1, D), g_ple=g_ple.reshape(1, D), g_final=g_final.reshape(1, D),
             rel_bias=rel_bias.reshape(rel_bias.shape[-2:]))

    loss, grad_x, dw, dsmall = _local_step(x.reshape(T, D), p.reshape(T, -1), loss_target.reshape(T, D), w, g)
    loss = lax.psum(loss[0, 0], ("x", "y", "c"))

    blocks = [dw[n] if n in COLUMN_SHARDED else dw[n].reshape((N_DEV, -1, dw[n].shape[-1])) for n in BIG]
    parts = dict(zip(BIG, _grad_exchange(blocks)))
    grad, delta, new_m, new_v = {}, {}, {}, {}
    for n in BIG:
        outs = _reduce_adamw(parts[n], shard[n], mom[n].reshape(shard[n].shape), var[n].reshape(shard[n].shape), "adamw_" + n)
        grad[n], delta[n], new_m[n], new_v[n] = [o.reshape(wts[n].shape) for o in outs]

    outs = _small_step(_pack_small(dsmall, D), _pack_small(wts, D), _pack_small(mom, D), _pack_small(var, D))
    for dst, a in zip((grad, delta, new_m, new_v), outs):
        dst.update(_unpack_small(a, wts))

    return (loss, grad_x.reshape(x.shape), *[grad[n] for n in WEIGHTS], *[delta[n] for n in WEIGHTS],
            *[new_m[n] for n in WEIGHTS], *[new_v[n] for n in WEIGHTS])
```

```python
import functools

import jax
import jax.numpy as jnp
from jax import lax
from jax.experimental import pallas as pl
from jax.experimental.pallas import tpu as pltpu

F32, BF16 = jnp.float32, jnp.bfloat16

N_DEV = 8
HEAD_DIM = 128
CHUNK = 64
LEFT_CHUNKS = 8
REL_CLIP = 128
N_REL = REL_CLIP + CHUNK
PAIR = 2 * CHUNK
PBAND = (LEFT_CHUNKS + 2) * CHUNK
PAD = LEFT_CHUNKS * CHUNK
SB_BLOCK = 256
ROWS = 256
EPS = 1e-6
NEG = -1e30
SCALE = HEAD_DIM ** -0.5
VMEM_LIMIT_BYTES = 56 * 1024 * 1024

ADAM_LR, ADAM_B1, ADAM_B2, ADAM_EPS, ADAM_WD, ADAM_STEP = 0.001, 0.9, 0.999, 1e-08, 0.01, 10

ANY = pl.BlockSpec(memory_space=pl.ANY)
NN = (((1,), (0,)), ((), ()))
NT = (((1,), (1,)), ((), ()))
TN = (((0,), (0,)), ((), ()))
MESH = pl.DeviceIdType.MESH


def _params(*sem):
    return pltpu.CompilerParams(dimension_semantics=sem or None, vmem_limit_bytes=VMEM_LIMIT_BYTES)


def _dot(a, b, dims=NN):
    return lax.dot_general(a, b, dims, preferred_element_type=F32)


def _mm(a, b, *, mode, tm, tn, tk, out_dtype, name, b_blocked=False, out_block=None, res=None, after=()):
    if mode == "nn":
        M, K = a.shape
        a_spec = pl.BlockSpec((tm, tk), lambda i, j, k: (i, k))
        if b_blocked:
            G, _, nb = b.shape
            N, per = G * nb, nb // tn
            assert nb % tn == 0
            b_spec = pl.BlockSpec((None, tk, tn), lambda i, j, k: (j // per, k, j % per))
        else:
            N = b.shape[1]
            b_spec = pl.BlockSpec((tk, tn), lambda i, j, k: (k, j))
        dims = NN
    elif mode == "nt":
        M, K = a.shape
        a_spec = pl.BlockSpec((tm, tk), lambda i, j, k: (i, k))
        if b_blocked:
            G, N, nb = b.shape
            per = nb // tk
            assert K == G * nb and nb % tk == 0
            b_spec = pl.BlockSpec((None, tn, tk), lambda i, j, k: (k // per, j, k % per))
        else:
            N = b.shape[0]
            b_spec = pl.BlockSpec((tn, tk), lambda i, j, k: (j, k))
        dims = NT
    else:
        K, M = a.shape
        N = b.shape[1]
        a_spec = pl.BlockSpec((tk, tm), lambda i, j, k: (k, i))
        b_spec = pl.BlockSpec((tk, tn), lambda i, j, k: (k, j))
        dims = TN
    assert M % tm == 0 and N % tn == 0 and K % tk == 0, (name, M, N, K, tm, tn, tk)
    nk = K // tk
    if out_block is None:
        out_shape = jax.ShapeDtypeStruct((M, N), out_dtype)
        o_spec = pl.BlockSpec((tm, tn), lambda i, j, k: (i, j))
    else:
        per_o = out_block // tn
        assert out_block % tn == 0
        out_shape = jax.ShapeDtypeStruct((N // out_block, M, out_block), out_dtype)
        o_spec = pl.BlockSpec((None, tm, tn), lambda i, j, k: (j // per_o, i, j % per_o))
    in_specs = [a_spec, b_spec]
    args = [a, b]
    if res is not None:
        in_specs.append(pl.BlockSpec((tm, tn), lambda i, j, k: (i, j)))
        args.append(res)
    n_in = len(args) + len(after)

    def body(*refs):
        a_ref, b_ref = refs[0], refs[1]
        r_ref = refs[2] if res is not None else None
        o_ref = refs[n_in]

        def finish(acc):
            if r_ref is not None:
                acc = acc + r_ref[...]
            o_ref[...] = acc.astype(o_ref.dtype)

        if nk == 1:
            finish(_dot(a_ref[...], b_ref[...], dims))
        else:
            acc_ref = refs[-1]
            k = pl.program_id(2)

            @pl.when(k == 0)
            def _():
                acc_ref[...] = jnp.zeros_like(acc_ref)

            acc_ref[...] += _dot(a_ref[...], b_ref[...], dims)

            @pl.when(k == nk - 1)
            def _():
                finish(acc_ref[...])

    return pl.pallas_call(
        body, grid=(M // tm, N // tn, nk), in_specs=in_specs + [ANY] * len(after), out_specs=o_spec, out_shape=out_shape,
        scratch_shapes=[] if nk == 1 else [pltpu.VMEM((tm, tn), F32)],
        compiler_params=_params("parallel", "parallel", "arbitrary"), name=name)(*args, *after)


def _row_spec(d, col=0):
    return pl.BlockSpec((ROWS, d), lambda i: (i, col))


def _vec_spec(d):
    return pl.BlockSpec((1, d), lambda i: (0, 0))


def _rms(x):
    return lax.rsqrt(jnp.mean(x * x, axis=-1, keepdims=True) + EPS)


def _norm_fwd(x, g, name):
    T, D = x.shape

    def body(x_ref, g_ref, h_ref):
        xv = x_ref[...]
        h_ref[...] = (xv * _rms(xv) * g_ref[...]).astype(BF16)

    return pl.pallas_call(body, grid=(T // ROWS,), in_specs=[_row_spec(D), _vec_spec(D)], out_specs=_row_spec(D),
                          out_shape=jax.ShapeDtypeStruct((T, D), BF16), compiler_params=_params("parallel"), name=name)(x, g)


def _norm_bwd_math(dh, xv, gv):
    r = _rms(xv)
    xhat = xv * r
    dxhat = dh * gv
    dx = r * (dxhat - xhat * jnp.mean(dxhat * xhat, axis=-1, keepdims=True))
    dg = jnp.sum(dh * xhat, axis=0, keepdims=True)
    return dx, dg


def _norm_bwd(dh, x, g, dres, name):
    T, D = x.shape

    def body(dh_ref, x_ref, g_ref, dres_ref, dx_ref, dxb_ref, dg_ref):
        dx, dg = _norm_bwd_math(dh_ref[...], x_ref[...], g_ref[...])
        dx = dx + dres_ref[...]
        dx_ref[...] = dx
        dxb_ref[...] = dx.astype(BF16)

        @pl.when(pl.program_id(0) == 0)
        def _():
            dg_ref[...] = jnp.zeros_like(dg_ref)

        dg_ref[...] += dg

    return pl.pallas_call(
        body, grid=(T // ROWS,), in_specs=[_row_spec(D), _row_spec(D), _vec_spec(D), _row_spec(D)],
        out_specs=[_row_spec(D), _row_spec(D), _vec_spec(D)],
        out_shape=[jax.ShapeDtypeStruct((T, D), F32), jax.ShapeDtypeStruct((T, D), BF16), jax.ShapeDtypeStruct((1, D), F32)],
        compiler_params=_params("arbitrary"), name=name)(dh, x, g, dres)


def _merge_fwd(proj, a_sb, a_ca, D, gate_col):
    T = proj.shape[0]

    def body(gs_ref, gc_ref, a_ref, b_ref, o_ref):
        o_ref[...] = (jax.nn.sigmoid(gs_ref[...]) * a_ref[...] + jax.nn.sigmoid(gc_ref[...]) * b_ref[...]).astype(BF16)

    return pl.pallas_call(
        body, grid=(T // ROWS,), in_specs=[_row_spec(D, gate_col), _row_spec(D, gate_col + 1), _row_spec(D), _row_spec(D)],
        out_specs=_row_spec(D), out_shape=jax.ShapeDtypeStruct((T, D), BF16),
        compiler_params=_params("parallel"), name="merge_fwd")(proj, proj, a_sb, a_ca)


def _merge_bwd(dm, proj, a_sb, a_ca, D, gate_col):
    T = proj.shape[0]

    def body(dm_ref, gs_ref, gc_ref, a_ref, b_ref, da_ref, db_ref, dgs_ref, dgc_ref):
        dmv = dm_ref[...]
        ss, sc = jax.nn.sigmoid(gs_ref[...]), jax.nn.sigmoid(gc_ref[...])
        da_ref[...] = (dmv * ss).astype(BF16)
        db_ref[...] = (dmv * sc).astype(BF16)
        dgs_ref[...] = (dmv * a_ref[...] * ss * (1.0 - ss)).astype(BF16)
        dgc_ref[...] = (dmv * b_ref[...] * sc * (1.0 - sc)).astype(BF16)

    return pl.pallas_call(
        body, grid=(T // ROWS,),
        in_specs=[_row_spec(D), _row_spec(D, gate_col), _row_spec(D, gate_col + 1), _row_spec(D), _row_spec(D)],
        out_specs=[_row_spec(D)] * 4, out_shape=[jax.ShapeDtypeStruct((T, D), BF16)] * 4,
        compiler_params=_params("parallel"), name="merge_bwd")(dm, proj, proj, a_sb, a_ca)


def _swiglu_fwd(gu, after=()):
    T, F2 = gu.shape
    F = F2 // 2

    def body(g_ref, u_ref, *rest):
        gv = g_ref[...]
        rest[-1][...] = (gv * jax.nn.sigmoid(gv) * u_ref[...]).astype(BF16)

    return pl.pallas_call(body, grid=(T // ROWS,), in_specs=[_row_spec(F, 0), _row_spec(F, 1)] + [ANY] * len(after), out_specs=_row_spec(F),
                          out_shape=jax.ShapeDtypeStruct((T, F), BF16), compiler_params=_params("parallel"), name="swiglu_fwd")(gu, gu, *after)


def _swiglu_bwd(dact, gu):
    T, F2 = gu.shape
    F = F2 // 2

    def body(d_ref, g_ref, u_ref, o_ref):
        dv, gv, uv = d_ref[...], g_ref[...], u_ref[...]
        s = jax.nn.sigmoid(gv)
        o_ref[:, 0:F] = (dv * uv * s * (1.0 + gv * (1.0 - s))).astype(BF16)
        o_ref[:, F:F2] = (dv * gv * s).astype(BF16)

    return pl.pallas_call(body, grid=(T // ROWS,), in_specs=[_row_spec(F), _row_spec(F, 0), _row_spec(F, 1)], out_specs=_row_spec(F2),
                          out_shape=jax.ShapeDtypeStruct((T, F2), BF16), compiler_params=_params("parallel"), name="swiglu_bwd")(dact, gu, gu)


def _tail(x3, zg, pe, g_final, target):
    T, D = x3.shape

    def body(x3_ref, zg_ref, pe_ref, g_ref, t_ref, loss_ref, dx_ref, dpe_ref, dzg_ref, dg_ref):
        gate = jax.nn.sigmoid(zg_ref[...])
        pev = pe_ref[...]
        x4 = x3_ref[...] + gate * pev
        gv = g_ref[...]
        xhat = x4 * _rms(x4)
        err = xhat * gv - t_ref[...]
        part = 0.5 * jnp.sum(jnp.mean(err * err, axis=-1, keepdims=True), axis=0, keepdims=True)
        dx, dg = _norm_bwd_math(err * (1.0 / D), x4, gv)
        dx_ref[...] = dx
        dpe_ref[...] = (dx * gate).astype(BF16)
        dzg_ref[...] = (dx * pev * gate * (1.0 - gate)).astype(BF16)

        @pl.when(pl.program_id(0) == 0)
        def _():
            dg_ref[...] = jnp.zeros_like(dg_ref)
            loss_ref[...] = jnp.zeros_like(loss_ref)

        dg_ref[...] += dg
        loss_ref[...] += jnp.broadcast_to(part, loss_ref.shape)

    return pl.pallas_call(
        body, grid=(T // ROWS,), in_specs=[_row_spec(D), _row_spec(D), _row_spec(D), _vec_spec(D), _row_spec(D)],
        out_specs=[_vec_spec(128), _row_spec(D), _row_spec(D), _row_spec(D), _vec_spec(D)],
        out_shape=[jax.ShapeDtypeStruct((1, 128), F32), jax.ShapeDtypeStruct((T, D), F32), jax.ShapeDtypeStruct((T, D), BF16),
                   jax.ShapeDtypeStruct((T, D), BF16), jax.ShapeDtypeStruct((1, D), F32)],
        compiler_params=_params("arbitrary"), name="tail")(x3, zg, pe, g_final, target)


def _cast_bf16(x, name):
    R, C = x.shape
    rows = next(r for r in (ROWS, 128, 64, 32, 16) if R % r == 0)

    def body(x_ref, o_ref):
        o_ref[...] = x_ref[...].astype(BF16)

    spec = pl.BlockSpec((rows, C), lambda i: (i, 0))
    return pl.pallas_call(body, grid=(R // rows,), in_specs=[spec], out_specs=spec, out_shape=jax.ShapeDtypeStruct((R, C), BF16),
                          compiler_params=_params("parallel"), name=name)(x)


def _head_spec(T, col0):
    return pl.BlockSpec((T, HEAD_DIM), lambda h, *_: (0, col0 + h))


def _sb_tile(qv, k_ref, v_ref, qb, kb, c_lk):
    B = SB_BLOCK
    ks = pl.multiple_of(kb * B, B)
    kk = k_ref[pl.ds(ks, B), :].astype(BF16)
    vv = v_ref[pl.ds(ks, B), :].astype(BF16)
    z = _dot(qv, kk, NT) * SCALE
    t_idx = qb * B + lax.broadcasted_iota(jnp.int32, (B, B), 0)
    s_idx = kb * B + lax.broadcasted_iota(jnp.int32, (B, B), 1)
    past = s_idx < t_idx
    sp = jnp.log(1.0 + jnp.exp(-jnp.abs(z)))
    ls_pos = jnp.minimum(z, 0.0) - sp
    lk = jnp.where(past, jnp.minimum(-z, 0.0) - sp, 0.0)
    between = c_lk + _lane_scan(lk, right=True)
    a = jnp.where(past, jnp.exp(ls_pos + between), 0.0)
    return kk, vv, past, ls_pos, lk, a


def _lane_scan(x, right):
    n = x.shape[1]
    j = lax.broadcasted_iota(jnp.int32, (n, n), 0)
    s = lax.broadcasted_iota(jnp.int32, (n, n), 1)
    tri = jnp.where((j > s) if right else (j < s), 1.0, 0.0).astype(BF16)
    hi = x.astype(BF16)
    lo = (x - hi.astype(F32)).astype(BF16)
    return _dot(hi, tri) + _dot(lo, tri)


def _sb_fwd(proj, n_heads, after=()):
    T = proj.shape[0]
    B = SB_BLOCK

    def body(q_ref, k_ref, v_ref, *rest):
        y_ref = rest[-1]
        qb = pl.program_id(1)
        qv = q_ref[...].astype(BF16)

        def tile(i, carry):
            acc, c_lk = carry
            kb = qb - i
            _, vv, _, _, lk, a = _sb_tile(qv, k_ref, v_ref, qb, kb, c_lk)
            return acc + _dot(a.astype(BF16), vv), c_lk + jnp.sum(lk, axis=1, keepdims=True)

        acc, _ = lax.fori_loop(0, qb + 1, tile, (jnp.zeros((B, HEAD_DIM), F32), jnp.zeros((B, 1), F32)))
        y_ref[...] = acc.astype(BF16)

    blk = pl.BlockSpec((B, HEAD_DIM), lambda h, i: (i, h))
    return pl.pallas_call(
        body, grid=(n_heads, T // B),
        in_specs=[blk, _head_spec(T, n_heads), _head_spec(T, 2 * n_heads)] + [ANY] * len(after), out_specs=blk,
        out_shape=jax.ShapeDtypeStruct((T, n_heads * HEAD_DIM), BF16),
        compiler_params=_params("parallel", "arbitrary"), name="sb_fwd")(proj, proj, proj, *after)


def _sb_bwd(proj, dy, n_heads):
    T = proj.shape[0]
    B = SB_BLOCK

    def body(q_ref, k_ref, v_ref, dy_ref, dq_ref, dk_ref, dv_ref, g_s, sig_s):
        qb = pl.program_id(1)

        @pl.when(qb == 0)
        def _():
            dk_ref[...] = jnp.zeros_like(dk_ref)
            dv_ref[...] = jnp.zeros_like(dv_ref)

        qv = q_ref[...].astype(BF16)
        dyb = dy_ref[...].astype(BF16)

        def sweep(i, c_lk):
            kb = qb - i
            ks = pl.multiple_of(kb * B, B)
            _, vv, _, ls_pos, lk, a = _sb_tile(qv, k_ref, v_ref, qb, kb, c_lk)
            g_s[kb] = _dot(dyb, vv, NT) * a
            sig_s[kb] = jnp.exp(ls_pos)
            dv_ref[pl.ds(ks, B), :] += _dot(a.astype(BF16), dyb, TN)
            return c_lk + jnp.sum(lk, axis=1, keepdims=True)

        lax.fori_loop(0, qb + 1, sweep, jnp.zeros((B, 1), F32))

        def back(kb, carry):
            dq, c_g = carry
            ks = pl.multiple_of(kb * B, B)
            kk = k_ref[pl.ds(ks, B), :].astype(BF16)
            g, sig = g_s[kb], sig_s[kb]
            t_idx = qb * B + lax.broadcasted_iota(jnp.int32, (B, B), 0)
            s_idx = kb * B + lax.broadcasted_iota(jnp.int32, (B, B), 1)
            before = c_g + _lane_scan(g, right=False)
            dz = (jnp.where(s_idx < t_idx, g * (1.0 - sig) - before * sig, 0.0) * SCALE).astype(BF16)
            dk_ref[pl.ds(ks, B), :] += _dot(dz, qv, TN)
            return dq + _dot(dz, kk), c_g + jnp.sum(g, axis=1, keepdims=True)

        dq, _ = lax.fori_loop(0, qb + 1, back, (jnp.zeros((B, HEAD_DIM), F32), jnp.zeros((B, 1), F32)))
        dq_ref[...] = dq

    blk = pl.BlockSpec((B, HEAD_DIM), lambda h, i: (i, h))
    full = _head_spec(T, 0)
    shp = jax.ShapeDtypeStruct((T, n_heads * HEAD_DIM), F32)
    return pl.pallas_call(
        body, grid=(n_heads, T // B),
        in_specs=[blk, _head_spec(T, n_heads), _head_spec(T, 2 * n_heads), blk], out_specs=[blk, full, full],
        out_shape=[shp, shp, shp], scratch_shapes=[pltpu.VMEM((T // B, B, B), F32)] * 2,
        compiler_params=_params("parallel", "arbitrary"), name="sb_bwd")(proj, proj, proj, dy)


def _rel_index(cols, col0):
    i = lax.broadcasted_iota(jnp.int32, (PAIR, cols), 0)
    j = col0 + lax.broadcasted_iota(jnp.int32, (PAIR, cols), 1)
    return jnp.clip(j - i - PAD, -REL_CLIP, CHUNK - 1) + REL_CLIP


NEAR0 = PBAND - 2 * PAIR


def _bias_expand(rel_bias):
    H = rel_bias.shape[0]

    def body(rb_ref, o_ref):
        h = pl.program_id(0)
        o_ref[:, 0:NEAR0] = jnp.full((PAIR, NEAR0), rb_ref[h, 0], F32)
        idx = _rel_index(2 * PAIR, NEAR0)

        def step(r, acc):
            return jnp.where(idx == r, rb_ref[h, r], acc)

        o_ref[:, NEAR0:PBAND] = lax.fori_loop(0, N_REL, step, jnp.zeros((PAIR, 2 * PAIR), F32))

    return pl.pallas_call(
        body, grid=(H,), in_specs=[pl.BlockSpec(memory_space=pltpu.SMEM)],
        out_specs=pl.BlockSpec((None, PAIR, PBAND), lambda h: (h, 0, 0)),
        out_shape=jax.ShapeDtypeStruct((H, PAIR, PBAND), F32), compiler_params=_params("parallel"), name="bias_expand")(rel_bias)


def _bias_reduce(dbias):
    H = dbias.shape[0]

    def body(d_ref, o_ref):
        far = jnp.sum(jnp.sum(d_ref[:, 0:NEAR0], axis=1, keepdims=True), axis=0, keepdims=True)
        near = d_ref[:, NEAR0:PBAND]
        idx = _rel_index(2 * PAIR, NEAR0)
        lane = lax.broadcasted_iota(jnp.int32, (1, 2 * PAIR), 1)

        def step(r, acc):
            s = jnp.sum(jnp.sum(jnp.where(idx == r, near, 0.0), axis=1, keepdims=True), axis=0, keepdims=True)
            return acc + jnp.where(lane == r, s, 0.0)

        out = lax.fori_loop(0, N_REL, step, jnp.zeros((1, 2 * PAIR), F32))
        o_ref[...] = out + jnp.where(lane == 0, far, 0.0)

    return pl.pallas_call(
        body, grid=(H,), in_specs=[pl.BlockSpec((None, PAIR, PBAND), lambda h: (h, 0, 0))],
        out_specs=pl.BlockSpec((None, 1, 2 * PAIR), lambda h: (h, 0, 0)),
        out_shape=jax.ShapeDtypeStruct((H, 1, 2 * PAIR), F32), compiler_params=_params("parallel"), name="bias_reduce")(dbias)


def _ca_pair(pr, q_ref, kpad, vpad, bias):
    r0 = pl.multiple_of(pr * PAIR, PAIR)
    qp = q_ref[pl.ds(r0, PAIR), :].astype(BF16)
    kb = kpad[pl.ds(r0, PBAND), :]
    vb = vpad[pl.ds(r0, PBAND), :]
    i = lax.broadcasted_iota(jnp.int32, (PAIR, PBAND), 0)
    j = lax.broadcasted_iota(jnp.int32, (PAIR, PBAND), 1)
    qc, kc = i // CHUNK, j // CHUNK
    valid = (kc >= qc) & (kc <= qc + LEFT_CHUNKS) & (pr * PAIR + j >= PAD)
    z = jnp.where(valid, _dot(qp, kb, NT) * SCALE + bias, NEG)
    e = jnp.exp(z - jnp.max(z, axis=1, keepdims=True))
    return qp, kb, vb, e / jnp.sum(e, axis=1, keepdims=True)


def _ca_fill(k_ref, v_ref, kpad, vpad):
    T = k_ref.shape[0]
    kpad[0:PAD, :] = jnp.zeros((PAD, HEAD_DIM), BF16)
    vpad[0:PAD, :] = jnp.zeros((PAD, HEAD_DIM), BF16)
    kpad[PAD:PAD + T, :] = k_ref[...].astype(BF16)
    vpad[PAD:PAD + T, :] = v_ref[...].astype(BF16)


def _ca_fwd(proj, bias, n_heads, col0):
    T = proj.shape[0]

    def body(q_ref, k_ref, v_ref, b_ref, y_ref, kpad, vpad):
        _ca_fill(k_ref, v_ref, kpad, vpad)
        bias_v = b_ref[...]

        def pair(pr, _):
            _, _, vb, w = _ca_pair(pr, q_ref, kpad, vpad, bias_v)
            y_ref[pl.ds(pl.multiple_of(pr * PAIR, PAIR), PAIR), :] = _dot(w.astype(BF16), vb).astype(BF16)
            return 0

        lax.fori_loop(0, T // PAIR, pair, 0)

    return pl.pallas_call(
        body, grid=(n_heads,),
        in_specs=[_head_spec(T, col0), _head_spec(T, col0 + n_heads), _head_spec(T, col0 + 2 * n_heads),
                  pl.BlockSpec((None, PAIR, PBAND), lambda h: (h, 0, 0))],
        out_specs=_head_spec(T, 0), out_shape=jax.ShapeDtypeStruct((T, n_heads * HEAD_DIM), BF16),
        scratch_shapes=[pltpu.VMEM((PAD + T, HEAD_DIM), BF16)] * 2,
        compiler_params=_params("parallel"), name="ca_fwd")(proj, proj, proj, bias)


def _ca_bwd(proj, bias, dy, n_heads, col0):
    T = proj.shape[0]

    def body(q_ref, k_ref, v_ref, b_ref, dy_ref, dq_ref, dk_ref, dv_ref, db_ref, kpad, vpad, dkpad, dvpad):
        _ca_fill(k_ref, v_ref, kpad, vpad)
        dkpad[...] = jnp.zeros_like(dkpad)
        dvpad[...] = jnp.zeros_like(dvpad)
        db_ref[...] = jnp.zeros_like(db_ref)
        bias_v = b_ref[...]

        def pair(pr, _):
            r0 = pl.multiple_of(pr * PAIR, PAIR)
            qp, kb, vb, w = _ca_pair(pr, q_ref, kpad, vpad, bias_v)
            dyp = dy_ref[pl.ds(r0, PAIR), :].astype(BF16)
            dw = _dot(dyp, vb, NT)
            dz = w * (dw - jnp.sum(dw * w, axis=1, keepdims=True))
            db_ref[...] += dz
            dzs = (dz * SCALE).astype(BF16)
            dq_ref[pl.ds(r0, PAIR), :] = _dot(dzs, kb)
            dkpad[pl.ds(r0, PBAND), :] += _dot(dzs, qp, TN)
            dvpad[pl.ds(r0, PBAND), :] += _dot(w.astype(BF16), dyp, TN)
            return 0

        lax.fori_loop(0, T // PAIR, pair, 0)
        dk_ref[...] = dkpad[PAD:PAD + T, :]
        dv_ref[...] = dvpad[PAD:PAD + T, :]

    full = _head_spec(T, 0)
    bspec = pl.BlockSpec((None, PAIR, PBAND), lambda h: (h, 0, 0))
    shp = jax.ShapeDtypeStruct((T, n_heads * HEAD_DIM), F32)
    return pl.pallas_call(
        body, grid=(n_heads,),
        in_specs=[_head_spec(T, col0), _head_spec(T, col0 + n_heads), _head_spec(T, col0 + 2 * n_heads), bspec, full],
        out_specs=[full, full, full, bspec],
        out_shape=[shp, shp, shp, jax.ShapeDtypeStruct((n_heads, PAIR, PBAND), F32)],
        scratch_shapes=[pltpu.VMEM((PAD + T, HEAD_DIM), BF16)] * 2 + [pltpu.VMEM((PAD + T, HEAD_DIM), F32)] * 2,
        compiler_params=_params("parallel"), name="ca_bwd")(proj, proj, proj, bias, dy)


def _local_step(x, p, target, comm, g):
    T, D = x.shape
    H = g["rel_bias"].shape[0]
    W = H * HEAD_DIM
    nb_in = comm.shapes["w_in"][2]
    nb_ff = comm.shapes["w_ffn_in"][2]
    nb_o = comm.shapes["w_sb_out"][2]
    nb_p = comm.shapes["w_ple_in"][2]
    tm = min(T, 1024)
    tn = min(D, 1024)
    gate_col = 6 * W // D

    h1 = _norm_fwd(x, g["g_mix"], "norm1")
    proj = _mm(h1, comm.weight("w_in"), mode="nn", tm=tm, tn=nb_in, tk=D, out_dtype=F32, b_blocked=True, after=comm.pending(), name="mm_in")
    comm.stage("mm_in", proj)
    y_sb = _sb_fwd(proj, H, comm.pending())
    bias = _bias_expand(g["rel_bias"])
    y_ca = _ca_fwd(proj, bias, H, 3 * H)
    comm.stage("attention", y_sb, y_ca)
    a_sb = _mm(y_sb, comm.weight("w_sb_out", y_ca), mode="nn", tm=tm, tn=nb_o, tk=W, out_dtype=F32, b_blocked=True, after=comm.pending(), name="mm_sb_out")
    a_ca = _mm(y_ca, comm.weight("w_ca_out"), mode="nn", tm=tm, tn=nb_o, tk=W, out_dtype=F32, b_blocked=True, name="mm_ca_out")
    merged = _merge_fwd(proj, a_sb, a_ca, D, gate_col)
    x2 = _mm(merged, comm.weight("w_mix_out"), mode="nn", tm=tm, tn=tn, tk=D, out_dtype=F32, res=x, name="mm_mix")
    h2 = _norm_fwd(x2, g["g_ffn"], "norm2")
    gu = _mm(h2, comm.weight("w_ffn_in", h2), mode="nn", tm=tm, tn=nb_ff, tk=D, out_dtype=F32, b_blocked=True, name="mm_ffn_in")
    comm.stage("mm_ffn_in", gu)
    act = _swiglu_fwd(gu, comm.pending())
    F = act.shape[1]
    tkf = F // 2 if F % 256 == 0 else F
    x3 = _mm(act, comm.weight("w_ffn_out", act), mode="nn", tm=tm, tn=tn, tk=tkf, out_dtype=F32, res=x2, name="mm_ffn_out")
    h3 = _norm_fwd(x3, g["g_ple"], "norm3")
    zg = _mm(h3, comm.weight("w_ple_gate"), mode="nn", tm=tm, tn=tn, tk=D, out_dtype=F32, name="mm_ple_gate")
    pb = _cast_bf16(p, "cast_p")
    P = p.shape[1]
    pe = _mm(pb, comm.weight("w_ple_in"), mode="nn", tm=tm, tn=nb_p, tk=P, out_dtype=F32, b_blocked=True, name="mm_ple_in")
    loss, dx4, dpe, dzg, dg_final = _tail(x3, zg, pe, g["g_final"], target)

    tw = min(D, 512)
    DW = BF16
    comm.grad("w_ple_in", _mm(pb, dpe, mode="tn", tm=P, tn=nb_p, tk=T, out_dtype=DW, out_block=nb_p, name="mm_d_ple_in"))
    comm.grad("w_ple_gate", _mm(h3, dzg, mode="tn", tm=tw, tn=tn, tk=T, out_dtype=DW, name="mm_d_ple_gate"))
    dh3 = _mm(dzg, comm.weight("w_ple_gate"), mode="nt", tm=tm, tn=tn, tk=D, out_dtype=F32, after=comm.pending(), name="mm_dh3")
    dx3, dx3b, dg_ple = _norm_bwd(dh3, x3, g["g_ple"], dx4, "norm3_bwd")
    comm.grad("w_ffn_out", _mm(act, dx3b, mode="tn", tm=F // 4, tn=tn, tk=T, out_dtype=DW, name="mm_d_ffn_out"))
    dact = _mm(dx3b, comm.weight("w_ffn_out"), mode="nt", tm=tm, tn=tkf, tk=D, out_dtype=F32, after=comm.pending(), name="mm_dact")
    dgu = _swiglu_bwd(dact, gu)
    comm.grad("w_ffn_in", _mm(h2, dgu, mode="tn", tm=tw, tn=nb_ff, tk=T, out_dtype=DW, out_block=nb_ff, name="mm_d_ffn_in"))
    dh2 = _mm(dgu, comm.weight("w_ffn_in"), mode="nt", tm=tm, tn=tn, tk=nb_ff, out_dtype=F32, b_blocked=True, after=comm.pending(), name="mm_dh2")
    dx2, dx2b, dg_ffn = _norm_bwd(dh2, x2, g["g_ffn"], dx3, "norm2_bwd")
    dmerged = _mm(dx2b, comm.weight("w_mix_out"), mode="nt", tm=tm, tn=tn, tk=D, out_dtype=F32, name="mm_dmerged")
    da_sb, da_ca, dgate_sb, dgate_ca = _merge_bwd(dmerged, proj, a_sb, a_ca, D, gate_col)
    comm.grad("w_mix_out", _mm(merged, dx2b, mode="tn", tm=tw, tn=tn, tk=T, out_dtype=DW, name="mm_d_mix"))
    comm.grad("w_sb_out", _mm(y_sb, da_sb, mode="tn", tm=min(W, 512), tn=nb_o, tk=T, out_dtype=DW, out_block=nb_o, name="mm_d_sb_out"))
    comm.grad("w_ca_out", _mm(y_ca, da_ca, mode="tn", tm=min(W, 512), tn=nb_o, tk=T, out_dtype=DW, out_block=nb_o, name="mm_d_ca_out"))
    dy_sb = _mm(da_sb, comm.weight("w_sb_out"), mode="nt", tm=tm, tn=W, tk=nb_o, out_dtype=F32, b_blocked=True, after=comm.pending(), name="mm_dy_sb")
    dy_ca = _mm(da_ca, comm.weight("w_ca_out"), mode="nt", tm=tm, tn=W, tk=nb_o, out_dtype=F32, b_blocked=True, name="mm_dy_ca")
    dq_sb, dk_sb, dv_sb = _sb_bwd(proj, dy_sb, H)
    dq_ca, dk_ca, dv_ca, dbias = _ca_bwd(proj, bias, dy_ca, H, 3 * H)
    d_rel = _bias_reduce(dbias)[:, 0, :N_REL]
    dproj = jnp.concatenate([t.astype(BF16) for t in (dq_sb, dk_sb, dv_sb, dq_ca, dk_ca, dv_ca)] + [dgate_sb, dgate_ca], axis=1)
    comm.grad("w_in", _mm(h1, dproj, mode="tn", tm=tw, tn=nb_in, tk=T, out_dtype=DW, out_block=nb_in, name="mm_d_in"))
    dh1 = _mm(dproj, comm.weight("w_in"), mode="nt", tm=tm, tn=tn, tk=nb_in, out_dtype=F32, b_blocked=True, after=comm.pending(), name="mm_dh1")
    grad_x, _, dg_mix = _norm_bwd(dh1, x, g["g_mix"], dx2, "norm1_bwd")
    small = dict(g_mix=dg_mix, g_ffn=dg_ffn, g_ple=dg_ple, g_final=dg_final, rel_bias=d_rel)
    return loss, grad_x, small


def _position():
    x, y, c = lax.axis_index("x"), lax.axis_index("y"), lax.axis_index("c")
    return x, y, c


def _block_of(px, py, pc):
    return 4 * px + 2 * py + pc


def _flip(pos, k):
    x, y, c = pos
    return (1 - x if k & 4 else x, 1 - y if k & 2 else y, 1 - c if k & 1 else c)


def _all_gather(shards):
    n = len(shards)

    def body(*refs):
        ins, outs = refs[:n], refs[n:2 * n]
        send, recv, lsem = refs[2 * n:]
        x, y, c = _position()
        me, sibling = (x, y, c), (x, y, 1 - c)
        chips = [(1 - x, y), (x, 1 - y), (1 - x, 1 - y)]

        def copy(wi, k, block, to, src=None):
            rows = outs[wi].at[_block_of(*block)]
            return pltpu.make_async_remote_copy(
                src_ref=rows if src is None else src, dst_ref=rows, send_sem=send.at[wi * 7 + k], recv_sem=recv.at[wi * 7 + k],
                device_id=to, device_id_type=MESH)

        local = [pltpu.make_async_copy(ins[wi], outs[wi].at[_block_of(*me)], lsem.at[wi]) for wi in range(n)]
        for cp in local:
            cp.start()
        first = []
        for wi in range(n):
            first += [copy(wi, 1 + j, me, (*chip, c), src=ins[wi]) for j, chip in enumerate(chips)]
            first.append(copy(wi, 0, me, sibling, src=ins[wi]))
        for cp in first:
            cp.start()
        passed = []
        for wi in range(n):
            for j, chip in enumerate(chips):
                copy(wi, 1 + j, (*chip, c), me).wait_recv()
                fwd = copy(wi, 4 + j, (*chip, c), sibling)
                fwd.start()
                passed.append(fwd)
        for wi in range(n):
            copy(wi, 0, sibling, me).wait_recv()
            for j, chip in enumerate(chips):
                copy(wi, 4 + j, (*chip, 1 - c), me).wait_recv()
        for cp in first + passed:
            cp.wait_send()
        for cp in local:
            cp.wait()

    return pl.pallas_call(
        body, in_specs=[ANY] * n, out_specs=[ANY] * n,
        out_shape=[jax.ShapeDtypeStruct((N_DEV,) + s.shape, s.dtype) for s in shards],
        scratch_shapes=[pltpu.SemaphoreType.DMA((7 * n,)), pltpu.SemaphoreType.DMA((7 * n,)), pltpu.SemaphoreType.DMA((n,))],
        name="all_gather")(*shards)


HBM = pl.BlockSpec(memory_space=pltpu.HBM)
SEM = pl.BlockSpec(memory_space=pltpu.SEMAPHORE)
VMEM_SPEC = pl.BlockSpec(memory_space=pltpu.VMEM)
EFFECT = pltpu.SideEffectType.DATAFLOW_SIDE_EFFECTING
TOKEN = jax.ShapeDtypeStruct((8, 128), F32)


def _hbm(a):
    return pltpu.HBM(a.shape, a.dtype)


def _landing(shape, dtype):
    return pltpu.with_memory_space_constraint(lax.empty(shape, dtype), pltpu.HBM)


def _gather_start(lands, after, name):
    n = len(lands)

    def body(*refs):
        ins = refs[:n]
        send, recv = refs[n + 1], refs[n + 2]
        token = refs[-1]
        x, y, c = _position()
        mine = _block_of(x, y, c)
        peers = [(x, y, 1 - c), (1 - x, y, c), (x, 1 - y, c), (1 - x, 1 - y, c)]
        for wi in range(n):
            for k, peer in enumerate(peers):
                pltpu.make_async_remote_copy(
                    src_ref=ins[wi].at[mine], dst_ref=ins[wi].at[mine], send_sem=send.at[4 * wi + k], recv_sem=recv.at[4 * wi + k],
                    device_id=peer, device_id_type=MESH).start()
        token[...] = jnp.zeros_like(token)

    outs = pl.pallas_call(
        body, name=name, in_specs=[HBM] * n + [ANY], out_specs=(SEM, SEM, *[HBM] * n, VMEM_SPEC),
        out_shape=(pltpu.SemaphoreType.DMA((4 * n,)), pltpu.SemaphoreType.DMA((4 * n,)), *[_hbm(a) for a in lands], TOKEN),
        input_output_aliases={i: 2 + i for i in range(n)},
        compiler_params=pltpu.CompilerParams(has_side_effects=EFFECT))(*[pltpu.with_memory_space_constraint(a, pltpu.HBM) for a in lands], after)
    return outs[0], outs[1], list(outs[2:2 + n]), outs[-1]


def _gather_forward(lands, send0, recv0, after, name):
    n = len(lands)

    def body(*refs):
        ins = refs[:n]
        send0, recv0 = refs[n], refs[n + 1]
        send1, recv1 = refs[n + 2 + len(after)], refs[n + 3 + len(after)]
        token = refs[-1]
        x, y, c = _position()
        chips = [(1 - x, y), (x, 1 - y), (1 - x, 1 - y)]
        for wi in range(n):
            for j, chip in enumerate(chips):
                rows = ins[wi].at[_block_of(*chip, c)]
                pltpu.make_async_remote_copy(
                    src_ref=rows, dst_ref=rows, send_sem=send0.at[4 * wi + 1 + j], recv_sem=recv0.at[4 * wi + 1 + j],
                    device_id=(*chip, c), device_id_type=MESH).wait_recv()
                pltpu.make_async_remote_copy(
                    src_ref=rows, dst_ref=rows, send_sem=send1.at[3 * wi + j], recv_sem=recv1.at[3 * wi + j],
                    device_id=(x, y, 1 - c), device_id_type=MESH).start()
        token[...] = jnp.zeros_like(token)

    outs = pl.pallas_call(
        body, name=name, in_specs=[HBM] * n + [SEM, SEM] + [ANY] * len(after), out_specs=(SEM, SEM, *[HBM] * n, VMEM_SPEC),
        out_shape=(pltpu.SemaphoreType.DMA((3 * n,)), pltpu.SemaphoreType.DMA((3 * n,)), *[_hbm(a) for a in lands], TOKEN),
        input_output_aliases={i: 2 + i for i in range(n)},
        compiler_params=pltpu.CompilerParams(has_side_effects=EFFECT))(*lands, send0, recv0, *after)
    return outs[0], outs[1], list(outs[2:2 + n]), outs[-1]


def _gather_wait(lands, send0, recv0, send1, recv1, after, name):
    n = len(lands)

    def body(*refs):
        ins = refs[:n]
        send0, recv0, send1, recv1 = refs[n:n + 4]
        x, y, c = _position()
        mine = _block_of(x, y, c)
        sibling = (x, y, 1 - c)
        peers = [sibling, (1 - x, y, c), (x, 1 - y, c), (1 - x, 1 - y, c)]
        chips = [(1 - x, y), (x, 1 - y), (1 - x, 1 - y)]
        for wi in range(n):
            own = ins[wi].at[mine]
            for k, peer in enumerate(peers):
                pltpu.make_async_remote_copy(src_ref=own, dst_ref=own, send_sem=send0.at[4 * wi + k], recv_sem=recv0.at[4 * wi + k],
                                             device_id=peer, device_id_type=MESH).wait_send()
            theirs = ins[wi].at[_block_of(*sibling)]
            pltpu.make_async_remote_copy(src_ref=theirs, dst_ref=theirs, send_sem=send0.at[4 * wi], recv_sem=recv0.at[4 * wi],
                                         device_id=sibling, device_id_type=MESH).wait_recv()
            for j, chip in enumerate(chips):
                sent = ins[wi].at[_block_of(*chip, c)]
                got = ins[wi].at[_block_of(*chip, 1 - c)]
                pltpu.make_async_remote_copy(src_ref=sent, dst_ref=sent, send_sem=send1.at[3 * wi + j], recv_sem=recv1.at[3 * wi + j],
                                             device_id=sibling, device_id_type=MESH).wait_send()
                pltpu.make_async_remote_copy(src_ref=got, dst_ref=got, send_sem=send1.at[3 * wi + j], recv_sem=recv1.at[3 * wi + j],
                                             device_id=sibling, device_id_type=MESH).wait_recv()

    outs = pl.pallas_call(
        body, name=name, in_specs=[HBM] * n + [SEM] * 4 + [ANY], out_specs=tuple([HBM] * n),
        out_shape=tuple(_hbm(a) for a in lands), input_output_aliases={i: i for i in range(n)},
        compiler_params=pltpu.CompilerParams(has_side_effects=EFFECT))(*lands, send0, recv0, send1, recv1, after)
    return list(outs)


def _exchange_start(blocks, name):
    n = len(blocks)

    def body(*refs):
        srcs, lands = refs[:n], refs[n:2 * n]
        send, recv = refs[2 * n], refs[2 * n + 1]
        token = refs[-1]
        me = _position()
        for wi in range(n):
            for k in range(1, N_DEV):
                peer = _flip(me, k)
                pltpu.make_async_remote_copy(
                    src_ref=srcs[wi].at[_block_of(*peer)], dst_ref=lands[wi].at[k - 1], send_sem=send.at[7 * wi + k - 1],
                    recv_sem=recv.at[7 * wi + k - 1], device_id=peer, device_id_type=MESH).start()
        token[...] = jnp.zeros_like(token)

    zones = [_landing((N_DEV - 1,) + b.shape[1:], b.dtype) for b in blocks]
    outs = pl.pallas_call(
        body, name=name, in_specs=[HBM] * (2 * n), out_specs=(SEM, SEM, *[HBM] * (2 * n), VMEM_SPEC),
        out_shape=(pltpu.SemaphoreType.DMA((7 * n,)), pltpu.SemaphoreType.DMA((7 * n,)), *[_hbm(a) for a in blocks],
                   *[_hbm(z) for z in zones], TOKEN),
        input_output_aliases={i: 2 + i for i in range(2 * n)},
        compiler_params=pltpu.CompilerParams(has_side_effects=EFFECT))(
            *[pltpu.with_memory_space_constraint(b, pltpu.HBM) for b in blocks], *zones)
    return outs[0], outs[1], list(outs[2:2 + n]), list(outs[2 + n:2 + 2 * n]), outs[-1]


def _exchange_wait(groups, after, name):
    flat, counts = [], []
    for send, recv, blocks, zones in groups:
        flat += [*blocks, *zones, send, recv]
        counts.append(len(blocks))

    def body(*refs):
        me = _position()
        pos = 0
        for n in counts:
            srcs, lands = refs[pos:pos + n], refs[pos + n:pos + 2 * n]
            send, recv = refs[pos + 2 * n], refs[pos + 2 * n + 1]
            pos += 2 * n + 2
            for wi in range(n):
                for k in range(1, N_DEV):
                    peer = _flip(me, k)
                    cp = pltpu.make_async_remote_copy(
                        src_ref=srcs[wi].at[_block_of(*peer)], dst_ref=lands[wi].at[k - 1], send_sem=send.at[7 * wi + k - 1],
                        recv_sem=recv.at[7 * wi + k - 1], device_id=peer, device_id_type=MESH)
                    cp.wait_send()
                    cp.wait_recv()

    in_specs, out_specs, out_shape, aliases = [], [], [], {}
    i = 0
    for n, (send, recv, blocks, zones) in zip(counts, groups):
        for a in (*blocks, *zones):
            aliases[i] = len(out_shape)
            in_specs.append(HBM)
            out_specs.append(HBM)
            out_shape.append(_hbm(a))
            i += 1
        in_specs += [SEM, SEM]
        i += 2
    outs = pl.pallas_call(
        body, name=name, in_specs=in_specs + [ANY], out_specs=tuple(out_specs), out_shape=tuple(out_shape),
        input_output_aliases=aliases, compiler_params=pltpu.CompilerParams(has_side_effects=EFFECT))(*flat, after)
    res, pos = [], 0
    for n in counts:
        res.append((list(outs[pos:pos + n]), list(outs[pos + n:pos + 2 * n])))
        pos += 2 * n
    return res


def _adamw(w, g, m, v):
    m = ADAM_B1 * m + (1.0 - ADAM_B1) * g
    v = ADAM_B2 * v + (1.0 - ADAM_B2) * (g * g)
    m_hat = m / (1.0 - ADAM_B1 ** ADAM_STEP)
    v_hat = v / (1.0 - ADAM_B2 ** ADAM_STEP)
    delta = -ADAM_LR * (m_hat / (jnp.sqrt(v_hat) + ADAM_EPS) + ADAM_WD * w)
    return delta, m, v


def _reduce_adamw(blocks, zone, mine, w, m, v, name):
    R, C = w.shape
    rt = 128 if R % 128 == 0 else 64
    assert R % rt == 0

    def body(mine_ref, own_ref, z_ref, w_ref, m_ref, v_ref, g_out, d_out, m_out, v_out):
        g = own_ref[...].astype(F32)
        for s in range(N_DEV - 1):
            g = g + z_ref[s].astype(F32)
        delta, m2, v2 = _adamw(w_ref[...], g, m_ref[...], v_ref[...])
        g_out[...] = g
        d_out[...] = delta
        m_out[...] = m2
        v_out[...] = v2

    spec = pl.BlockSpec((rt, C), lambda i, mine_ref: (i, 0))
    grid_spec = pltpu.PrefetchScalarGridSpec(
        num_scalar_prefetch=1, grid=(R // rt,),
        in_specs=[pl.BlockSpec((None, rt, C), lambda i, mine_ref: (mine_ref[0], i, 0)),
                  pl.BlockSpec((N_DEV - 1, rt, C), lambda i, mine_ref: (0, i, 0)), spec, spec, spec],
        out_specs=[spec] * 4)
    return pl.pallas_call(body, grid_spec=grid_spec, out_shape=[jax.ShapeDtypeStruct((R, C), F32)] * 4,
                          compiler_params=_params("parallel"), name=name)(mine, blocks, zone, w, m, v)


def _small_step(part, w, m, v):
    R, C = part.shape

    def body(part_ref, w_ref, m_ref, v_ref, g_out, d_out, m_out, v_out, gath, send, recv):
        me = _position()
        gath[_block_of(*me)] = part_ref[...]

        def copy(k, slot):
            return pltpu.make_async_remote_copy(
                src_ref=part_ref, dst_ref=gath.at[slot], send_sem=send.at[k - 1], recv_sem=recv.at[k - 1],
                device_id=_flip(me, k), device_id_type=MESH)

        sent = [copy(k, _block_of(*me)) for k in range(1, N_DEV)]
        for cp in sent:
            cp.start()
        for k in range(1, N_DEV):
            copy(k, _block_of(*_flip(me, k))).wait_recv()
        for cp in sent:
            cp.wait_send()
        g = gath[0]
        for s in range(1, N_DEV):
            g = g + gath[s]
        delta, m2, v2 = _adamw(w_ref[...], g, m_ref[...], v_ref[...])
        g_out[...] = g
        d_out[...] = delta
        m_out[...] = m2
        v_out[...] = v2

    vm = pl.BlockSpec(memory_space=pltpu.VMEM)
    return pl.pallas_call(
        body, in_specs=[vm] * 4, out_specs=[vm] * 4, out_shape=[jax.ShapeDtypeStruct((R, C), F32)] * 4,
        scratch_shapes=[pltpu.VMEM((N_DEV, R, C), F32), pltpu.SemaphoreType.DMA((7,)), pltpu.SemaphoreType.DMA((7,))],
        name="small_step")(part, w, m, v)


COLUMN_SHARDED = ("w_in", "w_sb_out", "w_ca_out", "w_ffn_in", "w_ple_in")
ROW_SHARDED = ("w_mix_out", "w_ffn_out", "w_ple_gate")
BIG = COLUMN_SHARDED + ROW_SHARDED
SMALL = ("g_mix", "g_ffn", "g_ple", "g_final", "rel_bias")
WEIGHTS = ("w_in", "w_sb_out", "w_ca_out", "w_mix_out", "rel_bias", "g_mix", "g_ffn", "g_ple", "g_final",
           "w_ffn_in", "w_ffn_out", "w_ple_in", "w_ple_gate")


def _pack_small(t, D):
    rows = [t[n].reshape(1, D) for n in SMALL[:4]]
    rb = t["rel_bias"].reshape(1, -1)
    rows.append(jnp.pad(rb, ((0, 0), (0, D - rb.shape[1]))))
    return jnp.concatenate(rows + [jnp.zeros((8 - len(rows), D), F32)], axis=0)


def _unpack_small(a, like):
    out = {n: a[i].reshape(like[n].shape) for i, n in enumerate(SMALL[:4])}
    out["rel_bias"] = a[4, :like["rel_bias"].size].reshape(like["rel_bias"].shape)
    return out


GATHER_GROUPS = (("w_sb_out", "w_ca_out", "w_mix_out"), ("w_ffn_in",), ("w_ffn_out", "w_ple_gate", "w_ple_in"))
FORWARD_AFTER = ("mm_in", "attention", "mm_ffn_in")
GRAD_GROUPS = (("w_ple_in", "w_ple_gate"), ("w_ffn_out",), ("w_ffn_in",), ("w_mix_out", "w_sb_out", "w_ca_out"), ("w_in",))


class _Exchange:
    def __init__(self, shards):
        me = _position()
        self.mine = _block_of(*me)
        self.shapes = {n: ((N_DEV,) + s.shape if n in COLUMN_SHARDED else (N_DEV * s.shape[0], s.shape[1])) for n, s in shards.items()}
        self.tokens = []
        w_in = _all_gather([shards["w_in"]])[0]
        self.ready = {"w_in": w_in}
        self.gathers = []
        for gi, names in enumerate(GATHER_GROUPS):
            lands = [lax.dynamic_update_slice(lax.empty((N_DEV,) + shards[n].shape, BF16), shards[n][None], (self.mine, 0, 0))
                     for n in names]
            behind = self.tokens[-1] if self.tokens else w_in
            send0, recv0, lands, token = _gather_start(lands, behind, f"gather_start_{gi}")
            self.tokens.append(token)
            self.gathers.append(dict(names=names, lands=lands, sems=(send0, recv0), token=token))
        self.grads = {}
        self.exchanges = []

    def pending(self):
        tokens, self.tokens = self.tokens, []
        return tokens

    def stage(self, tag, *made):
        gi = FORWARD_AFTER.index(tag)
        gth = self.gathers[gi]
        send1, recv1, lands, token = _gather_forward(gth["lands"], *gth["sems"], made, f"gather_forward_{gi}")
        gth.update(lands=lands, sems=gth["sems"] + (send1, recv1), token=token)
        self.tokens.append(token)

    def weight(self, name, after=None):
        if name not in self.ready:
            gi = next(i for i, names in enumerate(GATHER_GROUPS) if name in names)
            gth = self.gathers[gi]
            for n, a in zip(gth["names"], _gather_wait(gth["lands"], *gth["sems"], gth["token"] if after is None else after, f"gather_wait_{gi}")):
                self.ready[n] = a if n in COLUMN_SHARDED else a.reshape(self.shapes[n])
        return self.ready[name]

    def grad(self, name, blocks):
        self.grads[name] = blocks if name in COLUMN_SHARDED else blocks.reshape((N_DEV, -1, blocks.shape[-1]))
        names = next(names for names in GRAD_GROUPS if name in names)
        if all(n in self.grads for n in names):
            send, recv, blocks, zones, token = _exchange_start([self.grads[n] for n in names], "exchange_start_" + names[0])
            self.exchanges.append(dict(names=names, state=(send, recv, blocks, zones)))
            self.tokens.append(token)

    def collect(self, which, after, name):
        sel = [e for e in self.exchanges if GRAD_GROUPS.index(e["names"]) in which]
        out = {}
        for e, (blocks, zones) in zip(sel, _exchange_wait([e["state"] for e in sel], after, name)):
            out.update({n: (b, z) for n, b, z in zip(e["names"], blocks, zones)})
        return out


def kernel(x, p, w_in, w_sb_out, w_ca_out, w_mix_out, rel_bias, g_mix, g_ffn, g_ple, g_final, w_ffn_in, w_ffn_out, w_ple_in, w_ple_gate, loss_target, m_w_in, m_w_sb_out, m_w_ca_out, m_w_mix_out, m_rel_bias, m_g_mix, m_g_ffn, m_g_ple, m_g_final, m_w_ffn_in, m_w_ffn_out, m_w_ple_in, m_w_ple_gate, v_w_in, v_w_sb_out, v_w_ca_out, v_w_mix_out, v_rel_bias, v_g_mix, v_g_ffn, v_g_ple, v_g_final, v_w_ffn_in, v_w_ffn_out, v_w_ple_in, v_w_ple_gate):
    wts = dict(w_in=w_in, w_sb_out=w_sb_out, w_ca_out=w_ca_out, w_mix_out=w_mix_out, rel_bias=rel_bias, g_mix=g_mix, g_ffn=g_ffn,
               g_ple=g_ple, g_final=g_final, w_ffn_in=w_ffn_in, w_ffn_out=w_ffn_out, w_ple_in=w_ple_in, w_ple_gate=w_ple_gate)
    mom = dict(w_in=m_w_in, w_sb_out=m_w_sb_out, w_ca_out=m_w_ca_out, w_mix_out=m_w_mix_out, rel_bias=m_rel_bias, g_mix=m_g_mix,
               g_ffn=m_g_ffn, g_ple=m_g_ple, g_final=m_g_final, w_ffn_in=m_w_ffn_in, w_ffn_out=m_w_ffn_out, w_ple_in=m_w_ple_in,
               w_ple_gate=m_w_ple_gate)
    var = dict(w_in=v_w_in, w_sb_out=v_w_sb_out, w_ca_out=v_w_ca_out, w_mix_out=v_w_mix_out, rel_bias=v_rel_bias, g_mix=v_g_mix,
               g_ffn=v_g_ffn, g_ple=v_g_ple, g_final=v_g_final, w_ffn_in=v_w_ffn_in, w_ffn_out=v_w_ffn_out, w_ple_in=v_w_ple_in,
               w_ple_gate=v_w_ple_gate)
    T, D = x.shape[1], x.shape[2]
    shard = {n: wts[n].reshape(wts[n].shape[-2:]) for n in BIG}
    comm = _Exchange({n: _cast_bf16(shard[n], "cast_" + n) for n in BIG})
    g = dict(g_mix=g_mix.reshape(1, D), g_ffn=g_ffn.reshape(1, D), g_ple=g_ple.reshape(1, D), g_final=g_final.reshape(1, D),
             rel_bias=rel_bias.reshape(rel_bias.shape[-2:]))

    loss, grad_x, dsmall = _local_step(x.reshape(T, D), p.reshape(T, -1), loss_target.reshape(T, D), comm, g)
    loss = lax.psum(loss[0, 0], ("x", "y", "c"))

    mine = jnp.reshape(comm.mine, (1,)).astype(jnp.int32)
    grad, delta, new_m, new_v = {}, {}, {}, {}

    def update(parts):
        last = None
        for n, (blocks, zone) in parts.items():
            outs = _reduce_adamw(blocks, zone, mine, shard[n], mom[n].reshape(shard[n].shape), var[n].reshape(shard[n].shape), "adamw_" + n)
            grad[n], delta[n], new_m[n], new_v[n] = [o.reshape(wts[n].shape) for o in outs]
            last = outs[0]
        return last

    last = update(comm.collect(range(len(GRAD_GROUPS) - 1), grad_x, "exchange_wait_rest"))
    update(comm.collect([len(GRAD_GROUPS) - 1], last, "exchange_wait_w_in"))

    outs = _small_step(_pack_small(dsmall, D), _pack_small(wts, D), _pack_small(mom, D), _pack_small(var, D))
    for dst, a in zip((grad, delta, new_m, new_v), outs):
        dst.update(_unpack_small(a, wts))

    return (loss, grad_x.reshape(x.shape), *[grad[n] for n in WEIGHTS], *[delta[n] for n in WEIGHTS],
            *[new_m[n] for n in WEIGHTS], *[new_v[n] for n in WEIGHTS])
```

```python
import functools

import jax
import jax.numpy as jnp
from jax import lax
from jax.experimental import pallas as pl
from jax.experimental.pallas import tpu as pltpu

F32, BF16 = jnp.float32, jnp.bfloat16

N_DEV = 8
HEAD_DIM = 128
CHUNK = 64
LEFT_CHUNKS = 8
REL_CLIP = 128
N_REL = REL_CLIP + CHUNK
PAIR = 2 * CHUNK
PBAND = (LEFT_CHUNKS + 2) * CHUNK
PAD = LEFT_CHUNKS * CHUNK
SB_BLOCK = 256
ROWS = 256
EPS = 1e-6
NEG = -1e30
SCALE = HEAD_DIM ** -0.5
VMEM_LIMIT_BYTES = 56 * 1024 * 1024

ADAM_LR, ADAM_B1, ADAM_B2, ADAM_EPS, ADAM_WD, ADAM_STEP = 0.001, 0.9, 0.999, 1e-08, 0.01, 10

ANY = pl.BlockSpec(memory_space=pl.ANY)
NN = (((1,), (0,)), ((), ()))
NT = (((1,), (1,)), ((), ()))
TN = (((0,), (0,)), ((), ()))
MESH = pl.DeviceIdType.MESH


def _params(*sem):
    return pltpu.CompilerParams(dimension_semantics=sem or None, vmem_limit_bytes=VMEM_LIMIT_BYTES)


def _dot(a, b, dims=NN):
    return lax.dot_general(a, b, dims, preferred_element_type=F32)


def _mm(a, b, *, mode, tm, tn, tk, out_dtype, name, b_blocked=False, out_block=None, res=None, after=()):
    if mode == "nn":
        M, K = a.shape
        a_spec = pl.BlockSpec((tm, tk), lambda i, j, k: (i, k))
        if b_blocked:
            G, _, nb = b.shape
            N, per = G * nb, nb // tn
            assert nb % tn == 0
            b_spec = pl.BlockSpec((None, tk, tn), lambda i, j, k: (j // per, k, j % per))
        else:
            N = b.shape[1]
            b_spec = pl.BlockSpec((tk, tn), lambda i, j, k: (k, j))
        dims = NN
    elif mode == "nt":
        M, K = a.shape
        a_spec = pl.BlockSpec((tm, tk), lambda i, j, k: (i, k))
        if b_blocked:
            G, N, nb = b.shape
            per = nb // tk
            assert K == G * nb and nb % tk == 0
            b_spec = pl.BlockSpec((None, tn, tk), lambda i, j, k: (k // per, j, k % per))
        else:
            N = b.shape[0]
            b_spec = pl.BlockSpec((tn, tk), lambda i, j, k: (j, k))
        dims = NT
    else:
        K, M = a.shape
        N = b.shape[1]
        a_spec = pl.BlockSpec((tk, tm), lambda i, j, k: (k, i))
        b_spec = pl.BlockSpec((tk, tn), lambda i, j, k: (k, j))
        dims = TN
    assert M % tm == 0 and N % tn == 0 and K % tk == 0, (name, M, N, K, tm, tn, tk)
    nk = K // tk
    if out_block is None:
        out_shape = jax.ShapeDtypeStruct((M, N), out_dtype)
        o_spec = pl.BlockSpec((tm, tn), lambda i, j, k: (i, j))
    else:
        per_o = out_block // tn
        assert out_block % tn == 0
        out_shape = jax.ShapeDtypeStruct((N // out_block, M, out_block), out_dtype)
        o_spec = pl.BlockSpec((None, tm, tn), lambda i, j, k: (j // per_o, i, j % per_o))
    in_specs = [a_spec, b_spec]
    args = [a, b]
    if res is not None:
        in_specs.append(pl.BlockSpec((tm, tn), lambda i, j, k: (i, j)))
        args.append(res)
    n_in = len(args) + len(after)

    def body(*refs):
        a_ref, b_ref = refs[0], refs[1]
        r_ref = refs[2] if res is not None else None
        o_ref = refs[n_in]

        def finish(acc):
            if r_ref is not None:
                acc = acc + r_ref[...]
            o_ref[...] = acc.astype(o_ref.dtype)

        if nk == 1:
            finish(_dot(a_ref[...], b_ref[...], dims))
        else:
            acc_ref = refs[-1]
            k = pl.program_id(2)

            @pl.when(k == 0)
            def _():
                acc_ref[...] = jnp.zeros_like(acc_ref)

            acc_ref[...] += _dot(a_ref[...], b_ref[...], dims)

            @pl.when(k == nk - 1)
            def _():
                finish(acc_ref[...])

    return pl.pallas_call(
        body, grid=(M // tm, N // tn, nk), in_specs=in_specs + [ANY] * len(after), out_specs=o_spec, out_shape=out_shape,
        scratch_shapes=[] if nk == 1 else [pltpu.VMEM((tm, tn), F32)],
        compiler_params=_params("parallel", "parallel", "arbitrary"), name=name)(*args, *after)


def _row_spec(d, col=0):
    return pl.BlockSpec((ROWS, d), lambda i: (i, col))


def _vec_spec(d):
    return pl.BlockSpec((1, d), lambda i: (0, 0))


def _rms(x):
    return lax.rsqrt(jnp.mean(x * x, axis=-1, keepdims=True) + EPS)


def _norm_fwd(x, g, name):
    T, D = x.shape

    def body(x_ref, g_ref, h_ref):
        xv = x_ref[...]
        h_ref[...] = (xv * _rms(xv) * g_ref[...]).astype(BF16)

    return pl.pallas_call(body, grid=(T // ROWS,), in_specs=[_row_spec(D), _vec_spec(D)], out_specs=_row_spec(D),
                          out_shape=jax.ShapeDtypeStruct((T, D), BF16), compiler_params=_params("parallel"), name=name)(x, g)


def _norm_bwd_math(dh, xv, gv):
    r = _rms(xv)
    xhat = xv * r
    dxhat = dh * gv
    dx = r * (dxhat - xhat * jnp.mean(dxhat * xhat, axis=-1, keepdims=True))
    dg = jnp.sum(dh * xhat, axis=0, keepdims=True)
    return dx, dg


def _norm_bwd(dh, x, g, dres, name):
    T, D = x.shape

    def body(dh_ref, x_ref, g_ref, dres_ref, dx_ref, dxb_ref, dg_ref):
        dx, dg = _norm_bwd_math(dh_ref[...], x_ref[...], g_ref[...])
        dx = dx + dres_ref[...]
        dx_ref[...] = dx
        dxb_ref[...] = dx.astype(BF16)

        @pl.when(pl.program_id(0) == 0)
        def _():
            dg_ref[...] = jnp.zeros_like(dg_ref)

        dg_ref[...] += dg

    return pl.pallas_call(
        body, grid=(T // ROWS,), in_specs=[_row_spec(D), _row_spec(D), _vec_spec(D), _row_spec(D)],
        out_specs=[_row_spec(D), _row_spec(D), _vec_spec(D)],
        out_shape=[jax.ShapeDtypeStruct((T, D), F32), jax.ShapeDtypeStruct((T, D), BF16), jax.ShapeDtypeStruct((1, D), F32)],
        compiler_params=_params("arbitrary"), name=name)(dh, x, g, dres)


def _merge_fwd(proj, a_sb, a_ca, D, gate_col):
    T = proj.shape[0]

    def body(gs_ref, gc_ref, a_ref, b_ref, o_ref):
        o_ref[...] = (jax.nn.sigmoid(gs_ref[...]) * a_ref[...] + jax.nn.sigmoid(gc_ref[...]) * b_ref[...]).astype(BF16)

    return pl.pallas_call(
        body, grid=(T // ROWS,), in_specs=[_row_spec(D, gate_col), _row_spec(D, gate_col + 1), _row_spec(D), _row_spec(D)],
        out_specs=_row_spec(D), out_shape=jax.ShapeDtypeStruct((T, D), BF16),
        compiler_params=_params("parallel"), name="merge_fwd")(proj, proj, a_sb, a_ca)


def _merge_bwd(dm, proj, a_sb, a_ca, D, gate_col):
    T = proj.shape[0]

    def body(dm_ref, gs_ref, gc_ref, a_ref, b_ref, da_ref, db_ref, dgs_ref, dgc_ref):
        dmv = dm_ref[...]
        ss, sc = jax.nn.sigmoid(gs_ref[...]), jax.nn.sigmoid(gc_ref[...])
        da_ref[...] = (dmv * ss).astype(BF16)
        db_ref[...] = (dmv * sc).astype(BF16)
        dgs_ref[...] = (dmv * a_ref[...] * ss * (1.0 - ss)).astype(BF16)
        dgc_ref[...] = (dmv * b_ref[...] * sc * (1.0 - sc)).astype(BF16)

    return pl.pallas_call(
        body, grid=(T // ROWS,),
        in_specs=[_row_spec(D), _row_spec(D, gate_col), _row_spec(D, gate_col + 1), _row_spec(D), _row_spec(D)],
        out_specs=[_row_spec(D)] * 4, out_shape=[jax.ShapeDtypeStruct((T, D), BF16)] * 4,
        compiler_params=_params("parallel"), name="merge_bwd")(dm, proj, proj, a_sb, a_ca)


def _swiglu_fwd(gu, after=()):
    T, F2 = gu.shape
    F = F2 // 2

    def body(g_ref, u_ref, *rest):
        gv = g_ref[...]
        rest[-1][...] = (gv * jax.nn.sigmoid(gv) * u_ref[...]).astype(BF16)

    return pl.pallas_call(body, grid=(T // ROWS,), in_specs=[_row_spec(F, 0), _row_spec(F, 1)] + [ANY] * len(after), out_specs=_row_spec(F),
                          out_shape=jax.ShapeDtypeStruct((T, F), BF16), compiler_params=_params("parallel"), name="swiglu_fwd")(gu, gu, *after)


def _swiglu_bwd(dact, gu):
    T, F2 = gu.shape
    F = F2 // 2

    def body(d_ref, g_ref, u_ref, o_ref):
        dv, gv, uv = d_ref[...], g_ref[...], u_ref[...]
        s = jax.nn.sigmoid(gv)
        o_ref[:, 0:F] = (dv * uv * s * (1.0 + gv * (1.0 - s))).astype(BF16)
        o_ref[:, F:F2] = (dv * gv * s).astype(BF16)

    return pl.pallas_call(body, grid=(T // ROWS,), in_specs=[_row_spec(F), _row_spec(F, 0), _row_spec(F, 1)], out_specs=_row_spec(F2),
                          out_shape=jax.ShapeDtypeStruct((T, F2), BF16), compiler_params=_params("parallel"), name="swiglu_bwd")(dact, gu, gu)


def _tail(x3, zg, pe, g_final, target):
    T, D = x3.shape

    def body(x3_ref, zg_ref, pe_ref, g_ref, t_ref, loss_ref, dx_ref, dpe_ref, dzg_ref, dg_ref):
        gate = jax.nn.sigmoid(zg_ref[...])
        pev = pe_ref[...]
        x4 = x3_ref[...] + gate * pev
        gv = g_ref[...]
        xhat = x4 * _rms(x4)
        err = xhat * gv - t_ref[...]
        part = 0.5 * jnp.sum(jnp.mean(err * err, axis=-1, keepdims=True), axis=0, keepdims=True)
        dx, dg = _norm_bwd_math(err * (1.0 / D), x4, gv)
        dx_ref[...] = dx
        dpe_ref[...] = (dx * gate).astype(BF16)
        dzg_ref[...] = (dx * pev * gate * (1.0 - gate)).astype(BF16)

        @pl.when(pl.program_id(0) == 0)
        def _():
            dg_ref[...] = jnp.zeros_like(dg_ref)
            loss_ref[...] = jnp.zeros_like(loss_ref)

        dg_ref[...] += dg
        loss_ref[...] += jnp.broadcast_to(part, loss_ref.shape)

    return pl.pallas_call(
        body, grid=(T // ROWS,), in_specs=[_row_spec(D), _row_spec(D), _row_spec(D), _vec_spec(D), _row_spec(D)],
        out_specs=[_vec_spec(128), _row_spec(D), _row_spec(D), _row_spec(D), _vec_spec(D)],
        out_shape=[jax.ShapeDtypeStruct((1, 128), F32), jax.ShapeDtypeStruct((T, D), F32), jax.ShapeDtypeStruct((T, D), BF16),
                   jax.ShapeDtypeStruct((T, D), BF16), jax.ShapeDtypeStruct((1, D), F32)],
        compiler_params=_params("arbitrary"), name="tail")(x3, zg, pe, g_final, target)


def _cast_bf16(x, name):
    R, C = x.shape
    rows = next(r for r in (ROWS, 128, 64, 32, 16) if R % r == 0)

    def body(x_ref, o_ref):
        o_ref[...] = x_ref[...].astype(BF16)

    spec = pl.BlockSpec((rows, C), lambda i: (i, 0))
    return pl.pallas_call(body, grid=(R // rows,), in_specs=[spec], out_specs=spec, out_shape=jax.ShapeDtypeStruct((R, C), BF16),
                          compiler_params=_params("parallel"), name=name)(x)


def _head_spec(T, col0, heads=1):
    return pl.BlockSpec((T, heads * HEAD_DIM), lambda h, *_: (0, col0 + h))


SB_HEADS = 2


def _triangle(n, right):
    j = lax.broadcasted_iota(jnp.int32, (n, n), 0)
    s = lax.broadcasted_iota(jnp.int32, (n, n), 1)
    return jnp.where((j > s) if right else (j < s), 1.0, 0.0).astype(BF16)


def _lane_scan(x, tri):
    hi = x.astype(BF16)
    lo = (x - hi.astype(F32)).astype(BF16)
    return _dot(hi, tri) + _dot(lo, tri)


def _head_cols(ref, rows, hh):
    return ref[rows, hh * HEAD_DIM:(hh + 1) * HEAD_DIM]


def _sb_tile(qv, kk, past, c_lk, tri):
    z = _dot(qv, kk, NT) * SCALE
    sp = jnp.log(1.0 + jnp.exp(-jnp.abs(z)))
    ls_pos = jnp.minimum(z, 0.0) - sp
    lk = jnp.where(past, jnp.minimum(-z, 0.0) - sp, 0.0)
    a = jnp.where(past, jnp.exp(ls_pos + c_lk + _lane_scan(lk, tri)), 0.0)
    return ls_pos, lk, a


def _sb_past(qb, kb):
    B = SB_BLOCK
    t_idx = qb * B + lax.broadcasted_iota(jnp.int32, (B, B), 0)
    s_idx = kb * B + lax.broadcasted_iota(jnp.int32, (B, B), 1)
    return s_idx < t_idx


def _sb_fwd(proj, n_heads, after=()):
    T = proj.shape[0]
    B, HP = SB_BLOCK, SB_HEADS
    assert n_heads % HP == 0

    def body(q_ref, k_ref, v_ref, *rest):
        y_ref = rest[-1]
        qb = pl.program_id(1)
        tri = _triangle(B, right=True)
        qv = [_head_cols(q_ref, slice(None), hh).astype(BF16) for hh in range(HP)]

        def tile(i, carry):
            kb = qb - i
            rows = pl.ds(pl.multiple_of(kb * B, B), B)
            past = _sb_past(qb, kb)
            out = []
            for hh in range(HP):
                acc, c_lk = carry[hh]
                kk = _head_cols(k_ref, rows, hh).astype(BF16)
                vv = _head_cols(v_ref, rows, hh).astype(BF16)
                _, lk, a = _sb_tile(qv[hh], kk, past, c_lk, tri)
                out.append((acc + _dot(a.astype(BF16), vv), c_lk + jnp.sum(lk, axis=1, keepdims=True)))
            return tuple(out)

        init = tuple((jnp.zeros((B, HEAD_DIM), F32), jnp.zeros((B, 1), F32)) for _ in range(HP))
        res = lax.fori_loop(0, qb + 1, tile, init)
        for hh in range(HP):
            y_ref[:, hh * HEAD_DIM:(hh + 1) * HEAD_DIM] = res[hh][0].astype(BF16)

    blk = pl.BlockSpec((B, HP * HEAD_DIM), lambda h, i: (i, h))
    G = n_heads // HP
    return pl.pallas_call(
        body, grid=(G, T // B),
        in_specs=[blk, _head_spec(T, G, HP), _head_spec(T, 2 * G, HP)] + [ANY] * len(after), out_specs=blk,
        out_shape=jax.ShapeDtypeStruct((T, n_heads * HEAD_DIM), BF16),
        compiler_params=_params("parallel", "arbitrary"), name="sb_fwd")(proj, proj, proj, *after)


def _sb_bwd(proj, dy, n_heads):
    T = proj.shape[0]
    B, HP = SB_BLOCK, SB_HEADS
    nq = T // B

    def body(q_ref, k_ref, v_ref, dy_ref, dq_ref, dk_ref, dv_ref, g_s, sig_s, dk_s, dv_s):
        qb = pl.program_id(1)

        @pl.when(qb == 0)
        def _():
            dk_s[...] = jnp.zeros_like(dk_s)
            dv_s[...] = jnp.zeros_like(dv_s)

        tri_r = _triangle(B, right=True)
        tri_l = _triangle(B, right=False)
        qv = [_head_cols(q_ref, slice(None), hh).astype(BF16) for hh in range(HP)]
        dyb = [_head_cols(dy_ref, slice(None), hh).astype(BF16) for hh in range(HP)]

        def sweep(i, carry):
            kb = qb - i
            rows = pl.ds(pl.multiple_of(kb * B, B), B)
            past = _sb_past(qb, kb)
            out = []
            for hh in range(HP):
                kk = _head_cols(k_ref, rows, hh).astype(BF16)
                vv = _head_cols(v_ref, rows, hh).astype(BF16)
                ls_pos, lk, a = _sb_tile(qv[hh], kk, past, carry[hh], tri_r)
                g_s[hh, kb] = _dot(dyb[hh], vv, NT) * a
                sig_s[hh, kb] = jnp.exp(ls_pos)
                dv_s[rows, hh * HEAD_DIM:(hh + 1) * HEAD_DIM] += _dot(a.astype(BF16), dyb[hh], TN)
                out.append(carry[hh] + jnp.sum(lk, axis=1, keepdims=True))
            return tuple(out)

        lax.fori_loop(0, qb + 1, sweep, tuple(jnp.zeros((B, 1), F32) for _ in range(HP)))

        def back(kb, carry):
            rows = pl.ds(pl.multiple_of(kb * B, B), B)
            past = _sb_past(qb, kb)
            out = []
            for hh in range(HP):
                dq, c_g = carry[hh]
                kk = _head_cols(k_ref, rows, hh).astype(BF16)
                g, sig = g_s[hh, kb], sig_s[hh, kb]
                before = c_g + _lane_scan(g, tri_l)
                dz = (jnp.where(past, g * (1.0 - sig) - before * sig, 0.0) * SCALE).astype(BF16)
                dk_s[rows, hh * HEAD_DIM:(hh + 1) * HEAD_DIM] += _dot(dz, qv[hh], TN)
                out.append((dq + _dot(dz, kk), c_g + jnp.sum(g, axis=1, keepdims=True)))
            return tuple(out)

        init = tuple((jnp.zeros((B, HEAD_DIM), F32), jnp.zeros((B, 1), F32)) for _ in range(HP))
        res = lax.fori_loop(0, qb + 1, back, init)
        for hh in range(HP):
            dq_ref[:, hh * HEAD_DIM:(hh + 1) * HEAD_DIM] = res[hh][0].astype(BF16)

        @pl.when(qb == nq - 1)
        def _():
            dk_ref[...] = dk_s[...].astype(BF16)
            dv_ref[...] = dv_s[...].astype(BF16)

    blk = pl.BlockSpec((B, HP * HEAD_DIM), lambda h, i: (i, h))
    G = n_heads // HP
    full = _head_spec(T, 0, HP)
    shp = jax.ShapeDtypeStruct((T, n_heads * HEAD_DIM), BF16)
    return pl.pallas_call(
        body, grid=(G, nq),
        in_specs=[blk, _head_spec(T, G, HP), _head_spec(T, 2 * G, HP), blk], out_specs=[blk, full, full],
        out_shape=[shp, shp, shp],
        scratch_shapes=[pltpu.VMEM((HP, nq, B, B), F32)] * 2 + [pltpu.VMEM((T, HP * HEAD_DIM), F32)] * 2,
        compiler_params=_params("parallel", "arbitrary"), name="sb_bwd")(proj, proj, proj, dy)


DIAGS = PBAND + PAIR


def _diag_onehot():
    d = lax.broadcasted_iota(jnp.int32, (DIAGS, 2 * PAIR), 0)
    r = lax.broadcasted_iota(jnp.int32, (DIAGS, 2 * PAIR), 1)
    return jnp.where(jnp.clip(d - PAIR - PAD, -REL_CLIP, CHUNK - 1) + REL_CLIP == r, 1.0, 0.0)


def _bias_expand(rel_bias):
    H = rel_bias.shape[0]
    table = jnp.pad(rel_bias, ((0, 0), (0, 2 * PAIR - N_REL)))

    def body(rb_ref, o_ref):
        o_ref[...] = lax.dot_general(rb_ref[...], _diag_onehot(), NT, precision=lax.Precision.HIGHEST, preferred_element_type=F32)

    per_diag = pl.pallas_call(body, out_shape=jax.ShapeDtypeStruct((H, DIAGS), F32), name="bias_expand")(table)
    flat = jnp.tile(jnp.pad(per_diag, ((0, 0), (0, 1))), (1, PAIR))[:, :PAIR * DIAGS]
    return flat.reshape(H, PAIR, DIAGS)[:, :, PAIR:]


def _bias_reduce(dbias):
    H = dbias.shape[0]
    padded = jnp.pad(dbias, ((0, 0), (0, 1), (PAIR, 0))).reshape(H, -1)
    skewed = padded[:, :PAIR * (DIAGS + 1)].reshape(H, PAIR, DIAGS + 1)[:, :, :DIAGS]

    def body(s_ref, o_ref):
        per_diag = jnp.sum(s_ref[...], axis=0, keepdims=True)
        o_ref[...] = lax.dot_general(jnp.broadcast_to(per_diag, (8, DIAGS)), _diag_onehot(), NN, precision=lax.Precision.HIGHEST,
                                     preferred_element_type=F32)[0:1]

    return pl.pallas_call(
        body, grid=(H,), in_specs=[pl.BlockSpec((None, PAIR, DIAGS), lambda h: (h, 0, 0))],
        out_specs=pl.BlockSpec((None, 1, 2 * PAIR), lambda h: (h, 0, 0)),
        out_shape=jax.ShapeDtypeStruct((H, 1, 2 * PAIR), F32), compiler_params=_params("parallel"), name="bias_reduce")(skewed)[:, 0]


def _ca_pair(pr, q_ref, kpad, vpad, bias):
    r0 = pl.multiple_of(pr * PAIR, PAIR)
    qp = q_ref[pl.ds(r0, PAIR), :].astype(BF16)
    kb = kpad[pl.ds(r0, PBAND), :]
    vb = vpad[pl.ds(r0, PBAND), :]
    i = lax.broadcasted_iota(jnp.int32, (PAIR, PBAND), 0)
    j = lax.broadcasted_iota(jnp.int32, (PAIR, PBAND), 1)
    qc, kc = i // CHUNK, j // CHUNK
    valid = (kc >= qc) & (kc <= qc + LEFT_CHUNKS) & (pr * PAIR + j >= PAD)
    z = jnp.where(valid, _dot(qp, kb, NT) * SCALE + bias, NEG)
    e = jnp.exp(z - jnp.max(z, axis=1, keepdims=True))
    return qp, kb, vb, e / jnp.sum(e, axis=1, keepdims=True)


def _ca_fill(k_ref, v_ref, kpad, vpad):
    T = k_ref.shape[0]
    kpad[0:PAD, :] = jnp.zeros((PAD, HEAD_DIM), BF16)
    vpad[0:PAD, :] = jnp.zeros((PAD, HEAD_DIM), BF16)
    kpad[PAD:PAD + T, :] = k_ref[...].astype(BF16)
    vpad[PAD:PAD + T, :] = v_ref[...].astype(BF16)


def _ca_fwd(proj, bias, n_heads, col0):
    T = proj.shape[0]

    def body(q_ref, k_ref, v_ref, b_ref, y_ref, kpad, vpad):
        _ca_fill(k_ref, v_ref, kpad, vpad)
        bias_v = b_ref[...]

        def pair(pr, _):
            _, _, vb, w = _ca_pair(pr, q_ref, kpad, vpad, bias_v)
            y_ref[pl.ds(pl.multiple_of(pr * PAIR, PAIR), PAIR), :] = _dot(w.astype(BF16), vb).astype(BF16)
            return 0

        lax.fori_loop(0, T // PAIR, pair, 0)

    return pl.pallas_call(
        body, grid=(n_heads,),
        in_specs=[_head_spec(T, col0), _head_spec(T, col0 + n_heads), _head_spec(T, col0 + 2 * n_heads),
                  pl.BlockSpec((None, PAIR, PBAND), lambda h: (h, 0, 0))],
        out_specs=_head_spec(T, 0), out_shape=jax.ShapeDtypeStruct((T, n_heads * HEAD_DIM), BF16),
        scratch_shapes=[pltpu.VMEM((PAD + T, HEAD_DIM), BF16)] * 2,
        compiler_params=_params("parallel"), name="ca_fwd")(proj, proj, proj, bias)


def _ca_bwd(proj, bias, dy, n_heads, col0):
    T = proj.shape[0]

    def body(q_ref, k_ref, v_ref, b_ref, dy_ref, dq_ref, dk_ref, dv_ref, db_ref, kpad, vpad, dkpad, dvpad):
        _ca_fill(k_ref, v_ref, kpad, vpad)
        dkpad[...] = jnp.zeros_like(dkpad)
        dvpad[...] = jnp.zeros_like(dvpad)
        db_ref[...] = jnp.zeros_like(db_ref)
        bias_v = b_ref[...]

        def pair(pr, _):
            r0 = pl.multiple_of(pr * PAIR, PAIR)
            qp, kb, vb, w = _ca_pair(pr, q_ref, kpad, vpad, bias_v)
            dyp = dy_ref[pl.ds(r0, PAIR), :].astype(BF16)
            dw = _dot(dyp, vb, NT)
            dz = w * (dw - jnp.sum(dw * w, axis=1, keepdims=True))
            db_ref[...] += dz
            dzs = (dz * SCALE).astype(BF16)
            dq_ref[pl.ds(r0, PAIR), :] = _dot(dzs, kb).astype(BF16)
            dkpad[pl.ds(r0, PBAND), :] += _dot(dzs, qp, TN)
            dvpad[pl.ds(r0, PBAND), :] += _dot(w.astype(BF16), dyp, TN)
            return 0

        lax.fori_loop(0, T // PAIR, pair, 0)
        dk_ref[...] = dkpad[PAD:PAD + T, :].astype(BF16)
        dv_ref[...] = dvpad[PAD:PAD + T, :].astype(BF16)

    full = _head_spec(T, 0)
    bspec = pl.BlockSpec((None, PAIR, PBAND), lambda h: (h, 0, 0))
    shp = jax.ShapeDtypeStruct((T, n_heads * HEAD_DIM), BF16)
    return pl.pallas_call(
        body, grid=(n_heads,),
        in_specs=[_head_spec(T, col0), _head_spec(T, col0 + n_heads), _head_spec(T, col0 + 2 * n_heads), bspec, full],
        out_specs=[full, full, full, bspec],
        out_shape=[shp, shp, shp, jax.ShapeDtypeStruct((n_heads, PAIR, PBAND), F32)],
        scratch_shapes=[pltpu.VMEM((PAD + T, HEAD_DIM), BF16)] * 2 + [pltpu.VMEM((PAD + T, HEAD_DIM), F32)] * 2,
        compiler_params=_params("parallel"), name="ca_bwd")(proj, proj, proj, bias, dy)


def _local_step(x, p, target, comm, g):
    T, D = x.shape
    H = g["rel_bias"].shape[0]
    W = H * HEAD_DIM
    nb_in = comm.shapes["w_in"][2]
    nb_ff = comm.shapes["w_ffn_in"][2]
    nb_o = comm.shapes["w_sb_out"][2]
    nb_p = comm.shapes["w_ple_in"][2]
    tm = min(T, 1024)
    tn = min(D, 1024)
    gate_col = 6 * W // D

    h1 = _norm_fwd(x, g["g_mix"], "norm1")
    proj = _mm(h1, comm.weight("w_in"), mode="nn", tm=tm, tn=nb_in, tk=D, out_dtype=F32, b_blocked=True, after=comm.pending(), name="mm_in")
    comm.stage("mm_in", proj)
    y_sb = _sb_fwd(proj, H, comm.pending())
    bias = _bias_expand(g["rel_bias"])
    y_ca = _ca_fwd(proj, bias, H, 3 * H)
    comm.stage("attention", y_sb, y_ca)
    a_sb = _mm(y_sb, comm.weight("w_sb_out", y_ca), mode="nn", tm=tm, tn=nb_o, tk=W, out_dtype=F32, b_blocked=True, after=comm.pending(), name="mm_sb_out")
    a_ca = _mm(y_ca, comm.weight("w_ca_out"), mode="nn", tm=tm, tn=nb_o, tk=W, out_dtype=F32, b_blocked=True, name="mm_ca_out")
    merged = _merge_fwd(proj, a_sb, a_ca, D, gate_col)
    x2 = _mm(merged, comm.weight("w_mix_out"), mode="nn", tm=tm, tn=tn, tk=D, out_dtype=F32, res=x, name="mm_mix")
    h2 = _norm_fwd(x2, g["g_ffn"], "norm2")
    gu = _mm(h2, comm.weight("w_ffn_in", h2), mode="nn", tm=tm, tn=nb_ff, tk=D, out_dtype=F32, b_blocked=True, name="mm_ffn_in")
    comm.stage("mm_ffn_in", gu)
    act = _swiglu_fwd(gu, comm.pending())
    F = act.shape[1]
    tkf = F // 2 if F % 256 == 0 else F
    x3 = _mm(act, comm.weight("w_ffn_out", act), mode="nn", tm=tm, tn=tn, tk=tkf, out_dtype=F32, res=x2, name="mm_ffn_out")
    h3 = _norm_fwd(x3, g["g_ple"], "norm3")
    zg = _mm(h3, comm.weight("w_ple_gate"), mode="nn", tm=tm, tn=tn, tk=D, out_dtype=F32, name="mm_ple_gate")
    pb = _cast_bf16(p, "cast_p")
    P = p.shape[1]
    pe = _mm(pb, comm.weight("w_ple_in"), mode="nn", tm=tm, tn=nb_p, tk=P, out_dtype=F32, b_blocked=True, name="mm_ple_in")
    loss, dx4, dpe, dzg, dg_final = _tail(x3, zg, pe, g["g_final"], target)

    tw = min(D, 512)
    DW = BF16
    comm.grad("w_ple_in", _mm(pb, dpe, mode="tn", tm=P, tn=nb_p, tk=T, out_dtype=DW, out_block=nb_p, name="mm_d_ple_in"))
    comm.grad("w_ple_gate", _mm(h3, dzg, mode="tn", tm=tw, tn=tn, tk=T, out_dtype=DW, name="mm_d_ple_gate"))
    dh3 = _mm(dzg, comm.weight("w_ple_gate"), mode="nt", tm=tm, tn=tn, tk=D, out_dtype=F32, after=comm.pending(), name="mm_dh3")
    dx3, dx3b, dg_ple = _norm_bwd(dh3, x3, g["g_ple"], dx4, "norm3_bwd")
    comm.grad("w_ffn_out", _mm(act, dx3b, mode="tn", tm=F // 4, tn=tn, tk=T, out_dtype=DW, name="mm_d_ffn_out"))
    dact = _mm(dx3b, comm.weight("w_ffn_out"), mode="nt", tm=tm, tn=tkf, tk=D, out_dtype=F32, after=comm.pending(), name="mm_dact")
    dgu = _swiglu_bwd(dact, gu)
    comm.grad("w_ffn_in", _mm(h2, dgu, mode="tn", tm=tw, tn=nb_ff, tk=T, out_dtype=DW, out_block=nb_ff, name="mm_d_ffn_in"))
    dh2 = _mm(dgu, comm.weight("w_ffn_in"), mode="nt", tm=tm, tn=tn, tk=nb_ff, out_dtype=F32, b_blocked=True, after=comm.pending(), name="mm_dh2")
    dx2, dx2b, dg_ffn = _norm_bwd(dh2, x2, g["g_ffn"], dx3, "norm2_bwd")
    dmerged = _mm(dx2b, comm.weight("w_mix_out"), mode="nt", tm=tm, tn=tn, tk=D, out_dtype=F32, name="mm_dmerged")
    da_sb, da_ca, dgate_sb, dgate_ca = _merge_bwd(dmerged, proj, a_sb, a_ca, D, gate_col)
    comm.grad("w_mix_out", _mm(merged, dx2b, mode="tn", tm=tw, tn=tn, tk=T, out_dtype=DW, name="mm_d_mix"))
    comm.grad("w_sb_out", _mm(y_sb, da_sb, mode="tn", tm=min(W, 512), tn=nb_o, tk=T, out_dtype=DW, out_block=nb_o, name="mm_d_sb_out"))
    comm.grad("w_ca_out", _mm(y_ca, da_ca, mode="tn", tm=min(W, 512), tn=nb_o, tk=T, out_dtype=DW, out_block=nb_o, name="mm_d_ca_out"))
    dy_sb = _mm(da_sb, comm.weight("w_sb_out"), mode="nt", tm=tm, tn=W, tk=nb_o, out_dtype=F32, b_blocked=True, after=comm.pending(), name="mm_dy_sb")
    dy_ca = _mm(da_ca, comm.weight("w_ca_out"), mode="nt", tm=tm, tn=W, tk=nb_o, out_dtype=F32, b_blocked=True, name="mm_dy_ca")
    dq_sb, dk_sb, dv_sb = _sb_bwd(proj, dy_sb, H)
    dq_ca, dk_ca, dv_ca, dbias = _ca_bwd(proj, bias, dy_ca, H, 3 * H)
    d_rel = _bias_reduce(dbias)[:, :N_REL]
    dproj = jnp.concatenate([dq_sb, dk_sb, dv_sb, dq_ca, dk_ca, dv_ca, dgate_sb, dgate_ca], axis=1)
    comm.grad("w_in", _mm(h1, dproj, mode="tn", tm=tw, tn=nb_in, tk=T, out_dtype=DW, out_block=nb_in, name="mm_d_in"))
    dh1 = _mm(dproj, comm.weight("w_in"), mode="nt", tm=tm, tn=tn, tk=nb_in, out_dtype=F32, b_blocked=True, after=comm.pending(), name="mm_dh1")
    grad_x, _, dg_mix = _norm_bwd(dh1, x, g["g_mix"], dx2, "norm1_bwd")
    small = dict(g_mix=dg_mix, g_ffn=dg_ffn, g_ple=dg_ple, g_final=dg_final, rel_bias=d_rel)
    return loss, grad_x, small


def _position():
    x, y, c = lax.axis_index("x"), lax.axis_index("y"), lax.axis_index("c")
    return x, y, c


def _block_of(px, py, pc):
    return 4 * px + 2 * py + pc


def _flip(pos, k):
    x, y, c = pos
    return (1 - x if k & 4 else x, 1 - y if k & 2 else y, 1 - c if k & 1 else c)


def _all_gather(shards):
    n = len(shards)

    def body(*refs):
        ins, outs = refs[:n], refs[n:2 * n]
        send, recv, lsem = refs[2 * n:]
        x, y, c = _position()
        me, sibling = (x, y, c), (x, y, 1 - c)
        chips = [(1 - x, y), (x, 1 - y), (1 - x, 1 - y)]

        def copy(wi, k, block, to, src=None):
            rows = outs[wi].at[_block_of(*block)]
            return pltpu.make_async_remote_copy(
                src_ref=rows if src is None else src, dst_ref=rows, send_sem=send.at[wi * 7 + k], recv_sem=recv.at[wi * 7 + k],
                device_id=to, device_id_type=MESH)

        local = [pltpu.make_async_copy(ins[wi], outs[wi].at[_block_of(*me)], lsem.at[wi]) for wi in range(n)]
        for cp in local:
            cp.start()
        first = []
        for wi in range(n):
            first += [copy(wi, 1 + j, me, (*chip, c), src=ins[wi]) for j, chip in enumerate(chips)]
            first.append(copy(wi, 0, me, sibling, src=ins[wi]))
        for cp in first:
            cp.start()
        passed = []
        for wi in range(n):
            for j, chip in enumerate(chips):
                copy(wi, 1 + j, (*chip, c), me).wait_recv()
                fwd = copy(wi, 4 + j, (*chip, c), sibling)
                fwd.start()
                passed.append(fwd)
        for wi in range(n):
            copy(wi, 0, sibling, me).wait_recv()
            for j, chip in enumerate(chips):
                copy(wi, 4 + j, (*chip, 1 - c), me).wait_recv()
        for cp in first + passed:
            cp.wait_send()
        for cp in local:
            cp.wait()

    return pl.pallas_call(
        body, in_specs=[ANY] * n, out_specs=[ANY] * n,
        out_shape=[jax.ShapeDtypeStruct((N_DEV,) + s.shape, s.dtype) for s in shards],
        scratch_shapes=[pltpu.SemaphoreType.DMA((7 * n,)), pltpu.SemaphoreType.DMA((7 * n,)), pltpu.SemaphoreType.DMA((n,))],
        name="all_gather")(*shards)


HBM = pl.BlockSpec(memory_space=pltpu.HBM)
SEM = pl.BlockSpec(memory_space=pltpu.SEMAPHORE)
VMEM_SPEC = pl.BlockSpec(memory_space=pltpu.VMEM)
EFFECT = pltpu.SideEffectType.DATAFLOW_SIDE_EFFECTING
TOKEN = jax.ShapeDtypeStruct((8, 128), F32)


def _hbm(a):
    return pltpu.HBM(a.shape, a.dtype)


def _landing(shape, dtype):
    return pltpu.with_memory_space_constraint(lax.empty(shape, dtype), pltpu.HBM)


def _gather_start(lands, after, name):
    n = len(lands)

    def body(*refs):
        ins = refs[:n]
        send, recv = refs[n + 1], refs[n + 2]
        token = refs[-1]
        x, y, c = _position()
        mine = _block_of(x, y, c)
        peers = [(x, y, 1 - c), (1 - x, y, c), (x, 1 - y, c), (1 - x, 1 - y, c)]
        for wi in range(n):
            for k, peer in enumerate(peers):
                pltpu.make_async_remote_copy(
                    src_ref=ins[wi].at[mine], dst_ref=ins[wi].at[mine], send_sem=send.at[4 * wi + k], recv_sem=recv.at[4 * wi + k],
                    device_id=peer, device_id_type=MESH).start()
        token[...] = jnp.zeros_like(token)

    outs = pl.pallas_call(
        body, name=name, in_specs=[HBM] * n + [ANY], out_specs=(SEM, SEM, *[HBM] * n, VMEM_SPEC),
        out_shape=(pltpu.SemaphoreType.DMA((4 * n,)), pltpu.SemaphoreType.DMA((4 * n,)), *[_hbm(a) for a in lands], TOKEN),
        input_output_aliases={i: 2 + i for i in range(n)},
        compiler_params=pltpu.CompilerParams(has_side_effects=EFFECT))(*[pltpu.with_memory_space_constraint(a, pltpu.HBM) for a in lands], after)
    return outs[0], outs[1], list(outs[2:2 + n]), outs[-1]


def _gather_forward(lands, send0, recv0, after, name):
    n = len(lands)

    def body(*refs):
        ins = refs[:n]
        send0, recv0 = refs[n], refs[n + 1]
        send1, recv1 = refs[n + 2 + len(after)], refs[n + 3 + len(after)]
        token = refs[-1]
        x, y, c = _position()
        chips = [(1 - x, y), (x, 1 - y), (1 - x, 1 - y)]
        for wi in range(n):
            for j, chip in enumerate(chips):
                rows = ins[wi].at[_block_of(*chip, c)]
                pltpu.make_async_remote_copy(
                    src_ref=rows, dst_ref=rows, send_sem=send0.at[4 * wi + 1 + j], recv_sem=recv0.at[4 * wi + 1 + j],
                    device_id=(*chip, c), device_id_type=MESH).wait_recv()
                pltpu.make_async_remote_copy(
                    src_ref=rows, dst_ref=rows, send_sem=send1.at[3 * wi + j], recv_sem=recv1.at[3 * wi + j],
                    device_id=(x, y, 1 - c), device_id_type=MESH).start()
        token[...] = jnp.zeros_like(token)

    outs = pl.pallas_call(
        body, name=name, in_specs=[HBM] * n + [SEM, SEM] + [ANY] * len(after), out_specs=(SEM, SEM, *[HBM] * n, VMEM_SPEC),
        out_shape=(pltpu.SemaphoreType.DMA((3 * n,)), pltpu.SemaphoreType.DMA((3 * n,)), *[_hbm(a) for a in lands], TOKEN),
        input_output_aliases={i: 2 + i for i in range(n)},
        compiler_params=pltpu.CompilerParams(has_side_effects=EFFECT))(*lands, send0, recv0, *after)
    return outs[0], outs[1], list(outs[2:2 + n]), outs[-1]


def _gather_wait(lands, send0, recv0, send1, recv1, after, name):
    n = len(lands)

    def body(*refs):
        ins = refs[:n]
        send0, recv0, send1, recv1 = refs[n:n + 4]
        x, y, c = _position()
        mine = _block_of(x, y, c)
        sibling = (x, y, 1 - c)
        peers = [sibling, (1 - x, y, c), (x, 1 - y, c), (1 - x, 1 - y, c)]
        chips = [(1 - x, y), (x, 1 - y), (1 - x, 1 - y)]
        for wi in range(n):
            own = ins[wi].at[mine]
            for k, peer in enumerate(peers):
                pltpu.make_async_remote_copy(src_ref=own, dst_ref=own, send_sem=send0.at[4 * wi + k], recv_sem=recv0.at[4 * wi + k],
                                             device_id=peer, device_id_type=MESH).wait_send()
            theirs = ins[wi].at[_block_of(*sibling)]
            pltpu.make_async_remote_copy(src_ref=theirs, dst_ref=theirs, send_sem=send0.at[4 * wi], recv_sem=recv0.at[4 * wi],
                                         device_id=sibling, device_id_type=MESH).wait_recv()
            for j, chip in enumerate(chips):
                sent = ins[wi].at[_block_of(*chip, c)]
                got = ins[wi].at[_block_of(*chip, 1 - c)]
                pltpu.make_async_remote_copy(src_ref=sent, dst_ref=sent, send_sem=send1.at[3 * wi + j], recv_sem=recv1.at[3 * wi + j],
                                             device_id=sibling, device_id_type=MESH).wait_send()
                pltpu.make_async_remote_copy(src_ref=got, dst_ref=got, send_sem=send1.at[3 * wi + j], recv_sem=recv1.at[3 * wi + j],
                                             device_id=sibling, device_id_type=MESH).wait_recv()

    outs = pl.pallas_call(
        body, name=name, in_specs=[HBM] * n + [SEM] * 4 + [ANY], out_specs=tuple([HBM] * n),
        out_shape=tuple(_hbm(a) for a in lands), input_output_aliases={i: i for i in range(n)},
        compiler_params=pltpu.CompilerParams(has_side_effects=EFFECT))(*lands, send0, recv0, send1, recv1, after)
    return list(outs)


def _exchange_start(blocks, name):
    n = len(blocks)

    def body(*refs):
        srcs, lands = refs[:n], refs[n:2 * n]
        send, recv = refs[2 * n], refs[2 * n + 1]
        token = refs[-1]
        me = _position()
        for wi in range(n):
            for k in range(1, N_DEV):
                peer = _flip(me, k)
                pltpu.make_async_remote_copy(
                    src_ref=srcs[wi].at[_block_of(*peer)], dst_ref=lands[wi].at[k - 1], send_sem=send.at[7 * wi + k - 1],
                    recv_sem=recv.at[7 * wi + k - 1], device_id=peer, device_id_type=MESH).start()
        token[...] = jnp.zeros_like(token)

    zones = [_landing((N_DEV - 1,) + b.shape[1:], b.dtype) for b in blocks]
    outs = pl.pallas_call(
        body, name=name, in_specs=[HBM] * (2 * n), out_specs=(SEM, SEM, *[HBM] * (2 * n), VMEM_SPEC),
        out_shape=(pltpu.SemaphoreType.DMA((7 * n,)), pltpu.SemaphoreType.DMA((7 * n,)), *[_hbm(a) for a in blocks],
                   *[_hbm(z) for z in zones], TOKEN),
        input_output_aliases={i: 2 + i for i in range(2 * n)},
        compiler_params=pltpu.CompilerParams(has_side_effects=EFFECT))(
            *[pltpu.with_memory_space_constraint(b, pltpu.HBM) for b in blocks], *zones)
    return outs[0], outs[1], list(outs[2:2 + n]), list(outs[2 + n:2 + 2 * n]), outs[-1]


def _exchange_wait(groups, after, name):
    flat, counts = [], []
    for send, recv, blocks, zones in groups:
        flat += [*blocks, *zones, send, recv]
        counts.append(len(blocks))

    def body(*refs):
        me = _position()
        pos = 0
        for n in counts:
            srcs, lands = refs[pos:pos + n], refs[pos + n:pos + 2 * n]
            send, recv = refs[pos + 2 * n], refs[pos + 2 * n + 1]
            pos += 2 * n + 2
            for wi in range(n):
                for k in range(1, N_DEV):
                    peer = _flip(me, k)
                    cp = pltpu.make_async_remote_copy(
                        src_ref=srcs[wi].at[_block_of(*peer)], dst_ref=lands[wi].at[k - 1], send_sem=send.at[7 * wi + k - 1],
                        recv_sem=recv.at[7 * wi + k - 1], device_id=peer, device_id_type=MESH)
                    cp.wait_send()
                    cp.wait_recv()

    in_specs, out_specs, out_shape, aliases = [], [], [], {}
    i = 0
    for n, (send, recv, blocks, zones) in zip(counts, groups):
        for a in (*blocks, *zones):
            aliases[i] = len(out_shape)
            in_specs.append(HBM)
            out_specs.append(HBM)
            out_shape.append(_hbm(a))
            i += 1
        in_specs += [SEM, SEM]
        i += 2
    outs = pl.pallas_call(
        body, name=name, in_specs=in_specs + [ANY], out_specs=tuple(out_specs), out_shape=tuple(out_shape),
        input_output_aliases=aliases, compiler_params=pltpu.CompilerParams(has_side_effects=EFFECT))(*flat, after)
    res, pos = [], 0
    for n in counts:
        res.append((list(outs[pos:pos + n]), list(outs[pos + n:pos + 2 * n])))
        pos += 2 * n
    return res


def _adamw(w, g, m, v):
    m = ADAM_B1 * m + (1.0 - ADAM_B1) * g
    v = ADAM_B2 * v + (1.0 - ADAM_B2) * (g * g)
    m_hat = m / (1.0 - ADAM_B1 ** ADAM_STEP)
    v_hat = v / (1.0 - ADAM_B2 ** ADAM_STEP)
    delta = -ADAM_LR * (m_hat / (jnp.sqrt(v_hat) + ADAM_EPS) + ADAM_WD * w)
    return delta, m, v


def _reduce_adamw(blocks, zone, mine, w, m, v, name):
    R, C = w.shape
    rt = 128 if R % 128 == 0 else 64
    assert R % rt == 0

    def body(mine_ref, own_ref, z_ref, w_ref, m_ref, v_ref, g_out, d_out, m_out, v_out):
        g = own_ref[...].astype(F32)
        for s in range(N_DEV - 1):
            g = g + z_ref[s].astype(F32)
        delta, m2, v2 = _adamw(w_ref[...], g, m_ref[...], v_ref[...])
        g_out[...] = g
        d_out[...] = delta
        m_out[...] = m2
        v_out[...] = v2

    spec = pl.BlockSpec((rt, C), lambda i, mine_ref: (i, 0))
    grid_spec = pltpu.PrefetchScalarGridSpec(
        num_scalar_prefetch=1, grid=(R // rt,),
        in_specs=[pl.BlockSpec((None, rt, C), lambda i, mine_ref: (mine_ref[0], i, 0)),
                  pl.BlockSpec((N_DEV - 1, rt, C), lambda i, mine_ref: (0, i, 0)), spec, spec, spec],
        out_specs=[spec] * 4)
    return pl.pallas_call(body, grid_spec=grid_spec, out_shape=[jax.ShapeDtypeStruct((R, C), F32)] * 4,
                          compiler_params=_params("parallel"), name=name)(mine, blocks, zone, w, m, v)


def _small_step(part, w, m, v, after):
    R, C = part.shape

    def body(part_ref, w_ref, m_ref, v_ref, *rest):
        g_out, d_out, m_out, v_out, gath, send, recv = rest[len(after):]
        me = _position()
        gath[_block_of(*me)] = part_ref[...]

        def copy(k, slot):
            return pltpu.make_async_remote_copy(
                src_ref=part_ref, dst_ref=gath.at[slot], send_sem=send.at[k - 1], recv_sem=recv.at[k - 1],
                device_id=_flip(me, k), device_id_type=MESH)

        sent = [copy(k, _block_of(*me)) for k in range(1, N_DEV)]
        for cp in sent:
            cp.start()
        for k in range(1, N_DEV):
            copy(k, _block_of(*_flip(me, k))).wait_recv()
        for cp in sent:
            cp.wait_send()
        g = gath[0]
        for s in range(1, N_DEV):
            g = g + gath[s]
        delta, m2, v2 = _adamw(w_ref[...], g, m_ref[...], v_ref[...])
        g_out[...] = g
        d_out[...] = delta
        m_out[...] = m2
        v_out[...] = v2

    vm = pl.BlockSpec(memory_space=pltpu.VMEM)
    return pl.pallas_call(
        body, in_specs=[vm] * 4 + [ANY] * len(after), out_specs=[vm] * 4, out_shape=[jax.ShapeDtypeStruct((R, C), F32)] * 4,
        scratch_shapes=[pltpu.VMEM((N_DEV, R, C), F32), pltpu.SemaphoreType.DMA((7,)), pltpu.SemaphoreType.DMA((7,))],
        name="small_step")(part, w, m, v, *after)


COLUMN_SHARDED = ("w_in", "w_sb_out", "w_ca_out", "w_ffn_in", "w_ple_in")
ROW_SHARDED = ("w_mix_out", "w_ffn_out", "w_ple_gate")
BIG = COLUMN_SHARDED + ROW_SHARDED
SMALL = ("g_mix", "g_ffn", "g_ple", "g_final", "rel_bias")
WEIGHTS = ("w_in", "w_sb_out", "w_ca_out", "w_mix_out", "rel_bias", "g_mix", "g_ffn", "g_ple", "g_final",
           "w_ffn_in", "w_ffn_out", "w_ple_in", "w_ple_gate")


def _pack_small(t, D):
    rows = [t[n].reshape(1, D) for n in SMALL[:4]]
    rb = t["rel_bias"].reshape(1, -1)
    rows.append(jnp.pad(rb, ((0, 0), (0, D - rb.shape[1]))))
    return jnp.concatenate(rows + [jnp.zeros((8 - len(rows), D), F32)], axis=0)


def _unpack_small(a, like):
    out = {n: a[i].reshape(like[n].shape) for i, n in enumerate(SMALL[:4])}
    out["rel_bias"] = a[4, :like["rel_bias"].size].reshape(like["rel_bias"].shape)
    return out


GATHER_GROUPS = (("w_sb_out", "w_ca_out", "w_mix_out"), ("w_ffn_in",), ("w_ffn_out", "w_ple_gate", "w_ple_in"))
FORWARD_AFTER = ("mm_in", "attention", "mm_ffn_in")
GRAD_GROUPS = (("w_ple_in", "w_ple_gate"), ("w_ffn_out",), ("w_ffn_in",), ("w_mix_out", "w_sb_out", "w_ca_out"), ("w_in",))


class _Exchange:
    def __init__(self, shards):
        me = _position()
        self.mine = _block_of(*me)
        self.shapes = {n: ((N_DEV,) + s.shape if n in COLUMN_SHARDED else (N_DEV * s.shape[0], s.shape[1])) for n, s in shards.items()}
        self.tokens = []
        w_in = _all_gather([shards["w_in"]])[0]
        self.ready = {"w_in": w_in}
        self.gathers = []
        for gi, names in enumerate(GATHER_GROUPS):
            lands = [lax.dynamic_update_slice(lax.empty((N_DEV,) + shards[n].shape, BF16), shards[n][None], (self.mine, 0, 0))
                     for n in names]
            behind = self.tokens[-1] if self.tokens else w_in
            send0, recv0, lands, token = _gather_start(lands, behind, f"gather_start_{gi}")
            self.tokens.append(token)
            self.gathers.append(dict(names=names, lands=lands, sems=(send0, recv0), token=token))
        self.grads = {}
        self.exchanges = []

    def pending(self):
        tokens, self.tokens = self.tokens, []
        return tokens

    def stage(self, tag, *made):
        gi = FORWARD_AFTER.index(tag)
        gth = self.gathers[gi]
        send1, recv1, lands, token = _gather_forward(gth["lands"], *gth["sems"], made, f"gather_forward_{gi}")
        gth.update(lands=lands, sems=gth["sems"] + (send1, recv1), token=token)
        self.tokens.append(token)

    def weight(self, name, after=None):
        if name not in self.ready:
            gi = next(i for i, names in enumerate(GATHER_GROUPS) if name in names)
            gth = self.gathers[gi]
            for n, a in zip(gth["names"], _gather_wait(gth["lands"], *gth["sems"], gth["token"] if after is None else after, f"gather_wait_{gi}")):
                self.ready[n] = a if n in COLUMN_SHARDED else a.reshape(self.shapes[n])
        return self.ready[name]

    def grad(self, name, blocks):
        self.grads[name] = blocks if name in COLUMN_SHARDED else blocks.reshape((N_DEV, -1, blocks.shape[-1]))
        names = next(names for names in GRAD_GROUPS if name in names)
        if all(n in self.grads for n in names):
            send, recv, blocks, zones, token = _exchange_start([self.grads[n] for n in names], "exchange_start_" + names[0])
            self.exchanges.append(dict(names=names, state=(send, recv, blocks, zones)))
            self.tokens.append(token)

    def collect(self, which, after, name):
        sel = [e for e in self.exchanges if GRAD_GROUPS.index(e["names"]) in which]
        out = {}
        for e, (blocks, zones) in zip(sel, _exchange_wait([e["state"] for e in sel], after, name)):
            out.update({n: (b, z) for n, b, z in zip(e["names"], blocks, zones)})
        return out


def kernel(x, p, w_in, w_sb_out, w_ca_out, w_mix_out, rel_bias, g_mix, g_ffn, g_ple, g_final, w_ffn_in, w_ffn_out, w_ple_in, w_ple_gate, loss_target, m_w_in, m_w_sb_out, m_w_ca_out, m_w_mix_out, m_rel_bias, m_g_mix, m_g_ffn, m_g_ple, m_g_final, m_w_ffn_in, m_w_ffn_out, m_w_ple_in, m_w_ple_gate, v_w_in, v_w_sb_out, v_w_ca_out, v_w_mix_out, v_rel_bias, v_g_mix, v_g_ffn, v_g_ple, v_g_final, v_w_ffn_in, v_w_ffn_out, v_w_ple_in, v_w_ple_gate):
    wts = dict(w_in=w_in, w_sb_out=w_sb_out, w_ca_out=w_ca_out, w_mix_out=w_mix_out, rel_bias=rel_bias, g_mix=g_mix, g_ffn=g_ffn,
               g_ple=g_ple, g_final=g_final, w_ffn_in=w_ffn_in, w_ffn_out=w_ffn_out, w_ple_in=w_ple_in, w_ple_gate=w_ple_gate)
    mom = dict(w_in=m_w_in, w_sb_out=m_w_sb_out, w_ca_out=m_w_ca_out, w_mix_out=m_w_mix_out, rel_bias=m_rel_bias, g_mix=m_g_mix,
               g_ffn=m_g_ffn, g_ple=m_g_ple, g_final=m_g_final, w_ffn_in=m_w_ffn_in, w_ffn_out=m_w_ffn_out, w_ple_in=m_w_ple_in,
               w_ple_gate=m_w_ple_gate)
    var = dict(w_in=v_w_in, w_sb_out=v_w_sb_out, w_ca_out=v_w_ca_out, w_mix_out=v_w_mix_out, rel_bias=v_rel_bias, g_mix=v_g_mix,
               g_ffn=v_g_ffn, g_ple=v_g_ple, g_final=v_g_final, w_ffn_in=v_w_ffn_in, w_ffn_out=v_w_ffn_out, w_ple_in=v_w_ple_in,
               w_ple_gate=v_w_ple_gate)
    T, D = x.shape[1], x.shape[2]
    shard = {n: wts[n].reshape(wts[n].shape[-2:]) for n in BIG}
    comm = _Exchange({n: _cast_bf16(shard[n], "cast_" + n) for n in BIG})
    g = dict(g_mix=g_mix.reshape(1, D), g_ffn=g_ffn.reshape(1, D), g_ple=g_ple.reshape(1, D), g_final=g_final.reshape(1, D),
             rel_bias=rel_bias.reshape(rel_bias.shape[-2:]))

    loss, grad_x, dsmall = _local_step(x.reshape(T, D), p.reshape(T, -1), loss_target.reshape(T, D), comm, g)
    loss = lax.psum(loss[0, 0], ("x", "y", "c"))

    mine = jnp.reshape(comm.mine, (1,)).astype(jnp.int32)
    grad, delta, new_m, new_v = {}, {}, {}, {}

    def update(parts):
        done = []
        for n, (blocks, zone) in parts.items():
            outs = _reduce_adamw(blocks, zone, mine, shard[n], mom[n].reshape(shard[n].shape), var[n].reshape(shard[n].shape), "adamw_" + n)
            grad[n], delta[n], new_m[n], new_v[n] = [o.reshape(wts[n].shape) for o in outs]
            done.append(outs[0])
        return done

    done = update(comm.collect(range(len(GRAD_GROUPS) - 1), grad_x, "exchange_wait_rest"))
    outs = _small_step(_pack_small(dsmall, D), _pack_small(wts, D), _pack_small(mom, D), _pack_small(var, D), done)
    for dst, a in zip((grad, delta, new_m, new_v), outs):
        dst.update(_unpack_small(a, wts))
    update(comm.collect([len(GRAD_GROUPS) - 1], outs[0], "exchange_wait_w_in"))

    return (loss, grad_x.reshape(x.shape), *[grad[n] for n in WEIGHTS], *[delta[n] for n in WEIGHTS],
            *[new_m[n] for n in WEIGHTS], *[new_v[n] for n in WEIGHTS])
```

```python
import functools

import jax
import jax.numpy as jnp
from jax import lax
from jax.experimental import pallas as pl
from jax.experimental.pallas import tpu as pltpu

F32, BF16 = jnp.float32, jnp.bfloat16

N_DEV = 8
HEAD_DIM = 128
CHUNK = 64
LEFT_CHUNKS = 8
REL_CLIP = 128
N_REL = REL_CLIP + CHUNK
PAIR = 2 * CHUNK
PBAND = (LEFT_CHUNKS + 2) * CHUNK
PAD = LEFT_CHUNKS * CHUNK
SB_BLOCK = 256
ROWS = 256
EPS = 1e-6
NEG = -1e30
SCALE = HEAD_DIM ** -0.5
VMEM_LIMIT_BYTES = 56 * 1024 * 1024

ADAM_LR, ADAM_B1, ADAM_B2, ADAM_EPS, ADAM_WD, ADAM_STEP = 0.001, 0.9, 0.999, 1e-08, 0.01, 10

ANY = pl.BlockSpec(memory_space=pl.ANY)
NN = (((1,), (0,)), ((), ()))
NT = (((1,), (1,)), ((), ()))
TN = (((0,), (0,)), ((), ()))
MESH = pl.DeviceIdType.MESH


def _params(*sem):
    return pltpu.CompilerParams(dimension_semantics=sem or None, vmem_limit_bytes=VMEM_LIMIT_BYTES)


def _dot(a, b, dims=NN):
    return lax.dot_general(a, b, dims, preferred_element_type=F32)


def _mm(a, b, *, mode, tm, tn, tk, out_dtype, name, b_blocked=False, out_block=None, res=None, after=()):
    if mode == "nn":
        M, K = a.shape
        a_spec = pl.BlockSpec((tm, tk), lambda i, j, k: (i, k))
        if b_blocked:
            G, _, nb = b.shape
            N, per = G * nb, nb // tn
            assert nb % tn == 0
            b_spec = pl.BlockSpec((None, tk, tn), lambda i, j, k: (j // per, k, j % per))
        else:
            N = b.shape[1]
            b_spec = pl.BlockSpec((tk, tn), lambda i, j, k: (k, j))
        dims = NN
    elif mode == "nt":
        M, K = a.shape
        a_spec = pl.BlockSpec((tm, tk), lambda i, j, k: (i, k))
        if b_blocked:
            G, N, nb = b.shape
            per = nb // tk
            assert K == G * nb and nb % tk == 0
            b_spec = pl.BlockSpec((None, tn, tk), lambda i, j, k: (k // per, j, k % per))
        else:
            N = b.shape[0]
            b_spec = pl.BlockSpec((tn, tk), lambda i, j, k: (j, k))
        dims = NT
    else:
        K, M = a.shape
        N = b.shape[1]
        a_spec = pl.BlockSpec((tk, tm), lambda i, j, k: (k, i))
        b_spec = pl.BlockSpec((tk, tn), lambda i, j, k: (k, j))
        dims = TN
    assert M % tm == 0 and N % tn == 0 and K % tk == 0, (name, M, N, K, tm, tn, tk)
    nk = K // tk
    if out_block is None:
        out_shape = jax.ShapeDtypeStruct((M, N), out_dtype)
        o_spec = pl.BlockSpec((tm, tn), lambda i, j, k: (i, j))
    else:
        per_o = out_block // tn
        assert out_block % tn == 0
        out_shape = jax.ShapeDtypeStruct((N // out_block, M, out_block), out_dtype)
        o_spec = pl.BlockSpec((None, tm, tn), lambda i, j, k: (j // per_o, i, j % per_o))
    in_specs = [a_spec, b_spec]
    args = [a, b]
    if res is not None:
        in_specs.append(pl.BlockSpec((tm, tn), lambda i, j, k: (i, j)))
        args.append(res)
    n_in = len(args) + len(after)

    def body(*refs):
        a_ref, b_ref = refs[0], refs[1]
        r_ref = refs[2] if res is not None else None
        o_ref = refs[n_in]

        def finish(acc):
            if r_ref is not None:
                acc = acc + r_ref[...]
            o_ref[...] = acc.astype(o_ref.dtype)

        if nk == 1:
            finish(_dot(a_ref[...], b_ref[...], dims))
        else:
            acc_ref = refs[-1]
            k = pl.program_id(2)

            @pl.when(k == 0)
            def _():
                acc_ref[...] = jnp.zeros_like(acc_ref)

            acc_ref[...] += _dot(a_ref[...], b_ref[...], dims)

            @pl.when(k == nk - 1)
            def _():
                finish(acc_ref[...])

    return pl.pallas_call(
        body, grid=(M // tm, N // tn, nk), in_specs=in_specs + [ANY] * len(after), out_specs=o_spec, out_shape=out_shape,
        scratch_shapes=[] if nk == 1 else [pltpu.VMEM((tm, tn), F32)],
        compiler_params=_params("parallel", "parallel", "arbitrary"), name=name)(*args, *after)


def _row_spec(d, col=0):
    return pl.BlockSpec((ROWS, d), lambda i: (i, col))


def _vec_spec(d):
    return pl.BlockSpec((1, d), lambda i: (0, 0))


def _rms(x):
    return lax.rsqrt(jnp.mean(x * x, axis=-1, keepdims=True) + EPS)


def _norm_fwd(x, g, name):
    T, D = x.shape

    def body(x_ref, g_ref, h_ref):
        xv = x_ref[...]
        h_ref[...] = (xv * _rms(xv) * g_ref[...]).astype(BF16)

    return pl.pallas_call(body, grid=(T // ROWS,), in_specs=[_row_spec(D), _vec_spec(D)], out_specs=_row_spec(D),
                          out_shape=jax.ShapeDtypeStruct((T, D), BF16), compiler_params=_params("parallel"), name=name)(x, g)


def _norm_bwd_math(dh, xv, gv):
    r = _rms(xv)
    xhat = xv * r
    dxhat = dh * gv
    dx = r * (dxhat - xhat * jnp.mean(dxhat * xhat, axis=-1, keepdims=True))
    dg = jnp.sum(dh * xhat, axis=0, keepdims=True)
    return dx, dg


def _norm_bwd(dh, x, g, dres, name):
    T, D = x.shape

    def body(dh_ref, x_ref, g_ref, dres_ref, dx_ref, dxb_ref, dg_ref):
        dx, dg = _norm_bwd_math(dh_ref[...], x_ref[...], g_ref[...])
        dx = dx + dres_ref[...]
        dx_ref[...] = dx
        dxb_ref[...] = dx.astype(BF16)

        @pl.when(pl.program_id(0) == 0)
        def _():
            dg_ref[...] = jnp.zeros_like(dg_ref)

        dg_ref[...] += dg

    return pl.pallas_call(
        body, grid=(T // ROWS,), in_specs=[_row_spec(D), _row_spec(D), _vec_spec(D), _row_spec(D)],
        out_specs=[_row_spec(D), _row_spec(D), _vec_spec(D)],
        out_shape=[jax.ShapeDtypeStruct((T, D), F32), jax.ShapeDtypeStruct((T, D), BF16), jax.ShapeDtypeStruct((1, D), F32)],
        compiler_params=_params("arbitrary"), name=name)(dh, x, g, dres)


def _merge_fwd(proj, a_sb, a_ca, D, gate_col):
    T = proj.shape[0]

    def body(gs_ref, gc_ref, a_ref, b_ref, o_ref):
        o_ref[...] = (jax.nn.sigmoid(gs_ref[...]) * a_ref[...] + jax.nn.sigmoid(gc_ref[...]) * b_ref[...]).astype(BF16)

    return pl.pallas_call(
        body, grid=(T // ROWS,), in_specs=[_row_spec(D, gate_col), _row_spec(D, gate_col + 1), _row_spec(D), _row_spec(D)],
        out_specs=_row_spec(D), out_shape=jax.ShapeDtypeStruct((T, D), BF16),
        compiler_params=_params("parallel"), name="merge_fwd")(proj, proj, a_sb, a_ca)


def _merge_bwd(dm, proj, a_sb, a_ca, D, gate_col):
    T = proj.shape[0]

    def body(dm_ref, gs_ref, gc_ref, a_ref, b_ref, da_ref, db_ref, dgs_ref, dgc_ref):
        dmv = dm_ref[...]
        ss, sc = jax.nn.sigmoid(gs_ref[...]), jax.nn.sigmoid(gc_ref[...])
        da_ref[...] = (dmv * ss).astype(BF16)
        db_ref[...] = (dmv * sc).astype(BF16)
        dgs_ref[...] = (dmv * a_ref[...] * ss * (1.0 - ss)).astype(BF16)
        dgc_ref[...] = (dmv * b_ref[...] * sc * (1.0 - sc)).astype(BF16)

    return pl.pallas_call(
        body, grid=(T // ROWS,),
        in_specs=[_row_spec(D), _row_spec(D, gate_col), _row_spec(D, gate_col + 1), _row_spec(D), _row_spec(D)],
        out_specs=[_row_spec(D)] * 4, out_shape=[jax.ShapeDtypeStruct((T, D), BF16)] * 4,
        compiler_params=_params("parallel"), name="merge_bwd")(dm, proj, proj, a_sb, a_ca)


def _swiglu_fwd(gu, after=()):
    T, F2 = gu.shape
    F = F2 // 2

    def body(g_ref, u_ref, *rest):
        gv = g_ref[...]
        rest[-1][...] = (gv * jax.nn.sigmoid(gv) * u_ref[...]).astype(BF16)

    return pl.pallas_call(body, grid=(T // ROWS,), in_specs=[_row_spec(F, 0), _row_spec(F, 1)] + [ANY] * len(after), out_specs=_row_spec(F),
                          out_shape=jax.ShapeDtypeStruct((T, F), BF16), compiler_params=_params("parallel"), name="swiglu_fwd")(gu, gu, *after)


def _swiglu_bwd(dact, gu):
    T, F2 = gu.shape
    F = F2 // 2

    def body(d_ref, g_ref, u_ref, o_ref):
        dv, gv, uv = d_ref[...], g_ref[...], u_ref[...]
        s = jax.nn.sigmoid(gv)
        o_ref[:, 0:F] = (dv * uv * s * (1.0 + gv * (1.0 - s))).astype(BF16)
        o_ref[:, F:F2] = (dv * gv * s).astype(BF16)

    return pl.pallas_call(body, grid=(T // ROWS,), in_specs=[_row_spec(F), _row_spec(F, 0), _row_spec(F, 1)], out_specs=_row_spec(F2),
                          out_shape=jax.ShapeDtypeStruct((T, F2), BF16), compiler_params=_params("parallel"), name="swiglu_bwd")(dact, gu, gu)


def _tail(x3, zg, pe, g_final, target):
    T, D = x3.shape

    def body(x3_ref, zg_ref, pe_ref, g_ref, t_ref, loss_ref, dx_ref, dpe_ref, dzg_ref, dg_ref):
        gate = jax.nn.sigmoid(zg_ref[...])
        pev = pe_ref[...]
        x4 = x3_ref[...] + gate * pev
        gv = g_ref[...]
        xhat = x4 * _rms(x4)
        err = xhat * gv - t_ref[...]
        part = 0.5 * jnp.sum(jnp.mean(err * err, axis=-1, keepdims=True), axis=0, keepdims=True)
        dx, dg = _norm_bwd_math(err * (1.0 / D), x4, gv)
        dx_ref[...] = dx
        dpe_ref[...] = (dx * gate).astype(BF16)
        dzg_ref[...] = (dx * pev * gate * (1.0 - gate)).astype(BF16)

        @pl.when(pl.program_id(0) == 0)
        def _():
            dg_ref[...] = jnp.zeros_like(dg_ref)
            loss_ref[...] = jnp.zeros_like(loss_ref)

        dg_ref[...] += dg
        loss_ref[...] += jnp.broadcast_to(part, loss_ref.shape)

    return pl.pallas_call(
        body, grid=(T // ROWS,), in_specs=[_row_spec(D), _row_spec(D), _row_spec(D), _vec_spec(D), _row_spec(D)],
        out_specs=[_vec_spec(128), _row_spec(D), _row_spec(D), _row_spec(D), _vec_spec(D)],
        out_shape=[jax.ShapeDtypeStruct((1, 128), F32), jax.ShapeDtypeStruct((T, D), F32), jax.ShapeDtypeStruct((T, D), BF16),
                   jax.ShapeDtypeStruct((T, D), BF16), jax.ShapeDtypeStruct((1, D), F32)],
        compiler_params=_params("arbitrary"), name="tail")(x3, zg, pe, g_final, target)


def _cast_bf16(x, name):
    R, C = x.shape
    rows = next(r for r in (ROWS, 128, 64, 32, 16) if R % r == 0)

    def body(x_ref, o_ref):
        o_ref[...] = x_ref[...].astype(BF16)

    spec = pl.BlockSpec((rows, C), lambda i: (i, 0))
    return pl.pallas_call(body, grid=(R // rows,), in_specs=[spec], out_specs=spec, out_shape=jax.ShapeDtypeStruct((R, C), BF16),
                          compiler_params=_params("parallel"), name=name)(x)


def _head_spec(T, col0, heads=1):
    return pl.BlockSpec((T, heads * HEAD_DIM), lambda h, *_: (0, col0 + h))


SB_HEADS = 2


def _triangle(n, right):
    j = lax.broadcasted_iota(jnp.int32, (n, n), 0)
    s = lax.broadcasted_iota(jnp.int32, (n, n), 1)
    return jnp.where((j > s) if right else (j < s), 1.0, 0.0).astype(BF16)


def _lane_scan(x, tri):
    hi = x.astype(BF16)
    lo = (x - hi.astype(F32)).astype(BF16)
    return _dot(hi, tri) + _dot(lo, tri)


def _head_cols(ref, rows, hh):
    return ref[rows, hh * HEAD_DIM:(hh + 1) * HEAD_DIM]


def _sb_tile(qv, kk, past, c_lk, tri):
    z = _dot(qv, kk, NT) * SCALE
    sp = jnp.log(1.0 + jnp.exp(-jnp.abs(z)))
    ls_pos = jnp.minimum(z, 0.0) - sp
    lk = jnp.minimum(-z, 0.0) - sp
    if past is not None:
        lk = jnp.where(past, lk, 0.0)
    right = c_lk + _lane_scan(lk, tri)
    a = jnp.exp(ls_pos + right)
    if past is not None:
        a = jnp.where(past, a, 0.0)
    return ls_pos, a, right[:, 0:1] + lk[:, 0:1]


def _sb_diagonal():
    B = SB_BLOCK
    return lax.broadcasted_iota(jnp.int32, (B, B), 1) < lax.broadcasted_iota(jnp.int32, (B, B), 0)


def _sb_rows(kb):
    return pl.ds(pl.multiple_of(kb * SB_BLOCK, SB_BLOCK), SB_BLOCK)


def _sb_fwd(proj, n_heads, after=()):
    T = proj.shape[0]
    B, HP = SB_BLOCK, SB_HEADS
    assert n_heads % HP == 0

    def body(q_ref, k_ref, v_ref, *rest):
        y_ref = rest[-1]
        qb = pl.program_id(1)
        tri = _triangle(B, right=True)
        qv = [_head_cols(q_ref, slice(None), hh).astype(BF16) for hh in range(HP)]

        def tile(kb, carry, past):
            out = []
            for hh in range(HP):
                acc, c_lk = carry[hh]
                kk = _head_cols(k_ref, _sb_rows(kb), hh).astype(BF16)
                vv = _head_cols(v_ref, _sb_rows(kb), hh).astype(BF16)
                _, a, c_lk = _sb_tile(qv[hh], kk, past, c_lk, tri)
                out.append((acc + _dot(a.astype(BF16), vv), c_lk))
            return tuple(out)

        init = tuple((jnp.zeros((B, HEAD_DIM), F32), jnp.zeros((B, 1), F32)) for _ in range(HP))
        res = lax.fori_loop(1, qb + 1, lambda i, carry: tile(qb - i, carry, None), tile(qb, init, _sb_diagonal()))
        for hh in range(HP):
            y_ref[:, hh * HEAD_DIM:(hh + 1) * HEAD_DIM] = res[hh][0].astype(BF16)

    blk = pl.BlockSpec((B, HP * HEAD_DIM), lambda h, i: (i, h))
    G = n_heads // HP
    return pl.pallas_call(
        body, grid=(G, T // B),
        in_specs=[blk, _head_spec(T, G, HP), _head_spec(T, 2 * G, HP)] + [ANY] * len(after), out_specs=blk,
        out_shape=jax.ShapeDtypeStruct((T, n_heads * HEAD_DIM), BF16),
        compiler_params=_params("parallel", "arbitrary"), name="sb_fwd")(proj, proj, proj, *after)


def _sb_bwd(proj, dy, n_heads):
    T = proj.shape[0]
    B, HP = SB_BLOCK, SB_HEADS
    nq = T // B

    def body(q_ref, k_ref, v_ref, dy_ref, dq_ref, dk_ref, dv_ref, g_s, sig_s, dk_s, dv_s):
        qb = pl.program_id(1)

        @pl.when(qb == 0)
        def _():
            dk_s[...] = jnp.zeros_like(dk_s)
            dv_s[...] = jnp.zeros_like(dv_s)

        tri_r = _triangle(B, right=True)
        tri_l = _triangle(B, right=False)
        qv = [_head_cols(q_ref, slice(None), hh).astype(BF16) for hh in range(HP)]
        dyb = [_head_cols(dy_ref, slice(None), hh).astype(BF16) for hh in range(HP)]

        def sweep(kb, carry, past):
            out = []
            for hh in range(HP):
                kk = _head_cols(k_ref, _sb_rows(kb), hh).astype(BF16)
                vv = _head_cols(v_ref, _sb_rows(kb), hh).astype(BF16)
                ls_pos, a, c_lk = _sb_tile(qv[hh], kk, past, carry[hh], tri_r)
                g_s[hh, kb] = _dot(dyb[hh], vv, NT) * a
                sig_s[hh, kb] = jnp.exp(ls_pos)
                dv_s[_sb_rows(kb), hh * HEAD_DIM:(hh + 1) * HEAD_DIM] += _dot(a.astype(BF16), dyb[hh], TN)
                out.append(c_lk)
            return tuple(out)

        zeros = tuple(jnp.zeros((B, 1), F32) for _ in range(HP))
        lax.fori_loop(1, qb + 1, lambda i, carry: sweep(qb - i, carry, None), sweep(qb, zeros, _sb_diagonal()))

        def back(kb, carry, past):
            out = []
            for hh in range(HP):
                dq, c_g = carry[hh]
                kk = _head_cols(k_ref, _sb_rows(kb), hh).astype(BF16)
                g, sig = g_s[hh, kb], sig_s[hh, kb]
                left = c_g + _lane_scan(g, tri_l)
                dz = g * (1.0 - sig) - left * sig
                if past is not None:
                    dz = jnp.where(past, dz, 0.0)
                dz = (dz * SCALE).astype(BF16)
                dk_s[_sb_rows(kb), hh * HEAD_DIM:(hh + 1) * HEAD_DIM] += _dot(dz, qv[hh], TN)
                out.append((dq + _dot(dz, kk), left[:, B - 1:B] + g[:, B - 1:B]))
            return tuple(out)

        init = tuple((jnp.zeros((B, HEAD_DIM), F32), jnp.zeros((B, 1), F32)) for _ in range(HP))
        res = back(qb, lax.fori_loop(0, qb, lambda kb, carry: back(kb, carry, None), init), _sb_diagonal())
        for hh in range(HP):
            dq_ref[:, hh * HEAD_DIM:(hh + 1) * HEAD_DIM] = res[hh][0].astype(BF16)

        @pl.when(qb == nq - 1)
        def _():
            dk_ref[...] = dk_s[...].astype(BF16)
            dv_ref[...] = dv_s[...].astype(BF16)

    blk = pl.BlockSpec((B, HP * HEAD_DIM), lambda h, i: (i, h))
    G = n_heads // HP
    full = _head_spec(T, 0, HP)
    shp = jax.ShapeDtypeStruct((T, n_heads * HEAD_DIM), BF16)
    return pl.pallas_call(
        body, grid=(G, nq),
        in_specs=[blk, _head_spec(T, G, HP), _head_spec(T, 2 * G, HP), blk], out_specs=[blk, full, full],
        out_shape=[shp, shp, shp],
        scratch_shapes=[pltpu.VMEM((HP, nq, B, B), F32)] * 2 + [pltpu.VMEM((T, HP * HEAD_DIM), F32)] * 2,
        compiler_params=_params("parallel", "arbitrary"), name="sb_bwd")(proj, proj, proj, dy)


DIAGS = PBAND + PAIR


def _diag_onehot():
    d = lax.broadcasted_iota(jnp.int32, (DIAGS, 2 * PAIR), 0)
    r = lax.broadcasted_iota(jnp.int32, (DIAGS, 2 * PAIR), 1)
    return jnp.where(jnp.clip(d - PAIR - PAD, -REL_CLIP, CHUNK - 1) + REL_CLIP == r, 1.0, 0.0)


def _bias_expand(rel_bias):
    H = rel_bias.shape[0]
    table = jnp.pad(rel_bias, ((0, 0), (0, 2 * PAIR - N_REL)))

    def body(rb_ref, o_ref):
        o_ref[...] = lax.dot_general(rb_ref[...], _diag_onehot(), NT, precision=lax.Precision.HIGHEST, preferred_element_type=F32)

    per_diag = pl.pallas_call(body, out_shape=jax.ShapeDtypeStruct((H, DIAGS), F32), name="bias_expand")(table)
    flat = jnp.tile(jnp.pad(per_diag, ((0, 0), (0, 1))), (1, PAIR))[:, :PAIR * DIAGS]
    return flat.reshape(H, PAIR, DIAGS)[:, :, PAIR:]


def _bias_reduce(dbias):
    H = dbias.shape[0]
    padded = jnp.pad(dbias, ((0, 0), (0, 1), (PAIR, 0))).reshape(H, -1)
    skewed = padded[:, :PAIR * (DIAGS + 1)].reshape(H, PAIR, DIAGS + 1)[:, :, :DIAGS]

    def body(s_ref, o_ref):
        per_diag = jnp.sum(s_ref[...], axis=0, keepdims=True)
        o_ref[...] = lax.dot_general(jnp.broadcast_to(per_diag, (8, DIAGS)), _diag_onehot(), NN, precision=lax.Precision.HIGHEST,
                                     preferred_element_type=F32)[0:1]

    return pl.pallas_call(
        body, grid=(H,), in_specs=[pl.BlockSpec((None, PAIR, DIAGS), lambda h: (h, 0, 0))],
        out_specs=pl.BlockSpec((None, 1, 2 * PAIR), lambda h: (h, 0, 0)),
        out_shape=jax.ShapeDtypeStruct((H, 1, 2 * PAIR), F32), compiler_params=_params("parallel"), name="bias_reduce")(skewed)[:, 0]


CA_HEADS = 2


def _ca_mask():
    i = lax.broadcasted_iota(jnp.int32, (PAIR, PBAND), 0)
    j = lax.broadcasted_iota(jnp.int32, (PAIR, PBAND), 1)
    qc, kc = i // CHUNK, j // CHUNK
    return j, (kc >= qc) & (kc <= qc + LEFT_CHUNKS)


def _ca_weights(pr, qp, kb, bias, j, window):
    valid = window & (pr * PAIR + j >= PAD)
    z = jnp.where(valid, _dot(qp, kb, NT) * SCALE + bias, NEG)
    e = jnp.exp(z - jnp.max(z, axis=1, keepdims=True))
    return e / jnp.sum(e, axis=1, keepdims=True)


def _ca_fill(k_ref, v_ref, kpad, vpad):
    T, W = k_ref.shape
    kpad[0:PAD, :] = jnp.zeros((PAD, W), BF16)
    vpad[0:PAD, :] = jnp.zeros((PAD, W), BF16)
    kpad[PAD:PAD + T, :] = k_ref[...].astype(BF16)
    vpad[PAD:PAD + T, :] = v_ref[...].astype(BF16)


def _ca_fwd(proj, bias, n_heads, col0):
    T = proj.shape[0]
    HP = CA_HEADS
    G = n_heads // HP
    assert n_heads % HP == 0 and col0 % HP == 0

    def body(q_ref, k_ref, v_ref, b_ref, y_ref, kpad, vpad):
        _ca_fill(k_ref, v_ref, kpad, vpad)
        j, window = _ca_mask()

        def pair(pr, _):
            r0 = pl.multiple_of(pr * PAIR, PAIR)
            for hh in range(HP):
                qp = _head_cols(q_ref, pl.ds(r0, PAIR), hh).astype(BF16)
                kb = _head_cols(kpad, pl.ds(r0, PBAND), hh)
                vb = _head_cols(vpad, pl.ds(r0, PBAND), hh)
                w = _ca_weights(pr, qp, kb, b_ref[hh], j, window)
                y_ref[pl.ds(r0, PAIR), hh * HEAD_DIM:(hh + 1) * HEAD_DIM] = _dot(w.astype(BF16), vb).astype(BF16)
            return 0

        lax.fori_loop(0, T // PAIR, pair, 0)

    c = col0 // HP
    return pl.pallas_call(
        body, grid=(G,),
        in_specs=[_head_spec(T, c, HP), _head_spec(T, c + G, HP), _head_spec(T, c + 2 * G, HP),
                  pl.BlockSpec((HP, PAIR, PBAND), lambda h: (h, 0, 0))],
        out_specs=_head_spec(T, 0, HP), out_shape=jax.ShapeDtypeStruct((T, n_heads * HEAD_DIM), BF16),
        scratch_shapes=[pltpu.VMEM((PAD + T, HP * HEAD_DIM), BF16)] * 2,
        compiler_params=_params("parallel"), name="ca_fwd")(proj, proj, proj, bias)


def _ca_bwd(proj, bias, dy, n_heads, col0):
    T = proj.shape[0]
    HP = CA_HEADS
    G = n_heads // HP

    def body(q_ref, k_ref, v_ref, b_ref, dy_ref, dq_ref, dk_ref, dv_ref, db_ref, kpad, vpad, dkpad, dvpad):
        _ca_fill(k_ref, v_ref, kpad, vpad)
        dkpad[...] = jnp.zeros_like(dkpad)
        dvpad[...] = jnp.zeros_like(dvpad)
        db_ref[...] = jnp.zeros_like(db_ref)
        j, window = _ca_mask()

        def pair(pr, _):
            r0 = pl.multiple_of(pr * PAIR, PAIR)
            for hh in range(HP):
                cols = slice(hh * HEAD_DIM, (hh + 1) * HEAD_DIM)
                qp = _head_cols(q_ref, pl.ds(r0, PAIR), hh).astype(BF16)
                kb = _head_cols(kpad, pl.ds(r0, PBAND), hh)
                vb = _head_cols(vpad, pl.ds(r0, PBAND), hh)
                w = _ca_weights(pr, qp, kb, b_ref[hh], j, window)
                dyp = _head_cols(dy_ref, pl.ds(r0, PAIR), hh).astype(BF16)
                dw = _dot(dyp, vb, NT)
                dz = w * (dw - jnp.sum(dw * w, axis=1, keepdims=True))
                db_ref[hh] += dz
                dzs = (dz * SCALE).astype(BF16)
                dq_ref[pl.ds(r0, PAIR), cols] = _dot(dzs, kb).astype(BF16)
                dkpad[pl.ds(r0, PBAND), cols] += _dot(dzs, qp, TN)
                dvpad[pl.ds(r0, PBAND), cols] += _dot(w.astype(BF16), dyp, TN)
            return 0

        lax.fori_loop(0, T // PAIR, pair, 0)
        dk_ref[...] = dkpad[PAD:PAD + T, :].astype(BF16)
        dv_ref[...] = dvpad[PAD:PAD + T, :].astype(BF16)

    c = col0 // HP
    full = _head_spec(T, 0, HP)
    bspec = pl.BlockSpec((HP, PAIR, PBAND), lambda h: (h, 0, 0))
    shp = jax.ShapeDtypeStruct((T, n_heads * HEAD_DIM), BF16)
    return pl.pallas_call(
        body, grid=(G,),
        in_specs=[_head_spec(T, c, HP), _head_spec(T, c + G, HP), _head_spec(T, c + 2 * G, HP), bspec, full],
        out_specs=[full, full, full, bspec],
        out_shape=[shp, shp, shp, jax.ShapeDtypeStruct((n_heads, PAIR, PBAND), F32)],
        scratch_shapes=[pltpu.VMEM((PAD + T, HP * HEAD_DIM), BF16)] * 2 + [pltpu.VMEM((PAD + T, HP * HEAD_DIM), F32)] * 2,
        compiler_params=_params("parallel"), name="ca_bwd")(proj, proj, proj, bias, dy)


def _local_step(x, p, target, comm, g):
    T, D = x.shape
    H = g["rel_bias"].shape[0]
    W = H * HEAD_DIM
    nb_in = comm.shapes["w_in"][2]
    nb_ff = comm.shapes["w_ffn_in"][2]
    nb_o = comm.shapes["w_sb_out"][2]
    nb_p = comm.shapes["w_ple_in"][2]
    tm = min(T, 1024)
    tn = min(D, 1024)
    gate_col = 6 * W // D

    h1 = _norm_fwd(x, g["g_mix"], "norm1")
    comm.stage("norm1", h1)
    proj = _mm(h1, comm.weight("w_in", h1), mode="nn", tm=tm, tn=nb_in, tk=D, out_dtype=F32, b_blocked=True, after=comm.pending(), name="mm_in")
    comm.stage("mm_in", proj)
    y_sb = _sb_fwd(proj, H, comm.pending())
    bias = _bias_expand(g["rel_bias"])
    y_ca = _ca_fwd(proj, bias, H, 3 * H)
    comm.stage("attention", y_sb, y_ca)
    a_sb = _mm(y_sb, comm.weight("w_sb_out", y_ca), mode="nn", tm=tm, tn=nb_o, tk=W, out_dtype=F32, b_blocked=True, after=comm.pending(), name="mm_sb_out")
    a_ca = _mm(y_ca, comm.weight("w_ca_out"), mode="nn", tm=tm, tn=nb_o, tk=W, out_dtype=F32, b_blocked=True, name="mm_ca_out")
    merged = _merge_fwd(proj, a_sb, a_ca, D, gate_col)
    x2 = _mm(merged, comm.weight("w_mix_out"), mode="nn", tm=tm, tn=tn, tk=D, out_dtype=F32, res=x, name="mm_mix")
    h2 = _norm_fwd(x2, g["g_ffn"], "norm2")
    gu = _mm(h2, comm.weight("w_ffn_in", h2), mode="nn", tm=tm, tn=nb_ff, tk=D, out_dtype=F32, b_blocked=True, name="mm_ffn_in")
    comm.stage("mm_ffn_in", gu)
    act = _swiglu_fwd(gu, comm.pending())
    F = act.shape[1]
    tkf = F // 2 if F % 256 == 0 else F
    x3 = _mm(act, comm.weight("w_ffn_out", act), mode="nn", tm=tm, tn=tn, tk=tkf, out_dtype=F32, res=x2, name="mm_ffn_out")
    h3 = _norm_fwd(x3, g["g_ple"], "norm3")
    zg = _mm(h3, comm.weight("w_ple_gate"), mode="nn", tm=tm, tn=tn, tk=D, out_dtype=F32, name="mm_ple_gate")
    pb = _cast_bf16(p, "cast_p")
    P = p.shape[1]
    pe = _mm(pb, comm.weight("w_ple_in"), mode="nn", tm=tm, tn=nb_p, tk=P, out_dtype=F32, b_blocked=True, name="mm_ple_in")
    loss, dx4, dpe, dzg, dg_final = _tail(x3, zg, pe, g["g_final"], target)

    tw = min(D, 512)
    DW = BF16
    comm.grad("w_ple_in", _mm(pb, dpe, mode="tn", tm=P, tn=nb_p, tk=T, out_dtype=DW, out_block=nb_p, name="mm_d_ple_in"))
    comm.grad("w_ple_gate", _mm(h3, dzg, mode="tn", tm=tw, tn=tn, tk=T, out_dtype=DW, name="mm_d_ple_gate"))
    dh3 = _mm(dzg, comm.weight("w_ple_gate"), mode="nt", tm=tm, tn=tn, tk=D, out_dtype=F32, after=comm.pending(), name="mm_dh3")
    dx3, dx3b, dg_ple = _norm_bwd(dh3, x3, g["g_ple"], dx4, "norm3_bwd")
    comm.grad("w_ffn_out", _mm(act, dx3b, mode="tn", tm=F // 4, tn=tn, tk=T, out_dtype=DW, name="mm_d_ffn_out"))
    dact = _mm(dx3b, comm.weight("w_ffn_out"), mode="nt", tm=tm, tn=tkf, tk=D, out_dtype=F32, after=comm.pending(), name="mm_dact")
    dgu = _swiglu_bwd(dact, gu)
    comm.grad("w_ffn_in", _mm(h2, dgu, mode="tn", tm=tw, tn=nb_ff, tk=T, out_dtype=DW, out_block=nb_ff, name="mm_d_ffn_in"))
    dh2 = _mm(dgu, comm.weight("w_ffn_in"), mode="nt", tm=tm, tn=tn, tk=nb_ff, out_dtype=F32, b_blocked=True, after=comm.pending(), name="mm_dh2")
    dx2, dx2b, dg_ffn = _norm_bwd(dh2, x2, g["g_ffn"], dx3, "norm2_bwd")
    dmerged = _mm(dx2b, comm.weight("w_mix_out"), mode="nt", tm=tm, tn=tn, tk=D, out_dtype=F32, name="mm_dmerged")
    da_sb, da_ca, dgate_sb, dgate_ca = _merge_bwd(dmerged, proj, a_sb, a_ca, D, gate_col)
    comm.grad("w_mix_out", _mm(merged, dx2b, mode="tn", tm=tw, tn=tn, tk=T, out_dtype=DW, name="mm_d_mix"))
    comm.grad("w_sb_out", _mm(y_sb, da_sb, mode="tn", tm=min(W, 512), tn=nb_o, tk=T, out_dtype=DW, out_block=nb_o, name="mm_d_sb_out"))
    comm.grad("w_ca_out", _mm(y_ca, da_ca, mode="tn", tm=min(W, 512), tn=nb_o, tk=T, out_dtype=DW, out_block=nb_o, name="mm_d_ca_out"))
    dy_sb = _mm(da_sb, comm.weight("w_sb_out"), mode="nt", tm=tm, tn=W, tk=nb_o, out_dtype=F32, b_blocked=True, after=comm.pending(), name="mm_dy_sb")
    dy_ca = _mm(da_ca, comm.weight("w_ca_out"), mode="nt", tm=tm, tn=W, tk=nb_o, out_dtype=F32, b_blocked=True, name="mm_dy_ca")
    dq_sb, dk_sb, dv_sb = _sb_bwd(proj, dy_sb, H)
    dq_ca, dk_ca, dv_ca, dbias = _ca_bwd(proj, bias, dy_ca, H, 3 * H)
    d_rel = _bias_reduce(dbias)[:, :N_REL]
    dproj = jnp.concatenate([dq_sb, dk_sb, dv_sb, dq_ca, dk_ca, dv_ca, dgate_sb, dgate_ca], axis=1)
    comm.grad("w_in", _mm(h1, dproj, mode="tn", tm=tw, tn=nb_in, tk=T, out_dtype=DW, out_block=nb_in, name="mm_d_in"))
    dh1 = _mm(dproj, comm.weight("w_in"), mode="nt", tm=tm, tn=tn, tk=nb_in, out_dtype=F32, b_blocked=True, after=comm.pending(), name="mm_dh1")
    grad_x, _, dg_mix = _norm_bwd(dh1, x, g["g_mix"], dx2, "norm1_bwd")
    small = dict(g_mix=dg_mix, g_ffn=dg_ffn, g_ple=dg_ple, g_final=dg_final, rel_bias=d_rel)
    return loss, grad_x, small


def _position():
    x, y, c = lax.axis_index("x"), lax.axis_index("y"), lax.axis_index("c")
    return x, y, c


def _block_of(px, py, pc):
    return 4 * px + 2 * py + pc


def _flip(pos, k):
    x, y, c = pos
    return (1 - x if k & 4 else x, 1 - y if k & 2 else y, 1 - c if k & 1 else c)


HBM = pl.BlockSpec(memory_space=pltpu.HBM)
SEM = pl.BlockSpec(memory_space=pltpu.SEMAPHORE)
VMEM_SPEC = pl.BlockSpec(memory_space=pltpu.VMEM)
EFFECT = pltpu.SideEffectType.DATAFLOW_SIDE_EFFECTING
TOKEN = jax.ShapeDtypeStruct((8, 128), F32)


def _hbm(a):
    return pltpu.HBM(a.shape, a.dtype)


def _landing(shape, dtype):
    return pltpu.with_memory_space_constraint(lax.empty(shape, dtype), pltpu.HBM)


def _gather_start(lands, after, name):
    n = len(lands)

    def body(*refs):
        ins = refs[:n]
        send, recv = refs[n + 1], refs[n + 2]
        token = refs[-1]
        x, y, c = _position()
        mine = _block_of(x, y, c)
        peers = [(x, y, 1 - c), (1 - x, y, c), (x, 1 - y, c), (1 - x, 1 - y, c)]
        for wi in range(n):
            for k, peer in enumerate(peers):
                pltpu.make_async_remote_copy(
                    src_ref=ins[wi].at[mine], dst_ref=ins[wi].at[mine], send_sem=send.at[4 * wi + k], recv_sem=recv.at[4 * wi + k],
                    device_id=peer, device_id_type=MESH).start()
        token[...] = jnp.zeros_like(token)

    outs = pl.pallas_call(
        body, name=name, in_specs=[HBM] * n + [ANY], out_specs=(SEM, SEM, *[HBM] * n, VMEM_SPEC),
        out_shape=(pltpu.SemaphoreType.DMA((4 * n,)), pltpu.SemaphoreType.DMA((4 * n,)), *[_hbm(a) for a in lands], TOKEN),
        input_output_aliases={i: 2 + i for i in range(n)},
        compiler_params=pltpu.CompilerParams(has_side_effects=EFFECT))(*[pltpu.with_memory_space_constraint(a, pltpu.HBM) for a in lands], after)
    return outs[0], outs[1], list(outs[2:2 + n]), outs[-1]


def _gather_forward(lands, send0, recv0, after, name):
    n = len(lands)

    def body(*refs):
        ins = refs[:n]
        send0, recv0 = refs[n], refs[n + 1]
        send1, recv1 = refs[n + 2 + len(after)], refs[n + 3 + len(after)]
        token = refs[-1]
        x, y, c = _position()
        chips = [(1 - x, y), (x, 1 - y), (1 - x, 1 - y)]
        for wi in range(n):
            for j, chip in enumerate(chips):
                rows = ins[wi].at[_block_of(*chip, c)]
                pltpu.make_async_remote_copy(
                    src_ref=rows, dst_ref=rows, send_sem=send0.at[4 * wi + 1 + j], recv_sem=recv0.at[4 * wi + 1 + j],
                    device_id=(*chip, c), device_id_type=MESH).wait_recv()
                pltpu.make_async_remote_copy(
                    src_ref=rows, dst_ref=rows, send_sem=send1.at[3 * wi + j], recv_sem=recv1.at[3 * wi + j],
                    device_id=(x, y, 1 - c), device_id_type=MESH).start()
        token[...] = jnp.zeros_like(token)

    outs = pl.pallas_call(
        body, name=name, in_specs=[HBM] * n + [SEM, SEM] + [ANY] * len(after), out_specs=(SEM, SEM, *[HBM] * n, VMEM_SPEC),
        out_shape=(pltpu.SemaphoreType.DMA((3 * n,)), pltpu.SemaphoreType.DMA((3 * n,)), *[_hbm(a) for a in lands], TOKEN),
        input_output_aliases={i: 2 + i for i in range(n)},
        compiler_params=pltpu.CompilerParams(has_side_effects=EFFECT))(*lands, send0, recv0, *after)
    return outs[0], outs[1], list(outs[2:2 + n]), outs[-1]


def _gather_wait(lands, send0, recv0, send1, recv1, after, name):
    n = len(lands)

    def body(*refs):
        ins = refs[:n]
        send0, recv0, send1, recv1 = refs[n:n + 4]
        x, y, c = _position()
        mine = _block_of(x, y, c)
        sibling = (x, y, 1 - c)
        peers = [sibling, (1 - x, y, c), (x, 1 - y, c), (1 - x, 1 - y, c)]
        chips = [(1 - x, y), (x, 1 - y), (1 - x, 1 - y)]
        for wi in range(n):
            own = ins[wi].at[mine]
            for k, peer in enumerate(peers):
                pltpu.make_async_remote_copy(src_ref=own, dst_ref=own, send_sem=send0.at[4 * wi + k], recv_sem=recv0.at[4 * wi + k],
                                             device_id=peer, device_id_type=MESH).wait_send()
            theirs = ins[wi].at[_block_of(*sibling)]
            pltpu.make_async_remote_copy(src_ref=theirs, dst_ref=theirs, send_sem=send0.at[4 * wi], recv_sem=recv0.at[4 * wi],
                                         device_id=sibling, device_id_type=MESH).wait_recv()
            for j, chip in enumerate(chips):
                sent = ins[wi].at[_block_of(*chip, c)]
                got = ins[wi].at[_block_of(*chip, 1 - c)]
                pltpu.make_async_remote_copy(src_ref=sent, dst_ref=sent, send_sem=send1.at[3 * wi + j], recv_sem=recv1.at[3 * wi + j],
                                             device_id=sibling, device_id_type=MESH).wait_send()
                pltpu.make_async_remote_copy(src_ref=got, dst_ref=got, send_sem=send1.at[3 * wi + j], recv_sem=recv1.at[3 * wi + j],
                                             device_id=sibling, device_id_type=MESH).wait_recv()

    outs = pl.pallas_call(
        body, name=name, in_specs=[HBM] * n + [SEM] * 4 + [ANY], out_specs=tuple([HBM] * n),
        out_shape=tuple(_hbm(a) for a in lands), input_output_aliases={i: i for i in range(n)},
        compiler_params=pltpu.CompilerParams(has_side_effects=EFFECT))(*lands, send0, recv0, send1, recv1, after)
    return list(outs)


def _plan_direct(me):
    return [(_block_of(*_flip(me, k)), k - 1, _flip(me, k)) for k in range(1, N_DEV)]


def _plan_sibling(me):
    x, y, c = me
    return [(_block_of(ci // 2, ci % 2, 1 - c), ci, (x, y, 1 - c)) for ci in range(4)]


def _plan_chips(me):
    x, y, c = me
    out = []
    for k in range(1, 4):
        px, py = (1 - x if k & 2 else x), (1 - y if k & 1 else y)
        out.append((2 * px + py, k - 1, (px, py, c)))
    return out


def _exchange_start(blocks, plan, name):
    n = len(blocks)
    slots = len(plan((0, 0, 0)))

    def body(*refs):
        srcs, lands = refs[:n], refs[n:2 * n]
        send, recv = refs[2 * n], refs[2 * n + 1]
        token = refs[-1]
        for wi in range(n):
            for block, slot, peer in plan(_position()):
                pltpu.make_async_remote_copy(
                    src_ref=srcs[wi].at[block], dst_ref=lands[wi].at[slot], send_sem=send.at[slots * wi + slot],
                    recv_sem=recv.at[slots * wi + slot], device_id=peer, device_id_type=MESH).start()
        token[...] = jnp.zeros_like(token)

    zones = [_landing((slots,) + b.shape[1:], b.dtype) for b in blocks]
    outs = pl.pallas_call(
        body, name=name, in_specs=[HBM] * (2 * n), out_specs=(SEM, SEM, *[HBM] * (2 * n), VMEM_SPEC),
        out_shape=(pltpu.SemaphoreType.DMA((slots * n,)), pltpu.SemaphoreType.DMA((slots * n,)), *[_hbm(a) for a in blocks],
                   *[_hbm(z) for z in zones], TOKEN),
        input_output_aliases={i: 2 + i for i in range(2 * n)},
        compiler_params=pltpu.CompilerParams(has_side_effects=EFFECT))(
            *[pltpu.with_memory_space_constraint(b, pltpu.HBM) for b in blocks], *zones)
    return outs[0], outs[1], list(outs[2:2 + n]), list(outs[2 + n:2 + 2 * n]), outs[-1]


def _exchange_wait(groups, plan, after, name):
    flat, counts = [], []
    for send, recv, blocks, zones in groups:
        flat += [*blocks, *zones, send, recv]
        counts.append(len(blocks))
    slots = len(plan((0, 0, 0)))

    def body(*refs):
        pos = 0
        for n in counts:
            srcs, lands = refs[pos:pos + n], refs[pos + n:pos + 2 * n]
            send, recv = refs[pos + 2 * n], refs[pos + 2 * n + 1]
            pos += 2 * n + 2
            for wi in range(n):
                for block, slot, peer in plan(_position()):
                    cp = pltpu.make_async_remote_copy(
                        src_ref=srcs[wi].at[block], dst_ref=lands[wi].at[slot], send_sem=send.at[slots * wi + slot],
                        recv_sem=recv.at[slots * wi + slot], device_id=peer, device_id_type=MESH)
                    cp.wait_send()
                    cp.wait_recv()

    in_specs, out_specs, out_shape, aliases = [], [], [], {}
    i = 0
    for n, (send, recv, blocks, zones) in zip(counts, groups):
        for a in (*blocks, *zones):
            aliases[i] = len(out_shape)
            in_specs.append(HBM)
            out_specs.append(HBM)
            out_shape.append(_hbm(a))
            i += 1
        in_specs += [SEM, SEM]
        i += 2
    outs = pl.pallas_call(
        body, name=name, in_specs=in_specs + [ANY], out_specs=tuple(out_specs), out_shape=tuple(out_shape),
        input_output_aliases=aliases, compiler_params=pltpu.CompilerParams(has_side_effects=EFFECT))(*flat, after)
    res, pos = [], 0
    for n in counts:
        res.append((list(outs[pos:pos + n]), list(outs[pos + n:pos + 2 * n])))
        pos += 2 * n
    return res


def _sibling_sum(blocks, zone, core, name):
    _, R, C = zone.shape
    rt = 128 if R % 128 == 0 else 64
    assert R % rt == 0

    def body(core_ref, own_ref, z_ref, o_ref):
        o_ref[...] = (own_ref[...].astype(F32) + z_ref[...].astype(F32)).astype(o_ref.dtype)

    grid_spec = pltpu.PrefetchScalarGridSpec(
        num_scalar_prefetch=1, grid=(4, R // rt),
        in_specs=[pl.BlockSpec((None, rt, C), lambda ci, i, core_ref: (2 * ci + core_ref[0], i, 0)),
                  pl.BlockSpec((None, rt, C), lambda ci, i, core_ref: (ci, i, 0))],
        out_specs=pl.BlockSpec((None, rt, C), lambda ci, i, core_ref: (ci, i, 0)))
    return pl.pallas_call(body, grid_spec=grid_spec, out_shape=jax.ShapeDtypeStruct(zone.shape, zone.dtype),
                          compiler_params=_params("parallel", "parallel"), name=name)(core, blocks, zone)


def _adamw(w, g, m, v):
    m = ADAM_B1 * m + (1.0 - ADAM_B1) * g
    v = ADAM_B2 * v + (1.0 - ADAM_B2) * (g * g)
    m_hat = m / (1.0 - ADAM_B1 ** ADAM_STEP)
    v_hat = v / (1.0 - ADAM_B2 ** ADAM_STEP)
    delta = -ADAM_LR * (m_hat / (jnp.sqrt(v_hat) + ADAM_EPS) + ADAM_WD * w)
    return delta, m, v


def _reduce_adamw(blocks, zone, mine, w, m, v, name):
    R, C = w.shape
    rt = 128 if R % 128 == 0 else 64
    assert R % rt == 0

    def body(mine_ref, own_ref, z_ref, w_ref, m_ref, v_ref, g_out, d_out, m_out, v_out):
        g = own_ref[...].astype(F32)
        for s in range(zone.shape[0]):
            g = g + z_ref[s].astype(F32)
        delta, m2, v2 = _adamw(w_ref[...], g, m_ref[...], v_ref[...])
        g_out[...] = g
        d_out[...] = delta
        m_out[...] = m2
        v_out[...] = v2

    spec = pl.BlockSpec((rt, C), lambda i, mine_ref: (i, 0))
    grid_spec = pltpu.PrefetchScalarGridSpec(
        num_scalar_prefetch=1, grid=(R // rt,),
        in_specs=[pl.BlockSpec((None, rt, C), lambda i, mine_ref: (mine_ref[0], i, 0)),
                  pl.BlockSpec((zone.shape[0], rt, C), lambda i, mine_ref: (0, i, 0)), spec, spec, spec],
        out_specs=[spec] * 4)
    return pl.pallas_call(body, grid_spec=grid_spec, out_shape=[jax.ShapeDtypeStruct((R, C), F32)] * 4,
                          compiler_params=_params("parallel"), name=name)(mine, blocks, zone, w, m, v)


def _small_step(part, w, m, v, after):
    R, C = part.shape

    def body(part_ref, w_ref, m_ref, v_ref, *rest):
        g_out, d_out, m_out, v_out, gath, send, recv = rest[len(after):]
        me = _position()
        gath[_block_of(*me)] = part_ref[...]

        def copy(k, slot):
            return pltpu.make_async_remote_copy(
                src_ref=part_ref, dst_ref=gath.at[slot], send_sem=send.at[k - 1], recv_sem=recv.at[k - 1],
                device_id=_flip(me, k), device_id_type=MESH)

        sent = [copy(k, _block_of(*me)) for k in range(1, N_DEV)]
        for cp in sent:
            cp.start()
        for k in range(1, N_DEV):
            copy(k, _block_of(*_flip(me, k))).wait_recv()
        for cp in sent:
            cp.wait_send()
        g = gath[0]
        for s in range(1, N_DEV):
            g = g + gath[s]
        delta, m2, v2 = _adamw(w_ref[...], g, m_ref[...], v_ref[...])
        g_out[...] = g
        d_out[...] = delta
        m_out[...] = m2
        v_out[...] = v2

    vm = pl.BlockSpec(memory_space=pltpu.VMEM)
    return pl.pallas_call(
        body, in_specs=[vm] * 4 + [ANY] * len(after), out_specs=[vm] * 4, out_shape=[jax.ShapeDtypeStruct((R, C), F32)] * 4,
        scratch_shapes=[pltpu.VMEM((N_DEV, R, C), F32), pltpu.SemaphoreType.DMA((7,)), pltpu.SemaphoreType.DMA((7,))],
        name="small_step")(part, w, m, v, *after)


COLUMN_SHARDED = ("w_in", "w_sb_out", "w_ca_out", "w_ffn_in", "w_ple_in")
ROW_SHARDED = ("w_mix_out", "w_ffn_out", "w_ple_gate")
BIG = COLUMN_SHARDED + ROW_SHARDED
SMALL = ("g_mix", "g_ffn", "g_ple", "g_final", "rel_bias")
WEIGHTS = ("w_in", "w_sb_out", "w_ca_out", "w_mix_out", "rel_bias", "g_mix", "g_ffn", "g_ple", "g_final",
           "w_ffn_in", "w_ffn_out", "w_ple_in", "w_ple_gate")


def _pack_small(t, D):
    rows = [t[n].reshape(1, D) for n in SMALL[:4]]
    rb = t["rel_bias"].reshape(1, -1)
    rows.append(jnp.pad(rb, ((0, 0), (0, D - rb.shape[1]))))
    return jnp.concatenate(rows + [jnp.zeros((8 - len(rows), D), F32)], axis=0)


def _unpack_small(a, like):
    out = {n: a[i].reshape(like[n].shape) for i, n in enumerate(SMALL[:4])}
    out["rel_bias"] = a[4, :like["rel_bias"].size].reshape(like["rel_bias"].shape)
    return out


GATHER_GROUPS = (("w_in",), ("w_sb_out", "w_ca_out", "w_mix_out"), ("w_ffn_in",), ("w_ffn_out", "w_ple_gate", "w_ple_in"))
FORWARD_AFTER = ("norm1", "mm_in", "attention", "mm_ffn_in")
GRAD_GROUPS = (("w_ple_in", "w_ple_gate"), ("w_ffn_out",), ("w_ffn_in",), ("w_mix_out", "w_sb_out", "w_ca_out"), ("w_in",))


class _Exchange:
    def __init__(self, shards):
        me = _position()
        self.mine = _block_of(*me)
        self.chip = jnp.reshape(2 * me[0] + me[1], (1,)).astype(jnp.int32)
        self.core = jnp.reshape(me[2], (1,)).astype(jnp.int32)
        self.device = jnp.reshape(self.mine, (1,)).astype(jnp.int32)
        self.shapes = {n: ((N_DEV,) + s.shape if n in COLUMN_SHARDED else (N_DEV * s.shape[0], s.shape[1])) for n, s in shards.items()}
        self.tokens = []
        self.ready = {}
        self.gathers = []
        for gi, names in enumerate(GATHER_GROUPS):
            lands = [lax.dynamic_update_slice(lax.empty((N_DEV,) + shards[n].shape, BF16), shards[n][None], (self.mine, 0, 0))
                     for n in names]
            behind = self.tokens[-1] if self.tokens else shards[names[0]]
            send0, recv0, lands, token = _gather_start(lands, behind, f"gather_start_{gi}")
            self.tokens.append(token)
            self.gathers.append(dict(names=names, lands=lands, sems=(send0, recv0), token=token))
        self.grads = {}
        self.exchanges = []

    def pending(self):
        tokens, self.tokens = self.tokens, []
        return tokens

    def stage(self, tag, *made):
        gi = FORWARD_AFTER.index(tag)
        gth = self.gathers[gi]
        send1, recv1, lands, token = _gather_forward(gth["lands"], *gth["sems"], made + tuple(self.tokens), f"gather_forward_{gi}")
        gth.update(lands=lands, sems=gth["sems"] + (send1, recv1), token=token)
        self.tokens.append(token)

    def weight(self, name, after=None):
        if name not in self.ready:
            gi = next(i for i, names in enumerate(GATHER_GROUPS) if name in names)
            gth = self.gathers[gi]
            for n, a in zip(gth["names"], _gather_wait(gth["lands"], *gth["sems"], gth["token"] if after is None else after, f"gather_wait_{gi}")):
                self.ready[n] = a if n in COLUMN_SHARDED else a.reshape(self.shapes[n])
        return self.ready[name]

    def grad(self, name, blocks):
        self.grads[name] = blocks if name in COLUMN_SHARDED else blocks.reshape((N_DEV, -1, blocks.shape[-1]))
        names = next(names for names in GRAD_GROUPS if name in names)
        if not all(n in self.grads for n in names):
            return
        blocks = [self.grads[n] for n in names]
        last = names == GRAD_GROUPS[-1]
        if last:
            send, recv, blocks, zones, token = _exchange_start(blocks, _plan_sibling, "pair_start_" + names[0])
            (blocks, zones), = _exchange_wait([(send, recv, blocks, zones)], _plan_sibling, token, "pair_wait_" + names[0])
            blocks = [_sibling_sum(b, z, self.core, "pair_sum_" + n) for n, b, z in zip(names, blocks, zones)]
        plan = _plan_chips if last else _plan_direct
        send, recv, blocks, zones, token = _exchange_start(blocks, plan, "exchange_start_" + names[0])
        self.exchanges.append(dict(names=names, state=(send, recv, blocks, zones), plan=plan, own=self.chip if last else self.device))
        self.tokens.append(token)

    def collect(self, which, after, name):
        sel = [e for e in self.exchanges if GRAD_GROUPS.index(e["names"]) in which]
        out = {}
        for e, (blocks, zones) in zip(sel, _exchange_wait([e["state"] for e in sel], sel[0]["plan"], after, name)):
            out.update({n: (b, e["own"], z) for n, b, z in zip(e["names"], blocks, zones)})
        return out


def kernel(x, p, w_in, w_sb_out, w_ca_out, w_mix_out, rel_bias, g_mix, g_ffn, g_ple, g_final, w_ffn_in, w_ffn_out, w_ple_in, w_ple_gate, loss_target, m_w_in, m_w_sb_out, m_w_ca_out, m_w_mix_out, m_rel_bias, m_g_mix, m_g_ffn, m_g_ple, m_g_final, m_w_ffn_in, m_w_ffn_out, m_w_ple_in, m_w_ple_gate, v_w_in, v_w_sb_out, v_w_ca_out, v_w_mix_out, v_rel_bias, v_g_mix, v_g_ffn, v_g_ple, v_g_final, v_w_ffn_in, v_w_ffn_out, v_w_ple_in, v_w_ple_gate):
    wts = dict(w_in=w_in, w_sb_out=w_sb_out, w_ca_out=w_ca_out, w_mix_out=w_mix_out, rel_bias=rel_bias, g_mix=g_mix, g_ffn=g_ffn,
               g_ple=g_ple, g_final=g_final, w_ffn_in=w_ffn_in, w_ffn_out=w_ffn_out, w_ple_in=w_ple_in, w_ple_gate=w_ple_gate)
    mom = dict(w_in=m_w_in, w_sb_out=m_w_sb_out, w_ca_out=m_w_ca_out, w_mix_out=m_w_mix_out, rel_bias=m_rel_bias, g_mix=m_g_mix,
               g_ffn=m_g_ffn, g_ple=m_g_ple, g_final=m_g_final, w_ffn_in=m_w_ffn_in, w_ffn_out=m_w_ffn_out, w_ple_in=m_w_ple_in,
               w_ple_gate=m_w_ple_gate)
    var = dict(w_in=v_w_in, w_sb_out=v_w_sb_out, w_ca_out=v_w_ca_out, w_mix_out=v_w_mix_out, rel_bias=v_rel_bias, g_mix=v_g_mix,
               g_ffn=v_g_ffn, g_ple=v_g_ple, g_final=v_g_final, w_ffn_in=v_w_ffn_in, w_ffn_out=v_w_ffn_out, w_ple_in=v_w_ple_in,
               w_ple_gate=v_w_ple_gate)
    T, D = x.shape[1], x.shape[2]
    shard = {n: wts[n].reshape(wts[n].shape[-2:]) for n in BIG}
    comm = _Exchange({n: _cast_bf16(shard[n], "cast_" + n) for n in BIG})
    g = dict(g_mix=g_mix.reshape(1, D), g_ffn=g_ffn.reshape(1, D), g_ple=g_ple.reshape(1, D), g_final=g_final.reshape(1, D),
             rel_bias=rel_bias.reshape(rel_bias.shape[-2:]))

    loss, grad_x, dsmall = _local_step(x.reshape(T, D), p.reshape(T, -1), loss_target.reshape(T, D), comm, g)
    loss = lax.psum(loss[0, 0], ("x", "y", "c"))

    grad, delta, new_m, new_v = {}, {}, {}, {}

    def update(parts):
        done = []
        for n, (blocks, own, zone) in parts.items():
            outs = _reduce_adamw(blocks, zone, own, shard[n], mom[n].reshape(shard[n].shape), var[n].reshape(shard[n].shape), "adamw_" + n)
            grad[n], delta[n], new_m[n], new_v[n] = [o.reshape(wts[n].shape) for o in outs]
            done.append(outs[0])
        return done

    done = update(comm.collect(range(len(GRAD_GROUPS) - 1), grad_x, "exchange_wait_rest"))
    outs = _small_step(_pack_small(dsmall, D), _pack_small(wts, D), _pack_small(mom, D), _pack_small(var, D), done)
    for dst, a in zip((grad, delta, new_m, new_v), outs):
        dst.update(_unpack_small(a, wts))
    update(comm.collect([len(GRAD_GROUPS) - 1], outs[0], "exchange_wait_w_in"))

    return (loss, grad_x.reshape(x.shape), *[grad[n] for n in WEIGHTS], *[delta[n] for n in WEIGHTS],
            *[new_m[n] for n in WEIGHTS], *[new_v[n] for n in WEIGHTS])
```

```python
import functools

import jax
import jax.numpy as jnp
from jax import lax
from jax.experimental import pallas as pl
from jax.experimental.pallas import tpu as pltpu

F32, BF16 = jnp.float32, jnp.bfloat16

N_DEV = 8
HEAD_DIM = 128
CHUNK = 64
LEFT_CHUNKS = 8
REL_CLIP = 128
N_REL = REL_CLIP + CHUNK
PAIR = 2 * CHUNK
PBAND = (LEFT_CHUNKS + 2) * CHUNK
PAD = LEFT_CHUNKS * CHUNK
SB_BLOCK = 256
ROWS = 256
EPS = 1e-6
NEG = -1e30
SCALE = HEAD_DIM ** -0.5
VMEM_LIMIT_BYTES = 56 * 1024 * 1024

ADAM_LR, ADAM_B1, ADAM_B2, ADAM_EPS, ADAM_WD, ADAM_STEP = 0.001, 0.9, 0.999, 1e-08, 0.01, 10

ANY = pl.BlockSpec(memory_space=pl.ANY)
NN = (((1,), (0,)), ((), ()))
NT = (((1,), (1,)), ((), ()))
TN = (((0,), (0,)), ((), ()))
MESH = pl.DeviceIdType.MESH


def _params(*sem):
    return pltpu.CompilerParams(dimension_semantics=sem or None, vmem_limit_bytes=VMEM_LIMIT_BYTES)


def _dot(a, b, dims=NN):
    return lax.dot_general(a, b, dims, preferred_element_type=F32)


def _mm(a, b, *, mode, tm, tn, tk, out_dtype, name, b_blocked=False, out_block=None, res=None, after=()):
    bg = og = 1
    if mode == "nn":
        M, K = a.shape
        a_spec = pl.BlockSpec((tm, tk), lambda i, j, k: (i, k))
        if b_blocked:
            G, _, nb = b.shape
            N = G * nb
            if tn > nb:
                bg = tn // nb
                assert tn % nb == 0
                b_spec = pl.BlockSpec((bg, tk, nb), lambda i, j, k: (j, k, 0))
            else:
                per = nb // tn
                assert nb % tn == 0
                b_spec = pl.BlockSpec((None, tk, tn), lambda i, j, k: (j // per, k, j % per))
        else:
            N = b.shape[1]
            b_spec = pl.BlockSpec((tk, tn), lambda i, j, k: (k, j))
        dims = NN
    elif mode == "nt":
        M, K = a.shape
        a_spec = pl.BlockSpec((tm, tk), lambda i, j, k: (i, k))
        if b_blocked:
            G, N, nb = b.shape
            assert K == G * nb
            if tk > nb:
                bg = tk // nb
                assert tk % nb == 0
                b_spec = pl.BlockSpec((bg, tn, nb), lambda i, j, k: (k, j, 0))
            else:
                per = nb // tk
                assert nb % tk == 0
                b_spec = pl.BlockSpec((None, tn, tk), lambda i, j, k: (k // per, j, k % per))
        else:
            N = b.shape[0]
            b_spec = pl.BlockSpec((tn, tk), lambda i, j, k: (j, k))
        dims = NT
    else:
        K, M = a.shape
        N = b.shape[1]
        a_spec = pl.BlockSpec((tk, tm), lambda i, j, k: (k, i))
        b_spec = pl.BlockSpec((tk, tn), lambda i, j, k: (k, j))
        dims = TN
    assert M % tm == 0 and N % tn == 0 and K % tk == 0, (name, M, N, K, tm, tn, tk)
    nk = K // tk
    if out_block is None:
        out_shape = jax.ShapeDtypeStruct((M, N), out_dtype)
        o_spec = pl.BlockSpec((tm, tn), lambda i, j, k: (i, j))
    else:
        out_shape = jax.ShapeDtypeStruct((N // out_block, M, out_block), out_dtype)
        if tn > out_block:
            og = tn // out_block
            assert tn % out_block == 0
            o_spec = pl.BlockSpec((og, tm, out_block), lambda i, j, k: (j, i, 0))
        else:
            per_o = out_block // tn
            assert out_block % tn == 0
            o_spec = pl.BlockSpec((None, tm, tn), lambda i, j, k: (j // per_o, i, j % per_o))
    in_specs = [a_spec, b_spec]
    args = [a, b]
    if res is not None:
        in_specs.append(pl.BlockSpec((tm, tn), lambda i, j, k: (i, j)))
        args.append(res)
    n_in = len(args) + len(after)

    def product(a_ref, b_ref):
        if bg == 1:
            return _dot(a_ref[...], b_ref[...], dims)
        nb = b_ref.shape[2]
        if mode == "nn":
            return jnp.concatenate([_dot(a_ref[...], b_ref[g], dims) for g in range(bg)], axis=1)
        return sum(_dot(a_ref[:, g * nb:(g + 1) * nb], b_ref[g], dims) for g in range(bg))

    def body(*refs):
        a_ref, b_ref = refs[0], refs[1]
        r_ref = refs[2] if res is not None else None
        o_ref = refs[n_in]

        def finish(acc):
            if r_ref is not None:
                acc = acc + r_ref[...]
            if og == 1:
                o_ref[...] = acc.astype(o_ref.dtype)
            else:
                for g in range(og):
                    o_ref[g] = acc[:, g * out_block:(g + 1) * out_block].astype(o_ref.dtype)

        if nk == 1:
            finish(product(a_ref, b_ref))
        else:
            acc_ref = refs[-1]
            k = pl.program_id(2)

            @pl.when(k == 0)
            def _():
                acc_ref[...] = jnp.zeros_like(acc_ref)

            acc_ref[...] += product(a_ref, b_ref)

            @pl.when(k == nk - 1)
            def _():
                finish(acc_ref[...])

    return pl.pallas_call(
        body, grid=(M // tm, N // tn, nk), in_specs=in_specs + [ANY] * len(after), out_specs=o_spec, out_shape=out_shape,
        scratch_shapes=[] if nk == 1 else [pltpu.VMEM((tm, tn), F32)],
        compiler_params=_params("parallel", "parallel", "arbitrary"), name=name)(*args, *after)


def _row_spec(d, col=0):
    return pl.BlockSpec((ROWS, d), lambda i: (i, col))


def _vec_spec(d):
    return pl.BlockSpec((1, d), lambda i: (0, 0))


def _rms(x):
    return lax.rsqrt(jnp.mean(x * x, axis=-1, keepdims=True) + EPS)


def _norm_fwd(x, g, name):
    T, D = x.shape

    def body(x_ref, g_ref, h_ref):
        xv = x_ref[...]
        h_ref[...] = (xv * _rms(xv) * g_ref[...]).astype(BF16)

    return pl.pallas_call(body, grid=(T // ROWS,), in_specs=[_row_spec(D), _vec_spec(D)], out_specs=_row_spec(D),
                          out_shape=jax.ShapeDtypeStruct((T, D), BF16), compiler_params=_params("parallel"), name=name)(x, g)


def _norm_bwd_math(dh, xv, gv):
    r = _rms(xv)
    xhat = xv * r
    dxhat = dh * gv
    dx = r * (dxhat - xhat * jnp.mean(dxhat * xhat, axis=-1, keepdims=True))
    dg = jnp.sum(dh * xhat, axis=0, keepdims=True)
    return dx, dg


def _norm_bwd(dh, x, g, dres, name):
    T, D = x.shape

    def body(dh_ref, x_ref, g_ref, dres_ref, dx_ref, dxb_ref, dg_ref):
        dx, dg = _norm_bwd_math(dh_ref[...], x_ref[...], g_ref[...])
        dx = dx + dres_ref[...]
        dx_ref[...] = dx
        dxb_ref[...] = dx.astype(BF16)

        @pl.when(pl.program_id(0) == 0)
        def _():
            dg_ref[...] = jnp.zeros_like(dg_ref)

        dg_ref[...] += dg

    return pl.pallas_call(
        body, grid=(T // ROWS,), in_specs=[_row_spec(D), _row_spec(D), _vec_spec(D), _row_spec(D)],
        out_specs=[_row_spec(D), _row_spec(D), _vec_spec(D)],
        out_shape=[jax.ShapeDtypeStruct((T, D), F32), jax.ShapeDtypeStruct((T, D), BF16), jax.ShapeDtypeStruct((1, D), F32)],
        compiler_params=_params("arbitrary"), name=name)(dh, x, g, dres)


def _merge_fwd(proj, a_sb, a_ca, D, gate_col):
    T = proj.shape[0]

    def body(gs_ref, gc_ref, a_ref, b_ref, o_ref):
        o_ref[...] = (jax.nn.sigmoid(gs_ref[...]) * a_ref[...] + jax.nn.sigmoid(gc_ref[...]) * b_ref[...]).astype(BF16)

    return pl.pallas_call(
        body, grid=(T // ROWS,), in_specs=[_row_spec(D, gate_col), _row_spec(D, gate_col + 1), _row_spec(D), _row_spec(D)],
        out_specs=_row_spec(D), out_shape=jax.ShapeDtypeStruct((T, D), BF16),
        compiler_params=_params("parallel"), name="merge_fwd")(proj, proj, a_sb, a_ca)


def _merge_bwd(dm, proj, a_sb, a_ca, D, gate_col):
    T = proj.shape[0]

    def body(dm_ref, gs_ref, gc_ref, a_ref, b_ref, da_ref, db_ref, dgs_ref, dgc_ref):
        dmv = dm_ref[...]
        ss, sc = jax.nn.sigmoid(gs_ref[...]), jax.nn.sigmoid(gc_ref[...])
        da_ref[...] = (dmv * ss).astype(BF16)
        db_ref[...] = (dmv * sc).astype(BF16)
        dgs_ref[...] = (dmv * a_ref[...] * ss * (1.0 - ss)).astype(BF16)
        dgc_ref[...] = (dmv * b_ref[...] * sc * (1.0 - sc)).astype(BF16)

    return pl.pallas_call(
        body, grid=(T // ROWS,),
        in_specs=[_row_spec(D), _row_spec(D, gate_col), _row_spec(D, gate_col + 1), _row_spec(D), _row_spec(D)],
        out_specs=[_row_spec(D)] * 4, out_shape=[jax.ShapeDtypeStruct((T, D), BF16)] * 4,
        compiler_params=_params("parallel"), name="merge_bwd")(dm, proj, proj, a_sb, a_ca)


def _swiglu_fwd(gu, after=()):
    T, F2 = gu.shape
    F = F2 // 2

    def body(g_ref, u_ref, *rest):
        gv = g_ref[...]
        rest[-1][...] = (gv * jax.nn.sigmoid(gv) * u_ref[...]).astype(BF16)

    return pl.pallas_call(body, grid=(T // ROWS,), in_specs=[_row_spec(F, 0), _row_spec(F, 1)] + [ANY] * len(after), out_specs=_row_spec(F),
                          out_shape=jax.ShapeDtypeStruct((T, F), BF16), compiler_params=_params("parallel"), name="swiglu_fwd")(gu, gu, *after)


def _swiglu_bwd(dact, gu):
    T, F2 = gu.shape
    F = F2 // 2

    def body(d_ref, g_ref, u_ref, o_ref):
        dv, gv, uv = d_ref[...], g_ref[...], u_ref[...]
        s = jax.nn.sigmoid(gv)
        o_ref[:, 0:F] = (dv * uv * s * (1.0 + gv * (1.0 - s))).astype(BF16)
        o_ref[:, F:F2] = (dv * gv * s).astype(BF16)

    return pl.pallas_call(body, grid=(T // ROWS,), in_specs=[_row_spec(F), _row_spec(F, 0), _row_spec(F, 1)], out_specs=_row_spec(F2),
                          out_shape=jax.ShapeDtypeStruct((T, F2), BF16), compiler_params=_params("parallel"), name="swiglu_bwd")(dact, gu, gu)


def _tail(x3, zg, pe, g_final, target):
    T, D = x3.shape

    def body(x3_ref, zg_ref, pe_ref, g_ref, t_ref, loss_ref, dx_ref, dpe_ref, dzg_ref, dg_ref):
        gate = jax.nn.sigmoid(zg_ref[...])
        pev = pe_ref[...]
        x4 = x3_ref[...] + gate * pev
        gv = g_ref[...]
        xhat = x4 * _rms(x4)
        err = xhat * gv - t_ref[...]
        part = 0.5 * jnp.sum(jnp.mean(err * err, axis=-1, keepdims=True), axis=0, keepdims=True)
        dx, dg = _norm_bwd_math(err * (1.0 / D), x4, gv)
        dx_ref[...] = dx
        dpe_ref[...] = (dx * gate).astype(BF16)
        dzg_ref[...] = (dx * pev * gate * (1.0 - gate)).astype(BF16)

        @pl.when(pl.program_id(0) == 0)
        def _():
            dg_ref[...] = jnp.zeros_like(dg_ref)
            loss_ref[...] = jnp.zeros_like(loss_ref)

        dg_ref[...] += dg
        loss_ref[...] += jnp.broadcast_to(part, loss_ref.shape)

    return pl.pallas_call(
        body, grid=(T // ROWS,), in_specs=[_row_spec(D), _row_spec(D), _row_spec(D), _vec_spec(D), _row_spec(D)],
        out_specs=[_vec_spec(128), _row_spec(D), _row_spec(D), _row_spec(D), _vec_spec(D)],
        out_shape=[jax.ShapeDtypeStruct((1, 128), F32), jax.ShapeDtypeStruct((T, D), F32), jax.ShapeDtypeStruct((T, D), BF16),
                   jax.ShapeDtypeStruct((T, D), BF16), jax.ShapeDtypeStruct((1, D), F32)],
        compiler_params=_params("arbitrary"), name="tail")(x3, zg, pe, g_final, target)


def _cast_bf16(x, name):
    R, C = x.shape
    rows = next(r for r in (ROWS, 128, 64, 32, 16) if R % r == 0)

    def body(x_ref, o_ref):
        o_ref[...] = x_ref[...].astype(BF16)

    spec = pl.BlockSpec((rows, C), lambda i: (i, 0))
    return pl.pallas_call(body, grid=(R // rows,), in_specs=[spec], out_specs=spec, out_shape=jax.ShapeDtypeStruct((R, C), BF16),
                          compiler_params=_params("parallel"), name=name)(x)


def _head_spec(T, col0, heads=1):
    return pl.BlockSpec((T, heads * HEAD_DIM), lambda h, *_: (0, col0 + h))


SB_HEADS = 4


def _triangle(n, right):
    j = lax.broadcasted_iota(jnp.int32, (n, n), 0)
    s = lax.broadcasted_iota(jnp.int32, (n, n), 1)
    return jnp.where((j > s) if right else (j < s), 1.0, 0.0).astype(BF16)


def _lane_scan(x, tri):
    hi = x.astype(BF16)
    lo = (x - hi.astype(F32)).astype(BF16)
    return _dot(hi, tri) + _dot(lo, tri)


def _head_cols(ref, rows, hh):
    return ref[rows, hh * HEAD_DIM:(hh + 1) * HEAD_DIM]


def _sb_tile(qv, kk, past, c_lk, tri):
    z = _dot(qv, kk, NT) * SCALE
    sp = jnp.log(1.0 + jnp.exp(-jnp.abs(z)))
    ls_pos = jnp.minimum(z, 0.0) - sp
    lk = jnp.minimum(-z, 0.0) - sp
    if past is not None:
        lk = jnp.where(past, lk, 0.0)
    right = c_lk + _lane_scan(lk, tri)
    a = jnp.exp(ls_pos + right)
    if past is not None:
        a = jnp.where(past, a, 0.0)
    return ls_pos, a, right[:, 0:1] + lk[:, 0:1]


def _sb_diagonal():
    B = SB_BLOCK
    return lax.broadcasted_iota(jnp.int32, (B, B), 1) < lax.broadcasted_iota(jnp.int32, (B, B), 0)


def _sb_rows(kb):
    return pl.ds(pl.multiple_of(kb * SB_BLOCK, SB_BLOCK), SB_BLOCK)


def _sb_fwd(proj, n_heads, after=()):
    T = proj.shape[0]
    B, HP = SB_BLOCK, SB_HEADS
    assert n_heads % HP == 0

    def body(q_ref, k_ref, v_ref, *rest):
        y_ref = rest[-1]
        qb = pl.program_id(1)
        tri = _triangle(B, right=True)
        qv = [_head_cols(q_ref, slice(None), hh).astype(BF16) for hh in range(HP)]

        def tile(kb, carry, past):
            out = []
            for hh in range(HP):
                acc, c_lk = carry[hh]
                kk = _head_cols(k_ref, _sb_rows(kb), hh).astype(BF16)
                vv = _head_cols(v_ref, _sb_rows(kb), hh).astype(BF16)
                _, a, c_lk = _sb_tile(qv[hh], kk, past, c_lk, tri)
                out.append((acc + _dot(a.astype(BF16), vv), c_lk))
            return tuple(out)

        init = tuple((jnp.zeros((B, HEAD_DIM), F32), jnp.zeros((B, 1), F32)) for _ in range(HP))
        res = lax.fori_loop(1, qb + 1, lambda i, carry: tile(qb - i, carry, None), tile(qb, init, _sb_diagonal()))
        for hh in range(HP):
            y_ref[:, hh * HEAD_DIM:(hh + 1) * HEAD_DIM] = res[hh][0].astype(BF16)

    blk = pl.BlockSpec((B, HP * HEAD_DIM), lambda h, i: (i, h))
    G = n_heads // HP
    return pl.pallas_call(
        body, grid=(G, T // B),
        in_specs=[blk, _head_spec(T, G, HP), _head_spec(T, 2 * G, HP)] + [ANY] * len(after), out_specs=blk,
        out_shape=jax.ShapeDtypeStruct((T, n_heads * HEAD_DIM), BF16),
        compiler_params=_params("parallel", "arbitrary"), name="sb_fwd")(proj, proj, proj, *after)


def _sb_bwd(proj, dy, n_heads):
    T = proj.shape[0]
    B, HP = SB_BLOCK, SB_HEADS
    nq = T // B

    def body(q_ref, k_ref, v_ref, dy_ref, dq_ref, dk_ref, dv_ref, g_s, sig_s, dk_s, dv_s):
        qb = pl.program_id(1)

        @pl.when(qb == 0)
        def _():
            dk_s[...] = jnp.zeros_like(dk_s)
            dv_s[...] = jnp.zeros_like(dv_s)

        tri_r = _triangle(B, right=True)
        tri_l = _triangle(B, right=False)
        qv = [_head_cols(q_ref, slice(None), hh).astype(BF16) for hh in range(HP)]
        dyb = [_head_cols(dy_ref, slice(None), hh).astype(BF16) for hh in range(HP)]

        def sweep(kb, carry, past):
            out = []
            for hh in range(HP):
                kk = _head_cols(k_ref, _sb_rows(kb), hh).astype(BF16)
                vv = _head_cols(v_ref, _sb_rows(kb), hh).astype(BF16)
                ls_pos, a, c_lk = _sb_tile(qv[hh], kk, past, carry[hh], tri_r)
                g_s[hh, kb] = _dot(dyb[hh], vv, NT) * a
                sig_s[hh, kb] = jnp.exp(ls_pos)
                dv_s[_sb_rows(kb), hh * HEAD_DIM:(hh + 1) * HEAD_DIM] += _dot(a.astype(BF16), dyb[hh], TN)
                out.append(c_lk)
            return tuple(out)

        zeros = tuple(jnp.zeros((B, 1), F32) for _ in range(HP))
        lax.fori_loop(1, qb + 1, lambda i, carry: sweep(qb - i, carry, None), sweep(qb, zeros, _sb_diagonal()))

        def back(kb, carry, past):
            out = []
            for hh in range(HP):
                dq, c_g = carry[hh]
                kk = _head_cols(k_ref, _sb_rows(kb), hh).astype(BF16)
                g, sig = g_s[hh, kb], sig_s[hh, kb]
                left = c_g + _lane_scan(g, tri_l)
                dz = g * (1.0 - sig) - left * sig
                if past is not None:
                    dz = jnp.where(past, dz, 0.0)
                dz = (dz * SCALE).astype(BF16)
                dk_s[_sb_rows(kb), hh * HEAD_DIM:(hh + 1) * HEAD_DIM] += _dot(dz, qv[hh], TN)
                out.append((dq + _dot(dz, kk), left[:, B - 1:B] + g[:, B - 1:B]))
            return tuple(out)

        init = tuple((jnp.zeros((B, HEAD_DIM), F32), jnp.zeros((B, 1), F32)) for _ in range(HP))
        res = back(qb, lax.fori_loop(0, qb, lambda kb, carry: back(kb, carry, None), init), _sb_diagonal())
        for hh in range(HP):
            dq_ref[:, hh * HEAD_DIM:(hh + 1) * HEAD_DIM] = res[hh][0].astype(BF16)

        @pl.when(qb == nq - 1)
        def _():
            dk_ref[...] = dk_s[...].astype(BF16)
            dv_ref[...] = dv_s[...].astype(BF16)

    blk = pl.BlockSpec((B, HP * HEAD_DIM), lambda h, i: (i, h))
    G = n_heads // HP
    full = _head_spec(T, 0, HP)
    shp = jax.ShapeDtypeStruct((T, n_heads * HEAD_DIM), BF16)
    return pl.pallas_call(
        body, grid=(G, nq),
        in_specs=[blk, _head_spec(T, G, HP), _head_spec(T, 2 * G, HP), blk], out_specs=[blk, full, full],
        out_shape=[shp, shp, shp],
        scratch_shapes=[pltpu.VMEM((HP, nq, B, B), F32)] * 2 + [pltpu.VMEM((T, HP * HEAD_DIM), F32)] * 2,
        compiler_params=_params("parallel", "arbitrary"), name="sb_bwd")(proj, proj, proj, dy)


DIAGS = PBAND + PAIR


def _diag_onehot():
    d = lax.broadcasted_iota(jnp.int32, (DIAGS, 2 * PAIR), 0)
    r = lax.broadcasted_iota(jnp.int32, (DIAGS, 2 * PAIR), 1)
    return jnp.where(jnp.clip(d - PAIR - PAD, -REL_CLIP, CHUNK - 1) + REL_CLIP == r, 1.0, 0.0)


def _bias_expand(rel_bias):
    H = rel_bias.shape[0]
    table = jnp.pad(rel_bias, ((0, 0), (0, 2 * PAIR - N_REL)))

    def body(rb_ref, o_ref):
        o_ref[...] = lax.dot_general(rb_ref[...], _diag_onehot(), NT, precision=lax.Precision.HIGHEST, preferred_element_type=F32)

    per_diag = pl.pallas_call(body, out_shape=jax.ShapeDtypeStruct((H, DIAGS), F32), name="bias_expand")(table)
    flat = jnp.tile(jnp.pad(per_diag, ((0, 0), (0, 1))), (1, PAIR))[:, :PAIR * DIAGS]
    return flat.reshape(H, PAIR, DIAGS)[:, :, PAIR:]


def _bias_reduce(dbias):
    H = dbias.shape[0]
    padded = jnp.pad(dbias, ((0, 0), (0, 1), (PAIR, 0))).reshape(H, -1)
    skewed = padded[:, :PAIR * (DIAGS + 1)].reshape(H, PAIR, DIAGS + 1)[:, :, :DIAGS]

    def body(s_ref, o_ref):
        per_diag = jnp.sum(s_ref[...], axis=0, keepdims=True)
        o_ref[...] = lax.dot_general(jnp.broadcast_to(per_diag, (8, DIAGS)), _diag_onehot(), NN, precision=lax.Precision.HIGHEST,
                                     preferred_element_type=F32)[0:1]

    return pl.pallas_call(
        body, grid=(H,), in_specs=[pl.BlockSpec((None, PAIR, DIAGS), lambda h: (h, 0, 0))],
        out_specs=pl.BlockSpec((None, 1, 2 * PAIR), lambda h: (h, 0, 0)),
        out_shape=jax.ShapeDtypeStruct((H, 1, 2 * PAIR), F32), compiler_params=_params("parallel"), name="bias_reduce")(skewed)[:, 0]


CA_HEADS = 2


def _ca_mask():
    i = lax.broadcasted_iota(jnp.int32, (PAIR, PBAND), 0)
    j = lax.broadcasted_iota(jnp.int32, (PAIR, PBAND), 1)
    qc, kc = i // CHUNK, j // CHUNK
    return j, (kc >= qc) & (kc <= qc + LEFT_CHUNKS)


def _ca_weights(pr, qp, kb, bias, j, window):
    valid = window & (pr * PAIR + j >= PAD)
    z = jnp.where(valid, _dot(qp, kb, NT) * SCALE + bias, NEG)
    e = jnp.exp(z - jnp.max(z, axis=1, keepdims=True))
    return e / jnp.sum(e, axis=1, keepdims=True)


def _ca_fill(k_ref, v_ref, kpad, vpad):
    T, W = k_ref.shape
    kpad[0:PAD, :] = jnp.zeros((PAD, W), BF16)
    vpad[0:PAD, :] = jnp.zeros((PAD, W), BF16)
    kpad[PAD:PAD + T, :] = k_ref[...].astype(BF16)
    vpad[PAD:PAD + T, :] = v_ref[...].astype(BF16)


def _ca_fwd(proj, bias, n_heads, col0):
    T = proj.shape[0]
    HP = CA_HEADS
    G = n_heads // HP
    assert n_heads % HP == 0 and col0 % HP == 0

    def body(q_ref, k_ref, v_ref, b_ref, y_ref, kpad, vpad):
        _ca_fill(k_ref, v_ref, kpad, vpad)
        j, window = _ca_mask()

        def pair(pr, _):
            r0 = pl.multiple_of(pr * PAIR, PAIR)
            for hh in range(HP):
                qp = _head_cols(q_ref, pl.ds(r0, PAIR), hh).astype(BF16)
                kb = _head_cols(kpad, pl.ds(r0, PBAND), hh)
                vb = _head_cols(vpad, pl.ds(r0, PBAND), hh)
                w = _ca_weights(pr, qp, kb, b_ref[hh], j, window)
                y_ref[pl.ds(r0, PAIR), hh * HEAD_DIM:(hh + 1) * HEAD_DIM] = _dot(w.astype(BF16), vb).astype(BF16)
            return 0

        lax.fori_loop(0, T // PAIR, pair, 0)

    c = col0 // HP
    return pl.pallas_call(
        body, grid=(G,),
        in_specs=[_head_spec(T, c, HP), _head_spec(T, c + G, HP), _head_spec(T, c + 2 * G, HP),
                  pl.BlockSpec((HP, PAIR, PBAND), lambda h: (h, 0, 0))],
        out_specs=_head_spec(T, 0, HP), out_shape=jax.ShapeDtypeStruct((T, n_heads * HEAD_DIM), BF16),
        scratch_shapes=[pltpu.VMEM((PAD + T, HP * HEAD_DIM), BF16)] * 2,
        compiler_params=_params("parallel"), name="ca_fwd")(proj, proj, proj, bias)


def _ca_bwd(proj, bias, dy, n_heads, col0):
    T = proj.shape[0]
    HP = CA_HEADS
    G = n_heads // HP

    def body(q_ref, k_ref, v_ref, b_ref, dy_ref, dq_ref, dk_ref, dv_ref, db_ref, kpad, vpad, dkpad, dvpad):
        _ca_fill(k_ref, v_ref, kpad, vpad)
        dkpad[...] = jnp.zeros_like(dkpad)
        dvpad[...] = jnp.zeros_like(dvpad)
        db_ref[...] = jnp.zeros_like(db_ref)
        j, window = _ca_mask()

        def pair(pr, _):
            r0 = pl.multiple_of(pr * PAIR, PAIR)
            for hh in range(HP):
                cols = slice(hh * HEAD_DIM, (hh + 1) * HEAD_DIM)
                qp = _head_cols(q_ref, pl.ds(r0, PAIR), hh).astype(BF16)
                kb = _head_cols(kpad, pl.ds(r0, PBAND), hh)
                vb = _head_cols(vpad, pl.ds(r0, PBAND), hh)
                w = _ca_weights(pr, qp, kb, b_ref[hh], j, window)
                dyp = _head_cols(dy_ref, pl.ds(r0, PAIR), hh).astype(BF16)
                dw = _dot(dyp, vb, NT)
                dz = w * (dw - jnp.sum(dw * w, axis=1, keepdims=True))
                db_ref[hh] += dz
                dzs = (dz * SCALE).astype(BF16)
                dq_ref[pl.ds(r0, PAIR), cols] = _dot(dzs, kb).astype(BF16)
                dkpad[pl.ds(r0, PBAND), cols] += _dot(dzs, qp, TN)
                dvpad[pl.ds(r0, PBAND), cols] += _dot(w.astype(BF16), dyp, TN)
            return 0

        lax.fori_loop(0, T // PAIR, pair, 0)
        dk_ref[...] = dkpad[PAD:PAD + T, :].astype(BF16)
        dv_ref[...] = dvpad[PAD:PAD + T, :].astype(BF16)

    c = col0 // HP
    full = _head_spec(T, 0, HP)
    bspec = pl.BlockSpec((HP, PAIR, PBAND), lambda h: (h, 0, 0))
    shp = jax.ShapeDtypeStruct((T, n_heads * HEAD_DIM), BF16)
    return pl.pallas_call(
        body, grid=(G,),
        in_specs=[_head_spec(T, c, HP), _head_spec(T, c + G, HP), _head_spec(T, c + 2 * G, HP), bspec, full],
        out_specs=[full, full, full, bspec],
        out_shape=[shp, shp, shp, jax.ShapeDtypeStruct((n_heads, PAIR, PBAND), F32)],
        scratch_shapes=[pltpu.VMEM((PAD + T, HP * HEAD_DIM), BF16)] * 2 + [pltpu.VMEM((PAD + T, HP * HEAD_DIM), F32)] * 2,
        compiler_params=_params("parallel"), name="ca_bwd")(proj, proj, proj, bias, dy)


def _local_step(x, p, target, comm, g):
    T, D = x.shape
    H = g["rel_bias"].shape[0]
    W = H * HEAD_DIM
    nb_in = comm.shapes["w_in"][2]
    nb_ff = comm.shapes["w_ffn_in"][2]
    nb_o = comm.shapes["w_sb_out"][2]
    nb_p = comm.shapes["w_ple_in"][2]
    tm = min(T, 1024)
    tn = min(D, 1024)
    gate_col = 6 * W // D

    h1 = _norm_fwd(x, g["g_mix"], "norm1")
    comm.stage("norm1", h1)
    proj = _mm(h1, comm.weight("w_in", h1), mode="nn", tm=tm, tn=nb_in, tk=D, out_dtype=F32, b_blocked=True, after=comm.pending(), name="mm_in")
    comm.stage("mm_in", proj)
    y_sb = _sb_fwd(proj, H, comm.pending())
    bias = _bias_expand(g["rel_bias"])
    y_ca = _ca_fwd(proj, bias, H, 3 * H)
    comm.stage("attention", y_sb, y_ca)
    a_sb = _mm(y_sb, comm.weight("w_sb_out", y_ca), mode="nn", tm=tm, tn=tn, tk=W, out_dtype=F32, b_blocked=True, after=comm.pending(), name="mm_sb_out")
    a_ca = _mm(y_ca, comm.weight("w_ca_out"), mode="nn", tm=tm, tn=tn, tk=W, out_dtype=F32, b_blocked=True, name="mm_ca_out")
    merged = _merge_fwd(proj, a_sb, a_ca, D, gate_col)
    x2 = _mm(merged, comm.weight("w_mix_out"), mode="nn", tm=tm, tn=tn, tk=D, out_dtype=F32, res=x, name="mm_mix")
    h2 = _norm_fwd(x2, g["g_ffn"], "norm2")
    gu = _mm(h2, comm.weight("w_ffn_in", h2), mode="nn", tm=tm, tn=nb_ff, tk=D, out_dtype=F32, b_blocked=True, name="mm_ffn_in")
    comm.stage("mm_ffn_in", gu)
    act = _swiglu_fwd(gu, comm.pending())
    F = act.shape[1]
    tkf = F // 2 if F % 256 == 0 else F
    x3 = _mm(act, comm.weight("w_ffn_out", act), mode="nn", tm=tm, tn=tn, tk=tkf, out_dtype=F32, res=x2, name="mm_ffn_out")
    h3 = _norm_fwd(x3, g["g_ple"], "norm3")
    zg = _mm(h3, comm.weight("w_ple_gate"), mode="nn", tm=tm, tn=tn, tk=D, out_dtype=F32, name="mm_ple_gate")
    pb = _cast_bf16(p, "cast_p")
    P = p.shape[1]
    pe = _mm(pb, comm.weight("w_ple_in"), mode="nn", tm=tm, tn=nb_p, tk=P, out_dtype=F32, b_blocked=True, name="mm_ple_in")
    loss, dx4, dpe, dzg, dg_final = _tail(x3, zg, pe, g["g_final"], target)

    tw = min(D, 1024)
    DW = BF16
    comm.grad("w_ple_in", _mm(pb, dpe, mode="tn", tm=P, tn=nb_p, tk=T, out_dtype=DW, out_block=nb_p, name="mm_d_ple_in"))
    comm.grad("w_ple_gate", _mm(h3, dzg, mode="tn", tm=tw, tn=tn, tk=T, out_dtype=DW, name="mm_d_ple_gate"))
    dh3 = _mm(dzg, comm.weight("w_ple_gate"), mode="nt", tm=tm, tn=tn, tk=D, out_dtype=F32, after=comm.pending(), name="mm_dh3")
    dx3, dx3b, dg_ple = _norm_bwd(dh3, x3, g["g_ple"], dx4, "norm3_bwd")
    comm.grad("w_ffn_out", _mm(act, dx3b, mode="tn", tm=F // 4, tn=tn, tk=T, out_dtype=DW, name="mm_d_ffn_out"))
    dact = _mm(dx3b, comm.weight("w_ffn_out"), mode="nt", tm=tm, tn=tkf, tk=D, out_dtype=F32, after=comm.pending(), name="mm_dact")
    dgu = _swiglu_bwd(dact, gu)
    comm.grad("w_ffn_in", _mm(h2, dgu, mode="tn", tm=tw, tn=nb_ff, tk=T, out_dtype=DW, out_block=nb_ff, name="mm_d_ffn_in"))
    dh2 = _mm(dgu, comm.weight("w_ffn_in"), mode="nt", tm=tm, tn=D, tk=nb_ff, out_dtype=F32, b_blocked=True, after=comm.pending(), name="mm_dh2")
    dx2, dx2b, dg_ffn = _norm_bwd(dh2, x2, g["g_ffn"], dx3, "norm2_bwd")
    dmerged = _mm(dx2b, comm.weight("w_mix_out"), mode="nt", tm=tm, tn=tn, tk=D, out_dtype=F32, name="mm_dmerged")
    da_sb, da_ca, dgate_sb, dgate_ca = _merge_bwd(dmerged, proj, a_sb, a_ca, D, gate_col)
    comm.grad("w_mix_out", _mm(merged, dx2b, mode="tn", tm=tw, tn=tn, tk=T, out_dtype=DW, name="mm_d_mix"))
    comm.grad("w_sb_out", _mm(y_sb, da_sb, mode="tn", tm=min(W, 512), tn=tn, tk=T, out_dtype=DW, out_block=nb_o, name="mm_d_sb_out"))
    comm.grad("w_ca_out", _mm(y_ca, da_ca, mode="tn", tm=min(W, 512), tn=tn, tk=T, out_dtype=DW, out_block=nb_o, name="mm_d_ca_out"))
    dy_sb = _mm(da_sb, comm.weight("w_sb_out"), mode="nt", tm=tm, tn=W, tk=tn, out_dtype=F32, b_blocked=True, after=comm.pending(), name="mm_dy_sb")
    dy_ca = _mm(da_ca, comm.weight("w_ca_out"), mode="nt", tm=tm, tn=W, tk=tn, out_dtype=F32, b_blocked=True, name="mm_dy_ca")
    dq_sb, dk_sb, dv_sb = _sb_bwd(proj, dy_sb, H)
    dq_ca, dk_ca, dv_ca, dbias = _ca_bwd(proj, bias, dy_ca, H, 3 * H)
    d_rel = _bias_reduce(dbias)[:, :N_REL]
    dproj = jnp.concatenate([dq_sb, dk_sb, dv_sb, dq_ca, dk_ca, dv_ca, dgate_sb, dgate_ca], axis=1)
    comm.grad("w_in", _mm(h1, dproj, mode="tn", tm=tw, tn=nb_in, tk=T, out_dtype=DW, out_block=nb_in, name="mm_d_in"))
    dh1 = _mm(dproj, comm.weight("w_in"), mode="nt", tm=tm, tn=D, tk=nb_in, out_dtype=F32, b_blocked=True, after=comm.pending(), name="mm_dh1")
    grad_x, _, dg_mix = _norm_bwd(dh1, x, g["g_mix"], dx2, "norm1_bwd")
    small = dict(g_mix=dg_mix, g_ffn=dg_ffn, g_ple=dg_ple, g_final=dg_final, rel_bias=d_rel)
    return loss, grad_x, small


def _position():
    x, y, c = lax.axis_index("x"), lax.axis_index("y"), lax.axis_index("c")
    return x, y, c


def _block_of(px, py, pc):
    return 4 * px + 2 * py + pc


def _flip(pos, k):
    x, y, c = pos
    return (1 - x if k & 4 else x, 1 - y if k & 2 else y, 1 - c if k & 1 else c)


HBM = pl.BlockSpec(memory_space=pltpu.HBM)
SEM = pl.BlockSpec(memory_space=pltpu.SEMAPHORE)
VMEM_SPEC = pl.BlockSpec(memory_space=pltpu.VMEM)
EFFECT = pltpu.SideEffectType.DATAFLOW_SIDE_EFFECTING
TOKEN = jax.ShapeDtypeStruct((8, 128), F32)


def _hbm(a):
    return pltpu.HBM(a.shape, a.dtype)


def _landing(shape, dtype):
    return pltpu.with_memory_space_constraint(lax.empty(shape, dtype), pltpu.HBM)


def _gather_start(lands, after, name):
    n = len(lands)

    def body(*refs):
        ins = refs[:n]
        send, recv = refs[n + 1], refs[n + 2]
        token = refs[-1]
        x, y, c = _position()
        mine = _block_of(x, y, c)
        peers = [(x, y, 1 - c), (1 - x, y, c), (x, 1 - y, c), (1 - x, 1 - y, c)]
        for wi in range(n):
            for k, peer in enumerate(peers):
                pltpu.make_async_remote_copy(
                    src_ref=ins[wi].at[mine], dst_ref=ins[wi].at[mine], send_sem=send.at[4 * wi + k], recv_sem=recv.at[4 * wi + k],
                    device_id=peer, device_id_type=MESH).start()
        token[...] = jnp.zeros_like(token)

    outs = pl.pallas_call(
        body, name=name, in_specs=[HBM] * n + [ANY], out_specs=(SEM, SEM, *[HBM] * n, VMEM_SPEC),
        out_shape=(pltpu.SemaphoreType.DMA((4 * n,)), pltpu.SemaphoreType.DMA((4 * n,)), *[_hbm(a) for a in lands], TOKEN),
        input_output_aliases={i: 2 + i for i in range(n)},
        compiler_params=pltpu.CompilerParams(has_side_effects=EFFECT))(*[pltpu.with_memory_space_constraint(a, pltpu.HBM) for a in lands], after)
    return outs[0], outs[1], list(outs[2:2 + n]), outs[-1]


def _gather_forward(lands, send0, recv0, after, name):
    n = len(lands)

    def body(*refs):
        ins = refs[:n]
        send0, recv0 = refs[n], refs[n + 1]
        send1, recv1 = refs[n + 2 + len(after)], refs[n + 3 + len(after)]
        token = refs[-1]
        x, y, c = _position()
        chips = [(1 - x, y), (x, 1 - y), (1 - x, 1 - y)]
        for wi in range(n):
            for j, chip in enumerate(chips):
                rows = ins[wi].at[_block_of(*chip, c)]
                pltpu.make_async_remote_copy(
                    src_ref=rows, dst_ref=rows, send_sem=send0.at[4 * wi + 1 + j], recv_sem=recv0.at[4 * wi + 1 + j],
                    device_id=(*chip, c), device_id_type=MESH).wait_recv()
                pltpu.make_async_remote_copy(
                    src_ref=rows, dst_ref=rows, send_sem=send1.at[3 * wi + j], recv_sem=recv1.at[3 * wi + j],
                    device_id=(x, y, 1 - c), device_id_type=MESH).start()
        token[...] = jnp.zeros_like(token)

    outs = pl.pallas_call(
        body, name=name, in_specs=[HBM] * n + [SEM, SEM] + [ANY] * len(after), out_specs=(SEM, SEM, *[HBM] * n, VMEM_SPEC),
        out_shape=(pltpu.SemaphoreType.DMA((3 * n,)), pltpu.SemaphoreType.DMA((3 * n,)), *[_hbm(a) for a in lands], TOKEN),
        input_output_aliases={i: 2 + i for i in range(n)},
        compiler_params=pltpu.CompilerParams(has_side_effects=EFFECT))(*lands, send0, recv0, *after)
    return outs[0], outs[1], list(outs[2:2 + n]), outs[-1]


def _gather_wait(lands, send0, recv0, send1, recv1, after, name):
    n = len(lands)

    def body(*refs):
        ins = refs[:n]
        send0, recv0, send1, recv1 = refs[n:n + 4]
        x, y, c = _position()
        mine = _block_of(x, y, c)
        sibling = (x, y, 1 - c)
        peers = [sibling, (1 - x, y, c), (x, 1 - y, c), (1 - x, 1 - y, c)]
        chips = [(1 - x, y), (x, 1 - y), (1 - x, 1 - y)]
        for wi in range(n):
            own = ins[wi].at[mine]
            for k, peer in enumerate(peers):
                pltpu.make_async_remote_copy(src_ref=own, dst_ref=own, send_sem=send0.at[4 * wi + k], recv_sem=recv0.at[4 * wi + k],
                                             device_id=peer, device_id_type=MESH).wait_send()
            theirs = ins[wi].at[_block_of(*sibling)]
            pltpu.make_async_remote_copy(src_ref=theirs, dst_ref=theirs, send_sem=send0.at[4 * wi], recv_sem=recv0.at[4 * wi],
                                         device_id=sibling, device_id_type=MESH).wait_recv()
            for j, chip in enumerate(chips):
                sent = ins[wi].at[_block_of(*chip, c)]
                got = ins[wi].at[_block_of(*chip, 1 - c)]
                pltpu.make_async_remote_copy(src_ref=sent, dst_ref=sent, send_sem=send1.at[3 * wi + j], recv_sem=recv1.at[3 * wi + j],
                                             device_id=sibling, device_id_type=MESH).wait_send()
                pltpu.make_async_remote_copy(src_ref=got, dst_ref=got, send_sem=send1.at[3 * wi + j], recv_sem=recv1.at[3 * wi + j],
                                             device_id=sibling, device_id_type=MESH).wait_recv()

    outs = pl.pallas_call(
        body, name=name, in_specs=[HBM] * n + [SEM] * 4 + [ANY], out_specs=tuple([HBM] * n),
        out_shape=tuple(_hbm(a) for a in lands), input_output_aliases={i: i for i in range(n)},
        compiler_params=pltpu.CompilerParams(has_side_effects=EFFECT))(*lands, send0, recv0, send1, recv1, after)
    return list(outs)


def _plan_direct(me):
    return [(_block_of(*_flip(me, k)), k - 1, _flip(me, k)) for k in range(1, N_DEV)]


def _plan_sibling(me):
    x, y, c = me
    return [(_block_of(ci // 2, ci % 2, 1 - c), ci, (x, y, 1 - c)) for ci in range(4)]


def _plan_chips(me):
    x, y, c = me
    out = []
    for k in range(1, 4):
        px, py = (1 - x if k & 2 else x), (1 - y if k & 1 else y)
        out.append((2 * px + py, k - 1, (px, py, c)))
    return out


def _exchange_start(blocks, plan, name):
    n = len(blocks)
    slots = len(plan((0, 0, 0)))

    def body(*refs):
        srcs, lands = refs[:n], refs[n:2 * n]
        send, recv = refs[2 * n], refs[2 * n + 1]
        token = refs[-1]
        for wi in range(n):
            for block, slot, peer in plan(_position()):
                pltpu.make_async_remote_copy(
                    src_ref=srcs[wi].at[block], dst_ref=lands[wi].at[slot], send_sem=send.at[slots * wi + slot],
                    recv_sem=recv.at[slots * wi + slot], device_id=peer, device_id_type=MESH).start()
        token[...] = jnp.zeros_like(token)

    zones = [_landing((slots,) + b.shape[1:], b.dtype) for b in blocks]
    outs = pl.pallas_call(
        body, name=name, in_specs=[HBM] * (2 * n), out_specs=(SEM, SEM, *[HBM] * (2 * n), VMEM_SPEC),
        out_shape=(pltpu.SemaphoreType.DMA((slots * n,)), pltpu.SemaphoreType.DMA((slots * n,)), *[_hbm(a) for a in blocks],
                   *[_hbm(z) for z in zones], TOKEN),
        input_output_aliases={i: 2 + i for i in range(2 * n)},
        compiler_params=pltpu.CompilerParams(has_side_effects=EFFECT))(
            *[pltpu.with_memory_space_constraint(b, pltpu.HBM) for b in blocks], *zones)
    return outs[0], outs[1], list(outs[2:2 + n]), list(outs[2 + n:2 + 2 * n]), outs[-1]


def _exchange_wait(groups, plan, after, name):
    flat, counts = [], []
    for send, recv, blocks, zones in groups:
        flat += [*blocks, *zones, send, recv]
        counts.append(len(blocks))
    slots = len(plan((0, 0, 0)))

    def body(*refs):
        pos = 0
        for n in counts:
            srcs, lands = refs[pos:pos + n], refs[pos + n:pos + 2 * n]
            send, recv = refs[pos + 2 * n], refs[pos + 2 * n + 1]
            pos += 2 * n + 2
            for wi in range(n):
                for block, slot, peer in plan(_position()):
                    cp = pltpu.make_async_remote_copy(
                        src_ref=srcs[wi].at[block], dst_ref=lands[wi].at[slot], send_sem=send.at[slots * wi + slot],
                        recv_sem=recv.at[slots * wi + slot], device_id=peer, device_id_type=MESH)
                    cp.wait_send()
                    cp.wait_recv()

    in_specs, out_specs, out_shape, aliases = [], [], [], {}
    i = 0
    for n, (send, recv, blocks, zones) in zip(counts, groups):
        for a in (*blocks, *zones):
            aliases[i] = len(out_shape)
            in_specs.append(HBM)
            out_specs.append(HBM)
            out_shape.append(_hbm(a))
            i += 1
        in_specs += [SEM, SEM]
        i += 2
    outs = pl.pallas_call(
        body, name=name, in_specs=in_specs + [ANY], out_specs=tuple(out_specs), out_shape=tuple(out_shape),
        input_output_aliases=aliases, compiler_params=pltpu.CompilerParams(has_side_effects=EFFECT))(*flat, after)
    res, pos = [], 0
    for n in counts:
        res.append((list(outs[pos:pos + n]), list(outs[pos + n:pos + 2 * n])))
        pos += 2 * n
    return res


def _sibling_sum(blocks, zone, core, name):
    _, R, C = zone.shape
    rt = next(r for r in (R, R // 2, R // 4, 128, 64) if R % r == 0 and r % 16 == 0 and r * C <= 4 * 1024 * 1024)

    def body(core_ref, own_ref, z_ref, o_ref):
        o_ref[...] = (own_ref[...].astype(F32) + z_ref[...].astype(F32)).astype(o_ref.dtype)

    grid_spec = pltpu.PrefetchScalarGridSpec(
        num_scalar_prefetch=1, grid=(4, R // rt),
        in_specs=[pl.BlockSpec((None, rt, C), lambda ci, i, core_ref: (2 * ci + core_ref[0], i, 0)),
                  pl.BlockSpec((None, rt, C), lambda ci, i, core_ref: (ci, i, 0))],
        out_specs=pl.BlockSpec((None, rt, C), lambda ci, i, core_ref: (ci, i, 0)))
    return pl.pallas_call(body, grid_spec=grid_spec, out_shape=jax.ShapeDtypeStruct(zone.shape, zone.dtype),
                          compiler_params=_params("parallel", "parallel"), name=name)(core, blocks, zone)


def _adamw(w, g, m, v):
    m = ADAM_B1 * m + (1.0 - ADAM_B1) * g
    v = ADAM_B2 * v + (1.0 - ADAM_B2) * (g * g)
    m_hat = m / (1.0 - ADAM_B1 ** ADAM_STEP)
    v_hat = v / (1.0 - ADAM_B2 ** ADAM_STEP)
    delta = -ADAM_LR * (m_hat / (jnp.sqrt(v_hat) + ADAM_EPS) + ADAM_WD * w)
    return delta, m, v


def _reduce_adamw(blocks, zone, mine, w, m, v, name):
    R, C = w.shape
    rt = 128 if R % 128 == 0 else 64
    assert R % rt == 0

    def body(mine_ref, own_ref, z_ref, w_ref, m_ref, v_ref, g_out, d_out, m_out, v_out):
        g = own_ref[...].astype(F32)
        for s in range(zone.shape[0]):
            g = g + z_ref[s].astype(F32)
        delta, m2, v2 = _adamw(w_ref[...], g, m_ref[...], v_ref[...])
        g_out[...] = g
        d_out[...] = delta
        m_out[...] = m2
        v_out[...] = v2

    spec = pl.BlockSpec((rt, C), lambda i, mine_ref: (i, 0))
    grid_spec = pltpu.PrefetchScalarGridSpec(
        num_scalar_prefetch=1, grid=(R // rt,),
        in_specs=[pl.BlockSpec((None, rt, C), lambda i, mine_ref: (mine_ref[0], i, 0)),
                  pl.BlockSpec((zone.shape[0], rt, C), lambda i, mine_ref: (0, i, 0)), spec, spec, spec],
        out_specs=[spec] * 4)
    return pl.pallas_call(body, grid_spec=grid_spec, out_shape=[jax.ShapeDtypeStruct((R, C), F32)] * 4,
                          compiler_params=_params("parallel"), name=name)(mine, blocks, zone, w, m, v)


def _small_step(part, w, m, v, after):
    R, C = part.shape

    def body(part_ref, w_ref, m_ref, v_ref, *rest):
        g_out, d_out, m_out, v_out, gath, send, recv = rest[len(after):]
        me = _position()
        gath[_block_of(*me)] = part_ref[...]

        def copy(k, slot):
            return pltpu.make_async_remote_copy(
                src_ref=part_ref, dst_ref=gath.at[slot], send_sem=send.at[k - 1], recv_sem=recv.at[k - 1],
                device_id=_flip(me, k), device_id_type=MESH)

        sent = [copy(k, _block_of(*me)) for k in range(1, N_DEV)]
        for cp in sent:
            cp.start()
        for k in range(1, N_DEV):
            copy(k, _block_of(*_flip(me, k))).wait_recv()
        for cp in sent:
            cp.wait_send()
        g = gath[0]
        for s in range(1, N_DEV):
            g = g + gath[s]
        delta, m2, v2 = _adamw(w_ref[...], g, m_ref[...], v_ref[...])
        g_out[...] = g
        d_out[...] = delta
        m_out[...] = m2
        v_out[...] = v2

    vm = pl.BlockSpec(memory_space=pltpu.VMEM)
    return pl.pallas_call(
        body, in_specs=[vm] * 4 + [ANY] * len(after), out_specs=[vm] * 4, out_shape=[jax.ShapeDtypeStruct((R, C), F32)] * 4,
        scratch_shapes=[pltpu.VMEM((N_DEV, R, C), F32), pltpu.SemaphoreType.DMA((7,)), pltpu.SemaphoreType.DMA((7,))],
        name="small_step")(part, w, m, v, *after)


COLUMN_SHARDED = ("w_in", "w_sb_out", "w_ca_out", "w_ffn_in", "w_ple_in")
ROW_SHARDED = ("w_mix_out", "w_ffn_out", "w_ple_gate")
BIG = COLUMN_SHARDED + ROW_SHARDED
SMALL = ("g_mix", "g_ffn", "g_ple", "g_final", "rel_bias")
WEIGHTS = ("w_in", "w_sb_out", "w_ca_out", "w_mix_out", "rel_bias", "g_mix", "g_ffn", "g_ple", "g_final",
           "w_ffn_in", "w_ffn_out", "w_ple_in", "w_ple_gate")


def _pack_small(t, D):
    rows = [t[n].reshape(1, D) for n in SMALL[:4]]
    rb = t["rel_bias"].reshape(1, -1)
    rows.append(jnp.pad(rb, ((0, 0), (0, D - rb.shape[1]))))
    return jnp.concatenate(rows + [jnp.zeros((8 - len(rows), D), F32)], axis=0)


def _unpack_small(a, like):
    out = {n: a[i].reshape(like[n].shape) for i, n in enumerate(SMALL[:4])}
    out["rel_bias"] = a[4, :like["rel_bias"].size].reshape(like["rel_bias"].shape)
    return out


GATHER_GROUPS = (("w_in",), ("w_sb_out", "w_ca_out", "w_mix_out"), ("w_ffn_in",), ("w_ffn_out", "w_ple_gate", "w_ple_in"))
FORWARD_AFTER = ("norm1", "mm_in", "attention", "mm_ffn_in")
GRAD_GROUPS = (("w_ple_in", "w_ple_gate"), ("w_ffn_out",), ("w_ffn_in",), ("w_mix_out", "w_sb_out", "w_ca_out"), ("w_in",))


class _Exchange:
    def __init__(self, shards):
        me = _position()
        self.mine = _block_of(*me)
        self.chip = jnp.reshape(2 * me[0] + me[1], (1,)).astype(jnp.int32)
        self.core = jnp.reshape(me[2], (1,)).astype(jnp.int32)
        self.device = jnp.reshape(self.mine, (1,)).astype(jnp.int32)
        self.shapes = {n: ((N_DEV,) + s.shape if n in COLUMN_SHARDED else (N_DEV * s.shape[0], s.shape[1])) for n, s in shards.items()}
        self.tokens = []
        self.ready = {}
        self.gathers = []
        for gi, names in enumerate(GATHER_GROUPS):
            lands = [lax.dynamic_update_slice(lax.empty((N_DEV,) + shards[n].shape, BF16), shards[n][None], (self.mine, 0, 0))
                     for n in names]
            behind = self.tokens[-1] if self.tokens else shards[names[0]]
            send0, recv0, lands, token = _gather_start(lands, behind, f"gather_start_{gi}")
            self.tokens.append(token)
            self.gathers.append(dict(names=names, lands=lands, sems=(send0, recv0), token=token))
        self.grads = {}
        self.exchanges = []

    def pending(self):
        tokens, self.tokens = self.tokens, []
        return tokens

    def stage(self, tag, *made):
        gi = FORWARD_AFTER.index(tag)
        gth = self.gathers[gi]
        send1, recv1, lands, token = _gather_forward(gth["lands"], *gth["sems"], made + tuple(self.tokens), f"gather_forward_{gi}")
        gth.update(lands=lands, sems=gth["sems"] + (send1, recv1), token=token)
        self.tokens.append(token)

    def weight(self, name, after=None):
        if name not in self.ready:
            gi = next(i for i, names in enumerate(GATHER_GROUPS) if name in names)
            gth = self.gathers[gi]
            for n, a in zip(gth["names"], _gather_wait(gth["lands"], *gth["sems"], gth["token"] if after is None else after, f"gather_wait_{gi}")):
                self.ready[n] = a if n in COLUMN_SHARDED else a.reshape(self.shapes[n])
        return self.ready[name]

    def grad(self, name, blocks):
        self.grads[name] = blocks if name in COLUMN_SHARDED else blocks.reshape((N_DEV, -1, blocks.shape[-1]))
        names = next(names for names in GRAD_GROUPS if name in names)
        if not all(n in self.grads for n in names):
            return
        blocks = [self.grads[n] for n in names]
        last = names == GRAD_GROUPS[-1]
        if last:
            send, recv, blocks, zones, token = _exchange_start(blocks, _plan_sibling, "pair_start_" + names[0])
            (blocks, zones), = _exchange_wait([(send, recv, blocks, zones)], _plan_sibling, token, "pair_wait_" + names[0])
            blocks = [_sibling_sum(b, z, self.core, "pair_sum_" + n) for n, b, z in zip(names, blocks, zones)]
        plan = _plan_chips if last else _plan_direct
        send, recv, blocks, zones, token = _exchange_start(blocks, plan, "exchange_start_" + names[0])
        self.exchanges.append(dict(names=names, state=(send, recv, blocks, zones), plan=plan, own=self.chip if last else self.device))
        self.tokens.append(token)

    def collect(self, which, after, name):
        sel = [e for e in self.exchanges if GRAD_GROUPS.index(e["names"]) in which]
        out = {}
        for e, (blocks, zones) in zip(sel, _exchange_wait([e["state"] for e in sel], sel[0]["plan"], after, name)):
            out.update({n: (b, e["own"], z) for n, b, z in zip(e["names"], blocks, zones)})
        return out


def kernel(x, p, w_in, w_sb_out, w_ca_out, w_mix_out, rel_bias, g_mix, g_ffn, g_ple, g_final, w_ffn_in, w_ffn_out, w_ple_in, w_ple_gate, loss_target, m_w_in, m_w_sb_out, m_w_ca_out, m_w_mix_out, m_rel_bias, m_g_mix, m_g_ffn, m_g_ple, m_g_final, m_w_ffn_in, m_w_ffn_out, m_w_ple_in, m_w_ple_gate, v_w_in, v_w_sb_out, v_w_ca_out, v_w_mix_out, v_rel_bias, v_g_mix, v_g_ffn, v_g_ple, v_g_final, v_w_ffn_in, v_w_ffn_out, v_w_ple_in, v_w_ple_gate):
    wts = dict(w_in=w_in, w_sb_out=w_sb_out, w_ca_out=w_ca_out, w_mix_out=w_mix_out, rel_bias=rel_bias, g_mix=g_mix, g_ffn=g_ffn,
               g_ple=g_ple, g_final=g_final, w_ffn_in=w_ffn_in, w_ffn_out=w_ffn_out, w_ple_in=w_ple_in, w_ple_gate=w_ple_gate)
    mom = dict(w_in=m_w_in, w_sb_out=m_w_sb_out, w_ca_out=m_w_ca_out, w_mix_out=m_w_mix_out, rel_bias=m_rel_bias, g_mix=m_g_mix,
               g_ffn=m_g_ffn, g_ple=m_g_ple, g_final=m_g_final, w_ffn_in=m_w_ffn_in, w_ffn_out=m_w_ffn_out, w_ple_in=m_w_ple_in,
               w_ple_gate=m_w_ple_gate)
    var = dict(w_in=v_w_in, w_sb_out=v_w_sb_out, w_ca_out=v_w_ca_out, w_mix_out=v_w_mix_out, rel_bias=v_rel_bias, g_mix=v_g_mix,
               g_ffn=v_g_ffn, g_ple=v_g_ple, g_final=v_g_final, w_ffn_in=v_w_ffn_in, w_ffn_out=v_w_ffn_out, w_ple_in=v_w_ple_in,
               w_ple_gate=v_w_ple_gate)
    T, D = x.shape[1], x.shape[2]
    shard = {n: wts[n].reshape(wts[n].shape[-2:]) for n in BIG}
    comm = _Exchange({n: _cast_bf16(shard[n], "cast_" + n) for n in BIG})
    g = dict(g_mix=g_mix.reshape(1, D), g_ffn=g_ffn.reshape(1, D), g_ple=g_ple.reshape(1, D), g_final=g_final.reshape(1, D),
             rel_bias=rel_bias.reshape(rel_bias.shape[-2:]))

    loss, grad_x, dsmall = _local_step(x.reshape(T, D), p.reshape(T, -1), loss_target.reshape(T, D), comm, g)
    loss = lax.psum(loss[0, 0], ("x", "y", "c"))

    grad, delta, new_m, new_v = {}, {}, {}, {}

    def update(parts):
        done = []
        for n, (blocks, own, zone) in parts.items():
            outs = _reduce_adamw(blocks, zone, own, shard[n], mom[n].reshape(shard[n].shape), var[n].reshape(shard[n].shape), "adamw_" + n)
            grad[n], delta[n], new_m[n], new_v[n] = [o.reshape(wts[n].shape) for o in outs]
            done.append(outs[0])
        return done

    done = update(comm.collect(range(len(GRAD_GROUPS) - 1), grad_x, "exchange_wait_rest"))
    outs = _small_step(_pack_small(dsmall, D), _pack_small(wts, D), _pack_small(mom, D), _pack_small(var, D), done)
    for dst, a in zip((grad, delta, new_m, new_v), outs):
        dst.update(_unpack_small(a, wts))
    update(comm.collect([len(GRAD_GROUPS) - 1], outs[0], "exchange_wait_w_in"))

    return (loss, grad_x.reshape(x.shape), *[grad[n] for n in WEIGHTS], *[delta[n] for n in WEIGHTS],
            *[new_m[n] for n in WEIGHTS], *[new_v[n] for n in WEIGHTS])
```

```python
import functools

import jax
import jax.numpy as jnp
from jax import lax
from jax.experimental import pallas as pl
from jax.experimental.pallas import tpu as pltpu

F32, BF16 = jnp.float32, jnp.bfloat16

N_DEV = 8
HEAD_DIM = 128
CHUNK = 64
LEFT_CHUNKS = 8
REL_CLIP = 128
N_REL = REL_CLIP + CHUNK
PAIR = 2 * CHUNK
PBAND = (LEFT_CHUNKS + 2) * CHUNK
PAD = LEFT_CHUNKS * CHUNK
SB_BLOCK = 256
ROWS = 256
EPS = 1e-6
NEG = -1e30
SCALE = HEAD_DIM ** -0.5
VMEM_LIMIT_BYTES = 56 * 1024 * 1024

ADAM_LR, ADAM_B1, ADAM_B2, ADAM_EPS, ADAM_WD, ADAM_STEP = 0.001, 0.9, 0.999, 1e-08, 0.01, 10

ANY = pl.BlockSpec(memory_space=pl.ANY)
NN = (((1,), (0,)), ((), ()))
NT = (((1,), (1,)), ((), ()))
TN = (((0,), (0,)), ((), ()))
MESH = pl.DeviceIdType.MESH


def _params(*sem):
    return pltpu.CompilerParams(dimension_semantics=sem or None, vmem_limit_bytes=VMEM_LIMIT_BYTES)


def _dot(a, b, dims=NN):
    return lax.dot_general(a, b, dims, preferred_element_type=F32)


def _mm(a, b, *, mode, tm, tn, tk, out_dtype, name, b_blocked=False, out_block=None, res=None, after=(), a_cols=(1, 0), out_cols=(1, 1, 0), into=None):
    bg = og = 1
    if mode == "nn":
        M, K = a.shape
        a_spec = pl.BlockSpec((tm, tk), lambda i, j, k: (i, k))
        if b_blocked:
            G, _, nb = b.shape
            N = G * nb
            if tn > nb:
                bg = tn // nb
                assert tn % nb == 0
                b_spec = pl.BlockSpec((bg, tk, nb), lambda i, j, k: (j, k, 0))
            else:
                per = nb // tn
                assert nb % tn == 0
                b_spec = pl.BlockSpec((None, tk, tn), lambda i, j, k: (j // per, k, j % per))
        else:
            N = b.shape[1]
            b_spec = pl.BlockSpec((tk, tn), lambda i, j, k: (k, j))
        dims = NN
    elif mode == "nt":
        M, K = a.shape
        a_spec = pl.BlockSpec((tm, tk), lambda i, j, k: (i, k))
        if b_blocked:
            G, N, nb = b.shape
            K = G * nb
            if tk > nb:
                bg = tk // nb
                assert tk % nb == 0
                b_spec = pl.BlockSpec((bg, tn, nb), lambda i, j, k: (k, j, 0))
            else:
                per = nb // tk
                assert nb % tk == 0
                b_spec = pl.BlockSpec((None, tn, tk), lambda i, j, k: (k // per, j, k % per))
        else:
            N = b.shape[0]
            b_spec = pl.BlockSpec((tn, tk), lambda i, j, k: (j, k))
        dims = NT
    else:
        K, M = a.shape
        N = b.shape[1]
        a_spec = pl.BlockSpec((tk, tm), lambda i, j, k: (k, i))
        b_spec = pl.BlockSpec((tk, tn), lambda i, j, k: (k, j))
        dims = TN
    if mode != "tn":
        if mode == "nn":
            K = b.shape[-2]
        elif not b_blocked:
            K = b.shape[1]
        a_spec = pl.BlockSpec((tm, tk), lambda i, j, k: (i, k * a_cols[0] + a_cols[1]))
    assert M % tm == 0 and N % tn == 0 and K % tk == 0, (name, M, N, K, tm, tn, tk)
    nk = K // tk
    if out_block is None:
        out_shape = jax.ShapeDtypeStruct((M, N * out_cols[0]), out_dtype)
        o_spec = pl.BlockSpec((tm, tn), lambda i, j, k: (i, j * out_cols[1] + out_cols[2]))
    else:
        out_shape = jax.ShapeDtypeStruct((N // out_block, M, out_block), out_dtype)
        if tn > out_block:
            og = tn // out_block
            assert tn % out_block == 0
            o_spec = pl.BlockSpec((og, tm, out_block), lambda i, j, k: (j, i, 0))
        else:
            per_o = out_block // tn
            assert out_block % tn == 0
            o_spec = pl.BlockSpec((None, tm, tn), lambda i, j, k: (j // per_o, i, j % per_o))
    in_specs = [a_spec, b_spec]
    args = [a, b]
    if res is not None:
        in_specs.append(pl.BlockSpec((tm, tn), lambda i, j, k: (i, j * out_cols[1] + out_cols[2])))
        args.append(res)
    n_in = len(args) + len(after) + (into is not None)

    def product(a_ref, b_ref):
        if bg == 1:
            return _dot(a_ref[...], b_ref[...], dims)
        nb = b_ref.shape[2]
        if mode == "nn":
            return jnp.concatenate([_dot(a_ref[...], b_ref[g], dims) for g in range(bg)], axis=1)
        return sum(_dot(a_ref[:, g * nb:(g + 1) * nb], b_ref[g], dims) for g in range(bg))

    def body(*refs):
        a_ref, b_ref = refs[0], refs[1]
        r_ref = refs[2] if res is not None else None
        o_ref = refs[n_in]

        def finish(acc):
            if r_ref is not None:
                acc = acc + r_ref[...]
            if og == 1:
                o_ref[...] = acc.astype(o_ref.dtype)
            else:
                for g in range(og):
                    o_ref[g] = acc[:, g * out_block:(g + 1) * out_block].astype(o_ref.dtype)

        if nk == 1:
            finish(product(a_ref, b_ref))
        else:
            acc_ref = refs[-1]
            k = pl.program_id(2)

            @pl.when(k == 0)
            def _():
                acc_ref[...] = jnp.zeros_like(acc_ref)

            acc_ref[...] += product(a_ref, b_ref)

            @pl.when(k == nk - 1)
            def _():
                finish(acc_ref[...])

    return pl.pallas_call(
        body, grid=(M // tm, N // tn, nk), in_specs=in_specs + [ANY] * (n_in - len(args)), out_specs=o_spec, out_shape=out_shape,
        scratch_shapes=[] if nk == 1 else [pltpu.VMEM((tm, tn), F32)], input_output_aliases={} if into is None else {n_in - 1: 0},
        compiler_params=_params("parallel", "parallel", "arbitrary"), name=name)(*args, *after, *(() if into is None else (into,)))


def _row_spec(d, col=0):
    return pl.BlockSpec((ROWS, d), lambda i: (i, col))


def _vec_spec(d):
    return pl.BlockSpec((1, d), lambda i: (0, 0))


def _rms(x):
    return lax.rsqrt(jnp.mean(x * x, axis=-1, keepdims=True) + EPS)


def _norm_fwd(x, g, name):
    T, D = x.shape

    def body(x_ref, g_ref, h_ref):
        xv = x_ref[...]
        h_ref[...] = (xv * _rms(xv) * g_ref[...]).astype(BF16)

    return pl.pallas_call(body, grid=(T // ROWS,), in_specs=[_row_spec(D), _vec_spec(D)], out_specs=_row_spec(D),
                          out_shape=jax.ShapeDtypeStruct((T, D), BF16), compiler_params=_params("parallel"), name=name)(x, g)


def _norm_bwd_math(dh, xv, gv):
    r = _rms(xv)
    xhat = xv * r
    dxhat = dh * gv
    dx = r * (dxhat - xhat * jnp.mean(dxhat * xhat, axis=-1, keepdims=True))
    dg = jnp.sum(dh * xhat, axis=0, keepdims=True)
    return dx, dg


def _norm_bwd(dh, x, g, dres, name):
    T, D = x.shape

    def body(dh_ref, x_ref, g_ref, dres_ref, dx_ref, dxb_ref, dg_ref):
        dx, dg = _norm_bwd_math(dh_ref[...], x_ref[...], g_ref[...])
        dx = dx + dres_ref[...]
        dx_ref[...] = dx
        dxb_ref[...] = dx.astype(BF16)

        @pl.when(pl.program_id(0) == 0)
        def _():
            dg_ref[...] = jnp.zeros_like(dg_ref)

        dg_ref[...] += dg

    return pl.pallas_call(
        body, grid=(T // ROWS,), in_specs=[_row_spec(D), _row_spec(D), _vec_spec(D), _row_spec(D)],
        out_specs=[_row_spec(D), _row_spec(D), _vec_spec(D)],
        out_shape=[jax.ShapeDtypeStruct((T, D), F32), jax.ShapeDtypeStruct((T, D), BF16), jax.ShapeDtypeStruct((1, D), F32)],
        compiler_params=_params("arbitrary"), name=name)(dh, x, g, dres)


def _mm_merge(y_sb, y_ca, w_sb, w_ca, proj, gate_col, tm, tn, after):
    T, W = y_sb.shape
    G, _, nb = w_sb.shape
    D, bg = G * nb, tn // nb
    assert tn % nb == 0 and D % tn == 0
    per = D // tn

    def body(ys_ref, yc_ref, ws_ref, wc_ref, gs_ref, gc_ref, *rest):
        as_ref, ac_ref, m_ref = rest[len(after):]
        a = jnp.concatenate([_dot(ys_ref[...], ws_ref[g]) for g in range(bg)], axis=1)
        b = jnp.concatenate([_dot(yc_ref[...], wc_ref[g]) for g in range(bg)], axis=1)
        as_ref[...] = a
        ac_ref[...] = b
        m_ref[...] = (jax.nn.sigmoid(gs_ref[...]) * a + jax.nn.sigmoid(gc_ref[...]) * b).astype(BF16)

    y_spec = pl.BlockSpec((tm, W), lambda i, j: (i, 0))
    w_spec = pl.BlockSpec((bg, W, nb), lambda i, j: (j, 0, 0))
    out = pl.BlockSpec((tm, tn), lambda i, j: (i, j))
    f32 = jax.ShapeDtypeStruct((T, D), F32)
    return pl.pallas_call(
        body, grid=(T // tm, per),
        in_specs=[y_spec, y_spec, w_spec, w_spec, pl.BlockSpec((tm, tn), lambda i, j: (i, gate_col * per + j)),
                  pl.BlockSpec((tm, tn), lambda i, j: (i, (gate_col + 1) * per + j))] + [ANY] * len(after),
        out_specs=[out, out, out], out_shape=[f32, f32, jax.ShapeDtypeStruct((T, D), BF16)],
        compiler_params=_params("parallel", "parallel"), name="mm_merge")(y_sb, y_ca, w_sb, w_ca, proj, proj, *after)


def _merge_bwd(dm, proj, a_sb, a_ca, D, gate_col):
    T = proj.shape[0]

    def body(dm_ref, gs_ref, gc_ref, a_ref, b_ref, da_ref, db_ref, dgs_ref, dgc_ref):
        dmv = dm_ref[...]
        ss, sc = jax.nn.sigmoid(gs_ref[...]), jax.nn.sigmoid(gc_ref[...])
        da_ref[...] = (dmv * ss).astype(BF16)
        db_ref[...] = (dmv * sc).astype(BF16)
        dgs_ref[...] = (dmv * a_ref[...] * ss * (1.0 - ss)).astype(BF16)
        dgc_ref[...] = (dmv * b_ref[...] * sc * (1.0 - sc)).astype(BF16)

    return pl.pallas_call(
        body, grid=(T // ROWS,),
        in_specs=[_row_spec(D), _row_spec(D, gate_col), _row_spec(D, gate_col + 1), _row_spec(D), _row_spec(D)],
        out_specs=[_row_spec(D)] * 4, out_shape=[jax.ShapeDtypeStruct((T, D), BF16)] * 4,
        compiler_params=_params("parallel"), name="merge_bwd")(dm, proj, proj, a_sb, a_ca)


def _mm_swiglu(h, w, tm):
    T, D = h.shape
    G2, _, nb = w.shape
    G = G2 // 2

    def body(h_ref, wg_ref, wu_ref, g_ref, u_ref, act_ref):
        hv = h_ref[...]
        gv = _dot(hv, wg_ref[...])
        uv = _dot(hv, wu_ref[...])
        g_ref[...] = gv
        u_ref[...] = uv
        act_ref[...] = (gv * jax.nn.sigmoid(gv) * uv).astype(BF16)

    out = pl.BlockSpec((tm, nb), lambda j, i: (i, j))
    f32 = jax.ShapeDtypeStruct((T, G * nb), F32)
    return pl.pallas_call(
        body, grid=(G, T // tm),
        in_specs=[pl.BlockSpec((tm, D), lambda j, i: (i, 0)), pl.BlockSpec((None, D, nb), lambda j, i: (j, 0, 0)),
                  pl.BlockSpec((None, D, nb), lambda j, i: (j + G, 0, 0))],
        out_specs=[out, out, out], out_shape=[f32, f32, jax.ShapeDtypeStruct((T, G * nb), BF16)],
        compiler_params=_params("parallel", "parallel"), name="mm_ffn_in")(h, w, w)


def _swiglu_bwd(dact, gate, up):
    T, F = gate.shape

    def body(d_ref, g_ref, u_ref, o_ref):
        dv, gv, uv = d_ref[...], g_ref[...], u_ref[...]
        s = jax.nn.sigmoid(gv)
        o_ref[:, 0:F] = (dv * uv * s * (1.0 + gv * (1.0 - s))).astype(BF16)
        o_ref[:, F:2 * F] = (dv * gv * s).astype(BF16)

    return pl.pallas_call(body, grid=(T // ROWS,), in_specs=[_row_spec(F)] * 3, out_specs=_row_spec(2 * F),
                          out_shape=jax.ShapeDtypeStruct((T, 2 * F), BF16), compiler_params=_params("parallel"), name="swiglu_bwd")(dact, gate, up)


def _tail(x3, zg, pe, g_final, target):
    T, D = x3.shape

    def body(x3_ref, zg_ref, pe_ref, g_ref, t_ref, loss_ref, dx_ref, dpe_ref, dzg_ref, dg_ref):
        gate = jax.nn.sigmoid(zg_ref[...])
        pev = pe_ref[...]
        x4 = x3_ref[...] + gate * pev
        gv = g_ref[...]
        xhat = x4 * _rms(x4)
        err = xhat * gv - t_ref[...]
        part = 0.5 * jnp.sum(jnp.mean(err * err, axis=-1, keepdims=True), axis=0, keepdims=True)
        dx, dg = _norm_bwd_math(err * (1.0 / D), x4, gv)
        dx_ref[...] = dx
        dpe_ref[...] = (dx * gate).astype(BF16)
        dzg_ref[...] = (dx * pev * gate * (1.0 - gate)).astype(BF16)

        @pl.when(pl.program_id(0) == 0)
        def _():
            dg_ref[...] = jnp.zeros_like(dg_ref)
            loss_ref[...] = jnp.zeros_like(loss_ref)

        dg_ref[...] += dg
        loss_ref[...] += jnp.broadcast_to(part, loss_ref.shape)

    return pl.pallas_call(
        body, grid=(T // ROWS,), in_specs=[_row_spec(D), _row_spec(D), _row_spec(D), _vec_spec(D), _row_spec(D)],
        out_specs=[_vec_spec(128), _row_spec(D), _row_spec(D), _row_spec(D), _vec_spec(D)],
        out_shape=[jax.ShapeDtypeStruct((1, 128), F32), jax.ShapeDtypeStruct((T, D), F32), jax.ShapeDtypeStruct((T, D), BF16),
                   jax.ShapeDtypeStruct((T, D), BF16), jax.ShapeDtypeStruct((1, D), F32)],
        compiler_params=_params("arbitrary"), name="tail")(x3, zg, pe, g_final, target)


def _cast_bf16(x, name):
    R, C = x.shape
    rows = next(r for r in (ROWS, 128, 64, 32, 16) if R % r == 0)

    def body(x_ref, o_ref):
        o_ref[...] = x_ref[...].astype(BF16)

    spec = pl.BlockSpec((rows, C), lambda i: (i, 0))
    return pl.pallas_call(body, grid=(R // rows,), in_specs=[spec], out_specs=spec, out_shape=jax.ShapeDtypeStruct((R, C), BF16),
                          compiler_params=_params("parallel"), name=name)(x)


def _head_spec(T, col0, heads=1):
    return pl.BlockSpec((T, heads * HEAD_DIM), lambda h, *_: (0, col0 + h))


SB_HEADS = 4


def _triangle(n, right):
    j = lax.broadcasted_iota(jnp.int32, (n, n), 0)
    s = lax.broadcasted_iota(jnp.int32, (n, n), 1)
    return jnp.where((j > s) if right else (j < s), 1.0, 0.0).astype(BF16)


def _lane_scan(x, tri):
    hi = x.astype(BF16)
    lo = (x - hi.astype(F32)).astype(BF16)
    return _dot(hi, tri) + _dot(lo, tri)


def _head_cols(ref, rows, hh):
    return ref[rows, hh * HEAD_DIM:(hh + 1) * HEAD_DIM]


def _sb_tile(qv, kk, past, c_lk, tri):
    z = _dot(qv, kk, NT) * SCALE
    sp = jnp.log(1.0 + jnp.exp(-jnp.abs(z)))
    ls_pos = jnp.minimum(z, 0.0) - sp
    lk = jnp.minimum(-z, 0.0) - sp
    if past is not None:
        lk = jnp.where(past, lk, 0.0)
    right = c_lk + _lane_scan(lk, tri)
    a = jnp.exp(ls_pos + right)
    if past is not None:
        a = jnp.where(past, a, 0.0)
    return ls_pos, a, right[:, 0:1] + lk[:, 0:1]


def _sb_diagonal():
    B = SB_BLOCK
    return lax.broadcasted_iota(jnp.int32, (B, B), 1) < lax.broadcasted_iota(jnp.int32, (B, B), 0)


def _sb_rows(kb):
    return pl.ds(pl.multiple_of(kb * SB_BLOCK, SB_BLOCK), SB_BLOCK)


def _sb_fwd(proj, n_heads, after=()):
    T = proj.shape[0]
    B, HP = SB_BLOCK, SB_HEADS
    assert n_heads % HP == 0

    def body(q_ref, k_ref, v_ref, *rest):
        y_ref = rest[-1]
        qb = pl.program_id(1)
        tri = _triangle(B, right=True)
        qv = [_head_cols(q_ref, slice(None), hh).astype(BF16) for hh in range(HP)]

        def tile(kb, carry, past):
            out = []
            for hh in range(HP):
                acc, c_lk = carry[hh]
                kk = _head_cols(k_ref, _sb_rows(kb), hh).astype(BF16)
                vv = _head_cols(v_ref, _sb_rows(kb), hh).astype(BF16)
                _, a, c_lk = _sb_tile(qv[hh], kk, past, c_lk, tri)
                out.append((acc + _dot(a.astype(BF16), vv), c_lk))
            return tuple(out)

        init = tuple((jnp.zeros((B, HEAD_DIM), F32), jnp.zeros((B, 1), F32)) for _ in range(HP))
        res = lax.fori_loop(1, qb + 1, lambda i, carry: tile(qb - i, carry, None), tile(qb, init, _sb_diagonal()))
        for hh in range(HP):
            y_ref[:, hh * HEAD_DIM:(hh + 1) * HEAD_DIM] = res[hh][0].astype(BF16)

    blk = pl.BlockSpec((B, HP * HEAD_DIM), lambda h, i: (i, h))
    G = n_heads // HP
    return pl.pallas_call(
        body, grid=(G, T // B),
        in_specs=[blk, _head_spec(T, G, HP), _head_spec(T, 2 * G, HP)] + [ANY] * len(after), out_specs=blk,
        out_shape=jax.ShapeDtypeStruct((T, n_heads * HEAD_DIM), BF16),
        compiler_params=_params("parallel", "arbitrary"), name="sb_fwd")(proj, proj, proj, *after)


def _sb_bwd(proj, dy, n_heads):
    T = proj.shape[0]
    B, HP = SB_BLOCK, SB_HEADS
    nq = T // B

    def body(q_ref, k_ref, v_ref, dy_ref, dq_ref, dk_ref, dv_ref, g_s, sig_s, dk_s, dv_s):
        qb = pl.program_id(1)

        @pl.when(qb == 0)
        def _():
            dk_s[...] = jnp.zeros_like(dk_s)
            dv_s[...] = jnp.zeros_like(dv_s)

        tri_r = _triangle(B, right=True)
        tri_l = _triangle(B, right=False)
        qv = [_head_cols(q_ref, slice(None), hh).astype(BF16) for hh in range(HP)]
        dyb = [_head_cols(dy_ref, slice(None), hh).astype(BF16) for hh in range(HP)]

        def sweep(kb, carry, past):
            out = []
            for hh in range(HP):
                kk = _head_cols(k_ref, _sb_rows(kb), hh).astype(BF16)
                vv = _head_cols(v_ref, _sb_rows(kb), hh).astype(BF16)
                ls_pos, a, c_lk = _sb_tile(qv[hh], kk, past, carry[hh], tri_r)
                g_s[hh, kb] = _dot(dyb[hh], vv, NT) * a
                sig_s[hh, kb] = jnp.exp(ls_pos)
                dv_s[_sb_rows(kb), hh * HEAD_DIM:(hh + 1) * HEAD_DIM] += _dot(a.astype(BF16), dyb[hh], TN)
                out.append(c_lk)
            return tuple(out)

        zeros = tuple(jnp.zeros((B, 1), F32) for _ in range(HP))
        lax.fori_loop(1, qb + 1, lambda i, carry: sweep(qb - i, carry, None), sweep(qb, zeros, _sb_diagonal()))

        def back(kb, carry, past):
            out = []
            for hh in range(HP):
                dq, c_g = carry[hh]
                kk = _head_cols(k_ref, _sb_rows(kb), hh).astype(BF16)
                g, sig = g_s[hh, kb], sig_s[hh, kb]
                left = c_g + _lane_scan(g, tri_l)
                dz = g * (1.0 - sig) - left * sig
                if past is not None:
                    dz = jnp.where(past, dz, 0.0)
                dz = (dz * SCALE).astype(BF16)
                dk_s[_sb_rows(kb), hh * HEAD_DIM:(hh + 1) * HEAD_DIM] += _dot(dz, qv[hh], TN)
                out.append((dq + _dot(dz, kk), left[:, B - 1:B] + g[:, B - 1:B]))
            return tuple(out)

        init = tuple((jnp.zeros((B, HEAD_DIM), F32), jnp.zeros((B, 1), F32)) for _ in range(HP))
        res = back(qb, lax.fori_loop(0, qb, lambda kb, carry: back(kb, carry, None), init), _sb_diagonal())
        for hh in range(HP):
            dq_ref[:, hh * HEAD_DIM:(hh + 1) * HEAD_DIM] = res[hh][0].astype(BF16)

        @pl.when(qb == nq - 1)
        def _():
            dk_ref[...] = dk_s[...].astype(BF16)
            dv_ref[...] = dv_s[...].astype(BF16)

    blk = pl.BlockSpec((B, HP * HEAD_DIM), lambda h, i: (i, h))
    G = n_heads // HP
    full = _head_spec(T, 0, HP)
    shp = jax.ShapeDtypeStruct((T, n_heads * HEAD_DIM), BF16)
    return pl.pallas_call(
        body, grid=(G, nq),
        in_specs=[blk, _head_spec(T, G, HP), _head_spec(T, 2 * G, HP), blk], out_specs=[blk, full, full],
        out_shape=[shp, shp, shp],
        scratch_shapes=[pltpu.VMEM((HP, nq, B, B), F32)] * 2 + [pltpu.VMEM((T, HP * HEAD_DIM), F32)] * 2,
        compiler_params=_params("parallel", "arbitrary"), name="sb_bwd")(proj, proj, proj, dy)


DIAGS = PBAND + PAIR


def _diag_onehot():
    d = lax.broadcasted_iota(jnp.int32, (DIAGS, 2 * PAIR), 0)
    r = lax.broadcasted_iota(jnp.int32, (DIAGS, 2 * PAIR), 1)
    return jnp.where(jnp.clip(d - PAIR - PAD, -REL_CLIP, CHUNK - 1) + REL_CLIP == r, 1.0, 0.0)


def _bias_expand(rel_bias):
    H = rel_bias.shape[0]
    table = jnp.pad(rel_bias, ((0, 0), (0, 2 * PAIR - N_REL)))

    def body(rb_ref, o_ref):
        o_ref[...] = lax.dot_general(rb_ref[...], _diag_onehot(), NT, precision=lax.Precision.HIGHEST, preferred_element_type=F32)

    per_diag = pl.pallas_call(body, out_shape=jax.ShapeDtypeStruct((H, DIAGS), F32), name="bias_expand")(table)
    flat = jnp.tile(jnp.pad(per_diag, ((0, 0), (0, 1))), (1, PAIR))[:, :PAIR * DIAGS]
    return flat.reshape(H, PAIR, DIAGS)[:, :, PAIR:]


def _bias_reduce(dbias):
    H = dbias.shape[0]
    padded = jnp.pad(dbias, ((0, 0), (0, 1), (PAIR, 0))).reshape(H, -1)
    skewed = padded[:, :PAIR * (DIAGS + 1)].reshape(H, PAIR, DIAGS + 1)[:, :, :DIAGS]

    def body(s_ref, o_ref):
        per_diag = jnp.sum(s_ref[...], axis=0, keepdims=True)
        o_ref[...] = lax.dot_general(jnp.broadcast_to(per_diag, (8, DIAGS)), _diag_onehot(), NN, precision=lax.Precision.HIGHEST,
                                     preferred_element_type=F32)[0:1]

    return pl.pallas_call(
        body, grid=(H,), in_specs=[pl.BlockSpec((None, PAIR, DIAGS), lambda h: (h, 0, 0))],
        out_specs=pl.BlockSpec((None, 1, 2 * PAIR), lambda h: (h, 0, 0)),
        out_shape=jax.ShapeDtypeStruct((H, 1, 2 * PAIR), F32), compiler_params=_params("parallel"), name="bias_reduce")(skewed)[:, 0]


CA_HEADS = 2


def _ca_mask():
    i = lax.broadcasted_iota(jnp.int32, (PAIR, PBAND), 0)
    j = lax.broadcasted_iota(jnp.int32, (PAIR, PBAND), 1)
    qc, kc = i // CHUNK, j // CHUNK
    return j, (kc >= qc) & (kc <= qc + LEFT_CHUNKS)


def _ca_weights(pr, qp, kb, bias, j, window):
    valid = window & (pr * PAIR + j >= PAD)
    z = jnp.where(valid, _dot(qp, kb, NT) * SCALE + bias, NEG)
    e = jnp.exp(z - jnp.max(z, axis=1, keepdims=True))
    return e / jnp.sum(e, axis=1, keepdims=True)


def _ca_fill(k_ref, v_ref, kpad, vpad):
    T, W = k_ref.shape
    kpad[0:PAD, :] = jnp.zeros((PAD, W), BF16)
    vpad[0:PAD, :] = jnp.zeros((PAD, W), BF16)
    kpad[PAD:PAD + T, :] = k_ref[...].astype(BF16)
    vpad[PAD:PAD + T, :] = v_ref[...].astype(BF16)


def _ca_fwd(proj, bias, n_heads, col0):
    T = proj.shape[0]
    HP = CA_HEADS
    G = n_heads // HP
    assert n_heads % HP == 0 and col0 % HP == 0

    def body(q_ref, k_ref, v_ref, b_ref, y_ref, kpad, vpad):
        _ca_fill(k_ref, v_ref, kpad, vpad)
        j, window = _ca_mask()

        def pair(pr, _):
            r0 = pl.multiple_of(pr * PAIR, PAIR)
            for hh in range(HP):
                qp = _head_cols(q_ref, pl.ds(r0, PAIR), hh).astype(BF16)
                kb = _head_cols(kpad, pl.ds(r0, PBAND), hh)
                vb = _head_cols(vpad, pl.ds(r0, PBAND), hh)
                w = _ca_weights(pr, qp, kb, b_ref[hh], j, window)
                y_ref[pl.ds(r0, PAIR), hh * HEAD_DIM:(hh + 1) * HEAD_DIM] = _dot(w.astype(BF16), vb).astype(BF16)
            return 0

        lax.fori_loop(0, T // PAIR, pair, 0)

    c = col0 // HP
    return pl.pallas_call(
        body, grid=(G,),
        in_specs=[_head_spec(T, c, HP), _head_spec(T, c + G, HP), _head_spec(T, c + 2 * G, HP),
                  pl.BlockSpec((HP, PAIR, PBAND), lambda h: (h, 0, 0))],
        out_specs=_head_spec(T, 0, HP), out_shape=jax.ShapeDtypeStruct((T, n_heads * HEAD_DIM), BF16),
        scratch_shapes=[pltpu.VMEM((PAD + T, HP * HEAD_DIM), BF16)] * 2,
        compiler_params=_params("parallel"), name="ca_fwd")(proj, proj, proj, bias)


def _ca_bwd(proj, bias, dy, n_heads, col0):
    T = proj.shape[0]
    HP = CA_HEADS
    G = n_heads // HP

    def body(q_ref, k_ref, v_ref, b_ref, dy_ref, dq_ref, dk_ref, dv_ref, db_ref, kpad, vpad, dkpad, dvpad):
        _ca_fill(k_ref, v_ref, kpad, vpad)
        dkpad[...] = jnp.zeros_like(dkpad)
        dvpad[...] = jnp.zeros_like(dvpad)
        db_ref[...] = jnp.zeros_like(db_ref)
        j, window = _ca_mask()

        def pair(pr, _):
            r0 = pl.multiple_of(pr * PAIR, PAIR)
            for hh in range(HP):
                cols = slice(hh * HEAD_DIM, (hh + 1) * HEAD_DIM)
                qp = _head_cols(q_ref, pl.ds(r0, PAIR), hh).astype(BF16)
                kb = _head_cols(kpad, pl.ds(r0, PBAND), hh)
                vb = _head_cols(vpad, pl.ds(r0, PBAND), hh)
                w = _ca_weights(pr, qp, kb, b_ref[hh], j, window)
                dyp = _head_cols(dy_ref, pl.ds(r0, PAIR), hh).astype(BF16)
                dw = _dot(dyp, vb, NT)
                dz = w * (dw - jnp.sum(dw * w, axis=1, keepdims=True))
                db_ref[hh] += dz
                dzs = (dz * SCALE).astype(BF16)
                dq_ref[pl.ds(r0, PAIR), cols] = _dot(dzs, kb).astype(BF16)
                dkpad[pl.ds(r0, PBAND), cols] += _dot(dzs, qp, TN)
                dvpad[pl.ds(r0, PBAND), cols] += _dot(w.astype(BF16), dyp, TN)
            return 0

        lax.fori_loop(0, T // PAIR, pair, 0)
        dk_ref[...] = dkpad[PAD:PAD + T, :].astype(BF16)
        dv_ref[...] = dvpad[PAD:PAD + T, :].astype(BF16)

    c = col0 // HP
    full = _head_spec(T, 0, HP)
    bspec = pl.BlockSpec((HP, PAIR, PBAND), lambda h: (h, 0, 0))
    shp = jax.ShapeDtypeStruct((T, n_heads * HEAD_DIM), BF16)
    return pl.pallas_call(
        body, grid=(G,),
        in_specs=[_head_spec(T, c, HP), _head_spec(T, c + G, HP), _head_spec(T, c + 2 * G, HP), bspec, full],
        out_specs=[full, full, full, bspec],
        out_shape=[shp, shp, shp, jax.ShapeDtypeStruct((n_heads, PAIR, PBAND), F32)],
        scratch_shapes=[pltpu.VMEM((PAD + T, HP * HEAD_DIM), BF16)] * 2 + [pltpu.VMEM((PAD + T, HP * HEAD_DIM), F32)] * 2,
        compiler_params=_params("parallel"), name="ca_bwd")(proj, proj, proj, bias, dy)


def _local_step(x, p, target, comm, g):
    T, D = x.shape
    H = g["rel_bias"].shape[0]
    W = H * HEAD_DIM
    nb_in = comm.shapes["w_in_a"][2]
    nb_ff = comm.shapes["w_ffn_in"][2]
    nb_o = comm.shapes["w_sb_out"][2]
    nb_p = comm.shapes["w_ple_in"][2]
    tm = min(T, 1024)
    tn = min(D, 1024)
    gate_col = 6 * W // D

    h1 = _norm_fwd(x, g["g_mix"], "norm1")
    comm.stage("norm1", h1)
    proj = _mm(h1, comm.weight("w_in_a", h1), mode="nn", tm=tm, tn=nb_in, tk=D // 2, out_dtype=F32, b_blocked=True,
               after=comm.pending(), name="mm_in_a")
    comm.stage("mm_in_a", proj)
    proj = _mm(h1, comm.weight("w_in_b", proj), mode="nn", tm=tm, tn=nb_in, tk=D // 2, out_dtype=F32, b_blocked=True, a_cols=(1, 1),
               res=proj, after=comm.pending(), name="mm_in")
    comm.stage("mm_in", proj)
    y_sb = _sb_fwd(proj, H, comm.pending())
    bias = _bias_expand(g["rel_bias"])
    y_ca = _ca_fwd(proj, bias, H, 3 * H)
    comm.stage("attention", y_sb, y_ca)
    a_sb, a_ca, merged = _mm_merge(y_sb, y_ca, comm.weight("w_sb_out", y_ca), comm.weight("w_ca_out"), proj, gate_col, min(T, 512), tn, comm.pending())
    x2 = _mm(merged, comm.weight("w_mix_out"), mode="nn", tm=tm, tn=tn, tk=D, out_dtype=F32, res=x, name="mm_mix")
    h2 = _norm_fwd(x2, g["g_ffn"], "norm2")
    gate, up, act = _mm_swiglu(h2, comm.weight("w_ffn_in", h2), min(T, 512))
    comm.stage("mm_ffn_in", act)
    F = act.shape[1]
    tkf = F // 2 if F % 256 == 0 else F
    x3 = _mm(act, comm.weight("w_ffn_out", act), mode="nn", tm=tm, tn=tn, tk=tkf, out_dtype=F32, res=x2, after=comm.pending(), name="mm_ffn_out")
    h3 = _norm_fwd(x3, g["g_ple"], "norm3")
    zg = _mm(h3, comm.weight("w_ple_gate"), mode="nn", tm=tm, tn=tn, tk=D, out_dtype=F32, name="mm_ple_gate")
    pb = _cast_bf16(p, "cast_p")
    P = p.shape[1]
    pe = _mm(pb, comm.weight("w_ple_in"), mode="nn", tm=tm, tn=nb_p, tk=P, out_dtype=F32, b_blocked=True, name="mm_ple_in")
    loss, dx4, dpe, dzg, dg_final = _tail(x3, zg, pe, g["g_final"], target)

    tw = min(D, 1024)
    DW = BF16
    comm.grad("w_ple_in", _mm(pb, dpe, mode="tn", tm=P, tn=nb_p, tk=T, out_dtype=DW, out_block=nb_p, name="mm_d_ple_in"))
    comm.grad("w_ple_gate", _mm(h3, dzg, mode="tn", tm=tw, tn=tn, tk=T, out_dtype=DW, name="mm_d_ple_gate"))
    dh3 = _mm(dzg, comm.weight("w_ple_gate"), mode="nt", tm=tm, tn=tn, tk=D, out_dtype=F32, after=comm.pending(), name="mm_dh3")
    dx3, dx3b, dg_ple = _norm_bwd(dh3, x3, g["g_ple"], dx4, "norm3_bwd")
    comm.grad("w_ffn_out", _mm(act, dx3b, mode="tn", tm=F // 4, tn=tn, tk=T, out_dtype=DW, name="mm_d_ffn_out"))
    dact = _mm(dx3b, comm.weight("w_ffn_out"), mode="nt", tm=tm, tn=tkf, tk=D, out_dtype=F32, after=comm.pending(), name="mm_dact")
    dgu = _swiglu_bwd(dact, gate, up)
    comm.grad("w_ffn_in", _mm(h2, dgu, mode="tn", tm=tw, tn=nb_ff, tk=T, out_dtype=DW, out_block=nb_ff, name="mm_d_ffn_in"))
    dh2 = _mm(dgu, comm.weight("w_ffn_in"), mode="nt", tm=tm, tn=D, tk=nb_ff, out_dtype=F32, b_blocked=True, after=comm.pending(), name="mm_dh2")
    dx2, dx2b, dg_ffn = _norm_bwd(dh2, x2, g["g_ffn"], dx3, "norm2_bwd")
    dmerged = _mm(dx2b, comm.weight("w_mix_out"), mode="nt", tm=tm, tn=tn, tk=D, out_dtype=F32, name="mm_dmerged")
    da_sb, da_ca, dgate_sb, dgate_ca = _merge_bwd(dmerged, proj, a_sb, a_ca, D, gate_col)
    comm.grad("w_mix_out", _mm(merged, dx2b, mode="tn", tm=tw, tn=tn, tk=T, out_dtype=DW, name="mm_d_mix"))
    comm.grad("w_sb_out", _mm(y_sb, da_sb, mode="tn", tm=min(W, 512), tn=tn, tk=T, out_dtype=DW, out_block=nb_o, name="mm_d_sb_out"))
    comm.grad("w_ca_out", _mm(y_ca, da_ca, mode="tn", tm=min(W, 512), tn=tn, tk=T, out_dtype=DW, out_block=nb_o, name="mm_d_ca_out"))
    dy_sb = _mm(da_sb, comm.weight("w_sb_out"), mode="nt", tm=tm, tn=W, tk=tn, out_dtype=F32, b_blocked=True, after=comm.pending(), name="mm_dy_sb")
    dy_ca = _mm(da_ca, comm.weight("w_ca_out"), mode="nt", tm=tm, tn=W, tk=tn, out_dtype=F32, b_blocked=True, name="mm_dy_ca")
    dq_sb, dk_sb, dv_sb = _sb_bwd(proj, dy_sb, H)
    dq_ca, dk_ca, dv_ca, dbias = _ca_bwd(proj, bias, dy_ca, H, 3 * H)
    d_rel = _bias_reduce(dbias)[:, :N_REL]
    dproj = jnp.concatenate([dq_sb, dk_sb, dv_sb, dq_ca, dk_ca, dv_ca, dgate_sb, dgate_ca], axis=1)
    comm.grad("w_in", _mm(h1, dproj, mode="tn", tm=tw, tn=nb_in, tk=T, out_dtype=DW, out_block=nb_in, name="mm_d_in"))
    dh1 = _mm(dproj, comm.weight("w_in_a"), mode="nt", tm=tm, tn=D // 2, tk=nb_in, out_dtype=F32, b_blocked=True, out_cols=(2, 1, 0),
              after=comm.pending(), name="mm_dh1_a")
    dh1 = _mm(dproj, comm.weight("w_in_b"), mode="nt", tm=tm, tn=D // 2, tk=nb_in, out_dtype=F32, b_blocked=True, out_cols=(2, 1, 1),
              into=dh1, name="mm_dh1")
    grad_x, _, dg_mix = _norm_bwd(dh1, x, g["g_mix"], dx2, "norm1_bwd")
    small = dict(g_mix=dg_mix, g_ffn=dg_ffn, g_ple=dg_ple, g_final=dg_final, rel_bias=d_rel)
    return loss, grad_x, small


def _position():
    x, y, c = lax.axis_index("x"), lax.axis_index("y"), lax.axis_index("c")
    return x, y, c


def _block_of(px, py, pc):
    return 4 * px + 2 * py + pc


def _flip(pos, k):
    x, y, c = pos
    return (1 - x if k & 4 else x, 1 - y if k & 2 else y, 1 - c if k & 1 else c)


HBM = pl.BlockSpec(memory_space=pltpu.HBM)
SEM = pl.BlockSpec(memory_space=pltpu.SEMAPHORE)
VMEM_SPEC = pl.BlockSpec(memory_space=pltpu.VMEM)
EFFECT = pltpu.SideEffectType.DATAFLOW_SIDE_EFFECTING
TOKEN = jax.ShapeDtypeStruct((8, 128), F32)


def _hbm(a):
    return pltpu.HBM(a.shape, a.dtype)


def _landing(shape, dtype):
    return pltpu.with_memory_space_constraint(lax.empty(shape, dtype), pltpu.HBM)


def _gather_start(lands, after, name):
    n = len(lands)

    def body(*refs):
        ins = refs[:n]
        send, recv = refs[n + 1], refs[n + 2]
        token = refs[-1]
        x, y, c = _position()
        mine = _block_of(x, y, c)
        peers = [(x, y, 1 - c), (1 - x, y, c), (x, 1 - y, c), (1 - x, 1 - y, c)]
        for wi in range(n):
            for k, peer in enumerate(peers):
                pltpu.make_async_remote_copy(
                    src_ref=ins[wi].at[mine], dst_ref=ins[wi].at[mine], send_sem=send.at[4 * wi + k], recv_sem=recv.at[4 * wi + k],
                    device_id=peer, device_id_type=MESH).start()
        token[...] = jnp.zeros_like(token)

    outs = pl.pallas_call(
        body, name=name, in_specs=[HBM] * n + [ANY], out_specs=(SEM, SEM, *[HBM] * n, VMEM_SPEC),
        out_shape=(pltpu.SemaphoreType.DMA((4 * n,)), pltpu.SemaphoreType.DMA((4 * n,)), *[_hbm(a) for a in lands], TOKEN),
        input_output_aliases={i: 2 + i for i in range(n)},
        compiler_params=pltpu.CompilerParams(has_side_effects=EFFECT))(*[pltpu.with_memory_space_constraint(a, pltpu.HBM) for a in lands], after)
    return outs[0], outs[1], list(outs[2:2 + n]), outs[-1]


def _gather_forward(lands, send0, recv0, after, name):
    n = len(lands)

    def body(*refs):
        ins = refs[:n]
        send0, recv0 = refs[n], refs[n + 1]
        send1, recv1 = refs[n + 2 + len(after)], refs[n + 3 + len(after)]
        token = refs[-1]
        x, y, c = _position()
        chips = [(1 - x, y), (x, 1 - y), (1 - x, 1 - y)]
        for wi in range(n):
            for j, chip in enumerate(chips):
                rows = ins[wi].at[_block_of(*chip, c)]
                pltpu.make_async_remote_copy(
                    src_ref=rows, dst_ref=rows, send_sem=send0.at[4 * wi + 1 + j], recv_sem=recv0.at[4 * wi + 1 + j],
                    device_id=(*chip, c), device_id_type=MESH).wait_recv()
                pltpu.make_async_remote_copy(
                    src_ref=rows, dst_ref=rows, send_sem=send1.at[3 * wi + j], recv_sem=recv1.at[3 * wi + j],
                    device_id=(x, y, 1 - c), device_id_type=MESH).start()
        token[...] = jnp.zeros_like(token)

    outs = pl.pallas_call(
        body, name=name, in_specs=[HBM] * n + [SEM, SEM] + [ANY] * len(after), out_specs=(SEM, SEM, *[HBM] * n, VMEM_SPEC),
        out_shape=(pltpu.SemaphoreType.DMA((3 * n,)), pltpu.SemaphoreType.DMA((3 * n,)), *[_hbm(a) for a in lands], TOKEN),
        input_output_aliases={i: 2 + i for i in range(n)},
        compiler_params=pltpu.CompilerParams(has_side_effects=EFFECT))(*lands, send0, recv0, *after)
    return outs[0], outs[1], list(outs[2:2 + n]), outs[-1]


def _gather_wait(lands, send0, recv0, send1, recv1, after, name):
    n = len(lands)

    def body(*refs):
        ins = refs[:n]
        send0, recv0, send1, recv1 = refs[n:n + 4]
        x, y, c = _position()
        mine = _block_of(x, y, c)
        sibling = (x, y, 1 - c)
        peers = [sibling, (1 - x, y, c), (x, 1 - y, c), (1 - x, 1 - y, c)]
        chips = [(1 - x, y), (x, 1 - y), (1 - x, 1 - y)]
        for wi in range(n):
            own = ins[wi].at[mine]
            for k, peer in enumerate(peers):
                pltpu.make_async_remote_copy(src_ref=own, dst_ref=own, send_sem=send0.at[4 * wi + k], recv_sem=recv0.at[4 * wi + k],
                                             device_id=peer, device_id_type=MESH).wait_send()
            theirs = ins[wi].at[_block_of(*sibling)]
            pltpu.make_async_remote_copy(src_ref=theirs, dst_ref=theirs, send_sem=send0.at[4 * wi], recv_sem=recv0.at[4 * wi],
                                         device_id=sibling, device_id_type=MESH).wait_recv()
            for j, chip in enumerate(chips):
                sent = ins[wi].at[_block_of(*chip, c)]
                got = ins[wi].at[_block_of(*chip, 1 - c)]
                pltpu.make_async_remote_copy(src_ref=sent, dst_ref=sent, send_sem=send1.at[3 * wi + j], recv_sem=recv1.at[3 * wi + j],
                                             device_id=sibling, device_id_type=MESH).wait_send()
                pltpu.make_async_remote_copy(src_ref=got, dst_ref=got, send_sem=send1.at[3 * wi + j], recv_sem=recv1.at[3 * wi + j],
                                             device_id=sibling, device_id_type=MESH).wait_recv()

    outs = pl.pallas_call(
        body, name=name, in_specs=[HBM] * n + [SEM] * 4 + [ANY], out_specs=tuple([HBM] * n),
        out_shape=tuple(_hbm(a) for a in lands), input_output_aliases={i: i for i in range(n)},
        compiler_params=pltpu.CompilerParams(has_side_effects=EFFECT))(*lands, send0, recv0, send1, recv1, after)
    return list(outs)


def _plan_direct(me):
    return [(_block_of(*_flip(me, k)), k - 1, _flip(me, k)) for k in range(1, N_DEV)]


def _plan_sibling(me):
    x, y, c = me
    return [(_block_of(ci // 2, ci % 2, 1 - c), ci, (x, y, 1 - c)) for ci in range(4)]


def _plan_chips(me):
    x, y, c = me
    out = []
    for k in range(1, 4):
        px, py = (1 - x if k & 2 else x), (1 - y if k & 1 else y)
        out.append((2 * px + py, k - 1, (px, py, c)))
    return out


def _exchange_start(blocks, plan, name):
    n = len(blocks)
    slots = len(plan((0, 0, 0)))

    def body(*refs):
        srcs, lands = refs[:n], refs[n:2 * n]
        send, recv = refs[2 * n], refs[2 * n + 1]
        token = refs[-1]
        for wi in range(n):
            for block, slot, peer in plan(_position()):
                pltpu.make_async_remote_copy(
                    src_ref=srcs[wi].at[block], dst_ref=lands[wi].at[slot], send_sem=send.at[slots * wi + slot],
                    recv_sem=recv.at[slots * wi + slot], device_id=peer, device_id_type=MESH).start()
        token[...] = jnp.zeros_like(token)

    zones = [_landing((slots,) + b.shape[1:], b.dtype) for b in blocks]
    outs = pl.pallas_call(
        body, name=name, in_specs=[HBM] * (2 * n), out_specs=(SEM, SEM, *[HBM] * (2 * n), VMEM_SPEC),
        out_shape=(pltpu.SemaphoreType.DMA((slots * n,)), pltpu.SemaphoreType.DMA((slots * n,)), *[_hbm(a) for a in blocks],
                   *[_hbm(z) for z in zones], TOKEN),
        input_output_aliases={i: 2 + i for i in range(2 * n)},
        compiler_params=pltpu.CompilerParams(has_side_effects=EFFECT))(
            *[pltpu.with_memory_space_constraint(b, pltpu.HBM) for b in blocks], *zones)
    return outs[0], outs[1], list(outs[2:2 + n]), list(outs[2 + n:2 + 2 * n]), outs[-1]


def _exchange_wait(groups, plan, after, name):
    flat, counts = [], []
    for send, recv, blocks, zones in groups:
        flat += [*blocks, *zones, send, recv]
        counts.append(len(blocks))
    slots = len(plan((0, 0, 0)))

    def body(*refs):
        pos = 0
        for n in counts:
            srcs, lands = refs[pos:pos + n], refs[pos + n:pos + 2 * n]
            send, recv = refs[pos + 2 * n], refs[pos + 2 * n + 1]
            pos += 2 * n + 2
            for wi in range(n):
                for block, slot, peer in plan(_position()):
                    cp = pltpu.make_async_remote_copy(
                        src_ref=srcs[wi].at[block], dst_ref=lands[wi].at[slot], send_sem=send.at[slots * wi + slot],
                        recv_sem=recv.at[slots * wi + slot], device_id=peer, device_id_type=MESH)
                    cp.wait_send()
                    cp.wait_recv()

    in_specs, out_specs, out_shape, aliases = [], [], [], {}
    i = 0
    for n, (send, recv, blocks, zones) in zip(counts, groups):
        for a in (*blocks, *zones):
            aliases[i] = len(out_shape)
            in_specs.append(HBM)
            out_specs.append(HBM)
            out_shape.append(_hbm(a))
            i += 1
        in_specs += [SEM, SEM]
        i += 2
    outs = pl.pallas_call(
        body, name=name, in_specs=in_specs + [ANY], out_specs=tuple(out_specs), out_shape=tuple(out_shape),
        input_output_aliases=aliases, compiler_params=pltpu.CompilerParams(has_side_effects=EFFECT))(*flat, after)
    res, pos = [], 0
    for n in counts:
        res.append((list(outs[pos:pos + n]), list(outs[pos + n:pos + 2 * n])))
        pos += 2 * n
    return res


def _sibling_sum(blocks, zone, core, name):
    _, R, C = zone.shape
    rt = next(r for r in (R, R // 2, R // 4, 128, 64) if R % r == 0 and r % 16 == 0 and r * C <= 4 * 1024 * 1024)

    def body(core_ref, own_ref, z_ref, o_ref):
        o_ref[...] = (own_ref[...].astype(F32) + z_ref[...].astype(F32)).astype(o_ref.dtype)

    grid_spec = pltpu.PrefetchScalarGridSpec(
        num_scalar_prefetch=1, grid=(4, R // rt),
        in_specs=[pl.BlockSpec((None, rt, C), lambda ci, i, core_ref: (2 * ci + core_ref[0], i, 0)),
                  pl.BlockSpec((None, rt, C), lambda ci, i, core_ref: (ci, i, 0))],
        out_specs=pl.BlockSpec((None, rt, C), lambda ci, i, core_ref: (ci, i, 0)))
    return pl.pallas_call(body, grid_spec=grid_spec, out_shape=jax.ShapeDtypeStruct(zone.shape, zone.dtype),
                          compiler_params=_params("parallel", "parallel"), name=name)(core, blocks, zone)


def _adamw(w, g, m, v):
    m = ADAM_B1 * m + (1.0 - ADAM_B1) * g
    v = ADAM_B2 * v + (1.0 - ADAM_B2) * (g * g)
    m_hat = m / (1.0 - ADAM_B1 ** ADAM_STEP)
    v_hat = v / (1.0 - ADAM_B2 ** ADAM_STEP)
    delta = -ADAM_LR * (m_hat / (jnp.sqrt(v_hat) + ADAM_EPS) + ADAM_WD * w)
    return delta, m, v


def _reduce_adamw(blocks, zone, mine, w, m, v, name):
    R, C = w.shape
    rt = 128 if R % 128 == 0 else 64
    assert R % rt == 0

    def body(mine_ref, own_ref, z_ref, w_ref, m_ref, v_ref, g_out, d_out, m_out, v_out):
        g = own_ref[...].astype(F32)
        for s in range(zone.shape[0]):
            g = g + z_ref[s].astype(F32)
        delta, m2, v2 = _adamw(w_ref[...], g, m_ref[...], v_ref[...])
        g_out[...] = g
        d_out[...] = delta
        m_out[...] = m2
        v_out[...] = v2

    spec = pl.BlockSpec((rt, C), lambda i, mine_ref: (i, 0))
    grid_spec = pltpu.PrefetchScalarGridSpec(
        num_scalar_prefetch=1, grid=(R // rt,),
        in_specs=[pl.BlockSpec((None, rt, C), lambda i, mine_ref: (mine_ref[0], i, 0)),
                  pl.BlockSpec((zone.shape[0], rt, C), lambda i, mine_ref: (0, i, 0)), spec, spec, spec],
        out_specs=[spec] * 4)
    return pl.pallas_call(body, grid_spec=grid_spec, out_shape=[jax.ShapeDtypeStruct((R, C), F32)] * 4,
                          compiler_params=_params("parallel"), name=name)(mine, blocks, zone, w, m, v)


def _small_step(part, w, m, v, after):
    R, C = part.shape

    def body(part_ref, w_ref, m_ref, v_ref, *rest):
        g_out, d_out, m_out, v_out, gath, send, recv = rest[len(after):]
        me = _position()
        gath[_block_of(*me)] = part_ref[...]

        def copy(k, slot):
            return pltpu.make_async_remote_copy(
                src_ref=part_ref, dst_ref=gath.at[slot], send_sem=send.at[k - 1], recv_sem=recv.at[k - 1],
                device_id=_flip(me, k), device_id_type=MESH)

        sent = [copy(k, _block_of(*me)) for k in range(1, N_DEV)]
        for cp in sent:
            cp.start()
        for k in range(1, N_DEV):
            copy(k, _block_of(*_flip(me, k))).wait_recv()
        for cp in sent:
            cp.wait_send()
        g = gath[0]
        for s in range(1, N_DEV):
            g = g + gath[s]
        delta, m2, v2 = _adamw(w_ref[...], g, m_ref[...], v_ref[...])
        g_out[...] = g
        d_out[...] = delta
        m_out[...] = m2
        v_out[...] = v2

    vm = pl.BlockSpec(memory_space=pltpu.VMEM)
    return pl.pallas_call(
        body, in_specs=[vm] * 4 + [ANY] * len(after), out_specs=[vm] * 4, out_shape=[jax.ShapeDtypeStruct((R, C), F32)] * 4,
        scratch_shapes=[pltpu.VMEM((N_DEV, R, C), F32), pltpu.SemaphoreType.DMA((7,)), pltpu.SemaphoreType.DMA((7,))],
        name="small_step")(part, w, m, v, *after)


COLUMN_SHARDED = ("w_in", "w_sb_out", "w_ca_out", "w_ffn_in", "w_ple_in")
ROW_SHARDED = ("w_mix_out", "w_ffn_out", "w_ple_gate")
BIG = COLUMN_SHARDED + ROW_SHARDED
SMALL = ("g_mix", "g_ffn", "g_ple", "g_final", "rel_bias")
WEIGHTS = ("w_in", "w_sb_out", "w_ca_out", "w_mix_out", "rel_bias", "g_mix", "g_ffn", "g_ple", "g_final",
           "w_ffn_in", "w_ffn_out", "w_ple_in", "w_ple_gate")


def _pack_small(t, D):
    rows = [t[n].reshape(1, D) for n in SMALL[:4]]
    rb = t["rel_bias"].reshape(1, -1)
    rows.append(jnp.pad(rb, ((0, 0), (0, D - rb.shape[1]))))
    return jnp.concatenate(rows + [jnp.zeros((8 - len(rows), D), F32)], axis=0)


def _unpack_small(a, like):
    out = {n: a[i].reshape(like[n].shape) for i, n in enumerate(SMALL[:4])}
    out["rel_bias"] = a[4, :like["rel_bias"].size].reshape(like["rel_bias"].shape)
    return out


GATHER_GROUPS = (("w_in_a",), ("w_in_b",), ("w_sb_out", "w_ca_out", "w_mix_out"), ("w_ffn_in",), ("w_ffn_out", "w_ple_gate", "w_ple_in"))
FORWARD_AFTER = ("norm1", "mm_in_a", "mm_in", "attention", "mm_ffn_in")
GRAD_GROUPS = (("w_ple_in", "w_ple_gate"), ("w_ffn_out",), ("w_ffn_in",), ("w_mix_out", "w_sb_out", "w_ca_out"), ("w_in",))


class _Exchange:
    def __init__(self, shards):
        me = _position()
        self.mine = _block_of(*me)
        self.chip = jnp.reshape(2 * me[0] + me[1], (1,)).astype(jnp.int32)
        self.core = jnp.reshape(me[2], (1,)).astype(jnp.int32)
        self.device = jnp.reshape(self.mine, (1,)).astype(jnp.int32)
        self.shapes = {n: ((N_DEV * s.shape[0], s.shape[1]) if n in ROW_SHARDED else (N_DEV,) + s.shape) for n, s in shards.items()}
        self.tokens = []
        self.ready = {}
        self.gathers = []
        for gi, names in enumerate(GATHER_GROUPS):
            lands = [lax.dynamic_update_slice(lax.empty((N_DEV,) + shards[n].shape, BF16), shards[n][None], (self.mine, 0, 0))
                     for n in names]
            behind = self.tokens[-1] if self.tokens else shards[names[0]]
            send0, recv0, lands, token = _gather_start(lands, behind, f"gather_start_{gi}")
            self.tokens.append(token)
            self.gathers.append(dict(names=names, lands=lands, sems=(send0, recv0), token=token))
        self.grads = {}
        self.exchanges = []

    def pending(self):
        tokens, self.tokens = self.tokens, []
        return tokens

    def stage(self, tag, *made):
        gi = FORWARD_AFTER.index(tag)
        gth = self.gathers[gi]
        send1, recv1, lands, token = _gather_forward(gth["lands"], *gth["sems"], made + tuple(self.tokens), f"gather_forward_{gi}")
        gth.update(lands=lands, sems=gth["sems"] + (send1, recv1), token=token)
        self.tokens.append(token)

    def weight(self, name, after=None):
        if name not in self.ready:
            gi = next(i for i, names in enumerate(GATHER_GROUPS) if name in names)
            gth = self.gathers[gi]
            for n, a in zip(gth["names"], _gather_wait(gth["lands"], *gth["sems"], gth["token"] if after is None else after, f"gather_wait_{gi}")):
                self.ready[n] = a.reshape(self.shapes[n])
        return self.ready[name]

    def grad(self, name, blocks):
        self.grads[name] = blocks if name in COLUMN_SHARDED else blocks.reshape((N_DEV, -1, blocks.shape[-1]))
        names = next(names for names in GRAD_GROUPS if name in names)
        if not all(n in self.grads for n in names):
            return
        blocks = [self.grads[n] for n in names]
        last = names == GRAD_GROUPS[-1]
        if last:
            send, recv, blocks, zones, token = _exchange_start(blocks, _plan_sibling, "pair_start_" + names[0])
            (blocks, zones), = _exchange_wait([(send, recv, blocks, zones)], _plan_sibling, token, "pair_wait_" + names[0])
            blocks = [_sibling_sum(b, z, self.core, "pair_sum_" + n) for n, b, z in zip(names, blocks, zones)]
        plan = _plan_chips if last else _plan_direct
        send, recv, blocks, zones, token = _exchange_start(blocks, plan, "exchange_start_" + names[0])
        self.exchanges.append(dict(names=names, state=(send, recv, blocks, zones), plan=plan, own=self.chip if last else self.device))
        self.tokens.append(token)

    def collect(self, which, after, name):
        sel = [e for e in self.exchanges if GRAD_GROUPS.index(e["names"]) in which]
        out = {}
        for e, (blocks, zones) in zip(sel, _exchange_wait([e["state"] for e in sel], sel[0]["plan"], after, name)):
            out.update({n: (b, e["own"], z) for n, b, z in zip(e["names"], blocks, zones)})
        return out


def kernel(x, p, w_in, w_sb_out, w_ca_out, w_mix_out, rel_bias, g_mix, g_ffn, g_ple, g_final, w_ffn_in, w_ffn_out, w_ple_in, w_ple_gate, loss_target, m_w_in, m_w_sb_out, m_w_ca_out, m_w_mix_out, m_rel_bias, m_g_mix, m_g_ffn, m_g_ple, m_g_final, m_w_ffn_in, m_w_ffn_out, m_w_ple_in, m_w_ple_gate, v_w_in, v_w_sb_out, v_w_ca_out, v_w_mix_out, v_rel_bias, v_g_mix, v_g_ffn, v_g_ple, v_g_final, v_w_ffn_in, v_w_ffn_out, v_w_ple_in, v_w_ple_gate):
    wts = dict(w_in=w_in, w_sb_out=w_sb_out, w_ca_out=w_ca_out, w_mix_out=w_mix_out, rel_bias=rel_bias, g_mix=g_mix, g_ffn=g_ffn,
               g_ple=g_ple, g_final=g_final, w_ffn_in=w_ffn_in, w_ffn_out=w_ffn_out, w_ple_in=w_ple_in, w_ple_gate=w_ple_gate)
    mom = dict(w_in=m_w_in, w_sb_out=m_w_sb_out, w_ca_out=m_w_ca_out, w_mix_out=m_w_mix_out, rel_bias=m_rel_bias, g_mix=m_g_mix,
               g_ffn=m_g_ffn, g_ple=m_g_ple, g_final=m_g_final, w_ffn_in=m_w_ffn_in, w_ffn_out=m_w_ffn_out, w_ple_in=m_w_ple_in,
               w_ple_gate=m_w_ple_gate)
    var = dict(w_in=v_w_in, w_sb_out=v_w_sb_out, w_ca_out=v_w_ca_out, w_mix_out=v_w_mix_out, rel_bias=v_rel_bias, g_mix=v_g_mix,
               g_ffn=v_g_ffn, g_ple=v_g_ple, g_final=v_g_final, w_ffn_in=v_w_ffn_in, w_ffn_out=v_w_ffn_out, w_ple_in=v_w_ple_in,
               w_ple_gate=v_w_ple_gate)
    T, D = x.shape[1], x.shape[2]
    shard = {n: wts[n].reshape(wts[n].shape[-2:]) for n in BIG}
    bf = {n: _cast_bf16(shard[n], "cast_" + n) for n in BIG}
    half = bf["w_in"].shape[0] // 2
    bf["w_in_a"], bf["w_in_b"] = bf["w_in"][:half], bf.pop("w_in")[half:]
    comm = _Exchange(bf)
    g = dict(g_mix=g_mix.reshape(1, D), g_ffn=g_ffn.reshape(1, D), g_ple=g_ple.reshape(1, D), g_final=g_final.reshape(1, D),
             rel_bias=rel_bias.reshape(rel_bias.shape[-2:]))

    loss, grad_x, dsmall = _local_step(x.reshape(T, D), p.reshape(T, -1), loss_target.reshape(T, D), comm, g)
    loss = lax.psum(loss[0, 0], ("x", "y", "c"))

    grad, delta, new_m, new_v = {}, {}, {}, {}

    def update(parts):
        done = []
        for n, (blocks, own, zone) in parts.items():
            outs = _reduce_adamw(blocks, zone, own, shard[n], mom[n].reshape(shard[n].shape), var[n].reshape(shard[n].shape), "adamw_" + n)
            grad[n], delta[n], new_m[n], new_v[n] = [o.reshape(wts[n].shape) for o in outs]
            done.append(outs[0])
        return done

    done = update(comm.collect(range(len(GRAD_GROUPS) - 1), grad_x, "exchange_wait_rest"))
    outs = _small_step(_pack_small(dsmall, D), _pack_small(wts, D), _pack_small(mom, D), _pack_small(var, D), done)
    for dst, a in zip((grad, delta, new_m, new_v), outs):
        dst.update(_unpack_small(a, wts))
    update(comm.collect([len(GRAD_GROUPS) - 1], outs[0], "exchange_wait_w_in"))

    return (loss, grad_x.reshape(x.shape), *[grad[n] for n in WEIGHTS], *[delta[n] for n in WEIGHTS],
            *[new_m[n] for n in WEIGHTS], *[new_v[n] for n in WEIGHTS])
```

```python
import functools

import jax
import jax.numpy as jnp
from jax import lax
from jax.experimental import pallas as pl
from jax.experimental.pallas import tpu as pltpu

F32, BF16 = jnp.float32, jnp.bfloat16

N_DEV = 8
HEAD_DIM = 128
CHUNK = 64
LEFT_CHUNKS = 8
REL_CLIP = 128
N_REL = REL_CLIP + CHUNK
PAIR = 2 * CHUNK
PBAND = (LEFT_CHUNKS + 2) * CHUNK
PAD = LEFT_CHUNKS * CHUNK
SB_BLOCK = 256
ROWS = 256
EPS = 1e-6
NEG = -1e30
SCALE = HEAD_DIM ** -0.5
VMEM_LIMIT_BYTES = 56 * 1024 * 1024

ADAM_LR, ADAM_B1, ADAM_B2, ADAM_EPS, ADAM_WD, ADAM_STEP = 0.001, 0.9, 0.999, 1e-08, 0.01, 10

ANY = pl.BlockSpec(memory_space=pl.ANY)
NN = (((1,), (0,)), ((), ()))
NT = (((1,), (1,)), ((), ()))
TN = (((0,), (0,)), ((), ()))
MESH = pl.DeviceIdType.MESH


def _params(*sem):
    return pltpu.CompilerParams(dimension_semantics=sem or None, vmem_limit_bytes=VMEM_LIMIT_BYTES)


def _dot(a, b, dims=NN):
    return lax.dot_general(a, b, dims, preferred_element_type=F32)


def _mm(a, b, *, mode, tm, tn, tk, out_dtype, name, b_blocked=False, out_block=None, res=None, after=(), a_cols=(1, 0), out_cols=(1, 1, 0), into=None, b_first=0, b_count=None, o_first=0, o_count=None):
    bg = og = 1
    if mode == "nn":
        M, K = a.shape
        a_spec = pl.BlockSpec((tm, tk), lambda i, j, k: (i, k))
        if b_blocked:
            G, _, nb = b.shape
            N = G * nb
            if tn > nb:
                bg = tn // nb
                assert tn % nb == 0
                b_spec = pl.BlockSpec((bg, tk, nb), lambda i, j, k: (j, k, 0))
            else:
                per = nb // tn
                assert nb % tn == 0
                b_spec = pl.BlockSpec((None, tk, tn), lambda i, j, k: (j // per, k, j % per))
        else:
            N = b.shape[1]
            b_spec = pl.BlockSpec((tk, tn), lambda i, j, k: (k, j))
        dims = NN
    elif mode == "nt":
        M, K = a.shape
        a_spec = pl.BlockSpec((tm, tk), lambda i, j, k: (i, k))
        if b_blocked:
            G, N, nb = b.shape
            K = (b_count or G) * nb
            assert b_first == 0 or tk == nb
            if tk > nb:
                bg = tk // nb
                assert tk % nb == 0
                b_spec = pl.BlockSpec((bg, tn, nb), lambda i, j, k: (k, j, 0))
            else:
                per = nb // tk
                assert nb % tk == 0
                b_spec = pl.BlockSpec((None, tn, tk), lambda i, j, k: (b_first + k // per, j, k % per))
        else:
            N = b.shape[0]
            b_spec = pl.BlockSpec((tn, tk), lambda i, j, k: (j, k))
        dims = NT
    else:
        K, M = a.shape
        N = b.shape[1]
        a_spec = pl.BlockSpec((tk, tm), lambda i, j, k: (k, i))
        b_spec = pl.BlockSpec((tk, tn), lambda i, j, k: (k, j))
        dims = TN
    if mode != "tn":
        if mode == "nn":
            K = b.shape[-2]
        elif not b_blocked:
            K = b.shape[1]
        a_spec = pl.BlockSpec((tm, tk), lambda i, j, k: (i, k * a_cols[0] + a_cols[1]))
    assert M % tm == 0 and N % tn == 0 and K % tk == 0, (name, M, N, K, tm, tn, tk)
    nk = K // tk
    if out_block is None:
        out_shape = jax.ShapeDtypeStruct((M, N * out_cols[0]), out_dtype)
        o_spec = pl.BlockSpec((tm, tn), lambda i, j, k: (i, j * out_cols[1] + out_cols[2]))
    else:
        out_shape = jax.ShapeDtypeStruct((o_count or N // out_block, M, out_block), out_dtype)
        if tn > out_block:
            og = tn // out_block
            assert tn % out_block == 0 and o_first % og == 0
            o_spec = pl.BlockSpec((og, tm, out_block), lambda i, j, k: (o_first // og + j, i, 0))
        else:
            per_o = out_block // tn
            assert out_block % tn == 0
            o_spec = pl.BlockSpec((None, tm, tn), lambda i, j, k: (o_first + j // per_o, i, j % per_o))
    in_specs = [a_spec, b_spec]
    args = [a, b]
    if res is not None:
        in_specs.append(pl.BlockSpec((tm, tn), lambda i, j, k: (i, j * out_cols[1] + out_cols[2])))
        args.append(res)
    n_in = len(args) + len(after) + (into is not None)

    def product(a_ref, b_ref):
        if bg == 1:
            return _dot(a_ref[...], b_ref[...], dims)
        nb = b_ref.shape[2]
        if mode == "nn":
            return jnp.concatenate([_dot(a_ref[...], b_ref[g], dims) for g in range(bg)], axis=1)
        return sum(_dot(a_ref[:, g * nb:(g + 1) * nb], b_ref[g], dims) for g in range(bg))

    def body(*refs):
        a_ref, b_ref = refs[0], refs[1]
        r_ref = refs[2] if res is not None else None
        o_ref = refs[n_in]

        def finish(acc):
            if r_ref is not None:
                acc = acc + r_ref[...]
            if og == 1:
                o_ref[...] = acc.astype(o_ref.dtype)
            else:
                for g in range(og):
                    o_ref[g] = acc[:, g * out_block:(g + 1) * out_block].astype(o_ref.dtype)

        if nk == 1:
            finish(product(a_ref, b_ref))
        else:
            acc_ref = refs[-1]
            k = pl.program_id(2)

            @pl.when(k == 0)
            def _():
                acc_ref[...] = jnp.zeros_like(acc_ref)

            acc_ref[...] += product(a_ref, b_ref)

            @pl.when(k == nk - 1)
            def _():
                finish(acc_ref[...])

    return pl.pallas_call(
        body, grid=(M // tm, N // tn, nk), in_specs=in_specs + [ANY] * (n_in - len(args)), out_specs=o_spec, out_shape=out_shape,
        scratch_shapes=[] if nk == 1 else [pltpu.VMEM((tm, tn), F32)], input_output_aliases={} if into is None else {n_in - 1: 0},
        compiler_params=_params("parallel", "parallel", "arbitrary"), name=name)(*args, *after, *(() if into is None else (into,)))


def _mm_fused(a, b, tiles, fn, outs, *, mode, tm, tn, tk, name, sums=(), after=()):
    M, K = a.shape
    N = b.shape[1] if mode == "nn" else b.shape[0]
    nk = K // tk
    assert M % tm == 0 and N % tn == 0 and K % tk == 0 and (not sums or tn == N)
    b_spec = pl.BlockSpec((tk, tn), lambda i, j, k: (k, j)) if mode == "nn" else pl.BlockSpec((tn, tk), lambda i, j, k: (j, k))
    in_specs = [pl.BlockSpec((tm, tk), lambda i, j, k: (i, k)), b_spec]
    args = [a, b]
    for t in tiles:
        if isinstance(t, tuple):
            arr, off = t
            in_specs.append(pl.BlockSpec((tm, tn), lambda i, j, k, off=off: (i, off + j)))
        else:
            arr = t
            in_specs.append(pl.BlockSpec((1, tn), lambda i, j, k: (0, j)))
        args.append(arr)
    n_in = len(args) + len(after)
    n_out = len(outs) + len(sums)

    def body(*refs):
        a_ref, b_ref = refs[0], refs[1]
        t_refs = refs[2:2 + len(tiles)]
        o_refs = refs[n_in:n_in + n_out]

        def finish(acc):
            res = fn(acc, *[t[...] for t in t_refs])
            for o_ref, r in zip(o_refs[:len(outs)], res):
                o_ref[...] = r.astype(o_ref.dtype)
            if sums:
                @pl.when(pl.program_id(0) == 0)
                def _():
                    for o_ref in o_refs[len(outs):]:
                        o_ref[...] = jnp.zeros_like(o_ref)

                for o_ref, r in zip(o_refs[len(outs):], res[len(outs):]):
                    o_ref[...] += jnp.broadcast_to(r, o_ref.shape)

        if nk == 1:
            finish(_dot(a_ref[...], b_ref[...], NN if mode == "nn" else NT))
        else:
            acc_ref = refs[-1]
            k = pl.program_id(2)

            @pl.when(k == 0)
            def _():
                acc_ref[...] = jnp.zeros_like(acc_ref)

            acc_ref[...] += _dot(a_ref[...], b_ref[...], NN if mode == "nn" else NT)

            @pl.when(k == nk - 1)
            def _():
                finish(acc_ref[...])

    o_spec = pl.BlockSpec((tm, tn), lambda i, j, k: (i, j))
    return pl.pallas_call(
        body, grid=(M // tm, N // tn, nk), in_specs=in_specs + [ANY] * len(after),
        out_specs=[o_spec] * len(outs) + [pl.BlockSpec(sh, lambda i, j, k: (0, 0)) for sh in sums],
        out_shape=[jax.ShapeDtypeStruct((M, N), dt) for dt in outs] + [jax.ShapeDtypeStruct(sh, F32) for sh in sums],
        scratch_shapes=[] if nk == 1 else [pltpu.VMEM((tm, tn), F32)],
        compiler_params=_params("arbitrary" if sums else "parallel", "parallel", "arbitrary"), name=name)(*args, *after)


def _row_spec(d, col=0):
    return pl.BlockSpec((ROWS, d), lambda i: (i, col))


def _vec_spec(d):
    return pl.BlockSpec((1, d), lambda i: (0, 0))


def _rms(x):
    return lax.rsqrt(jnp.mean(x * x, axis=-1, keepdims=True) + EPS)


def _norm_fwd(x, g, name):
    T, D = x.shape

    def body(x_ref, g_ref, h_ref):
        xv = x_ref[...]
        h_ref[...] = (xv * _rms(xv) * g_ref[...]).astype(BF16)

    return pl.pallas_call(body, grid=(T // ROWS,), in_specs=[_row_spec(D), _vec_spec(D)], out_specs=_row_spec(D),
                          out_shape=jax.ShapeDtypeStruct((T, D), BF16), compiler_params=_params("parallel"), name=name)(x, g)


def _norm_bwd_math(dh, xv, gv):
    r = _rms(xv)
    xhat = xv * r
    dxhat = dh * gv
    dx = r * (dxhat - xhat * jnp.mean(dxhat * xhat, axis=-1, keepdims=True))
    dg = jnp.sum(dh * xhat, axis=0, keepdims=True)
    return dx, dg


def _norm_bwd(dh, x, g, dres, name):
    T, D = x.shape

    def body(dh_ref, x_ref, g_ref, dres_ref, dx_ref, dxb_ref, dg_ref):
        dx, dg = _norm_bwd_math(dh_ref[...], x_ref[...], g_ref[...])
        dx = dx + dres_ref[...]
        dx_ref[...] = dx
        dxb_ref[...] = dx.astype(BF16)

        @pl.when(pl.program_id(0) == 0)
        def _():
            dg_ref[...] = jnp.zeros_like(dg_ref)

        dg_ref[...] += dg

    return pl.pallas_call(
        body, grid=(T // ROWS,), in_specs=[_row_spec(D), _row_spec(D), _vec_spec(D), _row_spec(D)],
        out_specs=[_row_spec(D), _row_spec(D), _vec_spec(D)],
        out_shape=[jax.ShapeDtypeStruct((T, D), F32), jax.ShapeDtypeStruct((T, D), BF16), jax.ShapeDtypeStruct((1, D), F32)],
        compiler_params=_params("arbitrary"), name=name)(dh, x, g, dres)


def _mm_merge(y_sb, y_ca, w_sb, w_ca, proj, gate_col, tm, tn, after):
    T, W = y_sb.shape
    G, _, nb = w_sb.shape
    D, bg = G * nb, tn // nb
    assert tn % nb == 0 and D % tn == 0
    per = D // tn

    def body(ys_ref, yc_ref, ws_ref, wc_ref, gs_ref, gc_ref, *rest):
        as_ref, ac_ref, m_ref = rest[len(after):]
        a = jnp.concatenate([_dot(ys_ref[...], ws_ref[g]) for g in range(bg)], axis=1)
        b = jnp.concatenate([_dot(yc_ref[...], wc_ref[g]) for g in range(bg)], axis=1)
        as_ref[...] = a
        ac_ref[...] = b
        m_ref[...] = (jax.nn.sigmoid(gs_ref[...]) * a + jax.nn.sigmoid(gc_ref[...]) * b).astype(BF16)

    y_spec = pl.BlockSpec((tm, W), lambda i, j: (i, 0))
    w_spec = pl.BlockSpec((bg, W, nb), lambda i, j: (j, 0, 0))
    out = pl.BlockSpec((tm, tn), lambda i, j: (i, j))
    f32 = jax.ShapeDtypeStruct((T, D), F32)
    return pl.pallas_call(
        body, grid=(T // tm, per),
        in_specs=[y_spec, y_spec, w_spec, w_spec, pl.BlockSpec((tm, tn), lambda i, j: (i, gate_col * per + j)),
                  pl.BlockSpec((tm, tn), lambda i, j: (i, (gate_col + 1) * per + j))] + [ANY] * len(after),
        out_specs=[out, out, out], out_shape=[f32, f32, jax.ShapeDtypeStruct((T, D), BF16)],
        compiler_params=_params("parallel", "parallel"), name="mm_merge")(y_sb, y_ca, w_sb, w_ca, proj, proj, *after)


def _merge_bwd(dm, gs, gc, a, b):
    ss, sc = jax.nn.sigmoid(gs), jax.nn.sigmoid(gc)
    return dm * ss, dm * sc, dm * a * ss * (1.0 - ss), dm * b * sc * (1.0 - sc)


def _mm_swiglu(h, w, tm):
    T, D = h.shape
    G2, _, nb = w.shape
    G = G2 // 2

    def body(h_ref, wg_ref, wu_ref, g_ref, u_ref, act_ref):
        hv = h_ref[...]
        gv = _dot(hv, wg_ref[...])
        uv = _dot(hv, wu_ref[...])
        g_ref[...] = gv
        u_ref[...] = uv
        act_ref[...] = (gv * jax.nn.sigmoid(gv) * uv).astype(BF16)

    out = pl.BlockSpec((tm, nb), lambda j, i: (i, j))
    f32 = jax.ShapeDtypeStruct((T, G * nb), F32)
    return pl.pallas_call(
        body, grid=(G, T // tm),
        in_specs=[pl.BlockSpec((tm, D), lambda j, i: (i, 0)), pl.BlockSpec((None, D, nb), lambda j, i: (j, 0, 0)),
                  pl.BlockSpec((None, D, nb), lambda j, i: (j + G, 0, 0))],
        out_specs=[out, out, out], out_shape=[f32, f32, jax.ShapeDtypeStruct((T, G * nb), BF16)],
        compiler_params=_params("parallel", "parallel"), name="mm_ffn_in")(h, w, w)


def _swiglu_bwd(dact, gate, up):
    s = jax.nn.sigmoid(gate)
    return dact * up * s * (1.0 + gate * (1.0 - s)), dact * gate * s


def _tail(zg, x3, pe, target, g_final):
    D = x3.shape[-1]
    gate = jax.nn.sigmoid(zg)
    x4 = x3 + gate * pe
    err = x4 * _rms(x4) * g_final - target
    part = 0.5 * jnp.sum(jnp.mean(err * err, axis=-1, keepdims=True), axis=0, keepdims=True)
    dx, dg = _norm_bwd_math(err * (1.0 / D), x4, g_final)
    return dx, dx * gate, dx * pe * gate * (1.0 - gate), part, dg


def _cast_bf16(x, name):
    R, C = x.shape
    rows = next(r for r in (ROWS, 128, 64, 32, 16) if R % r == 0)

    def body(x_ref, o_ref):
        o_ref[...] = x_ref[...].astype(BF16)

    spec = pl.BlockSpec((rows, C), lambda i: (i, 0))
    return pl.pallas_call(body, grid=(R // rows,), in_specs=[spec], out_specs=spec, out_shape=jax.ShapeDtypeStruct((R, C), BF16),
                          compiler_params=_params("parallel"), name=name)(x)


def _head_spec(T, col0, heads=1):
    return pl.BlockSpec((T, heads * HEAD_DIM), lambda h, *_: (0, col0 + h))


SB_HEADS = 4


def _triangle(n, right):
    j = lax.broadcasted_iota(jnp.int32, (n, n), 0)
    s = lax.broadcasted_iota(jnp.int32, (n, n), 1)
    return jnp.where((j > s) if right else (j < s), 1.0, 0.0).astype(BF16)


def _lane_scan(x, tri):
    hi = x.astype(BF16)
    lo = (x - hi.astype(F32)).astype(BF16)
    return _dot(hi, tri) + _dot(lo, tri)


def _head_cols(ref, rows, hh):
    return ref[rows, hh * HEAD_DIM:(hh + 1) * HEAD_DIM]


def _sb_tile(qv, kk, past, c_lk, tri):
    z = _dot(qv, kk, NT) * SCALE
    sp = jnp.log(1.0 + jnp.exp(-jnp.abs(z)))
    ls_pos = jnp.minimum(z, 0.0) - sp
    lk = jnp.minimum(-z, 0.0) - sp
    if past is not None:
        lk = jnp.where(past, lk, 0.0)
    right = c_lk + _lane_scan(lk, tri)
    a = jnp.exp(ls_pos + right)
    if past is not None:
        a = jnp.where(past, a, 0.0)
    return ls_pos, a, right[:, 0:1] + lk[:, 0:1]


def _sb_diagonal():
    B = SB_BLOCK
    return lax.broadcasted_iota(jnp.int32, (B, B), 1) < lax.broadcasted_iota(jnp.int32, (B, B), 0)


def _sb_rows(kb):
    return pl.ds(pl.multiple_of(kb * SB_BLOCK, SB_BLOCK), SB_BLOCK)


def _sb_fwd(proj, n_heads, after=()):
    T = proj.shape[0]
    B, HP = SB_BLOCK, SB_HEADS
    assert n_heads % HP == 0

    def body(q_ref, k_ref, v_ref, *rest):
        y_ref = rest[-1]
        qb = pl.program_id(1)
        tri = _triangle(B, right=True)
        qv = [_head_cols(q_ref, slice(None), hh).astype(BF16) for hh in range(HP)]

        def tile(kb, carry, past):
            out = []
            for hh in range(HP):
                acc, c_lk = carry[hh]
                kk = _head_cols(k_ref, _sb_rows(kb), hh).astype(BF16)
                vv = _head_cols(v_ref, _sb_rows(kb), hh).astype(BF16)
                _, a, c_lk = _sb_tile(qv[hh], kk, past, c_lk, tri)
                out.append((acc + _dot(a.astype(BF16), vv), c_lk))
            return tuple(out)

        init = tuple((jnp.zeros((B, HEAD_DIM), F32), jnp.zeros((B, 1), F32)) for _ in range(HP))
        res = lax.fori_loop(1, qb + 1, lambda i, carry: tile(qb - i, carry, None), tile(qb, init, _sb_diagonal()))
        for hh in range(HP):
            y_ref[:, hh * HEAD_DIM:(hh + 1) * HEAD_DIM] = res[hh][0].astype(BF16)

    blk = pl.BlockSpec((B, HP * HEAD_DIM), lambda h, i: (i, h))
    G = n_heads // HP
    return pl.pallas_call(
        body, grid=(G, T // B),
        in_specs=[blk, _head_spec(T, G, HP), _head_spec(T, 2 * G, HP)] + [ANY] * len(after), out_specs=blk,
        out_shape=jax.ShapeDtypeStruct((T, n_heads * HEAD_DIM), BF16),
        compiler_params=_params("parallel", "arbitrary"), name="sb_fwd")(proj, proj, proj, *after)


def _sb_bwd(proj, dy, n_heads):
    T = proj.shape[0]
    B, HP = SB_BLOCK, SB_HEADS
    nq = T // B

    def body(q_ref, k_ref, v_ref, dy_ref, dq_ref, dk_ref, dv_ref, g_s, sig_s, dk_s, dv_s):
        qb = pl.program_id(1)

        @pl.when(qb == 0)
        def _():
            dk_s[...] = jnp.zeros_like(dk_s)
            dv_s[...] = jnp.zeros_like(dv_s)

        tri_r = _triangle(B, right=True)
        tri_l = _triangle(B, right=False)
        qv = [_head_cols(q_ref, slice(None), hh).astype(BF16) for hh in range(HP)]
        dyb = [_head_cols(dy_ref, slice(None), hh).astype(BF16) for hh in range(HP)]

        def sweep(kb, carry, past):
            out = []
            for hh in range(HP):
                kk = _head_cols(k_ref, _sb_rows(kb), hh).astype(BF16)
                vv = _head_cols(v_ref, _sb_rows(kb), hh).astype(BF16)
                ls_pos, a, c_lk = _sb_tile(qv[hh], kk, past, carry[hh], tri_r)
                g_s[hh, kb] = _dot(dyb[hh], vv, NT) * a
                sig_s[hh, kb] = jnp.exp(ls_pos)
                dv_s[_sb_rows(kb), hh * HEAD_DIM:(hh + 1) * HEAD_DIM] += _dot(a.astype(BF16), dyb[hh], TN)
                out.append(c_lk)
            return tuple(out)

        zeros = tuple(jnp.zeros((B, 1), F32) for _ in range(HP))
        lax.fori_loop(1, qb + 1, lambda i, carry: sweep(qb - i, carry, None), sweep(qb, zeros, _sb_diagonal()))

        def back(kb, carry, past):
            out = []
            for hh in range(HP):
                dq, c_g = carry[hh]
                kk = _head_cols(k_ref, _sb_rows(kb), hh).astype(BF16)
                g, sig = g_s[hh, kb], sig_s[hh, kb]
                left = c_g + _lane_scan(g, tri_l)
                dz = g * (1.0 - sig) - left * sig
                if past is not None:
                    dz = jnp.where(past, dz, 0.0)
                dz = (dz * SCALE).astype(BF16)
                dk_s[_sb_rows(kb), hh * HEAD_DIM:(hh + 1) * HEAD_DIM] += _dot(dz, qv[hh], TN)
                out.append((dq + _dot(dz, kk), left[:, B - 1:B] + g[:, B - 1:B]))
            return tuple(out)

        init = tuple((jnp.zeros((B, HEAD_DIM), F32), jnp.zeros((B, 1), F32)) for _ in range(HP))
        res = back(qb, lax.fori_loop(0, qb, lambda kb, carry: back(kb, carry, None), init), _sb_diagonal())
        for hh in range(HP):
            dq_ref[:, hh * HEAD_DIM:(hh + 1) * HEAD_DIM] = res[hh][0].astype(BF16)

        @pl.when(qb == nq - 1)
        def _():
            dk_ref[...] = dk_s[...].astype(BF16)
            dv_ref[...] = dv_s[...].astype(BF16)

    blk = pl.BlockSpec((B, HP * HEAD_DIM), lambda h, i: (i, h))
    G = n_heads // HP
    full = _head_spec(T, 0, HP)
    shp = jax.ShapeDtypeStruct((T, n_heads * HEAD_DIM), BF16)
    return pl.pallas_call(
        body, grid=(G, nq),
        in_specs=[blk, _head_spec(T, G, HP), _head_spec(T, 2 * G, HP), blk], out_specs=[blk, full, full],
        out_shape=[shp, shp, shp],
        scratch_shapes=[pltpu.VMEM((HP, nq, B, B), F32)] * 2 + [pltpu.VMEM((T, HP * HEAD_DIM), F32)] * 2,
        compiler_params=_params("parallel", "arbitrary"), name="sb_bwd")(proj, proj, proj, dy)


DIAGS = PBAND + PAIR


def _diag_onehot():
    d = lax.broadcasted_iota(jnp.int32, (DIAGS, 2 * PAIR), 0)
    r = lax.broadcasted_iota(jnp.int32, (DIAGS, 2 * PAIR), 1)
    return jnp.where(jnp.clip(d - PAIR - PAD, -REL_CLIP, CHUNK - 1) + REL_CLIP == r, 1.0, 0.0)


def _bias_expand(rel_bias):
    H = rel_bias.shape[0]
    table = jnp.pad(rel_bias, ((0, 0), (0, 2 * PAIR - N_REL)))

    def body(rb_ref, o_ref):
        o_ref[...] = lax.dot_general(rb_ref[...], _diag_onehot(), NT, precision=lax.Precision.HIGHEST, preferred_element_type=F32)

    per_diag = pl.pallas_call(body, out_shape=jax.ShapeDtypeStruct((H, DIAGS), F32), name="bias_expand")(table)
    flat = jnp.tile(jnp.pad(per_diag, ((0, 0), (0, 1))), (1, PAIR))[:, :PAIR * DIAGS]
    return flat.reshape(H, PAIR, DIAGS)[:, :, PAIR:]


def _bias_reduce(dbias):
    H = dbias.shape[0]
    padded = jnp.pad(dbias, ((0, 0), (0, 1), (PAIR, 0))).reshape(H, -1)
    skewed = padded[:, :PAIR * (DIAGS + 1)].reshape(H, PAIR, DIAGS + 1)[:, :, :DIAGS]

    def body(s_ref, o_ref):
        per_diag = jnp.sum(s_ref[...], axis=0, keepdims=True)
        o_ref[...] = lax.dot_general(jnp.broadcast_to(per_diag, (8, DIAGS)), _diag_onehot(), NN, precision=lax.Precision.HIGHEST,
                                     preferred_element_type=F32)[0:1]

    return pl.pallas_call(
        body, grid=(H,), in_specs=[pl.BlockSpec((None, PAIR, DIAGS), lambda h: (h, 0, 0))],
        out_specs=pl.BlockSpec((None, 1, 2 * PAIR), lambda h: (h, 0, 0)),
        out_shape=jax.ShapeDtypeStruct((H, 1, 2 * PAIR), F32), compiler_params=_params("parallel"), name="bias_reduce")(skewed)[:, 0]


CA_HEADS = 2


def _ca_mask():
    i = lax.broadcasted_iota(jnp.int32, (PAIR, PBAND), 0)
    j = lax.broadcasted_iota(jnp.int32, (PAIR, PBAND), 1)
    qc, kc = i // CHUNK, j // CHUNK
    return j, (kc >= qc) & (kc <= qc + LEFT_CHUNKS)


def _ca_weights(pr, qp, kb, bias, j, window):
    valid = window & (pr * PAIR + j >= PAD)
    z = jnp.where(valid, _dot(qp, kb, NT) * SCALE + bias, NEG)
    e = jnp.exp(z - jnp.max(z, axis=1, keepdims=True))
    return e / jnp.sum(e, axis=1, keepdims=True)


def _ca_fill(k_ref, v_ref, kpad, vpad):
    T, W = k_ref.shape
    kpad[0:PAD, :] = jnp.zeros((PAD, W), BF16)
    vpad[0:PAD, :] = jnp.zeros((PAD, W), BF16)
    kpad[PAD:PAD + T, :] = k_ref[...].astype(BF16)
    vpad[PAD:PAD + T, :] = v_ref[...].astype(BF16)


def _ca_fwd(proj, bias, n_heads, col0):
    T = proj.shape[0]
    HP = CA_HEADS
    G = n_heads // HP
    assert n_heads % HP == 0 and col0 % HP == 0

    def body(q_ref, k_ref, v_ref, b_ref, y_ref, kpad, vpad):
        _ca_fill(k_ref, v_ref, kpad, vpad)
        j, window = _ca_mask()

        def pair(pr, _):
            r0 = pl.multiple_of(pr * PAIR, PAIR)
            for hh in range(HP):
                qp = _head_cols(q_ref, pl.ds(r0, PAIR), hh).astype(BF16)
                kb = _head_cols(kpad, pl.ds(r0, PBAND), hh)
                vb = _head_cols(vpad, pl.ds(r0, PBAND), hh)
                w = _ca_weights(pr, qp, kb, b_ref[hh], j, window)
                y_ref[pl.ds(r0, PAIR), hh * HEAD_DIM:(hh + 1) * HEAD_DIM] = _dot(w.astype(BF16), vb).astype(BF16)
            return 0

        lax.fori_loop(0, T // PAIR, pair, 0)

    c = col0 // HP
    return pl.pallas_call(
        body, grid=(G,),
        in_specs=[_head_spec(T, c, HP), _head_spec(T, c + G, HP), _head_spec(T, c + 2 * G, HP),
                  pl.BlockSpec((HP, PAIR, PBAND), lambda h: (h, 0, 0))],
        out_specs=_head_spec(T, 0, HP), out_shape=jax.ShapeDtypeStruct((T, n_heads * HEAD_DIM), BF16),
        scratch_shapes=[pltpu.VMEM((PAD + T, HP * HEAD_DIM), BF16)] * 2,
        compiler_params=_params("parallel"), name="ca_fwd")(proj, proj, proj, bias)


def _ca_bwd(proj, bias, dy, n_heads, col0):
    T = proj.shape[0]
    HP = CA_HEADS
    G = n_heads // HP

    def body(q_ref, k_ref, v_ref, b_ref, dy_ref, dq_ref, dk_ref, dv_ref, db_ref, kpad, vpad, dkpad, dvpad):
        _ca_fill(k_ref, v_ref, kpad, vpad)
        dkpad[...] = jnp.zeros_like(dkpad)
        dvpad[...] = jnp.zeros_like(dvpad)
        db_ref[...] = jnp.zeros_like(db_ref)
        j, window = _ca_mask()

        def pair(pr, _):
            r0 = pl.multiple_of(pr * PAIR, PAIR)
            for hh in range(HP):
                cols = slice(hh * HEAD_DIM, (hh + 1) * HEAD_DIM)
                qp = _head_cols(q_ref, pl.ds(r0, PAIR), hh).astype(BF16)
                kb = _head_cols(kpad, pl.ds(r0, PBAND), hh)
                vb = _head_cols(vpad, pl.ds(r0, PBAND), hh)
                w = _ca_weights(pr, qp, kb, b_ref[hh], j, window)
                dyp = _head_cols(dy_ref, pl.ds(r0, PAIR), hh).astype(BF16)
                dw = _dot(dyp, vb, NT)
                dz = w * (dw - jnp.sum(dw * w, axis=1, keepdims=True))
                db_ref[hh] += dz
                dzs = (dz * SCALE).astype(BF16)
                dq_ref[pl.ds(r0, PAIR), cols] = _dot(dzs, kb).astype(BF16)
                dkpad[pl.ds(r0, PBAND), cols] += _dot(dzs, qp, TN)
                dvpad[pl.ds(r0, PBAND), cols] += _dot(w.astype(BF16), dyp, TN)
            return 0

        lax.fori_loop(0, T // PAIR, pair, 0)
        dk_ref[...] = dkpad[PAD:PAD + T, :].astype(BF16)
        dv_ref[...] = dvpad[PAD:PAD + T, :].astype(BF16)

    c = col0 // HP
    full = _head_spec(T, 0, HP)
    bspec = pl.BlockSpec((HP, PAIR, PBAND), lambda h: (h, 0, 0))
    shp = jax.ShapeDtypeStruct((T, n_heads * HEAD_DIM), BF16)
    return pl.pallas_call(
        body, grid=(G,),
        in_specs=[_head_spec(T, c, HP), _head_spec(T, c + G, HP), _head_spec(T, c + 2 * G, HP), bspec, full],
        out_specs=[full, full, full, bspec],
        out_shape=[shp, shp, shp, jax.ShapeDtypeStruct((n_heads, PAIR, PBAND), F32)],
        scratch_shapes=[pltpu.VMEM((PAD + T, HP * HEAD_DIM), BF16)] * 2 + [pltpu.VMEM((PAD + T, HP * HEAD_DIM), F32)] * 2,
        compiler_params=_params("parallel"), name="ca_bwd")(proj, proj, proj, bias, dy)


def _local_step(x, p, target, comm, g):
    T, D = x.shape
    H = g["rel_bias"].shape[0]
    W = H * HEAD_DIM
    nb_in = comm.shapes["w_in_a"][2]
    nb_ff = comm.shapes["w_ffn_in"][2]
    nb_o = comm.shapes["w_sb_out"][2]
    nb_p = comm.shapes["w_ple_in"][2]
    tm = min(T, 1024)
    tn = min(D, 1024)
    gate_col = 6 * W // D

    h1 = _norm_fwd(x, g["g_mix"], "norm1")
    comm.stage("norm1", h1)
    proj = _mm(h1, comm.weight("w_in_a", h1), mode="nn", tm=tm, tn=nb_in, tk=D // 2, out_dtype=F32, b_blocked=True,
               after=comm.pending(), name="mm_in_a")
    comm.stage("mm_in_a", proj)
    proj = _mm(h1, comm.weight("w_in_b", proj), mode="nn", tm=tm, tn=nb_in, tk=D // 2, out_dtype=F32, b_blocked=True, a_cols=(1, 1),
               res=proj, after=comm.pending(), name="mm_in")
    comm.stage("mm_in", proj)
    y_sb = _sb_fwd(proj, H, comm.pending())
    bias = _bias_expand(g["rel_bias"])
    y_ca = _ca_fwd(proj, bias, H, 3 * H)
    comm.stage("attention", y_sb, y_ca)
    a_sb, a_ca, merged = _mm_merge(y_sb, y_ca, comm.weight("w_sb_out", y_ca), comm.weight("w_ca_out"), proj, gate_col, min(T, 512), tn, comm.pending())
    x2 = _mm(merged, comm.weight("w_mix_out"), mode="nn", tm=tm, tn=tn, tk=D, out_dtype=F32, res=x, name="mm_mix")
    h2 = _norm_fwd(x2, g["g_ffn"], "norm2")
    gate, up, act = _mm_swiglu(h2, comm.weight("w_ffn_in", h2), min(T, 512))
    comm.stage("mm_ffn_in", act)
    F = act.shape[1]
    tkf = F // 2 if F % 256 == 0 else F
    x3 = _mm(act, comm.weight("w_ffn_out", act), mode="nn", tm=tm, tn=tn, tk=tkf, out_dtype=F32, res=x2, after=comm.pending(), name="mm_ffn_out")
    h3 = _norm_fwd(x3, g["g_ple"], "norm3")
    pb = _cast_bf16(p, "cast_p")
    P = p.shape[1]
    pe = _mm(pb, comm.weight("w_ple_in"), mode="nn", tm=tm, tn=nb_p, tk=P, out_dtype=F32, b_blocked=True, name="mm_ple_in")
    dx4, dpe, dzg, loss, dg_final = _mm_fused(
        h3, comm.weight("w_ple_gate"), [(x3, 0), (pe, 0), (target, 0), g["g_final"]], _tail, [F32, BF16, BF16],
        mode="nn", tm=min(T, 256), tn=D, tk=D, sums=[(1, 128), (1, D)], name="mm_ple_gate")

    tw = min(D, 1024)
    DW = BF16
    comm.grad("w_ple_in", _mm(pb, dpe, mode="tn", tm=P, tn=nb_p, tk=T, out_dtype=DW, out_block=nb_p, name="mm_d_ple_in"))
    comm.grad("w_ple_gate", _mm(h3, dzg, mode="tn", tm=tw, tn=tn, tk=T, out_dtype=DW, name="mm_d_ple_gate"))
    dh3 = _mm(dzg, comm.weight("w_ple_gate"), mode="nt", tm=tm, tn=tn, tk=D, out_dtype=F32, after=comm.pending(), name="mm_dh3")
    dx3, dx3b, dg_ple = _norm_bwd(dh3, x3, g["g_ple"], dx4, "norm3_bwd")
    comm.grad("w_ffn_out", _mm(act, dx3b, mode="tn", tm=F // 4, tn=tn, tk=T, out_dtype=DW, name="mm_d_ffn_out"))
    dgate, dup = _mm_fused(dx3b, comm.weight("w_ffn_out"), [(gate, 0), (up, 0)], _swiglu_bwd, [BF16, BF16],
                           mode="nt", tm=min(T, 512), tn=nb_ff, tk=D, after=comm.pending(), name="mm_dact")
    half = comm.shapes["w_ffn_in"][0] // 2
    d_ffn_in = _mm(h2, dgate, mode="tn", tm=tw, tn=nb_ff, tk=T, out_dtype=DW, out_block=nb_ff, o_count=2 * half, name="mm_d_ffn_in_gate")
    comm.grad("w_ffn_in", _mm(h2, dup, mode="tn", tm=tw, tn=nb_ff, tk=T, out_dtype=DW, out_block=nb_ff, o_first=half, o_count=2 * half,
                              into=d_ffn_in, name="mm_d_ffn_in"))
    dh2 = _mm(dgate, comm.weight("w_ffn_in"), mode="nt", tm=tm, tn=D, tk=nb_ff, out_dtype=F32, b_blocked=True, b_count=half,
              after=comm.pending(), name="mm_dh2_gate")
    dh2 = _mm(dup, comm.weight("w_ffn_in"), mode="nt", tm=min(T, 512), tn=D, tk=nb_ff, out_dtype=F32, b_blocked=True, b_first=half, b_count=half,
              res=dh2, name="mm_dh2")
    dx2, dx2b, dg_ffn = _norm_bwd(dh2, x2, g["g_ffn"], dx3, "norm2_bwd")
    per = D // tn
    da_sb, da_ca, dgate_sb, dgate_ca = _mm_fused(
        dx2b, comm.weight("w_mix_out"), [(proj, gate_col * per), (proj, (gate_col + 1) * per), (a_sb, 0), (a_ca, 0)], _merge_bwd, [BF16] * 4,
        mode="nt", tm=min(T, 512), tn=tn, tk=D, name="mm_dmerged")
    comm.grad("w_mix_out", _mm(merged, dx2b, mode="tn", tm=tw, tn=tn, tk=T, out_dtype=DW, name="mm_d_mix"))
    comm.grad("w_sb_out", _mm(y_sb, da_sb, mode="tn", tm=min(W, 512), tn=tn, tk=T, out_dtype=DW, out_block=nb_o, name="mm_d_sb_out"))
    comm.grad("w_ca_out", _mm(y_ca, da_ca, mode="tn", tm=min(W, 512), tn=tn, tk=T, out_dtype=DW, out_block=nb_o, name="mm_d_ca_out"))
    dy_sb = _mm(da_sb, comm.weight("w_sb_out"), mode="nt", tm=tm, tn=W, tk=tn, out_dtype=F32, b_blocked=True, after=comm.pending(), name="mm_dy_sb")
    dy_ca = _mm(da_ca, comm.weight("w_ca_out"), mode="nt", tm=tm, tn=W, tk=tn, out_dtype=F32, b_blocked=True, name="mm_dy_ca")
    dq_sb, dk_sb, dv_sb = _sb_bwd(proj, dy_sb, H)
    dq_ca, dk_ca, dv_ca, dbias = _ca_bwd(proj, bias, dy_ca, H, 3 * H)
    d_rel = _bias_reduce(dbias)[:, :N_REL]
    dproj = jnp.concatenate([dq_sb, dk_sb, dv_sb, dq_ca, dk_ca, dv_ca, dgate_sb, dgate_ca], axis=1)
    comm.grad("w_in", _mm(h1, dproj, mode="tn", tm=tw, tn=nb_in, tk=T, out_dtype=DW, out_block=nb_in, name="mm_d_in"))
    dh1 = _mm(dproj, comm.weight("w_in_a"), mode="nt", tm=tm, tn=D // 2, tk=nb_in, out_dtype=F32, b_blocked=True, out_cols=(2, 1, 0),
              after=comm.pending(), name="mm_dh1_a")
    comm.pair_done(dh1)
    dh1 = _mm(dproj, comm.weight("w_in_b"), mode="nt", tm=tm, tn=D // 2, tk=nb_in, out_dtype=F32, b_blocked=True, out_cols=(2, 1, 1),
              into=dh1, after=comm.pending(), name="mm_dh1")
    grad_x, _, dg_mix = _norm_bwd(dh1, x, g["g_mix"], dx2, "norm1_bwd")
    small = dict(g_mix=dg_mix, g_ffn=dg_ffn, g_ple=dg_ple, g_final=dg_final, rel_bias=d_rel)
    return loss, grad_x, small


def _position():
    x, y, c = lax.axis_index("x"), lax.axis_index("y"), lax.axis_index("c")
    return x, y, c


def _block_of(px, py, pc):
    return 4 * px + 2 * py + pc


def _flip(pos, k):
    x, y, c = pos
    return (1 - x if k & 4 else x, 1 - y if k & 2 else y, 1 - c if k & 1 else c)


HBM = pl.BlockSpec(memory_space=pltpu.HBM)
SEM = pl.BlockSpec(memory_space=pltpu.SEMAPHORE)
VMEM_SPEC = pl.BlockSpec(memory_space=pltpu.VMEM)
EFFECT = pltpu.SideEffectType.DATAFLOW_SIDE_EFFECTING
TOKEN = jax.ShapeDtypeStruct((8, 128), F32)


def _hbm(a):
    return pltpu.HBM(a.shape, a.dtype)


def _landing(shape, dtype):
    return pltpu.with_memory_space_constraint(lax.empty(shape, dtype), pltpu.HBM)


def _gather_start(lands, after, name):
    n = len(lands)

    def body(*refs):
        ins = refs[:n]
        send, recv = refs[n + 1], refs[n + 2]
        token = refs[-1]
        x, y, c = _position()
        mine = _block_of(x, y, c)
        peers = [(x, y, 1 - c), (1 - x, y, c), (x, 1 - y, c), (1 - x, 1 - y, c)]
        for wi in range(n):
            for k, peer in enumerate(peers):
                pltpu.make_async_remote_copy(
                    src_ref=ins[wi].at[mine], dst_ref=ins[wi].at[mine], send_sem=send.at[4 * wi + k], recv_sem=recv.at[4 * wi + k],
                    device_id=peer, device_id_type=MESH).start()
        token[...] = jnp.zeros_like(token)

    outs = pl.pallas_call(
        body, name=name, in_specs=[HBM] * n + [ANY], out_specs=(SEM, SEM, *[HBM] * n, VMEM_SPEC),
        out_shape=(pltpu.SemaphoreType.DMA((4 * n,)), pltpu.SemaphoreType.DMA((4 * n,)), *[_hbm(a) for a in lands], TOKEN),
        input_output_aliases={i: 2 + i for i in range(n)},
        compiler_params=pltpu.CompilerParams(has_side_effects=EFFECT))(*[pltpu.with_memory_space_constraint(a, pltpu.HBM) for a in lands], after)
    return outs[0], outs[1], list(outs[2:2 + n]), outs[-1]


def _gather_forward(lands, send0, recv0, after, name):
    n = len(lands)

    def body(*refs):
        ins = refs[:n]
        send0, recv0 = refs[n], refs[n + 1]
        send1, recv1 = refs[n + 2 + len(after)], refs[n + 3 + len(after)]
        token = refs[-1]
        x, y, c = _position()
        chips = [(1 - x, y), (x, 1 - y), (1 - x, 1 - y)]
        for wi in range(n):
            for j, chip in enumerate(chips):
                rows = ins[wi].at[_block_of(*chip, c)]
                pltpu.make_async_remote_copy(
                    src_ref=rows, dst_ref=rows, send_sem=send0.at[4 * wi + 1 + j], recv_sem=recv0.at[4 * wi + 1 + j],
                    device_id=(*chip, c), device_id_type=MESH).wait_recv()
                pltpu.make_async_remote_copy(
                    src_ref=rows, dst_ref=rows, send_sem=send1.at[3 * wi + j], recv_sem=recv1.at[3 * wi + j],
                    device_id=(x, y, 1 - c), device_id_type=MESH).start()
        token[...] = jnp.zeros_like(token)

    outs = pl.pallas_call(
        body, name=name, in_specs=[HBM] * n + [SEM, SEM] + [ANY] * len(after), out_specs=(SEM, SEM, *[HBM] * n, VMEM_SPEC),
        out_shape=(pltpu.SemaphoreType.DMA((3 * n,)), pltpu.SemaphoreType.DMA((3 * n,)), *[_hbm(a) for a in lands], TOKEN),
        input_output_aliases={i: 2 + i for i in range(n)},
        compiler_params=pltpu.CompilerParams(has_side_effects=EFFECT))(*lands, send0, recv0, *after)
    return outs[0], outs[1], list(outs[2:2 + n]), outs[-1]


def _gather_wait(lands, send0, recv0, send1, recv1, after, name):
    n = len(lands)

    def body(*refs):
        ins = refs[:n]
        send0, recv0, send1, recv1 = refs[n:n + 4]
        x, y, c = _position()
        mine = _block_of(x, y, c)
        sibling = (x, y, 1 - c)
        peers = [sibling, (1 - x, y, c), (x, 1 - y, c), (1 - x, 1 - y, c)]
        chips = [(1 - x, y), (x, 1 - y), (1 - x, 1 - y)]
        for wi in range(n):
            own = ins[wi].at[mine]
            for k, peer in enumerate(peers):
                pltpu.make_async_remote_copy(src_ref=own, dst_ref=own, send_sem=send0.at[4 * wi + k], recv_sem=recv0.at[4 * wi + k],
                                             device_id=peer, device_id_type=MESH).wait_send()
            theirs = ins[wi].at[_block_of(*sibling)]
            pltpu.make_async_remote_copy(src_ref=theirs, dst_ref=theirs, send_sem=send0.at[4 * wi], recv_sem=recv0.at[4 * wi],
                                         device_id=sibling, device_id_type=MESH).wait_recv()
            for j, chip in enumerate(chips):
                sent = ins[wi].at[_block_of(*chip, c)]
                got = ins[wi].at[_block_of(*chip, 1 - c)]
                pltpu.make_async_remote_copy(src_ref=sent, dst_ref=sent, send_sem=send1.at[3 * wi + j], recv_sem=recv1.at[3 * wi + j],
                                             device_id=sibling, device_id_type=MESH).wait_send()
                pltpu.make_async_remote_copy(src_ref=got, dst_ref=got, send_sem=send1.at[3 * wi + j], recv_sem=recv1.at[3 * wi + j],
                                             device_id=sibling, device_id_type=MESH).wait_recv()

    outs = pl.pallas_call(
        body, name=name, in_specs=[HBM] * n + [SEM] * 4 + [ANY], out_specs=tuple([HBM] * n),
        out_shape=tuple(_hbm(a) for a in lands), input_output_aliases={i: i for i in range(n)},
        compiler_params=pltpu.CompilerParams(has_side_effects=EFFECT))(*lands, send0, recv0, send1, recv1, after)
    return list(outs)


def _plan_direct(me):
    return [(_block_of(*_flip(me, k)), k - 1, _flip(me, k)) for k in range(1, N_DEV)]


def _plan_sibling(me):
    x, y, c = me
    return [(_block_of(ci // 2, ci % 2, 1 - c), ci, (x, y, 1 - c)) for ci in range(4)]


def _plan_chips(me):
    x, y, c = me
    out = []
    for k in range(1, 4):
        px, py = (1 - x if k & 2 else x), (1 - y if k & 1 else y)
        out.append((2 * px + py, k - 1, (px, py, c)))
    return out


def _exchange_start(blocks, plan, name):
    n = len(blocks)
    slots = len(plan((0, 0, 0)))

    def body(*refs):
        srcs, lands = refs[:n], refs[n:2 * n]
        send, recv = refs[2 * n], refs[2 * n + 1]
        token = refs[-1]
        for wi in range(n):
            for block, slot, peer in plan(_position()):
                pltpu.make_async_remote_copy(
                    src_ref=srcs[wi].at[block], dst_ref=lands[wi].at[slot], send_sem=send.at[slots * wi + slot],
                    recv_sem=recv.at[slots * wi + slot], device_id=peer, device_id_type=MESH).start()
        token[...] = jnp.zeros_like(token)

    zones = [_landing((slots,) + b.shape[1:], b.dtype) for b in blocks]
    outs = pl.pallas_call(
        body, name=name, in_specs=[HBM] * (2 * n), out_specs=(SEM, SEM, *[HBM] * (2 * n), VMEM_SPEC),
        out_shape=(pltpu.SemaphoreType.DMA((slots * n,)), pltpu.SemaphoreType.DMA((slots * n,)), *[_hbm(a) for a in blocks],
                   *[_hbm(z) for z in zones], TOKEN),
        input_output_aliases={i: 2 + i for i in range(2 * n)},
        compiler_params=pltpu.CompilerParams(has_side_effects=EFFECT))(
            *[pltpu.with_memory_space_constraint(b, pltpu.HBM) for b in blocks], *zones)
    return outs[0], outs[1], list(outs[2:2 + n]), list(outs[2 + n:2 + 2 * n]), outs[-1]


def _exchange_wait(groups, plan, after, name):
    flat, counts = [], []
    for send, recv, blocks, zones in groups:
        flat += [*blocks, *zones, send, recv]
        counts.append(len(blocks))
    slots = len(plan((0, 0, 0)))

    def body(*refs):
        pos = 0
        for n in counts:
            srcs, lands = refs[pos:pos + n], refs[pos + n:pos + 2 * n]
            send, recv = refs[pos + 2 * n], refs[pos + 2 * n + 1]
            pos += 2 * n + 2
            for wi in range(n):
                for block, slot, peer in plan(_position()):
                    cp = pltpu.make_async_remote_copy(
                        src_ref=srcs[wi].at[block], dst_ref=lands[wi].at[slot], send_sem=send.at[slots * wi + slot],
                        recv_sem=recv.at[slots * wi + slot], device_id=peer, device_id_type=MESH)
                    cp.wait_send()
                    cp.wait_recv()

    in_specs, out_specs, out_shape, aliases = [], [], [], {}
    i = 0
    for n, (send, recv, blocks, zones) in zip(counts, groups):
        for a in (*blocks, *zones):
            aliases[i] = len(out_shape)
            in_specs.append(HBM)
            out_specs.append(HBM)
            out_shape.append(_hbm(a))
            i += 1
        in_specs += [SEM, SEM]
        i += 2
    outs = pl.pallas_call(
        body, name=name, in_specs=in_specs + [ANY], out_specs=tuple(out_specs), out_shape=tuple(out_shape),
        input_output_aliases=aliases, compiler_params=pltpu.CompilerParams(has_side_effects=EFFECT))(*flat, after)
    res, pos = [], 0
    for n in counts:
        res.append((list(outs[pos:pos + n]), list(outs[pos + n:pos + 2 * n])))
        pos += 2 * n
    return res


def _sibling_sum(blocks, zone, core, name):
    _, R, C = zone.shape
    rt = next(r for r in (R, R // 2, R // 4, 128, 64) if R % r == 0 and r % 16 == 0 and r * C <= 4 * 1024 * 1024)

    def body(core_ref, own_ref, z_ref, o_ref):
        o_ref[...] = (own_ref[...].astype(F32) + z_ref[...].astype(F32)).astype(o_ref.dtype)

    grid_spec = pltpu.PrefetchScalarGridSpec(
        num_scalar_prefetch=1, grid=(4, R // rt),
        in_specs=[pl.BlockSpec((None, rt, C), lambda ci, i, core_ref: (2 * ci + core_ref[0], i, 0)),
                  pl.BlockSpec((None, rt, C), lambda ci, i, core_ref: (ci, i, 0))],
        out_specs=pl.BlockSpec((None, rt, C), lambda ci, i, core_ref: (ci, i, 0)))
    return pl.pallas_call(body, grid_spec=grid_spec, out_shape=jax.ShapeDtypeStruct(zone.shape, zone.dtype),
                          compiler_params=_params("parallel", "parallel"), name=name)(core, blocks, zone)


def _adamw(w, g, m, v):
    m = ADAM_B1 * m + (1.0 - ADAM_B1) * g
    v = ADAM_B2 * v + (1.0 - ADAM_B2) * (g * g)
    m_hat = m / (1.0 - ADAM_B1 ** ADAM_STEP)
    v_hat = v / (1.0 - ADAM_B2 ** ADAM_STEP)
    delta = -ADAM_LR * (m_hat / (jnp.sqrt(v_hat) + ADAM_EPS) + ADAM_WD * w)
    return delta, m, v


def _reduce_adamw(blocks, zone, mine, w, m, v, name):
    R, C = w.shape
    rt = 128 if R % 128 == 0 else 64
    assert R % rt == 0

    def body(mine_ref, own_ref, z_ref, w_ref, m_ref, v_ref, g_out, d_out, m_out, v_out):
        g = own_ref[...].astype(F32)
        for s in range(zone.shape[0]):
            g = g + z_ref[s].astype(F32)
        delta, m2, v2 = _adamw(w_ref[...], g, m_ref[...], v_ref[...])
        g_out[...] = g
        d_out[...] = delta
        m_out[...] = m2
        v_out[...] = v2

    spec = pl.BlockSpec((rt, C), lambda i, mine_ref: (i, 0))
    grid_spec = pltpu.PrefetchScalarGridSpec(
        num_scalar_prefetch=1, grid=(R // rt,),
        in_specs=[pl.BlockSpec((None, rt, C), lambda i, mine_ref: (mine_ref[0], i, 0)),
                  pl.BlockSpec((zone.shape[0], rt, C), lambda i, mine_ref: (0, i, 0)), spec, spec, spec],
        out_specs=[spec] * 4)
    return pl.pallas_call(body, grid_spec=grid_spec, out_shape=[jax.ShapeDtypeStruct((R, C), F32)] * 4,
                          compiler_params=_params("parallel"), name=name)(mine, blocks, zone, w, m, v)


def _small_step(part, w, m, v, after):
    R, C = part.shape

    def body(part_ref, w_ref, m_ref, v_ref, *rest):
        g_out, d_out, m_out, v_out, gath, send, recv = rest[len(after):]
        me = _position()
        gath[_block_of(*me)] = part_ref[...]

        def copy(k, slot):
            return pltpu.make_async_remote_copy(
                src_ref=part_ref, dst_ref=gath.at[slot], send_sem=send.at[k - 1], recv_sem=recv.at[k - 1],
                device_id=_flip(me, k), device_id_type=MESH)

        sent = [copy(k, _block_of(*me)) for k in range(1, N_DEV)]
        for cp in sent:
            cp.start()
        for k in range(1, N_DEV):
            copy(k, _block_of(*_flip(me, k))).wait_recv()
        for cp in sent:
            cp.wait_send()
        g = gath[0]
        for s in range(1, N_DEV):
            g = g + gath[s]
        delta, m2, v2 = _adamw(w_ref[...], g, m_ref[...], v_ref[...])
        g_out[...] = g
        d_out[...] = delta
        m_out[...] = m2
        v_out[...] = v2

    vm = pl.BlockSpec(memory_space=pltpu.VMEM)
    return pl.pallas_call(
        body, in_specs=[vm] * 4 + [ANY] * len(after), out_specs=[vm] * 4, out_shape=[jax.ShapeDtypeStruct((R, C), F32)] * 4,
        scratch_shapes=[pltpu.VMEM((N_DEV, R, C), F32), pltpu.SemaphoreType.DMA((7,)), pltpu.SemaphoreType.DMA((7,))],
        name="small_step")(part, w, m, v, *after)


COLUMN_SHARDED = ("w_in", "w_sb_out", "w_ca_out", "w_ffn_in", "w_ple_in")
ROW_SHARDED = ("w_mix_out", "w_ffn_out", "w_ple_gate")
BIG = COLUMN_SHARDED + ROW_SHARDED
SMALL = ("g_mix", "g_ffn", "g_ple", "g_final", "rel_bias")
WEIGHTS = ("w_in", "w_sb_out", "w_ca_out", "w_mix_out", "rel_bias", "g_mix", "g_ffn", "g_ple", "g_final",
           "w_ffn_in", "w_ffn_out", "w_ple_in", "w_ple_gate")


def _pack_small(t, D):
    rows = [t[n].reshape(1, D) for n in SMALL[:4]]
    rb = t["rel_bias"].reshape(1, -1)
    rows.append(jnp.pad(rb, ((0, 0), (0, D - rb.shape[1]))))
    return jnp.concatenate(rows + [jnp.zeros((8 - len(rows), D), F32)], axis=0)


def _unpack_small(a, like):
    out = {n: a[i].reshape(like[n].shape) for i, n in enumerate(SMALL[:4])}
    out["rel_bias"] = a[4, :like["rel_bias"].size].reshape(like["rel_bias"].shape)
    return out


GATHER_GROUPS = (("w_in_a",), ("w_in_b",), ("w_sb_out", "w_ca_out", "w_mix_out"), ("w_ffn_in",), ("w_ffn_out", "w_ple_gate", "w_ple_in"))
FORWARD_AFTER = ("norm1", "mm_in_a", "mm_in", "attention", "mm_ffn_in")
GRAD_GROUPS = (("w_ple_in", "w_ple_gate"), ("w_ffn_out",), ("w_ffn_in",), ("w_mix_out", "w_sb_out", "w_ca_out"), ("w_in",))


class _Exchange:
    def __init__(self, shards):
        me = _position()
        self.mine = _block_of(*me)
        self.chip = jnp.reshape(2 * me[0] + me[1], (1,)).astype(jnp.int32)
        self.core = jnp.reshape(me[2], (1,)).astype(jnp.int32)
        self.device = jnp.reshape(self.mine, (1,)).astype(jnp.int32)
        self.shapes = {n: ((N_DEV * s.shape[0], s.shape[1]) if n in ROW_SHARDED else (N_DEV,) + s.shape) for n, s in shards.items()}
        self.tokens = []
        self.ready = {}
        self.gathers = []
        for gi, names in enumerate(GATHER_GROUPS):
            lands = [lax.dynamic_update_slice(lax.empty((N_DEV,) + shards[n].shape, BF16), shards[n][None], (self.mine, 0, 0))
                     for n in names]
            behind = self.tokens[-1] if self.tokens else shards[names[0]]
            send0, recv0, lands, token = _gather_start(lands, behind, f"gather_start_{gi}")
            self.tokens.append(token)
            self.gathers.append(dict(names=names, lands=lands, sems=(send0, recv0), token=token))
        self.grads = {}
        self.exchanges = []

    def pending(self):
        tokens, self.tokens = self.tokens, []
        return tokens

    def stage(self, tag, *made):
        gi = FORWARD_AFTER.index(tag)
        gth = self.gathers[gi]
        send1, recv1, lands, token = _gather_forward(gth["lands"], *gth["sems"], made + tuple(self.tokens), f"gather_forward_{gi}")
        gth.update(lands=lands, sems=gth["sems"] + (send1, recv1), token=token)
        self.tokens.append(token)

    def weight(self, name, after=None):
        if name not in self.ready:
            gi = next(i for i, names in enumerate(GATHER_GROUPS) if name in names)
            gth = self.gathers[gi]
            for n, a in zip(gth["names"], _gather_wait(gth["lands"], *gth["sems"], gth["token"] if after is None else after, f"gather_wait_{gi}")):
                self.ready[n] = a.reshape(self.shapes[n])
        return self.ready[name]

    def grad(self, name, blocks):
        self.grads[name] = blocks if name in COLUMN_SHARDED else blocks.reshape((N_DEV, -1, blocks.shape[-1]))
        names = next(names for names in GRAD_GROUPS if name in names)
        if not all(n in self.grads for n in names):
            return
        blocks = [self.grads[n] for n in names]
        if names == GRAD_GROUPS[-1]:
            send, recv, blocks, zones, token = _exchange_start(blocks, _plan_sibling, "pair_start_" + names[0])
            self.pair = (send, recv, blocks, zones)
        else:
            send, recv, blocks, zones, token = _exchange_start(blocks, _plan_direct, "exchange_start_" + names[0])
            self.exchanges.append(dict(names=names, state=(send, recv, blocks, zones), plan=_plan_direct, own=self.device))
        self.tokens.append(token)

    def pair_done(self, after):
        names = GRAD_GROUPS[-1]
        (blocks, zones), = _exchange_wait([self.pair], _plan_sibling, after, "pair_wait_" + names[0])
        blocks = [_sibling_sum(b, z, self.core, "pair_sum_" + n) for n, b, z in zip(names, blocks, zones)]
        send, recv, blocks, zones, token = _exchange_start(blocks, _plan_chips, "exchange_start_" + names[0])
        self.exchanges.append(dict(names=names, state=(send, recv, blocks, zones), plan=_plan_chips, own=self.chip))
        self.tokens.append(token)

    def collect(self, which, after, name):
        sel = [e for e in self.exchanges if GRAD_GROUPS.index(e["names"]) in which]
        out = {}
        for e, (blocks, zones) in zip(sel, _exchange_wait([e["state"] for e in sel], sel[0]["plan"], after, name)):
            out.update({n: (b, e["own"], z) for n, b, z in zip(e["names"], blocks, zones)})
        return out


def kernel(x, p, w_in, w_sb_out, w_ca_out, w_mix_out, rel_bias, g_mix, g_ffn, g_ple, g_final, w_ffn_in, w_ffn_out, w_ple_in, w_ple_gate, loss_target, m_w_in, m_w_sb_out, m_w_ca_out, m_w_mix_out, m_rel_bias, m_g_mix, m_g_ffn, m_g_ple, m_g_final, m_w_ffn_in, m_w_ffn_out, m_w_ple_in, m_w_ple_gate, v_w_in, v_w_sb_out, v_w_ca_out, v_w_mix_out, v_rel_bias, v_g_mix, v_g_ffn, v_g_ple, v_g_final, v_w_ffn_in, v_w_ffn_out, v_w_ple_in, v_w_ple_gate):
    wts = dict(w_in=w_in, w_sb_out=w_sb_out, w_ca_out=w_ca_out, w_mix_out=w_mix_out, rel_bias=rel_bias, g_mix=g_mix, g_ffn=g_ffn,
               g_ple=g_ple, g_final=g_final, w_ffn_in=w_ffn_in, w_ffn_out=w_ffn_out, w_ple_in=w_ple_in, w_ple_gate=w_ple_gate)
    mom = dict(w_in=m_w_in, w_sb_out=m_w_sb_out, w_ca_out=m_w_ca_out, w_mix_out=m_w_mix_out, rel_bias=m_rel_bias, g_mix=m_g_mix,
               g_ffn=m_g_ffn, g_ple=m_g_ple, g_final=m_g_final, w_ffn_in=m_w_ffn_in, w_ffn_out=m_w_ffn_out, w_ple_in=m_w_ple_in,
               w_ple_gate=m_w_ple_gate)
    var = dict(w_in=v_w_in, w_sb_out=v_w_sb_out, w_ca_out=v_w_ca_out, w_mix_out=v_w_mix_out, rel_bias=v_rel_bias, g_mix=v_g_mix,
               g_ffn=v_g_ffn, g_ple=v_g_ple, g_final=v_g_final, w_ffn_in=v_w_ffn_in, w_ffn_out=v_w_ffn_out, w_ple_in=v_w_ple_in,
               w_ple_gate=v_w_ple_gate)
    T, D = x.shape[1], x.shape[2]
    shard = {n: wts[n].reshape(wts[n].shape[-2:]) for n in BIG}
    bf = {n: _cast_bf16(shard[n], "cast_" + n) for n in BIG}
    half = bf["w_in"].shape[0] // 2
    bf["w_in_a"], bf["w_in_b"] = bf["w_in"][:half], bf.pop("w_in")[half:]
    comm = _Exchange(bf)
    g = dict(g_mix=g_mix.reshape(1, D), g_ffn=g_ffn.reshape(1, D), g_ple=g_ple.reshape(1, D), g_final=g_final.reshape(1, D),
             rel_bias=rel_bias.reshape(rel_bias.shape[-2:]))

    loss, grad_x, dsmall = _local_step(x.reshape(T, D), p.reshape(T, -1), loss_target.reshape(T, D), comm, g)
    loss = lax.psum(loss[0, 0], ("x", "y", "c"))

    grad, delta, new_m, new_v = {}, {}, {}, {}

    def update(parts):
        done = []
        for n, (blocks, own, zone) in parts.items():
            outs = _reduce_adamw(blocks, zone, own, shard[n], mom[n].reshape(shard[n].shape), var[n].reshape(shard[n].shape), "adamw_" + n)
            grad[n], delta[n], new_m[n], new_v[n] = [o.reshape(wts[n].shape) for o in outs]
            done.append(outs[0])
        return done

    done = update(comm.collect(range(len(GRAD_GROUPS) - 1), grad_x, "exchange_wait_rest"))
    outs = _small_step(_pack_small(dsmall, D), _pack_small(wts, D), _pack_small(mom, D), _pack_small(var, D), done)
    for dst, a in zip((grad, delta, new_m, new_v), outs):
        dst.update(_unpack_small(a, wts))
    update(comm.collect([len(GRAD_GROUPS) - 1], outs[0], "exchange_wait_w_in"))

    return (loss, grad_x.reshape(x.shape), *[grad[n] for n in WEIGHTS], *[delta[n] for n in WEIGHTS],
            *[new_m[n] for n in WEIGHTS], *[new_v[n] for n in WEIGHTS])
```

```python
import functools

import jax
import jax.numpy as jnp
from jax import lax
from jax.experimental import pallas as pl
from jax.experimental.pallas import tpu as pltpu

F32, BF16 = jnp.float32, jnp.bfloat16

N_DEV = 8
HEAD_DIM = 128
CHUNK = 64
LEFT_CHUNKS = 8
REL_CLIP = 128
N_REL = REL_CLIP + CHUNK
PAIR = 2 * CHUNK
PBAND = (LEFT_CHUNKS + 2) * CHUNK
CA_PAIRS = 2
CA_ROWS = CA_PAIRS * PAIR
CA_BAND = PBAND + CA_ROWS - PAIR
PAD = LEFT_CHUNKS * CHUNK
SB_BLOCK = 256
ROWS = 256
EPS = 1e-6
NEG = -1e30
SCALE = HEAD_DIM ** -0.5
VMEM_LIMIT_BYTES = 56 * 1024 * 1024

ADAM_LR, ADAM_B1, ADAM_B2, ADAM_EPS, ADAM_WD, ADAM_STEP = 0.001, 0.9, 0.999, 1e-08, 0.01, 10

ANY = pl.BlockSpec(memory_space=pl.ANY)
NN = (((1,), (0,)), ((), ()))
NT = (((1,), (1,)), ((), ()))
TN = (((0,), (0,)), ((), ()))
MESH = pl.DeviceIdType.MESH


def _params(*sem):
    return pltpu.CompilerParams(dimension_semantics=sem or None, vmem_limit_bytes=VMEM_LIMIT_BYTES)


def _dot(a, b, dims=NN):
    return lax.dot_general(a, b, dims, preferred_element_type=F32)


def _mm(a, b, *, mode, tm, tn, tk, out_dtype, name, b_blocked=False, out_block=None, res=None, after=(), a_cols=(1, 0), out_cols=(1, 1, 0), into=None, b_first=0, b_count=None, o_first=0, o_count=None):
    bg = og = 1
    if mode == "nn":
        M, K = a.shape
        a_spec = pl.BlockSpec((tm, tk), lambda i, j, k: (i, k))
        if b_blocked:
            G, _, nb = b.shape
            N = G * nb
            if tn > nb:
                bg = tn // nb
                assert tn % nb == 0
                b_spec = pl.BlockSpec((bg, tk, nb), lambda i, j, k: (j, k, 0))
            else:
                per = nb // tn
                assert nb % tn == 0
                b_spec = pl.BlockSpec((None, tk, tn), lambda i, j, k: (j // per, k, j % per))
        else:
            N = b.shape[1]
            b_spec = pl.BlockSpec((tk, tn), lambda i, j, k: (k, j))
        dims = NN
    elif mode == "nt":
        M, K = a.shape
        a_spec = pl.BlockSpec((tm, tk), lambda i, j, k: (i, k))
        if b_blocked:
            G, N, nb = b.shape
            K = (b_count or G) * nb
            assert b_first == 0 or tk == nb
            if tk > nb:
                bg = tk // nb
                assert tk % nb == 0
                b_spec = pl.BlockSpec((bg, tn, nb), lambda i, j, k: (k, j, 0))
            else:
                per = nb // tk
                assert nb % tk == 0
                b_spec = pl.BlockSpec((None, tn, tk), lambda i, j, k: (b_first + k // per, j, k % per))
        else:
            N = b.shape[0]
            b_spec = pl.BlockSpec((tn, tk), lambda i, j, k: (j, k))
        dims = NT
    else:
        K, M = a.shape
        N = b.shape[1]
        a_spec = pl.BlockSpec((tk, tm), lambda i, j, k: (k, i))
        b_spec = pl.BlockSpec((tk, tn), lambda i, j, k: (k, j))
        dims = TN
    if mode != "tn":
        if mode == "nn":
            K = b.shape[-2]
        elif not b_blocked:
            K = b.shape[1]
        a_spec = pl.BlockSpec((tm, tk), lambda i, j, k: (i, k * a_cols[0] + a_cols[1]))
    assert M % tm == 0 and N % tn == 0 and K % tk == 0, (name, M, N, K, tm, tn, tk)
    nk = K // tk
    if out_block is None:
        out_shape = jax.ShapeDtypeStruct((M, N * out_cols[0]), out_dtype)
        o_spec = pl.BlockSpec((tm, tn), lambda i, j, k: (i, j * out_cols[1] + out_cols[2]))
    else:
        out_shape = jax.ShapeDtypeStruct((o_count or N // out_block, M, out_block), out_dtype)
        if tn > out_block:
            og = tn // out_block
            assert tn % out_block == 0 and o_first % og == 0
            o_spec = pl.BlockSpec((og, tm, out_block), lambda i, j, k: (o_first // og + j, i, 0))
        else:
            per_o = out_block // tn
            assert out_block % tn == 0
            o_spec = pl.BlockSpec((None, tm, tn), lambda i, j, k: (o_first + j // per_o, i, j % per_o))
    in_specs = [a_spec, b_spec]
    args = [a, b]
    if res is not None:
        in_specs.append(pl.BlockSpec((tm, tn), lambda i, j, k: (i, j * out_cols[1] + out_cols[2])))
        args.append(res)
    n_in = len(args) + len(after) + (into is not None)

    def product(a_ref, b_ref):
        if bg == 1:
            return _dot(a_ref[...], b_ref[...], dims)
        nb = b_ref.shape[2]
        if mode == "nn":
            return jnp.concatenate([_dot(a_ref[...], b_ref[g], dims) for g in range(bg)], axis=1)
        return sum(_dot(a_ref[:, g * nb:(g + 1) * nb], b_ref[g], dims) for g in range(bg))

    def body(*refs):
        a_ref, b_ref = refs[0], refs[1]
        r_ref = refs[2] if res is not None else None
        o_ref = refs[n_in]

        def finish(acc):
            if r_ref is not None:
                acc = acc + r_ref[...]
            if og == 1:
                o_ref[...] = acc.astype(o_ref.dtype)
            else:
                for g in range(og):
                    o_ref[g] = acc[:, g * out_block:(g + 1) * out_block].astype(o_ref.dtype)

        if nk == 1:
            finish(product(a_ref, b_ref))
        else:
            acc_ref = refs[-1]
            k = pl.program_id(2)

            @pl.when(k == 0)
            def _():
                acc_ref[...] = jnp.zeros_like(acc_ref)

            acc_ref[...] += product(a_ref, b_ref)

            @pl.when(k == nk - 1)
            def _():
                finish(acc_ref[...])

    return pl.pallas_call(
        body, grid=(M // tm, N // tn, nk), in_specs=in_specs + [ANY] * (n_in - len(args)), out_specs=o_spec, out_shape=out_shape,
        scratch_shapes=[] if nk == 1 else [pltpu.VMEM((tm, tn), F32)], input_output_aliases={} if into is None else {n_in - 1: 0},
        compiler_params=_params("parallel", "parallel", "arbitrary"), name=name)(*args, *after, *(() if into is None else (into,)))


def _mm_fused(a, b, tiles, fn, outs, *, mode, tm, tn, tk, name, sums=(), after=(), b_outer=False):
    M, K = a.shape
    N = b.shape[1] if mode == "nn" else b.shape[0]
    nk = K // tk
    assert M % tm == 0 and N % tn == 0 and K % tk == 0 and (not sums or tn == N)
    def at(f):
        return (lambda j, i, k: f(i, j, k)) if b_outer else f

    b_spec = pl.BlockSpec((tk, tn), at(lambda i, j, k: (k, j))) if mode == "nn" else pl.BlockSpec((tn, tk), at(lambda i, j, k: (j, k)))
    in_specs = [pl.BlockSpec((tm, tk), at(lambda i, j, k: (i, k))), b_spec]
    args = [a, b]
    for t in tiles:
        if isinstance(t, tuple):
            arr, off = t
            in_specs.append(pl.BlockSpec((tm, tn), at(lambda i, j, k, off=off: (i, off + j))))
        else:
            arr = t
            in_specs.append(pl.BlockSpec((1, tn), at(lambda i, j, k: (0, j))))
        args.append(arr)
    n_in = len(args) + len(after)
    n_out = len(outs) + len(sums)

    def body(*refs):
        a_ref, b_ref = refs[0], refs[1]
        t_refs = refs[2:2 + len(tiles)]
        o_refs = refs[n_in:n_in + n_out]

        def finish(acc):
            res = fn(acc, *[t[...] for t in t_refs])
            for o_ref, r in zip(o_refs[:len(outs)], res):
                o_ref[...] = r.astype(o_ref.dtype)
            if sums:
                @pl.when(pl.program_id(0) == 0)
                def _():
                    for o_ref in o_refs[len(outs):]:
                        o_ref[...] = jnp.zeros_like(o_ref)

                for o_ref, r in zip(o_refs[len(outs):], res[len(outs):]):
                    o_ref[...] += jnp.broadcast_to(r, o_ref.shape)

        if nk == 1:
            finish(_dot(a_ref[...], b_ref[...], NN if mode == "nn" else NT))
        else:
            acc_ref = refs[-1]
            k = pl.program_id(2)

            @pl.when(k == 0)
            def _():
                acc_ref[...] = jnp.zeros_like(acc_ref)

            acc_ref[...] += _dot(a_ref[...], b_ref[...], NN if mode == "nn" else NT)

            @pl.when(k == nk - 1)
            def _():
                finish(acc_ref[...])

    assert not (b_outer and sums)
    o_spec = pl.BlockSpec((tm, tn), at(lambda i, j, k: (i, j)))
    return pl.pallas_call(
        body, grid=(N // tn, M // tm, nk) if b_outer else (M // tm, N // tn, nk), in_specs=in_specs + [ANY] * len(after),
        out_specs=[o_spec] * len(outs) + [pl.BlockSpec(sh, lambda i, j, k: (0, 0)) for sh in sums],
        out_shape=[jax.ShapeDtypeStruct((M, N), dt) for dt in outs] + [jax.ShapeDtypeStruct(sh, F32) for sh in sums],
        scratch_shapes=[] if nk == 1 else [pltpu.VMEM((tm, tn), F32)],
        compiler_params=_params("arbitrary" if sums else "parallel", "parallel", "arbitrary"), name=name)(*args, *after)


def _row_spec(d, col=0):
    return pl.BlockSpec((ROWS, d), lambda i: (i, col))


def _vec_spec(d):
    return pl.BlockSpec((1, d), lambda i: (0, 0))


def _rms(x):
    return lax.rsqrt(jnp.mean(x * x, axis=-1, keepdims=True) + EPS)


def _norm_fwd(x, g, name):
    T, D = x.shape

    def body(x_ref, g_ref, h_ref):
        xv = x_ref[...]
        h_ref[...] = (xv * _rms(xv) * g_ref[...]).astype(BF16)

    return pl.pallas_call(body, grid=(T // ROWS,), in_specs=[_row_spec(D), _vec_spec(D)], out_specs=_row_spec(D),
                          out_shape=jax.ShapeDtypeStruct((T, D), BF16), compiler_params=_params("parallel"), name=name)(x, g)


def _residual_norm(y, x, g):
    x = x + y
    return x, x * _rms(x) * g


def _norm_bwd_math(dh, xv, gv):
    r = _rms(xv)
    xhat = xv * r
    dxhat = dh * gv
    dx = r * (dxhat - xhat * jnp.mean(dxhat * xhat, axis=-1, keepdims=True))
    dg = jnp.sum(dh * xhat, axis=0, keepdims=True)
    return dx, dg


def _norm_bwd(dh, x, g, dres, name):
    T, D = x.shape

    def body(dh_ref, x_ref, g_ref, dres_ref, dx_ref, dxb_ref, dg_ref):
        dx, dg = _norm_bwd_math(dh_ref[...], x_ref[...], g_ref[...])
        dx = dx + dres_ref[...]
        dx_ref[...] = dx
        dxb_ref[...] = dx.astype(BF16)

        @pl.when(pl.program_id(0) == 0)
        def _():
            dg_ref[...] = jnp.zeros_like(dg_ref)

        dg_ref[...] += dg

    return pl.pallas_call(
        body, grid=(T // ROWS,), in_specs=[_row_spec(D), _row_spec(D), _vec_spec(D), _row_spec(D)],
        out_specs=[_row_spec(D), _row_spec(D), _vec_spec(D)],
        out_shape=[jax.ShapeDtypeStruct((T, D), F32), jax.ShapeDtypeStruct((T, D), BF16), jax.ShapeDtypeStruct((1, D), F32)],
        compiler_params=_params("arbitrary"), name=name)(dh, x, g, dres)


def _mm_merge(y_sb, y_ca, w_sb, w_ca, proj, gate_col, tm, tn, after):
    T, W = y_sb.shape
    G, _, nb = w_sb.shape
    D, bg = G * nb, tn // nb
    assert tn % nb == 0 and D % tn == 0
    per = D // tn

    def body(ys_ref, yc_ref, ws_ref, wc_ref, gs_ref, gc_ref, *rest):
        as_ref, ac_ref, m_ref = rest[len(after):]
        a = jnp.concatenate([_dot(ys_ref[...], ws_ref[g]) for g in range(bg)], axis=1)
        b = jnp.concatenate([_dot(yc_ref[...], wc_ref[g]) for g in range(bg)], axis=1)
        as_ref[...] = a
        ac_ref[...] = b
        m_ref[...] = (jax.nn.sigmoid(gs_ref[...]) * a + jax.nn.sigmoid(gc_ref[...]) * b).astype(BF16)

    y_spec = pl.BlockSpec((tm, W), lambda i, j: (i, 0))
    w_spec = pl.BlockSpec((bg, W, nb), lambda i, j: (j, 0, 0))
    out = pl.BlockSpec((tm, tn), lambda i, j: (i, j))
    f32 = jax.ShapeDtypeStruct((T, D), F32)
    return pl.pallas_call(
        body, grid=(T // tm, per),
        in_specs=[y_spec, y_spec, w_spec, w_spec, pl.BlockSpec((tm, tn), lambda i, j: (i, gate_col * per + j)),
                  pl.BlockSpec((tm, tn), lambda i, j: (i, (gate_col + 1) * per + j))] + [ANY] * len(after),
        out_specs=[out, out, out], out_shape=[f32, f32, jax.ShapeDtypeStruct((T, D), BF16)],
        compiler_params=_params("parallel", "parallel"), name="mm_merge")(y_sb, y_ca, w_sb, w_ca, proj, proj, *after)


def _merge_bwd(dm, gs, gc, a, b):
    ss, sc = jax.nn.sigmoid(gs), jax.nn.sigmoid(gc)
    return dm * ss, dm * sc, dm * a * ss * (1.0 - ss), dm * b * sc * (1.0 - sc)


def _mm_swiglu(h, w, tm):
    T, D = h.shape
    G2, _, nb = w.shape
    G = G2 // 2

    def body(h_ref, wg_ref, wu_ref, g_ref, u_ref, act_ref):
        hv = h_ref[...]
        gv = _dot(hv, wg_ref[...])
        uv = _dot(hv, wu_ref[...])
        g_ref[...] = gv
        u_ref[...] = uv
        act_ref[...] = (gv * jax.nn.sigmoid(gv) * uv).astype(BF16)

    out = pl.BlockSpec((tm, nb), lambda j, i: (i, j))
    f32 = jax.ShapeDtypeStruct((T, G * nb), F32)
    return pl.pallas_call(
        body, grid=(G, T // tm),
        in_specs=[pl.BlockSpec((tm, D), lambda j, i: (i, 0)), pl.BlockSpec((None, D, nb), lambda j, i: (j, 0, 0)),
                  pl.BlockSpec((None, D, nb), lambda j, i: (j + G, 0, 0))],
        out_specs=[out, out, out], out_shape=[f32, f32, jax.ShapeDtypeStruct((T, G * nb), BF16)],
        compiler_params=_params("parallel", "parallel"), name="mm_ffn_in")(h, w, w)


def _swiglu_bwd(dact, gate, up):
    s = jax.nn.sigmoid(gate)
    return dact * up * s * (1.0 + gate * (1.0 - s)), dact * gate * s


def _tail(zg, x3, pe, target, g_final):
    D = x3.shape[-1]
    gate = jax.nn.sigmoid(zg)
    x4 = x3 + gate * pe
    err = x4 * _rms(x4) * g_final - target
    part = 0.5 * jnp.sum(jnp.mean(err * err, axis=-1, keepdims=True), axis=0, keepdims=True)
    dx, dg = _norm_bwd_math(err * (1.0 / D), x4, g_final)
    return dx, dx * gate, dx * pe * gate * (1.0 - gate), part, dg


def _cast_bf16(x, name):
    R, C = x.shape
    rows = next(r for r in (ROWS, 128, 64, 32, 16) if R % r == 0)

    def body(x_ref, o_ref):
        o_ref[...] = x_ref[...].astype(BF16)

    spec = pl.BlockSpec((rows, C), lambda i: (i, 0))
    return pl.pallas_call(body, grid=(R // rows,), in_specs=[spec], out_specs=spec, out_shape=jax.ShapeDtypeStruct((R, C), BF16),
                          compiler_params=_params("parallel"), name=name)(x)


def _head_spec(T, col0, heads=1):
    return pl.BlockSpec((T, heads * HEAD_DIM), lambda h, *_: (0, col0 + h))


SB_HEADS = 4


def _triangle(n, right):
    j = lax.broadcasted_iota(jnp.int32, (n, n), 0)
    s = lax.broadcasted_iota(jnp.int32, (n, n), 1)
    return jnp.where((j > s) if right else (j < s), 1.0, 0.0).astype(BF16)


def _lane_scan(x, tri):
    hi = x.astype(BF16)
    lo = (x - hi.astype(F32)).astype(BF16)
    return _dot(hi, tri) + _dot(lo, tri)


def _head_cols(ref, rows, hh):
    return ref[rows, hh * HEAD_DIM:(hh + 1) * HEAD_DIM]


def _sb_tile(qv, kk, past, c_lk, tri):
    z = _dot(qv, kk, NT) * SCALE
    sp = jnp.log(1.0 + jnp.exp(-jnp.abs(z)))
    ls_pos = jnp.minimum(z, 0.0) - sp
    lk = jnp.minimum(-z, 0.0) - sp
    if past is not None:
        lk = jnp.where(past, lk, 0.0)
    right = c_lk + _lane_scan(lk, tri)
    a = jnp.exp(ls_pos + right)
    if past is not None:
        a = jnp.where(past, a, 0.0)
    return ls_pos, a, right[:, 0:1] + lk[:, 0:1]


def _sb_diagonal():
    B = SB_BLOCK
    return lax.broadcasted_iota(jnp.int32, (B, B), 1) < lax.broadcasted_iota(jnp.int32, (B, B), 0)


def _sb_rows(kb):
    return pl.ds(pl.multiple_of(kb * SB_BLOCK, SB_BLOCK), SB_BLOCK)


def _sb_fwd(proj, n_heads, after=()):
    T = proj.shape[0]
    B, HP = SB_BLOCK, SB_HEADS
    assert n_heads % HP == 0

    def body(q_ref, k_ref, v_ref, *rest):
        y_ref = rest[-1]
        qb = pl.program_id(1)
        tri = _triangle(B, right=True)
        qv = [_head_cols(q_ref, slice(None), hh).astype(BF16) for hh in range(HP)]

        def tile(kb, carry, past):
            out = []
            for hh in range(HP):
                acc, c_lk = carry[hh]
                kk = _head_cols(k_ref, _sb_rows(kb), hh).astype(BF16)
                vv = _head_cols(v_ref, _sb_rows(kb), hh).astype(BF16)
                _, a, c_lk = _sb_tile(qv[hh], kk, past, c_lk, tri)
                out.append((acc + _dot(a.astype(BF16), vv), c_lk))
            return tuple(out)

        init = tuple((jnp.zeros((B, HEAD_DIM), F32), jnp.zeros((B, 1), F32)) for _ in range(HP))
        res = lax.fori_loop(1, qb + 1, lambda i, carry: tile(qb - i, carry, None), tile(qb, init, _sb_diagonal()))
        for hh in range(HP):
            y_ref[:, hh * HEAD_DIM:(hh + 1) * HEAD_DIM] = res[hh][0].astype(BF16)

    blk = pl.BlockSpec((B, HP * HEAD_DIM), lambda h, i: (i, h))
    G = n_heads // HP
    return pl.pallas_call(
        body, grid=(G, T // B),
        in_specs=[blk, _head_spec(T, G, HP), _head_spec(T, 2 * G, HP)] + [ANY] * len(after), out_specs=blk,
        out_shape=jax.ShapeDtypeStruct((T, n_heads * HEAD_DIM), BF16),
        compiler_params=_params("parallel", "arbitrary"), name="sb_fwd")(proj, proj, proj, *after)


def _sb_bwd(proj, dy, n_heads):
    T = proj.shape[0]
    B, HP = SB_BLOCK, SB_HEADS
    nq = T // B

    def body(q_ref, k_ref, v_ref, dy_ref, dq_ref, dk_ref, dv_ref, g_s, sig_s, dk_s, dv_s):
        qb = pl.program_id(1)

        @pl.when(qb == 0)
        def _():
            dk_s[...] = jnp.zeros_like(dk_s)
            dv_s[...] = jnp.zeros_like(dv_s)

        tri_r = _triangle(B, right=True)
        tri_l = _triangle(B, right=False)
        qv = [_head_cols(q_ref, slice(None), hh).astype(BF16) for hh in range(HP)]
        dyb = [_head_cols(dy_ref, slice(None), hh).astype(BF16) for hh in range(HP)]

        def sweep(kb, carry, past):
            out = []
            for hh in range(HP):
                kk = _head_cols(k_ref, _sb_rows(kb), hh).astype(BF16)
                vv = _head_cols(v_ref, _sb_rows(kb), hh).astype(BF16)
                ls_pos, a, c_lk = _sb_tile(qv[hh], kk, past, carry[hh], tri_r)
                g_s[hh, kb] = _dot(dyb[hh], vv, NT) * a
                sig_s[hh, kb] = jnp.exp(ls_pos)
                dv_s[_sb_rows(kb), hh * HEAD_DIM:(hh + 1) * HEAD_DIM] += _dot(a.astype(BF16), dyb[hh], TN)
                out.append(c_lk)
            return tuple(out)

        zeros = tuple(jnp.zeros((B, 1), F32) for _ in range(HP))
        lax.fori_loop(1, qb + 1, lambda i, carry: sweep(qb - i, carry, None), sweep(qb, zeros, _sb_diagonal()))

        def back(kb, carry, past):
            out = []
            for hh in range(HP):
                dq, c_g = carry[hh]
                kk = _head_cols(k_ref, _sb_rows(kb), hh).astype(BF16)
                g, sig = g_s[hh, kb], sig_s[hh, kb]
                left = c_g + _lane_scan(g, tri_l)
                dz = g * (1.0 - sig) - left * sig
                if past is not None:
                    dz = jnp.where(past, dz, 0.0)
                dz = (dz * SCALE).astype(BF16)
                dk_s[_sb_rows(kb), hh * HEAD_DIM:(hh + 1) * HEAD_DIM] += _dot(dz, qv[hh], TN)
                out.append((dq + _dot(dz, kk), left[:, B - 1:B] + g[:, B - 1:B]))
            return tuple(out)

        init = tuple((jnp.zeros((B, HEAD_DIM), F32), jnp.zeros((B, 1), F32)) for _ in range(HP))
        res = back(qb, lax.fori_loop(0, qb, lambda kb, carry: back(kb, carry, None), init), _sb_diagonal())
        for hh in range(HP):
            dq_ref[:, hh * HEAD_DIM:(hh + 1) * HEAD_DIM] = res[hh][0].astype(BF16)

        @pl.when(qb == nq - 1)
        def _():
            dk_ref[...] = dk_s[...].astype(BF16)
            dv_ref[...] = dv_s[...].astype(BF16)

    blk = pl.BlockSpec((B, HP * HEAD_DIM), lambda h, i: (i, h))
    G = n_heads // HP
    full = _head_spec(T, 0, HP)
    shp = jax.ShapeDtypeStruct((T, n_heads * HEAD_DIM), BF16)
    return pl.pallas_call(
        body, grid=(G, nq),
        in_specs=[blk, _head_spec(T, G, HP), _head_spec(T, 2 * G, HP), blk], out_specs=[blk, full, full],
        out_shape=[shp, shp, shp],
        scratch_shapes=[pltpu.VMEM((HP, nq, B, B), F32)] * 2 + [pltpu.VMEM((T, HP * HEAD_DIM), F32)] * 2,
        compiler_params=_params("parallel", "arbitrary"), name="sb_bwd")(proj, proj, proj, dy)


DIAGS = PBAND + PAIR


def _diag_onehot():
    d = lax.broadcasted_iota(jnp.int32, (DIAGS, 2 * PAIR), 0)
    r = lax.broadcasted_iota(jnp.int32, (DIAGS, 2 * PAIR), 1)
    return jnp.where(jnp.clip(d - PAIR - PAD, -REL_CLIP, CHUNK - 1) + REL_CLIP == r, 1.0, 0.0)


def _bias_expand(rel_bias):
    H = rel_bias.shape[0]
    table = jnp.pad(rel_bias, ((0, 0), (0, 2 * PAIR - N_REL)))

    def body(rb_ref, o_ref):
        o_ref[...] = lax.dot_general(rb_ref[...], _diag_onehot(), NT, precision=lax.Precision.HIGHEST, preferred_element_type=F32)

    per_diag = pl.pallas_call(body, out_shape=jax.ShapeDtypeStruct((H, DIAGS), F32), name="bias_expand")(table)
    flat = jnp.tile(jnp.pad(per_diag, ((0, 0), (0, 1))), (1, PAIR))[:, :PAIR * DIAGS]
    return flat.reshape(H, PAIR, DIAGS)[:, :, PAIR:]


def _bias_reduce(dbias):
    H = dbias.shape[0]
    padded = jnp.pad(dbias, ((0, 0), (0, 1), (PAIR, 0))).reshape(H, -1)
    skewed = padded[:, :PAIR * (DIAGS + 1)].reshape(H, PAIR, DIAGS + 1)[:, :, :DIAGS]

    def body(s_ref, o_ref):
        per_diag = jnp.sum(s_ref[...], axis=0, keepdims=True)
        o_ref[...] = lax.dot_general(jnp.broadcast_to(per_diag, (8, DIAGS)), _diag_onehot(), NN, precision=lax.Precision.HIGHEST,
                                     preferred_element_type=F32)[0:1]

    return pl.pallas_call(
        body, grid=(H,), in_specs=[pl.BlockSpec((None, PAIR, DIAGS), lambda h: (h, 0, 0))],
        out_specs=pl.BlockSpec((None, 1, 2 * PAIR), lambda h: (h, 0, 0)),
        out_shape=jax.ShapeDtypeStruct((H, 1, 2 * PAIR), F32), compiler_params=_params("parallel"), name="bias_reduce")(skewed)[:, 0]


CA_HEADS = 2


def _ca_mask():
    i = lax.broadcasted_iota(jnp.int32, (CA_ROWS, CA_BAND), 0)
    j = lax.broadcasted_iota(jnp.int32, (CA_ROWS, CA_BAND), 1)
    qc, kc = i // CHUNK, j // CHUNK
    return j, (kc >= qc) & (kc <= qc + LEFT_CHUNKS)


def _ca_bias(pair_bias):
    rows = []
    for q in range(CA_PAIRS):
        parts = [jnp.zeros((PAIR, q * PAIR), F32)] * (q > 0) + [pair_bias] + [jnp.zeros((PAIR, (CA_PAIRS - 1 - q) * PAIR), F32)] * (q < CA_PAIRS - 1)
        rows.append(jnp.concatenate(parts, axis=1) if len(parts) > 1 else parts[0])
    return jnp.concatenate(rows, axis=0)


def _ca_weights(pr, qp, kb, bias, j, window):
    valid = window & (pr * CA_ROWS + j >= PAD)
    z = jnp.where(valid, _dot(qp, kb, NT) * SCALE + bias, NEG)
    e = jnp.exp(z - jnp.max(z, axis=1, keepdims=True))
    return e / jnp.sum(e, axis=1, keepdims=True)


def _ca_fill(k_ref, v_ref, kpad, vpad):
    T, W = k_ref.shape
    kpad[0:PAD, :] = jnp.zeros((PAD, W), BF16)
    vpad[0:PAD, :] = jnp.zeros((PAD, W), BF16)
    kpad[PAD:PAD + T, :] = k_ref[...].astype(BF16)
    vpad[PAD:PAD + T, :] = v_ref[...].astype(BF16)


def _ca_fwd(proj, bias, n_heads, col0):
    T = proj.shape[0]
    HP = CA_HEADS
    G = n_heads // HP
    assert n_heads % HP == 0 and col0 % HP == 0

    def body(q_ref, k_ref, v_ref, b_ref, y_ref, kpad, vpad):
        _ca_fill(k_ref, v_ref, kpad, vpad)
        j, window = _ca_mask()
        bias = [_ca_bias(b_ref[hh]) for hh in range(HP)]

        def step(pr, _):
            r0 = pl.multiple_of(pr * CA_ROWS, CA_ROWS)
            for hh in range(HP):
                qp = _head_cols(q_ref, pl.ds(r0, CA_ROWS), hh).astype(BF16)
                kb = _head_cols(kpad, pl.ds(r0, CA_BAND), hh)
                vb = _head_cols(vpad, pl.ds(r0, CA_BAND), hh)
                w = _ca_weights(pr, qp, kb, bias[hh], j, window)
                y_ref[pl.ds(r0, CA_ROWS), hh * HEAD_DIM:(hh + 1) * HEAD_DIM] = _dot(w.astype(BF16), vb).astype(BF16)
            return 0

        lax.fori_loop(0, T // CA_ROWS, step, 0)

    c = col0 // HP
    return pl.pallas_call(
        body, grid=(G,),
        in_specs=[_head_spec(T, c, HP), _head_spec(T, c + G, HP), _head_spec(T, c + 2 * G, HP),
                  pl.BlockSpec((HP, PAIR, PBAND), lambda h: (h, 0, 0))],
        out_specs=_head_spec(T, 0, HP), out_shape=jax.ShapeDtypeStruct((T, n_heads * HEAD_DIM), BF16),
        scratch_shapes=[pltpu.VMEM((PAD + T, HP * HEAD_DIM), BF16)] * 2,
        compiler_params=_params("parallel"), name="ca_fwd")(proj, proj, proj, bias)


def _ca_bwd(proj, bias, dy, n_heads, col0):
    T = proj.shape[0]
    HP = CA_HEADS
    G = n_heads // HP

    def body(q_ref, k_ref, v_ref, b_ref, dy_ref, dq_ref, dk_ref, dv_ref, db_ref, kpad, vpad, dkpad, dvpad):
        _ca_fill(k_ref, v_ref, kpad, vpad)
        dkpad[...] = jnp.zeros_like(dkpad)
        dvpad[...] = jnp.zeros_like(dvpad)
        db_ref[...] = jnp.zeros_like(db_ref)
        j, window = _ca_mask()
        bias = [_ca_bias(b_ref[hh]) for hh in range(HP)]

        def step(pr, _):
            r0 = pl.multiple_of(pr * CA_ROWS, CA_ROWS)
            for hh in range(HP):
                cols = slice(hh * HEAD_DIM, (hh + 1) * HEAD_DIM)
                qp = _head_cols(q_ref, pl.ds(r0, CA_ROWS), hh).astype(BF16)
                kb = _head_cols(kpad, pl.ds(r0, CA_BAND), hh)
                vb = _head_cols(vpad, pl.ds(r0, CA_BAND), hh)
                w = _ca_weights(pr, qp, kb, bias[hh], j, window)
                dyp = _head_cols(dy_ref, pl.ds(r0, CA_ROWS), hh).astype(BF16)
                dw = _dot(dyp, vb, NT)
                dz = w * (dw - jnp.sum(dw * w, axis=1, keepdims=True))
                db_ref[hh] += sum(dz[q * PAIR:(q + 1) * PAIR, q * PAIR:q * PAIR + PBAND] for q in range(CA_PAIRS))
                dzs = (dz * SCALE).astype(BF16)
                dq_ref[pl.ds(r0, CA_ROWS), cols] = _dot(dzs, kb).astype(BF16)
                dkpad[pl.ds(r0, CA_BAND), cols] += _dot(dzs, qp, TN)
                dvpad[pl.ds(r0, CA_BAND), cols] += _dot(w.astype(BF16), dyp, TN)
            return 0

        lax.fori_loop(0, T // CA_ROWS, step, 0)
        dk_ref[...] = dkpad[PAD:PAD + T, :].astype(BF16)
        dv_ref[...] = dvpad[PAD:PAD + T, :].astype(BF16)

    c = col0 // HP
    full = _head_spec(T, 0, HP)
    bspec = pl.BlockSpec((HP, PAIR, PBAND), lambda h: (h, 0, 0))
    shp = jax.ShapeDtypeStruct((T, n_heads * HEAD_DIM), BF16)
    return pl.pallas_call(
        body, grid=(G,),
        in_specs=[_head_spec(T, c, HP), _head_spec(T, c + G, HP), _head_spec(T, c + 2 * G, HP), bspec, full],
        out_specs=[full, full, full, bspec],
        out_shape=[shp, shp, shp, jax.ShapeDtypeStruct((n_heads, PAIR, PBAND), F32)],
        scratch_shapes=[pltpu.VMEM((PAD + T, HP * HEAD_DIM), BF16)] * 2 + [pltpu.VMEM((PAD + T, HP * HEAD_DIM), F32)] * 2,
        compiler_params=_params("parallel"), name="ca_bwd")(proj, proj, proj, bias, dy)


def _local_step(x, p, target, comm, g):
    T, D = x.shape
    H = g["rel_bias"].shape[0]
    W = H * HEAD_DIM
    nb_in = comm.shapes["w_in_a"][2]
    nb_ff = comm.shapes["w_ffn_in"][2]
    nb_o = comm.shapes["w_sb_out"][2]
    nb_p = comm.shapes["w_ple_in"][2]
    tm = min(T, 1024)
    tn = min(D, 1024)
    gate_col = 6 * W // D

    h1 = _norm_fwd(x, g["g_mix"], "norm1")
    comm.stage("norm1", h1)
    proj = _mm(h1, comm.weight("w_in_a", h1), mode="nn", tm=tm, tn=nb_in, tk=D // 2, out_dtype=F32, b_blocked=True,
               after=comm.pending(), name="mm_in_a")
    comm.stage("mm_in_a", proj)
    proj = _mm(h1, comm.weight("w_in_b", proj), mode="nn", tm=tm, tn=nb_in, tk=D // 2, out_dtype=F32, b_blocked=True, a_cols=(1, 1),
               res=proj, after=comm.pending(), name="mm_in")
    comm.stage("mm_in", proj)
    y_sb = _sb_fwd(proj, H, comm.pending())
    bias = _bias_expand(g["rel_bias"])
    y_ca = _ca_fwd(proj, bias, H, 3 * H)
    comm.stage("attention", y_sb, y_ca)
    a_sb, a_ca, merged = _mm_merge(y_sb, y_ca, comm.weight("w_sb_out", y_ca), comm.weight("w_ca_out"), proj, gate_col, min(T, 512), tn, comm.pending())
    x2, h2 = _mm_fused(merged, comm.weight("w_mix_out"), [(x, 0), g["g_ffn"]], _residual_norm, [F32, BF16],
                       mode="nn", tm=min(T, 512), tn=D, tk=D, name="mm_mix")
    gate, up, act = _mm_swiglu(h2, comm.weight("w_ffn_in", h2), min(T, 512))
    comm.stage("mm_ffn_in", act)
    F = act.shape[1]
    tkf = F // 2 if F % 256 == 0 else F
    x3 = _mm(act, comm.weight("w_ffn_out", act), mode="nn", tm=tm, tn=tn, tk=tkf, out_dtype=F32, res=x2, after=comm.pending(), name="mm_ffn_out")
    h3 = _norm_fwd(x3, g["g_ple"], "norm3")
    pb = _cast_bf16(p, "cast_p")
    P = p.shape[1]
    pe = _mm(pb, comm.weight("w_ple_in"), mode="nn", tm=tm, tn=nb_p, tk=P, out_dtype=F32, b_blocked=True, name="mm_ple_in")
    dx4, dpe, dzg, loss, dg_final = _mm_fused(
        h3, comm.weight("w_ple_gate"), [(x3, 0), (pe, 0), (target, 0), g["g_final"]], _tail, [F32, BF16, BF16],
        mode="nn", tm=min(T, 256), tn=D, tk=D, sums=[(1, 128), (1, D)], name="mm_ple_gate")

    tw = min(D, 1024)
    DW = BF16
    comm.grad("w_ple_in", _mm(pb, dpe, mode="tn", tm=P, tn=nb_p, tk=T, out_dtype=DW, out_block=nb_p, name="mm_d_ple_in"))
    comm.grad("w_ple_gate", _mm(h3, dzg, mode="tn", tm=tw, tn=tn, tk=T, out_dtype=DW, name="mm_d_ple_gate"))
    dh3 = _mm(dzg, comm.weight("w_ple_gate"), mode="nt", tm=tm, tn=tn, tk=D, out_dtype=F32, after=comm.pending(), name="mm_dh3")
    dx3, dx3b, dg_ple = _norm_bwd(dh3, x3, g["g_ple"], dx4, "norm3_bwd")
    comm.grad("w_ffn_out", _mm(act, dx3b, mode="tn", tm=F // 4, tn=tn, tk=T, out_dtype=DW, name="mm_d_ffn_out"))
    dgate, dup = _mm_fused(dx3b, comm.weight("w_ffn_out"), [(gate, 0), (up, 0)], _swiglu_bwd, [BF16, BF16],
                           mode="nt", tm=min(T, 512), tn=nb_ff, tk=D, after=comm.pending(), b_outer=True, name="mm_dact")
    half = comm.shapes["w_ffn_in"][0] // 2
    d_ffn_in = _mm(h2, dgate, mode="tn", tm=tw, tn=nb_ff, tk=T, out_dtype=DW, out_block=nb_ff, o_count=2 * half, name="mm_d_ffn_in_gate")
    comm.grad("w_ffn_in", _mm(h2, dup, mode="tn", tm=tw, tn=nb_ff, tk=T, out_dtype=DW, out_block=nb_ff, o_first=half, o_count=2 * half,
                              into=d_ffn_in, name="mm_d_ffn_in"))
    dh2 = _mm(dgate, comm.weight("w_ffn_in"), mode="nt", tm=tm, tn=D, tk=nb_ff, out_dtype=F32, b_blocked=True, b_count=half,
              after=comm.pending(), name="mm_dh2_gate")
    dh2 = _mm(dup, comm.weight("w_ffn_in"), mode="nt", tm=min(T, 512), tn=D, tk=nb_ff, out_dtype=F32, b_blocked=True, b_first=half, b_count=half,
              res=dh2, name="mm_dh2")
    dx2, dx2b, dg_ffn = _norm_bwd(dh2, x2, g["g_ffn"], dx3, "norm2_bwd")
    per = D // tn
    da_sb, da_ca, dgate_sb, dgate_ca = _mm_fused(
        dx2b, comm.weight("w_mix_out"), [(proj, gate_col * per), (proj, (gate_col + 1) * per), (a_sb, 0), (a_ca, 0)], _merge_bwd, [BF16] * 4,
        mode="nt", tm=min(T, 512), tn=tn, tk=D, name="mm_dmerged")
    comm.grad("w_mix_out", _mm(merged, dx2b, mode="tn", tm=tw, tn=tn, tk=T, out_dtype=DW, name="mm_d_mix"))
    comm.grad("w_sb_out", _mm(y_sb, da_sb, mode="tn", tm=min(W, 512), tn=tn, tk=T, out_dtype=DW, out_block=nb_o, name="mm_d_sb_out"))
    comm.grad("w_ca_out", _mm(y_ca, da_ca, mode="tn", tm=min(W, 512), tn=tn, tk=T, out_dtype=DW, out_block=nb_o, name="mm_d_ca_out"))
    dy_sb = _mm(da_sb, comm.weight("w_sb_out"), mode="nt", tm=tm, tn=W, tk=tn, out_dtype=F32, b_blocked=True, after=comm.pending(), name="mm_dy_sb")
    dy_ca = _mm(da_ca, comm.weight("w_ca_out"), mode="nt", tm=tm, tn=W, tk=tn, out_dtype=F32, b_blocked=True, name="mm_dy_ca")
    dq_sb, dk_sb, dv_sb = _sb_bwd(proj, dy_sb, H)
    dq_ca, dk_ca, dv_ca, dbias = _ca_bwd(proj, bias, dy_ca, H, 3 * H)
    d_rel = _bias_reduce(dbias)[:, :N_REL]
    dproj = jnp.concatenate([dq_sb, dk_sb, dv_sb, dq_ca, dk_ca, dv_ca, dgate_sb, dgate_ca], axis=1)
    comm.grad("w_in", _mm(h1, dproj, mode="tn", tm=tw, tn=nb_in, tk=T, out_dtype=DW, out_block=nb_in, name="mm_d_in"))
    dh1 = _mm(dproj, comm.weight("w_in_a"), mode="nt", tm=tm, tn=D // 2, tk=nb_in, out_dtype=F32, b_blocked=True, out_cols=(2, 1, 0),
              after=comm.pending(), name="mm_dh1_a")
    comm.pair_done(dh1)
    dh1 = _mm(dproj, comm.weight("w_in_b"), mode="nt", tm=tm, tn=D // 2, tk=nb_in, out_dtype=F32, b_blocked=True, out_cols=(2, 1, 1),
              into=dh1, after=comm.pending(), name="mm_dh1")
    grad_x, _, dg_mix = _norm_bwd(dh1, x, g["g_mix"], dx2, "norm1_bwd")
    small = dict(g_mix=dg_mix, g_ffn=dg_ffn, g_ple=dg_ple, g_final=dg_final, rel_bias=d_rel)
    return loss, grad_x, small


def _position():
    x, y, c = lax.axis_index("x"), lax.axis_index("y"), lax.axis_index("c")
    return x, y, c


def _block_of(px, py, pc):
    return 4 * px + 2 * py + pc


def _flip(pos, k):
    x, y, c = pos
    return (1 - x if k & 4 else x, 1 - y if k & 2 else y, 1 - c if k & 1 else c)


HBM = pl.BlockSpec(memory_space=pltpu.HBM)
SEM = pl.BlockSpec(memory_space=pltpu.SEMAPHORE)
VMEM_SPEC = pl.BlockSpec(memory_space=pltpu.VMEM)
EFFECT = pltpu.SideEffectType.DATAFLOW_SIDE_EFFECTING
TOKEN = jax.ShapeDtypeStruct((8, 128), F32)


def _hbm(a):
    return pltpu.HBM(a.shape, a.dtype)


def _landing(shape, dtype):
    return pltpu.with_memory_space_constraint(lax.empty(shape, dtype), pltpu.HBM)


def _gather_start(lands, after, name):
    n = len(lands)

    def body(*refs):
        ins = refs[:n]
        send, recv = refs[n + 1], refs[n + 2]
        token = refs[-1]
        x, y, c = _position()
        mine = _block_of(x, y, c)
        peers = [(x, y, 1 - c), (1 - x, y, c), (x, 1 - y, c), (1 - x, 1 - y, c)]
        for wi in range(n):
            for k, peer in enumerate(peers):
                pltpu.make_async_remote_copy(
                    src_ref=ins[wi].at[mine], dst_ref=ins[wi].at[mine], send_sem=send.at[4 * wi + k], recv_sem=recv.at[4 * wi + k],
                    device_id=peer, device_id_type=MESH).start()
        token[...] = jnp.zeros_like(token)

    outs = pl.pallas_call(
        body, name=name, in_specs=[HBM] * n + [ANY], out_specs=(SEM, SEM, *[HBM] * n, VMEM_SPEC),
        out_shape=(pltpu.SemaphoreType.DMA((4 * n,)), pltpu.SemaphoreType.DMA((4 * n,)), *[_hbm(a) for a in lands], TOKEN),
        input_output_aliases={i: 2 + i for i in range(n)},
        compiler_params=pltpu.CompilerParams(has_side_effects=EFFECT))(*[pltpu.with_memory_space_constraint(a, pltpu.HBM) for a in lands], after)
    return outs[0], outs[1], list(outs[2:2 + n]), outs[-1]


def _gather_forward(lands, send0, recv0, after, name):
    n = len(lands)

    def body(*refs):
        ins = refs[:n]
        send0, recv0 = refs[n], refs[n + 1]
        send1, recv1 = refs[n + 2 + len(after)], refs[n + 3 + len(after)]
        token = refs[-1]
        x, y, c = _position()
        chips = [(1 - x, y), (x, 1 - y), (1 - x, 1 - y)]
        for wi in range(n):
            for j, chip in enumerate(chips):
                rows = ins[wi].at[_block_of(*chip, c)]
                pltpu.make_async_remote_copy(
                    src_ref=rows, dst_ref=rows, send_sem=send0.at[4 * wi + 1 + j], recv_sem=recv0.at[4 * wi + 1 + j],
                    device_id=(*chip, c), device_id_type=MESH).wait_recv()
                pltpu.make_async_remote_copy(
                    src_ref=rows, dst_ref=rows, send_sem=send1.at[3 * wi + j], recv_sem=recv1.at[3 * wi + j],
                    device_id=(x, y, 1 - c), device_id_type=MESH).start()
        token[...] = jnp.zeros_like(token)

    outs = pl.pallas_call(
        body, name=name, in_specs=[HBM] * n + [SEM, SEM] + [ANY] * len(after), out_specs=(SEM, SEM, *[HBM] * n, VMEM_SPEC),
        out_shape=(pltpu.SemaphoreType.DMA((3 * n,)), pltpu.SemaphoreType.DMA((3 * n,)), *[_hbm(a) for a in lands], TOKEN),
        input_output_aliases={i: 2 + i for i in range(n)},
        compiler_params=pltpu.CompilerParams(has_side_effects=EFFECT))(*lands, send0, recv0, *after)
    return outs[0], outs[1], list(outs[2:2 + n]), outs[-1]


def _gather_wait(lands, send0, recv0, send1, recv1, after, name):
    n = len(lands)

    def body(*refs):
        ins = refs[:n]
        send0, recv0, send1, recv1 = refs[n:n + 4]
        x, y, c = _position()
        mine = _block_of(x, y, c)
        sibling = (x, y, 1 - c)
        peers = [sibling, (1 - x, y, c), (x, 1 - y, c), (1 - x, 1 - y, c)]
        chips = [(1 - x, y), (x, 1 - y), (1 - x, 1 - y)]
        for wi in range(n):
            own = ins[wi].at[mine]
            for k, peer in enumerate(peers):
                pltpu.make_async_remote_copy(src_ref=own, dst_ref=own, send_sem=send0.at[4 * wi + k], recv_sem=recv0.at[4 * wi + k],
                                             device_id=peer, device_id_type=MESH).wait_send()
            theirs = ins[wi].at[_block_of(*sibling)]
            pltpu.make_async_remote_copy(src_ref=theirs, dst_ref=theirs, send_sem=send0.at[4 * wi], recv_sem=recv0.at[4 * wi],
                                         device_id=sibling, device_id_type=MESH).wait_recv()
            for j, chip in enumerate(chips):
                sent = ins[wi].at[_block_of(*chip, c)]
                got = ins[wi].at[_block_of(*chip, 1 - c)]
                pltpu.make_async_remote_copy(src_ref=sent, dst_ref=sent, send_sem=send1.at[3 * wi + j], recv_sem=recv1.at[3 * wi + j],
                                             device_id=sibling, device_id_type=MESH).wait_send()
                pltpu.make_async_remote_copy(src_ref=got, dst_ref=got, send_sem=send1.at[3 * wi + j], recv_sem=recv1.at[3 * wi + j],
                                             device_id=sibling, device_id_type=MESH).wait_recv()

    outs = pl.pallas_call(
        body, name=name, in_specs=[HBM] * n + [SEM] * 4 + [ANY], out_specs=tuple([HBM] * n),
        out_shape=tuple(_hbm(a) for a in lands), input_output_aliases={i: i for i in range(n)},
        compiler_params=pltpu.CompilerParams(has_side_effects=EFFECT))(*lands, send0, recv0, send1, recv1, after)
    return list(outs)


def _plan_direct(me):
    return [(_block_of(*_flip(me, k)), k - 1, _flip(me, k)) for k in range(1, N_DEV)]


def _plan_sibling(me):
    x, y, c = me
    return [(_block_of(ci // 2, ci % 2, 1 - c), ci, (x, y, 1 - c)) for ci in range(4)]


def _plan_chips(me):
    x, y, c = me
    out = []
    for k in range(1, 4):
        px, py = (1 - x if k & 2 else x), (1 - y if k & 1 else y)
        out.append((2 * px + py, k - 1, (px, py, c)))
    return out


def _exchange_start(blocks, plan, name):
    n = len(blocks)
    slots = len(plan((0, 0, 0)))

    def body(*refs):
        srcs, lands = refs[:n], refs[n:2 * n]
        send, recv = refs[2 * n], refs[2 * n + 1]
        token = refs[-1]
        for wi in range(n):
            for block, slot, peer in plan(_position()):
                pltpu.make_async_remote_copy(
                    src_ref=srcs[wi].at[block], dst_ref=lands[wi].at[slot], send_sem=send.at[slots * wi + slot],
                    recv_sem=recv.at[slots * wi + slot], device_id=peer, device_id_type=MESH).start()
        token[...] = jnp.zeros_like(token)

    zones = [_landing((slots,) + b.shape[1:], b.dtype) for b in blocks]
    outs = pl.pallas_call(
        body, name=name, in_specs=[HBM] * (2 * n), out_specs=(SEM, SEM, *[HBM] * (2 * n), VMEM_SPEC),
        out_shape=(pltpu.SemaphoreType.DMA((slots * n,)), pltpu.SemaphoreType.DMA((slots * n,)), *[_hbm(a) for a in blocks],
                   *[_hbm(z) for z in zones], TOKEN),
        input_output_aliases={i: 2 + i for i in range(2 * n)},
        compiler_params=pltpu.CompilerParams(has_side_effects=EFFECT))(
            *[pltpu.with_memory_space_constraint(b, pltpu.HBM) for b in blocks], *zones)
    return outs[0], outs[1], list(outs[2:2 + n]), list(outs[2 + n:2 + 2 * n]), outs[-1]


def _exchange_wait(groups, plan, after, name):
    flat, counts = [], []
    for send, recv, blocks, zones in groups:
        flat += [*blocks, *zones, send, recv]
        counts.append(len(blocks))
    slots = len(plan((0, 0, 0)))

    def body(*refs):
        pos = 0
        for n in counts:
            srcs, lands = refs[pos:pos + n], refs[pos + n:pos + 2 * n]
            send, recv = refs[pos + 2 * n], refs[pos + 2 * n + 1]
            pos += 2 * n + 2
            for wi in range(n):
                for block, slot, peer in plan(_position()):
                    cp = pltpu.make_async_remote_copy(
                        src_ref=srcs[wi].at[block], dst_ref=lands[wi].at[slot], send_sem=send.at[slots * wi + slot],
                        recv_sem=recv.at[slots * wi + slot], device_id=peer, device_id_type=MESH)
                    cp.wait_send()
                    cp.wait_recv()

    in_specs, out_specs, out_shape, aliases = [], [], [], {}
    i = 0
    for n, (send, recv, blocks, zones) in zip(counts, groups):
        for a in (*blocks, *zones):
            aliases[i] = len(out_shape)
            in_specs.append(HBM)
            out_specs.append(HBM)
            out_shape.append(_hbm(a))
            i += 1
        in_specs += [SEM, SEM]
        i += 2
    outs = pl.pallas_call(
        body, name=name, in_specs=in_specs + [ANY], out_specs=tuple(out_specs), out_shape=tuple(out_shape),
        input_output_aliases=aliases, compiler_params=pltpu.CompilerParams(has_side_effects=EFFECT))(*flat, after)
    res, pos = [], 0
    for n in counts:
        res.append((list(outs[pos:pos + n]), list(outs[pos + n:pos + 2 * n])))
        pos += 2 * n
    return res


def _sibling_sum(blocks, zone, core, name):
    _, R, C = zone.shape
    rt = next(r for r in (R, R // 2, R // 4, 128, 64) if R % r == 0 and r % 16 == 0 and r * C <= 4 * 1024 * 1024)

    def body(core_ref, own_ref, z_ref, o_ref):
        o_ref[...] = (own_ref[...].astype(F32) + z_ref[...].astype(F32)).astype(o_ref.dtype)

    grid_spec = pltpu.PrefetchScalarGridSpec(
        num_scalar_prefetch=1, grid=(4, R // rt),
        in_specs=[pl.BlockSpec((None, rt, C), lambda ci, i, core_ref: (2 * ci + core_ref[0], i, 0)),
                  pl.BlockSpec((None, rt, C), lambda ci, i, core_ref: (ci, i, 0))],
        out_specs=pl.BlockSpec((None, rt, C), lambda ci, i, core_ref: (ci, i, 0)))
    return pl.pallas_call(body, grid_spec=grid_spec, out_shape=jax.ShapeDtypeStruct(zone.shape, zone.dtype),
                          compiler_params=_params("parallel", "parallel"), name=name)(core, blocks, zone)


def _adamw(w, g, m, v):
    m = ADAM_B1 * m + (1.0 - ADAM_B1) * g
    v = ADAM_B2 * v + (1.0 - ADAM_B2) * (g * g)
    m_hat = m / (1.0 - ADAM_B1 ** ADAM_STEP)
    v_hat = v / (1.0 - ADAM_B2 ** ADAM_STEP)
    delta = -ADAM_LR * (m_hat / (jnp.sqrt(v_hat) + ADAM_EPS) + ADAM_WD * w)
    return delta, m, v


def _reduce_adamw(blocks, zone, mine, w, m, v, name):
    R, C = w.shape
    rt = 128 if R % 128 == 0 else 64
    assert R % rt == 0

    def body(mine_ref, own_ref, z_ref, w_ref, m_ref, v_ref, g_out, d_out, m_out, v_out):
        g = own_ref[...].astype(F32)
        for s in range(zone.shape[0]):
            g = g + z_ref[s].astype(F32)
        delta, m2, v2 = _adamw(w_ref[...], g, m_ref[...], v_ref[...])
        g_out[...] = g
        d_out[...] = delta
        m_out[...] = m2
        v_out[...] = v2

    spec = pl.BlockSpec((rt, C), lambda i, mine_ref: (i, 0))
    grid_spec = pltpu.PrefetchScalarGridSpec(
        num_scalar_prefetch=1, grid=(R // rt,),
        in_specs=[pl.BlockSpec((None, rt, C), lambda i, mine_ref: (mine_ref[0], i, 0)),
                  pl.BlockSpec((zone.shape[0], rt, C), lambda i, mine_ref: (0, i, 0)), spec, spec, spec],
        out_specs=[spec] * 4)
    return pl.pallas_call(body, grid_spec=grid_spec, out_shape=[jax.ShapeDtypeStruct((R, C), F32)] * 4,
                          compiler_params=_params("parallel"), name=name)(mine, blocks, zone, w, m, v)


def _small_step(part, w, m, v, after):
    R, C = part.shape

    def body(part_ref, w_ref, m_ref, v_ref, *rest):
        g_out, d_out, m_out, v_out, gath, send, recv = rest[len(after):]
        me = _position()
        gath[_block_of(*me)] = part_ref[...]

        def copy(k, slot):
            return pltpu.make_async_remote_copy(
                src_ref=part_ref, dst_ref=gath.at[slot], send_sem=send.at[k - 1], recv_sem=recv.at[k - 1],
                device_id=_flip(me, k), device_id_type=MESH)

        sent = [copy(k, _block_of(*me)) for k in range(1, N_DEV)]
        for cp in sent:
            cp.start()
        for k in range(1, N_DEV):
            copy(k, _block_of(*_flip(me, k))).wait_recv()
        for cp in sent:
            cp.wait_send()
        g = gath[0]
        for s in range(1, N_DEV):
            g = g + gath[s]
        delta, m2, v2 = _adamw(w_ref[...], g, m_ref[...], v_ref[...])
        g_out[...] = g
        d_out[...] = delta
        m_out[...] = m2
        v_out[...] = v2

    vm = pl.BlockSpec(memory_space=pltpu.VMEM)
    return pl.pallas_call(
        body, in_specs=[vm] * 4 + [ANY] * len(after), out_specs=[vm] * 4, out_shape=[jax.ShapeDtypeStruct((R, C), F32)] * 4,
        scratch_shapes=[pltpu.VMEM((N_DEV, R, C), F32), pltpu.SemaphoreType.DMA((7,)), pltpu.SemaphoreType.DMA((7,))],
        name="small_step")(part, w, m, v, *after)


COLUMN_SHARDED = ("w_in", "w_sb_out", "w_ca_out", "w_ffn_in", "w_ple_in")
ROW_SHARDED = ("w_mix_out", "w_ffn_out", "w_ple_gate")
BIG = COLUMN_SHARDED + ROW_SHARDED
SMALL = ("g_mix", "g_ffn", "g_ple", "g_final", "rel_bias")
WEIGHTS = ("w_in", "w_sb_out", "w_ca_out", "w_mix_out", "rel_bias", "g_mix", "g_ffn", "g_ple", "g_final",
           "w_ffn_in", "w_ffn_out", "w_ple_in", "w_ple_gate")


def _pack_small(t, D):
    rows = [t[n].reshape(1, D) for n in SMALL[:4]]
    rb = t["rel_bias"].reshape(1, -1)
    rows.append(jnp.pad(rb, ((0, 0), (0, D - rb.shape[1]))))
    return jnp.concatenate(rows + [jnp.zeros((8 - len(rows), D), F32)], axis=0)


def _unpack_small(a, like):
    out = {n: a[i].reshape(like[n].shape) for i, n in enumerate(SMALL[:4])}
    out["rel_bias"] = a[4, :like["rel_bias"].size].reshape(like["rel_bias"].shape)
    return out


GATHER_GROUPS = (("w_in_a",), ("w_in_b",), ("w_sb_out", "w_ca_out", "w_mix_out"), ("w_ffn_in",), ("w_ffn_out", "w_ple_gate", "w_ple_in"))
FORWARD_AFTER = ("norm1", "mm_in_a", "mm_in", "attention", "mm_ffn_in")
GRAD_GROUPS = (("w_ple_in", "w_ple_gate"), ("w_ffn_out",), ("w_ffn_in",), ("w_mix_out", "w_sb_out", "w_ca_out"), ("w_in",))


class _Exchange:
    def __init__(self, shards):
        me = _position()
        self.mine = _block_of(*me)
        self.chip = jnp.reshape(2 * me[0] + me[1], (1,)).astype(jnp.int32)
        self.core = jnp.reshape(me[2], (1,)).astype(jnp.int32)
        self.device = jnp.reshape(self.mine, (1,)).astype(jnp.int32)
        self.shapes = {n: ((N_DEV * s.shape[0], s.shape[1]) if n in ROW_SHARDED else (N_DEV,) + s.shape) for n, s in shards.items()}
        self.tokens = []
        self.ready = {}
        self.gathers = []
        for gi, names in enumerate(GATHER_GROUPS):
            lands = [lax.dynamic_update_slice(lax.empty((N_DEV,) + shards[n].shape, BF16), shards[n][None], (self.mine, 0, 0))
                     for n in names]
            behind = self.tokens[-1] if self.tokens else shards[names[0]]
            send0, recv0, lands, token = _gather_start(lands, behind, f"gather_start_{gi}")
            self.tokens.append(token)
            self.gathers.append(dict(names=names, lands=lands, sems=(send0, recv0), token=token))
        self.grads = {}
        self.exchanges = []

    def pending(self):
        tokens, self.tokens = self.tokens, []
        return tokens

    def stage(self, tag, *made):
        gi = FORWARD_AFTER.index(tag)
        gth = self.gathers[gi]
        send1, recv1, lands, token = _gather_forward(gth["lands"], *gth["sems"], made + tuple(self.tokens), f"gather_forward_{gi}")
        gth.update(lands=lands, sems=gth["sems"] + (send1, recv1), token=token)
        self.tokens.append(token)

    def weight(self, name, after=None):
        if name not in self.ready:
            gi = next(i for i, names in enumerate(GATHER_GROUPS) if name in names)
            gth = self.gathers[gi]
            for n, a in zip(gth["names"], _gather_wait(gth["lands"], *gth["sems"], gth["token"] if after is None else after, f"gather_wait_{gi}")):
                self.ready[n] = a.reshape(self.shapes[n])
        return self.ready[name]

    def grad(self, name, blocks):
        self.grads[name] = blocks if name in COLUMN_SHARDED else blocks.reshape((N_DEV, -1, blocks.shape[-1]))
        names = next(names for names in GRAD_GROUPS if name in names)
        if not all(n in self.grads for n in names):
            return
        blocks = [self.grads[n] for n in names]
        if names == GRAD_GROUPS[-1]:
            send, recv, blocks, zones, token = _exchange_start(blocks, _plan_sibling, "pair_start_" + names[0])
            self.pair = (send, recv, blocks, zones)
        else:
            send, recv, blocks, zones, token = _exchange_start(blocks, _plan_direct, "exchange_start_" + names[0])
            self.exchanges.append(dict(names=names, state=(send, recv, blocks, zones), plan=_plan_direct, own=self.device))
        self.tokens.append(token)

    def pair_done(self, after):
        names = GRAD_GROUPS[-1]
        (blocks, zones), = _exchange_wait([self.pair], _plan_sibling, after, "pair_wait_" + names[0])
        blocks = [_sibling_sum(b, z, self.core, "pair_sum_" + n) for n, b, z in zip(names, blocks, zones)]
        send, recv, blocks, zones, token = _exchange_start(blocks, _plan_chips, "exchange_start_" + names[0])
        self.exchanges.append(dict(names=names, state=(send, recv, blocks, zones), plan=_plan_chips, own=self.chip))
        self.tokens.append(token)

    def collect(self, which, after, name):
        sel = [e for e in self.exchanges if GRAD_GROUPS.index(e["names"]) in which]
        out = {}
        for e, (blocks, zones) in zip(sel, _exchange_wait([e["state"] for e in sel], sel[0]["plan"], after, name)):
            out.update({n: (b, e["own"], z) for n, b, z in zip(e["names"], blocks, zones)})
        return out


def kernel(x, p, w_in, w_sb_out, w_ca_out, w_mix_out, rel_bias, g_mix, g_ffn, g_ple, g_final, w_ffn_in, w_ffn_out, w_ple_in, w_ple_gate, loss_target, m_w_in, m_w_sb_out, m_w_ca_out, m_w_mix_out, m_rel_bias, m_g_mix, m_g_ffn, m_g_ple, m_g_final, m_w_ffn_in, m_w_ffn_out, m_w_ple_in, m_w_ple_gate, v_w_in, v_w_sb_out, v_w_ca_out, v_w_mix_out, v_rel_bias, v_g_mix, v_g_ffn, v_g_ple, v_g_final, v_w_ffn_in, v_w_ffn_out, v_w_ple_in, v_w_ple_gate):
    wts = dict(w_in=w_in, w_sb_out=w_sb_out, w_ca_out=w_ca_out, w_mix_out=w_mix_out, rel_bias=rel_bias, g_mix=g_mix, g_ffn=g_ffn,
               g_ple=g_ple, g_final=g_final, w_ffn_in=w_ffn_in, w_ffn_out=w_ffn_out, w_ple_in=w_ple_in, w_ple_gate=w_ple_gate)
    mom = dict(w_in=m_w_in, w_sb_out=m_w_sb_out, w_ca_out=m_w_ca_out, w_mix_out=m_w_mix_out, rel_bias=m_rel_bias, g_mix=m_g_mix,
               g_ffn=m_g_ffn, g_ple=m_g_ple, g_final=m_g_final, w_ffn_in=m_w_ffn_in, w_ffn_out=m_w_ffn_out, w_ple_in=m_w_ple_in,
               w_ple_gate=m_w_ple_gate)
    var = dict(w_in=v_w_in, w_sb_out=v_w_sb_out, w_ca_out=v_w_ca_out, w_mix_out=v_w_mix_out, rel_bias=v_rel_bias, g_mix=v_g_mix,
               g_ffn=v_g_ffn, g_ple=v_g_ple, g_final=v_g_final, w_ffn_in=v_w_ffn_in, w_ffn_out=v_w_ffn_out, w_ple_in=v_w_ple_in,
               w_ple_gate=v_w_ple_gate)
    T, D = x.shape[1], x.shape[2]
    shard = {n: wts[n].reshape(wts[n].shape[-2:]) for n in BIG}
    bf = {n: _cast_bf16(shard[n], "cast_" + n) for n in BIG}
    half = bf["w_in"].shape[0] // 2
    bf["w_in_a"], bf["w_in_b"] = bf["w_in"][:half], bf.pop("w_in")[half:]
    comm = _Exchange(bf)
    g = dict(g_mix=g_mix.reshape(1, D), g_ffn=g_ffn.reshape(1, D), g_ple=g_ple.reshape(1, D), g_final=g_final.reshape(1, D),
             rel_bias=rel_bias.reshape(rel_bias.shape[-2:]))

    loss, grad_x, dsmall = _local_step(x.reshape(T, D), p.reshape(T, -1), loss_target.reshape(T, D), comm, g)
    loss = lax.psum(loss[0, 0], ("x", "y", "c"))

    grad, delta, new_m, new_v = {}, {}, {}, {}

    def update(parts):
        done = []
        for n, (blocks, own, zone) in parts.items():
            outs = _reduce_adamw(blocks, zone, own, shard[n], mom[n].reshape(shard[n].shape), var[n].reshape(shard[n].shape), "adamw_" + n)
            grad[n], delta[n], new_m[n], new_v[n] = [o.reshape(wts[n].shape) for o in outs]
            done.append(outs[0])
        return done

    done = update(comm.collect(range(len(GRAD_GROUPS) - 1), grad_x, "exchange_wait_rest"))
    outs = _small_step(_pack_small(dsmall, D), _pack_small(wts, D), _pack_small(mom, D), _pack_small(var, D), done)
    for dst, a in zip((grad, delta, new_m, new_v), outs):
        dst.update(_unpack_small(a, wts))
    update(comm.collect([len(GRAD_GROUPS) - 1], outs[0], "exchange_wait_w_in"))

    return (loss, grad_x.reshape(x.shape), *[grad[n] for n in WEIGHTS], *[delta[n] for n in WEIGHTS],
            *[new_m[n] for n in WEIGHTS], *[new_v[n] for n in WEIGHTS])
```

```python
import functools

import jax
import jax.numpy as jnp
from jax import lax
from jax.experimental import pallas as pl
from jax.experimental.pallas import tpu as pltpu

F32, BF16 = jnp.float32, jnp.bfloat16

N_DEV = 8
HEAD_DIM = 128
CHUNK = 64
LEFT_CHUNKS = 8
REL_CLIP = 128
N_REL = REL_CLIP + CHUNK
PAIR = 2 * CHUNK
PBAND = (LEFT_CHUNKS + 2) * CHUNK
CA_PAIRS = 2
CA_ROWS = CA_PAIRS * PAIR
CA_BAND = PBAND + CA_ROWS - PAIR
PAD = LEFT_CHUNKS * CHUNK
SB_BLOCK = 256
ROWS = 256
EPS = 1e-6
NEG = -1e30
SCALE = HEAD_DIM ** -0.5
VMEM_LIMIT_BYTES = 56 * 1024 * 1024

ADAM_LR, ADAM_B1, ADAM_B2, ADAM_EPS, ADAM_WD, ADAM_STEP = 0.001, 0.9, 0.999, 1e-08, 0.01, 10

ANY = pl.BlockSpec(memory_space=pl.ANY)
NN = (((1,), (0,)), ((), ()))
NT = (((1,), (1,)), ((), ()))
TN = (((0,), (0,)), ((), ()))
MESH = pl.DeviceIdType.MESH


def _params(*sem):
    return pltpu.CompilerParams(dimension_semantics=sem or None, vmem_limit_bytes=VMEM_LIMIT_BYTES)


def _dot(a, b, dims=NN):
    return lax.dot_general(a, b, dims, preferred_element_type=F32)


def _mm(a, b, *, mode, tm, tn, tk, out_dtype, name, b_blocked=False, out_block=None, res=None, after=(), a_cols=(1, 0), out_cols=(1, 1, 0), into=None, b_first=0, b_count=None, o_first=0, o_count=None):
    bg = og = 1
    if mode == "nn":
        M, K = a.shape
        a_spec = pl.BlockSpec((tm, tk), lambda i, j, k: (i, k))
        if b_blocked:
            G, _, nb = b.shape
            N = G * nb
            if tn > nb:
                bg = tn // nb
                assert tn % nb == 0
                b_spec = pl.BlockSpec((bg, tk, nb), lambda i, j, k: (j, k, 0))
            else:
                per = nb // tn
                assert nb % tn == 0
                b_spec = pl.BlockSpec((None, tk, tn), lambda i, j, k: (j // per, k, j % per))
        else:
            N = b.shape[1]
            b_spec = pl.BlockSpec((tk, tn), lambda i, j, k: (k, j))
        dims = NN
    elif mode == "nt":
        M, K = a.shape
        a_spec = pl.BlockSpec((tm, tk), lambda i, j, k: (i, k))
        if b_blocked:
            G, N, nb = b.shape
            K = (b_count or G) * nb
            assert b_first == 0 or tk == nb
            if tk > nb:
                bg = tk // nb
                assert tk % nb == 0
                b_spec = pl.BlockSpec((bg, tn, nb), lambda i, j, k: (k, j, 0))
            else:
                per = nb // tk
                assert nb % tk == 0
                b_spec = pl.BlockSpec((None, tn, tk), lambda i, j, k: (b_first + k // per, j, k % per))
        else:
            N = b.shape[0]
            b_spec = pl.BlockSpec((tn, tk), lambda i, j, k: (j, k))
        dims = NT
    else:
        K, M = a.shape
        N = b.shape[1]
        a_spec = pl.BlockSpec((tk, tm), lambda i, j, k: (k, i))
        b_spec = pl.BlockSpec((tk, tn), lambda i, j, k: (k, j))
        dims = TN
    if mode != "tn":
        if mode == "nn":
            K = b.shape[-2]
        elif not b_blocked:
            K = b.shape[1]
        a_spec = pl.BlockSpec((tm, tk), lambda i, j, k: (i, k * a_cols[0] + a_cols[1]))
    assert M % tm == 0 and N % tn == 0 and K % tk == 0, (name, M, N, K, tm, tn, tk)
    nk = K // tk
    if out_block is None:
        out_shape = jax.ShapeDtypeStruct((M, N * out_cols[0]), out_dtype)
        o_spec = pl.BlockSpec((tm, tn), lambda i, j, k: (i, j * out_cols[1] + out_cols[2]))
    else:
        out_shape = jax.ShapeDtypeStruct((o_count or N // out_block, M, out_block), out_dtype)
        if tn > out_block:
            og = tn // out_block
            assert tn % out_block == 0 and o_first % og == 0
            o_spec = pl.BlockSpec((og, tm, out_block), lambda i, j, k: (o_first // og + j, i, 0))
        else:
            per_o = out_block // tn
            assert out_block % tn == 0
            o_spec = pl.BlockSpec((None, tm, tn), lambda i, j, k: (o_first + j // per_o, i, j % per_o))
    in_specs = [a_spec, b_spec]
    args = [a, b]
    if res is not None:
        in_specs.append(pl.BlockSpec((tm, tn), lambda i, j, k: (i, j * out_cols[1] + out_cols[2])))
        args.append(res)
    n_in = len(args) + len(after) + (into is not None)

    def product(a_ref, b_ref):
        if bg == 1:
            return _dot(a_ref[...], b_ref[...], dims)
        nb = b_ref.shape[2]
        if mode == "nn":
            return jnp.concatenate([_dot(a_ref[...], b_ref[g], dims) for g in range(bg)], axis=1)
        return sum(_dot(a_ref[:, g * nb:(g + 1) * nb], b_ref[g], dims) for g in range(bg))

    def body(*refs):
        a_ref, b_ref = refs[0], refs[1]
        r_ref = refs[2] if res is not None else None
        o_ref = refs[n_in]

        def finish(acc):
            if r_ref is not None:
                acc = acc + r_ref[...]
            if og == 1:
                o_ref[...] = acc.astype(o_ref.dtype)
            else:
                for g in range(og):
                    o_ref[g] = acc[:, g * out_block:(g + 1) * out_block].astype(o_ref.dtype)

        if nk == 1:
            finish(product(a_ref, b_ref))
        else:
            acc_ref = refs[-1]
            k = pl.program_id(2)

            @pl.when(k == 0)
            def _():
                acc_ref[...] = jnp.zeros_like(acc_ref)

            acc_ref[...] += product(a_ref, b_ref)

            @pl.when(k == nk - 1)
            def _():
                finish(acc_ref[...])

    return pl.pallas_call(
        body, grid=(M // tm, N // tn, nk), in_specs=in_specs + [ANY] * (n_in - len(args)), out_specs=o_spec, out_shape=out_shape,
        scratch_shapes=[] if nk == 1 else [pltpu.VMEM((tm, tn), F32)], input_output_aliases={} if into is None else {n_in - 1: 0},
        compiler_params=_params("parallel", "parallel", "arbitrary"), name=name)(*args, *after, *(() if into is None else (into,)))


def _mm_fused(a, b, tiles, fn, outs, *, mode, tm, tn, tk, name, sums=(), after=(), b_outer=False):
    M, K = a.shape
    N = b.shape[1] if mode == "nn" else b.shape[0]
    nk = K // tk
    assert M % tm == 0 and N % tn == 0 and K % tk == 0 and (not sums or tn == N)
    def at(f):
        return (lambda j, i, k: f(i, j, k)) if b_outer else f

    b_spec = pl.BlockSpec((tk, tn), at(lambda i, j, k: (k, j))) if mode == "nn" else pl.BlockSpec((tn, tk), at(lambda i, j, k: (j, k)))
    in_specs = [pl.BlockSpec((tm, tk), at(lambda i, j, k: (i, k))), b_spec]
    args = [a, b]
    for t in tiles:
        if isinstance(t, tuple):
            arr, off = t
            in_specs.append(pl.BlockSpec((tm, tn), at(lambda i, j, k, off=off: (i, off + j))))
        else:
            arr = t
            in_specs.append(pl.BlockSpec((1, tn), at(lambda i, j, k: (0, j))))
        args.append(arr)
    n_in = len(args) + len(after)
    n_out = len(outs) + len(sums)

    def body(*refs):
        a_ref, b_ref = refs[0], refs[1]
        t_refs = refs[2:2 + len(tiles)]
        o_refs = refs[n_in:n_in + n_out]

        def finish(acc):
            res = fn(acc, *[t[...] for t in t_refs])
            for o_ref, r in zip(o_refs[:len(outs)], res):
                o_ref[...] = r.astype(o_ref.dtype)
            if sums:
                @pl.when(pl.program_id(0) == 0)
                def _():
                    for o_ref in o_refs[len(outs):]:
                        o_ref[...] = jnp.zeros_like(o_ref)

                for o_ref, r in zip(o_refs[len(outs):], res[len(outs):]):
                    o_ref[...] += jnp.broadcast_to(r, o_ref.shape)

        if nk == 1:
            finish(_dot(a_ref[...], b_ref[...], NN if mode == "nn" else NT))
        else:
            acc_ref = refs[-1]
            k = pl.program_id(2)

            @pl.when(k == 0)
            def _():
                acc_ref[...] = jnp.zeros_like(acc_ref)

            acc_ref[...] += _dot(a_ref[...], b_ref[...], NN if mode == "nn" else NT)

            @pl.when(k == nk - 1)
            def _():
                finish(acc_ref[...])

    assert not (b_outer and sums)
    o_spec = pl.BlockSpec((tm, tn), at(lambda i, j, k: (i, j)))
    return pl.pallas_call(
        body, grid=(N // tn, M // tm, nk) if b_outer else (M // tm, N // tn, nk), in_specs=in_specs + [ANY] * len(after),
        out_specs=[o_spec] * len(outs) + [pl.BlockSpec(sh, lambda i, j, k: (0, 0)) for sh in sums],
        out_shape=[jax.ShapeDtypeStruct((M, N), dt) for dt in outs] + [jax.ShapeDtypeStruct(sh, F32) for sh in sums],
        scratch_shapes=[] if nk == 1 else [pltpu.VMEM((tm, tn), F32)],
        compiler_params=_params("arbitrary" if sums else "parallel", "parallel", "arbitrary"), name=name)(*args, *after)


def _row_spec(d, col=0):
    return pl.BlockSpec((ROWS, d), lambda i: (i, col))


def _vec_spec(d):
    return pl.BlockSpec((1, d), lambda i: (0, 0))


def _rms(x):
    return lax.rsqrt(jnp.mean(x * x, axis=-1, keepdims=True) + EPS)


def _norm_fwd(x, g, name):
    T, D = x.shape

    def body(x_ref, g_ref, h_ref):
        xv = x_ref[...]
        h_ref[...] = (xv * _rms(xv) * g_ref[...]).astype(BF16)

    return pl.pallas_call(body, grid=(T // ROWS,), in_specs=[_row_spec(D), _vec_spec(D)], out_specs=_row_spec(D),
                          out_shape=jax.ShapeDtypeStruct((T, D), BF16), compiler_params=_params("parallel"), name=name)(x, g)


def _residual_norm(y, x, g):
    x = x + y
    return x, x * _rms(x) * g


def _norm_bwd_math(dh, xv, gv):
    r = _rms(xv)
    xhat = xv * r
    dxhat = dh * gv
    dx = r * (dxhat - xhat * jnp.mean(dxhat * xhat, axis=-1, keepdims=True))
    dg = jnp.sum(dh * xhat, axis=0, keepdims=True)
    return dx, dg


def _norm_bwd(dh, x, g, dres, name):
    T, D = x.shape

    def body(dh_ref, x_ref, g_ref, dres_ref, dx_ref, dxb_ref, dg_ref):
        dx, dg = _norm_bwd_math(dh_ref[...], x_ref[...], g_ref[...])
        dx = dx + dres_ref[...]
        dx_ref[...] = dx
        dxb_ref[...] = dx.astype(BF16)

        @pl.when(pl.program_id(0) == 0)
        def _():
            dg_ref[...] = jnp.zeros_like(dg_ref)

        dg_ref[...] += dg

    return pl.pallas_call(
        body, grid=(T // ROWS,), in_specs=[_row_spec(D), _row_spec(D), _vec_spec(D), _row_spec(D)],
        out_specs=[_row_spec(D), _row_spec(D), _vec_spec(D)],
        out_shape=[jax.ShapeDtypeStruct((T, D), F32), jax.ShapeDtypeStruct((T, D), BF16), jax.ShapeDtypeStruct((1, D), F32)],
        compiler_params=_params("arbitrary"), name=name)(dh, x, g, dres)


def _mm_merge(y_sb, y_ca, w_sb, w_ca, proj, gate_col, tm, tn, after):
    T, W = y_sb.shape
    G, _, nb = w_sb.shape
    D, bg = G * nb, tn // nb
    assert tn % nb == 0 and D % tn == 0
    per = D // tn

    def body(ys_ref, yc_ref, ws_ref, wc_ref, gs_ref, gc_ref, *rest):
        as_ref, ac_ref, m_ref = rest[len(after):]
        a = jnp.concatenate([_dot(ys_ref[...], ws_ref[g]) for g in range(bg)], axis=1)
        b = jnp.concatenate([_dot(yc_ref[...], wc_ref[g]) for g in range(bg)], axis=1)
        as_ref[...] = a
        ac_ref[...] = b
        m_ref[...] = (jax.nn.sigmoid(gs_ref[...]) * a + jax.nn.sigmoid(gc_ref[...]) * b).astype(BF16)

    y_spec = pl.BlockSpec((tm, W), lambda i, j: (i, 0))
    w_spec = pl.BlockSpec((bg, W, nb), lambda i, j: (j, 0, 0))
    out = pl.BlockSpec((tm, tn), lambda i, j: (i, j))
    f32 = jax.ShapeDtypeStruct((T, D), F32)
    return pl.pallas_call(
        body, grid=(T // tm, per),
        in_specs=[y_spec, y_spec, w_spec, w_spec, pl.BlockSpec((tm, tn), lambda i, j: (i, gate_col * per + j)),
                  pl.BlockSpec((tm, tn), lambda i, j: (i, (gate_col + 1) * per + j))] + [ANY] * len(after),
        out_specs=[out, out, out], out_shape=[f32, f32, jax.ShapeDtypeStruct((T, D), BF16)],
        compiler_params=_params("parallel", "parallel"), name="mm_merge")(y_sb, y_ca, w_sb, w_ca, proj, proj, *after)


def _merge_bwd(dm, gs, gc, a, b):
    ss, sc = jax.nn.sigmoid(gs), jax.nn.sigmoid(gc)
    return dm * ss, dm * sc, dm * a * ss * (1.0 - ss), dm * b * sc * (1.0 - sc)


def _mm_swiglu(h, w, tm):
    T, D = h.shape
    G2, _, nb = w.shape
    G = G2 // 2

    def body(h_ref, wg_ref, wu_ref, g_ref, u_ref, act_ref):
        hv = h_ref[...]
        gv = _dot(hv, wg_ref[...])
        uv = _dot(hv, wu_ref[...])
        g_ref[...] = gv
        u_ref[...] = uv
        act_ref[...] = (gv * jax.nn.sigmoid(gv) * uv).astype(BF16)

    out = pl.BlockSpec((tm, nb), lambda j, i: (i, j))
    f32 = jax.ShapeDtypeStruct((T, G * nb), F32)
    return pl.pallas_call(
        body, grid=(G, T // tm),
        in_specs=[pl.BlockSpec((tm, D), lambda j, i: (i, 0)), pl.BlockSpec((None, D, nb), lambda j, i: (j, 0, 0)),
                  pl.BlockSpec((None, D, nb), lambda j, i: (j + G, 0, 0))],
        out_specs=[out, out, out], out_shape=[f32, f32, jax.ShapeDtypeStruct((T, G * nb), BF16)],
        compiler_params=_params("parallel", "parallel"), name="mm_ffn_in")(h, w, w)


def _swiglu_bwd(dact, gate, up):
    s = jax.nn.sigmoid(gate)
    return dact * up * s * (1.0 + gate * (1.0 - s)), dact * gate * s


def _tail(zg, x3, pe, target, g_final):
    D = x3.shape[-1]
    gate = jax.nn.sigmoid(zg)
    x4 = x3 + gate * pe
    err = x4 * _rms(x4) * g_final - target
    part = 0.5 * jnp.sum(jnp.mean(err * err, axis=-1, keepdims=True), axis=0, keepdims=True)
    dx, dg = _norm_bwd_math(err * (1.0 / D), x4, g_final)
    return dx, dx * gate, dx * pe * gate * (1.0 - gate), part, dg


def _cast_bf16(x, name):
    R, C = x.shape
    rows = next(r for r in (ROWS, 128, 64, 32, 16) if R % r == 0)

    def body(x_ref, o_ref):
        o_ref[...] = x_ref[...].astype(BF16)

    spec = pl.BlockSpec((rows, C), lambda i: (i, 0))
    return pl.pallas_call(body, grid=(R // rows,), in_specs=[spec], out_specs=spec, out_shape=jax.ShapeDtypeStruct((R, C), BF16),
                          compiler_params=_params("parallel"), name=name)(x)


def _head_spec(T, col0, heads=1):
    return pl.BlockSpec((T, heads * HEAD_DIM), lambda h, *_: (0, col0 + h))


SB_HEADS = 4


def _triangle(n, right):
    j = lax.broadcasted_iota(jnp.int32, (n, n), 0)
    s = lax.broadcasted_iota(jnp.int32, (n, n), 1)
    return jnp.where((j > s) if right else (j < s), 1.0, 0.0).astype(BF16)


def _lane_scan(x, tri):
    hi = x.astype(BF16)
    lo = (x - hi.astype(F32)).astype(BF16)
    return _dot(hi, tri) + _dot(lo, tri)


def _head_cols(ref, rows, hh):
    return ref[rows, hh * HEAD_DIM:(hh + 1) * HEAD_DIM]


def _sb_tile(qv, kk, past, c_lk, tri):
    z = _dot(qv, kk, NT) * SCALE
    sp = jnp.log(1.0 + jnp.exp(-jnp.abs(z)))
    ls_pos = jnp.minimum(z, 0.0) - sp
    lk = jnp.minimum(-z, 0.0) - sp
    if past is not None:
        lk = jnp.where(past, lk, 0.0)
    right = c_lk + _lane_scan(lk, tri)
    a = jnp.exp(ls_pos + right)
    if past is not None:
        a = jnp.where(past, a, 0.0)
    return ls_pos, a, right[:, 0:1] + lk[:, 0:1]


SB_Q = 512
SB_HEADS_BWD = 2


def _sb_mask(d):
    B, r = SB_BLOCK, SB_Q // SB_BLOCK
    return lax.broadcasted_iota(jnp.int32, (SB_Q, B), 1) + (r - 1 - d) * B < lax.broadcasted_iota(jnp.int32, (SB_Q, B), 0)


def _sb_rows(kb):
    return pl.ds(pl.multiple_of(kb * SB_BLOCK, SB_BLOCK), SB_BLOCK)


def _sb_fwd(proj, n_heads, after=()):
    T = proj.shape[0]
    B, Q, HP = SB_BLOCK, SB_Q, SB_HEADS
    r = Q // B
    assert n_heads % HP == 0 and T % Q == 0

    def body(q_ref, k_ref, v_ref, *rest):
        y_ref = rest[-1]
        qb = pl.program_id(1)
        tri = _triangle(B, right=True)
        qv = [_head_cols(q_ref, slice(None), hh).astype(BF16) for hh in range(HP)]

        def tile(kb, carry, past):
            out = []
            for hh in range(HP):
                acc, c_lk = carry[hh]
                kk = _head_cols(k_ref, _sb_rows(kb), hh).astype(BF16)
                vv = _head_cols(v_ref, _sb_rows(kb), hh).astype(BF16)
                _, a, c_lk = _sb_tile(qv[hh], kk, past, c_lk, tri)
                out.append((acc + _dot(a.astype(BF16), vv), c_lk))
            return tuple(out)

        carry = tuple((jnp.zeros((Q, HEAD_DIM), F32), jnp.zeros((Q, 1), F32)) for _ in range(HP))
        for d in range(r):
            carry = tile(r * qb + r - 1 - d, carry, _sb_mask(d))
        res = lax.fori_loop(0, r * qb, lambda i, c: tile(r * qb - 1 - i, c, None), carry)
        for hh in range(HP):
            y_ref[:, hh * HEAD_DIM:(hh + 1) * HEAD_DIM] = res[hh][0].astype(BF16)

    blk = pl.BlockSpec((Q, HP * HEAD_DIM), lambda h, i: (i, h))
    G = n_heads // HP
    return pl.pallas_call(
        body, grid=(G, T // Q),
        in_specs=[blk, _head_spec(T, G, HP), _head_spec(T, 2 * G, HP)] + [ANY] * len(after), out_specs=blk,
        out_shape=jax.ShapeDtypeStruct((T, n_heads * HEAD_DIM), BF16),
        compiler_params=_params("parallel", "arbitrary"), name="sb_fwd")(proj, proj, proj, *after)


def _sb_bwd(proj, dy, n_heads):
    T = proj.shape[0]
    B, Q, HP = SB_BLOCK, SB_Q, SB_HEADS_BWD
    r, nq = Q // B, T // Q

    def body(q_ref, k_ref, v_ref, dy_ref, dq_ref, dk_ref, dv_ref, g_s, sig_s, dk_s, dv_s):
        qb = pl.program_id(1)

        @pl.when(qb == 0)
        def _():
            dk_s[...] = jnp.zeros_like(dk_s)
            dv_s[...] = jnp.zeros_like(dv_s)

        tri_r = _triangle(B, right=True)
        tri_l = _triangle(B, right=False)
        qv = [_head_cols(q_ref, slice(None), hh).astype(BF16) for hh in range(HP)]
        dyb = [_head_cols(dy_ref, slice(None), hh).astype(BF16) for hh in range(HP)]

        def sweep(kb, carry, past):
            out = []
            for hh in range(HP):
                kk = _head_cols(k_ref, _sb_rows(kb), hh).astype(BF16)
                vv = _head_cols(v_ref, _sb_rows(kb), hh).astype(BF16)
                ls_pos, a, c_lk = _sb_tile(qv[hh], kk, past, carry[hh], tri_r)
                g_s[hh, kb] = _dot(dyb[hh], vv, NT) * a
                sig_s[hh, kb] = jnp.exp(ls_pos)
                dv_s[_sb_rows(kb), hh * HEAD_DIM:(hh + 1) * HEAD_DIM] += _dot(a.astype(BF16), dyb[hh], TN)
                out.append(c_lk)
            return tuple(out)

        carry = tuple(jnp.zeros((Q, 1), F32) for _ in range(HP))
        for d in range(r):
            carry = sweep(r * qb + r - 1 - d, carry, _sb_mask(d))
        lax.fori_loop(0, r * qb, lambda i, c: sweep(r * qb - 1 - i, c, None), carry)

        def back(kb, carry, past):
            out = []
            for hh in range(HP):
                dq, c_g = carry[hh]
                kk = _head_cols(k_ref, _sb_rows(kb), hh).astype(BF16)
                g, sig = g_s[hh, kb], sig_s[hh, kb]
                left = c_g + _lane_scan(g, tri_l)
                dz = g * (1.0 - sig) - left * sig
                if past is not None:
                    dz = jnp.where(past, dz, 0.0)
                dz = (dz * SCALE).astype(BF16)
                dk_s[_sb_rows(kb), hh * HEAD_DIM:(hh + 1) * HEAD_DIM] += _dot(dz, qv[hh], TN)
                out.append((dq + _dot(dz, kk), left[:, B - 1:B] + g[:, B - 1:B]))
            return tuple(out)

        init = tuple((jnp.zeros((Q, HEAD_DIM), F32), jnp.zeros((Q, 1), F32)) for _ in range(HP))
        res = lax.fori_loop(0, r * qb, lambda kb, c: back(kb, c, None), init)
        for d in reversed(range(r)):
            res = back(r * qb + r - 1 - d, res, _sb_mask(d))
        for hh in range(HP):
            dq_ref[:, hh * HEAD_DIM:(hh + 1) * HEAD_DIM] = res[hh][0].astype(BF16)

        @pl.when(qb == nq - 1)
        def _():
            dk_ref[...] = dk_s[...].astype(BF16)
            dv_ref[...] = dv_s[...].astype(BF16)

    blk = pl.BlockSpec((Q, HP * HEAD_DIM), lambda h, i: (i, h))
    G = n_heads // HP
    full = _head_spec(T, 0, HP)
    shp = jax.ShapeDtypeStruct((T, n_heads * HEAD_DIM), BF16)
    return pl.pallas_call(
        body, grid=(G, nq),
        in_specs=[blk, _head_spec(T, G, HP), _head_spec(T, 2 * G, HP), blk], out_specs=[blk, full, full],
        out_shape=[shp, shp, shp],
        scratch_shapes=[pltpu.VMEM((HP, T // B, Q, B), F32)] * 2 + [pltpu.VMEM((T, HP * HEAD_DIM), F32)] * 2,
        compiler_params=_params("parallel", "arbitrary"), name="sb_bwd")(proj, proj, proj, dy)


DIAGS = PBAND + PAIR


def _diag_onehot():
    d = lax.broadcasted_iota(jnp.int32, (DIAGS, 2 * PAIR), 0)
    r = lax.broadcasted_iota(jnp.int32, (DIAGS, 2 * PAIR), 1)
    return jnp.where(jnp.clip(d - PAIR - PAD, -REL_CLIP, CHUNK - 1) + REL_CLIP == r, 1.0, 0.0)


def _bias_expand(rel_bias):
    H = rel_bias.shape[0]
    table = jnp.pad(rel_bias, ((0, 0), (0, 2 * PAIR - N_REL)))

    def body(rb_ref, o_ref):
        o_ref[...] = lax.dot_general(rb_ref[...], _diag_onehot(), NT, precision=lax.Precision.HIGHEST, preferred_element_type=F32)

    per_diag = pl.pallas_call(body, out_shape=jax.ShapeDtypeStruct((H, DIAGS), F32), name="bias_expand")(table)
    flat = jnp.tile(jnp.pad(per_diag, ((0, 0), (0, 1))), (1, PAIR))[:, :PAIR * DIAGS]
    return flat.reshape(H, PAIR, DIAGS)[:, :, PAIR:]


def _bias_reduce(dbias):
    H = dbias.shape[0]
    padded = jnp.pad(dbias, ((0, 0), (0, 1), (PAIR, 0))).reshape(H, -1)
    skewed = padded[:, :PAIR * (DIAGS + 1)].reshape(H, PAIR, DIAGS + 1)[:, :, :DIAGS]

    def body(s_ref, o_ref):
        per_diag = jnp.sum(s_ref[...], axis=0, keepdims=True)
        o_ref[...] = lax.dot_general(jnp.broadcast_to(per_diag, (8, DIAGS)), _diag_onehot(), NN, precision=lax.Precision.HIGHEST,
                                     preferred_element_type=F32)[0:1]

    return pl.pallas_call(
        body, grid=(H,), in_specs=[pl.BlockSpec((None, PAIR, DIAGS), lambda h: (h, 0, 0))],
        out_specs=pl.BlockSpec((None, 1, 2 * PAIR), lambda h: (h, 0, 0)),
        out_shape=jax.ShapeDtypeStruct((H, 1, 2 * PAIR), F32), compiler_params=_params("parallel"), name="bias_reduce")(skewed)[:, 0]


CA_HEADS = 2


def _ca_mask():
    i = lax.broadcasted_iota(jnp.int32, (CA_ROWS, CA_BAND), 0)
    j = lax.broadcasted_iota(jnp.int32, (CA_ROWS, CA_BAND), 1)
    qc, kc = i // CHUNK, j // CHUNK
    return j, (kc >= qc) & (kc <= qc + LEFT_CHUNKS)


def _ca_bias(pair_bias):
    rows = []
    for q in range(CA_PAIRS):
        parts = [jnp.zeros((PAIR, q * PAIR), F32)] * (q > 0) + [pair_bias] + [jnp.zeros((PAIR, (CA_PAIRS - 1 - q) * PAIR), F32)] * (q < CA_PAIRS - 1)
        rows.append(jnp.concatenate(parts, axis=1) if len(parts) > 1 else parts[0])
    return jnp.concatenate(rows, axis=0)


def _ca_weights(pr, qp, kb, bias, j, window):
    valid = window & (pr * CA_ROWS + j >= PAD)
    z = jnp.where(valid, _dot(qp, kb, NT) * SCALE + bias, NEG)
    e = jnp.exp(z - jnp.max(z, axis=1, keepdims=True))
    return e / jnp.sum(e, axis=1, keepdims=True)


def _ca_fill(k_ref, v_ref, kpad, vpad):
    T, W = k_ref.shape
    kpad[0:PAD, :] = jnp.zeros((PAD, W), BF16)
    vpad[0:PAD, :] = jnp.zeros((PAD, W), BF16)
    kpad[PAD:PAD + T, :] = k_ref[...].astype(BF16)
    vpad[PAD:PAD + T, :] = v_ref[...].astype(BF16)


def _ca_fwd(proj, bias, n_heads, col0):
    T = proj.shape[0]
    HP = CA_HEADS
    G = n_heads // HP
    assert n_heads % HP == 0 and col0 % HP == 0

    def body(q_ref, k_ref, v_ref, b_ref, y_ref, kpad, vpad):
        _ca_fill(k_ref, v_ref, kpad, vpad)
        j, window = _ca_mask()
        bias = [_ca_bias(b_ref[hh]) for hh in range(HP)]

        def step(pr, _):
            r0 = pl.multiple_of(pr * CA_ROWS, CA_ROWS)
            for hh in range(HP):
                qp = _head_cols(q_ref, pl.ds(r0, CA_ROWS), hh).astype(BF16)
                kb = _head_cols(kpad, pl.ds(r0, CA_BAND), hh)
                vb = _head_cols(vpad, pl.ds(r0, CA_BAND), hh)
                w = _ca_weights(pr, qp, kb, bias[hh], j, window)
                y_ref[pl.ds(r0, CA_ROWS), hh * HEAD_DIM:(hh + 1) * HEAD_DIM] = _dot(w.astype(BF16), vb).astype(BF16)
            return 0

        lax.fori_loop(0, T // CA_ROWS, step, 0)

    c = col0 // HP
    return pl.pallas_call(
        body, grid=(G,),
        in_specs=[_head_spec(T, c, HP), _head_spec(T, c + G, HP), _head_spec(T, c + 2 * G, HP),
                  pl.BlockSpec((HP, PAIR, PBAND), lambda h: (h, 0, 0))],
        out_specs=_head_spec(T, 0, HP), out_shape=jax.ShapeDtypeStruct((T, n_heads * HEAD_DIM), BF16),
        scratch_shapes=[pltpu.VMEM((PAD + T, HP * HEAD_DIM), BF16)] * 2,
        compiler_params=_params("parallel"), name="ca_fwd")(proj, proj, proj, bias)


def _ca_bwd(proj, bias, dy, n_heads, col0):
    T = proj.shape[0]
    HP = CA_HEADS
    G = n_heads // HP

    def body(q_ref, k_ref, v_ref, b_ref, dy_ref, dq_ref, dk_ref, dv_ref, db_ref, kpad, vpad, dkpad, dvpad):
        _ca_fill(k_ref, v_ref, kpad, vpad)
        dkpad[...] = jnp.zeros_like(dkpad)
        dvpad[...] = jnp.zeros_like(dvpad)
        db_ref[...] = jnp.zeros_like(db_ref)
        j, window = _ca_mask()
        bias = [_ca_bias(b_ref[hh]) for hh in range(HP)]

        def step(pr, _):
            r0 = pl.multiple_of(pr * CA_ROWS, CA_ROWS)
            for hh in range(HP):
                cols = slice(hh * HEAD_DIM, (hh + 1) * HEAD_DIM)
                qp = _head_cols(q_ref, pl.ds(r0, CA_ROWS), hh).astype(BF16)
                kb = _head_cols(kpad, pl.ds(r0, CA_BAND), hh)
                vb = _head_cols(vpad, pl.ds(r0, CA_BAND), hh)
                w = _ca_weights(pr, qp, kb, bias[hh], j, window)
                dyp = _head_cols(dy_ref, pl.ds(r0, CA_ROWS), hh).astype(BF16)
                dw = _dot(dyp, vb, NT)
                dz = w * (dw - jnp.sum(dw * w, axis=1, keepdims=True))
                db_ref[hh] += sum(dz[q * PAIR:(q + 1) * PAIR, q * PAIR:q * PAIR + PBAND] for q in range(CA_PAIRS))
                dzs = (dz * SCALE).astype(BF16)
                dq_ref[pl.ds(r0, CA_ROWS), cols] = _dot(dzs, kb).astype(BF16)
                dkpad[pl.ds(r0, CA_BAND), cols] += _dot(dzs, qp, TN)
                dvpad[pl.ds(r0, CA_BAND), cols] += _dot(w.astype(BF16), dyp, TN)
            return 0

        lax.fori_loop(0, T // CA_ROWS, step, 0)
        dk_ref[...] = dkpad[PAD:PAD + T, :].astype(BF16)
        dv_ref[...] = dvpad[PAD:PAD + T, :].astype(BF16)

    c = col0 // HP
    full = _head_spec(T, 0, HP)
    bspec = pl.BlockSpec((HP, PAIR, PBAND), lambda h: (h, 0, 0))
    shp = jax.ShapeDtypeStruct((T, n_heads * HEAD_DIM), BF16)
    return pl.pallas_call(
        body, grid=(G,),
        in_specs=[_head_spec(T, c, HP), _head_spec(T, c + G, HP), _head_spec(T, c + 2 * G, HP), bspec, full],
        out_specs=[full, full, full, bspec],
        out_shape=[shp, shp, shp, jax.ShapeDtypeStruct((n_heads, PAIR, PBAND), F32)],
        scratch_shapes=[pltpu.VMEM((PAD + T, HP * HEAD_DIM), BF16)] * 2 + [pltpu.VMEM((PAD + T, HP * HEAD_DIM), F32)] * 2,
        compiler_params=_params("parallel"), name="ca_bwd")(proj, proj, proj, bias, dy)


def _local_step(x, p, target, comm, g):
    T, D = x.shape
    H = g["rel_bias"].shape[0]
    W = H * HEAD_DIM
    nb_in = comm.shapes["w_in_a"][2]
    nb_ff = comm.shapes["w_ffn_in"][2]
    nb_o = comm.shapes["w_sb_out"][2]
    nb_p = comm.shapes["w_ple_in"][2]
    tm = min(T, 1024)
    tn = min(D, 1024)
    gate_col = 6 * W // D

    h1 = _norm_fwd(x, g["g_mix"], "norm1")
    comm.stage("norm1", h1)
    proj = _mm(h1, comm.weight("w_in_a", h1), mode="nn", tm=tm, tn=nb_in, tk=D // 2, out_dtype=F32, b_blocked=True,
               after=comm.pending(), name="mm_in_a")
    comm.stage("mm_in_a", proj)
    proj = _mm(h1, comm.weight("w_in_b", proj), mode="nn", tm=tm, tn=nb_in, tk=D // 2, out_dtype=F32, b_blocked=True, a_cols=(1, 1),
               res=proj, after=comm.pending(), name="mm_in")
    comm.stage("mm_in", proj)
    y_sb = _sb_fwd(proj, H, comm.pending())
    bias = _bias_expand(g["rel_bias"])
    y_ca = _ca_fwd(proj, bias, H, 3 * H)
    comm.stage("attention", y_sb, y_ca)
    a_sb, a_ca, merged = _mm_merge(y_sb, y_ca, comm.weight("w_sb_out", y_ca), comm.weight("w_ca_out"), proj, gate_col, min(T, 512), tn, comm.pending())
    x2, h2 = _mm_fused(merged, comm.weight("w_mix_out"), [(x, 0), g["g_ffn"]], _residual_norm, [F32, BF16],
                       mode="nn", tm=min(T, 512), tn=D, tk=D, name="mm_mix")
    gate, up, act = _mm_swiglu(h2, comm.weight("w_ffn_in", h2), min(T, 512))
    comm.stage("mm_ffn_in", act)
    F = act.shape[1]
    tkf = F // 2 if F % 256 == 0 else F
    x3 = _mm(act, comm.weight("w_ffn_out", act), mode="nn", tm=tm, tn=tn, tk=tkf, out_dtype=F32, res=x2, after=comm.pending(), name="mm_ffn_out")
    h3 = _norm_fwd(x3, g["g_ple"], "norm3")
    pb = _cast_bf16(p, "cast_p")
    P = p.shape[1]
    pe = _mm(pb, comm.weight("w_ple_in"), mode="nn", tm=tm, tn=nb_p, tk=P, out_dtype=F32, b_blocked=True, name="mm_ple_in")
    dx4, dpe, dzg, loss, dg_final = _mm_fused(
        h3, comm.weight("w_ple_gate"), [(x3, 0), (pe, 0), (target, 0), g["g_final"]], _tail, [F32, BF16, BF16],
        mode="nn", tm=min(T, 256), tn=D, tk=D, sums=[(1, 128), (1, D)], name="mm_ple_gate")

    tw = min(D, 1024)
    DW = BF16
    comm.grad("w_ple_in", _mm(pb, dpe, mode="tn", tm=P, tn=nb_p, tk=T, out_dtype=DW, out_block=nb_p, name="mm_d_ple_in"))
    comm.grad("w_ple_gate", _mm(h3, dzg, mode="tn", tm=tw, tn=tn, tk=T, out_dtype=DW, name="mm_d_ple_gate"))
    dh3 = _mm(dzg, comm.weight("w_ple_gate"), mode="nt", tm=tm, tn=tn, tk=D, out_dtype=F32, after=comm.pending(), name="mm_dh3")
    dx3, dx3b, dg_ple = _norm_bwd(dh3, x3, g["g_ple"], dx4, "norm3_bwd")
    comm.grad("w_ffn_out", _mm(act, dx3b, mode="tn", tm=F // 4, tn=tn, tk=T, out_dtype=DW, name="mm_d_ffn_out"))
    dgate, dup = _mm_fused(dx3b, comm.weight("w_ffn_out"), [(gate, 0), (up, 0)], _swiglu_bwd, [BF16, BF16],
                           mode="nt", tm=min(T, 512), tn=nb_ff, tk=D, after=comm.pending(), b_outer=True, name="mm_dact")
    half = comm.shapes["w_ffn_in"][0] // 2
    d_ffn_in = _mm(h2, dgate, mode="tn", tm=tw, tn=nb_ff, tk=T, out_dtype=DW, out_block=nb_ff, o_count=2 * half, name="mm_d_ffn_in_gate")
    comm.grad("w_ffn_in", _mm(h2, dup, mode="tn", tm=tw, tn=nb_ff, tk=T, out_dtype=DW, out_block=nb_ff, o_first=half, o_count=2 * half,
                              into=d_ffn_in, name="mm_d_ffn_in"))
    dh2 = _mm(dgate, comm.weight("w_ffn_in"), mode="nt", tm=tm, tn=D, tk=nb_ff, out_dtype=F32, b_blocked=True, b_count=half,
              after=comm.pending(), name="mm_dh2_gate")
    dh2 = _mm(dup, comm.weight("w_ffn_in"), mode="nt", tm=min(T, 512), tn=D, tk=nb_ff, out_dtype=F32, b_blocked=True, b_first=half, b_count=half,
              res=dh2, name="mm_dh2")
    dx2, dx2b, dg_ffn = _norm_bwd(dh2, x2, g["g_ffn"], dx3, "norm2_bwd")
    per = D // tn
    da_sb, da_ca, dgate_sb, dgate_ca = _mm_fused(
        dx2b, comm.weight("w_mix_out"), [(proj, gate_col * per), (proj, (gate_col + 1) * per), (a_sb, 0), (a_ca, 0)], _merge_bwd, [BF16] * 4,
        mode="nt", tm=min(T, 512), tn=tn, tk=D, name="mm_dmerged")
    comm.grad("w_mix_out", _mm(merged, dx2b, mode="tn", tm=tw, tn=tn, tk=T, out_dtype=DW, name="mm_d_mix"))
    comm.grad("w_sb_out", _mm(y_sb, da_sb, mode="tn", tm=min(W, 512), tn=tn, tk=T, out_dtype=DW, out_block=nb_o, name="mm_d_sb_out"))
    comm.grad("w_ca_out", _mm(y_ca, da_ca, mode="tn", tm=min(W, 512), tn=tn, tk=T, out_dtype=DW, out_block=nb_o, name="mm_d_ca_out"))
    dy_sb = _mm(da_sb, comm.weight("w_sb_out"), mode="nt", tm=tm, tn=W, tk=tn, out_dtype=F32, b_blocked=True, after=comm.pending(), name="mm_dy_sb")
    dy_ca = _mm(da_ca, comm.weight("w_ca_out"), mode="nt", tm=tm, tn=W, tk=tn, out_dtype=F32, b_blocked=True, name="mm_dy_ca")
    dq_sb, dk_sb, dv_sb = _sb_bwd(proj, dy_sb, H)
    dq_ca, dk_ca, dv_ca, dbias = _ca_bwd(proj, bias, dy_ca, H, 3 * H)
    d_rel = _bias_reduce(dbias)[:, :N_REL]
    dproj = jnp.concatenate([dq_sb, dk_sb, dv_sb, dq_ca, dk_ca, dv_ca, dgate_sb, dgate_ca], axis=1)
    comm.grad("w_in", _mm(h1, dproj, mode="tn", tm=tw, tn=nb_in, tk=T, out_dtype=DW, out_block=nb_in, name="mm_d_in"))
    dh1 = _mm(dproj, comm.weight("w_in_a"), mode="nt", tm=tm, tn=D // 2, tk=nb_in, out_dtype=F32, b_blocked=True, out_cols=(2, 1, 0),
              after=comm.pending(), name="mm_dh1_a")
    comm.pair_done(dh1)
    dh1 = _mm(dproj, comm.weight("w_in_b"), mode="nt", tm=tm, tn=D // 2, tk=nb_in, out_dtype=F32, b_blocked=True, out_cols=(2, 1, 1),
              into=dh1, after=comm.pending(), name="mm_dh1")
    grad_x, _, dg_mix = _norm_bwd(dh1, x, g["g_mix"], dx2, "norm1_bwd")
    small = dict(g_mix=dg_mix, g_ffn=dg_ffn, g_ple=dg_ple, g_final=dg_final, rel_bias=d_rel)
    return loss, grad_x, small


def _position():
    x, y, c = lax.axis_index("x"), lax.axis_index("y"), lax.axis_index("c")
    return x, y, c


def _block_of(px, py, pc):
    return 4 * px + 2 * py + pc


def _flip(pos, k):
    x, y, c = pos
    return (1 - x if k & 4 else x, 1 - y if k & 2 else y, 1 - c if k & 1 else c)


HBM = pl.BlockSpec(memory_space=pltpu.HBM)
SEM = pl.BlockSpec(memory_space=pltpu.SEMAPHORE)
VMEM_SPEC = pl.BlockSpec(memory_space=pltpu.VMEM)
EFFECT = pltpu.SideEffectType.DATAFLOW_SIDE_EFFECTING
TOKEN = jax.ShapeDtypeStruct((8, 128), F32)


def _hbm(a):
    return pltpu.HBM(a.shape, a.dtype)


def _landing(shape, dtype):
    return pltpu.with_memory_space_constraint(lax.empty(shape, dtype), pltpu.HBM)


def _gather_start(lands, after, name):
    n = len(lands)

    def body(*refs):
        ins = refs[:n]
        send, recv = refs[n + 1], refs[n + 2]
        token = refs[-1]
        x, y, c = _position()
        mine = _block_of(x, y, c)
        peers = [(x, y, 1 - c), (1 - x, y, c), (x, 1 - y, c), (1 - x, 1 - y, c)]
        for wi in range(n):
            for k, peer in enumerate(peers):
                pltpu.make_async_remote_copy(
                    src_ref=ins[wi].at[mine], dst_ref=ins[wi].at[mine], send_sem=send.at[4 * wi + k], recv_sem=recv.at[4 * wi + k],
                    device_id=peer, device_id_type=MESH).start()
        token[...] = jnp.zeros_like(token)

    outs = pl.pallas_call(
        body, name=name, in_specs=[HBM] * n + [ANY], out_specs=(SEM, SEM, *[HBM] * n, VMEM_SPEC),
        out_shape=(pltpu.SemaphoreType.DMA((4 * n,)), pltpu.SemaphoreType.DMA((4 * n,)), *[_hbm(a) for a in lands], TOKEN),
        input_output_aliases={i: 2 + i for i in range(n)},
        compiler_params=pltpu.CompilerParams(has_side_effects=EFFECT))(*[pltpu.with_memory_space_constraint(a, pltpu.HBM) for a in lands], after)
    return outs[0], outs[1], list(outs[2:2 + n]), outs[-1]


def _gather_forward(lands, send0, recv0, after, name):
    n = len(lands)

    def body(*refs):
        ins = refs[:n]
        send0, recv0 = refs[n], refs[n + 1]
        send1, recv1 = refs[n + 2 + len(after)], refs[n + 3 + len(after)]
        token = refs[-1]
        x, y, c = _position()
        chips = [(1 - x, y), (x, 1 - y), (1 - x, 1 - y)]
        for wi in range(n):
            for j, chip in enumerate(chips):
                rows = ins[wi].at[_block_of(*chip, c)]
                pltpu.make_async_remote_copy(
                    src_ref=rows, dst_ref=rows, send_sem=send0.at[4 * wi + 1 + j], recv_sem=recv0.at[4 * wi + 1 + j],
                    device_id=(*chip, c), device_id_type=MESH).wait_recv()
                pltpu.make_async_remote_copy(
                    src_ref=rows, dst_ref=rows, send_sem=send1.at[3 * wi + j], recv_sem=recv1.at[3 * wi + j],
                    device_id=(x, y, 1 - c), device_id_type=MESH).start()
        token[...] = jnp.zeros_like(token)

    outs = pl.pallas_call(
        body, name=name, in_specs=[HBM] * n + [SEM, SEM] + [ANY] * len(after), out_specs=(SEM, SEM, *[HBM] * n, VMEM_SPEC),
        out_shape=(pltpu.SemaphoreType.DMA((3 * n,)), pltpu.SemaphoreType.DMA((3 * n,)), *[_hbm(a) for a in lands], TOKEN),
        input_output_aliases={i: 2 + i for i in range(n)},
        compiler_params=pltpu.CompilerParams(has_side_effects=EFFECT))(*lands, send0, recv0, *after)
    return outs[0], outs[1], list(outs[2:2 + n]), outs[-1]


def _gather_wait(lands, send0, recv0, send1, recv1, after, name):
    n = len(lands)

    def body(*refs):
        ins = refs[:n]
        send0, recv0, send1, recv1 = refs[n:n + 4]
        x, y, c = _position()
        mine = _block_of(x, y, c)
        sibling = (x, y, 1 - c)
        peers = [sibling, (1 - x, y, c), (x, 1 - y, c), (1 - x, 1 - y, c)]
        chips = [(1 - x, y), (x, 1 - y), (1 - x, 1 - y)]
        for wi in range(n):
            own = ins[wi].at[mine]
            for k, peer in enumerate(peers):
                pltpu.make_async_remote_copy(src_ref=own, dst_ref=own, send_sem=send0.at[4 * wi + k], recv_sem=recv0.at[4 * wi + k],
                                             device_id=peer, device_id_type=MESH).wait_send()
            theirs = ins[wi].at[_block_of(*sibling)]
            pltpu.make_async_remote_copy(src_ref=theirs, dst_ref=theirs, send_sem=send0.at[4 * wi], recv_sem=recv0.at[4 * wi],
                                         device_id=sibling, device_id_type=MESH).wait_recv()
            for j, chip in enumerate(chips):
                sent = ins[wi].at[_block_of(*chip, c)]
                got = ins[wi].at[_block_of(*chip, 1 - c)]
                pltpu.make_async_remote_copy(src_ref=sent, dst_ref=sent, send_sem=send1.at[3 * wi + j], recv_sem=recv1.at[3 * wi + j],
                                             device_id=sibling, device_id_type=MESH).wait_send()
                pltpu.make_async_remote_copy(src_ref=got, dst_ref=got, send_sem=send1.at[3 * wi + j], recv_sem=recv1.at[3 * wi + j],
                                             device_id=sibling, device_id_type=MESH).wait_recv()

    outs = pl.pallas_call(
        body, name=name, in_specs=[HBM] * n + [SEM] * 4 + [ANY], out_specs=tuple([HBM] * n),
        out_shape=tuple(_hbm(a) for a in lands), input_output_aliases={i: i for i in range(n)},
        compiler_params=pltpu.CompilerParams(has_side_effects=EFFECT))(*lands, send0, recv0, send1, recv1, after)
    return list(outs)


def _plan_direct(me):
    return [(_block_of(*_flip(me, k)), k - 1, _flip(me, k)) for k in range(1, N_DEV)]


def _plan_sibling(me):
    x, y, c = me
    return [(_block_of(ci // 2, ci % 2, 1 - c), ci, (x, y, 1 - c)) for ci in range(4)]


def _plan_chips(me):
    x, y, c = me
    out = []
    for k in range(1, 4):
        px, py = (1 - x if k & 2 else x), (1 - y if k & 1 else y)
        out.append((2 * px + py, k - 1, (px, py, c)))
    return out


def _exchange_start(blocks, plan, name):
    n = len(blocks)
    slots = len(plan((0, 0, 0)))

    def body(*refs):
        srcs, lands = refs[:n], refs[n:2 * n]
        send, recv = refs[2 * n], refs[2 * n + 1]
        token = refs[-1]
        for wi in range(n):
            for block, slot, peer in plan(_position()):
                pltpu.make_async_remote_copy(
                    src_ref=srcs[wi].at[block], dst_ref=lands[wi].at[slot], send_sem=send.at[slots * wi + slot],
                    recv_sem=recv.at[slots * wi + slot], device_id=peer, device_id_type=MESH).start()
        token[...] = jnp.zeros_like(token)

    zones = [_landing((slots,) + b.shape[1:], b.dtype) for b in blocks]
    outs = pl.pallas_call(
        body, name=name, in_specs=[HBM] * (2 * n), out_specs=(SEM, SEM, *[HBM] * (2 * n), VMEM_SPEC),
        out_shape=(pltpu.SemaphoreType.DMA((slots * n,)), pltpu.SemaphoreType.DMA((slots * n,)), *[_hbm(a) for a in blocks],
                   *[_hbm(z) for z in zones], TOKEN),
        input_output_aliases={i: 2 + i for i in range(2 * n)},
        compiler_params=pltpu.CompilerParams(has_side_effects=EFFECT))(
            *[pltpu.with_memory_space_constraint(b, pltpu.HBM) for b in blocks], *zones)
    return outs[0], outs[1], list(outs[2:2 + n]), list(outs[2 + n:2 + 2 * n]), outs[-1]


def _exchange_wait(groups, plan, after, name):
    flat, counts = [], []
    for send, recv, blocks, zones in groups:
        flat += [*blocks, *zones, send, recv]
        counts.append(len(blocks))
    slots = len(plan((0, 0, 0)))

    def body(*refs):
        pos = 0
        for n in counts:
            srcs, lands = refs[pos:pos + n], refs[pos + n:pos + 2 * n]
            send, recv = refs[pos + 2 * n], refs[pos + 2 * n + 1]
            pos += 2 * n + 2
            for wi in range(n):
                for block, slot, peer in plan(_position()):
                    cp = pltpu.make_async_remote_copy(
                        src_ref=srcs[wi].at[block], dst_ref=lands[wi].at[slot], send_sem=send.at[slots * wi + slot],
                        recv_sem=recv.at[slots * wi + slot], device_id=peer, device_id_type=MESH)
                    cp.wait_send()
                    cp.wait_recv()

    in_specs, out_specs, out_shape, aliases = [], [], [], {}
    i = 0
    for n, (send, recv, blocks, zones) in zip(counts, groups):
        for a in (*blocks, *zones):
            aliases[i] = len(out_shape)
            in_specs.append(HBM)
            out_specs.append(HBM)
            out_shape.append(_hbm(a))
            i += 1
        in_specs += [SEM, SEM]
        i += 2
    outs = pl.pallas_call(
        body, name=name, in_specs=in_specs + [ANY], out_specs=tuple(out_specs), out_shape=tuple(out_shape),
        input_output_aliases=aliases, compiler_params=pltpu.CompilerParams(has_side_effects=EFFECT))(*flat, after)
    res, pos = [], 0
    for n in counts:
        res.append((list(outs[pos:pos + n]), list(outs[pos + n:pos + 2 * n])))
        pos += 2 * n
    return res


def _sibling_sum(blocks, zone, core, name):
    _, R, C = zone.shape
    rt = next(r for r in (R, R // 2, R // 4, 128, 64) if R % r == 0 and r % 16 == 0 and r * C <= 4 * 1024 * 1024)

    def body(core_ref, own_ref, z_ref, o_ref):
        o_ref[...] = (own_ref[...].astype(F32) + z_ref[...].astype(F32)).astype(o_ref.dtype)

    grid_spec = pltpu.PrefetchScalarGridSpec(
        num_scalar_prefetch=1, grid=(4, R // rt),
        in_specs=[pl.BlockSpec((None, rt, C), lambda ci, i, core_ref: (2 * ci + core_ref[0], i, 0)),
                  pl.BlockSpec((None, rt, C), lambda ci, i, core_ref: (ci, i, 0))],
        out_specs=pl.BlockSpec((None, rt, C), lambda ci, i, core_ref: (ci, i, 0)))
    return pl.pallas_call(body, grid_spec=grid_spec, out_shape=jax.ShapeDtypeStruct(zone.shape, zone.dtype),
                          compiler_params=_params("parallel", "parallel"), name=name)(core, blocks, zone)


def _adamw(w, g, m, v):
    m = ADAM_B1 * m + (1.0 - ADAM_B1) * g
    v = ADAM_B2 * v + (1.0 - ADAM_B2) * (g * g)
    m_hat = m / (1.0 - ADAM_B1 ** ADAM_STEP)
    v_hat = v / (1.0 - ADAM_B2 ** ADAM_STEP)
    delta = -ADAM_LR * (m_hat / (jnp.sqrt(v_hat) + ADAM_EPS) + ADAM_WD * w)
    return delta, m, v


def _reduce_adamw(blocks, zone, mine, w, m, v, name):
    R, C = w.shape
    rt = 128 if R % 128 == 0 else 64
    assert R % rt == 0

    def body(mine_ref, own_ref, z_ref, w_ref, m_ref, v_ref, g_out, d_out, m_out, v_out):
        g = own_ref[...].astype(F32)
        for s in range(zone.shape[0]):
            g = g + z_ref[s].astype(F32)
        delta, m2, v2 = _adamw(w_ref[...], g, m_ref[...], v_ref[...])
        g_out[...] = g
        d_out[...] = delta
        m_out[...] = m2
        v_out[...] = v2

    spec = pl.BlockSpec((rt, C), lambda i, mine_ref: (i, 0))
    grid_spec = pltpu.PrefetchScalarGridSpec(
        num_scalar_prefetch=1, grid=(R // rt,),
        in_specs=[pl.BlockSpec((None, rt, C), lambda i, mine_ref: (mine_ref[0], i, 0)),
                  pl.BlockSpec((zone.shape[0], rt, C), lambda i, mine_ref: (0, i, 0)), spec, spec, spec],
        out_specs=[spec] * 4)
    return pl.pallas_call(body, grid_spec=grid_spec, out_shape=[jax.ShapeDtypeStruct((R, C), F32)] * 4,
                          compiler_params=_params("parallel"), name=name)(mine, blocks, zone, w, m, v)


def _small_step(part, w, m, v, after):
    R, C = part.shape

    def body(part_ref, w_ref, m_ref, v_ref, *rest):
        g_out, d_out, m_out, v_out, gath, send, recv = rest[len(after):]
        me = _position()
        gath[_block_of(*me)] = part_ref[...]

        def copy(k, slot):
            return pltpu.make_async_remote_copy(
                src_ref=part_ref, dst_ref=gath.at[slot], send_sem=send.at[k - 1], recv_sem=recv.at[k - 1],
                device_id=_flip(me, k), device_id_type=MESH)

        sent = [copy(k, _block_of(*me)) for k in range(1, N_DEV)]
        for cp in sent:
            cp.start()
        for k in range(1, N_DEV):
            copy(k, _block_of(*_flip(me, k))).wait_recv()
        for cp in sent:
            cp.wait_send()
        g = gath[0]
        for s in range(1, N_DEV):
            g = g + gath[s]
        delta, m2, v2 = _adamw(w_ref[...], g, m_ref[...], v_ref[...])
        g_out[...] = g
        d_out[...] = delta
        m_out[...] = m2
        v_out[...] = v2

    vm = pl.BlockSpec(memory_space=pltpu.VMEM)
    return pl.pallas_call(
        body, in_specs=[vm] * 4 + [ANY] * len(after), out_specs=[vm] * 4, out_shape=[jax.ShapeDtypeStruct((R, C), F32)] * 4,
        scratch_shapes=[pltpu.VMEM((N_DEV, R, C), F32), pltpu.SemaphoreType.DMA((7,)), pltpu.SemaphoreType.DMA((7,))],
        name="small_step")(part, w, m, v, *after)


COLUMN_SHARDED = ("w_in", "w_sb_out", "w_ca_out", "w_ffn_in", "w_ple_in")
ROW_SHARDED = ("w_mix_out", "w_ffn_out", "w_ple_gate")
BIG = COLUMN_SHARDED + ROW_SHARDED
SMALL = ("g_mix", "g_ffn", "g_ple", "g_final", "rel_bias")
WEIGHTS = ("w_in", "w_sb_out", "w_ca_out", "w_mix_out", "rel_bias", "g_mix", "g_ffn", "g_ple", "g_final",
           "w_ffn_in", "w_ffn_out", "w_ple_in", "w_ple_gate")


def _pack_small(t, D):
    rows = [t[n].reshape(1, D) for n in SMALL[:4]]
    rb = t["rel_bias"].reshape(1, -1)
    rows.append(jnp.pad(rb, ((0, 0), (0, D - rb.shape[1]))))
    return jnp.concatenate(rows + [jnp.zeros((8 - len(rows), D), F32)], axis=0)


def _unpack_small(a, like):
    out = {n: a[i].reshape(like[n].shape) for i, n in enumerate(SMALL[:4])}
    out["rel_bias"] = a[4, :like["rel_bias"].size].reshape(like["rel_bias"].shape)
    return out


GATHER_GROUPS = (("w_in_a",), ("w_in_b",), ("w_sb_out", "w_ca_out", "w_mix_out"), ("w_ffn_in",), ("w_ffn_out", "w_ple_gate", "w_ple_in"))
FORWARD_AFTER = ("norm1", "mm_in_a", "mm_in", "attention", "mm_ffn_in")
GRAD_GROUPS = (("w_ple_in", "w_ple_gate"), ("w_ffn_out",), ("w_ffn_in",), ("w_mix_out", "w_sb_out", "w_ca_out"), ("w_in",))


class _Exchange:
    def __init__(self, shards):
        me = _position()
        self.mine = _block_of(*me)
        self.chip = jnp.reshape(2 * me[0] + me[1], (1,)).astype(jnp.int32)
        self.core = jnp.reshape(me[2], (1,)).astype(jnp.int32)
        self.device = jnp.reshape(self.mine, (1,)).astype(jnp.int32)
        self.shapes = {n: ((N_DEV * s.shape[0], s.shape[1]) if n in ROW_SHARDED else (N_DEV,) + s.shape) for n, s in shards.items()}
        self.tokens = []
        self.ready = {}
        self.gathers = []
        for gi, names in enumerate(GATHER_GROUPS):
            lands = [lax.dynamic_update_slice(lax.empty((N_DEV,) + shards[n].shape, BF16), shards[n][None], (self.mine, 0, 0))
                     for n in names]
            behind = self.tokens[-1] if self.tokens else shards[names[0]]
            send0, recv0, lands, token = _gather_start(lands, behind, f"gather_start_{gi}")
            self.tokens.append(token)
            self.gathers.append(dict(names=names, lands=lands, sems=(send0, recv0), token=token))
        self.grads = {}
        self.exchanges = []

    def pending(self):
        tokens, self.tokens = self.tokens, []
        return tokens

    def stage(self, tag, *made):
        gi = FORWARD_AFTER.index(tag)
        gth = self.gathers[gi]
        send1, recv1, lands, token = _gather_forward(gth["lands"], *gth["sems"], made + tuple(self.tokens), f"gather_forward_{gi}")
        gth.update(lands=lands, sems=gth["sems"] + (send1, recv1), token=token)
        self.tokens.append(token)

    def weight(self, name, after=None):
        if name not in self.ready:
            gi = next(i for i, names in enumerate(GATHER_GROUPS) if name in names)
            gth = self.gathers[gi]
            for n, a in zip(gth["names"], _gather_wait(gth["lands"], *gth["sems"], gth["token"] if after is None else after, f"gather_wait_{gi}")):
                self.ready[n] = a.reshape(self.shapes[n])
        return self.ready[name]

    def grad(self, name, blocks):
        self.grads[name] = blocks if name in COLUMN_SHARDED else blocks.reshape((N_DEV, -1, blocks.shape[-1]))
        names = next(names for names in GRAD_GROUPS if name in names)
        if not all(n in self.grads for n in names):
            return
        blocks = [self.grads[n] for n in names]
        if names == GRAD_GROUPS[-1]:
            send, recv, blocks, zones, token = _exchange_start(blocks, _plan_sibling, "pair_start_" + names[0])
            self.pair = (send, recv, blocks, zones)
        else:
            send, recv, blocks, zones, token = _exchange_start(blocks, _plan_direct, "exchange_start_" + names[0])
            self.exchanges.append(dict(names=names, state=(send, recv, blocks, zones), plan=_plan_direct, own=self.device))
        self.tokens.append(token)

    def pair_done(self, after):
        names = GRAD_GROUPS[-1]
        (blocks, zones), = _exchange_wait([self.pair], _plan_sibling, after, "pair_wait_" + names[0])
        blocks = [_sibling_sum(b, z, self.core, "pair_sum_" + n) for n, b, z in zip(names, blocks, zones)]
        send, recv, blocks, zones, token = _exchange_start(blocks, _plan_chips, "exchange_start_" + names[0])
        self.exchanges.append(dict(names=names, state=(send, recv, blocks, zones), plan=_plan_chips, own=self.chip))
        self.tokens.append(token)

    def collect(self, which, after, name):
        sel = [e for e in self.exchanges if GRAD_GROUPS.index(e["names"]) in which]
        out = {}
        for e, (blocks, zones) in zip(sel, _exchange_wait([e["state"] for e in sel], sel[0]["plan"], after, name)):
            out.update({n: (b, e["own"], z) for n, b, z in zip(e["names"], blocks, zones)})
        return out


def kernel(x, p, w_in, w_sb_out, w_ca_out, w_mix_out, rel_bias, g_mix, g_ffn, g_ple, g_final, w_ffn_in, w_ffn_out, w_ple_in, w_ple_gate, loss_target, m_w_in, m_w_sb_out, m_w_ca_out, m_w_mix_out, m_rel_bias, m_g_mix, m_g_ffn, m_g_ple, m_g_final, m_w_ffn_in, m_w_ffn_out, m_w_ple_in, m_w_ple_gate, v_w_in, v_w_sb_out, v_w_ca_out, v_w_mix_out, v_rel_bias, v_g_mix, v_g_ffn, v_g_ple, v_g_final, v_w_ffn_in, v_w_ffn_out, v_w_ple_in, v_w_ple_gate):
    wts = dict(w_in=w_in, w_sb_out=w_sb_out, w_ca_out=w_ca_out, w_mix_out=w_mix_out, rel_bias=rel_bias, g_mix=g_mix, g_ffn=g_ffn,
               g_ple=g_ple, g_final=g_final, w_ffn_in=w_ffn_in, w_ffn_out=w_ffn_out, w_ple_in=w_ple_in, w_ple_gate=w_ple_gate)
    mom = dict(w_in=m_w_in, w_sb_out=m_w_sb_out, w_ca_out=m_w_ca_out, w_mix_out=m_w_mix_out, rel_bias=m_rel_bias, g_mix=m_g_mix,
               g_ffn=m_g_ffn, g_ple=m_g_ple, g_final=m_g_final, w_ffn_in=m_w_ffn_in, w_ffn_out=m_w_ffn_out, w_ple_in=m_w_ple_in,
               w_ple_gate=m_w_ple_gate)
    var = dict(w_in=v_w_in, w_sb_out=v_w_sb_out, w_ca_out=v_w_ca_out, w_mix_out=v_w_mix_out, rel_bias=v_rel_bias, g_mix=v_g_mix,
               g_ffn=v_g_ffn, g_ple=v_g_ple, g_final=v_g_final, w_ffn_in=v_w_ffn_in, w_ffn_out=v_w_ffn_out, w_ple_in=v_w_ple_in,
               w_ple_gate=v_w_ple_gate)
    T, D = x.shape[1], x.shape[2]
    shard = {n: wts[n].reshape(wts[n].shape[-2:]) for n in BIG}
    bf = {n: _cast_bf16(shard[n], "cast_" + n) for n in BIG}
    half = bf["w_in"].shape[0] // 2
    bf["w_in_a"], bf["w_in_b"] = bf["w_in"][:half], bf.pop("w_in")[half:]
    comm = _Exchange(bf)
    g = dict(g_mix=g_mix.reshape(1, D), g_ffn=g_ffn.reshape(1, D), g_ple=g_ple.reshape(1, D), g_final=g_final.reshape(1, D),
             rel_bias=rel_bias.reshape(rel_bias.shape[-2:]))

    loss, grad_x, dsmall = _local_step(x.reshape(T, D), p.reshape(T, -1), loss_target.reshape(T, D), comm, g)
    loss = lax.psum(loss[0, 0], ("x", "y", "c"))

    grad, delta, new_m, new_v = {}, {}, {}, {}

    def update(parts):
        done = []
        for n, (blocks, own, zone) in parts.items():
            outs = _reduce_adamw(blocks, zone, own, shard[n], mom[n].reshape(shard[n].shape), var[n].reshape(shard[n].shape), "adamw_" + n)
            grad[n], delta[n], new_m[n], new_v[n] = [o.reshape(wts[n].shape) for o in outs]
            done.append(outs[0])
        return done

    done = update(comm.collect(range(len(GRAD_GROUPS) - 1), grad_x, "exchange_wait_rest"))
    outs = _small_step(_pack_small(dsmall, D), _pack_small(wts, D), _pack_small(mom, D), _pack_small(var, D), done)
    for dst, a in zip((grad, delta, new_m, new_v), outs):
        dst.update(_unpack_small(a, wts))
    update(comm.collect([len(GRAD_GROUPS) - 1], outs[0], "exchange_wait_w_in"))

    return (loss, grad_x.reshape(x.shape), *[grad[n] for n in WEIGHTS], *[delta[n] for n in WEIGHTS],
            *[new_m[n] for n in WEIGHTS], *[new_v[n] for n in WEIGHTS])
```

```python
import functools

import jax
import jax.numpy as jnp
from jax import lax
from jax.experimental import pallas as pl
from jax.experimental.pallas import tpu as pltpu

F32, BF16 = jnp.float32, jnp.bfloat16

N_DEV = 8
HEAD_DIM = 128
CHUNK = 64
LEFT_CHUNKS = 8
REL_CLIP = 128
N_REL = REL_CLIP + CHUNK
PAIR = 2 * CHUNK
PBAND = (LEFT_CHUNKS + 2) * CHUNK
CA_PAIRS = 2
CA_ROWS = CA_PAIRS * PAIR
CA_BAND = PBAND + CA_ROWS - PAIR
PAD = LEFT_CHUNKS * CHUNK
SB_BLOCK = 256
ROWS = 256
EPS = 1e-6
NEG = -1e30
SCALE = HEAD_DIM ** -0.5
VMEM_LIMIT_BYTES = 56 * 1024 * 1024

ADAM_LR, ADAM_B1, ADAM_B2, ADAM_EPS, ADAM_WD, ADAM_STEP = 0.001, 0.9, 0.999, 1e-08, 0.01, 10

ANY = pl.BlockSpec(memory_space=pl.ANY)
NN = (((1,), (0,)), ((), ()))
NT = (((1,), (1,)), ((), ()))
TN = (((0,), (0,)), ((), ()))
MESH = pl.DeviceIdType.MESH


def _params(*sem):
    return pltpu.CompilerParams(dimension_semantics=sem or None, vmem_limit_bytes=VMEM_LIMIT_BYTES)


def _dot(a, b, dims=NN):
    return lax.dot_general(a, b, dims, preferred_element_type=F32)


def _mm(a, b, *, mode, tm, tn, tk, out_dtype, name, b_blocked=False, out_block=None, res=None, after=(), a_cols=(1, 0), out_cols=(1, 1, 0), into=None, b_first=0, b_count=None, o_first=0, o_count=None):
    bg = og = 1
    if mode == "nn":
        M, K = a.shape
        a_spec = pl.BlockSpec((tm, tk), lambda i, j, k: (i, k))
        if b_blocked:
            G, _, nb = b.shape
            N = G * nb
            if tn > nb:
                bg = tn // nb
                assert tn % nb == 0
                b_spec = pl.BlockSpec((bg, tk, nb), lambda i, j, k: (j, k, 0))
            else:
                per = nb // tn
                assert nb % tn == 0
                b_spec = pl.BlockSpec((None, tk, tn), lambda i, j, k: (j // per, k, j % per))
        else:
            N = b.shape[1]
            b_spec = pl.BlockSpec((tk, tn), lambda i, j, k: (k, j))
        dims = NN
    elif mode == "nt":
        M, K = a.shape
        a_spec = pl.BlockSpec((tm, tk), lambda i, j, k: (i, k))
        if b_blocked:
            G, N, nb = b.shape
            K = (b_count or G) * nb
            assert b_first == 0 or tk == nb
            if tk > nb:
                bg = tk // nb
                assert tk % nb == 0
                b_spec = pl.BlockSpec((bg, tn, nb), lambda i, j, k: (k, j, 0))
            else:
                per = nb // tk
                assert nb % tk == 0
                b_spec = pl.BlockSpec((None, tn, tk), lambda i, j, k: (b_first + k // per, j, k % per))
        else:
            N = b.shape[0]
            b_spec = pl.BlockSpec((tn, tk), lambda i, j, k: (j, k))
        dims = NT
    else:
        K, M = a.shape
        N = b.shape[1]
        a_spec = pl.BlockSpec((tk, tm), lambda i, j, k: (k, i))
        b_spec = pl.BlockSpec((tk, tn), lambda i, j, k: (k, j))
        dims = TN
    if mode != "tn":
        if mode == "nn":
            K = b.shape[-2]
        elif not b_blocked:
            K = b.shape[1]
        a_spec = pl.BlockSpec((tm, tk), lambda i, j, k: (i, k * a_cols[0] + a_cols[1]))
    assert M % tm == 0 and N % tn == 0 and K % tk == 0, (name, M, N, K, tm, tn, tk)
    nk = K // tk
    if out_block is None:
        out_shape = jax.ShapeDtypeStruct((M, N * out_cols[0]), out_dtype)
        o_spec = pl.BlockSpec((tm, tn), lambda i, j, k: (i, j * out_cols[1] + out_cols[2]))
    else:
        out_shape = jax.ShapeDtypeStruct((o_count or N // out_block, M, out_block), out_dtype)
        if tn > out_block:
            og = tn // out_block
            assert tn % out_block == 0 and o_first % og == 0
            o_spec = pl.BlockSpec((og, tm, out_block), lambda i, j, k: (o_first // og + j, i, 0))
        else:
            per_o = out_block // tn
            assert out_block % tn == 0
            o_spec = pl.BlockSpec((None, tm, tn), lambda i, j, k: (o_first + j // per_o, i, j % per_o))
    in_specs = [a_spec, b_spec]
    args = [a, b]
    if res is not None:
        in_specs.append(pl.BlockSpec((tm, tn), lambda i, j, k: (i, j * out_cols[1] + out_cols[2])))
        args.append(res)
    n_in = len(args) + len(after) + (into is not None)

    def product(a_ref, b_ref):
        if bg == 1:
            return _dot(a_ref[...], b_ref[...], dims)
        nb = b_ref.shape[2]
        if mode == "nn":
            return jnp.concatenate([_dot(a_ref[...], b_ref[g], dims) for g in range(bg)], axis=1)
        return sum(_dot(a_ref[:, g * nb:(g + 1) * nb], b_ref[g], dims) for g in range(bg))

    def body(*refs):
        a_ref, b_ref = refs[0], refs[1]
        r_ref = refs[2] if res is not None else None
        o_ref = refs[n_in]

        def finish(acc):
            if r_ref is not None:
                acc = acc + r_ref[...]
            if og == 1:
                o_ref[...] = acc.astype(o_ref.dtype)
            else:
                for g in range(og):
                    o_ref[g] = acc[:, g * out_block:(g + 1) * out_block].astype(o_ref.dtype)

        if nk == 1:
            finish(product(a_ref, b_ref))
        else:
            acc_ref = refs[-1]
            k = pl.program_id(2)

            @pl.when(k == 0)
            def _():
                acc_ref[...] = jnp.zeros_like(acc_ref)

            acc_ref[...] += product(a_ref, b_ref)

            @pl.when(k == nk - 1)
            def _():
                finish(acc_ref[...])

    return pl.pallas_call(
        body, grid=(M // tm, N // tn, nk), in_specs=in_specs + [ANY] * (n_in - len(args)), out_specs=o_spec, out_shape=out_shape,
        scratch_shapes=[] if nk == 1 else [pltpu.VMEM((tm, tn), F32)], input_output_aliases={} if into is None else {n_in - 1: 0},
        compiler_params=_params("parallel", "parallel", "arbitrary"), name=name)(*args, *after, *(() if into is None else (into,)))


def _mm_fused(a, b, tiles, fn, outs, *, mode, tm, tn, tk, name, sums=(), after=(), b_outer=False):
    M, K = a.shape
    N = b.shape[1] if mode == "nn" else b.shape[0]
    nk = K // tk
    assert M % tm == 0 and N % tn == 0 and K % tk == 0 and (not sums or tn == N)
    def at(f):
        return (lambda j, i, k: f(i, j, k)) if b_outer else f

    b_spec = pl.BlockSpec((tk, tn), at(lambda i, j, k: (k, j))) if mode == "nn" else pl.BlockSpec((tn, tk), at(lambda i, j, k: (j, k)))
    in_specs = [pl.BlockSpec((tm, tk), at(lambda i, j, k: (i, k))), b_spec]
    args = [a, b]
    for t in tiles:
        if isinstance(t, tuple):
            arr, off = t
            in_specs.append(pl.BlockSpec((tm, tn), at(lambda i, j, k, off=off: (i, off + j))))
        else:
            arr = t
            in_specs.append(pl.BlockSpec((1, tn), at(lambda i, j, k: (0, j))))
        args.append(arr)
    n_in = len(args) + len(after)
    n_out = len(outs) + len(sums)

    def body(*refs):
        a_ref, b_ref = refs[0], refs[1]
        t_refs = refs[2:2 + len(tiles)]
        o_refs = refs[n_in:n_in + n_out]

        def finish(acc):
            res = fn(acc, *[t[...] for t in t_refs])
            for o_ref, r in zip(o_refs[:len(outs)], res):
                o_ref[...] = r.astype(o_ref.dtype)
            if sums:
                @pl.when(pl.program_id(0) == 0)
                def _():
                    for o_ref in o_refs[len(outs):]:
                        o_ref[...] = jnp.zeros_like(o_ref)

                for o_ref, r in zip(o_refs[len(outs):], res[len(outs):]):
                    o_ref[...] += jnp.broadcast_to(r, o_ref.shape)

        if nk == 1:
            finish(_dot(a_ref[...], b_ref[...], NN if mode == "nn" else NT))
        else:
            acc_ref = refs[-1]
            k = pl.program_id(2)

            @pl.when(k == 0)
            def _():
                acc_ref[...] = jnp.zeros_like(acc_ref)

            acc_ref[...] += _dot(a_ref[...], b_ref[...], NN if mode == "nn" else NT)

            @pl.when(k == nk - 1)
            def _():
                finish(acc_ref[...])

    assert not (b_outer and sums)
    o_spec = pl.BlockSpec((tm, tn), at(lambda i, j, k: (i, j)))
    return pl.pallas_call(
        body, grid=(N // tn, M // tm, nk) if b_outer else (M // tm, N // tn, nk), in_specs=in_specs + [ANY] * len(after),
        out_specs=[o_spec] * len(outs) + [pl.BlockSpec(sh, lambda i, j, k: (0, 0)) for sh in sums],
        out_shape=[jax.ShapeDtypeStruct((M, N), dt) for dt in outs] + [jax.ShapeDtypeStruct(sh, F32) for sh in sums],
        scratch_shapes=[] if nk == 1 else [pltpu.VMEM((tm, tn), F32)],
        compiler_params=_params("arbitrary" if sums else "parallel", "parallel", "arbitrary"), name=name)(*args, *after)


def _row_spec(d, col=0):
    return pl.BlockSpec((ROWS, d), lambda i: (i, col))


def _vec_spec(d):
    return pl.BlockSpec((1, d), lambda i: (0, 0))


def _rms(x):
    return lax.rsqrt(jnp.mean(x * x, axis=-1, keepdims=True) + EPS)


def _norm_fwd(x, g, name):
    T, D = x.shape

    def body(x_ref, g_ref, h_ref):
        xv = x_ref[...]
        h_ref[...] = (xv * _rms(xv) * g_ref[...]).astype(BF16)

    return pl.pallas_call(body, grid=(T // ROWS,), in_specs=[_row_spec(D), _vec_spec(D)], out_specs=_row_spec(D),
                          out_shape=jax.ShapeDtypeStruct((T, D), BF16), compiler_params=_params("parallel"), name=name)(x, g)


def _residual_norm(y, x, g):
    x = x + y
    return x, x * _rms(x) * g


def _norm_bwd_math(dh, xv, gv):
    r = _rms(xv)
    xhat = xv * r
    dxhat = dh * gv
    dx = r * (dxhat - xhat * jnp.mean(dxhat * xhat, axis=-1, keepdims=True))
    dg = jnp.sum(dh * xhat, axis=0, keepdims=True)
    return dx, dg


def _residual_norm_bwd(dh, x, dres, g):
    dx, dg = _norm_bwd_math(dh, x, g)
    dx = dx + dres
    return dx, dx, dg


def _norm_bwd(dh, x, g, dres, name):
    T, D = x.shape

    def body(dh_ref, x_ref, g_ref, dres_ref, dx_ref, dxb_ref, dg_ref):
        dx, dg = _norm_bwd_math(dh_ref[...], x_ref[...], g_ref[...])
        dx = dx + dres_ref[...]
        dx_ref[...] = dx
        dxb_ref[...] = dx.astype(BF16)

        @pl.when(pl.program_id(0) == 0)
        def _():
            dg_ref[...] = jnp.zeros_like(dg_ref)

        dg_ref[...] += dg

    return pl.pallas_call(
        body, grid=(T // ROWS,), in_specs=[_row_spec(D), _row_spec(D), _vec_spec(D), _row_spec(D)],
        out_specs=[_row_spec(D), _row_spec(D), _vec_spec(D)],
        out_shape=[jax.ShapeDtypeStruct((T, D), F32), jax.ShapeDtypeStruct((T, D), BF16), jax.ShapeDtypeStruct((1, D), F32)],
        compiler_params=_params("arbitrary"), name=name)(dh, x, g, dres)


def _mm_merge(y_sb, y_ca, w_sb, w_ca, proj, gate_col, tm, tn, after):
    T, W = y_sb.shape
    G, _, nb = w_sb.shape
    D, bg = G * nb, tn // nb
    assert tn % nb == 0 and D % tn == 0
    per = D // tn

    def body(ys_ref, yc_ref, ws_ref, wc_ref, gs_ref, gc_ref, *rest):
        as_ref, ac_ref, m_ref = rest[len(after):]
        a = jnp.concatenate([_dot(ys_ref[...], ws_ref[g]) for g in range(bg)], axis=1)
        b = jnp.concatenate([_dot(yc_ref[...], wc_ref[g]) for g in range(bg)], axis=1)
        as_ref[...] = a
        ac_ref[...] = b
        m_ref[...] = (jax.nn.sigmoid(gs_ref[...]) * a + jax.nn.sigmoid(gc_ref[...]) * b).astype(BF16)

    y_spec = pl.BlockSpec((tm, W), lambda i, j: (i, 0))
    w_spec = pl.BlockSpec((bg, W, nb), lambda i, j: (j, 0, 0))
    out = pl.BlockSpec((tm, tn), lambda i, j: (i, j))
    f32 = jax.ShapeDtypeStruct((T, D), F32)
    return pl.pallas_call(
        body, grid=(T // tm, per),
        in_specs=[y_spec, y_spec, w_spec, w_spec, pl.BlockSpec((tm, tn), lambda i, j: (i, gate_col * per + j)),
                  pl.BlockSpec((tm, tn), lambda i, j: (i, (gate_col + 1) * per + j))] + [ANY] * len(after),
        out_specs=[out, out, out], out_shape=[f32, f32, jax.ShapeDtypeStruct((T, D), BF16)],
        compiler_params=_params("parallel", "parallel"), name="mm_merge")(y_sb, y_ca, w_sb, w_ca, proj, proj, *after)


def _merge_bwd(dm, gs, gc, a, b):
    ss, sc = jax.nn.sigmoid(gs), jax.nn.sigmoid(gc)
    return dm * ss, dm * sc, dm * a * ss * (1.0 - ss), dm * b * sc * (1.0 - sc)


def _mm_swiglu(h, w, tm):
    T, D = h.shape
    G2, _, nb = w.shape
    G = G2 // 2

    def body(h_ref, wg_ref, wu_ref, g_ref, u_ref, act_ref):
        hv = h_ref[...]
        gv = _dot(hv, wg_ref[...])
        uv = _dot(hv, wu_ref[...])
        g_ref[...] = gv
        u_ref[...] = uv
        act_ref[...] = (gv * jax.nn.sigmoid(gv) * uv).astype(BF16)

    out = pl.BlockSpec((tm, nb), lambda j, i: (i, j))
    f32 = jax.ShapeDtypeStruct((T, G * nb), F32)
    return pl.pallas_call(
        body, grid=(G, T // tm),
        in_specs=[pl.BlockSpec((tm, D), lambda j, i: (i, 0)), pl.BlockSpec((None, D, nb), lambda j, i: (j, 0, 0)),
                  pl.BlockSpec((None, D, nb), lambda j, i: (j + G, 0, 0))],
        out_specs=[out, out, out], out_shape=[f32, f32, jax.ShapeDtypeStruct((T, G * nb), BF16)],
        compiler_params=_params("parallel", "parallel"), name="mm_ffn_in")(h, w, w)


def _swiglu_bwd(dact, gate, up):
    s = jax.nn.sigmoid(gate)
    return dact * up * s * (1.0 + gate * (1.0 - s)), dact * gate * s


def _tail(zg, x3, pe, target, g_final):
    D = x3.shape[-1]
    gate = jax.nn.sigmoid(zg)
    x4 = x3 + gate * pe
    err = x4 * _rms(x4) * g_final - target
    part = 0.5 * jnp.sum(jnp.mean(err * err, axis=-1, keepdims=True), axis=0, keepdims=True)
    dx, dg = _norm_bwd_math(err * (1.0 / D), x4, g_final)
    return dx, dx * gate, dx * pe * gate * (1.0 - gate), part, dg


def _cast_bf16(x, name):
    R, C = x.shape
    rows = next(r for r in (ROWS, 128, 64, 32, 16) if R % r == 0)

    def body(x_ref, o_ref):
        o_ref[...] = x_ref[...].astype(BF16)

    spec = pl.BlockSpec((rows, C), lambda i: (i, 0))
    return pl.pallas_call(body, grid=(R // rows,), in_specs=[spec], out_specs=spec, out_shape=jax.ShapeDtypeStruct((R, C), BF16),
                          compiler_params=_params("parallel"), name=name)(x)


def _head_spec(T, col0, heads=1):
    return pl.BlockSpec((T, heads * HEAD_DIM), lambda h, *_: (0, col0 + h))


SB_HEADS = 4


def _triangle(n, right):
    j = lax.broadcasted_iota(jnp.int32, (n, n), 0)
    s = lax.broadcasted_iota(jnp.int32, (n, n), 1)
    return jnp.where((j > s) if right else (j < s), 1.0, 0.0).astype(BF16)


def _lane_scan(x, tri):
    hi = x.astype(BF16)
    lo = (x - hi.astype(F32)).astype(BF16)
    return _dot(hi, tri) + _dot(lo, tri)


def _head_cols(ref, rows, hh):
    return ref[rows, hh * HEAD_DIM:(hh + 1) * HEAD_DIM]


def _sb_tile(qv, kk, past, c_lk, tri):
    z = _dot(qv, kk, NT) * SCALE
    sp = jnp.log(1.0 + jnp.exp(-jnp.abs(z)))
    ls_pos = jnp.minimum(z, 0.0) - sp
    lk = jnp.minimum(-z, 0.0) - sp
    if past is not None:
        lk = jnp.where(past, lk, 0.0)
    right = c_lk + _lane_scan(lk, tri)
    a = jnp.exp(ls_pos + right)
    if past is not None:
        a = jnp.where(past, a, 0.0)
    return ls_pos, a, right[:, 0:1] + lk[:, 0:1]


SB_Q = 512
SB_HEADS_BWD = 2


def _sb_mask(d):
    B, r = SB_BLOCK, SB_Q // SB_BLOCK
    return lax.broadcasted_iota(jnp.int32, (SB_Q, B), 1) + (r - 1 - d) * B < lax.broadcasted_iota(jnp.int32, (SB_Q, B), 0)


def _sb_rows(kb):
    return pl.ds(pl.multiple_of(kb * SB_BLOCK, SB_BLOCK), SB_BLOCK)


def _sb_fwd(proj, n_heads, after=()):
    T = proj.shape[0]
    B, Q, HP = SB_BLOCK, SB_Q, SB_HEADS
    r = Q // B
    assert n_heads % HP == 0 and T % Q == 0

    def body(q_ref, k_ref, v_ref, *rest):
        y_ref = rest[-1]
        qb = pl.program_id(1)
        tri = _triangle(B, right=True)
        qv = [_head_cols(q_ref, slice(None), hh).astype(BF16) for hh in range(HP)]

        def tile(kb, carry, past):
            out = []
            for hh in range(HP):
                acc, c_lk = carry[hh]
                kk = _head_cols(k_ref, _sb_rows(kb), hh).astype(BF16)
                vv = _head_cols(v_ref, _sb_rows(kb), hh).astype(BF16)
                _, a, c_lk = _sb_tile(qv[hh], kk, past, c_lk, tri)
                out.append((acc + _dot(a.astype(BF16), vv), c_lk))
            return tuple(out)

        carry = tuple((jnp.zeros((Q, HEAD_DIM), F32), jnp.zeros((Q, 1), F32)) for _ in range(HP))
        for d in range(r):
            carry = tile(r * qb + r - 1 - d, carry, _sb_mask(d))
        res = lax.fori_loop(0, r * qb, lambda i, c: tile(r * qb - 1 - i, c, None), carry)
        for hh in range(HP):
            y_ref[:, hh * HEAD_DIM:(hh + 1) * HEAD_DIM] = res[hh][0].astype(BF16)

    blk = pl.BlockSpec((Q, HP * HEAD_DIM), lambda h, i: (i, h))
    G = n_heads // HP
    return pl.pallas_call(
        body, grid=(G, T // Q),
        in_specs=[blk, _head_spec(T, G, HP), _head_spec(T, 2 * G, HP)] + [ANY] * len(after), out_specs=blk,
        out_shape=jax.ShapeDtypeStruct((T, n_heads * HEAD_DIM), BF16),
        compiler_params=_params("parallel", "arbitrary"), name="sb_fwd")(proj, proj, proj, *after)


def _sb_bwd(proj, dy, n_heads):
    T = proj.shape[0]
    B, Q, HP = SB_BLOCK, SB_Q, SB_HEADS_BWD
    r, nq = Q // B, T // Q

    def body(q_ref, k_ref, v_ref, dy_ref, dq_ref, dk_ref, dv_ref, g_s, sig_s, dk_s, dv_s):
        qb = pl.program_id(1)

        @pl.when(qb == 0)
        def _():
            dk_s[...] = jnp.zeros_like(dk_s)
            dv_s[...] = jnp.zeros_like(dv_s)

        tri_r = _triangle(B, right=True)
        tri_l = _triangle(B, right=False)
        qv = [_head_cols(q_ref, slice(None), hh).astype(BF16) for hh in range(HP)]
        dyb = [_head_cols(dy_ref, slice(None), hh).astype(BF16) for hh in range(HP)]

        def sweep(kb, carry, past):
            out = []
            for hh in range(HP):
                kk = _head_cols(k_ref, _sb_rows(kb), hh).astype(BF16)
                vv = _head_cols(v_ref, _sb_rows(kb), hh).astype(BF16)
                ls_pos, a, c_lk = _sb_tile(qv[hh], kk, past, carry[hh], tri_r)
                g_s[hh, kb] = _dot(dyb[hh], vv, NT) * a
                sig_s[hh, kb] = jnp.exp(ls_pos)
                dv_s[_sb_rows(kb), hh * HEAD_DIM:(hh + 1) * HEAD_DIM] += _dot(a.astype(BF16), dyb[hh], TN)
                out.append(c_lk)
            return tuple(out)

        carry = tuple(jnp.zeros((Q, 1), F32) for _ in range(HP))
        for d in range(r):
            carry = sweep(r * qb + r - 1 - d, carry, _sb_mask(d))
        lax.fori_loop(0, r * qb, lambda i, c: sweep(r * qb - 1 - i, c, None), carry)

        def back(kb, carry, past):
            out = []
            for hh in range(HP):
                dq, c_g = carry[hh]
                kk = _head_cols(k_ref, _sb_rows(kb), hh).astype(BF16)
                g, sig = g_s[hh, kb], sig_s[hh, kb]
                left = c_g + _lane_scan(g, tri_l)
                dz = g * (1.0 - sig) - left * sig
                if past is not None:
                    dz = jnp.where(past, dz, 0.0)
                dz = (dz * SCALE).astype(BF16)
                dk_s[_sb_rows(kb), hh * HEAD_DIM:(hh + 1) * HEAD_DIM] += _dot(dz, qv[hh], TN)
                out.append((dq + _dot(dz, kk), left[:, B - 1:B] + g[:, B - 1:B]))
            return tuple(out)

        init = tuple((jnp.zeros((Q, HEAD_DIM), F32), jnp.zeros((Q, 1), F32)) for _ in range(HP))
        res = lax.fori_loop(0, r * qb, lambda kb, c: back(kb, c, None), init)
        for d in reversed(range(r)):
            res = back(r * qb + r - 1 - d, res, _sb_mask(d))
        for hh in range(HP):
            dq_ref[:, hh * HEAD_DIM:(hh + 1) * HEAD_DIM] = res[hh][0].astype(BF16)

        @pl.when(qb == nq - 1)
        def _():
            dk_ref[...] = dk_s[...].astype(BF16)
            dv_ref[...] = dv_s[...].astype(BF16)

    blk = pl.BlockSpec((Q, HP * HEAD_DIM), lambda h, i: (i, h))
    G = n_heads // HP
    full = _head_spec(T, 0, HP)
    shp = jax.ShapeDtypeStruct((T, n_heads * HEAD_DIM), BF16)
    return pl.pallas_call(
        body, grid=(G, nq),
        in_specs=[blk, _head_spec(T, G, HP), _head_spec(T, 2 * G, HP), blk], out_specs=[blk, full, full],
        out_shape=[shp, shp, shp],
        scratch_shapes=[pltpu.VMEM((HP, T // B, Q, B), F32)] * 2 + [pltpu.VMEM((T, HP * HEAD_DIM), F32)] * 2,
        compiler_params=_params("parallel", "arbitrary"), name="sb_bwd")(proj, proj, proj, dy)


DIAGS = PBAND + PAIR


def _diag_onehot():
    d = lax.broadcasted_iota(jnp.int32, (DIAGS, 2 * PAIR), 0)
    r = lax.broadcasted_iota(jnp.int32, (DIAGS, 2 * PAIR), 1)
    return jnp.where(jnp.clip(d - PAIR - PAD, -REL_CLIP, CHUNK - 1) + REL_CLIP == r, 1.0, 0.0)


def _bias_expand(rel_bias):
    H = rel_bias.shape[0]
    table = jnp.pad(rel_bias, ((0, 0), (0, 2 * PAIR - N_REL)))

    def body(rb_ref, o_ref):
        o_ref[...] = lax.dot_general(rb_ref[...], _diag_onehot(), NT, precision=lax.Precision.HIGHEST, preferred_element_type=F32)

    per_diag = pl.pallas_call(body, out_shape=jax.ShapeDtypeStruct((H, DIAGS), F32), name="bias_expand")(table)
    flat = jnp.tile(jnp.pad(per_diag, ((0, 0), (0, 1))), (1, PAIR))[:, :PAIR * DIAGS]
    return flat.reshape(H, PAIR, DIAGS)[:, :, PAIR:]


def _bias_reduce(dbias):
    H = dbias.shape[0]
    padded = jnp.pad(dbias, ((0, 0), (0, 1), (PAIR, 0))).reshape(H, -1)
    skewed = padded[:, :PAIR * (DIAGS + 1)].reshape(H, PAIR, DIAGS + 1)[:, :, :DIAGS]

    def body(s_ref, o_ref):
        per_diag = jnp.sum(s_ref[...], axis=0, keepdims=True)
        o_ref[...] = lax.dot_general(jnp.broadcast_to(per_diag, (8, DIAGS)), _diag_onehot(), NN, precision=lax.Precision.HIGHEST,
                                     preferred_element_type=F32)[0:1]

    return pl.pallas_call(
        body, grid=(H,), in_specs=[pl.BlockSpec((None, PAIR, DIAGS), lambda h: (h, 0, 0))],
        out_specs=pl.BlockSpec((None, 1, 2 * PAIR), lambda h: (h, 0, 0)),
        out_shape=jax.ShapeDtypeStruct((H, 1, 2 * PAIR), F32), compiler_params=_params("parallel"), name="bias_reduce")(skewed)[:, 0]


CA_HEADS = 2


def _ca_mask():
    i = lax.broadcasted_iota(jnp.int32, (CA_ROWS, CA_BAND), 0)
    j = lax.broadcasted_iota(jnp.int32, (CA_ROWS, CA_BAND), 1)
    qc, kc = i // CHUNK, j // CHUNK
    return j, (kc >= qc) & (kc <= qc + LEFT_CHUNKS)


def _ca_bias(pair_bias):
    rows = []
    for q in range(CA_PAIRS):
        parts = [jnp.zeros((PAIR, q * PAIR), F32)] * (q > 0) + [pair_bias] + [jnp.zeros((PAIR, (CA_PAIRS - 1 - q) * PAIR), F32)] * (q < CA_PAIRS - 1)
        rows.append(jnp.concatenate(parts, axis=1) if len(parts) > 1 else parts[0])
    return jnp.concatenate(rows, axis=0)


def _ca_weights(pr, qp, kb, bias, j, window):
    valid = window & (pr * CA_ROWS + j >= PAD)
    z = jnp.where(valid, _dot(qp, kb, NT) * SCALE + bias, NEG)
    e = jnp.exp(z - jnp.max(z, axis=1, keepdims=True))
    return e / jnp.sum(e, axis=1, keepdims=True)


def _ca_fill(k_ref, v_ref, kpad, vpad):
    T, W = k_ref.shape
    kpad[0:PAD, :] = jnp.zeros((PAD, W), BF16)
    vpad[0:PAD, :] = jnp.zeros((PAD, W), BF16)
    kpad[PAD:PAD + T, :] = k_ref[...].astype(BF16)
    vpad[PAD:PAD + T, :] = v_ref[...].astype(BF16)


def _ca_fwd(proj, bias, n_heads, col0):
    T = proj.shape[0]
    HP = CA_HEADS
    G = n_heads // HP
    assert n_heads % HP == 0 and col0 % HP == 0

    def body(q_ref, k_ref, v_ref, b_ref, y_ref, kpad, vpad):
        _ca_fill(k_ref, v_ref, kpad, vpad)
        j, window = _ca_mask()
        bias = [_ca_bias(b_ref[hh]) for hh in range(HP)]

        def step(pr, _):
            r0 = pl.multiple_of(pr * CA_ROWS, CA_ROWS)
            for hh in range(HP):
                qp = _head_cols(q_ref, pl.ds(r0, CA_ROWS), hh).astype(BF16)
                kb = _head_cols(kpad, pl.ds(r0, CA_BAND), hh)
                vb = _head_cols(vpad, pl.ds(r0, CA_BAND), hh)
                w = _ca_weights(pr, qp, kb, bias[hh], j, window)
                y_ref[pl.ds(r0, CA_ROWS), hh * HEAD_DIM:(hh + 1) * HEAD_DIM] = _dot(w.astype(BF16), vb).astype(BF16)
            return 0

        lax.fori_loop(0, T // CA_ROWS, step, 0)

    c = col0 // HP
    return pl.pallas_call(
        body, grid=(G,),
        in_specs=[_head_spec(T, c, HP), _head_spec(T, c + G, HP), _head_spec(T, c + 2 * G, HP),
                  pl.BlockSpec((HP, PAIR, PBAND), lambda h: (h, 0, 0))],
        out_specs=_head_spec(T, 0, HP), out_shape=jax.ShapeDtypeStruct((T, n_heads * HEAD_DIM), BF16),
        scratch_shapes=[pltpu.VMEM((PAD + T, HP * HEAD_DIM), BF16)] * 2,
        compiler_params=_params("parallel"), name="ca_fwd")(proj, proj, proj, bias)


def _ca_bwd(proj, bias, dy, n_heads, col0):
    T = proj.shape[0]
    HP = CA_HEADS
    G = n_heads // HP

    def body(q_ref, k_ref, v_ref, b_ref, dy_ref, dq_ref, dk_ref, dv_ref, db_ref, kpad, vpad, dkpad, dvpad):
        _ca_fill(k_ref, v_ref, kpad, vpad)
        dkpad[...] = jnp.zeros_like(dkpad)
        dvpad[...] = jnp.zeros_like(dvpad)
        db_ref[...] = jnp.zeros_like(db_ref)
        j, window = _ca_mask()
        bias = [_ca_bias(b_ref[hh]) for hh in range(HP)]

        def step(pr, _):
            r0 = pl.multiple_of(pr * CA_ROWS, CA_ROWS)
            for hh in range(HP):
                cols = slice(hh * HEAD_DIM, (hh + 1) * HEAD_DIM)
                qp = _head_cols(q_ref, pl.ds(r0, CA_ROWS), hh).astype(BF16)
                kb = _head_cols(kpad, pl.ds(r0, CA_BAND), hh)
                vb = _head_cols(vpad, pl.ds(r0, CA_BAND), hh)
                w = _ca_weights(pr, qp, kb, bias[hh], j, window)
                dyp = _head_cols(dy_ref, pl.ds(r0, CA_ROWS), hh).astype(BF16)
                dw = _dot(dyp, vb, NT)
                dz = w * (dw - jnp.sum(dw * w, axis=1, keepdims=True))
                db_ref[hh] += sum(dz[q * PAIR:(q + 1) * PAIR, q * PAIR:q * PAIR + PBAND] for q in range(CA_PAIRS))
                dzs = (dz * SCALE).astype(BF16)
                dq_ref[pl.ds(r0, CA_ROWS), cols] = _dot(dzs, kb).astype(BF16)
                dkpad[pl.ds(r0, CA_BAND), cols] += _dot(dzs, qp, TN)
                dvpad[pl.ds(r0, CA_BAND), cols] += _dot(w.astype(BF16), dyp, TN)
            return 0

        lax.fori_loop(0, T // CA_ROWS, step, 0)
        dk_ref[...] = dkpad[PAD:PAD + T, :].astype(BF16)
        dv_ref[...] = dvpad[PAD:PAD + T, :].astype(BF16)

    c = col0 // HP
    full = _head_spec(T, 0, HP)
    bspec = pl.BlockSpec((HP, PAIR, PBAND), lambda h: (h, 0, 0))
    shp = jax.ShapeDtypeStruct((T, n_heads * HEAD_DIM), BF16)
    return pl.pallas_call(
        body, grid=(G,),
        in_specs=[_head_spec(T, c, HP), _head_spec(T, c + G, HP), _head_spec(T, c + 2 * G, HP), bspec, full],
        out_specs=[full, full, full, bspec],
        out_shape=[shp, shp, shp, jax.ShapeDtypeStruct((n_heads, PAIR, PBAND), F32)],
        scratch_shapes=[pltpu.VMEM((PAD + T, HP * HEAD_DIM), BF16)] * 2 + [pltpu.VMEM((PAD + T, HP * HEAD_DIM), F32)] * 2,
        compiler_params=_params("parallel"), name="ca_bwd")(proj, proj, proj, bias, dy)


def _local_step(x, p, target, comm, g):
    T, D = x.shape
    H = g["rel_bias"].shape[0]
    W = H * HEAD_DIM
    nb_in = comm.shapes["w_in_a"][2]
    nb_ff = comm.shapes["w_ffn_in"][2]
    nb_o = comm.shapes["w_sb_out"][2]
    nb_p = comm.shapes["w_ple_in"][2]
    tm = min(T, 1024)
    tn = min(D, 1024)
    gate_col = 6 * W // D

    h1 = _norm_fwd(x, g["g_mix"], "norm1")
    comm.stage("norm1", h1)
    proj = _mm(h1, comm.weight("w_in_a", h1), mode="nn", tm=tm, tn=nb_in, tk=D // 2, out_dtype=F32, b_blocked=True,
               after=comm.pending(), name="mm_in_a")
    comm.stage("mm_in_a", proj)
    proj = _mm(h1, comm.weight("w_in_b", proj), mode="nn", tm=tm, tn=nb_in, tk=D // 2, out_dtype=F32, b_blocked=True, a_cols=(1, 1),
               res=proj, after=comm.pending(), name="mm_in")
    comm.stage("mm_in", proj)
    y_sb = _sb_fwd(proj, H, comm.pending())
    bias = _bias_expand(g["rel_bias"])
    y_ca = _ca_fwd(proj, bias, H, 3 * H)
    comm.stage("attention", y_sb, y_ca)
    a_sb, a_ca, merged = _mm_merge(y_sb, y_ca, comm.weight("w_sb_out", y_ca), comm.weight("w_ca_out"), proj, gate_col, min(T, 512), tn, comm.pending())
    x2, h2 = _mm_fused(merged, comm.weight("w_mix_out"), [(x, 0), g["g_ffn"]], _residual_norm, [F32, BF16],
                       mode="nn", tm=min(T, 512), tn=D, tk=D, name="mm_mix")
    gate, up, act = _mm_swiglu(h2, comm.weight("w_ffn_in", h2), min(T, 512))
    comm.stage("mm_ffn_in", act)
    F = act.shape[1]
    tkf = F // 2 if F % 256 == 0 else F
    x3 = _mm(act, comm.weight("w_ffn_out", act), mode="nn", tm=tm, tn=tn, tk=tkf, out_dtype=F32, res=x2, after=comm.pending(), name="mm_ffn_out")
    h3 = _norm_fwd(x3, g["g_ple"], "norm3")
    pb = _cast_bf16(p, "cast_p")
    P = p.shape[1]
    pe = _mm(pb, comm.weight("w_ple_in"), mode="nn", tm=tm, tn=tn, tk=P, out_dtype=F32, b_blocked=True, name="mm_ple_in")
    dx4, dpe, dzg, loss, dg_final = _mm_fused(
        h3, comm.weight("w_ple_gate"), [(x3, 0), (pe, 0), (target, 0), g["g_final"]], _tail, [F32, BF16, BF16],
        mode="nn", tm=min(T, 256), tn=D, tk=D, sums=[(1, 128), (1, D)], name="mm_ple_gate")

    tw = min(D, 1024)
    DW = BF16
    comm.grad("w_ple_in", _mm(pb, dpe, mode="tn", tm=P, tn=nb_p, tk=T, out_dtype=DW, out_block=nb_p, name="mm_d_ple_in"))
    comm.grad("w_ple_gate", _mm(h3, dzg, mode="tn", tm=tw, tn=tn, tk=T, out_dtype=DW, name="mm_d_ple_gate"))
    dx3, dx3b, dg_ple = _mm_fused(dzg, comm.weight("w_ple_gate"), [(x3, 0), (dx4, 0), g["g_ple"]], _residual_norm_bwd, [F32, BF16],
                                  mode="nt", tm=min(T, 256), tn=D, tk=D, sums=[(1, D)], after=comm.pending(), name="mm_dh3")
    comm.grad("w_ffn_out", _mm(act, dx3b, mode="tn", tm=F // 4, tn=tn, tk=T, out_dtype=DW, name="mm_d_ffn_out"))
    dgate, dup = _mm_fused(dx3b, comm.weight("w_ffn_out"), [(gate, 0), (up, 0)], _swiglu_bwd, [BF16, BF16],
                           mode="nt", tm=min(T, 512), tn=nb_ff, tk=D, after=comm.pending(), b_outer=True, name="mm_dact")
    half = comm.shapes["w_ffn_in"][0] // 2
    d_ffn_in = _mm(h2, dgate, mode="tn", tm=tw, tn=nb_ff, tk=T, out_dtype=DW, out_block=nb_ff, o_count=2 * half, name="mm_d_ffn_in_gate")
    comm.grad("w_ffn_in", _mm(h2, dup, mode="tn", tm=tw, tn=nb_ff, tk=T, out_dtype=DW, out_block=nb_ff, o_first=half, o_count=2 * half,
                              into=d_ffn_in, name="mm_d_ffn_in"))
    dh2 = _mm(dgate, comm.weight("w_ffn_in"), mode="nt", tm=tm, tn=D, tk=nb_ff, out_dtype=F32, b_blocked=True, b_count=half,
              after=comm.pending(), name="mm_dh2_gate")
    dh2 = _mm(dup, comm.weight("w_ffn_in"), mode="nt", tm=min(T, 512), tn=D, tk=nb_ff, out_dtype=F32, b_blocked=True, b_first=half, b_count=half,
              res=dh2, name="mm_dh2")
    dx2, dx2b, dg_ffn = _norm_bwd(dh2, x2, g["g_ffn"], dx3, "norm2_bwd")
    per = D // tn
    da_sb, da_ca, dgate_sb, dgate_ca = _mm_fused(
        dx2b, comm.weight("w_mix_out"), [(proj, gate_col * per), (proj, (gate_col + 1) * per), (a_sb, 0), (a_ca, 0)], _merge_bwd, [BF16] * 4,
        mode="nt", tm=min(T, 512), tn=tn, tk=D, name="mm_dmerged")
    comm.grad("w_mix_out", _mm(merged, dx2b, mode="tn", tm=tw, tn=tn, tk=T, out_dtype=DW, name="mm_d_mix"))
    comm.grad("w_sb_out", _mm(y_sb, da_sb, mode="tn", tm=min(W, 512), tn=tn, tk=T, out_dtype=DW, out_block=nb_o, name="mm_d_sb_out"))
    comm.grad("w_ca_out", _mm(y_ca, da_ca, mode="tn", tm=min(W, 512), tn=tn, tk=T, out_dtype=DW, out_block=nb_o, name="mm_d_ca_out"))
    dy_sb = _mm(da_sb, comm.weight("w_sb_out"), mode="nt", tm=tm, tn=W, tk=tn, out_dtype=BF16, b_blocked=True, after=comm.pending(), name="mm_dy_sb")
    dy_ca = _mm(da_ca, comm.weight("w_ca_out"), mode="nt", tm=tm, tn=W, tk=tn, out_dtype=BF16, b_blocked=True, name="mm_dy_ca")
    dq_sb, dk_sb, dv_sb = _sb_bwd(proj, dy_sb, H)
    dq_ca, dk_ca, dv_ca, dbias = _ca_bwd(proj, bias, dy_ca, H, 3 * H)
    d_rel = _bias_reduce(dbias)[:, :N_REL]
    dproj = jnp.concatenate([dq_sb, dk_sb, dv_sb, dq_ca, dk_ca, dv_ca, dgate_sb, dgate_ca], axis=1)
    comm.grad("w_in", _mm(h1, dproj, mode="tn", tm=tw, tn=nb_in, tk=T, out_dtype=DW, out_block=nb_in, name="mm_d_in"))
    dh1 = _mm(dproj, comm.weight("w_in_a"), mode="nt", tm=tm, tn=D // 2, tk=nb_in, out_dtype=F32, b_blocked=True, out_cols=(2, 1, 0),
              after=comm.pending(), name="mm_dh1_a")
    comm.pair_done(dh1)
    dh1 = _mm(dproj, comm.weight("w_in_b"), mode="nt", tm=tm, tn=D // 2, tk=nb_in, out_dtype=F32, b_blocked=True, out_cols=(2, 1, 1),
              into=dh1, after=comm.pending(), name="mm_dh1")
    grad_x, _, dg_mix = _norm_bwd(dh1, x, g["g_mix"], dx2, "norm1_bwd")
    small = dict(g_mix=dg_mix, g_ffn=dg_ffn, g_ple=dg_ple, g_final=dg_final, rel_bias=d_rel)
    return loss, grad_x, small


def _position():
    x, y, c = lax.axis_index("x"), lax.axis_index("y"), lax.axis_index("c")
    return x, y, c


def _block_of(px, py, pc):
    return 4 * px + 2 * py + pc


def _flip(pos, k):
    x, y, c = pos
    return (1 - x if k & 4 else x, 1 - y if k & 2 else y, 1 - c if k & 1 else c)


HBM = pl.BlockSpec(memory_space=pltpu.HBM)
SEM = pl.BlockSpec(memory_space=pltpu.SEMAPHORE)
VMEM_SPEC = pl.BlockSpec(memory_space=pltpu.VMEM)
EFFECT = pltpu.SideEffectType.DATAFLOW_SIDE_EFFECTING
TOKEN = jax.ShapeDtypeStruct((8, 128), F32)


def _hbm(a):
    return pltpu.HBM(a.shape, a.dtype)


def _landing(shape, dtype):
    return pltpu.with_memory_space_constraint(lax.empty(shape, dtype), pltpu.HBM)


def _gather_start(lands, after, name):
    n = len(lands)

    def body(*refs):
        ins = refs[:n]
        send, recv = refs[n + 1], refs[n + 2]
        token = refs[-1]
        x, y, c = _position()
        mine = _block_of(x, y, c)
        peers = [(x, y, 1 - c), (1 - x, y, c), (x, 1 - y, c), (1 - x, 1 - y, c)]
        for wi in range(n):
            for k, peer in enumerate(peers):
                pltpu.make_async_remote_copy(
                    src_ref=ins[wi].at[mine], dst_ref=ins[wi].at[mine], send_sem=send.at[4 * wi + k], recv_sem=recv.at[4 * wi + k],
                    device_id=peer, device_id_type=MESH).start()
        token[...] = jnp.zeros_like(token)

    outs = pl.pallas_call(
        body, name=name, in_specs=[HBM] * n + [ANY], out_specs=(SEM, SEM, *[HBM] * n, VMEM_SPEC),
        out_shape=(pltpu.SemaphoreType.DMA((4 * n,)), pltpu.SemaphoreType.DMA((4 * n,)), *[_hbm(a) for a in lands], TOKEN),
        input_output_aliases={i: 2 + i for i in range(n)},
        compiler_params=pltpu.CompilerParams(has_side_effects=EFFECT))(*[pltpu.with_memory_space_constraint(a, pltpu.HBM) for a in lands], after)
    return outs[0], outs[1], list(outs[2:2 + n]), outs[-1]


def _gather_forward(lands, send0, recv0, after, name):
    n = len(lands)

    def body(*refs):
        ins = refs[:n]
        send0, recv0 = refs[n], refs[n + 1]
        send1, recv1 = refs[n + 2 + len(after)], refs[n + 3 + len(after)]
        token = refs[-1]
        x, y, c = _position()
        chips = [(1 - x, y), (x, 1 - y), (1 - x, 1 - y)]
        for wi in range(n):
            for j, chip in enumerate(chips):
                rows = ins[wi].at[_block_of(*chip, c)]
                pltpu.make_async_remote_copy(
                    src_ref=rows, dst_ref=rows, send_sem=send0.at[4 * wi + 1 + j], recv_sem=recv0.at[4 * wi + 1 + j],
                    device_id=(*chip, c), device_id_type=MESH).wait_recv()
                pltpu.make_async_remote_copy(
                    src_ref=rows, dst_ref=rows, send_sem=send1.at[3 * wi + j], recv_sem=recv1.at[3 * wi + j],
                    device_id=(x, y, 1 - c), device_id_type=MESH).start()
        token[...] = jnp.zeros_like(token)

    outs = pl.pallas_call(
        body, name=name, in_specs=[HBM] * n + [SEM, SEM] + [ANY] * len(after), out_specs=(SEM, SEM, *[HBM] * n, VMEM_SPEC),
        out_shape=(pltpu.SemaphoreType.DMA((3 * n,)), pltpu.SemaphoreType.DMA((3 * n,)), *[_hbm(a) for a in lands], TOKEN),
        input_output_aliases={i: 2 + i for i in range(n)},
        compiler_params=pltpu.CompilerParams(has_side_effects=EFFECT))(*lands, send0, recv0, *after)
    return outs[0], outs[1], list(outs[2:2 + n]), outs[-1]


def _gather_wait(lands, send0, recv0, send1, recv1, after, name):
    n = len(lands)

    def body(*refs):
        ins = refs[:n]
        send0, recv0, send1, recv1 = refs[n:n + 4]
        x, y, c = _position()
        mine = _block_of(x, y, c)
        sibling = (x, y, 1 - c)
        peers = [sibling, (1 - x, y, c), (x, 1 - y, c), (1 - x, 1 - y, c)]
        chips = [(1 - x, y), (x, 1 - y), (1 - x, 1 - y)]
        for wi in range(n):
            own = ins[wi].at[mine]
            for k, peer in enumerate(peers):
                pltpu.make_async_remote_copy(src_ref=own, dst_ref=own, send_sem=send0.at[4 * wi + k], recv_sem=recv0.at[4 * wi + k],
                                             device_id=peer, device_id_type=MESH).wait_send()
            theirs = ins[wi].at[_block_of(*sibling)]
            pltpu.make_async_remote_copy(src_ref=theirs, dst_ref=theirs, send_sem=send0.at[4 * wi], recv_sem=recv0.at[4 * wi],
                                         device_id=sibling, device_id_type=MESH).wait_recv()
            for j, chip in enumerate(chips):
                sent = ins[wi].at[_block_of(*chip, c)]
                got = ins[wi].at[_block_of(*chip, 1 - c)]
                pltpu.make_async_remote_copy(src_ref=sent, dst_ref=sent, send_sem=send1.at[3 * wi + j], recv_sem=recv1.at[3 * wi + j],
                                             device_id=sibling, device_id_type=MESH).wait_send()
                pltpu.make_async_remote_copy(src_ref=got, dst_ref=got, send_sem=send1.at[3 * wi + j], recv_sem=recv1.at[3 * wi + j],
                                             device_id=sibling, device_id_type=MESH).wait_recv()

    outs = pl.pallas_call(
        body, name=name, in_specs=[HBM] * n + [SEM] * 4 + [ANY], out_specs=tuple([HBM] * n),
        out_shape=tuple(_hbm(a) for a in lands), input_output_aliases={i: i for i in range(n)},
        compiler_params=pltpu.CompilerParams(has_side_effects=EFFECT))(*lands, send0, recv0, send1, recv1, after)
    return list(outs)


def _plan_direct(me):
    return [(_block_of(*_flip(me, k)), k - 1, _flip(me, k)) for k in range(1, N_DEV)]


def _plan_sibling(me):
    x, y, c = me
    return [(_block_of(ci // 2, ci % 2, 1 - c), ci, (x, y, 1 - c)) for ci in range(4)]


def _plan_chips(me):
    x, y, c = me
    out = []
    for k in range(1, 4):
        px, py = (1 - x if k & 2 else x), (1 - y if k & 1 else y)
        out.append((2 * px + py, k - 1, (px, py, c)))
    return out


def _exchange_start(blocks, plan, name):
    n = len(blocks)
    slots = len(plan((0, 0, 0)))

    def body(*refs):
        srcs, lands = refs[:n], refs[n:2 * n]
        send, recv = refs[2 * n], refs[2 * n + 1]
        token = refs[-1]
        for wi in range(n):
            for block, slot, peer in plan(_position()):
                pltpu.make_async_remote_copy(
                    src_ref=srcs[wi].at[block], dst_ref=lands[wi].at[slot], send_sem=send.at[slots * wi + slot],
                    recv_sem=recv.at[slots * wi + slot], device_id=peer, device_id_type=MESH).start()
        token[...] = jnp.zeros_like(token)

    zones = [_landing((slots,) + b.shape[1:], b.dtype) for b in blocks]
    outs = pl.pallas_call(
        body, name=name, in_specs=[HBM] * (2 * n), out_specs=(SEM, SEM, *[HBM] * (2 * n), VMEM_SPEC),
        out_shape=(pltpu.SemaphoreType.DMA((slots * n,)), pltpu.SemaphoreType.DMA((slots * n,)), *[_hbm(a) for a in blocks],
                   *[_hbm(z) for z in zones], TOKEN),
        input_output_aliases={i: 2 + i for i in range(2 * n)},
        compiler_params=pltpu.CompilerParams(has_side_effects=EFFECT))(
            *[pltpu.with_memory_space_constraint(b, pltpu.HBM) for b in blocks], *zones)
    return outs[0], outs[1], list(outs[2:2 + n]), list(outs[2 + n:2 + 2 * n]), outs[-1]


def _exchange_wait(groups, plan, after, name):
    flat, counts = [], []
    for send, recv, blocks, zones in groups:
        flat += [*blocks, *zones, send, recv]
        counts.append(len(blocks))
    slots = len(plan((0, 0, 0)))

    def body(*refs):
        pos = 0
        for n in counts:
            srcs, lands = refs[pos:pos + n], refs[pos + n:pos + 2 * n]
            send, recv = refs[pos + 2 * n], refs[pos + 2 * n + 1]
            pos += 2 * n + 2
            for wi in range(n):
                for block, slot, peer in plan(_position()):
                    cp = pltpu.make_async_remote_copy(
                        src_ref=srcs[wi].at[block], dst_ref=lands[wi].at[slot], send_sem=send.at[slots * wi + slot],
                        recv_sem=recv.at[slots * wi + slot], device_id=peer, device_id_type=MESH)
                    cp.wait_send()
                    cp.wait_recv()

    in_specs, out_specs, out_shape, aliases = [], [], [], {}
    i = 0
    for n, (send, recv, blocks, zones) in zip(counts, groups):
        for a in (*blocks, *zones):
            aliases[i] = len(out_shape)
            in_specs.append(HBM)
            out_specs.append(HBM)
            out_shape.append(_hbm(a))
            i += 1
        in_specs += [SEM, SEM]
        i += 2
    outs = pl.pallas_call(
        body, name=name, in_specs=in_specs + [ANY], out_specs=tuple(out_specs), out_shape=tuple(out_shape),
        input_output_aliases=aliases, compiler_params=pltpu.CompilerParams(has_side_effects=EFFECT))(*flat, after)
    res, pos = [], 0
    for n in counts:
        res.append((list(outs[pos:pos + n]), list(outs[pos + n:pos + 2 * n])))
        pos += 2 * n
    return res


def _sibling_sum(blocks, zone, core, name):
    _, R, C = zone.shape
    rt = next(r for r in (R, R // 2, R // 4, 128, 64) if R % r == 0 and r % 16 == 0 and r * C <= 4 * 1024 * 1024)

    def body(core_ref, own_ref, z_ref, o_ref):
        o_ref[...] = (own_ref[...].astype(F32) + z_ref[...].astype(F32)).astype(o_ref.dtype)

    grid_spec = pltpu.PrefetchScalarGridSpec(
        num_scalar_prefetch=1, grid=(4, R // rt),
        in_specs=[pl.BlockSpec((None, rt, C), lambda ci, i, core_ref: (2 * ci + core_ref[0], i, 0)),
                  pl.BlockSpec((None, rt, C), lambda ci, i, core_ref: (ci, i, 0))],
        out_specs=pl.BlockSpec((None, rt, C), lambda ci, i, core_ref: (ci, i, 0)))
    return pl.pallas_call(body, grid_spec=grid_spec, out_shape=jax.ShapeDtypeStruct(zone.shape, zone.dtype),
                          compiler_params=_params("parallel", "parallel"), name=name)(core, blocks, zone)


def _adamw(w, g, m, v):
    m = ADAM_B1 * m + (1.0 - ADAM_B1) * g
    v = ADAM_B2 * v + (1.0 - ADAM_B2) * (g * g)
    m_hat = m / (1.0 - ADAM_B1 ** ADAM_STEP)
    v_hat = v / (1.0 - ADAM_B2 ** ADAM_STEP)
    delta = -ADAM_LR * (m_hat / (jnp.sqrt(v_hat) + ADAM_EPS) + ADAM_WD * w)
    return delta, m, v


def _reduce_adamw(blocks, zone, mine, w, m, v, name):
    R, C = w.shape
    rt = next(r for r in (256, 128, 64) if R % r == 0 and r * C <= 512 * 1024)

    def body(mine_ref, own_ref, z_ref, w_ref, m_ref, v_ref, g_out, d_out, m_out, v_out):
        g = own_ref[...].astype(F32)
        for s in range(zone.shape[0]):
            g = g + z_ref[s].astype(F32)
        delta, m2, v2 = _adamw(w_ref[...], g, m_ref[...], v_ref[...])
        g_out[...] = g
        d_out[...] = delta
        m_out[...] = m2
        v_out[...] = v2

    spec = pl.BlockSpec((rt, C), lambda i, mine_ref: (i, 0))
    grid_spec = pltpu.PrefetchScalarGridSpec(
        num_scalar_prefetch=1, grid=(R // rt,),
        in_specs=[pl.BlockSpec((None, rt, C), lambda i, mine_ref: (mine_ref[0], i, 0)),
                  pl.BlockSpec((zone.shape[0], rt, C), lambda i, mine_ref: (0, i, 0)), spec, spec, spec],
        out_specs=[spec] * 4)
    return pl.pallas_call(body, grid_spec=grid_spec, out_shape=[jax.ShapeDtypeStruct((R, C), F32)] * 4,
                          compiler_params=_params("parallel"), name=name)(mine, blocks, zone, w, m, v)


def _small_step(part, w, m, v, after):
    R, C = part.shape

    def body(part_ref, w_ref, m_ref, v_ref, *rest):
        g_out, d_out, m_out, v_out, gath, send, recv = rest[len(after):]
        me = _position()
        gath[_block_of(*me)] = part_ref[...]

        def copy(k, slot):
            return pltpu.make_async_remote_copy(
                src_ref=part_ref, dst_ref=gath.at[slot], send_sem=send.at[k - 1], recv_sem=recv.at[k - 1],
                device_id=_flip(me, k), device_id_type=MESH)

        sent = [copy(k, _block_of(*me)) for k in range(1, N_DEV)]
        for cp in sent:
            cp.start()
        for k in range(1, N_DEV):
            copy(k, _block_of(*_flip(me, k))).wait_recv()
        for cp in sent:
            cp.wait_send()
        g = gath[0]
        for s in range(1, N_DEV):
            g = g + gath[s]
        delta, m2, v2 = _adamw(w_ref[...], g, m_ref[...], v_ref[...])
        g_out[...] = g
        d_out[...] = delta
        m_out[...] = m2
        v_out[...] = v2

    vm = pl.BlockSpec(memory_space=pltpu.VMEM)
    return pl.pallas_call(
        body, in_specs=[vm] * 4 + [ANY] * len(after), out_specs=[vm] * 4, out_shape=[jax.ShapeDtypeStruct((R, C), F32)] * 4,
        scratch_shapes=[pltpu.VMEM((N_DEV, R, C), F32), pltpu.SemaphoreType.DMA((7,)), pltpu.SemaphoreType.DMA((7,))],
        name="small_step")(part, w, m, v, *after)


COLUMN_SHARDED = ("w_in", "w_sb_out", "w_ca_out", "w_ffn_in", "w_ple_in")
ROW_SHARDED = ("w_mix_out", "w_ffn_out", "w_ple_gate")
BIG = COLUMN_SHARDED + ROW_SHARDED
SMALL = ("g_mix", "g_ffn", "g_ple", "g_final", "rel_bias")
WEIGHTS = ("w_in", "w_sb_out", "w_ca_out", "w_mix_out", "rel_bias", "g_mix", "g_ffn", "g_ple", "g_final",
           "w_ffn_in", "w_ffn_out", "w_ple_in", "w_ple_gate")


def _pack_small(t, D):
    rows = [t[n].reshape(1, D) for n in SMALL[:4]]
    rb = t["rel_bias"].reshape(1, -1)
    rows.append(jnp.pad(rb, ((0, 0), (0, D - rb.shape[1]))))
    return jnp.concatenate(rows + [jnp.zeros((8 - len(rows), D), F32)], axis=0)


def _unpack_small(a, like):
    out = {n: a[i].reshape(like[n].shape) for i, n in enumerate(SMALL[:4])}
    out["rel_bias"] = a[4, :like["rel_bias"].size].reshape(like["rel_bias"].shape)
    return out


GATHER_GROUPS = (("w_in_a",), ("w_in_b",), ("w_sb_out", "w_ca_out", "w_mix_out"), ("w_ffn_in",), ("w_ffn_out", "w_ple_gate", "w_ple_in"))
FORWARD_AFTER = ("norm1", "mm_in_a", "mm_in", "attention", "mm_ffn_in")
GRAD_GROUPS = (("w_ple_in", "w_ple_gate"), ("w_ffn_out",), ("w_ffn_in",), ("w_mix_out", "w_sb_out", "w_ca_out"), ("w_in",))


class _Exchange:
    def __init__(self, shards):
        me = _position()
        self.mine = _block_of(*me)
        self.chip = jnp.reshape(2 * me[0] + me[1], (1,)).astype(jnp.int32)
        self.core = jnp.reshape(me[2], (1,)).astype(jnp.int32)
        self.device = jnp.reshape(self.mine, (1,)).astype(jnp.int32)
        self.shapes = {n: ((N_DEV * s.shape[0], s.shape[1]) if n in ROW_SHARDED else (N_DEV,) + s.shape) for n, s in shards.items()}
        self.tokens = []
        self.ready = {}
        self.gathers = []
        for gi, names in enumerate(GATHER_GROUPS):
            lands = [lax.dynamic_update_slice(lax.empty((N_DEV,) + shards[n].shape, BF16), shards[n][None], (self.mine, 0, 0))
                     for n in names]
            behind = self.tokens[-1] if self.tokens else shards[names[0]]
            send0, recv0, lands, token = _gather_start(lands, behind, f"gather_start_{gi}")
            self.tokens.append(token)
            self.gathers.append(dict(names=names, lands=lands, sems=(send0, recv0), token=token))
        self.grads = {}
        self.exchanges = []

    def pending(self):
        tokens, self.tokens = self.tokens, []
        return tokens

    def stage(self, tag, *made):
        gi = FORWARD_AFTER.index(tag)
        gth = self.gathers[gi]
        send1, recv1, lands, token = _gather_forward(gth["lands"], *gth["sems"], made + tuple(self.tokens), f"gather_forward_{gi}")
        gth.update(lands=lands, sems=gth["sems"] + (send1, recv1), token=token)
        self.tokens.append(token)

    def weight(self, name, after=None):
        if name not in self.ready:
            gi = next(i for i, names in enumerate(GATHER_GROUPS) if name in names)
            gth = self.gathers[gi]
            for n, a in zip(gth["names"], _gather_wait(gth["lands"], *gth["sems"], gth["token"] if after is None else after, f"gather_wait_{gi}")):
                self.ready[n] = a.reshape(self.shapes[n])
        return self.ready[name]

    def grad(self, name, blocks):
        self.grads[name] = blocks if name in COLUMN_SHARDED else blocks.reshape((N_DEV, -1, blocks.shape[-1]))
        names = next(names for names in GRAD_GROUPS if name in names)
        if not all(n in self.grads for n in names):
            return
        blocks = [self.grads[n] for n in names]
        if names == GRAD_GROUPS[-1]:
            send, recv, blocks, zones, token = _exchange_start(blocks, _plan_sibling, "pair_start_" + names[0])
            self.pair = (send, recv, blocks, zones)
        else:
            send, recv, blocks, zones, token = _exchange_start(blocks, _plan_direct, "exchange_start_" + names[0])
            self.exchanges.append(dict(names=names, state=(send, recv, blocks, zones), plan=_plan_direct, own=self.device))
        self.tokens.append(token)

    def pair_done(self, after):
        names = GRAD_GROUPS[-1]
        (blocks, zones), = _exchange_wait([self.pair], _plan_sibling, after, "pair_wait_" + names[0])
        blocks = [_sibling_sum(b, z, self.core, "pair_sum_" + n) for n, b, z in zip(names, blocks, zones)]
        send, recv, blocks, zones, token = _exchange_start(blocks, _plan_chips, "exchange_start_" + names[0])
        self.exchanges.append(dict(names=names, state=(send, recv, blocks, zones), plan=_plan_chips, own=self.chip))
        self.tokens.append(token)

    def collect(self, which, after, name):
        sel = [e for e in self.exchanges if GRAD_GROUPS.index(e["names"]) in which]
        out = {}
        for e, (blocks, zones) in zip(sel, _exchange_wait([e["state"] for e in sel], sel[0]["plan"], after, name)):
            out.update({n: (b, e["own"], z) for n, b, z in zip(e["names"], blocks, zones)})
        return out


def kernel(x, p, w_in, w_sb_out, w_ca_out, w_mix_out, rel_bias, g_mix, g_ffn, g_ple, g_final, w_ffn_in, w_ffn_out, w_ple_in, w_ple_gate, loss_target, m_w_in, m_w_sb_out, m_w_ca_out, m_w_mix_out, m_rel_bias, m_g_mix, m_g_ffn, m_g_ple, m_g_final, m_w_ffn_in, m_w_ffn_out, m_w_ple_in, m_w_ple_gate, v_w_in, v_w_sb_out, v_w_ca_out, v_w_mix_out, v_rel_bias, v_g_mix, v_g_ffn, v_g_ple, v_g_final, v_w_ffn_in, v_w_ffn_out, v_w_ple_in, v_w_ple_gate):
    wts = dict(w_in=w_in, w_sb_out=w_sb_out, w_ca_out=w_ca_out, w_mix_out=w_mix_out, rel_bias=rel_bias, g_mix=g_mix, g_ffn=g_ffn,
               g_ple=g_ple, g_final=g_final, w_ffn_in=w_ffn_in, w_ffn_out=w_ffn_out, w_ple_in=w_ple_in, w_ple_gate=w_ple_gate)
    mom = dict(w_in=m_w_in, w_sb_out=m_w_sb_out, w_ca_out=m_w_ca_out, w_mix_out=m_w_mix_out, rel_bias=m_rel_bias, g_mix=m_g_mix,
               g_ffn=m_g_ffn, g_ple=m_g_ple, g_final=m_g_final, w_ffn_in=m_w_ffn_in, w_ffn_out=m_w_ffn_out, w_ple_in=m_w_ple_in,
               w_ple_gate=m_w_ple_gate)
    var = dict(w_in=v_w_in, w_sb_out=v_w_sb_out, w_ca_out=v_w_ca_out, w_mix_out=v_w_mix_out, rel_bias=v_rel_bias, g_mix=v_g_mix,
               g_ffn=v_g_ffn, g_ple=v_g_ple, g_final=v_g_final, w_ffn_in=v_w_ffn_in, w_ffn_out=v_w_ffn_out, w_ple_in=v_w_ple_in,
               w_ple_gate=v_w_ple_gate)
    T, D = x.shape[1], x.shape[2]
    shard = {n: wts[n].reshape(wts[n].shape[-2:]) for n in BIG}
    bf = {n: _cast_bf16(shard[n], "cast_" + n) for n in BIG}
    half = bf["w_in"].shape[0] // 2
    bf["w_in_a"], bf["w_in_b"] = bf["w_in"][:half], bf.pop("w_in")[half:]
    comm = _Exchange(bf)
    g = dict(g_mix=g_mix.reshape(1, D), g_ffn=g_ffn.reshape(1, D), g_ple=g_ple.reshape(1, D), g_final=g_final.reshape(1, D),
             rel_bias=rel_bias.reshape(rel_bias.shape[-2:]))

    loss, grad_x, dsmall = _local_step(x.reshape(T, D), p.reshape(T, -1), loss_target.reshape(T, D), comm, g)
    loss = lax.psum(loss[0, 0], ("x", "y", "c"))

    grad, delta, new_m, new_v = {}, {}, {}, {}

    def update(parts):
        done = []
        for n, (blocks, own, zone) in parts.items():
            outs = _reduce_adamw(blocks, zone, own, shard[n], mom[n].reshape(shard[n].shape), var[n].reshape(shard[n].shape), "adamw_" + n)
            grad[n], delta[n], new_m[n], new_v[n] = [o.reshape(wts[n].shape) for o in outs]
            done.append(outs[0])
        return done

    done = update(comm.collect(range(len(GRAD_GROUPS) - 1), grad_x, "exchange_wait_rest"))
    outs = _small_step(_pack_small(dsmall, D), _pack_small(wts, D), _pack_small(mom, D), _pack_small(var, D), done)
    for dst, a in zip((grad, delta, new_m, new_v), outs):
        dst.update(_unpack_small(a, wts))
    update(comm.collect([len(GRAD_GROUPS) - 1], outs[0], "exchange_wait_w_in"))

    return (loss, grad_x.reshape(x.shape), *[grad[n] for n in WEIGHTS], *[delta[n] for n in WEIGHTS],
            *[new_m[n] for n in WEIGHTS], *[new_v[n] for n in WEIGHTS])
```

```python
import functools

import jax
import jax.numpy as jnp
from jax import lax
from jax.experimental import pallas as pl
from jax.experimental.pallas import tpu as pltpu

F32, BF16 = jnp.float32, jnp.bfloat16

N_DEV = 8
HEAD_DIM = 128
CHUNK = 64
LEFT_CHUNKS = 8
REL_CLIP = 128
N_REL = REL_CLIP + CHUNK
PAIR = 2 * CHUNK
PBAND = (LEFT_CHUNKS + 2) * CHUNK
CA_PAIRS = 2
CA_ROWS = CA_PAIRS * PAIR
CA_BAND = PBAND + CA_ROWS - PAIR
PAD = LEFT_CHUNKS * CHUNK
SB_BLOCK = 256
ROWS = 256
EPS = 1e-6
NEG = -1e30
SCALE = HEAD_DIM ** -0.5
VMEM_LIMIT_BYTES = 56 * 1024 * 1024

ADAM_LR, ADAM_B1, ADAM_B2, ADAM_EPS, ADAM_WD, ADAM_STEP = 0.001, 0.9, 0.999, 1e-08, 0.01, 10

ANY = pl.BlockSpec(memory_space=pl.ANY)
NN = (((1,), (0,)), ((), ()))
NT = (((1,), (1,)), ((), ()))
TN = (((0,), (0,)), ((), ()))
MESH = pl.DeviceIdType.MESH


def _params(*sem):
    return pltpu.CompilerParams(dimension_semantics=sem or None, vmem_limit_bytes=VMEM_LIMIT_BYTES)


def _dot(a, b, dims=NN):
    return lax.dot_general(a, b, dims, preferred_element_type=F32)


def _mm(a, b, *, mode, tm, tn, tk, out_dtype, name, b_blocked=False, out_block=None, res=None, after=(), a_cols=(1, 0), out_cols=(1, 1, 0), into=None, b_first=0, b_count=None, o_first=0, o_count=None):
    bg = og = 1
    if mode == "nn":
        M, K = a.shape
        a_spec = pl.BlockSpec((tm, tk), lambda i, j, k: (i, k))
        if b_blocked:
            G, _, nb = b.shape
            N = G * nb
            if tn > nb:
                bg = tn // nb
                assert tn % nb == 0
                b_spec = pl.BlockSpec((bg, tk, nb), lambda i, j, k: (j, k, 0))
            else:
                per = nb // tn
                assert nb % tn == 0
                b_spec = pl.BlockSpec((None, tk, tn), lambda i, j, k: (j // per, k, j % per))
        else:
            N = b.shape[1]
            b_spec = pl.BlockSpec((tk, tn), lambda i, j, k: (k, j))
        dims = NN
    elif mode == "nt":
        M, K = a.shape
        a_spec = pl.BlockSpec((tm, tk), lambda i, j, k: (i, k))
        if b_blocked:
            G, N, nb = b.shape
            K = (b_count or G) * nb
            assert b_first == 0 or tk == nb
            if tk > nb:
                bg = tk // nb
                assert tk % nb == 0
                b_spec = pl.BlockSpec((bg, tn, nb), lambda i, j, k: (k, j, 0))
            else:
                per = nb // tk
                assert nb % tk == 0
                b_spec = pl.BlockSpec((None, tn, tk), lambda i, j, k: (b_first + k // per, j, k % per))
        else:
            N = b.shape[0]
            b_spec = pl.BlockSpec((tn, tk), lambda i, j, k: (j, k))
        dims = NT
    else:
        K, M = a.shape
        N = b.shape[1]
        a_spec = pl.BlockSpec((tk, tm), lambda i, j, k: (k, i))
        b_spec = pl.BlockSpec((tk, tn), lambda i, j, k: (k, j))
        dims = TN
    if mode != "tn":
        if mode == "nn":
            K = b.shape[-2]
        elif not b_blocked:
            K = b.shape[1]
        a_spec = pl.BlockSpec((tm, tk), lambda i, j, k: (i, k * a_cols[0] + a_cols[1]))
    assert M % tm == 0 and N % tn == 0 and K % tk == 0, (name, M, N, K, tm, tn, tk)
    nk = K // tk
    if out_block is None:
        out_shape = jax.ShapeDtypeStruct((M, N * out_cols[0]), out_dtype)
        o_spec = pl.BlockSpec((tm, tn), lambda i, j, k: (i, j * out_cols[1] + out_cols[2]))
    else:
        out_shape = jax.ShapeDtypeStruct((o_count or N // out_block, M, out_block), out_dtype)
        if tn > out_block:
            og = tn // out_block
            assert tn % out_block == 0 and o_first % og == 0
            o_spec = pl.BlockSpec((og, tm, out_block), lambda i, j, k: (o_first // og + j, i, 0))
        else:
            per_o = out_block // tn
            assert out_block % tn == 0
            o_spec = pl.BlockSpec((None, tm, tn), lambda i, j, k: (o_first + j // per_o, i, j % per_o))
    in_specs = [a_spec, b_spec]
    args = [a, b]
    if res is not None:
        in_specs.append(pl.BlockSpec((tm, tn), lambda i, j, k: (i, j * out_cols[1] + out_cols[2])))
        args.append(res)
    n_in = len(args) + len(after) + (into is not None)

    def product(a_ref, b_ref):
        if bg == 1:
            return _dot(a_ref[...], b_ref[...], dims)
        nb = b_ref.shape[2]
        if mode == "nn":
            return jnp.concatenate([_dot(a_ref[...], b_ref[g], dims) for g in range(bg)], axis=1)
        return sum(_dot(a_ref[:, g * nb:(g + 1) * nb], b_ref[g], dims) for g in range(bg))

    def body(*refs):
        a_ref, b_ref = refs[0], refs[1]
        r_ref = refs[2] if res is not None else None
        o_ref = refs[n_in]

        def finish(acc):
            if r_ref is not None:
                acc = acc + r_ref[...]
            if og == 1:
                o_ref[...] = acc.astype(o_ref.dtype)
            else:
                for g in range(og):
                    o_ref[g] = acc[:, g * out_block:(g + 1) * out_block].astype(o_ref.dtype)

        if nk == 1:
            finish(product(a_ref, b_ref))
        else:
            acc_ref = refs[-1]
            k = pl.program_id(2)

            @pl.when(k == 0)
            def _():
                acc_ref[...] = jnp.zeros_like(acc_ref)

            acc_ref[...] += product(a_ref, b_ref)

            @pl.when(k == nk - 1)
            def _():
                finish(acc_ref[...])

    return pl.pallas_call(
        body, grid=(M // tm, N // tn, nk), in_specs=in_specs + [ANY] * (n_in - len(args)), out_specs=o_spec, out_shape=out_shape,
        scratch_shapes=[] if nk == 1 else [pltpu.VMEM((tm, tn), F32)], input_output_aliases={} if into is None else {n_in - 1: 0},
        compiler_params=_params("parallel", "parallel", "arbitrary"), name=name)(*args, *after, *(() if into is None else (into,)))


def _mm_fused(a, b, tiles, fn, outs, *, mode, tm, tn, tk, name, sums=(), after=(), b_outer=False):
    M, K = a.shape
    N = b.shape[1] if mode == "nn" else b.shape[0]
    nk = K // tk
    assert M % tm == 0 and N % tn == 0 and K % tk == 0 and (not sums or tn == N)
    def at(f):
        return (lambda j, i, k: f(i, j, k)) if b_outer else f

    b_spec = pl.BlockSpec((tk, tn), at(lambda i, j, k: (k, j))) if mode == "nn" else pl.BlockSpec((tn, tk), at(lambda i, j, k: (j, k)))
    in_specs = [pl.BlockSpec((tm, tk), at(lambda i, j, k: (i, k))), b_spec]
    args = [a, b]
    for t in tiles:
        if isinstance(t, tuple):
            arr, off = t
            in_specs.append(pl.BlockSpec((tm, tn), at(lambda i, j, k, off=off: (i, off + j))))
        else:
            arr = t
            in_specs.append(pl.BlockSpec((1, tn), at(lambda i, j, k: (0, j))))
        args.append(arr)
    n_in = len(args) + len(after)
    n_out = len(outs) + len(sums)

    def body(*refs):
        a_ref, b_ref = refs[0], refs[1]
        t_refs = refs[2:2 + len(tiles)]
        o_refs = refs[n_in:n_in + n_out]

        def finish(acc):
            res = fn(acc, *[t[...] for t in t_refs])
            for o_ref, r in zip(o_refs[:len(outs)], res):
                o_ref[...] = r.astype(o_ref.dtype)
            if sums:
                @pl.when(pl.program_id(0) == 0)
                def _():
                    for o_ref in o_refs[len(outs):]:
                        o_ref[...] = jnp.zeros_like(o_ref)

                for o_ref, r in zip(o_refs[len(outs):], res[len(outs):]):
                    o_ref[...] += jnp.broadcast_to(r, o_ref.shape)

        if nk == 1:
            finish(_dot(a_ref[...], b_ref[...], NN if mode == "nn" else NT))
        else:
            acc_ref = refs[-1]
            k = pl.program_id(2)

            @pl.when(k == 0)
            def _():
                acc_ref[...] = jnp.zeros_like(acc_ref)

            acc_ref[...] += _dot(a_ref[...], b_ref[...], NN if mode == "nn" else NT)

            @pl.when(k == nk - 1)
            def _():
                finish(acc_ref[...])

    assert not (b_outer and sums)
    o_spec = pl.BlockSpec((tm, tn), at(lambda i, j, k: (i, j)))
    return pl.pallas_call(
        body, grid=(N // tn, M // tm, nk) if b_outer else (M // tm, N // tn, nk), in_specs=in_specs + [ANY] * len(after),
        out_specs=[o_spec] * len(outs) + [pl.BlockSpec(sh, lambda i, j, k: (0, 0)) for sh in sums],
        out_shape=[jax.ShapeDtypeStruct((M, N), dt) for dt in outs] + [jax.ShapeDtypeStruct(sh, F32) for sh in sums],
        scratch_shapes=[] if nk == 1 else [pltpu.VMEM((tm, tn), F32)],
        compiler_params=_params("arbitrary" if sums else "parallel", "parallel", "arbitrary"), name=name)(*args, *after)


def _row_spec(d, col=0):
    return pl.BlockSpec((ROWS, d), lambda i: (i, col))


def _vec_spec(d):
    return pl.BlockSpec((1, d), lambda i: (0, 0))


def _rms(x):
    return lax.rsqrt(jnp.mean(x * x, axis=-1, keepdims=True) + EPS)


def _norm_fwd(x, g, name):
    T, D = x.shape

    def body(x_ref, g_ref, h_ref):
        xv = x_ref[...]
        h_ref[...] = (xv * _rms(xv) * g_ref[...]).astype(BF16)

    return pl.pallas_call(body, grid=(T // ROWS,), in_specs=[_row_spec(D), _vec_spec(D)], out_specs=_row_spec(D),
                          out_shape=jax.ShapeDtypeStruct((T, D), BF16), compiler_params=_params("parallel"), name=name)(x, g)


def _residual_norm(y, x, g):
    x = x + y
    return x, x * _rms(x) * g


def _norm_bwd_math(dh, xv, gv):
    r = _rms(xv)
    xhat = xv * r
    dxhat = dh * gv
    dx = r * (dxhat - xhat * jnp.mean(dxhat * xhat, axis=-1, keepdims=True))
    dg = jnp.sum(dh * xhat, axis=0, keepdims=True)
    return dx, dg


def _residual_norm_bwd(dh, x, dres, g):
    dx, dg = _norm_bwd_math(dh, x, g)
    dx = dx + dres
    return dx, dx, dg


def _norm_bwd(dh, x, g, dres, name):
    T, D = x.shape

    def body(dh_ref, x_ref, g_ref, dres_ref, dx_ref, dxb_ref, dg_ref):
        dx, dg = _norm_bwd_math(dh_ref[...], x_ref[...], g_ref[...])
        dx = dx + dres_ref[...]
        dx_ref[...] = dx
        dxb_ref[...] = dx.astype(BF16)

        @pl.when(pl.program_id(0) == 0)
        def _():
            dg_ref[...] = jnp.zeros_like(dg_ref)

        dg_ref[...] += dg

    return pl.pallas_call(
        body, grid=(T // ROWS,), in_specs=[_row_spec(D), _row_spec(D), _vec_spec(D), _row_spec(D)],
        out_specs=[_row_spec(D), _row_spec(D), _vec_spec(D)],
        out_shape=[jax.ShapeDtypeStruct((T, D), F32), jax.ShapeDtypeStruct((T, D), BF16), jax.ShapeDtypeStruct((1, D), F32)],
        compiler_params=_params("arbitrary"), name=name)(dh, x, g, dres)


def _mm_merge(y_sb, y_ca, w_sb, w_ca, proj, gate_col, tm, tn, after):
    T, W = y_sb.shape
    G, _, nb = w_sb.shape
    D, bg = G * nb, tn // nb
    assert tn % nb == 0 and D % tn == 0
    per = D // tn

    def body(ys_ref, yc_ref, ws_ref, wc_ref, gs_ref, gc_ref, *rest):
        as_ref, ac_ref, m_ref = rest[len(after):]
        a = jnp.concatenate([_dot(ys_ref[...], ws_ref[g]) for g in range(bg)], axis=1)
        b = jnp.concatenate([_dot(yc_ref[...], wc_ref[g]) for g in range(bg)], axis=1)
        as_ref[...] = a
        ac_ref[...] = b
        m_ref[...] = (jax.nn.sigmoid(gs_ref[...]) * a + jax.nn.sigmoid(gc_ref[...]) * b).astype(BF16)

    y_spec = pl.BlockSpec((tm, W), lambda i, j: (i, 0))
    w_spec = pl.BlockSpec((bg, W, nb), lambda i, j: (j, 0, 0))
    out = pl.BlockSpec((tm, tn), lambda i, j: (i, j))
    f32 = jax.ShapeDtypeStruct((T, D), F32)
    return pl.pallas_call(
        body, grid=(T // tm, per),
        in_specs=[y_spec, y_spec, w_spec, w_spec, pl.BlockSpec((tm, tn), lambda i, j: (i, gate_col * per + j)),
                  pl.BlockSpec((tm, tn), lambda i, j: (i, (gate_col + 1) * per + j))] + [ANY] * len(after),
        out_specs=[out, out, out], out_shape=[f32, f32, jax.ShapeDtypeStruct((T, D), BF16)],
        compiler_params=_params("parallel", "parallel"), name="mm_merge")(y_sb, y_ca, w_sb, w_ca, proj, proj, *after)


def _merge_bwd(dm, gs, gc, a, b):
    ss, sc = jax.nn.sigmoid(gs), jax.nn.sigmoid(gc)
    return dm * ss, dm * sc, dm * a * ss * (1.0 - ss), dm * b * sc * (1.0 - sc)


def _mm_swiglu(h, w, tm):
    T, D = h.shape
    G2, _, nb = w.shape
    G = G2 // 2

    def body(h_ref, wg_ref, wu_ref, g_ref, u_ref, act_ref):
        hv = h_ref[...]
        gv = _dot(hv, wg_ref[...])
        uv = _dot(hv, wu_ref[...])
        g_ref[...] = gv
        u_ref[...] = uv
        act_ref[...] = (gv * jax.nn.sigmoid(gv) * uv).astype(BF16)

    out = pl.BlockSpec((tm, nb), lambda j, i: (i, j))
    f32 = jax.ShapeDtypeStruct((T, G * nb), F32)
    return pl.pallas_call(
        body, grid=(G, T // tm),
        in_specs=[pl.BlockSpec((tm, D), lambda j, i: (i, 0)), pl.BlockSpec((None, D, nb), lambda j, i: (j, 0, 0)),
                  pl.BlockSpec((None, D, nb), lambda j, i: (j + G, 0, 0))],
        out_specs=[out, out, out], out_shape=[f32, f32, jax.ShapeDtypeStruct((T, G * nb), BF16)],
        compiler_params=_params("parallel", "parallel"), name="mm_ffn_in")(h, w, w)


def _swiglu_bwd(dact, gate, up):
    s = jax.nn.sigmoid(gate)
    return dact * up * s * (1.0 + gate * (1.0 - s)), dact * gate * s


def _tail(zg, x3, pe, target, g_final):
    D = x3.shape[-1]
    gate = jax.nn.sigmoid(zg)
    x4 = x3 + gate * pe
    err = x4 * _rms(x4) * g_final - target
    part = 0.5 * jnp.sum(jnp.mean(err * err, axis=-1, keepdims=True), axis=0, keepdims=True)
    dx, dg = _norm_bwd_math(err * (1.0 / D), x4, g_final)
    return dx, dx * gate, dx * pe * gate * (1.0 - gate), part, dg


def _cast_bf16(x, name):
    R, C = x.shape
    rows = next(r for r in (ROWS, 128, 64, 32, 16) if R % r == 0)

    def body(x_ref, o_ref):
        o_ref[...] = x_ref[...].astype(BF16)

    spec = pl.BlockSpec((rows, C), lambda i: (i, 0))
    return pl.pallas_call(body, grid=(R // rows,), in_specs=[spec], out_specs=spec, out_shape=jax.ShapeDtypeStruct((R, C), BF16),
                          compiler_params=_params("parallel"), name=name)(x)


def _head_spec(T, col0, heads=1, single=False):
    return pl.BlockSpec((T, heads * HEAD_DIM), lambda h, *_: (0, col0 + h), pipeline_mode=pl.Buffered(1) if single else None)


SB_HEADS = 4


def _triangle(n, right):
    j = lax.broadcasted_iota(jnp.int32, (n, n), 0)
    s = lax.broadcasted_iota(jnp.int32, (n, n), 1)
    return jnp.where((j > s) if right else (j < s), 1.0, 0.0).astype(BF16)


def _lane_scan(x, tri):
    hi = x.astype(BF16)
    lo = (x - hi.astype(F32)).astype(BF16)
    return _dot(hi, tri) + _dot(lo, tri)


def _head_cols(ref, rows, hh):
    return ref[rows, hh * HEAD_DIM:(hh + 1) * HEAD_DIM]


def _sb_tile(qv, kk, past, c_lk, tri):
    z = _dot(qv, kk, NT) * SCALE
    sp = jnp.log(1.0 + jnp.exp(-jnp.abs(z)))
    ls_pos = jnp.minimum(z, 0.0) - sp
    lk = jnp.minimum(-z, 0.0) - sp
    if past is not None:
        lk = jnp.where(past, lk, 0.0)
    right = c_lk + _lane_scan(lk, tri)
    a = jnp.exp(ls_pos + right)
    if past is not None:
        a = jnp.where(past, a, 0.0)
    return ls_pos, a, right[:, 0:1] + lk[:, 0:1]


SB_Q = 512
SB_HEADS_BWD = 4


def _sb_mask(d):
    B, r = SB_BLOCK, SB_Q // SB_BLOCK
    return lax.broadcasted_iota(jnp.int32, (SB_Q, B), 1) + (r - 1 - d) * B < lax.broadcasted_iota(jnp.int32, (SB_Q, B), 0)


def _sb_rows(kb):
    return pl.ds(pl.multiple_of(kb * SB_BLOCK, SB_BLOCK), SB_BLOCK)


def _sb_fwd(proj, n_heads, after=()):
    T = proj.shape[0]
    B, Q, HP = SB_BLOCK, SB_Q, SB_HEADS
    r = Q // B
    assert n_heads % HP == 0 and T % Q == 0

    def body(q_ref, k_ref, v_ref, *rest):
        y_ref = rest[-1]
        qb = pl.program_id(1)
        tri = _triangle(B, right=True)
        qv = [_head_cols(q_ref, slice(None), hh).astype(BF16) for hh in range(HP)]

        def tile(kb, carry, past):
            out = []
            for hh in range(HP):
                acc, c_lk = carry[hh]
                kk = _head_cols(k_ref, _sb_rows(kb), hh).astype(BF16)
                vv = _head_cols(v_ref, _sb_rows(kb), hh).astype(BF16)
                _, a, c_lk = _sb_tile(qv[hh], kk, past, c_lk, tri)
                out.append((acc + _dot(a.astype(BF16), vv), c_lk))
            return tuple(out)

        carry = tuple((jnp.zeros((Q, HEAD_DIM), F32), jnp.zeros((Q, 1), F32)) for _ in range(HP))
        for d in range(r):
            carry = tile(r * qb + r - 1 - d, carry, _sb_mask(d))
        res = lax.fori_loop(0, r * qb, lambda i, c: tile(r * qb - 1 - i, c, None), carry)
        for hh in range(HP):
            y_ref[:, hh * HEAD_DIM:(hh + 1) * HEAD_DIM] = res[hh][0].astype(BF16)

    blk = pl.BlockSpec((Q, HP * HEAD_DIM), lambda h, i: (i, h))
    G = n_heads // HP
    return pl.pallas_call(
        body, grid=(G, T // Q),
        in_specs=[blk, _head_spec(T, G, HP), _head_spec(T, 2 * G, HP)] + [ANY] * len(after), out_specs=blk,
        out_shape=jax.ShapeDtypeStruct((T, n_heads * HEAD_DIM), BF16),
        compiler_params=_params("parallel", "arbitrary"), name="sb_fwd")(proj, proj, proj, *after)


def _sb_bwd(proj, dy, n_heads):
    T = proj.shape[0]
    B, Q, HP = SB_BLOCK, SB_Q, SB_HEADS_BWD
    r, nq = Q // B, T // Q

    def body(q_ref, k_ref, v_ref, dy_ref, dq_ref, dk_ref, dv_ref, g_s, sig_s, dk_s, dv_s):
        qb = pl.program_id(1)

        @pl.when(qb == 0)
        def _():
            dk_s[...] = jnp.zeros_like(dk_s)
            dv_s[...] = jnp.zeros_like(dv_s)

        tri_r = _triangle(B, right=True)
        tri_l = _triangle(B, right=False)
        qv = [_head_cols(q_ref, slice(None), hh).astype(BF16) for hh in range(HP)]
        dyb = [_head_cols(dy_ref, slice(None), hh).astype(BF16) for hh in range(HP)]

        def sweep(kb, carry, past):
            out = []
            for hh in range(HP):
                kk = _head_cols(k_ref, _sb_rows(kb), hh).astype(BF16)
                vv = _head_cols(v_ref, _sb_rows(kb), hh).astype(BF16)
                ls_pos, a, c_lk = _sb_tile(qv[hh], kk, past, carry[hh], tri_r)
                g_s[hh, kb] = _dot(dyb[hh], vv, NT) * a
                sig_s[hh, kb] = jnp.exp(ls_pos).astype(BF16)
                dv_s[_sb_rows(kb), hh * HEAD_DIM:(hh + 1) * HEAD_DIM] += _dot(a.astype(BF16), dyb[hh], TN)
                out.append(c_lk)
            return tuple(out)

        carry = tuple(jnp.zeros((Q, 1), F32) for _ in range(HP))
        for d in range(r):
            carry = sweep(r * qb + r - 1 - d, carry, _sb_mask(d))
        lax.fori_loop(0, r * qb, lambda i, c: sweep(r * qb - 1 - i, c, None), carry)

        def back(kb, carry, past):
            out = []
            for hh in range(HP):
                dq, c_g = carry[hh]
                kk = _head_cols(k_ref, _sb_rows(kb), hh).astype(BF16)
                g, sig = g_s[hh, kb], sig_s[hh, kb].astype(F32)
                left = c_g + _lane_scan(g, tri_l)
                dz = g * (1.0 - sig) - left * sig
                if past is not None:
                    dz = jnp.where(past, dz, 0.0)
                dz = (dz * SCALE).astype(BF16)
                dk_s[_sb_rows(kb), hh * HEAD_DIM:(hh + 1) * HEAD_DIM] += _dot(dz, qv[hh], TN)
                out.append((dq + _dot(dz, kk), left[:, B - 1:B] + g[:, B - 1:B]))
            return tuple(out)

        init = tuple((jnp.zeros((Q, HEAD_DIM), F32), jnp.zeros((Q, 1), F32)) for _ in range(HP))
        res = lax.fori_loop(0, r * qb, lambda kb, c: back(kb, c, None), init)
        for d in reversed(range(r)):
            res = back(r * qb + r - 1 - d, res, _sb_mask(d))
        for hh in range(HP):
            dq_ref[:, hh * HEAD_DIM:(hh + 1) * HEAD_DIM] = res[hh][0].astype(BF16)

        @pl.when(qb == nq - 1)
        def _():
            dk_ref[...] = dk_s[...].astype(BF16)
            dv_ref[...] = dv_s[...].astype(BF16)

    blk = pl.BlockSpec((Q, HP * HEAD_DIM), lambda h, i: (i, h))
    G = n_heads // HP
    full = _head_spec(T, 0, HP, single=True)
    shp = jax.ShapeDtypeStruct((T, n_heads * HEAD_DIM), BF16)
    return pl.pallas_call(
        body, grid=(G, nq),
        in_specs=[blk, _head_spec(T, G, HP, single=True), _head_spec(T, 2 * G, HP, single=True), blk], out_specs=[blk, full, full],
        out_shape=[shp, shp, shp],
        scratch_shapes=[pltpu.VMEM((HP, T // B, Q, B), F32), pltpu.VMEM((HP, T // B, Q, B), BF16)] + [pltpu.VMEM((T, HP * HEAD_DIM), F32)] * 2,
        compiler_params=_params("parallel", "arbitrary"), name="sb_bwd")(proj, proj, proj, dy)


DIAGS = PBAND + PAIR


def _diag_onehot():
    d = lax.broadcasted_iota(jnp.int32, (DIAGS, 2 * PAIR), 0)
    r = lax.broadcasted_iota(jnp.int32, (DIAGS, 2 * PAIR), 1)
    return jnp.where(jnp.clip(d - PAIR - PAD, -REL_CLIP, CHUNK - 1) + REL_CLIP == r, 1.0, 0.0)


def _bias_expand(rel_bias):
    H = rel_bias.shape[0]
    table = jnp.pad(rel_bias, ((0, 0), (0, 2 * PAIR - N_REL)))

    def body(rb_ref, o_ref):
        o_ref[...] = lax.dot_general(rb_ref[...], _diag_onehot(), NT, precision=lax.Precision.HIGHEST, preferred_element_type=F32)

    per_diag = pl.pallas_call(body, out_shape=jax.ShapeDtypeStruct((H, DIAGS), F32), name="bias_expand")(table)
    flat = jnp.tile(jnp.pad(per_diag, ((0, 0), (0, 1))), (1, PAIR))[:, :PAIR * DIAGS]
    return flat.reshape(H, PAIR, DIAGS)[:, :, PAIR:]


def _bias_reduce(dbias):
    H = dbias.shape[0]
    padded = jnp.pad(dbias, ((0, 0), (0, 1), (PAIR, 0))).reshape(H, -1)
    skewed = padded[:, :PAIR * (DIAGS + 1)].reshape(H, PAIR, DIAGS + 1)[:, :, :DIAGS]

    def body(s_ref, o_ref):
        per_diag = jnp.sum(s_ref[...], axis=0, keepdims=True)
        o_ref[...] = lax.dot_general(jnp.broadcast_to(per_diag, (8, DIAGS)), _diag_onehot(), NN, precision=lax.Precision.HIGHEST,
                                     preferred_element_type=F32)[0:1]

    return pl.pallas_call(
        body, grid=(H,), in_specs=[pl.BlockSpec((None, PAIR, DIAGS), lambda h: (h, 0, 0))],
        out_specs=pl.BlockSpec((None, 1, 2 * PAIR), lambda h: (h, 0, 0)),
        out_shape=jax.ShapeDtypeStruct((H, 1, 2 * PAIR), F32), compiler_params=_params("parallel"), name="bias_reduce")(skewed)[:, 0]


CA_HEADS = 2


def _ca_mask():
    i = lax.broadcasted_iota(jnp.int32, (CA_ROWS, CA_BAND), 0)
    j = lax.broadcasted_iota(jnp.int32, (CA_ROWS, CA_BAND), 1)
    qc, kc = i // CHUNK, j // CHUNK
    return j, (kc >= qc) & (kc <= qc + LEFT_CHUNKS)


def _ca_bias(pair_bias):
    rows = []
    for q in range(CA_PAIRS):
        parts = [jnp.zeros((PAIR, q * PAIR), F32)] * (q > 0) + [pair_bias] + [jnp.zeros((PAIR, (CA_PAIRS - 1 - q) * PAIR), F32)] * (q < CA_PAIRS - 1)
        rows.append(jnp.concatenate(parts, axis=1) if len(parts) > 1 else parts[0])
    return jnp.concatenate(rows, axis=0)


def _ca_weights(pr, qp, kb, bias, j, window):
    valid = window & (pr * CA_ROWS + j >= PAD)
    z = jnp.where(valid, _dot(qp, kb, NT) * SCALE + bias, NEG)
    e = jnp.exp(z - jnp.max(z, axis=1, keepdims=True))
    return e / jnp.sum(e, axis=1, keepdims=True)


def _ca_fill(k_ref, v_ref, kpad, vpad):
    T, W = k_ref.shape
    kpad[0:PAD, :] = jnp.zeros((PAD, W), BF16)
    vpad[0:PAD, :] = jnp.zeros((PAD, W), BF16)
    kpad[PAD:PAD + T, :] = k_ref[...].astype(BF16)
    vpad[PAD:PAD + T, :] = v_ref[...].astype(BF16)


def _ca_fwd(proj, bias, n_heads, col0):
    T = proj.shape[0]
    HP = CA_HEADS
    G = n_heads // HP
    assert n_heads % HP == 0 and col0 % HP == 0

    def body(q_ref, k_ref, v_ref, b_ref, y_ref, kpad, vpad):
        _ca_fill(k_ref, v_ref, kpad, vpad)
        j, window = _ca_mask()
        bias = [_ca_bias(b_ref[hh]) for hh in range(HP)]

        def step(pr, _):
            r0 = pl.multiple_of(pr * CA_ROWS, CA_ROWS)
            for hh in range(HP):
                qp = _head_cols(q_ref, pl.ds(r0, CA_ROWS), hh).astype(BF16)
                kb = _head_cols(kpad, pl.ds(r0, CA_BAND), hh)
                vb = _head_cols(vpad, pl.ds(r0, CA_BAND), hh)
                w = _ca_weights(pr, qp, kb, bias[hh], j, window)
                y_ref[pl.ds(r0, CA_ROWS), hh * HEAD_DIM:(hh + 1) * HEAD_DIM] = _dot(w.astype(BF16), vb).astype(BF16)
            return 0

        lax.fori_loop(0, T // CA_ROWS, step, 0)

    c = col0 // HP
    return pl.pallas_call(
        body, grid=(G,),
        in_specs=[_head_spec(T, c, HP), _head_spec(T, c + G, HP), _head_spec(T, c + 2 * G, HP),
                  pl.BlockSpec((HP, PAIR, PBAND), lambda h: (h, 0, 0))],
        out_specs=_head_spec(T, 0, HP), out_shape=jax.ShapeDtypeStruct((T, n_heads * HEAD_DIM), BF16),
        scratch_shapes=[pltpu.VMEM((PAD + T, HP * HEAD_DIM), BF16)] * 2,
        compiler_params=_params("parallel"), name="ca_fwd")(proj, proj, proj, bias)


def _ca_bwd(proj, bias, dy, n_heads, col0):
    T = proj.shape[0]
    HP = CA_HEADS
    G = n_heads // HP

    def body(q_ref, k_ref, v_ref, b_ref, dy_ref, dq_ref, dk_ref, dv_ref, db_ref, kpad, vpad, dkpad, dvpad):
        _ca_fill(k_ref, v_ref, kpad, vpad)
        dkpad[...] = jnp.zeros_like(dkpad)
        dvpad[...] = jnp.zeros_like(dvpad)
        db_ref[...] = jnp.zeros_like(db_ref)
        j, window = _ca_mask()
        bias = [_ca_bias(b_ref[hh]) for hh in range(HP)]

        def step(pr, _):
            r0 = pl.multiple_of(pr * CA_ROWS, CA_ROWS)
            for hh in range(HP):
                cols = slice(hh * HEAD_DIM, (hh + 1) * HEAD_DIM)
                qp = _head_cols(q_ref, pl.ds(r0, CA_ROWS), hh).astype(BF16)
                kb = _head_cols(kpad, pl.ds(r0, CA_BAND), hh)
                vb = _head_cols(vpad, pl.ds(r0, CA_BAND), hh)
                w = _ca_weights(pr, qp, kb, bias[hh], j, window)
                dyp = _head_cols(dy_ref, pl.ds(r0, CA_ROWS), hh).astype(BF16)
                dw = _dot(dyp, vb, NT)
                dz = w * (dw - jnp.sum(dw * w, axis=1, keepdims=True))
                db_ref[hh] += sum(dz[q * PAIR:(q + 1) * PAIR, q * PAIR:q * PAIR + PBAND] for q in range(CA_PAIRS))
                dzs = (dz * SCALE).astype(BF16)
                dq_ref[pl.ds(r0, CA_ROWS), cols] = _dot(dzs, kb).astype(BF16)
                dkpad[pl.ds(r0, CA_BAND), cols] += _dot(dzs, qp, TN)
                dvpad[pl.ds(r0, CA_BAND), cols] += _dot(w.astype(BF16), dyp, TN)
            return 0

        lax.fori_loop(0, T // CA_ROWS, step, 0)
        dk_ref[...] = dkpad[PAD:PAD + T, :].astype(BF16)
        dv_ref[...] = dvpad[PAD:PAD + T, :].astype(BF16)

    c = col0 // HP
    full = _head_spec(T, 0, HP)
    bspec = pl.BlockSpec((HP, PAIR, PBAND), lambda h: (h, 0, 0))
    shp = jax.ShapeDtypeStruct((T, n_heads * HEAD_DIM), BF16)
    return pl.pallas_call(
        body, grid=(G,),
        in_specs=[_head_spec(T, c, HP), _head_spec(T, c + G, HP), _head_spec(T, c + 2 * G, HP), bspec, full],
        out_specs=[full, full, full, bspec],
        out_shape=[shp, shp, shp, jax.ShapeDtypeStruct((n_heads, PAIR, PBAND), F32)],
        scratch_shapes=[pltpu.VMEM((PAD + T, HP * HEAD_DIM), BF16)] * 2 + [pltpu.VMEM((PAD + T, HP * HEAD_DIM), F32)] * 2,
        compiler_params=_params("parallel"), name="ca_bwd")(proj, proj, proj, bias, dy)


def _local_step(x, p, target, comm, g):
    T, D = x.shape
    H = g["rel_bias"].shape[0]
    W = H * HEAD_DIM
    nb_in = comm.shapes["w_in_a"][2]
    nb_ff = comm.shapes["w_ffn_in"][2]
    nb_o = comm.shapes["w_sb_out"][2]
    nb_p = comm.shapes["w_ple_in"][2]
    tm = min(T, 1024)
    tn = min(D, 1024)
    gate_col = 6 * W // D

    h1 = _norm_fwd(x, g["g_mix"], "norm1")
    comm.stage("norm1", h1)
    proj = _mm(h1, comm.weight("w_in_a", h1), mode="nn", tm=tm, tn=nb_in, tk=D // 2, out_dtype=F32, b_blocked=True,
               after=comm.pending(), name="mm_in_a")
    comm.stage("mm_in_a", proj)
    proj = _mm(h1, comm.weight("w_in_b", proj), mode="nn", tm=tm, tn=nb_in, tk=D // 2, out_dtype=F32, b_blocked=True, a_cols=(1, 1),
               res=proj, after=comm.pending(), name="mm_in")
    comm.stage("mm_in", proj)
    y_sb = _sb_fwd(proj, H, comm.pending())
    bias = _bias_expand(g["rel_bias"])
    y_ca = _ca_fwd(proj, bias, H, 3 * H)
    comm.stage("attention", y_sb, y_ca)
    a_sb, a_ca, merged = _mm_merge(y_sb, y_ca, comm.weight("w_sb_out", y_ca), comm.weight("w_ca_out"), proj, gate_col, min(T, 512), tn, comm.pending())
    x2, h2 = _mm_fused(merged, comm.weight("w_mix_out"), [(x, 0), g["g_ffn"]], _residual_norm, [F32, BF16],
                       mode="nn", tm=min(T, 512), tn=D, tk=D, name="mm_mix")
    gate, up, act = _mm_swiglu(h2, comm.weight("w_ffn_in", h2), min(T, 512))
    comm.stage("mm_ffn_in", act)
    F = act.shape[1]
    tkf = F // 2 if F % 256 == 0 else F
    x3 = _mm(act, comm.weight("w_ffn_out", act), mode="nn", tm=tm, tn=tn, tk=tkf, out_dtype=F32, res=x2, after=comm.pending(), name="mm_ffn_out")
    h3 = _norm_fwd(x3, g["g_ple"], "norm3")
    pb = _cast_bf16(p, "cast_p")
    P = p.shape[1]
    pe = _mm(pb, comm.weight("w_ple_in"), mode="nn", tm=tm, tn=tn, tk=P, out_dtype=F32, b_blocked=True, name="mm_ple_in")
    dx4, dpe, dzg, loss, dg_final = _mm_fused(
        h3, comm.weight("w_ple_gate"), [(x3, 0), (pe, 0), (target, 0), g["g_final"]], _tail, [F32, BF16, BF16],
        mode="nn", tm=min(T, 256), tn=D, tk=D, sums=[(1, 128), (1, D)], name="mm_ple_gate")

    tw = min(D, 1024)
    DW = BF16
    comm.grad("w_ple_in", _mm(pb, dpe, mode="tn", tm=P, tn=nb_p, tk=T, out_dtype=DW, out_block=nb_p, name="mm_d_ple_in"))
    comm.grad("w_ple_gate", _mm(h3, dzg, mode="tn", tm=tw, tn=tn, tk=T, out_dtype=DW, name="mm_d_ple_gate"))
    dx3, dx3b, dg_ple = _mm_fused(dzg, comm.weight("w_ple_gate"), [(x3, 0), (dx4, 0), g["g_ple"]], _residual_norm_bwd, [F32, BF16],
                                  mode="nt", tm=min(T, 256), tn=D, tk=D, sums=[(1, D)], after=comm.pending(), name="mm_dh3")
    comm.grad("w_ffn_out", _mm(act, dx3b, mode="tn", tm=F // 4, tn=tn, tk=T, out_dtype=DW, name="mm_d_ffn_out"))
    dgate, dup = _mm_fused(dx3b, comm.weight("w_ffn_out"), [(gate, 0), (up, 0)], _swiglu_bwd, [BF16, BF16],
                           mode="nt", tm=min(T, 512), tn=nb_ff, tk=D, after=comm.pending(), b_outer=True, name="mm_dact")
    half = comm.shapes["w_ffn_in"][0] // 2
    d_ffn_in = _mm(h2, dgate, mode="tn", tm=tw, tn=nb_ff, tk=T, out_dtype=DW, out_block=nb_ff, o_count=2 * half, name="mm_d_ffn_in_gate")
    comm.grad("w_ffn_in", _mm(h2, dup, mode="tn", tm=tw, tn=nb_ff, tk=T, out_dtype=DW, out_block=nb_ff, o_first=half, o_count=2 * half,
                              into=d_ffn_in, name="mm_d_ffn_in"))
    dh2 = _mm(dgate, comm.weight("w_ffn_in"), mode="nt", tm=tm, tn=D, tk=nb_ff, out_dtype=F32, b_blocked=True, b_count=half,
              after=comm.pending(), name="mm_dh2_gate")
    dh2 = _mm(dup, comm.weight("w_ffn_in"), mode="nt", tm=min(T, 512), tn=D, tk=nb_ff, out_dtype=F32, b_blocked=True, b_first=half, b_count=half,
              res=dh2, name="mm_dh2")
    dx2, dx2b, dg_ffn = _norm_bwd(dh2, x2, g["g_ffn"], dx3, "norm2_bwd")
    per = D // tn
    da_sb, da_ca, dgate_sb, dgate_ca = _mm_fused(
        dx2b, comm.weight("w_mix_out"), [(proj, gate_col * per), (proj, (gate_col + 1) * per), (a_sb, 0), (a_ca, 0)], _merge_bwd, [BF16] * 4,
        mode="nt", tm=min(T, 512), tn=tn, tk=D, name="mm_dmerged")
    comm.grad("w_mix_out", _mm(merged, dx2b, mode="tn", tm=tw, tn=tn, tk=T, out_dtype=DW, name="mm_d_mix"))
    comm.grad("w_sb_out", _mm(y_sb, da_sb, mode="tn", tm=min(W, 512), tn=tn, tk=T, out_dtype=DW, out_block=nb_o, name="mm_d_sb_out"))
    comm.grad("w_ca_out", _mm(y_ca, da_ca, mode="tn", tm=min(W, 512), tn=tn, tk=T, out_dtype=DW, out_block=nb_o, name="mm_d_ca_out"))
    dy_sb = _mm(da_sb, comm.weight("w_sb_out"), mode="nt", tm=tm, tn=W, tk=tn, out_dtype=BF16, b_blocked=True, after=comm.pending(), name="mm_dy_sb")
    dy_ca = _mm(da_ca, comm.weight("w_ca_out"), mode="nt", tm=tm, tn=W, tk=tn, out_dtype=BF16, b_blocked=True, name="mm_dy_ca")
    dq_sb, dk_sb, dv_sb = _sb_bwd(proj, dy_sb, H)
    dq_ca, dk_ca, dv_ca, dbias = _ca_bwd(proj, bias, dy_ca, H, 3 * H)
    d_rel = _bias_reduce(dbias)[:, :N_REL]
    dproj = jnp.concatenate([dq_sb, dk_sb, dv_sb, dq_ca, dk_ca, dv_ca, dgate_sb, dgate_ca], axis=1)
    comm.grad("w_in", _mm(h1, dproj, mode="tn", tm=tw, tn=nb_in, tk=T, out_dtype=DW, out_block=nb_in, name="mm_d_in"))
    dh1 = _mm(dproj, comm.weight("w_in_a"), mode="nt", tm=tm, tn=D // 2, tk=nb_in, out_dtype=F32, b_blocked=True, out_cols=(2, 1, 0),
              after=comm.pending(), name="mm_dh1_a")
    comm.pair_done(dh1)
    dh1 = _mm(dproj, comm.weight("w_in_b"), mode="nt", tm=tm, tn=D // 2, tk=nb_in, out_dtype=F32, b_blocked=True, out_cols=(2, 1, 1),
              into=dh1, after=comm.pending(), name="mm_dh1")
    grad_x, _, dg_mix = _norm_bwd(dh1, x, g["g_mix"], dx2, "norm1_bwd")
    small = dict(g_mix=dg_mix, g_ffn=dg_ffn, g_ple=dg_ple, g_final=dg_final, rel_bias=d_rel)
    return loss, grad_x, small


def _position():
    x, y, c = lax.axis_index("x"), lax.axis_index("y"), lax.axis_index("c")
    return x, y, c


def _block_of(px, py, pc):
    return 4 * px + 2 * py + pc


def _flip(pos, k):
    x, y, c = pos
    return (1 - x if k & 4 else x, 1 - y if k & 2 else y, 1 - c if k & 1 else c)


HBM = pl.BlockSpec(memory_space=pltpu.HBM)
SEM = pl.BlockSpec(memory_space=pltpu.SEMAPHORE)
VMEM_SPEC = pl.BlockSpec(memory_space=pltpu.VMEM)
EFFECT = pltpu.SideEffectType.DATAFLOW_SIDE_EFFECTING
TOKEN = jax.ShapeDtypeStruct((8, 128), F32)


def _hbm(a):
    return pltpu.HBM(a.shape, a.dtype)


def _landing(shape, dtype):
    return pltpu.with_memory_space_constraint(lax.empty(shape, dtype), pltpu.HBM)


def _gather_start(lands, after, name):
    n = len(lands)

    def body(*refs):
        ins = refs[:n]
        send, recv = refs[n + 1], refs[n + 2]
        token = refs[-1]
        x, y, c = _position()
        mine = _block_of(x, y, c)
        peers = [(x, y, 1 - c), (1 - x, y, c), (x, 1 - y, c), (1 - x, 1 - y, c)]
        for wi in range(n):
            for k, peer in enumerate(peers):
                pltpu.make_async_remote_copy(
                    src_ref=ins[wi].at[mine], dst_ref=ins[wi].at[mine], send_sem=send.at[4 * wi + k], recv_sem=recv.at[4 * wi + k],
                    device_id=peer, device_id_type=MESH).start()
        token[...] = jnp.zeros_like(token)

    outs = pl.pallas_call(
        body, name=name, in_specs=[HBM] * n + [ANY], out_specs=(SEM, SEM, *[HBM] * n, VMEM_SPEC),
        out_shape=(pltpu.SemaphoreType.DMA((4 * n,)), pltpu.SemaphoreType.DMA((4 * n,)), *[_hbm(a) for a in lands], TOKEN),
        input_output_aliases={i: 2 + i for i in range(n)},
        compiler_params=pltpu.CompilerParams(has_side_effects=EFFECT))(*[pltpu.with_memory_space_constraint(a, pltpu.HBM) for a in lands], after)
    return outs[0], outs[1], list(outs[2:2 + n]), outs[-1]


def _gather_forward(lands, send0, recv0, after, name):
    n = len(lands)

    def body(*refs):
        ins = refs[:n]
        send0, recv0 = refs[n], refs[n + 1]
        send1, recv1 = refs[n + 2 + len(after)], refs[n + 3 + len(after)]
        token = refs[-1]
        x, y, c = _position()
        chips = [(1 - x, y), (x, 1 - y), (1 - x, 1 - y)]
        for wi in range(n):
            for j, chip in enumerate(chips):
                rows = ins[wi].at[_block_of(*chip, c)]
                pltpu.make_async_remote_copy(
                    src_ref=rows, dst_ref=rows, send_sem=send0.at[4 * wi + 1 + j], recv_sem=recv0.at[4 * wi + 1 + j],
                    device_id=(*chip, c), device_id_type=MESH).wait_recv()
                pltpu.make_async_remote_copy(
                    src_ref=rows, dst_ref=rows, send_sem=send1.at[3 * wi + j], recv_sem=recv1.at[3 * wi + j],
                    device_id=(x, y, 1 - c), device_id_type=MESH).start()
        token[...] = jnp.zeros_like(token)

    outs = pl.pallas_call(
        body, name=name, in_specs=[HBM] * n + [SEM, SEM] + [ANY] * len(after), out_specs=(SEM, SEM, *[HBM] * n, VMEM_SPEC),
        out_shape=(pltpu.SemaphoreType.DMA((3 * n,)), pltpu.SemaphoreType.DMA((3 * n,)), *[_hbm(a) for a in lands], TOKEN),
        input_output_aliases={i: 2 + i for i in range(n)},
        compiler_params=pltpu.CompilerParams(has_side_effects=EFFECT))(*lands, send0, recv0, *after)
    return outs[0], outs[1], list(outs[2:2 + n]), outs[-1]


def _gather_wait(lands, send0, recv0, send1, recv1, after, name):
    n = len(lands)

    def body(*refs):
        ins = refs[:n]
        send0, recv0, send1, recv1 = refs[n:n + 4]
        x, y, c = _position()
        mine = _block_of(x, y, c)
        sibling = (x, y, 1 - c)
        peers = [sibling, (1 - x, y, c), (x, 1 - y, c), (1 - x, 1 - y, c)]
        chips = [(1 - x, y), (x, 1 - y), (1 - x, 1 - y)]
        for wi in range(n):
            own = ins[wi].at[mine]
            for k, peer in enumerate(peers):
                pltpu.make_async_remote_copy(src_ref=own, dst_ref=own, send_sem=send0.at[4 * wi + k], recv_sem=recv0.at[4 * wi + k],
                                             device_id=peer, device_id_type=MESH).wait_send()
            theirs = ins[wi].at[_block_of(*sibling)]
            pltpu.make_async_remote_copy(src_ref=theirs, dst_ref=theirs, send_sem=send0.at[4 * wi], recv_sem=recv0.at[4 * wi],
                                         device_id=sibling, device_id_type=MESH).wait_recv()
            for j, chip in enumerate(chips):
                sent = ins[wi].at[_block_of(*chip, c)]
                got = ins[wi].at[_block_of(*chip, 1 - c)]
                pltpu.make_async_remote_copy(src_ref=sent, dst_ref=sent, send_sem=send1.at[3 * wi + j], recv_sem=recv1.at[3 * wi + j],
                                             device_id=sibling, device_id_type=MESH).wait_send()
                pltpu.make_async_remote_copy(src_ref=got, dst_ref=got, send_sem=send1.at[3 * wi + j], recv_sem=recv1.at[3 * wi + j],
                                             device_id=sibling, device_id_type=MESH).wait_recv()

    outs = pl.pallas_call(
        body, name=name, in_specs=[HBM] * n + [SEM] * 4 + [ANY], out_specs=tuple([HBM] * n),
        out_shape=tuple(_hbm(a) for a in lands), input_output_aliases={i: i for i in range(n)},
        compiler_params=pltpu.CompilerParams(has_side_effects=EFFECT))(*lands, send0, recv0, send1, recv1, after)
    return list(outs)


def _plan_direct(me):
    return [(_block_of(*_flip(me, k)), k - 1, _flip(me, k)) for k in range(1, N_DEV)]


def _plan_sibling(me):
    x, y, c = me
    return [(_block_of(ci // 2, ci % 2, 1 - c), ci, (x, y, 1 - c)) for ci in range(4)]


def _plan_chips(me):
    x, y, c = me
    out = []
    for k in range(1, 4):
        px, py = (1 - x if k & 2 else x), (1 - y if k & 1 else y)
        out.append((2 * px + py, k - 1, (px, py, c)))
    return out


def _exchange_start(blocks, plan, name):
    n = len(blocks)
    slots = len(plan((0, 0, 0)))

    def body(*refs):
        srcs, lands = refs[:n], refs[n:2 * n]
        send, recv = refs[2 * n], refs[2 * n + 1]
        token = refs[-1]
        for wi in range(n):
            for block, slot, peer in plan(_position()):
                pltpu.make_async_remote_copy(
                    src_ref=srcs[wi].at[block], dst_ref=lands[wi].at[slot], send_sem=send.at[slots * wi + slot],
                    recv_sem=recv.at[slots * wi + slot], device_id=peer, device_id_type=MESH).start()
        token[...] = jnp.zeros_like(token)

    zones = [_landing((slots,) + b.shape[1:], b.dtype) for b in blocks]
    outs = pl.pallas_call(
        body, name=name, in_specs=[HBM] * (2 * n), out_specs=(SEM, SEM, *[HBM] * (2 * n), VMEM_SPEC),
        out_shape=(pltpu.SemaphoreType.DMA((slots * n,)), pltpu.SemaphoreType.DMA((slots * n,)), *[_hbm(a) for a in blocks],
                   *[_hbm(z) for z in zones], TOKEN),
        input_output_aliases={i: 2 + i for i in range(2 * n)},
        compiler_params=pltpu.CompilerParams(has_side_effects=EFFECT))(
            *[pltpu.with_memory_space_constraint(b, pltpu.HBM) for b in blocks], *zones)
    return outs[0], outs[1], list(outs[2:2 + n]), list(outs[2 + n:2 + 2 * n]), outs[-1]


def _exchange_wait(groups, plan, after, name):
    flat, counts = [], []
    for send, recv, blocks, zones in groups:
        flat += [*blocks, *zones, send, recv]
        counts.append(len(blocks))
    slots = len(plan((0, 0, 0)))

    def body(*refs):
        pos = 0
        for n in counts:
            srcs, lands = refs[pos:pos + n], refs[pos + n:pos + 2 * n]
            send, recv = refs[pos + 2 * n], refs[pos + 2 * n + 1]
            pos += 2 * n + 2
            for wi in range(n):
                for block, slot, peer in plan(_position()):
                    cp = pltpu.make_async_remote_copy(
                        src_ref=srcs[wi].at[block], dst_ref=lands[wi].at[slot], send_sem=send.at[slots * wi + slot],
                        recv_sem=recv.at[slots * wi + slot], device_id=peer, device_id_type=MESH)
                    cp.wait_send()
                    cp.wait_recv()

    in_specs, out_specs, out_shape, aliases = [], [], [], {}
    i = 0
    for n, (send, recv, blocks, zones) in zip(counts, groups):
        for a in (*blocks, *zones):
            aliases[i] = len(out_shape)
            in_specs.append(HBM)
            out_specs.append(HBM)
            out_shape.append(_hbm(a))
            i += 1
        in_specs += [SEM, SEM]
        i += 2
    outs = pl.pallas_call(
        body, name=name, in_specs=in_specs + [ANY], out_specs=tuple(out_specs), out_shape=tuple(out_shape),
        input_output_aliases=aliases, compiler_params=pltpu.CompilerParams(has_side_effects=EFFECT))(*flat, after)
    res, pos = [], 0
    for n in counts:
        res.append((list(outs[pos:pos + n]), list(outs[pos + n:pos + 2 * n])))
        pos += 2 * n
    return res


def _sibling_sum(blocks, zone, core, name):
    _, R, C = zone.shape
    rt = next(r for r in (R, R // 2, R // 4, 128, 64) if R % r == 0 and r % 16 == 0 and r * C <= 4 * 1024 * 1024)

    def body(core_ref, own_ref, z_ref, o_ref):
        o_ref[...] = (own_ref[...].astype(F32) + z_ref[...].astype(F32)).astype(o_ref.dtype)

    grid_spec = pltpu.PrefetchScalarGridSpec(
        num_scalar_prefetch=1, grid=(4, R // rt),
        in_specs=[pl.BlockSpec((None, rt, C), lambda ci, i, core_ref: (2 * ci + core_ref[0], i, 0)),
                  pl.BlockSpec((None, rt, C), lambda ci, i, core_ref: (ci, i, 0))],
        out_specs=pl.BlockSpec((None, rt, C), lambda ci, i, core_ref: (ci, i, 0)))
    return pl.pallas_call(body, grid_spec=grid_spec, out_shape=jax.ShapeDtypeStruct(zone.shape, zone.dtype),
                          compiler_params=_params("parallel", "parallel"), name=name)(core, blocks, zone)


def _adamw(w, g, m, v):
    m = ADAM_B1 * m + (1.0 - ADAM_B1) * g
    v = ADAM_B2 * v + (1.0 - ADAM_B2) * (g * g)
    m_hat = m / (1.0 - ADAM_B1 ** ADAM_STEP)
    v_hat = v / (1.0 - ADAM_B2 ** ADAM_STEP)
    delta = -ADAM_LR * (m_hat / (jnp.sqrt(v_hat) + ADAM_EPS) + ADAM_WD * w)
    return delta, m, v


def _reduce_adamw(blocks, zone, mine, w, m, v, name):
    R, C = w.shape
    rt = next(r for r in (256, 128, 64) if R % r == 0 and r * C <= 512 * 1024)

    def body(mine_ref, own_ref, z_ref, w_ref, m_ref, v_ref, g_out, d_out, m_out, v_out):
        g = own_ref[...].astype(F32)
        for s in range(zone.shape[0]):
            g = g + z_ref[s].astype(F32)
        delta, m2, v2 = _adamw(w_ref[...], g, m_ref[...], v_ref[...])
        g_out[...] = g
        d_out[...] = delta
        m_out[...] = m2
        v_out[...] = v2

    spec = pl.BlockSpec((rt, C), lambda i, mine_ref: (i, 0))
    grid_spec = pltpu.PrefetchScalarGridSpec(
        num_scalar_prefetch=1, grid=(R // rt,),
        in_specs=[pl.BlockSpec((None, rt, C), lambda i, mine_ref: (mine_ref[0], i, 0)),
                  pl.BlockSpec((zone.shape[0], rt, C), lambda i, mine_ref: (0, i, 0)), spec, spec, spec],
        out_specs=[spec] * 4)
    return pl.pallas_call(body, grid_spec=grid_spec, out_shape=[jax.ShapeDtypeStruct((R, C), F32)] * 4,
                          compiler_params=_params("parallel"), name=name)(mine, blocks, zone, w, m, v)


def _small_step(part, w, m, v, after):
    R, C = part.shape

    def body(part_ref, w_ref, m_ref, v_ref, *rest):
        g_out, d_out, m_out, v_out, gath, send, recv = rest[len(after):]
        me = _position()
        gath[_block_of(*me)] = part_ref[...]

        def copy(k, slot):
            return pltpu.make_async_remote_copy(
                src_ref=part_ref, dst_ref=gath.at[slot], send_sem=send.at[k - 1], recv_sem=recv.at[k - 1],
                device_id=_flip(me, k), device_id_type=MESH)

        sent = [copy(k, _block_of(*me)) for k in range(1, N_DEV)]
        for cp in sent:
            cp.start()
        for k in range(1, N_DEV):
            copy(k, _block_of(*_flip(me, k))).wait_recv()
        for cp in sent:
            cp.wait_send()
        g = gath[0]
        for s in range(1, N_DEV):
            g = g + gath[s]
        delta, m2, v2 = _adamw(w_ref[...], g, m_ref[...], v_ref[...])
        g_out[...] = g
        d_out[...] = delta
        m_out[...] = m2
        v_out[...] = v2

    vm = pl.BlockSpec(memory_space=pltpu.VMEM)
    return pl.pallas_call(
        body, in_specs=[vm] * 4 + [ANY] * len(after), out_specs=[vm] * 4, out_shape=[jax.ShapeDtypeStruct((R, C), F32)] * 4,
        scratch_shapes=[pltpu.VMEM((N_DEV, R, C), F32), pltpu.SemaphoreType.DMA((7,)), pltpu.SemaphoreType.DMA((7,))],
        name="small_step")(part, w, m, v, *after)


COLUMN_SHARDED = ("w_in", "w_sb_out", "w_ca_out", "w_ffn_in", "w_ple_in")
ROW_SHARDED = ("w_mix_out", "w_ffn_out", "w_ple_gate")
BIG = COLUMN_SHARDED + ROW_SHARDED
SMALL = ("g_mix", "g_ffn", "g_ple", "g_final", "rel_bias")
WEIGHTS = ("w_in", "w_sb_out", "w_ca_out", "w_mix_out", "rel_bias", "g_mix", "g_ffn", "g_ple", "g_final",
           "w_ffn_in", "w_ffn_out", "w_ple_in", "w_ple_gate")


def _pack_small(t, D):
    rows = [t[n].reshape(1, D) for n in SMALL[:4]]
    rb = t["rel_bias"].reshape(1, -1)
    rows.append(jnp.pad(rb, ((0, 0), (0, D - rb.shape[1]))))
    return jnp.concatenate(rows + [jnp.zeros((8 - len(rows), D), F32)], axis=0)


def _unpack_small(a, like):
    out = {n: a[i].reshape(like[n].shape) for i, n in enumerate(SMALL[:4])}
    out["rel_bias"] = a[4, :like["rel_bias"].size].reshape(like["rel_bias"].shape)
    return out


GATHER_GROUPS = (("w_in_a",), ("w_in_b",), ("w_sb_out", "w_ca_out", "w_mix_out"), ("w_ffn_in",), ("w_ffn_out", "w_ple_gate", "w_ple_in"))
FORWARD_AFTER = ("norm1", "mm_in_a", "mm_in", "attention", "mm_ffn_in")
GRAD_GROUPS = (("w_ple_in", "w_ple_gate"), ("w_ffn_out",), ("w_ffn_in",), ("w_mix_out", "w_sb_out", "w_ca_out"), ("w_in",))


class _Exchange:
    def __init__(self, shards):
        me = _position()
        self.mine = _block_of(*me)
        self.chip = jnp.reshape(2 * me[0] + me[1], (1,)).astype(jnp.int32)
        self.core = jnp.reshape(me[2], (1,)).astype(jnp.int32)
        self.device = jnp.reshape(self.mine, (1,)).astype(jnp.int32)
        self.shapes = {n: ((N_DEV * s.shape[0], s.shape[1]) if n in ROW_SHARDED else (N_DEV,) + s.shape) for n, s in shards.items()}
        self.tokens = []
        self.ready = {}
        self.gathers = []
        for gi, names in enumerate(GATHER_GROUPS):
            lands = [lax.dynamic_update_slice(lax.empty((N_DEV,) + shards[n].shape, BF16), shards[n][None], (self.mine, 0, 0))
                     for n in names]
            behind = self.tokens[-1] if self.tokens else shards[names[0]]
            send0, recv0, lands, token = _gather_start(lands, behind, f"gather_start_{gi}")
            self.tokens.append(token)
            self.gathers.append(dict(names=names, lands=lands, sems=(send0, recv0), token=token))
        self.grads = {}
        self.exchanges = []

    def pending(self):
        tokens, self.tokens = self.tokens, []
        return tokens

    def stage(self, tag, *made):
        gi = FORWARD_AFTER.index(tag)
        gth = self.gathers[gi]
        send1, recv1, lands, token = _gather_forward(gth["lands"], *gth["sems"], made + tuple(self.tokens), f"gather_forward_{gi}")
        gth.update(lands=lands, sems=gth["sems"] + (send1, recv1), token=token)
        self.tokens.append(token)

    def weight(self, name, after=None):
        if name not in self.ready:
            gi = next(i for i, names in enumerate(GATHER_GROUPS) if name in names)
            gth = self.gathers[gi]
            for n, a in zip(gth["names"], _gather_wait(gth["lands"], *gth["sems"], gth["token"] if after is None else after, f"gather_wait_{gi}")):
                self.ready[n] = a.reshape(self.shapes[n])
        return self.ready[name]

    def grad(self, name, blocks):
        self.grads[name] = blocks if name in COLUMN_SHARDED else blocks.reshape((N_DEV, -1, blocks.shape[-1]))
        names = next(names for names in GRAD_GROUPS if name in names)
        if not all(n in self.grads for n in names):
            return
        blocks = [self.grads[n] for n in names]
        if names == GRAD_GROUPS[-1]:
            send, recv, blocks, zones, token = _exchange_start(blocks, _plan_sibling, "pair_start_" + names[0])
            self.pair = (send, recv, blocks, zones)
        else:
            send, recv, blocks, zones, token = _exchange_start(blocks, _plan_direct, "exchange_start_" + names[0])
            self.exchanges.append(dict(names=names, state=(send, recv, blocks, zones), plan=_plan_direct, own=self.device))
        self.tokens.append(token)

    def pair_done(self, after):
        names = GRAD_GROUPS[-1]
        (blocks, zones), = _exchange_wait([self.pair], _plan_sibling, after, "pair_wait_" + names[0])
        blocks = [_sibling_sum(b, z, self.core, "pair_sum_" + n) for n, b, z in zip(names, blocks, zones)]
        send, recv, blocks, zones, token = _exchange_start(blocks, _plan_chips, "exchange_start_" + names[0])
        self.exchanges.append(dict(names=names, state=(send, recv, blocks, zones), plan=_plan_chips, own=self.chip))
        self.tokens.append(token)

    def collect(self, which, after, name):
        sel = [e for e in self.exchanges if GRAD_GROUPS.index(e["names"]) in which]
        out = {}
        for e, (blocks, zones) in zip(sel, _exchange_wait([e["state"] for e in sel], sel[0]["plan"], after, name)):
            out.update({n: (b, e["own"], z) for n, b, z in zip(e["names"], blocks, zones)})
        return out


def kernel(x, p, w_in, w_sb_out, w_ca_out, w_mix_out, rel_bias, g_mix, g_ffn, g_ple, g_final, w_ffn_in, w_ffn_out, w_ple_in, w_ple_gate, loss_target, m_w_in, m_w_sb_out, m_w_ca_out, m_w_mix_out, m_rel_bias, m_g_mix, m_g_ffn, m_g_ple, m_g_final, m_w_ffn_in, m_w_ffn_out, m_w_ple_in, m_w_ple_gate, v_w_in, v_w_sb_out, v_w_ca_out, v_w_mix_out, v_rel_bias, v_g_mix, v_g_ffn, v_g_ple, v_g_final, v_w_ffn_in, v_w_ffn_out, v_w_ple_in, v_w_ple_gate):
    wts = dict(w_in=w_in, w_sb_out=w_sb_out, w_ca_out=w_ca_out, w_mix_out=w_mix_out, rel_bias=rel_bias, g_mix=g_mix, g_ffn=g_ffn,
               g_ple=g_ple, g_final=g_final, w_ffn_in=w_ffn_in, w_ffn_out=w_ffn_out, w_ple_in=w_ple_in, w_ple_gate=w_ple_gate)
    mom = dict(w_in=m_w_in, w_sb_out=m_w_sb_out, w_ca_out=m_w_ca_out, w_mix_out=m_w_mix_out, rel_bias=m_rel_bias, g_mix=m_g_mix,
               g_ffn=m_g_ffn, g_ple=m_g_ple, g_final=m_g_final, w_ffn_in=m_w_ffn_in, w_ffn_out=m_w_ffn_out, w_ple_in=m_w_ple_in,
               w_ple_gate=m_w_ple_gate)
    var = dict(w_in=v_w_in, w_sb_out=v_w_sb_out, w_ca_out=v_w_ca_out, w_mix_out=v_w_mix_out, rel_bias=v_rel_bias, g_mix=v_g_mix,
               g_ffn=v_g_ffn, g_ple=v_g_ple, g_final=v_g_final, w_ffn_in=v_w_ffn_in, w_ffn_out=v_w_ffn_out, w_ple_in=v_w_ple_in,
               w_ple_gate=v_w_ple_gate)
    T, D = x.shape[1], x.shape[2]
    shard = {n: wts[n].reshape(wts[n].shape[-2:]) for n in BIG}
    bf = {n: _cast_bf16(shard[n], "cast_" + n) for n in BIG}
    half = bf["w_in"].shape[0] // 2
    bf["w_in_a"], bf["w_in_b"] = bf["w_in"][:half], bf.pop("w_in")[half:]
    comm = _Exchange(bf)
    g = dict(g_mix=g_mix.reshape(1, D), g_ffn=g_ffn.reshape(1, D), g_ple=g_ple.reshape(1, D), g_final=g_final.reshape(1, D),
             rel_bias=rel_bias.reshape(rel_bias.shape[-2:]))

    loss, grad_x, dsmall = _local_step(x.reshape(T, D), p.reshape(T, -1), loss_target.reshape(T, D), comm, g)
    loss = lax.psum(loss[0, 0], ("x", "y", "c"))

    grad, delta, new_m, new_v = {}, {}, {}, {}

    def update(parts):
        done = []
        for n, (blocks, own, zone) in parts.items():
            outs = _reduce_adamw(blocks, zone, own, shard[n], mom[n].reshape(shard[n].shape), var[n].reshape(shard[n].shape), "adamw_" + n)
            grad[n], delta[n], new_m[n], new_v[n] = [o.reshape(wts[n].shape) for o in outs]
            done.append(outs[0])
        return done

    done = update(comm.collect(range(len(GRAD_GROUPS) - 1), grad_x, "exchange_wait_rest"))
    outs = _small_step(_pack_small(dsmall, D), _pack_small(wts, D), _pack_small(mom, D), _pack_small(var, D), done)
    for dst, a in zip((grad, delta, new_m, new_v), outs):
        dst.update(_unpack_small(a, wts))
    update(comm.collect([len(GRAD_GROUPS) - 1], outs[0], "exchange_wait_w_in"))

    return (loss, grad_x.reshape(x.shape), *[grad[n] for n in WEIGHTS], *[delta[n] for n in WEIGHTS],
            *[new_m[n] for n in WEIGHTS], *[new_v[n] for n in WEIGHTS])
```

```python
import functools

import jax
import jax.numpy as jnp
from jax import lax
from jax.experimental import pallas as pl
from jax.experimental.pallas import tpu as pltpu

F32, BF16 = jnp.float32, jnp.bfloat16

N_DEV = 8
HEAD_DIM = 128
CHUNK = 64
LEFT_CHUNKS = 8
REL_CLIP = 128
N_REL = REL_CLIP + CHUNK
PAIR = 2 * CHUNK
PBAND = (LEFT_CHUNKS + 2) * CHUNK
CA_PAIRS = 2
CA_ROWS = CA_PAIRS * PAIR
CA_BAND = PBAND + CA_ROWS - PAIR
PAD = LEFT_CHUNKS * CHUNK
SB_BLOCK = 256
ROWS = 256
EPS = 1e-6
NEG = -1e30
SCALE = HEAD_DIM ** -0.5
VMEM_LIMIT_BYTES = 56 * 1024 * 1024

ADAM_LR, ADAM_B1, ADAM_B2, ADAM_EPS, ADAM_WD, ADAM_STEP = 0.001, 0.9, 0.999, 1e-08, 0.01, 10

ANY = pl.BlockSpec(memory_space=pl.ANY)
NN = (((1,), (0,)), ((), ()))
NT = (((1,), (1,)), ((), ()))
TN = (((0,), (0,)), ((), ()))
MESH = pl.DeviceIdType.MESH


def _params(*sem):
    return pltpu.CompilerParams(dimension_semantics=sem or None, vmem_limit_bytes=VMEM_LIMIT_BYTES)


def _dot(a, b, dims=NN):
    return lax.dot_general(a, b, dims, preferred_element_type=F32)


def _mm(a, b, *, mode, tm, tn, tk, out_dtype, name, b_blocked=False, out_block=None, res=None, after=(), a_cols=(1, 0), out_cols=(1, 1, 0), into=None, b_first=0, b_count=None, o_first=0, o_count=None):
    bg = og = 1
    if mode == "nn":
        M, K = a.shape
        a_spec = pl.BlockSpec((tm, tk), lambda i, j, k: (i, k))
        if b_blocked:
            G, _, nb = b.shape
            N = G * nb
            if tn > nb:
                bg = tn // nb
                assert tn % nb == 0
                b_spec = pl.BlockSpec((bg, tk, nb), lambda i, j, k: (j, k, 0))
            else:
                per = nb // tn
                assert nb % tn == 0
                b_spec = pl.BlockSpec((None, tk, tn), lambda i, j, k: (j // per, k, j % per))
        else:
            N = b.shape[1]
            b_spec = pl.BlockSpec((tk, tn), lambda i, j, k: (k, j))
        dims = NN
    elif mode == "nt":
        M, K = a.shape
        a_spec = pl.BlockSpec((tm, tk), lambda i, j, k: (i, k))
        if b_blocked:
            G, N, nb = b.shape
            K = (b_count or G) * nb
            assert b_first == 0 or tk == nb
            if tk > nb:
                bg = tk // nb
                assert tk % nb == 0
                b_spec = pl.BlockSpec((bg, tn, nb), lambda i, j, k: (k, j, 0))
            else:
                per = nb // tk
                assert nb % tk == 0
                b_spec = pl.BlockSpec((None, tn, tk), lambda i, j, k: (b_first + k // per, j, k % per))
        else:
            N = b.shape[0]
            b_spec = pl.BlockSpec((tn, tk), lambda i, j, k: (j, k))
        dims = NT
    else:
        K, M = a.shape
        N = b.shape[1]
        a_spec = pl.BlockSpec((tk, tm), lambda i, j, k: (k, i))
        b_spec = pl.BlockSpec((tk, tn), lambda i, j, k: (k, j))
        dims = TN
    if mode != "tn":
        if mode == "nn":
            K = b.shape[-2]
        elif not b_blocked:
            K = b.shape[1]
        a_spec = pl.BlockSpec((tm, tk), lambda i, j, k: (i, k * a_cols[0] + a_cols[1]))
    assert M % tm == 0 and N % tn == 0 and K % tk == 0, (name, M, N, K, tm, tn, tk)
    nk = K // tk
    if out_block is None:
        out_shape = jax.ShapeDtypeStruct((M, N * out_cols[0]), out_dtype)
        o_spec = pl.BlockSpec((tm, tn), lambda i, j, k: (i, j * out_cols[1] + out_cols[2]))
    else:
        out_shape = jax.ShapeDtypeStruct((o_count or N // out_block, M, out_block), out_dtype)
        if tn > out_block:
            og = tn // out_block
            assert tn % out_block == 0 and o_first % og == 0
            o_spec = pl.BlockSpec((og, tm, out_block), lambda i, j, k: (o_first // og + j, i, 0))
        else:
            per_o = out_block // tn
            assert out_block % tn == 0
            o_spec = pl.BlockSpec((None, tm, tn), lambda i, j, k: (o_first + j // per_o, i, j % per_o))
    in_specs = [a_spec, b_spec]
    args = [a, b]
    if res is not None:
        in_specs.append(pl.BlockSpec((tm, tn), lambda i, j, k: (i, j * out_cols[1] + out_cols[2])))
        args.append(res)
    n_in = len(args) + len(after) + (into is not None)

    def product(a_ref, b_ref):
        if bg == 1:
            return _dot(a_ref[...], b_ref[...], dims)
        nb = b_ref.shape[2]
        if mode == "nn":
            return jnp.concatenate([_dot(a_ref[...], b_ref[g], dims) for g in range(bg)], axis=1)
        return sum(_dot(a_ref[:, g * nb:(g + 1) * nb], b_ref[g], dims) for g in range(bg))

    def body(*refs):
        a_ref, b_ref = refs[0], refs[1]
        r_ref = refs[2] if res is not None else None
        o_ref = refs[n_in]

        def finish(acc):
            if r_ref is not None:
                acc = acc + r_ref[...]
            if og == 1:
                o_ref[...] = acc.astype(o_ref.dtype)
            else:
                for g in range(og):
                    o_ref[g] = acc[:, g * out_block:(g + 1) * out_block].astype(o_ref.dtype)

        if nk == 1:
            finish(product(a_ref, b_ref))
        else:
            acc_ref = refs[-1]
            k = pl.program_id(2)

            @pl.when(k == 0)
            def _():
                acc_ref[...] = jnp.zeros_like(acc_ref)

            acc_ref[...] += product(a_ref, b_ref)

            @pl.when(k == nk - 1)
            def _():
                finish(acc_ref[...])

    return pl.pallas_call(
        body, grid=(M // tm, N // tn, nk), in_specs=in_specs + [ANY] * (n_in - len(args)), out_specs=o_spec, out_shape=out_shape,
        scratch_shapes=[] if nk == 1 else [pltpu.VMEM((tm, tn), F32)], input_output_aliases={} if into is None else {n_in - 1: 0},
        compiler_params=_params("parallel", "parallel", "arbitrary"), name=name)(*args, *after, *(() if into is None else (into,)))


def _mm_fused(a, b, tiles, fn, outs, *, mode, tm, tn, tk, name, sums=(), after=(), b_outer=False):
    M, K = a.shape
    N = b.shape[1] if mode == "nn" else b.shape[0]
    nk = K // tk
    assert M % tm == 0 and N % tn == 0 and K % tk == 0 and (not sums or tn == N)
    def at(f):
        return (lambda j, i, k: f(i, j, k)) if b_outer else f

    b_spec = pl.BlockSpec((tk, tn), at(lambda i, j, k: (k, j))) if mode == "nn" else pl.BlockSpec((tn, tk), at(lambda i, j, k: (j, k)))
    in_specs = [pl.BlockSpec((tm, tk), at(lambda i, j, k: (i, k))), b_spec]
    args = [a, b]
    for t in tiles:
        if isinstance(t, tuple):
            arr, off = t
            in_specs.append(pl.BlockSpec((tm, tn), at(lambda i, j, k, off=off: (i, off + j))))
        else:
            arr = t
            in_specs.append(pl.BlockSpec((1, tn), at(lambda i, j, k: (0, j))))
        args.append(arr)
    n_in = len(args) + len(after)
    n_out = len(outs) + len(sums)

    def body(*refs):
        a_ref, b_ref = refs[0], refs[1]
        t_refs = refs[2:2 + len(tiles)]
        o_refs = refs[n_in:n_in + n_out]

        def finish(acc):
            res = fn(acc, *[t[...] for t in t_refs])
            for o_ref, r in zip(o_refs[:len(outs)], res):
                o_ref[...] = r.astype(o_ref.dtype)
            if sums:
                @pl.when(pl.program_id(0) == 0)
                def _():
                    for o_ref in o_refs[len(outs):]:
                        o_ref[...] = jnp.zeros_like(o_ref)

                for o_ref, r in zip(o_refs[len(outs):], res[len(outs):]):
                    o_ref[...] += jnp.broadcast_to(r, o_ref.shape)

        if nk == 1:
            finish(_dot(a_ref[...], b_ref[...], NN if mode == "nn" else NT))
        else:
            acc_ref = refs[-1]
            k = pl.program_id(2)

            @pl.when(k == 0)
            def _():
                acc_ref[...] = jnp.zeros_like(acc_ref)

            acc_ref[...] += _dot(a_ref[...], b_ref[...], NN if mode == "nn" else NT)

            @pl.when(k == nk - 1)
            def _():
                finish(acc_ref[...])

    assert not (b_outer and sums)
    o_spec = pl.BlockSpec((tm, tn), at(lambda i, j, k: (i, j)))
    return pl.pallas_call(
        body, grid=(N // tn, M // tm, nk) if b_outer else (M // tm, N // tn, nk), in_specs=in_specs + [ANY] * len(after),
        out_specs=[o_spec] * len(outs) + [pl.BlockSpec(sh, lambda i, j, k: (0, 0)) for sh in sums],
        out_shape=[jax.ShapeDtypeStruct((M, N), dt) for dt in outs] + [jax.ShapeDtypeStruct(sh, F32) for sh in sums],
        scratch_shapes=[] if nk == 1 else [pltpu.VMEM((tm, tn), F32)],
        compiler_params=_params("arbitrary" if sums else "parallel", "parallel", "arbitrary"), name=name)(*args, *after)


def _row_spec(d, col=0):
    return pl.BlockSpec((ROWS, d), lambda i: (i, col))


def _vec_spec(d):
    return pl.BlockSpec((1, d), lambda i: (0, 0))


def _rms(x):
    return lax.rsqrt(jnp.mean(x * x, axis=-1, keepdims=True) + EPS)


def _norm_fwd(x, g, name):
    T, D = x.shape

    def body(x_ref, g_ref, h_ref):
        xv = x_ref[...]
        h_ref[...] = (xv * _rms(xv) * g_ref[...]).astype(BF16)

    return pl.pallas_call(body, grid=(T // ROWS,), in_specs=[_row_spec(D), _vec_spec(D)], out_specs=_row_spec(D),
                          out_shape=jax.ShapeDtypeStruct((T, D), BF16), compiler_params=_params("parallel"), name=name)(x, g)


def _residual_norm(y, x, g):
    x = x + y
    return x, x * _rms(x) * g


def _norm_bwd_math(dh, xv, gv):
    r = _rms(xv)
    xhat = xv * r
    dxhat = dh * gv
    dx = r * (dxhat - xhat * jnp.mean(dxhat * xhat, axis=-1, keepdims=True))
    dg = jnp.sum(dh * xhat, axis=0, keepdims=True)
    return dx, dg


def _residual_norm_bwd(dh, x, dres, g):
    dx, dg = _norm_bwd_math(dh, x, g)
    dx = dx + dres
    return dx, dx, dg


def _norm_bwd(dh, x, g, dres, name):
    T, D = x.shape

    def body(dh_ref, x_ref, g_ref, dres_ref, dx_ref, dxb_ref, dg_ref):
        dx, dg = _norm_bwd_math(dh_ref[...], x_ref[...], g_ref[...])
        dx = dx + dres_ref[...]
        dx_ref[...] = dx
        dxb_ref[...] = dx.astype(BF16)

        @pl.when(pl.program_id(0) == 0)
        def _():
            dg_ref[...] = jnp.zeros_like(dg_ref)

        dg_ref[...] += dg

    return pl.pallas_call(
        body, grid=(T // ROWS,), in_specs=[_row_spec(D), _row_spec(D), _vec_spec(D), _row_spec(D)],
        out_specs=[_row_spec(D), _row_spec(D), _vec_spec(D)],
        out_shape=[jax.ShapeDtypeStruct((T, D), F32), jax.ShapeDtypeStruct((T, D), BF16), jax.ShapeDtypeStruct((1, D), F32)],
        compiler_params=_params("arbitrary"), name=name)(dh, x, g, dres)


def _mm_merge(y_sb, y_ca, w_sb, w_ca, proj, gate_col, tm, tn, after):
    T, W = y_sb.shape
    G, _, nb = w_sb.shape
    D, bg = G * nb, tn // nb
    assert tn % nb == 0 and D % tn == 0
    per = D // tn

    def body(ys_ref, yc_ref, ws_ref, wc_ref, gs_ref, gc_ref, *rest):
        as_ref, ac_ref, m_ref = rest[len(after):]
        a = jnp.concatenate([_dot(ys_ref[...], ws_ref[g]) for g in range(bg)], axis=1)
        b = jnp.concatenate([_dot(yc_ref[...], wc_ref[g]) for g in range(bg)], axis=1)
        as_ref[...] = a
        ac_ref[...] = b
        m_ref[...] = (jax.nn.sigmoid(gs_ref[...]) * a + jax.nn.sigmoid(gc_ref[...]) * b).astype(BF16)

    y_spec = pl.BlockSpec((tm, W), lambda i, j: (i, 0))
    w_spec = pl.BlockSpec((bg, W, nb), lambda i, j: (j, 0, 0))
    out = pl.BlockSpec((tm, tn), lambda i, j: (i, j))
    f32 = jax.ShapeDtypeStruct((T, D), F32)
    return pl.pallas_call(
        body, grid=(T // tm, per),
        in_specs=[y_spec, y_spec, w_spec, w_spec, pl.BlockSpec((tm, tn), lambda i, j: (i, gate_col * per + j)),
                  pl.BlockSpec((tm, tn), lambda i, j: (i, (gate_col + 1) * per + j))] + [ANY] * len(after),
        out_specs=[out, out, out], out_shape=[f32, f32, jax.ShapeDtypeStruct((T, D), BF16)],
        compiler_params=_params("parallel", "parallel"), name="mm_merge")(y_sb, y_ca, w_sb, w_ca, proj, proj, *after)


def _merge_bwd(dm, gs, gc, a, b):
    ss, sc = jax.nn.sigmoid(gs), jax.nn.sigmoid(gc)
    return dm * ss, dm * sc, dm * a * ss * (1.0 - ss), dm * b * sc * (1.0 - sc)


def _mm_swiglu(h, w, tm, after=()):
    T, D = h.shape
    G2, _, nb = w.shape
    G = G2 // 2

    def body(h_ref, wg_ref, wu_ref, *rest):
        g_ref, u_ref, act_ref = rest[len(after):]
        hv = h_ref[...]
        gv = _dot(hv, wg_ref[...])
        uv = _dot(hv, wu_ref[...])
        g_ref[...] = gv
        u_ref[...] = uv
        act_ref[...] = (gv * jax.nn.sigmoid(gv) * uv).astype(BF16)

    out = pl.BlockSpec((tm, nb), lambda j, i: (i, j))
    f32 = jax.ShapeDtypeStruct((T, G * nb), F32)
    return pl.pallas_call(
        body, grid=(G, T // tm),
        in_specs=[pl.BlockSpec((tm, D), lambda j, i: (i, 0)), pl.BlockSpec((None, D, nb), lambda j, i: (j, 0, 0)),
                  pl.BlockSpec((None, D, nb), lambda j, i: (j + G, 0, 0))] + [ANY] * len(after),
        out_specs=[out, out, out], out_shape=[f32, f32, jax.ShapeDtypeStruct((T, G * nb), BF16)],
        compiler_params=_params("parallel", "parallel"), name="mm_ffn_in")(h, w, w, *after)


def _swiglu_bwd(dact, gate, up):
    s = jax.nn.sigmoid(gate)
    return dact * up * s * (1.0 + gate * (1.0 - s)), dact * gate * s


def _tail(zg, x3, pe, target, g_final):
    D = x3.shape[-1]
    gate = jax.nn.sigmoid(zg)
    x4 = x3 + gate * pe
    err = x4 * _rms(x4) * g_final - target
    part = 0.5 * jnp.sum(jnp.mean(err * err, axis=-1, keepdims=True), axis=0, keepdims=True)
    dx, dg = _norm_bwd_math(err * (1.0 / D), x4, g_final)
    return dx, dx * gate, dx * pe * gate * (1.0 - gate), part, dg


def _cast_bf16(x, name):
    R, C = x.shape
    rows = next(r for r in (ROWS, 128, 64, 32, 16) if R % r == 0)

    def body(x_ref, o_ref):
        o_ref[...] = x_ref[...].astype(BF16)

    spec = pl.BlockSpec((rows, C), lambda i: (i, 0))
    return pl.pallas_call(body, grid=(R // rows,), in_specs=[spec], out_specs=spec, out_shape=jax.ShapeDtypeStruct((R, C), BF16),
                          compiler_params=_params("parallel"), name=name)(x)


def _head_spec(T, col0, heads=1, single=False):
    return pl.BlockSpec((T, heads * HEAD_DIM), lambda h, *_: (0, col0 + h), pipeline_mode=pl.Buffered(1) if single else None)


SB_HEADS = 4


def _triangle(n, right):
    j = lax.broadcasted_iota(jnp.int32, (n, n), 0)
    s = lax.broadcasted_iota(jnp.int32, (n, n), 1)
    return jnp.where((j > s) if right else (j < s), 1.0, 0.0).astype(BF16)


def _lane_scan(x, tri):
    hi = x.astype(BF16)
    lo = (x - hi.astype(F32)).astype(BF16)
    return _dot(hi, tri) + _dot(lo, tri)


def _head_cols(ref, rows, hh):
    return ref[rows, hh * HEAD_DIM:(hh + 1) * HEAD_DIM]


def _sb_tile(qv, kk, past, c_lk, tri):
    z = _dot(qv, kk, NT) * SCALE
    sp = jnp.log(1.0 + jnp.exp(-jnp.abs(z)))
    ls_pos = jnp.minimum(z, 0.0) - sp
    lk = jnp.minimum(-z, 0.0) - sp
    if past is not None:
        lk = jnp.where(past, lk, 0.0)
    right = c_lk + _lane_scan(lk, tri)
    a = jnp.exp(ls_pos + right)
    if past is not None:
        a = jnp.where(past, a, 0.0)
    return ls_pos, a, right[:, 0:1] + lk[:, 0:1]


SB_Q = 512
SB_HEADS_BWD = 4


def _sb_mask(d):
    B, r = SB_BLOCK, SB_Q // SB_BLOCK
    return lax.broadcasted_iota(jnp.int32, (SB_Q, B), 1) + (r - 1 - d) * B < lax.broadcasted_iota(jnp.int32, (SB_Q, B), 0)


def _sb_rows(kb):
    return pl.ds(pl.multiple_of(kb * SB_BLOCK, SB_BLOCK), SB_BLOCK)


def _sb_fwd(proj, n_heads, after=()):
    T = proj.shape[0]
    B, Q, HP = SB_BLOCK, SB_Q, SB_HEADS
    r = Q // B
    assert n_heads % HP == 0 and T % Q == 0

    def body(q_ref, k_ref, v_ref, *rest):
        y_ref = rest[-1]
        qb = pl.program_id(1)
        tri = _triangle(B, right=True)
        qv = [_head_cols(q_ref, slice(None), hh).astype(BF16) for hh in range(HP)]

        def tile(kb, carry, past):
            out = []
            for hh in range(HP):
                acc, c_lk = carry[hh]
                kk = _head_cols(k_ref, _sb_rows(kb), hh).astype(BF16)
                vv = _head_cols(v_ref, _sb_rows(kb), hh).astype(BF16)
                _, a, c_lk = _sb_tile(qv[hh], kk, past, c_lk, tri)
                out.append((acc + _dot(a.astype(BF16), vv), c_lk))
            return tuple(out)

        carry = tuple((jnp.zeros((Q, HEAD_DIM), F32), jnp.zeros((Q, 1), F32)) for _ in range(HP))
        for d in range(r):
            carry = tile(r * qb + r - 1 - d, carry, _sb_mask(d))
        res = lax.fori_loop(0, r * qb, lambda i, c: tile(r * qb - 1 - i, c, None), carry)
        for hh in range(HP):
            y_ref[:, hh * HEAD_DIM:(hh + 1) * HEAD_DIM] = res[hh][0].astype(BF16)

    blk = pl.BlockSpec((Q, HP * HEAD_DIM), lambda h, i: (i, h))
    G = n_heads // HP
    return pl.pallas_call(
        body, grid=(G, T // Q),
        in_specs=[blk, _head_spec(T, G, HP), _head_spec(T, 2 * G, HP)] + [ANY] * len(after), out_specs=blk,
        out_shape=jax.ShapeDtypeStruct((T, n_heads * HEAD_DIM), BF16),
        compiler_params=_params("parallel", "arbitrary"), name="sb_fwd")(proj, proj, proj, *after)


def _sb_bwd(proj, dy, n_heads):
    T = proj.shape[0]
    B, Q, HP = SB_BLOCK, SB_Q, SB_HEADS_BWD
    r, nq = Q // B, T // Q

    def body(q_ref, k_ref, v_ref, dy_ref, dq_ref, dk_ref, dv_ref, g_s, sig_s, dk_s, dv_s):
        qb = pl.program_id(1)

        @pl.when(qb == 0)
        def _():
            dk_s[...] = jnp.zeros_like(dk_s)
            dv_s[...] = jnp.zeros_like(dv_s)

        tri_r = _triangle(B, right=True)
        tri_l = _triangle(B, right=False)
        qv = [_head_cols(q_ref, slice(None), hh).astype(BF16) for hh in range(HP)]
        dyb = [_head_cols(dy_ref, slice(None), hh).astype(BF16) for hh in range(HP)]

        def sweep(kb, carry, past):
            out = []
            for hh in range(HP):
                kk = _head_cols(k_ref, _sb_rows(kb), hh).astype(BF16)
                vv = _head_cols(v_ref, _sb_rows(kb), hh).astype(BF16)
                ls_pos, a, c_lk = _sb_tile(qv[hh], kk, past, carry[hh], tri_r)
                g_s[hh, kb] = _dot(dyb[hh], vv, NT) * a
                sig_s[hh, kb] = jnp.exp(ls_pos).astype(BF16)
                dv_s[_sb_rows(kb), hh * HEAD_DIM:(hh + 1) * HEAD_DIM] += _dot(a.astype(BF16), dyb[hh], TN)
                out.append(c_lk)
            return tuple(out)

        carry = tuple(jnp.zeros((Q, 1), F32) for _ in range(HP))
        for d in range(r):
            carry = sweep(r * qb + r - 1 - d, carry, _sb_mask(d))
        lax.fori_loop(0, r * qb, lambda i, c: sweep(r * qb - 1 - i, c, None), carry)

        def back(kb, carry, past):
            out = []
            for hh in range(HP):
                dq, c_g = carry[hh]
                kk = _head_cols(k_ref, _sb_rows(kb), hh).astype(BF16)
                g, sig = g_s[hh, kb], sig_s[hh, kb].astype(F32)
                left = c_g + _lane_scan(g, tri_l)
                dz = g * (1.0 - sig) - left * sig
                if past is not None:
                    dz = jnp.where(past, dz, 0.0)
                dz = (dz * SCALE).astype(BF16)
                dk_s[_sb_rows(kb), hh * HEAD_DIM:(hh + 1) * HEAD_DIM] += _dot(dz, qv[hh], TN)
                out.append((dq + _dot(dz, kk), left[:, B - 1:B] + g[:, B - 1:B]))
            return tuple(out)

        init = tuple((jnp.zeros((Q, HEAD_DIM), F32), jnp.zeros((Q, 1), F32)) for _ in range(HP))
        res = lax.fori_loop(0, r * qb, lambda kb, c: back(kb, c, None), init)
        for d in reversed(range(r)):
            res = back(r * qb + r - 1 - d, res, _sb_mask(d))
        for hh in range(HP):
            dq_ref[:, hh * HEAD_DIM:(hh + 1) * HEAD_DIM] = res[hh][0].astype(BF16)

        @pl.when(qb == nq - 1)
        def _():
            dk_ref[...] = dk_s[...].astype(BF16)
            dv_ref[...] = dv_s[...].astype(BF16)

    blk = pl.BlockSpec((Q, HP * HEAD_DIM), lambda h, i: (i, h))
    G = n_heads // HP
    full = _head_spec(T, 0, HP, single=True)
    shp = jax.ShapeDtypeStruct((T, n_heads * HEAD_DIM), BF16)
    return pl.pallas_call(
        body, grid=(G, nq),
        in_specs=[blk, _head_spec(T, G, HP, single=True), _head_spec(T, 2 * G, HP, single=True), blk], out_specs=[blk, full, full],
        out_shape=[shp, shp, shp],
        scratch_shapes=[pltpu.VMEM((HP, T // B, Q, B), F32), pltpu.VMEM((HP, T // B, Q, B), BF16)] + [pltpu.VMEM((T, HP * HEAD_DIM), F32)] * 2,
        compiler_params=_params("parallel", "arbitrary"), name="sb_bwd")(proj, proj, proj, dy)


DIAGS = PBAND + PAIR


def _diag_onehot():
    d = lax.broadcasted_iota(jnp.int32, (DIAGS, 2 * PAIR), 0)
    r = lax.broadcasted_iota(jnp.int32, (DIAGS, 2 * PAIR), 1)
    return jnp.where(jnp.clip(d - PAIR - PAD, -REL_CLIP, CHUNK - 1) + REL_CLIP == r, 1.0, 0.0)


def _bias_expand(rel_bias):
    H = rel_bias.shape[0]
    table = jnp.pad(rel_bias, ((0, 0), (0, 2 * PAIR - N_REL)))

    def body(rb_ref, o_ref):
        o_ref[...] = lax.dot_general(rb_ref[...], _diag_onehot(), NT, precision=lax.Precision.HIGHEST, preferred_element_type=F32)

    per_diag = pl.pallas_call(body, out_shape=jax.ShapeDtypeStruct((H, DIAGS), F32), name="bias_expand")(table)
    flat = jnp.tile(jnp.pad(per_diag, ((0, 0), (0, 1))), (1, PAIR))[:, :PAIR * DIAGS]
    return flat.reshape(H, PAIR, DIAGS)[:, :, PAIR:]


def _bias_reduce(dbias):
    H = dbias.shape[0]
    padded = jnp.pad(dbias, ((0, 0), (0, 1), (PAIR, 0))).reshape(H, -1)
    skewed = padded[:, :PAIR * (DIAGS + 1)].reshape(H, PAIR, DIAGS + 1)[:, :, :DIAGS]

    def body(s_ref, o_ref):
        per_diag = jnp.sum(s_ref[...], axis=0, keepdims=True)
        o_ref[...] = lax.dot_general(jnp.broadcast_to(per_diag, (8, DIAGS)), _diag_onehot(), NN, precision=lax.Precision.HIGHEST,
                                     preferred_element_type=F32)[0:1]

    return pl.pallas_call(
        body, grid=(H,), in_specs=[pl.BlockSpec((None, PAIR, DIAGS), lambda h: (h, 0, 0))],
        out_specs=pl.BlockSpec((None, 1, 2 * PAIR), lambda h: (h, 0, 0)),
        out_shape=jax.ShapeDtypeStruct((H, 1, 2 * PAIR), F32), compiler_params=_params("parallel"), name="bias_reduce")(skewed)[:, 0]


CA_HEADS = 2


def _ca_mask():
    i = lax.broadcasted_iota(jnp.int32, (CA_ROWS, CA_BAND), 0)
    j = lax.broadcasted_iota(jnp.int32, (CA_ROWS, CA_BAND), 1)
    qc, kc = i // CHUNK, j // CHUNK
    return j, (kc >= qc) & (kc <= qc + LEFT_CHUNKS)


def _ca_bias(pair_bias):
    rows = []
    for q in range(CA_PAIRS):
        parts = [jnp.zeros((PAIR, q * PAIR), F32)] * (q > 0) + [pair_bias] + [jnp.zeros((PAIR, (CA_PAIRS - 1 - q) * PAIR), F32)] * (q < CA_PAIRS - 1)
        rows.append(jnp.concatenate(parts, axis=1) if len(parts) > 1 else parts[0])
    return jnp.concatenate(rows, axis=0)


def _ca_weights(pr, qp, kb, bias, j, window):
    valid = window & (pr * CA_ROWS + j >= PAD)
    z = jnp.where(valid, _dot(qp, kb, NT) * SCALE + bias, NEG)
    e = jnp.exp(z - jnp.max(z, axis=1, keepdims=True))
    return e / jnp.sum(e, axis=1, keepdims=True)


def _ca_fill(k_ref, v_ref, kpad, vpad):
    T, W = k_ref.shape
    kpad[0:PAD, :] = jnp.zeros((PAD, W), BF16)
    vpad[0:PAD, :] = jnp.zeros((PAD, W), BF16)
    kpad[PAD:PAD + T, :] = k_ref[...].astype(BF16)
    vpad[PAD:PAD + T, :] = v_ref[...].astype(BF16)


def _ca_fwd(proj, bias, n_heads, col0):
    T = proj.shape[0]
    HP = CA_HEADS
    G = n_heads // HP
    assert n_heads % HP == 0 and col0 % HP == 0

    def body(q_ref, k_ref, v_ref, b_ref, y_ref, kpad, vpad):
        _ca_fill(k_ref, v_ref, kpad, vpad)
        j, window = _ca_mask()
        bias = [_ca_bias(b_ref[hh]) for hh in range(HP)]

        def step(pr, _):
            r0 = pl.multiple_of(pr * CA_ROWS, CA_ROWS)
            for hh in range(HP):
                qp = _head_cols(q_ref, pl.ds(r0, CA_ROWS), hh).astype(BF16)
                kb = _head_cols(kpad, pl.ds(r0, CA_BAND), hh)
                vb = _head_cols(vpad, pl.ds(r0, CA_BAND), hh)
                w = _ca_weights(pr, qp, kb, bias[hh], j, window)
                y_ref[pl.ds(r0, CA_ROWS), hh * HEAD_DIM:(hh + 1) * HEAD_DIM] = _dot(w.astype(BF16), vb).astype(BF16)
            return 0

        lax.fori_loop(0, T // CA_ROWS, step, 0)

    c = col0 // HP
    return pl.pallas_call(
        body, grid=(G,),
        in_specs=[_head_spec(T, c, HP), _head_spec(T, c + G, HP), _head_spec(T, c + 2 * G, HP),
                  pl.BlockSpec((HP, PAIR, PBAND), lambda h: (h, 0, 0))],
        out_specs=_head_spec(T, 0, HP), out_shape=jax.ShapeDtypeStruct((T, n_heads * HEAD_DIM), BF16),
        scratch_shapes=[pltpu.VMEM((PAD + T, HP * HEAD_DIM), BF16)] * 2,
        compiler_params=_params("parallel"), name="ca_fwd")(proj, proj, proj, bias)


def _ca_bwd(proj, bias, dy, n_heads, col0):
    T = proj.shape[0]
    HP = CA_HEADS
    G = n_heads // HP

    def body(q_ref, k_ref, v_ref, b_ref, dy_ref, dq_ref, dk_ref, dv_ref, db_ref, kpad, vpad, dkpad, dvpad):
        _ca_fill(k_ref, v_ref, kpad, vpad)
        dkpad[...] = jnp.zeros_like(dkpad)
        dvpad[...] = jnp.zeros_like(dvpad)
        db_ref[...] = jnp.zeros_like(db_ref)
        j, window = _ca_mask()
        bias = [_ca_bias(b_ref[hh]) for hh in range(HP)]

        def step(pr, _):
            r0 = pl.multiple_of(pr * CA_ROWS, CA_ROWS)
            for hh in range(HP):
                cols = slice(hh * HEAD_DIM, (hh + 1) * HEAD_DIM)
                qp = _head_cols(q_ref, pl.ds(r0, CA_ROWS), hh).astype(BF16)
                kb = _head_cols(kpad, pl.ds(r0, CA_BAND), hh)
                vb = _head_cols(vpad, pl.ds(r0, CA_BAND), hh)
                w = _ca_weights(pr, qp, kb, bias[hh], j, window)
                dyp = _head_cols(dy_ref, pl.ds(r0, CA_ROWS), hh).astype(BF16)
                dw = _dot(dyp, vb, NT)
                dz = w * (dw - jnp.sum(dw * w, axis=1, keepdims=True))
                db_ref[hh] += sum(dz[q * PAIR:(q + 1) * PAIR, q * PAIR:q * PAIR + PBAND] for q in range(CA_PAIRS))
                dzs = (dz * SCALE).astype(BF16)
                dq_ref[pl.ds(r0, CA_ROWS), cols] = _dot(dzs, kb).astype(BF16)
                dkpad[pl.ds(r0, CA_BAND), cols] += _dot(dzs, qp, TN)
                dvpad[pl.ds(r0, CA_BAND), cols] += _dot(w.astype(BF16), dyp, TN)
            return 0

        lax.fori_loop(0, T // CA_ROWS, step, 0)
        dk_ref[...] = dkpad[PAD:PAD + T, :].astype(BF16)
        dv_ref[...] = dvpad[PAD:PAD + T, :].astype(BF16)

    c = col0 // HP
    full = _head_spec(T, 0, HP)
    bspec = pl.BlockSpec((HP, PAIR, PBAND), lambda h: (h, 0, 0))
    shp = jax.ShapeDtypeStruct((T, n_heads * HEAD_DIM), BF16)
    return pl.pallas_call(
        body, grid=(G,),
        in_specs=[_head_spec(T, c, HP), _head_spec(T, c + G, HP), _head_spec(T, c + 2 * G, HP), bspec, full],
        out_specs=[full, full, full, bspec],
        out_shape=[shp, shp, shp, jax.ShapeDtypeStruct((n_heads, PAIR, PBAND), F32)],
        scratch_shapes=[pltpu.VMEM((PAD + T, HP * HEAD_DIM), BF16)] * 2 + [pltpu.VMEM((PAD + T, HP * HEAD_DIM), F32)] * 2,
        compiler_params=_params("parallel"), name="ca_bwd")(proj, proj, proj, bias, dy)


def _local_step(x, p, target, comm, g):
    T, D = x.shape
    H = g["rel_bias"].shape[0]
    W = H * HEAD_DIM
    nb_in = comm.shapes["w_in_a"][2]
    nb_ff = comm.shapes["w_ffn_in"][2]
    nb_o = comm.shapes["w_sb_out"][2]
    nb_p = comm.shapes["w_ple_in"][2]
    tm = min(T, 1024)
    tn = min(D, 1024)
    gate_col = 6 * W // D

    h1 = _norm_fwd(x, g["g_mix"], "norm1")
    comm.stage("norm1", h1)
    proj = _mm(h1, comm.weight("w_in_a", h1), mode="nn", tm=tm, tn=nb_in, tk=D // 2, out_dtype=F32, b_blocked=True,
               after=comm.pending(), name="mm_in_a")
    comm.stage("mm_in_a", proj)
    proj = _mm(h1, comm.weight("w_in_b", proj), mode="nn", tm=tm, tn=nb_in, tk=D // 2, out_dtype=F32, b_blocked=True, a_cols=(1, 1),
               res=proj, after=comm.pending(), name="mm_in")
    comm.stage("mm_in", proj)
    y_sb = _sb_fwd(proj, H, comm.pending())
    bias = _bias_expand(g["rel_bias"])
    y_ca = _ca_fwd(proj, bias, H, 3 * H)
    w_sb_out = comm.weight("w_sb_out", (y_sb, y_ca))
    comm.stage("attention", y_sb, w_sb_out)
    a_sb, a_ca, merged = _mm_merge(y_sb, y_ca, w_sb_out, comm.weight("w_ca_out"), proj, gate_col, min(T, 512), tn, comm.pending())
    x2, h2 = _mm_fused(merged, comm.weight("w_mix_out"), [(x, 0), g["g_ffn"]], _residual_norm, [F32, BF16],
                       mode="nn", tm=min(T, 512), tn=D, tk=D, name="mm_mix")
    w_ffn_in = comm.weight("w_ffn_in", h2)
    comm.stage("mm_mix", w_ffn_in)
    gate, up, act = _mm_swiglu(h2, w_ffn_in, min(T, 512), comm.pending())
    F = act.shape[1]
    tkf = F // 2 if F % 256 == 0 else F
    x3 = _mm(act, comm.weight("w_ffn_out", act), mode="nn", tm=tm, tn=tn, tk=tkf, out_dtype=F32, res=x2, after=comm.pending(), name="mm_ffn_out")
    h3 = _norm_fwd(x3, g["g_ple"], "norm3")
    pb = _cast_bf16(p, "cast_p")
    P = p.shape[1]
    pe = _mm(pb, comm.weight("w_ple_in"), mode="nn", tm=tm, tn=tn, tk=P, out_dtype=F32, b_blocked=True, name="mm_ple_in")
    dx4, dpe, dzg, loss, dg_final = _mm_fused(
        h3, comm.weight("w_ple_gate"), [(x3, 0), (pe, 0), (target, 0), g["g_final"]], _tail, [F32, BF16, BF16],
        mode="nn", tm=min(T, 256), tn=D, tk=D, sums=[(1, 128), (1, D)], name="mm_ple_gate")

    tw = min(D, 1024)
    DW = BF16
    comm.grad("w_ple_in", _mm(pb, dpe, mode="tn", tm=P, tn=nb_p, tk=T, out_dtype=DW, out_block=nb_p, name="mm_d_ple_in"))
    comm.grad("w_ple_gate", _mm(h3, dzg, mode="tn", tm=tw, tn=tn, tk=T, out_dtype=DW, name="mm_d_ple_gate"))
    dx3, dx3b, dg_ple = _mm_fused(dzg, comm.weight("w_ple_gate"), [(x3, 0), (dx4, 0), g["g_ple"]], _residual_norm_bwd, [F32, BF16],
                                  mode="nt", tm=min(T, 256), tn=D, tk=D, sums=[(1, D)], after=comm.pending(), name="mm_dh3")
    comm.grad("w_ffn_out", _mm(act, dx3b, mode="tn", tm=F // 4, tn=tn, tk=T, out_dtype=DW, name="mm_d_ffn_out"))
    dgate, dup = _mm_fused(dx3b, comm.weight("w_ffn_out"), [(gate, 0), (up, 0)], _swiglu_bwd, [BF16, BF16],
                           mode="nt", tm=min(T, 512), tn=nb_ff, tk=D, after=comm.pending(), b_outer=True, name="mm_dact")
    half = comm.shapes["w_ffn_in"][0] // 2
    d_ffn_in = _mm(h2, dgate, mode="tn", tm=tw, tn=nb_ff, tk=T, out_dtype=DW, out_block=nb_ff, o_count=2 * half, name="mm_d_ffn_in_gate")
    comm.grad("w_ffn_in", _mm(h2, dup, mode="tn", tm=tw, tn=nb_ff, tk=T, out_dtype=DW, out_block=nb_ff, o_first=half, o_count=2 * half,
                              into=d_ffn_in, name="mm_d_ffn_in"))
    dh2 = _mm(dgate, comm.weight("w_ffn_in"), mode="nt", tm=tm, tn=D, tk=nb_ff, out_dtype=F32, b_blocked=True, b_count=half,
              after=comm.pending(), name="mm_dh2_gate")
    dh2 = _mm(dup, comm.weight("w_ffn_in"), mode="nt", tm=min(T, 512), tn=D, tk=nb_ff, out_dtype=F32, b_blocked=True, b_first=half, b_count=half,
              res=dh2, name="mm_dh2")
    dx2, dx2b, dg_ffn = _norm_bwd(dh2, x2, g["g_ffn"], dx3, "norm2_bwd")
    per = D // tn
    da_sb, da_ca, dgate_sb, dgate_ca = _mm_fused(
        dx2b, comm.weight("w_mix_out"), [(proj, gate_col * per), (proj, (gate_col + 1) * per), (a_sb, 0), (a_ca, 0)], _merge_bwd, [BF16] * 4,
        mode="nt", tm=min(T, 512), tn=tn, tk=D, name="mm_dmerged")
    comm.grad("w_mix_out", _mm(merged, dx2b, mode="tn", tm=tw, tn=tn, tk=T, out_dtype=DW, name="mm_d_mix"))
    comm.grad("w_sb_out", _mm(y_sb, da_sb, mode="tn", tm=min(W, 512), tn=tn, tk=T, out_dtype=DW, out_block=nb_o, name="mm_d_sb_out"))
    comm.grad("w_ca_out", _mm(y_ca, da_ca, mode="tn", tm=min(W, 512), tn=tn, tk=T, out_dtype=DW, out_block=nb_o, name="mm_d_ca_out"))
    dy_sb = _mm(da_sb, comm.weight("w_sb_out"), mode="nt", tm=tm, tn=W, tk=tn, out_dtype=BF16, b_blocked=True, after=comm.pending(), name="mm_dy_sb")
    dy_ca = _mm(da_ca, comm.weight("w_ca_out"), mode="nt", tm=tm, tn=W, tk=tn, out_dtype=BF16, b_blocked=True, name="mm_dy_ca")
    dq_sb, dk_sb, dv_sb = _sb_bwd(proj, dy_sb, H)
    dq_ca, dk_ca, dv_ca, dbias = _ca_bwd(proj, bias, dy_ca, H, 3 * H)
    d_rel = _bias_reduce(dbias)[:, :N_REL]
    dproj = jnp.concatenate([dq_sb, dk_sb, dv_sb, dq_ca, dk_ca, dv_ca, dgate_sb, dgate_ca], axis=1)
    comm.grad("w_in", _mm(h1, dproj, mode="tn", tm=tw, tn=nb_in, tk=T, out_dtype=DW, out_block=nb_in, name="mm_d_in"))
    dh1 = _mm(dproj, comm.weight("w_in_a"), mode="nt", tm=tm, tn=D // 2, tk=nb_in, out_dtype=F32, b_blocked=True, out_cols=(2, 1, 0),
              after=comm.pending(), name="mm_dh1_a")
    comm.pair_done(dh1)
    dh1 = _mm(dproj, comm.weight("w_in_b"), mode="nt", tm=tm, tn=D // 2, tk=nb_in, out_dtype=F32, b_blocked=True, out_cols=(2, 1, 1),
              into=dh1, after=comm.pending(), name="mm_dh1")
    grad_x, _, dg_mix = _norm_bwd(dh1, x, g["g_mix"], dx2, "norm1_bwd")
    small = dict(g_mix=dg_mix, g_ffn=dg_ffn, g_ple=dg_ple, g_final=dg_final, rel_bias=d_rel)
    return loss, grad_x, small


def _position():
    x, y, c = lax.axis_index("x"), lax.axis_index("y"), lax.axis_index("c")
    return x, y, c


def _block_of(px, py, pc):
    return 4 * px + 2 * py + pc


def _flip(pos, k):
    x, y, c = pos
    return (1 - x if k & 4 else x, 1 - y if k & 2 else y, 1 - c if k & 1 else c)


HBM = pl.BlockSpec(memory_space=pltpu.HBM)
SEM = pl.BlockSpec(memory_space=pltpu.SEMAPHORE)
VMEM_SPEC = pl.BlockSpec(memory_space=pltpu.VMEM)
EFFECT = pltpu.SideEffectType.DATAFLOW_SIDE_EFFECTING
TOKEN = jax.ShapeDtypeStruct((8, 128), F32)


def _hbm(a):
    return pltpu.HBM(a.shape, a.dtype)


def _landing(shape, dtype):
    return pltpu.with_memory_space_constraint(lax.empty(shape, dtype), pltpu.HBM)


def _gather_start(lands, after, name, relay):
    n = len(lands)

    def body(*refs):
        ins = refs[:n]
        send, recv = refs[n + 1], refs[n + 2]
        token = refs[-1]
        x, y, c = _position()
        mine = _block_of(x, y, c)
        peers = [(x, y, 1 - c), (1 - x, y, c), (x, 1 - y, c), (1 - x, 1 - y, c)][:3 if relay else 4]
        for wi in range(n):
            for k, peer in enumerate(peers):
                pltpu.make_async_remote_copy(
                    src_ref=ins[wi].at[mine], dst_ref=ins[wi].at[mine], send_sem=send.at[4 * wi + k], recv_sem=recv.at[4 * wi + k],
                    device_id=peer, device_id_type=MESH).start()
        token[...] = jnp.zeros_like(token)

    outs = pl.pallas_call(
        body, name=name, in_specs=[HBM] * n + [ANY], out_specs=(SEM, SEM, *[HBM] * n, VMEM_SPEC),
        out_shape=(pltpu.SemaphoreType.DMA((4 * n,)), pltpu.SemaphoreType.DMA((4 * n,)), *[_hbm(a) for a in lands], TOKEN),
        input_output_aliases={i: 2 + i for i in range(n)},
        compiler_params=pltpu.CompilerParams(has_side_effects=EFFECT))(*[pltpu.with_memory_space_constraint(a, pltpu.HBM) for a in lands], after)
    return outs[0], outs[1], list(outs[2:2 + n]), outs[-1]


def _relays(ref, wi, send2, recv2, pos, received):
    x, y, c = pos
    half = ref.shape[1] // 2
    out = []
    for h, (origin, to) in enumerate((((1 - x, y, c), (x, 1 - y, c)), ((x, 1 - y, c), (1 - x, y, c)))):
        rows = ref.at[_block_of(1 - x, 1 - y, c) if received else _block_of(*origin), pl.ds(h * half, half)]
        out.append(pltpu.make_async_remote_copy(src_ref=rows, dst_ref=rows, send_sem=send2.at[2 * wi + h], recv_sem=recv2.at[2 * wi + h],
                                                device_id=to, device_id_type=MESH))
    return out


def _gather_forward(lands, send0, recv0, after, name, relay):
    n = len(lands)

    def body(*refs):
        ins = refs[:n]
        send0, recv0 = refs[n], refs[n + 1]
        send1, recv1, send2, recv2 = refs[n + 2 + len(after):n + 6 + len(after)]
        token = refs[-1]
        x, y, c = _position()
        chips = [(1 - x, y), (x, 1 - y), (1 - x, 1 - y)][:2 if relay else 3]
        for wi in range(n):
            for j, chip in enumerate(chips):
                rows = ins[wi].at[_block_of(*chip, c)]
                pltpu.make_async_remote_copy(
                    src_ref=rows, dst_ref=rows, send_sem=send0.at[4 * wi + 1 + j], recv_sem=recv0.at[4 * wi + 1 + j],
                    device_id=(*chip, c), device_id_type=MESH).wait_recv()
                pltpu.make_async_remote_copy(
                    src_ref=rows, dst_ref=rows, send_sem=send1.at[3 * wi + j], recv_sem=recv1.at[3 * wi + j],
                    device_id=(x, y, 1 - c), device_id_type=MESH).start()
            if relay:
                for sent in _relays(ins[wi], wi, send2, recv2, (x, y, c), False):
                    sent.start()
        token[...] = jnp.zeros_like(token)

    outs = pl.pallas_call(
        body, name=name, in_specs=[HBM] * n + [SEM, SEM] + [ANY] * len(after), out_specs=(SEM, SEM, SEM, SEM, *[HBM] * n, VMEM_SPEC),
        out_shape=(pltpu.SemaphoreType.DMA((3 * n,)), pltpu.SemaphoreType.DMA((3 * n,)), pltpu.SemaphoreType.DMA((2 * n,)),
                   pltpu.SemaphoreType.DMA((2 * n,)), *[_hbm(a) for a in lands], TOKEN),
        input_output_aliases={i: 4 + i for i in range(n)},
        compiler_params=pltpu.CompilerParams(has_side_effects=EFFECT))(*lands, send0, recv0, *after)
    return outs[:4], list(outs[4:4 + n]), outs[-1]


def _gather_far(lands, send1, recv1, send2, recv2, after, name):
    n = len(lands)

    def body(*refs):
        ins = refs[:n]
        send1, recv1, send2, recv2 = refs[n:n + 4]
        token = refs[-1]
        x, y, c = _position()
        for wi in range(n):
            for sent, got in zip(_relays(ins[wi], wi, send2, recv2, (x, y, c), False), _relays(ins[wi], wi, send2, recv2, (x, y, c), True)):
                sent.wait_send()
                got.wait_recv()
            far = ins[wi].at[_block_of(1 - x, 1 - y, c)]
            pltpu.make_async_remote_copy(src_ref=far, dst_ref=far, send_sem=send1.at[3 * wi + 2], recv_sem=recv1.at[3 * wi + 2],
                                         device_id=(x, y, 1 - c), device_id_type=MESH).start()
        token[...] = jnp.zeros_like(token)

    outs = pl.pallas_call(
        body, name=name, in_specs=[HBM] * n + [SEM] * 4 + [ANY] * len(after), out_specs=(*[HBM] * n, VMEM_SPEC),
        out_shape=(*[_hbm(a) for a in lands], TOKEN), input_output_aliases={i: i for i in range(n)},
        compiler_params=pltpu.CompilerParams(has_side_effects=EFFECT))(*lands, send1, recv1, send2, recv2, *after)
    return list(outs[:n]), outs[-1]


def _gather_wait(lands, send0, recv0, send1, recv1, send2, recv2, after, name, relay):
    n = len(lands)

    def body(*refs):
        ins = refs[:n]
        send0, recv0, send1, recv1, send2, recv2 = refs[n:n + 6]
        x, y, c = _position()
        mine = _block_of(x, y, c)
        sibling = (x, y, 1 - c)
        peers = [sibling, (1 - x, y, c), (x, 1 - y, c), (1 - x, 1 - y, c)][:3 if relay else 4]
        chips = [(1 - x, y), (x, 1 - y), (1 - x, 1 - y)]
        for wi in range(n):
            own = ins[wi].at[mine]
            for k, peer in enumerate(peers):
                pltpu.make_async_remote_copy(src_ref=own, dst_ref=own, send_sem=send0.at[4 * wi + k], recv_sem=recv0.at[4 * wi + k],
                                             device_id=peer, device_id_type=MESH).wait_send()
            theirs = ins[wi].at[_block_of(*sibling)]
            pltpu.make_async_remote_copy(src_ref=theirs, dst_ref=theirs, send_sem=send0.at[4 * wi], recv_sem=recv0.at[4 * wi],
                                         device_id=sibling, device_id_type=MESH).wait_recv()
            for j, chip in enumerate(chips):
                sent = ins[wi].at[_block_of(*chip, c)]
                got = ins[wi].at[_block_of(*chip, 1 - c)]
                pltpu.make_async_remote_copy(src_ref=sent, dst_ref=sent, send_sem=send1.at[3 * wi + j], recv_sem=recv1.at[3 * wi + j],
                                             device_id=sibling, device_id_type=MESH).wait_send()
                pltpu.make_async_remote_copy(src_ref=got, dst_ref=got, send_sem=send1.at[3 * wi + j], recv_sem=recv1.at[3 * wi + j],
                                             device_id=sibling, device_id_type=MESH).wait_recv()

    outs = pl.pallas_call(
        body, name=name, in_specs=[HBM] * n + [SEM] * 6 + [ANY], out_specs=tuple([HBM] * n),
        out_shape=tuple(_hbm(a) for a in lands), input_output_aliases={i: i for i in range(n)},
        compiler_params=pltpu.CompilerParams(has_side_effects=EFFECT))(*lands, send0, recv0, send1, recv1, send2, recv2, after)
    return list(outs)


def _plan_direct(me):
    return [(_block_of(*_flip(me, k)), k - 1, _flip(me, k)) for k in range(1, N_DEV)]


def _plan_sibling(me):
    x, y, c = me
    return [(_block_of(ci // 2, ci % 2, 1 - c), ci, (x, y, 1 - c)) for ci in range(4)]


def _plan_chips(me):
    x, y, c = me
    out = []
    for k in range(1, 4):
        px, py = (1 - x if k & 2 else x), (1 - y if k & 1 else y)
        out.append((2 * px + py, k - 1, (px, py, c)))
    return out


def _exchange_start(blocks, plan, name):
    n = len(blocks)
    slots = len(plan((0, 0, 0)))

    def body(*refs):
        srcs, lands = refs[:n], refs[n:2 * n]
        send, recv = refs[2 * n], refs[2 * n + 1]
        token = refs[-1]
        for wi in range(n):
            for block, slot, peer in plan(_position()):
                pltpu.make_async_remote_copy(
                    src_ref=srcs[wi].at[block], dst_ref=lands[wi].at[slot], send_sem=send.at[slots * wi + slot],
                    recv_sem=recv.at[slots * wi + slot], device_id=peer, device_id_type=MESH).start()
        token[...] = jnp.zeros_like(token)

    zones = [_landing((slots,) + b.shape[1:], b.dtype) for b in blocks]
    outs = pl.pallas_call(
        body, name=name, in_specs=[HBM] * (2 * n), out_specs=(SEM, SEM, *[HBM] * (2 * n), VMEM_SPEC),
        out_shape=(pltpu.SemaphoreType.DMA((slots * n,)), pltpu.SemaphoreType.DMA((slots * n,)), *[_hbm(a) for a in blocks],
                   *[_hbm(z) for z in zones], TOKEN),
        input_output_aliases={i: 2 + i for i in range(2 * n)},
        compiler_params=pltpu.CompilerParams(has_side_effects=EFFECT))(
            *[pltpu.with_memory_space_constraint(b, pltpu.HBM) for b in blocks], *zones)
    return outs[0], outs[1], list(outs[2:2 + n]), list(outs[2 + n:2 + 2 * n]), outs[-1]


def _exchange_wait(groups, plan, after, name):
    flat, counts = [], []
    for send, recv, blocks, zones in groups:
        flat += [*blocks, *zones, send, recv]
        counts.append(len(blocks))
    slots = len(plan((0, 0, 0)))

    def body(*refs):
        pos = 0
        for n in counts:
            srcs, lands = refs[pos:pos + n], refs[pos + n:pos + 2 * n]
            send, recv = refs[pos + 2 * n], refs[pos + 2 * n + 1]
            pos += 2 * n + 2
            for wi in range(n):
                for block, slot, peer in plan(_position()):
                    cp = pltpu.make_async_remote_copy(
                        src_ref=srcs[wi].at[block], dst_ref=lands[wi].at[slot], send_sem=send.at[slots * wi + slot],
                        recv_sem=recv.at[slots * wi + slot], device_id=peer, device_id_type=MESH)
                    cp.wait_send()
                    cp.wait_recv()

    in_specs, out_specs, out_shape, aliases = [], [], [], {}
    i = 0
    for n, (send, recv, blocks, zones) in zip(counts, groups):
        for a in (*blocks, *zones):
            aliases[i] = len(out_shape)
            in_specs.append(HBM)
            out_specs.append(HBM)
            out_shape.append(_hbm(a))
            i += 1
        in_specs += [SEM, SEM]
        i += 2
    outs = pl.pallas_call(
        body, name=name, in_specs=in_specs + [ANY], out_specs=tuple(out_specs), out_shape=tuple(out_shape),
        input_output_aliases=aliases, compiler_params=pltpu.CompilerParams(has_side_effects=EFFECT))(*flat, after)
    res, pos = [], 0
    for n in counts:
        res.append((list(outs[pos:pos + n]), list(outs[pos + n:pos + 2 * n])))
        pos += 2 * n
    return res


def _sibling_sum(blocks, zone, core, name):
    _, R, C = zone.shape
    rt = next(r for r in (R, R // 2, R // 4, 128, 64) if R % r == 0 and r % 16 == 0 and r * C <= 4 * 1024 * 1024)

    def body(core_ref, own_ref, z_ref, o_ref):
        o_ref[...] = (own_ref[...].astype(F32) + z_ref[...].astype(F32)).astype(o_ref.dtype)

    grid_spec = pltpu.PrefetchScalarGridSpec(
        num_scalar_prefetch=1, grid=(4, R // rt),
        in_specs=[pl.BlockSpec((None, rt, C), lambda ci, i, core_ref: (2 * ci + core_ref[0], i, 0)),
                  pl.BlockSpec((None, rt, C), lambda ci, i, core_ref: (ci, i, 0))],
        out_specs=pl.BlockSpec((None, rt, C), lambda ci, i, core_ref: (ci, i, 0)))
    return pl.pallas_call(body, grid_spec=grid_spec, out_shape=jax.ShapeDtypeStruct(zone.shape, zone.dtype),
                          compiler_params=_params("parallel", "parallel"), name=name)(core, blocks, zone)


def _adamw(w, g, m, v):
    m = ADAM_B1 * m + (1.0 - ADAM_B1) * g
    v = ADAM_B2 * v + (1.0 - ADAM_B2) * (g * g)
    m_hat = m / (1.0 - ADAM_B1 ** ADAM_STEP)
    v_hat = v / (1.0 - ADAM_B2 ** ADAM_STEP)
    delta = -ADAM_LR * (m_hat / (jnp.sqrt(v_hat) + ADAM_EPS) + ADAM_WD * w)
    return delta, m, v


def _reduce_adamw(blocks, zone, mine, w, m, v, name):
    R, C = w.shape
    rt = next(r for r in (256, 128, 64) if R % r == 0 and r * C <= 512 * 1024)

    def body(mine_ref, own_ref, z_ref, w_ref, m_ref, v_ref, g_out, d_out, m_out, v_out):
        g = own_ref[...].astype(F32)
        for s in range(zone.shape[0]):
            g = g + z_ref[s].astype(F32)
        delta, m2, v2 = _adamw(w_ref[...], g, m_ref[...], v_ref[...])
        g_out[...] = g
        d_out[...] = delta
        m_out[...] = m2
        v_out[...] = v2

    spec = pl.BlockSpec((rt, C), lambda i, mine_ref: (i, 0))
    grid_spec = pltpu.PrefetchScalarGridSpec(
        num_scalar_prefetch=1, grid=(R // rt,),
        in_specs=[pl.BlockSpec((None, rt, C), lambda i, mine_ref: (mine_ref[0], i, 0)),
                  pl.BlockSpec((zone.shape[0], rt, C), lambda i, mine_ref: (0, i, 0)), spec, spec, spec],
        out_specs=[spec] * 4)
    return pl.pallas_call(body, grid_spec=grid_spec, out_shape=[jax.ShapeDtypeStruct((R, C), F32)] * 4,
                          compiler_params=_params("parallel"), name=name)(mine, blocks, zone, w, m, v)


def _small_step(part, w, m, v, after):
    R, C = part.shape

    def body(part_ref, w_ref, m_ref, v_ref, *rest):
        g_out, d_out, m_out, v_out, gath, send, recv = rest[len(after):]
        me = _position()
        gath[_block_of(*me)] = part_ref[...]

        def copy(k, slot):
            return pltpu.make_async_remote_copy(
                src_ref=part_ref, dst_ref=gath.at[slot], send_sem=send.at[k - 1], recv_sem=recv.at[k - 1],
                device_id=_flip(me, k), device_id_type=MESH)

        sent = [copy(k, _block_of(*me)) for k in range(1, N_DEV)]
        for cp in sent:
            cp.start()
        for k in range(1, N_DEV):
            copy(k, _block_of(*_flip(me, k))).wait_recv()
        for cp in sent:
            cp.wait_send()
        g = gath[0]
        for s in range(1, N_DEV):
            g = g + gath[s]
        delta, m2, v2 = _adamw(w_ref[...], g, m_ref[...], v_ref[...])
        g_out[...] = g
        d_out[...] = delta
        m_out[...] = m2
        v_out[...] = v2

    vm = pl.BlockSpec(memory_space=pltpu.VMEM)
    return pl.pallas_call(
        body, in_specs=[vm] * 4 + [ANY] * len(after), out_specs=[vm] * 4, out_shape=[jax.ShapeDtypeStruct((R, C), F32)] * 4,
        scratch_shapes=[pltpu.VMEM((N_DEV, R, C), F32), pltpu.SemaphoreType.DMA((7,)), pltpu.SemaphoreType.DMA((7,))],
        name="small_step")(part, w, m, v, *after)


COLUMN_SHARDED = ("w_in", "w_sb_out", "w_ca_out", "w_ffn_in", "w_ple_in")
ROW_SHARDED = ("w_mix_out", "w_ffn_out", "w_ple_gate")
BIG = COLUMN_SHARDED + ROW_SHARDED
SMALL = ("g_mix", "g_ffn", "g_ple", "g_final", "rel_bias")
WEIGHTS = ("w_in", "w_sb_out", "w_ca_out", "w_mix_out", "rel_bias", "g_mix", "g_ffn", "g_ple", "g_final",
           "w_ffn_in", "w_ffn_out", "w_ple_in", "w_ple_gate")


def _pack_small(t, D):
    rows = [t[n].reshape(1, D) for n in SMALL[:4]]
    rb = t["rel_bias"].reshape(1, -1)
    rows.append(jnp.pad(rb, ((0, 0), (0, D - rb.shape[1]))))
    return jnp.concatenate(rows + [jnp.zeros((8 - len(rows), D), F32)], axis=0)


def _unpack_small(a, like):
    out = {n: a[i].reshape(like[n].shape) for i, n in enumerate(SMALL[:4])}
    out["rel_bias"] = a[4, :like["rel_bias"].size].reshape(like["rel_bias"].shape)
    return out


GATHER_GROUPS = (("w_in_a",), ("w_in_b",), ("w_sb_out", "w_ca_out", "w_mix_out"), ("w_ffn_in",), ("w_ffn_out", "w_ple_gate", "w_ple_in"))
FORWARD_AFTER = ("norm1", "mm_in_a", "mm_in", "attention", "mm_mix")
RELAYED = (False, False, True, True, True)
GRAD_GROUPS = (("w_ple_in", "w_ple_gate"), ("w_ffn_out",), ("w_ffn_in",), ("w_mix_out", "w_sb_out", "w_ca_out"), ("w_in",))


class _Exchange:
    def __init__(self, shards):
        me = _position()
        self.mine = _block_of(*me)
        self.chip = jnp.reshape(2 * me[0] + me[1], (1,)).astype(jnp.int32)
        self.core = jnp.reshape(me[2], (1,)).astype(jnp.int32)
        self.device = jnp.reshape(self.mine, (1,)).astype(jnp.int32)
        self.shapes = {n: ((N_DEV * s.shape[0], s.shape[1]) if n in ROW_SHARDED else (N_DEV,) + s.shape) for n, s in shards.items()}
        self.tokens = []
        self.ready = {}
        self.gathers = []
        for gi, names in enumerate(GATHER_GROUPS):
            lands = [lax.dynamic_update_slice(lax.empty((N_DEV,) + shards[n].shape, BF16), shards[n][None], (self.mine, 0, 0))
                     for n in names]
            behind = self.tokens[-1] if self.tokens else shards[names[0]]
            send0, recv0, lands, token = _gather_start(lands, behind, f"gather_start_{gi}", RELAYED[gi])
            self.tokens.append(token)
            self.gathers.append(dict(names=names, lands=lands, sems=(send0, recv0), token=token))
        self.grads = {}
        self.exchanges = []

    def pending(self):
        tokens, self.tokens = self.tokens, []
        return tokens

    def stage(self, tag, *made):
        gi = FORWARD_AFTER.index(tag)
        gth = self.gathers[gi]
        sems, lands, token = _gather_forward(gth["lands"], *gth["sems"], made + tuple(self.tokens), f"gather_forward_{gi}", RELAYED[gi])
        gth.update(lands=lands, sems=gth["sems"] + tuple(sems), token=token)
        self.tokens.append(token)

    def weight(self, name, after=None):
        if name not in self.ready:
            gi = next(i for i, names in enumerate(GATHER_GROUPS) if name in names)
            gth = self.gathers[gi]
            after = gth["token"] if after is None else after
            if RELAYED[gi]:
                gth["lands"], after = _gather_far(gth["lands"], *gth["sems"][2:], after if isinstance(after, tuple) else (after,), f"gather_far_{gi}")
            for n, a in zip(gth["names"], _gather_wait(gth["lands"], *gth["sems"], after, f"gather_wait_{gi}", RELAYED[gi])):
                self.ready[n] = a.reshape(self.shapes[n])
        return self.ready[name]

    def grad(self, name, blocks):
        self.grads[name] = blocks if name in COLUMN_SHARDED else blocks.reshape((N_DEV, -1, blocks.shape[-1]))
        names = next(names for names in GRAD_GROUPS if name in names)
        if not all(n in self.grads for n in names):
            return
        blocks = [self.grads[n] for n in names]
        if names == GRAD_GROUPS[-1]:
            send, recv, blocks, zones, token = _exchange_start(blocks, _plan_sibling, "pair_start_" + names[0])
            self.pair = (send, recv, blocks, zones)
        else:
            send, recv, blocks, zones, token = _exchange_start(blocks, _plan_direct, "exchange_start_" + names[0])
            self.exchanges.append(dict(names=names, state=(send, recv, blocks, zones), plan=_plan_direct, own=self.device))
        self.tokens.append(token)

    def pair_done(self, after):
        names = GRAD_GROUPS[-1]
        (blocks, zones), = _exchange_wait([self.pair], _plan_sibling, after, "pair_wait_" + names[0])
        blocks = [_sibling_sum(b, z, self.core, "pair_sum_" + n) for n, b, z in zip(names, blocks, zones)]
        send, recv, blocks, zones, token = _exchange_start(blocks, _plan_chips, "exchange_start_" + names[0])
        self.exchanges.append(dict(names=names, state=(send, recv, blocks, zones), plan=_plan_chips, own=self.chip))
        self.tokens.append(token)

    def collect(self, which, after, name):
        sel = [e for e in self.exchanges if GRAD_GROUPS.index(e["names"]) in which]
        out = {}
        for e, (blocks, zones) in zip(sel, _exchange_wait([e["state"] for e in sel], sel[0]["plan"], after, name)):
            out.update({n: (b, e["own"], z) for n, b, z in zip(e["names"], blocks, zones)})
        return out


def kernel(x, p, w_in, w_sb_out, w_ca_out, w_mix_out, rel_bias, g_mix, g_ffn, g_ple, g_final, w_ffn_in, w_ffn_out, w_ple_in, w_ple_gate, loss_target, m_w_in, m_w_sb_out, m_w_ca_out, m_w_mix_out, m_rel_bias, m_g_mix, m_g_ffn, m_g_ple, m_g_final, m_w_ffn_in, m_w_ffn_out, m_w_ple_in, m_w_ple_gate, v_w_in, v_w_sb_out, v_w_ca_out, v_w_mix_out, v_rel_bias, v_g_mix, v_g_ffn, v_g_ple, v_g_final, v_w_ffn_in, v_w_ffn_out, v_w_ple_in, v_w_ple_gate):
    wts = dict(w_in=w_in, w_sb_out=w_sb_out, w_ca_out=w_ca_out, w_mix_out=w_mix_out, rel_bias=rel_bias, g_mix=g_mix, g_ffn=g_ffn,
               g_ple=g_ple, g_final=g_final, w_ffn_in=w_ffn_in, w_ffn_out=w_ffn_out, w_ple_in=w_ple_in, w_ple_gate=w_ple_gate)
    mom = dict(w_in=m_w_in, w_sb_out=m_w_sb_out, w_ca_out=m_w_ca_out, w_mix_out=m_w_mix_out, rel_bias=m_rel_bias, g_mix=m_g_mix,
               g_ffn=m_g_ffn, g_ple=m_g_ple, g_final=m_g_final, w_ffn_in=m_w_ffn_in, w_ffn_out=m_w_ffn_out, w_ple_in=m_w_ple_in,
               w_ple_gate=m_w_ple_gate)
    var = dict(w_in=v_w_in, w_sb_out=v_w_sb_out, w_ca_out=v_w_ca_out, w_mix_out=v_w_mix_out, rel_bias=v_rel_bias, g_mix=v_g_mix,
               g_ffn=v_g_ffn, g_ple=v_g_ple, g_final=v_g_final, w_ffn_in=v_w_ffn_in, w_ffn_out=v_w_ffn_out, w_ple_in=v_w_ple_in,
               w_ple_gate=v_w_ple_gate)
    T, D = x.shape[1], x.shape[2]
    shard = {n: wts[n].reshape(wts[n].shape[-2:]) for n in BIG}
    bf = {n: _cast_bf16(shard[n], "cast_" + n) for n in BIG}
    half = bf["w_in"].shape[0] // 2
    bf["w_in_a"], bf["w_in_b"] = bf["w_in"][:half], bf.pop("w_in")[half:]
    comm = _Exchange(bf)
    g = dict(g_mix=g_mix.reshape(1, D), g_ffn=g_ffn.reshape(1, D), g_ple=g_ple.reshape(1, D), g_final=g_final.reshape(1, D),
             rel_bias=rel_bias.reshape(rel_bias.shape[-2:]))

    loss, grad_x, dsmall = _local_step(x.reshape(T, D), p.reshape(T, -1), loss_target.reshape(T, D), comm, g)
    loss = lax.psum(loss[0, 0], ("x", "y", "c"))

    grad, delta, new_m, new_v = {}, {}, {}, {}

    def update(parts):
        done = []
        for n, (blocks, own, zone) in parts.items():
            outs = _reduce_adamw(blocks, zone, own, shard[n], mom[n].reshape(shard[n].shape), var[n].reshape(shard[n].shape), "adamw_" + n)
            grad[n], delta[n], new_m[n], new_v[n] = [o.reshape(wts[n].shape) for o in outs]
            done.append(outs[0])
        return done

    done = update(comm.collect(range(len(GRAD_GROUPS) - 1), grad_x, "exchange_wait_rest"))
    outs = _small_step(_pack_small(dsmall, D), _pack_small(wts, D), _pack_small(mom, D), _pack_small(var, D), done)
    for dst, a in zip((grad, delta, new_m, new_v), outs):
        dst.update(_unpack_small(a, wts))
    update(comm.collect([len(GRAD_GROUPS) - 1], outs[0], "exchange_wait_w_in"))

    return (loss, grad_x.reshape(x.shape), *[grad[n] for n in WEIGHTS], *[delta[n] for n in WEIGHTS],
            *[new_m[n] for n in WEIGHTS], *[new_v[n] for n in WEIGHTS])
```

```python
import functools

import jax
import jax.numpy as jnp
from jax import lax
from jax.experimental import pallas as pl
from jax.experimental.pallas import tpu as pltpu

F32, BF16 = jnp.float32, jnp.bfloat16

N_DEV = 8
HEAD_DIM = 128
CHUNK = 64
LEFT_CHUNKS = 8
REL_CLIP = 128
N_REL = REL_CLIP + CHUNK
PAIR = 2 * CHUNK
PBAND = (LEFT_CHUNKS + 2) * CHUNK
CA_PAIRS = 2
CA_ROWS = CA_PAIRS * PAIR
CA_BAND = PBAND + CA_ROWS - PAIR
PAD = LEFT_CHUNKS * CHUNK
SB_BLOCK = 256
ROWS = 256
EPS = 1e-6
NEG = -1e30
SCALE = HEAD_DIM ** -0.5
VMEM_LIMIT_BYTES = 56 * 1024 * 1024

ADAM_LR, ADAM_B1, ADAM_B2, ADAM_EPS, ADAM_WD, ADAM_STEP = 0.001, 0.9, 0.999, 1e-08, 0.01, 10

ANY = pl.BlockSpec(memory_space=pl.ANY)
NN = (((1,), (0,)), ((), ()))
NT = (((1,), (1,)), ((), ()))
TN = (((0,), (0,)), ((), ()))
MESH = pl.DeviceIdType.MESH


def _params(*sem):
    return pltpu.CompilerParams(dimension_semantics=sem or None, vmem_limit_bytes=VMEM_LIMIT_BYTES)


def _dot(a, b, dims=NN):
    return lax.dot_general(a, b, dims, preferred_element_type=F32)


def _mm(a, b, *, mode, tm, tn, tk, out_dtype, name, b_blocked=False, out_block=None, res=None, after=(), a_cols=(1, 0), out_cols=(1, 1, 0), into=None, b_first=0, b_count=None, o_first=0, o_count=None):
    bg = og = 1
    if mode == "nn":
        M, K = a.shape
        a_spec = pl.BlockSpec((tm, tk), lambda i, j, k: (i, k))
        if b_blocked:
            G, _, nb = b.shape
            N = G * nb
            if tn > nb:
                bg = tn // nb
                assert tn % nb == 0
                b_spec = pl.BlockSpec((bg, tk, nb), lambda i, j, k: (j, k, 0))
            else:
                per = nb // tn
                assert nb % tn == 0
                b_spec = pl.BlockSpec((None, tk, tn), lambda i, j, k: (j // per, k, j % per))
        else:
            N = b.shape[1]
            b_spec = pl.BlockSpec((tk, tn), lambda i, j, k: (k, j))
        dims = NN
    elif mode == "nt":
        M, K = a.shape
        a_spec = pl.BlockSpec((tm, tk), lambda i, j, k: (i, k))
        if b_blocked:
            G, N, nb = b.shape
            K = (b_count or G) * nb
            assert b_first == 0 or tk == nb
            if tk > nb:
                bg = tk // nb
                assert tk % nb == 0
                b_spec = pl.BlockSpec((bg, tn, nb), lambda i, j, k: (k, j, 0))
            else:
                per = nb // tk
                assert nb % tk == 0
                b_spec = pl.BlockSpec((None, tn, tk), lambda i, j, k: (b_first + k // per, j, k % per))
        else:
            N = b.shape[0]
            b_spec = pl.BlockSpec((tn, tk), lambda i, j, k: (j, k))
        dims = NT
    else:
        K, M = a.shape
        N = b.shape[1]
        a_spec = pl.BlockSpec((tk, tm), lambda i, j, k: (k, i))
        b_spec = pl.BlockSpec((tk, tn), lambda i, j, k: (k, j))
        dims = TN
    if mode != "tn":
        if mode == "nn":
            K = b.shape[-2]
        elif not b_blocked:
            K = b.shape[1]
        a_spec = pl.BlockSpec((tm, tk), lambda i, j, k: (i, k * a_cols[0] + a_cols[1]))
    assert M % tm == 0 and N % tn == 0 and K % tk == 0, (name, M, N, K, tm, tn, tk)
    nk = K // tk
    if out_block is None:
        out_shape = jax.ShapeDtypeStruct((M, N * out_cols[0]), out_dtype)
        o_spec = pl.BlockSpec((tm, tn), lambda i, j, k: (i, j * out_cols[1] + out_cols[2]))
    else:
        out_shape = jax.ShapeDtypeStruct((o_count or N // out_block, M, out_block), out_dtype)
        if tn > out_block:
            og = tn // out_block
            assert tn % out_block == 0 and o_first % og == 0
            o_spec = pl.BlockSpec((og, tm, out_block), lambda i, j, k: (o_first // og + j, i, 0))
        else:
            per_o = out_block // tn
            assert out_block % tn == 0
            o_spec = pl.BlockSpec((None, tm, tn), lambda i, j, k: (o_first + j // per_o, i, j % per_o))
    in_specs = [a_spec, b_spec]
    args = [a, b]
    if res is not None:
        in_specs.append(pl.BlockSpec((tm, tn), lambda i, j, k: (i, j * out_cols[1] + out_cols[2])))
        args.append(res)
    n_in = len(args) + len(after) + (into is not None)

    def product(a_ref, b_ref):
        if bg == 1:
            return _dot(a_ref[...], b_ref[...], dims)
        nb = b_ref.shape[2]
        if mode == "nn":
            return jnp.concatenate([_dot(a_ref[...], b_ref[g], dims) for g in range(bg)], axis=1)
        return sum(_dot(a_ref[:, g * nb:(g + 1) * nb], b_ref[g], dims) for g in range(bg))

    def body(*refs):
        a_ref, b_ref = refs[0], refs[1]
        r_ref = refs[2] if res is not None else None
        o_ref = refs[n_in]

        def finish(acc):
            if r_ref is not None:
                acc = acc + r_ref[...]
            if og == 1:
                o_ref[...] = acc.astype(o_ref.dtype)
            else:
                for g in range(og):
                    o_ref[g] = acc[:, g * out_block:(g + 1) * out_block].astype(o_ref.dtype)

        if nk == 1:
            finish(product(a_ref, b_ref))
        else:
            acc_ref = refs[-1]
            k = pl.program_id(2)

            @pl.when(k == 0)
            def _():
                acc_ref[...] = jnp.zeros_like(acc_ref)

            acc_ref[...] += product(a_ref, b_ref)

            @pl.when(k == nk - 1)
            def _():
                finish(acc_ref[...])

    return pl.pallas_call(
        body, grid=(M // tm, N // tn, nk), in_specs=in_specs + [ANY] * (n_in - len(args)), out_specs=o_spec, out_shape=out_shape,
        scratch_shapes=[] if nk == 1 else [pltpu.VMEM((tm, tn), F32)], input_output_aliases={} if into is None else {n_in - 1: 0},
        compiler_params=_params("parallel", "parallel", "arbitrary"), name=name)(*args, *after, *(() if into is None else (into,)))


def _mm_fused(a, b, tiles, fn, outs, *, mode, tm, tn, tk, name, sums=(), after=(), b_outer=False):
    M, K = a.shape
    N = b.shape[1] if mode == "nn" else b.shape[0]
    nk = K // tk
    assert M % tm == 0 and N % tn == 0 and K % tk == 0 and (not sums or tn == N)
    def at(f):
        return (lambda j, i, k: f(i, j, k)) if b_outer else f

    b_spec = pl.BlockSpec((tk, tn), at(lambda i, j, k: (k, j))) if mode == "nn" else pl.BlockSpec((tn, tk), at(lambda i, j, k: (j, k)))
    in_specs = [pl.BlockSpec((tm, tk), at(lambda i, j, k: (i, k))), b_spec]
    args = [a, b]
    for t in tiles:
        if isinstance(t, tuple):
            arr, off = t
            in_specs.append(pl.BlockSpec((tm, tn), at(lambda i, j, k, off=off: (i, off + j))))
        else:
            arr = t
            in_specs.append(pl.BlockSpec((1, tn), at(lambda i, j, k: (0, j))))
        args.append(arr)
    n_in = len(args) + len(after)
    n_out = len(outs) + len(sums)

    def body(*refs):
        a_ref, b_ref = refs[0], refs[1]
        t_refs = refs[2:2 + len(tiles)]
        o_refs = refs[n_in:n_in + n_out]

        def finish(acc):
            res = fn(acc, *[t[...] for t in t_refs])
            for o_ref, r in zip(o_refs[:len(outs)], res):
                o_ref[...] = r.astype(o_ref.dtype)
            if sums:
                @pl.when(pl.program_id(0) == 0)
                def _():
                    for o_ref in o_refs[len(outs):]:
                        o_ref[...] = jnp.zeros_like(o_ref)

                for o_ref, r in zip(o_refs[len(outs):], res[len(outs):]):
                    o_ref[...] += jnp.broadcast_to(r, o_ref.shape)

        if nk == 1:
            finish(_dot(a_ref[...], b_ref[...], NN if mode == "nn" else NT))
        else:
            acc_ref = refs[-1]
            k = pl.program_id(2)

            @pl.when(k == 0)
            def _():
                acc_ref[...] = jnp.zeros_like(acc_ref)

            acc_ref[...] += _dot(a_ref[...], b_ref[...], NN if mode == "nn" else NT)

            @pl.when(k == nk - 1)
            def _():
                finish(acc_ref[...])

    assert not (b_outer and sums)
    o_spec = pl.BlockSpec((tm, tn), at(lambda i, j, k: (i, j)))
    return pl.pallas_call(
        body, grid=(N // tn, M // tm, nk) if b_outer else (M // tm, N // tn, nk), in_specs=in_specs + [ANY] * len(after),
        out_specs=[o_spec] * len(outs) + [pl.BlockSpec(sh, lambda i, j, k: (0, 0)) for sh in sums],
        out_shape=[jax.ShapeDtypeStruct((M, N), dt) for dt in outs] + [jax.ShapeDtypeStruct(sh, F32) for sh in sums],
        scratch_shapes=[] if nk == 1 else [pltpu.VMEM((tm, tn), F32)],
        compiler_params=_params("arbitrary" if sums else "parallel", "parallel", "arbitrary"), name=name)(*args, *after)


def _row_spec(d, col=0):
    return pl.BlockSpec((ROWS, d), lambda i: (i, col))


def _vec_spec(d):
    return pl.BlockSpec((1, d), lambda i: (0, 0))


def _rms(x):
    return lax.rsqrt(jnp.mean(x * x, axis=-1, keepdims=True) + EPS)


def _norm_fwd(x, g, name):
    T, D = x.shape

    def body(x_ref, g_ref, h_ref):
        xv = x_ref[...]
        h_ref[...] = (xv * _rms(xv) * g_ref[...]).astype(BF16)

    return pl.pallas_call(body, grid=(T // ROWS,), in_specs=[_row_spec(D), _vec_spec(D)], out_specs=_row_spec(D),
                          out_shape=jax.ShapeDtypeStruct((T, D), BF16), compiler_params=_params("parallel"), name=name)(x, g)


def _residual_norm(y, x, g):
    x = x + y
    return x, x * _rms(x) * g


def _norm_bwd_math(dh, xv, gv):
    r = _rms(xv)
    xhat = xv * r
    dxhat = dh * gv
    dx = r * (dxhat - xhat * jnp.mean(dxhat * xhat, axis=-1, keepdims=True))
    dg = jnp.sum(dh * xhat, axis=0, keepdims=True)
    return dx, dg


def _residual_norm_bwd(dh, x, dres, g):
    dx, dg = _norm_bwd_math(dh, x, g)
    dx = dx + dres
    return dx, dx, dg


def _norm_bwd(dh, x, g, dres, name):
    T, D = x.shape

    def body(dh_ref, x_ref, g_ref, dres_ref, dx_ref, dxb_ref, dg_ref):
        dx, dg = _norm_bwd_math(dh_ref[...], x_ref[...], g_ref[...])
        dx = dx + dres_ref[...]
        dx_ref[...] = dx
        dxb_ref[...] = dx.astype(BF16)

        @pl.when(pl.program_id(0) == 0)
        def _():
            dg_ref[...] = jnp.zeros_like(dg_ref)

        dg_ref[...] += dg

    return pl.pallas_call(
        body, grid=(T // ROWS,), in_specs=[_row_spec(D), _row_spec(D), _vec_spec(D), _row_spec(D)],
        out_specs=[_row_spec(D), _row_spec(D), _vec_spec(D)],
        out_shape=[jax.ShapeDtypeStruct((T, D), F32), jax.ShapeDtypeStruct((T, D), BF16), jax.ShapeDtypeStruct((1, D), F32)],
        compiler_params=_params("arbitrary"), name=name)(dh, x, g, dres)


def _mm_merge(y_sb, y_ca, w_sb, w_ca, proj, gate_col, tm, tn, after):
    T, W = y_sb.shape
    G, _, nb = w_sb.shape
    D, bg = G * nb, tn // nb
    assert tn % nb == 0 and D % tn == 0
    per = D // tn

    def body(ys_ref, yc_ref, ws_ref, wc_ref, gs_ref, gc_ref, *rest):
        as_ref, ac_ref, m_ref = rest[len(after):]
        a = jnp.concatenate([_dot(ys_ref[...], ws_ref[g]) for g in range(bg)], axis=1)
        b = jnp.concatenate([_dot(yc_ref[...], wc_ref[g]) for g in range(bg)], axis=1)
        as_ref[...] = a
        ac_ref[...] = b
        m_ref[...] = (jax.nn.sigmoid(gs_ref[...]) * a + jax.nn.sigmoid(gc_ref[...]) * b).astype(BF16)

    y_spec = pl.BlockSpec((tm, W), lambda i, j: (i, 0))
    w_spec = pl.BlockSpec((bg, W, nb), lambda i, j: (j, 0, 0))
    out = pl.BlockSpec((tm, tn), lambda i, j: (i, j))
    f32 = jax.ShapeDtypeStruct((T, D), F32)
    return pl.pallas_call(
        body, grid=(T // tm, per),
        in_specs=[y_spec, y_spec, w_spec, w_spec, pl.BlockSpec((tm, tn), lambda i, j: (i, gate_col * per + j)),
                  pl.BlockSpec((tm, tn), lambda i, j: (i, (gate_col + 1) * per + j))] + [ANY] * len(after),
        out_specs=[out, out, out], out_shape=[f32, f32, jax.ShapeDtypeStruct((T, D), BF16)],
        compiler_params=_params("parallel", "parallel"), name="mm_merge")(y_sb, y_ca, w_sb, w_ca, proj, proj, *after)


def _merge_bwd(dm, gs, gc, a, b):
    ss, sc = jax.nn.sigmoid(gs), jax.nn.sigmoid(gc)
    return dm * ss, dm * sc, dm * a * ss * (1.0 - ss), dm * b * sc * (1.0 - sc)


def _mm_swiglu(h, w, tm, after=()):
    T, D = h.shape
    G2, _, nb = w.shape
    G = G2 // 2

    def body(h_ref, wg_ref, wu_ref, *rest):
        g_ref, u_ref, act_ref = rest[len(after):]
        hv = h_ref[...]
        gv = _dot(hv, wg_ref[...])
        uv = _dot(hv, wu_ref[...])
        g_ref[...] = gv
        u_ref[...] = uv
        act_ref[...] = (gv * jax.nn.sigmoid(gv) * uv).astype(BF16)

    out = pl.BlockSpec((tm, nb), lambda j, i: (i, j))
    f32 = jax.ShapeDtypeStruct((T, G * nb), F32)
    return pl.pallas_call(
        body, grid=(G, T // tm),
        in_specs=[pl.BlockSpec((tm, D), lambda j, i: (i, 0)), pl.BlockSpec((None, D, nb), lambda j, i: (j, 0, 0)),
                  pl.BlockSpec((None, D, nb), lambda j, i: (j + G, 0, 0))] + [ANY] * len(after),
        out_specs=[out, out, out], out_shape=[f32, f32, jax.ShapeDtypeStruct((T, G * nb), BF16)],
        compiler_params=_params("parallel", "parallel"), name="mm_ffn_in")(h, w, w, *after)


def _swiglu_bwd(dact, gate, up):
    s = jax.nn.sigmoid(gate)
    return dact * up * s * (1.0 + gate * (1.0 - s)), dact * gate * s


def _tail(zg, x3, pe, target, g_final):
    D = x3.shape[-1]
    gate = jax.nn.sigmoid(zg)
    x4 = x3 + gate * pe
    err = x4 * _rms(x4) * g_final - target
    part = 0.5 * jnp.sum(jnp.mean(err * err, axis=-1, keepdims=True), axis=0, keepdims=True)
    dx, dg = _norm_bwd_math(err * (1.0 / D), x4, g_final)
    return dx, dx * gate, dx * pe * gate * (1.0 - gate), part, dg


def _cast_bf16(x, name):
    R, C = x.shape
    rows = next(r for r in (ROWS, 128, 64, 32, 16) if R % r == 0)

    def body(x_ref, o_ref):
        o_ref[...] = x_ref[...].astype(BF16)

    spec = pl.BlockSpec((rows, C), lambda i: (i, 0))
    return pl.pallas_call(body, grid=(R // rows,), in_specs=[spec], out_specs=spec, out_shape=jax.ShapeDtypeStruct((R, C), BF16),
                          compiler_params=_params("parallel"), name=name)(x)


def _head_spec(T, col0, heads=1, single=False):
    return pl.BlockSpec((T, heads * HEAD_DIM), lambda h, *_: (0, col0 + h), pipeline_mode=pl.Buffered(1) if single else None)


SB_HEADS = 4


def _triangle(n, right):
    j = lax.broadcasted_iota(jnp.int32, (n, n), 0)
    s = lax.broadcasted_iota(jnp.int32, (n, n), 1)
    return jnp.where((j > s) if right else (j < s), 1.0, 0.0).astype(BF16)


def _lane_scan(x, tri):
    hi = x.astype(BF16)
    lo = (x - hi.astype(F32)).astype(BF16)
    return _dot(hi, tri) + _dot(lo, tri)


def _head_cols(ref, rows, hh):
    return ref[rows, hh * HEAD_DIM:(hh + 1) * HEAD_DIM]


def _sb_tile(qv, kk, past, c_lk, tri):
    z = _dot(qv, kk, NT) * SCALE
    sp = jnp.log(1.0 + jnp.exp(-jnp.abs(z)))
    ls_pos = jnp.minimum(z, 0.0) - sp
    lk = jnp.minimum(-z, 0.0) - sp
    if past is not None:
        lk = jnp.where(past, lk, 0.0)
    right = c_lk + _lane_scan(lk, tri)
    a = jnp.exp(ls_pos + right)
    if past is not None:
        a = jnp.where(past, a, 0.0)
    return ls_pos, a, right[:, 0:1] + lk[:, 0:1]


SB_Q = 512
SB_HEADS_BWD = 4


def _sb_mask(d):
    B, r = SB_BLOCK, SB_Q // SB_BLOCK
    return lax.broadcasted_iota(jnp.int32, (SB_Q, B), 1) + (r - 1 - d) * B < lax.broadcasted_iota(jnp.int32, (SB_Q, B), 0)


def _sb_rows(kb):
    return pl.ds(pl.multiple_of(kb * SB_BLOCK, SB_BLOCK), SB_BLOCK)


def _sb_fwd(proj, n_heads, after=()):
    T = proj.shape[0]
    B, Q, HP = SB_BLOCK, SB_Q, SB_HEADS
    r = Q // B
    assert n_heads % HP == 0 and T % Q == 0

    def body(q_ref, k_ref, v_ref, *rest):
        y_ref = rest[-1]
        qb = pl.program_id(1)
        tri = _triangle(B, right=True)
        qv = [_head_cols(q_ref, slice(None), hh).astype(BF16) for hh in range(HP)]

        def tile(kb, carry, past):
            out = []
            for hh in range(HP):
                acc, c_lk = carry[hh]
                kk = _head_cols(k_ref, _sb_rows(kb), hh).astype(BF16)
                vv = _head_cols(v_ref, _sb_rows(kb), hh).astype(BF16)
                _, a, c_lk = _sb_tile(qv[hh], kk, past, c_lk, tri)
                out.append((acc + _dot(a.astype(BF16), vv), c_lk))
            return tuple(out)

        carry = tuple((jnp.zeros((Q, HEAD_DIM), F32), jnp.zeros((Q, 1), F32)) for _ in range(HP))
        for d in range(r):
            carry = tile(r * qb + r - 1 - d, carry, _sb_mask(d))
        res = lax.fori_loop(0, r * qb, lambda i, c: tile(r * qb - 1 - i, c, None), carry)
        for hh in range(HP):
            y_ref[:, hh * HEAD_DIM:(hh + 1) * HEAD_DIM] = res[hh][0].astype(BF16)

    blk = pl.BlockSpec((Q, HP * HEAD_DIM), lambda h, i: (i, h))
    G = n_heads // HP
    return pl.pallas_call(
        body, grid=(G, T // Q),
        in_specs=[blk, _head_spec(T, G, HP), _head_spec(T, 2 * G, HP)] + [ANY] * len(after), out_specs=blk,
        out_shape=jax.ShapeDtypeStruct((T, n_heads * HEAD_DIM), BF16),
        compiler_params=_params("parallel", "arbitrary"), name="sb_fwd")(proj, proj, proj, *after)


def _sb_bwd(proj, dy, n_heads):
    T = proj.shape[0]
    B, Q, HP = SB_BLOCK, SB_Q, SB_HEADS_BWD
    r, nq = Q // B, T // Q

    def body(q_ref, k_ref, v_ref, dy_ref, dq_ref, dk_ref, dv_ref, g_s, sig_s, dk_s, dv_s):
        qb = pl.program_id(1)

        @pl.when(qb == 0)
        def _():
            dk_s[...] = jnp.zeros_like(dk_s)
            dv_s[...] = jnp.zeros_like(dv_s)

        tri_r = _triangle(B, right=True)
        tri_l = _triangle(B, right=False)
        qv = [_head_cols(q_ref, slice(None), hh).astype(BF16) for hh in range(HP)]
        dyb = [_head_cols(dy_ref, slice(None), hh).astype(BF16) for hh in range(HP)]

        def sweep(kb, carry, past):
            out = []
            for hh in range(HP):
                kk = _head_cols(k_ref, _sb_rows(kb), hh).astype(BF16)
                vv = _head_cols(v_ref, _sb_rows(kb), hh).astype(BF16)
                ls_pos, a, c_lk = _sb_tile(qv[hh], kk, past, carry[hh], tri_r)
                g_s[hh, kb] = _dot(dyb[hh], vv, NT) * a
                sig_s[hh, kb] = jnp.exp(ls_pos).astype(BF16)
                dv_s[_sb_rows(kb), hh * HEAD_DIM:(hh + 1) * HEAD_DIM] += _dot(a.astype(BF16), dyb[hh], TN)
                out.append(c_lk)
            return tuple(out)

        carry = tuple(jnp.zeros((Q, 1), F32) for _ in range(HP))
        for d in range(r):
            carry = sweep(r * qb + r - 1 - d, carry, _sb_mask(d))
        lax.fori_loop(0, r * qb, lambda i, c: sweep(r * qb - 1 - i, c, None), carry)

        def back(kb, carry, past):
            out = []
            for hh in range(HP):
                dq, c_g = carry[hh]
                kk = _head_cols(k_ref, _sb_rows(kb), hh).astype(BF16)
                g, sig = g_s[hh, kb], sig_s[hh, kb].astype(F32)
                left = c_g + _lane_scan(g, tri_l)
                dz = g * (1.0 - sig) - left * sig
                if past is not None:
                    dz = jnp.where(past, dz, 0.0)
                dz = (dz * SCALE).astype(BF16)
                dk_s[_sb_rows(kb), hh * HEAD_DIM:(hh + 1) * HEAD_DIM] += _dot(dz, qv[hh], TN)
                out.append((dq + _dot(dz, kk), left[:, B - 1:B] + g[:, B - 1:B]))
            return tuple(out)

        init = tuple((jnp.zeros((Q, HEAD_DIM), F32), jnp.zeros((Q, 1), F32)) for _ in range(HP))
        res = lax.fori_loop(0, r * qb, lambda kb, c: back(kb, c, None), init)
        for d in reversed(range(r)):
            res = back(r * qb + r - 1 - d, res, _sb_mask(d))
        for hh in range(HP):
            dq_ref[:, hh * HEAD_DIM:(hh + 1) * HEAD_DIM] = res[hh][0].astype(BF16)

        @pl.when(qb == nq - 1)
        def _():
            dk_ref[...] = dk_s[...].astype(BF16)
            dv_ref[...] = dv_s[...].astype(BF16)

    blk = pl.BlockSpec((Q, HP * HEAD_DIM), lambda h, i: (i, h))
    G = n_heads // HP
    full = _head_spec(T, 0, HP, single=True)
    shp = jax.ShapeDtypeStruct((T, n_heads * HEAD_DIM), BF16)
    return pl.pallas_call(
        body, grid=(G, nq),
        in_specs=[blk, _head_spec(T, G, HP, single=True), _head_spec(T, 2 * G, HP, single=True), blk], out_specs=[blk, full, full],
        out_shape=[shp, shp, shp],
        scratch_shapes=[pltpu.VMEM((HP, T // B, Q, B), F32), pltpu.VMEM((HP, T // B, Q, B), BF16)] + [pltpu.VMEM((T, HP * HEAD_DIM), F32)] * 2,
        compiler_params=_params("parallel", "arbitrary"), name="sb_bwd")(proj, proj, proj, dy)


DIAGS = PBAND + PAIR


def _diag_onehot():
    d = lax.broadcasted_iota(jnp.int32, (DIAGS, 2 * PAIR), 0)
    r = lax.broadcasted_iota(jnp.int32, (DIAGS, 2 * PAIR), 1)
    return jnp.where(jnp.clip(d - PAIR - PAD, -REL_CLIP, CHUNK - 1) + REL_CLIP == r, 1.0, 0.0)


def _bias_expand(rel_bias):
    H = rel_bias.shape[0]
    table = jnp.pad(rel_bias, ((0, 0), (0, 2 * PAIR - N_REL)))

    def body(rb_ref, o_ref):
        o_ref[...] = lax.dot_general(rb_ref[...], _diag_onehot(), NT, precision=lax.Precision.HIGHEST, preferred_element_type=F32)

    per_diag = pl.pallas_call(body, out_shape=jax.ShapeDtypeStruct((H, DIAGS), F32), name="bias_expand")(table)
    flat = jnp.tile(jnp.pad(per_diag, ((0, 0), (0, 1))), (1, PAIR))[:, :PAIR * DIAGS]
    return flat.reshape(H, PAIR, DIAGS)[:, :, PAIR:]


def _bias_reduce(dbias):
    H = dbias.shape[0]
    padded = jnp.pad(dbias, ((0, 0), (0, 1), (PAIR, 0))).reshape(H, -1)
    skewed = padded[:, :PAIR * (DIAGS + 1)].reshape(H, PAIR, DIAGS + 1)[:, :, :DIAGS]

    def body(s_ref, o_ref):
        per_diag = jnp.sum(s_ref[...], axis=0, keepdims=True)
        o_ref[...] = lax.dot_general(jnp.broadcast_to(per_diag, (8, DIAGS)), _diag_onehot(), NN, precision=lax.Precision.HIGHEST,
                                     preferred_element_type=F32)[0:1]

    return pl.pallas_call(
        body, grid=(H,), in_specs=[pl.BlockSpec((None, PAIR, DIAGS), lambda h: (h, 0, 0))],
        out_specs=pl.BlockSpec((None, 1, 2 * PAIR), lambda h: (h, 0, 0)),
        out_shape=jax.ShapeDtypeStruct((H, 1, 2 * PAIR), F32), compiler_params=_params("parallel"), name="bias_reduce")(skewed)[:, 0]


CA_HEADS = 2


def _ca_mask():
    i = lax.broadcasted_iota(jnp.int32, (CA_ROWS, CA_BAND), 0)
    j = lax.broadcasted_iota(jnp.int32, (CA_ROWS, CA_BAND), 1)
    qc, kc = i // CHUNK, j // CHUNK
    return j, (kc >= qc) & (kc <= qc + LEFT_CHUNKS)


def _ca_bias(pair_bias):
    rows = []
    for q in range(CA_PAIRS):
        parts = [jnp.zeros((PAIR, q * PAIR), F32)] * (q > 0) + [pair_bias] + [jnp.zeros((PAIR, (CA_PAIRS - 1 - q) * PAIR), F32)] * (q < CA_PAIRS - 1)
        rows.append(jnp.concatenate(parts, axis=1) if len(parts) > 1 else parts[0])
    return jnp.concatenate(rows, axis=0)


def _ca_weights(pr, qp, kb, bias, j, window):
    valid = window & (pr * CA_ROWS + j >= PAD)
    z = jnp.where(valid, _dot(qp, kb, NT) * SCALE + bias, NEG)
    e = jnp.exp(z - jnp.max(z, axis=1, keepdims=True))
    return e / jnp.sum(e, axis=1, keepdims=True)


def _ca_fill(k_ref, v_ref, kpad, vpad):
    T, W = k_ref.shape
    kpad[0:PAD, :] = jnp.zeros((PAD, W), BF16)
    vpad[0:PAD, :] = jnp.zeros((PAD, W), BF16)
    kpad[PAD:PAD + T, :] = k_ref[...].astype(BF16)
    vpad[PAD:PAD + T, :] = v_ref[...].astype(BF16)


def _ca_fwd(proj, bias, n_heads, col0):
    T = proj.shape[0]
    HP = CA_HEADS
    G = n_heads // HP
    assert n_heads % HP == 0 and col0 % HP == 0

    def body(q_ref, k_ref, v_ref, b_ref, y_ref, kpad, vpad):
        _ca_fill(k_ref, v_ref, kpad, vpad)
        j, window = _ca_mask()
        bias = [_ca_bias(b_ref[hh]) for hh in range(HP)]

        def step(pr, _):
            r0 = pl.multiple_of(pr * CA_ROWS, CA_ROWS)
            for hh in range(HP):
                qp = _head_cols(q_ref, pl.ds(r0, CA_ROWS), hh).astype(BF16)
                kb = _head_cols(kpad, pl.ds(r0, CA_BAND), hh)
                vb = _head_cols(vpad, pl.ds(r0, CA_BAND), hh)
                w = _ca_weights(pr, qp, kb, bias[hh], j, window)
                y_ref[pl.ds(r0, CA_ROWS), hh * HEAD_DIM:(hh + 1) * HEAD_DIM] = _dot(w.astype(BF16), vb).astype(BF16)
            return 0

        lax.fori_loop(0, T // CA_ROWS, step, 0)

    c = col0 // HP
    return pl.pallas_call(
        body, grid=(G,),
        in_specs=[_head_spec(T, c, HP), _head_spec(T, c + G, HP), _head_spec(T, c + 2 * G, HP),
                  pl.BlockSpec((HP, PAIR, PBAND), lambda h: (h, 0, 0))],
        out_specs=_head_spec(T, 0, HP), out_shape=jax.ShapeDtypeStruct((T, n_heads * HEAD_DIM), BF16),
        scratch_shapes=[pltpu.VMEM((PAD + T, HP * HEAD_DIM), BF16)] * 2,
        compiler_params=_params("parallel"), name="ca_fwd")(proj, proj, proj, bias)


def _ca_bwd(proj, bias, dy, n_heads, col0):
    T = proj.shape[0]
    HP = CA_HEADS
    G = n_heads // HP

    def body(q_ref, k_ref, v_ref, b_ref, dy_ref, dq_ref, dk_ref, dv_ref, db_ref, kpad, vpad, dkpad, dvpad):
        _ca_fill(k_ref, v_ref, kpad, vpad)
        dkpad[...] = jnp.zeros_like(dkpad)
        dvpad[...] = jnp.zeros_like(dvpad)
        db_ref[...] = jnp.zeros_like(db_ref)
        j, window = _ca_mask()
        bias = [_ca_bias(b_ref[hh]) for hh in range(HP)]

        def step(pr, _):
            r0 = pl.multiple_of(pr * CA_ROWS, CA_ROWS)
            for hh in range(HP):
                cols = slice(hh * HEAD_DIM, (hh + 1) * HEAD_DIM)
                qp = _head_cols(q_ref, pl.ds(r0, CA_ROWS), hh).astype(BF16)
                kb = _head_cols(kpad, pl.ds(r0, CA_BAND), hh)
                vb = _head_cols(vpad, pl.ds(r0, CA_BAND), hh)
                w = _ca_weights(pr, qp, kb, bias[hh], j, window)
                dyp = _head_cols(dy_ref, pl.ds(r0, CA_ROWS), hh).astype(BF16)
                dw = _dot(dyp, vb, NT)
                dz = w * (dw - jnp.sum(dw * w, axis=1, keepdims=True))
                db_ref[hh] += sum(dz[q * PAIR:(q + 1) * PAIR, q * PAIR:q * PAIR + PBAND] for q in range(CA_PAIRS))
                dzs = (dz * SCALE).astype(BF16)
                dq_ref[pl.ds(r0, CA_ROWS), cols] = _dot(dzs, kb).astype(BF16)
                dkpad[pl.ds(r0, CA_BAND), cols] += _dot(dzs, qp, TN)
                dvpad[pl.ds(r0, CA_BAND), cols] += _dot(w.astype(BF16), dyp, TN)
            return 0

        lax.fori_loop(0, T // CA_ROWS, step, 0)
        dk_ref[...] = dkpad[PAD:PAD + T, :].astype(BF16)
        dv_ref[...] = dvpad[PAD:PAD + T, :].astype(BF16)

    c = col0 // HP
    full = _head_spec(T, 0, HP)
    bspec = pl.BlockSpec((HP, PAIR, PBAND), lambda h: (h, 0, 0))
    shp = jax.ShapeDtypeStruct((T, n_heads * HEAD_DIM), BF16)
    return pl.pallas_call(
        body, grid=(G,),
        in_specs=[_head_spec(T, c, HP), _head_spec(T, c + G, HP), _head_spec(T, c + 2 * G, HP), bspec, full],
        out_specs=[full, full, full, bspec],
        out_shape=[shp, shp, shp, jax.ShapeDtypeStruct((n_heads, PAIR, PBAND), F32)],
        scratch_shapes=[pltpu.VMEM((PAD + T, HP * HEAD_DIM), BF16)] * 2 + [pltpu.VMEM((PAD + T, HP * HEAD_DIM), F32)] * 2,
        compiler_params=_params("parallel"), name="ca_bwd")(proj, proj, proj, bias, dy)


def _local_step(x, p, target, comm, g):
    T, D = x.shape
    H = g["rel_bias"].shape[0]
    W = H * HEAD_DIM
    nb_in = comm.shapes["w_in_a"][2]
    nb_ff = comm.shapes["w_ffn_in"][2]
    nb_o = comm.shapes["w_sb_out"][2]
    nb_p = comm.shapes["w_ple_in"][2]
    tm = min(T, 1024)
    tn = min(D, 1024)
    gate_col = 6 * W // D

    h1 = _norm_fwd(x, g["g_mix"], "norm1")
    bias = _bias_expand(g["rel_bias"])
    pb = _cast_bf16(p, "cast_p")
    comm.stage("norm1", h1, bias, pb)
    proj = _mm(h1, comm.weight("w_in_a", h1), mode="nn", tm=tm, tn=nb_in, tk=D // 2, out_dtype=F32, b_blocked=True,
               after=comm.pending(), name="mm_in_a")
    comm.stage("mm_in_a", proj)
    proj = _mm(h1, comm.weight("w_in_b", proj), mode="nn", tm=tm, tn=nb_in, tk=D // 2, out_dtype=F32, b_blocked=True, a_cols=(1, 1),
               res=proj, after=comm.pending(), name="mm_in")
    comm.stage("mm_in", proj)
    y_sb = _sb_fwd(proj, H, comm.pending())
    y_ca = _ca_fwd(proj, bias, H, 3 * H)
    w_sb_out = comm.weight("w_sb_out", (y_sb, y_ca))
    comm.stage("attention", y_sb, w_sb_out)
    a_sb, a_ca, merged = _mm_merge(y_sb, y_ca, w_sb_out, comm.weight("w_ca_out"), proj, gate_col, min(T, 512), tn, comm.pending())
    x2, h2 = _mm_fused(merged, comm.weight("w_mix_out"), [(x, 0), g["g_ffn"]], _residual_norm, [F32, BF16],
                       mode="nn", tm=min(T, 512), tn=D, tk=D, name="mm_mix")
    w_ffn_in = comm.weight("w_ffn_in", h2)
    comm.stage("mm_mix", w_ffn_in)
    gate, up, act = _mm_swiglu(h2, w_ffn_in, min(T, 512), comm.pending())
    F = act.shape[1]
    tkf = F // 2 if F % 256 == 0 else F
    x3 = _mm(act, comm.weight("w_ffn_out", act), mode="nn", tm=tm, tn=tn, tk=tkf, out_dtype=F32, res=x2, after=comm.pending(), name="mm_ffn_out")
    h3 = _norm_fwd(x3, g["g_ple"], "norm3")
    P = p.shape[1]
    pe = _mm(pb, comm.weight("w_ple_in"), mode="nn", tm=tm, tn=tn, tk=P, out_dtype=F32, b_blocked=True, name="mm_ple_in")
    dx4, dpe, dzg, loss, dg_final = _mm_fused(
        h3, comm.weight("w_ple_gate"), [(x3, 0), (pe, 0), (target, 0), g["g_final"]], _tail, [F32, BF16, BF16],
        mode="nn", tm=min(T, 256), tn=D, tk=D, sums=[(1, 128), (1, D)], name="mm_ple_gate")

    tw = min(D, 1024)
    DW = BF16
    comm.grad("w_ple_in", _mm(pb, dpe, mode="tn", tm=P, tn=nb_p, tk=T, out_dtype=DW, out_block=nb_p, name="mm_d_ple_in"))
    comm.grad("w_ple_gate", _mm(h3, dzg, mode="tn", tm=tw, tn=tn, tk=T, out_dtype=DW, name="mm_d_ple_gate"))
    dx3, dx3b, dg_ple = _mm_fused(dzg, comm.weight("w_ple_gate"), [(x3, 0), (dx4, 0), g["g_ple"]], _residual_norm_bwd, [F32, BF16],
                                  mode="nt", tm=min(T, 256), tn=D, tk=D, sums=[(1, D)], after=comm.pending(), name="mm_dh3")
    comm.grad("w_ffn_out", _mm(act, dx3b, mode="tn", tm=F // 4, tn=tn, tk=T, out_dtype=DW, name="mm_d_ffn_out"))
    dgate, dup = _mm_fused(dx3b, comm.weight("w_ffn_out"), [(gate, 0), (up, 0)], _swiglu_bwd, [BF16, BF16],
                           mode="nt", tm=min(T, 512), tn=nb_ff, tk=D, after=comm.pending(), b_outer=True, name="mm_dact")
    half = comm.shapes["w_ffn_in"][0] // 2
    d_ffn_in = _mm(h2, dgate, mode="tn", tm=tw, tn=nb_ff, tk=T, out_dtype=DW, out_block=nb_ff, o_count=2 * half, name="mm_d_ffn_in_gate")
    comm.grad("w_ffn_in", _mm(h2, dup, mode="tn", tm=tw, tn=nb_ff, tk=T, out_dtype=DW, out_block=nb_ff, o_first=half, o_count=2 * half,
                              into=d_ffn_in, name="mm_d_ffn_in"))
    dh2 = _mm(dgate, comm.weight("w_ffn_in"), mode="nt", tm=tm, tn=D, tk=nb_ff, out_dtype=F32, b_blocked=True, b_count=half,
              after=comm.pending(), name="mm_dh2_gate")
    dh2 = _mm(dup, comm.weight("w_ffn_in"), mode="nt", tm=min(T, 512), tn=D, tk=nb_ff, out_dtype=F32, b_blocked=True, b_first=half, b_count=half,
              res=dh2, name="mm_dh2")
    dx2, dx2b, dg_ffn = _norm_bwd(dh2, x2, g["g_ffn"], dx3, "norm2_bwd")
    per = D // tn
    da_sb, da_ca, dgate_sb, dgate_ca = _mm_fused(
        dx2b, comm.weight("w_mix_out"), [(proj, gate_col * per), (proj, (gate_col + 1) * per), (a_sb, 0), (a_ca, 0)], _merge_bwd, [BF16] * 4,
        mode="nt", tm=min(T, 512), tn=tn, tk=D, name="mm_dmerged")
    comm.grad("w_mix_out", _mm(merged, dx2b, mode="tn", tm=tw, tn=tn, tk=T, out_dtype=DW, name="mm_d_mix"))
    comm.grad("w_sb_out", _mm(y_sb, da_sb, mode="tn", tm=min(W, 512), tn=tn, tk=T, out_dtype=DW, out_block=nb_o, name="mm_d_sb_out"))
    comm.grad("w_ca_out", _mm(y_ca, da_ca, mode="tn", tm=min(W, 512), tn=tn, tk=T, out_dtype=DW, out_block=nb_o, name="mm_d_ca_out"))
    dy_sb = _mm(da_sb, comm.weight("w_sb_out"), mode="nt", tm=tm, tn=W, tk=tn, out_dtype=BF16, b_blocked=True, after=comm.pending(), name="mm_dy_sb")
    dy_ca = _mm(da_ca, comm.weight("w_ca_out"), mode="nt", tm=tm, tn=W, tk=tn, out_dtype=BF16, b_blocked=True, name="mm_dy_ca")
    dq_sb, dk_sb, dv_sb = _sb_bwd(proj, dy_sb, H)
    dq_ca, dk_ca, dv_ca, dbias = _ca_bwd(proj, bias, dy_ca, H, 3 * H)
    d_rel = _bias_reduce(dbias)[:, :N_REL]
    dproj = jnp.concatenate([dq_sb, dk_sb, dv_sb, dq_ca, dk_ca, dv_ca, dgate_sb, dgate_ca], axis=1)
    comm.grad("w_in", _mm(h1, dproj, mode="tn", tm=tw, tn=nb_in, tk=T, out_dtype=DW, out_block=nb_in, name="mm_d_in"))
    dh1 = _mm(dproj, comm.weight("w_in_a"), mode="nt", tm=tm, tn=D // 2, tk=nb_in, out_dtype=F32, b_blocked=True, out_cols=(2, 1, 0),
              after=comm.pending(), name="mm_dh1_a")
    comm.pair_done(dh1)
    dh1 = _mm(dproj, comm.weight("w_in_b"), mode="nt", tm=tm, tn=D // 2, tk=nb_in, out_dtype=F32, b_blocked=True, out_cols=(2, 1, 1),
              into=dh1, after=comm.pending(), name="mm_dh1")
    grad_x, _, dg_mix = _norm_bwd(dh1, x, g["g_mix"], dx2, "norm1_bwd")
    small = dict(g_mix=dg_mix, g_ffn=dg_ffn, g_ple=dg_ple, g_final=dg_final, rel_bias=d_rel)
    return loss, grad_x, small


def _position():
    x, y, c = lax.axis_index("x"), lax.axis_index("y"), lax.axis_index("c")
    return x, y, c


def _block_of(px, py, pc):
    return 4 * px + 2 * py + pc


def _flip(pos, k):
    x, y, c = pos
    return (1 - x if k & 4 else x, 1 - y if k & 2 else y, 1 - c if k & 1 else c)


HBM = pl.BlockSpec(memory_space=pltpu.HBM)
SEM = pl.BlockSpec(memory_space=pltpu.SEMAPHORE)
VMEM_SPEC = pl.BlockSpec(memory_space=pltpu.VMEM)
EFFECT = pltpu.SideEffectType.DATAFLOW_SIDE_EFFECTING
TOKEN = jax.ShapeDtypeStruct((8, 128), F32)


def _hbm(a):
    return pltpu.HBM(a.shape, a.dtype)


def _landing(shape, dtype):
    return pltpu.with_memory_space_constraint(lax.empty(shape, dtype), pltpu.HBM)


def _gather_start(lands, after, name, relay):
    n = len(lands)

    def body(*refs):
        ins = refs[:n]
        send, recv = refs[n + 1], refs[n + 2]
        token = refs[-1]
        x, y, c = _position()
        mine = _block_of(x, y, c)
        peers = [(x, y, 1 - c), (1 - x, y, c), (x, 1 - y, c), (1 - x, 1 - y, c)][:3 if relay else 4]
        for wi in range(n):
            for k, peer in enumerate(peers):
                pltpu.make_async_remote_copy(
                    src_ref=ins[wi].at[mine], dst_ref=ins[wi].at[mine], send_sem=send.at[4 * wi + k], recv_sem=recv.at[4 * wi + k],
                    device_id=peer, device_id_type=MESH).start()
        token[...] = jnp.zeros_like(token)

    outs = pl.pallas_call(
        body, name=name, in_specs=[HBM] * n + [ANY], out_specs=(SEM, SEM, *[HBM] * n, VMEM_SPEC),
        out_shape=(pltpu.SemaphoreType.DMA((4 * n,)), pltpu.SemaphoreType.DMA((4 * n,)), *[_hbm(a) for a in lands], TOKEN),
        input_output_aliases={i: 2 + i for i in range(n)},
        compiler_params=pltpu.CompilerParams(has_side_effects=EFFECT))(*[pltpu.with_memory_space_constraint(a, pltpu.HBM) for a in lands], after)
    return outs[0], outs[1], list(outs[2:2 + n]), outs[-1]


def _relays(ref, wi, send2, recv2, pos, received):
    x, y, c = pos
    half = ref.shape[1] // 2
    out = []
    for h, (origin, to) in enumerate((((1 - x, y, c), (x, 1 - y, c)), ((x, 1 - y, c), (1 - x, y, c)))):
        rows = ref.at[_block_of(1 - x, 1 - y, c) if received else _block_of(*origin), pl.ds(h * half, half)]
        out.append(pltpu.make_async_remote_copy(src_ref=rows, dst_ref=rows, send_sem=send2.at[2 * wi + h], recv_sem=recv2.at[2 * wi + h],
                                                device_id=to, device_id_type=MESH))
    return out


def _gather_forward(lands, send0, recv0, after, name, relay):
    n = len(lands)

    def body(*refs):
        ins = refs[:n]
        send0, recv0 = refs[n], refs[n + 1]
        send1, recv1, send2, recv2 = refs[n + 2 + len(after):n + 6 + len(after)]
        token = refs[-1]
        x, y, c = _position()
        chips = [(1 - x, y), (x, 1 - y), (1 - x, 1 - y)][:2 if relay else 3]
        for wi in range(n):
            for j, chip in enumerate(chips):
                rows = ins[wi].at[_block_of(*chip, c)]
                pltpu.make_async_remote_copy(
                    src_ref=rows, dst_ref=rows, send_sem=send0.at[4 * wi + 1 + j], recv_sem=recv0.at[4 * wi + 1 + j],
                    device_id=(*chip, c), device_id_type=MESH).wait_recv()
                pltpu.make_async_remote_copy(
                    src_ref=rows, dst_ref=rows, send_sem=send1.at[3 * wi + j], recv_sem=recv1.at[3 * wi + j],
                    device_id=(x, y, 1 - c), device_id_type=MESH).start()
            if relay:
                for sent in _relays(ins[wi], wi, send2, recv2, (x, y, c), False):
                    sent.start()
        token[...] = jnp.zeros_like(token)

    outs = pl.pallas_call(
        body, name=name, in_specs=[HBM] * n + [SEM, SEM] + [ANY] * len(after), out_specs=(SEM, SEM, SEM, SEM, *[HBM] * n, VMEM_SPEC),
        out_shape=(pltpu.SemaphoreType.DMA((3 * n,)), pltpu.SemaphoreType.DMA((3 * n,)), pltpu.SemaphoreType.DMA((2 * n,)),
                   pltpu.SemaphoreType.DMA((2 * n,)), *[_hbm(a) for a in lands], TOKEN),
        input_output_aliases={i: 4 + i for i in range(n)},
        compiler_params=pltpu.CompilerParams(has_side_effects=EFFECT))(*lands, send0, recv0, *after)
    return outs[:4], list(outs[4:4 + n]), outs[-1]


def _gather_far(lands, send1, recv1, send2, recv2, after, name):
    n = len(lands)

    def body(*refs):
        ins = refs[:n]
        send1, recv1, send2, recv2 = refs[n:n + 4]
        token = refs[-1]
        x, y, c = _position()
        for wi in range(n):
            for sent, got in zip(_relays(ins[wi], wi, send2, recv2, (x, y, c), False), _relays(ins[wi], wi, send2, recv2, (x, y, c), True)):
                sent.wait_send()
                got.wait_recv()
            far = ins[wi].at[_block_of(1 - x, 1 - y, c)]
            pltpu.make_async_remote_copy(src_ref=far, dst_ref=far, send_sem=send1.at[3 * wi + 2], recv_sem=recv1.at[3 * wi + 2],
                                         device_id=(x, y, 1 - c), device_id_type=MESH).start()
        token[...] = jnp.zeros_like(token)

    outs = pl.pallas_call(
        body, name=name, in_specs=[HBM] * n + [SEM] * 4 + [ANY] * len(after), out_specs=(*[HBM] * n, VMEM_SPEC),
        out_shape=(*[_hbm(a) for a in lands], TOKEN), input_output_aliases={i: i for i in range(n)},
        compiler_params=pltpu.CompilerParams(has_side_effects=EFFECT))(*lands, send1, recv1, send2, recv2, *after)
    return list(outs[:n]), outs[-1]


def _gather_wait(lands, send0, recv0, send1, recv1, send2, recv2, after, name, relay):
    n = len(lands)

    def body(*refs):
        ins = refs[:n]
        send0, recv0, send1, recv1, send2, recv2 = refs[n:n + 6]
        x, y, c = _position()
        mine = _block_of(x, y, c)
        sibling = (x, y, 1 - c)
        peers = [sibling, (1 - x, y, c), (x, 1 - y, c), (1 - x, 1 - y, c)][:3 if relay else 4]
        chips = [(1 - x, y), (x, 1 - y), (1 - x, 1 - y)]
        for wi in range(n):
            own = ins[wi].at[mine]
            for k, peer in enumerate(peers):
                pltpu.make_async_remote_copy(src_ref=own, dst_ref=own, send_sem=send0.at[4 * wi + k], recv_sem=recv0.at[4 * wi + k],
                                             device_id=peer, device_id_type=MESH).wait_send()
            theirs = ins[wi].at[_block_of(*sibling)]
            pltpu.make_async_remote_copy(src_ref=theirs, dst_ref=theirs, send_sem=send0.at[4 * wi], recv_sem=recv0.at[4 * wi],
                                         device_id=sibling, device_id_type=MESH).wait_recv()
            for j, chip in enumerate(chips):
                sent = ins[wi].at[_block_of(*chip, c)]
                got = ins[wi].at[_block_of(*chip, 1 - c)]
                pltpu.make_async_remote_copy(src_ref=sent, dst_ref=sent, send_sem=send1.at[3 * wi + j], recv_sem=recv1.at[3 * wi + j],
                                             device_id=sibling, device_id_type=MESH).wait_send()
                pltpu.make_async_remote_copy(src_ref=got, dst_ref=got, send_sem=send1.at[3 * wi + j], recv_sem=recv1.at[3 * wi + j],
                                             device_id=sibling, device_id_type=MESH).wait_recv()

    outs = pl.pallas_call(
        body, name=name, in_specs=[HBM] * n + [SEM] * 6 + [ANY], out_specs=tuple([HBM] * n),
        out_shape=tuple(_hbm(a) for a in lands), input_output_aliases={i: i for i in range(n)},
        compiler_params=pltpu.CompilerParams(has_side_effects=EFFECT))(*lands, send0, recv0, send1, recv1, send2, recv2, after)
    return list(outs)


def _plan_direct(me):
    return [(_block_of(*_flip(me, k)), k - 1, _flip(me, k)) for k in range(1, N_DEV)]


def _plan_sibling(me):
    x, y, c = me
    return [(_block_of(ci // 2, ci % 2, 1 - c), ci, (x, y, 1 - c)) for ci in range(4)]


def _plan_chips(me):
    x, y, c = me
    out = []
    for k in range(1, 4):
        px, py = (1 - x if k & 2 else x), (1 - y if k & 1 else y)
        out.append((2 * px + py, k - 1, (px, py, c)))
    return out


def _exchange_start(blocks, plan, name):
    n = len(blocks)
    slots = len(plan((0, 0, 0)))

    def body(*refs):
        srcs, lands = refs[:n], refs[n:2 * n]
        send, recv = refs[2 * n], refs[2 * n + 1]
        token = refs[-1]
        for wi in range(n):
            for block, slot, peer in plan(_position()):
                pltpu.make_async_remote_copy(
                    src_ref=srcs[wi].at[block], dst_ref=lands[wi].at[slot], send_sem=send.at[slots * wi + slot],
                    recv_sem=recv.at[slots * wi + slot], device_id=peer, device_id_type=MESH).start()
        token[...] = jnp.zeros_like(token)

    zones = [_landing((slots,) + b.shape[1:], b.dtype) for b in blocks]
    outs = pl.pallas_call(
        body, name=name, in_specs=[HBM] * (2 * n), out_specs=(SEM, SEM, *[HBM] * (2 * n), VMEM_SPEC),
        out_shape=(pltpu.SemaphoreType.DMA((slots * n,)), pltpu.SemaphoreType.DMA((slots * n,)), *[_hbm(a) for a in blocks],
                   *[_hbm(z) for z in zones], TOKEN),
        input_output_aliases={i: 2 + i for i in range(2 * n)},
        compiler_params=pltpu.CompilerParams(has_side_effects=EFFECT))(
            *[pltpu.with_memory_space_constraint(b, pltpu.HBM) for b in blocks], *zones)
    return outs[0], outs[1], list(outs[2:2 + n]), list(outs[2 + n:2 + 2 * n]), outs[-1]


def _exchange_wait(groups, plan, after, name):
    flat, counts = [], []
    for send, recv, blocks, zones in groups:
        flat += [*blocks, *zones, send, recv]
        counts.append(len(blocks))
    slots = len(plan((0, 0, 0)))

    def body(*refs):
        pos = 0
        for n in counts:
            srcs, lands = refs[pos:pos + n], refs[pos + n:pos + 2 * n]
            send, recv = refs[pos + 2 * n], refs[pos + 2 * n + 1]
            pos += 2 * n + 2
            for wi in range(n):
                for block, slot, peer in plan(_position()):
                    cp = pltpu.make_async_remote_copy(
                        src_ref=srcs[wi].at[block], dst_ref=lands[wi].at[slot], send_sem=send.at[slots * wi + slot],
                        recv_sem=recv.at[slots * wi + slot], device_id=peer, device_id_type=MESH)
                    cp.wait_send()
                    cp.wait_recv()

    in_specs, out_specs, out_shape, aliases = [], [], [], {}
    i = 0
    for n, (send, recv, blocks, zones) in zip(counts, groups):
        for a in (*blocks, *zones):
            aliases[i] = len(out_shape)
            in_specs.append(HBM)
            out_specs.append(HBM)
            out_shape.append(_hbm(a))
            i += 1
        in_specs += [SEM, SEM]
        i += 2
    outs = pl.pallas_call(
        body, name=name, in_specs=in_specs + [ANY], out_specs=tuple(out_specs), out_shape=tuple(out_shape),
        input_output_aliases=aliases, compiler_params=pltpu.CompilerParams(has_side_effects=EFFECT))(*flat, after)
    res, pos = [], 0
    for n in counts:
        res.append((list(outs[pos:pos + n]), list(outs[pos + n:pos + 2 * n])))
        pos += 2 * n
    return res


def _sibling_sum(blocks, zone, core, name):
    _, R, C = zone.shape
    rt = next(r for r in (R, R // 2, R // 4, 128, 64) if R % r == 0 and r % 16 == 0 and r * C <= 4 * 1024 * 1024)

    def body(core_ref, own_ref, z_ref, o_ref):
        o_ref[...] = (own_ref[...].astype(F32) + z_ref[...].astype(F32)).astype(o_ref.dtype)

    grid_spec = pltpu.PrefetchScalarGridSpec(
        num_scalar_prefetch=1, grid=(4, R // rt),
        in_specs=[pl.BlockSpec((None, rt, C), lambda ci, i, core_ref: (2 * ci + core_ref[0], i, 0)),
                  pl.BlockSpec((None, rt, C), lambda ci, i, core_ref: (ci, i, 0))],
        out_specs=pl.BlockSpec((None, rt, C), lambda ci, i, core_ref: (ci, i, 0)))
    return pl.pallas_call(body, grid_spec=grid_spec, out_shape=jax.ShapeDtypeStruct(zone.shape, zone.dtype),
                          compiler_params=_params("parallel", "parallel"), name=name)(core, blocks, zone)


def _adamw(w, g, m, v):
    m = ADAM_B1 * m + (1.0 - ADAM_B1) * g
    v = ADAM_B2 * v + (1.0 - ADAM_B2) * (g * g)
    m_hat = m / (1.0 - ADAM_B1 ** ADAM_STEP)
    v_hat = v / (1.0 - ADAM_B2 ** ADAM_STEP)
    delta = -ADAM_LR * (m_hat / (jnp.sqrt(v_hat) + ADAM_EPS) + ADAM_WD * w)
    return delta, m, v


def _reduce_adamw(blocks, zone, mine, w, m, v, name):
    R, C = w.shape
    rt = next(r for r in (256, 128, 64) if R % r == 0 and r * C <= 512 * 1024)

    def body(mine_ref, own_ref, z_ref, w_ref, m_ref, v_ref, g_out, d_out, m_out, v_out):
        g = own_ref[...].astype(F32)
        for s in range(zone.shape[0]):
            g = g + z_ref[s].astype(F32)
        delta, m2, v2 = _adamw(w_ref[...], g, m_ref[...], v_ref[...])
        g_out[...] = g
        d_out[...] = delta
        m_out[...] = m2
        v_out[...] = v2

    spec = pl.BlockSpec((rt, C), lambda i, mine_ref: (i, 0))
    grid_spec = pltpu.PrefetchScalarGridSpec(
        num_scalar_prefetch=1, grid=(R // rt,),
        in_specs=[pl.BlockSpec((None, rt, C), lambda i, mine_ref: (mine_ref[0], i, 0)),
                  pl.BlockSpec((zone.shape[0], rt, C), lambda i, mine_ref: (0, i, 0)), spec, spec, spec],
        out_specs=[spec] * 4)
    return pl.pallas_call(body, grid_spec=grid_spec, out_shape=[jax.ShapeDtypeStruct((R, C), F32)] * 4,
                          compiler_params=_params("parallel"), name=name)(mine, blocks, zone, w, m, v)


def _small_step(part, w, m, v, after):
    R, C = part.shape

    def body(part_ref, w_ref, m_ref, v_ref, *rest):
        g_out, d_out, m_out, v_out, gath, send, recv = rest[len(after):]
        me = _position()
        gath[_block_of(*me)] = part_ref[...]

        def copy(k, slot):
            return pltpu.make_async_remote_copy(
                src_ref=part_ref, dst_ref=gath.at[slot], send_sem=send.at[k - 1], recv_sem=recv.at[k - 1],
                device_id=_flip(me, k), device_id_type=MESH)

        sent = [copy(k, _block_of(*me)) for k in range(1, N_DEV)]
        for cp in sent:
            cp.start()
        for k in range(1, N_DEV):
            copy(k, _block_of(*_flip(me, k))).wait_recv()
        for cp in sent:
            cp.wait_send()
        g = gath[0]
        for s in range(1, N_DEV):
            g = g + gath[s]
        delta, m2, v2 = _adamw(w_ref[...], g, m_ref[...], v_ref[...])
        g_out[...] = g
        d_out[...] = delta
        m_out[...] = m2
        v_out[...] = v2

    vm = pl.BlockSpec(memory_space=pltpu.VMEM)
    return pl.pallas_call(
        body, in_specs=[vm] * 4 + [ANY] * len(after), out_specs=[vm] * 4, out_shape=[jax.ShapeDtypeStruct((R, C), F32)] * 4,
        scratch_shapes=[pltpu.VMEM((N_DEV, R, C), F32), pltpu.SemaphoreType.DMA((7,)), pltpu.SemaphoreType.DMA((7,))],
        name="small_step")(part, w, m, v, *after)


COLUMN_SHARDED = ("w_in", "w_sb_out", "w_ca_out", "w_ffn_in", "w_ple_in")
ROW_SHARDED = ("w_mix_out", "w_ffn_out", "w_ple_gate")
BIG = COLUMN_SHARDED + ROW_SHARDED
SMALL = ("g_mix", "g_ffn", "g_ple", "g_final", "rel_bias")
WEIGHTS = ("w_in", "w_sb_out", "w_ca_out", "w_mix_out", "rel_bias", "g_mix", "g_ffn", "g_ple", "g_final",
           "w_ffn_in", "w_ffn_out", "w_ple_in", "w_ple_gate")


def _pack_small(t, D):
    rows = [t[n].reshape(1, D) for n in SMALL[:4]]
    rb = t["rel_bias"].reshape(1, -1)
    rows.append(jnp.pad(rb, ((0, 0), (0, D - rb.shape[1]))))
    return jnp.concatenate(rows + [jnp.zeros((8 - len(rows), D), F32)], axis=0)


def _unpack_small(a, like):
    out = {n: a[i].reshape(like[n].shape) for i, n in enumerate(SMALL[:4])}
    out["rel_bias"] = a[4, :like["rel_bias"].size].reshape(like["rel_bias"].shape)
    return out


GATHER_GROUPS = (("w_in_a",), ("w_in_b",), ("w_sb_out", "w_ca_out", "w_mix_out"), ("w_ffn_in",), ("w_ffn_out", "w_ple_gate", "w_ple_in"))
FORWARD_AFTER = ("norm1", "mm_in_a", "mm_in", "attention", "mm_mix")
RELAYED = (False, False, True, True, True)
GRAD_GROUPS = (("w_ple_in", "w_ple_gate"), ("w_ffn_out",), ("w_ffn_in",), ("w_mix_out", "w_sb_out", "w_ca_out"), ("w_in",))


class _Exchange:
    def __init__(self, shards):
        me = _position()
        self.mine = _block_of(*me)
        self.chip = jnp.reshape(2 * me[0] + me[1], (1,)).astype(jnp.int32)
        self.core = jnp.reshape(me[2], (1,)).astype(jnp.int32)
        self.device = jnp.reshape(self.mine, (1,)).astype(jnp.int32)
        self.shapes = {n: ((N_DEV * s.shape[0], s.shape[1]) if n in ROW_SHARDED else (N_DEV,) + s.shape) for n, s in shards.items()}
        self.tokens = []
        self.ready = {}
        self.gathers = []
        for gi, names in enumerate(GATHER_GROUPS):
            lands = [lax.dynamic_update_slice(lax.empty((N_DEV,) + shards[n].shape, BF16), shards[n][None], (self.mine, 0, 0))
                     for n in names]
            behind = self.tokens[-1] if self.tokens else shards[names[0]]
            send0, recv0, lands, token = _gather_start(lands, behind, f"gather_start_{gi}", RELAYED[gi])
            self.tokens.append(token)
            self.gathers.append(dict(names=names, lands=lands, sems=(send0, recv0), token=token))
        self.grads = {}
        self.exchanges = []

    def pending(self):
        tokens, self.tokens = self.tokens, []
        return tokens

    def stage(self, tag, *made):
        gi = FORWARD_AFTER.index(tag)
        gth = self.gathers[gi]
        sems, lands, token = _gather_forward(gth["lands"], *gth["sems"], made + tuple(self.tokens), f"gather_forward_{gi}", RELAYED[gi])
        gth.update(lands=lands, sems=gth["sems"] + tuple(sems), token=token)
        self.tokens.append(token)

    def weight(self, name, after=None):
        if name not in self.ready:
            gi = next(i for i, names in enumerate(GATHER_GROUPS) if name in names)
            gth = self.gathers[gi]
            after = gth["token"] if after is None else after
            if RELAYED[gi]:
                gth["lands"], after = _gather_far(gth["lands"], *gth["sems"][2:], after if isinstance(after, tuple) else (after,), f"gather_far_{gi}")
            for n, a in zip(gth["names"], _gather_wait(gth["lands"], *gth["sems"], after, f"gather_wait_{gi}", RELAYED[gi])):
                self.ready[n] = a.reshape(self.shapes[n])
        return self.ready[name]

    def grad(self, name, blocks):
        self.grads[name] = blocks if name in COLUMN_SHARDED else blocks.reshape((N_DEV, -1, blocks.shape[-1]))
        names = next(names for names in GRAD_GROUPS if name in names)
        if not all(n in self.grads for n in names):
            return
        blocks = [self.grads[n] for n in names]
        if names == GRAD_GROUPS[-1]:
            send, recv, blocks, zones, token = _exchange_start(blocks, _plan_sibling, "pair_start_" + names[0])
            self.pair = (send, recv, blocks, zones)
        else:
            send, recv, blocks, zones, token = _exchange_start(blocks, _plan_direct, "exchange_start_" + names[0])
            self.exchanges.append(dict(names=names, state=(send, recv, blocks, zones), plan=_plan_direct, own=self.device))
        self.tokens.append(token)

    def pair_done(self, after):
        names = GRAD_GROUPS[-1]
        (blocks, zones), = _exchange_wait([self.pair], _plan_sibling, after, "pair_wait_" + names[0])
        blocks = [_sibling_sum(b, z, self.core, "pair_sum_" + n) for n, b, z in zip(names, blocks, zones)]
        send, recv, blocks, zones, token = _exchange_start(blocks, _plan_chips, "exchange_start_" + names[0])
        self.exchanges.append(dict(names=names, state=(send, recv, blocks, zones), plan=_plan_chips, own=self.chip))
        self.tokens.append(token)

    def collect(self, which, after, name):
        sel = [e for e in self.exchanges if GRAD_GROUPS.index(e["names"]) in which]
        out = {}
        for e, (blocks, zones) in zip(sel, _exchange_wait([e["state"] for e in sel], sel[0]["plan"], after, name)):
            out.update({n: (b, e["own"], z) for n, b, z in zip(e["names"], blocks, zones)})
        return out


def kernel(x, p, w_in, w_sb_out, w_ca_out, w_mix_out, rel_bias, g_mix, g_ffn, g_ple, g_final, w_ffn_in, w_ffn_out, w_ple_in, w_ple_gate, loss_target, m_w_in, m_w_sb_out, m_w_ca_out, m_w_mix_out, m_rel_bias, m_g_mix, m_g_ffn, m_g_ple, m_g_final, m_w_ffn_in, m_w_ffn_out, m_w_ple_in, m_w_ple_gate, v_w_in, v_w_sb_out, v_w_ca_out, v_w_mix_out, v_rel_bias, v_g_mix, v_g_ffn, v_g_ple, v_g_final, v_w_ffn_in, v_w_ffn_out, v_w_ple_in, v_w_ple_gate):
    wts = dict(w_in=w_in, w_sb_out=w_sb_out, w_ca_out=w_ca_out, w_mix_out=w_mix_out, rel_bias=rel_bias, g_mix=g_mix, g_ffn=g_ffn,
               g_ple=g_ple, g_final=g_final, w_ffn_in=w_ffn_in, w_ffn_out=w_ffn_out, w_ple_in=w_ple_in, w_ple_gate=w_ple_gate)
    mom = dict(w_in=m_w_in, w_sb_out=m_w_sb_out, w_ca_out=m_w_ca_out, w_mix_out=m_w_mix_out, rel_bias=m_rel_bias, g_mix=m_g_mix,
               g_ffn=m_g_ffn, g_ple=m_g_ple, g_final=m_g_final, w_ffn_in=m_w_ffn_in, w_ffn_out=m_w_ffn_out, w_ple_in=m_w_ple_in,
               w_ple_gate=m_w_ple_gate)
    var = dict(w_in=v_w_in, w_sb_out=v_w_sb_out, w_ca_out=v_w_ca_out, w_mix_out=v_w_mix_out, rel_bias=v_rel_bias, g_mix=v_g_mix,
               g_ffn=v_g_ffn, g_ple=v_g_ple, g_final=v_g_final, w_ffn_in=v_w_ffn_in, w_ffn_out=v_w_ffn_out, w_ple_in=v_w_ple_in,
               w_ple_gate=v_w_ple_gate)
    T, D = x.shape[1], x.shape[2]
    shard = {n: wts[n].reshape(wts[n].shape[-2:]) for n in BIG}
    bf = {n: _cast_bf16(shard[n], "cast_" + n) for n in BIG}
    half = bf["w_in"].shape[0] // 2
    bf["w_in_a"], bf["w_in_b"] = bf["w_in"][:half], bf.pop("w_in")[half:]
    comm = _Exchange(bf)
    g = dict(g_mix=g_mix.reshape(1, D), g_ffn=g_ffn.reshape(1, D), g_ple=g_ple.reshape(1, D), g_final=g_final.reshape(1, D),
             rel_bias=rel_bias.reshape(rel_bias.shape[-2:]))

    loss, grad_x, dsmall = _local_step(x.reshape(T, D), p.reshape(T, -1), loss_target.reshape(T, D), comm, g)
    loss = lax.psum(loss[0, 0], ("x", "y", "c"))

    grad, delta, new_m, new_v = {}, {}, {}, {}

    def update(parts):
        done = []
        for n, (blocks, own, zone) in parts.items():
            outs = _reduce_adamw(blocks, zone, own, shard[n], mom[n].reshape(shard[n].shape), var[n].reshape(shard[n].shape), "adamw_" + n)
            grad[n], delta[n], new_m[n], new_v[n] = [o.reshape(wts[n].shape) for o in outs]
            done.append(outs[0])
        return done

    done = update(comm.collect(range(len(GRAD_GROUPS) - 1), grad_x, "exchange_wait_rest"))
    outs = _small_step(_pack_small(dsmall, D), _pack_small(wts, D), _pack_small(mom, D), _pack_small(var, D), done)
    for dst, a in zip((grad, delta, new_m, new_v), outs):
        dst.update(_unpack_small(a, wts))
    update(comm.collect([len(GRAD_GROUPS) - 1], outs[0], "exchange_wait_w_in"))

    return (loss, grad_x.reshape(x.shape), *[grad[n] for n in WEIGHTS], *[delta[n] for n in WEIGHTS],
            *[new_m[n] for n in WEIGHTS], *[new_v[n] for n in WEIGHTS])
```

```python
import functools

import jax
import jax.numpy as jnp
from jax import lax
from jax.experimental import pallas as pl
from jax.experimental.pallas import tpu as pltpu

F32, BF16 = jnp.float32, jnp.bfloat16

N_DEV = 8
HEAD_DIM = 128
CHUNK = 64
LEFT_CHUNKS = 8
REL_CLIP = 128
N_REL = REL_CLIP + CHUNK
PAIR = 2 * CHUNK
PBAND = (LEFT_CHUNKS + 2) * CHUNK
CA_PAIRS = 2
CA_ROWS = CA_PAIRS * PAIR
CA_BAND = PBAND + CA_ROWS - PAIR
PAD = LEFT_CHUNKS * CHUNK
SB_BLOCK = 256
ROWS = 256
EPS = 1e-6
NEG = -1e30
SCALE = HEAD_DIM ** -0.5
VMEM_LIMIT_BYTES = 56 * 1024 * 1024

ADAM_LR, ADAM_B1, ADAM_B2, ADAM_EPS, ADAM_WD, ADAM_STEP = 0.001, 0.9, 0.999, 1e-08, 0.01, 10

ANY = pl.BlockSpec(memory_space=pl.ANY)
NN = (((1,), (0,)), ((), ()))
NT = (((1,), (1,)), ((), ()))
TN = (((0,), (0,)), ((), ()))
MESH = pl.DeviceIdType.MESH


def _params(*sem):
    return pltpu.CompilerParams(dimension_semantics=sem or None, vmem_limit_bytes=VMEM_LIMIT_BYTES)


def _dot(a, b, dims=NN):
    return lax.dot_general(a, b, dims, preferred_element_type=F32)


def _mm(a, b, *, mode, tm, tn, tk, out_dtype, name, b_blocked=False, out_block=None, res=None, after=(), a_cols=(1, 0), out_cols=(1, 1, 0), into=None, b_first=0, b_count=None, o_first=0, o_count=None):
    bg = og = 1
    if mode == "nn":
        M, K = a.shape
        a_spec = pl.BlockSpec((tm, tk), lambda i, j, k: (i, k))
        if b_blocked:
            G, _, nb = b.shape
            N = G * nb
            if tn > nb:
                bg = tn // nb
                assert tn % nb == 0
                b_spec = pl.BlockSpec((bg, tk, nb), lambda i, j, k: (j, k, 0))
            else:
                per = nb // tn
                assert nb % tn == 0
                b_spec = pl.BlockSpec((None, tk, tn), lambda i, j, k: (j // per, k, j % per))
        else:
            N = b.shape[1]
            b_spec = pl.BlockSpec((tk, tn), lambda i, j, k: (k, j))
        dims = NN
    elif mode == "nt":
        M, K = a.shape
        a_spec = pl.BlockSpec((tm, tk), lambda i, j, k: (i, k))
        if b_blocked:
            G, N, nb = b.shape
            K = (b_count or G) * nb
            assert b_first == 0 or tk == nb
            if tk > nb:
                bg = tk // nb
                assert tk % nb == 0
                b_spec = pl.BlockSpec((bg, tn, nb), lambda i, j, k: (k, j, 0))
            else:
                per = nb // tk
                assert nb % tk == 0
                b_spec = pl.BlockSpec((None, tn, tk), lambda i, j, k: (b_first + k // per, j, k % per))
        else:
            N = b.shape[0]
            b_spec = pl.BlockSpec((tn, tk), lambda i, j, k: (j, k))
        dims = NT
    else:
        K, M = a.shape
        N = b.shape[1]
        a_spec = pl.BlockSpec((tk, tm), lambda i, j, k: (k, i))
        b_spec = pl.BlockSpec((tk, tn), lambda i, j, k: (k, j))
        dims = TN
    if mode != "tn":
        if mode == "nn":
            K = b.shape[-2]
        elif not b_blocked:
            K = b.shape[1]
        a_spec = pl.BlockSpec((tm, tk), lambda i, j, k: (i, k * a_cols[0] + a_cols[1]))
    assert M % tm == 0 and N % tn == 0 and K % tk == 0, (name, M, N, K, tm, tn, tk)
    nk = K // tk
    if out_block is None:
        out_shape = jax.ShapeDtypeStruct((M, N * out_cols[0]), out_dtype)
        o_spec = pl.BlockSpec((tm, tn), lambda i, j, k: (i, j * out_cols[1] + out_cols[2]))
    else:
        out_shape = jax.ShapeDtypeStruct((o_count or N // out_block, M, out_block), out_dtype)
        if tn > out_block:
            og = tn // out_block
            assert tn % out_block == 0 and o_first % og == 0
            o_spec = pl.BlockSpec((og, tm, out_block), lambda i, j, k: (o_first // og + j, i, 0))
        else:
            per_o = out_block // tn
            assert out_block % tn == 0
            o_spec = pl.BlockSpec((None, tm, tn), lambda i, j, k: (o_first + j // per_o, i, j % per_o))
    in_specs = [a_spec, b_spec]
    args = [a, b]
    if res is not None:
        in_specs.append(pl.BlockSpec((tm, tn), lambda i, j, k: (i, j * out_cols[1] + out_cols[2])))
        args.append(res)
    n_in = len(args) + len(after) + (into is not None)

    def product(a_ref, b_ref):
        if bg == 1:
            return _dot(a_ref[...], b_ref[...], dims)
        nb = b_ref.shape[2]
        if mode == "nn":
            return jnp.concatenate([_dot(a_ref[...], b_ref[g], dims) for g in range(bg)], axis=1)
        return sum(_dot(a_ref[:, g * nb:(g + 1) * nb], b_ref[g], dims) for g in range(bg))

    def body(*refs):
        a_ref, b_ref = refs[0], refs[1]
        r_ref = refs[2] if res is not None else None
        o_ref = refs[n_in]

        def finish(acc):
            if r_ref is not None:
                acc = acc + r_ref[...]
            if og == 1:
                o_ref[...] = acc.astype(o_ref.dtype)
            else:
                for g in range(og):
                    o_ref[g] = acc[:, g * out_block:(g + 1) * out_block].astype(o_ref.dtype)

        if nk == 1:
            finish(product(a_ref, b_ref))
        else:
            acc_ref = refs[-1]
            k = pl.program_id(2)

            @pl.when(k == 0)
            def _():
                acc_ref[...] = jnp.zeros_like(acc_ref)

            acc_ref[...] += product(a_ref, b_ref)

            @pl.when(k == nk - 1)
            def _():
                finish(acc_ref[...])

    return pl.pallas_call(
        body, grid=(M // tm, N // tn, nk), in_specs=in_specs + [ANY] * (n_in - len(args)), out_specs=o_spec, out_shape=out_shape,
        scratch_shapes=[] if nk == 1 else [pltpu.VMEM((tm, tn), F32)], input_output_aliases={} if into is None else {n_in - 1: 0},
        compiler_params=_params("parallel", "parallel", "arbitrary"), name=name)(*args, *after, *(() if into is None else (into,)))


def _mm_fused(a, b, tiles, fn, outs, *, mode, tm, tn, tk, name, sums=(), after=(), b_outer=False):
    M, K = a.shape
    N = b.shape[1] if mode == "nn" else b.shape[0]
    nk = K // tk
    assert M % tm == 0 and N % tn == 0 and K % tk == 0 and (not sums or tn == N)
    def at(f):
        return (lambda j, i, k: f(i, j, k)) if b_outer else f

    b_spec = pl.BlockSpec((tk, tn), at(lambda i, j, k: (k, j))) if mode == "nn" else pl.BlockSpec((tn, tk), at(lambda i, j, k: (j, k)))
    in_specs = [pl.BlockSpec((tm, tk), at(lambda i, j, k: (i, k))), b_spec]
    args = [a, b]
    for t in tiles:
        if isinstance(t, tuple):
            arr, off = t
            in_specs.append(pl.BlockSpec((tm, tn), at(lambda i, j, k, off=off: (i, off + j))))
        else:
            arr = t
            in_specs.append(pl.BlockSpec((1, tn), at(lambda i, j, k: (0, j))))
        args.append(arr)
    n_in = len(args) + len(after)
    n_out = len(outs) + len(sums)

    def body(*refs):
        a_ref, b_ref = refs[0], refs[1]
        t_refs = refs[2:2 + len(tiles)]
        o_refs = refs[n_in:n_in + n_out]

        def finish(acc):
            res = fn(acc, *[t[...] for t in t_refs])
            for o_ref, r in zip(o_refs[:len(outs)], res):
                o_ref[...] = r.astype(o_ref.dtype)
            if sums:
                @pl.when(pl.program_id(0) == 0)
                def _():
                    for o_ref in o_refs[len(outs):]:
                        o_ref[...] = jnp.zeros_like(o_ref)

                for o_ref, r in zip(o_refs[len(outs):], res[len(outs):]):
                    o_ref[...] += jnp.broadcast_to(r, o_ref.shape)

        if nk == 1:
            finish(_dot(a_ref[...], b_ref[...], NN if mode == "nn" else NT))
        else:
            acc_ref = refs[-1]
            k = pl.program_id(2)

            @pl.when(k == 0)
            def _():
                acc_ref[...] = jnp.zeros_like(acc_ref)

            acc_ref[...] += _dot(a_ref[...], b_ref[...], NN if mode == "nn" else NT)

            @pl.when(k == nk - 1)
            def _():
                finish(acc_ref[...])

    assert not (b_outer and sums)
    o_spec = pl.BlockSpec((tm, tn), at(lambda i, j, k: (i, j)))
    return pl.pallas_call(
        body, grid=(N // tn, M // tm, nk) if b_outer else (M // tm, N // tn, nk), in_specs=in_specs + [ANY] * len(after),
        out_specs=[o_spec] * len(outs) + [pl.BlockSpec(sh, lambda i, j, k: (0, 0)) for sh in sums],
        out_shape=[jax.ShapeDtypeStruct((M, N), dt) for dt in outs] + [jax.ShapeDtypeStruct(sh, F32) for sh in sums],
        scratch_shapes=[] if nk == 1 else [pltpu.VMEM((tm, tn), F32)],
        compiler_params=_params("arbitrary" if sums else "parallel", "parallel", "arbitrary"), name=name)(*args, *after)


def _row_spec(d, col=0):
    return pl.BlockSpec((ROWS, d), lambda i: (i, col))


def _vec_spec(d):
    return pl.BlockSpec((1, d), lambda i: (0, 0))


def _rms(x):
    return lax.rsqrt(jnp.mean(x * x, axis=-1, keepdims=True) + EPS)


def _norm_fwd(x, g, name):
    T, D = x.shape

    def body(x_ref, g_ref, h_ref):
        xv = x_ref[...]
        h_ref[...] = (xv * _rms(xv) * g_ref[...]).astype(BF16)

    return pl.pallas_call(body, grid=(T // ROWS,), in_specs=[_row_spec(D), _vec_spec(D)], out_specs=_row_spec(D),
                          out_shape=jax.ShapeDtypeStruct((T, D), BF16), compiler_params=_params("parallel"), name=name)(x, g)


def _residual_norm(y, x, g):
    x = x + y
    return x, x * _rms(x) * g


def _norm_bwd_math(dh, xv, gv):
    r = _rms(xv)
    xhat = xv * r
    dxhat = dh * gv
    dx = r * (dxhat - xhat * jnp.mean(dxhat * xhat, axis=-1, keepdims=True))
    dg = jnp.sum(dh * xhat, axis=0, keepdims=True)
    return dx, dg


def _residual_norm_bwd(dh, x, dres, g):
    dx, dg = _norm_bwd_math(dh, x, g)
    dx = dx + dres
    return dx, dx, dg


def _norm_bwd(dh, x, g, dres, name):
    T, D = x.shape

    def body(dh_ref, x_ref, g_ref, dres_ref, dx_ref, dxb_ref, dg_ref):
        dx, dg = _norm_bwd_math(dh_ref[...], x_ref[...], g_ref[...])
        dx = dx + dres_ref[...]
        dx_ref[...] = dx
        dxb_ref[...] = dx.astype(BF16)

        @pl.when(pl.program_id(0) == 0)
        def _():
            dg_ref[...] = jnp.zeros_like(dg_ref)

        dg_ref[...] += dg

    return pl.pallas_call(
        body, grid=(T // ROWS,), in_specs=[_row_spec(D), _row_spec(D), _vec_spec(D), _row_spec(D)],
        out_specs=[_row_spec(D), _row_spec(D), _vec_spec(D)],
        out_shape=[jax.ShapeDtypeStruct((T, D), F32), jax.ShapeDtypeStruct((T, D), BF16), jax.ShapeDtypeStruct((1, D), F32)],
        compiler_params=_params("arbitrary"), name=name)(dh, x, g, dres)


def _mm_merge(y_sb, y_ca, w_sb, w_ca, proj, gate_col, tm, tn, after):
    T, W = y_sb.shape
    G, _, nb = w_sb.shape
    D, bg = G * nb, tn // nb
    assert tn % nb == 0 and D % tn == 0
    per = D // tn

    def body(ys_ref, yc_ref, ws_ref, wc_ref, gs_ref, gc_ref, *rest):
        as_ref, ac_ref, m_ref = rest[len(after):]
        a = jnp.concatenate([_dot(ys_ref[...], ws_ref[g]) for g in range(bg)], axis=1)
        b = jnp.concatenate([_dot(yc_ref[...], wc_ref[g]) for g in range(bg)], axis=1)
        as_ref[...] = a
        ac_ref[...] = b
        m_ref[...] = (jax.nn.sigmoid(gs_ref[...]) * a + jax.nn.sigmoid(gc_ref[...]) * b).astype(BF16)

    y_spec = pl.BlockSpec((tm, W), lambda i, j: (i, 0))
    w_spec = pl.BlockSpec((bg, W, nb), lambda i, j: (j, 0, 0))
    out = pl.BlockSpec((tm, tn), lambda i, j: (i, j))
    f32 = jax.ShapeDtypeStruct((T, D), F32)
    return pl.pallas_call(
        body, grid=(T // tm, per),
        in_specs=[y_spec, y_spec, w_spec, w_spec, pl.BlockSpec((tm, tn), lambda i, j: (i, gate_col * per + j)),
                  pl.BlockSpec((tm, tn), lambda i, j: (i, (gate_col + 1) * per + j))] + [ANY] * len(after),
        out_specs=[out, out, out], out_shape=[f32, f32, jax.ShapeDtypeStruct((T, D), BF16)],
        compiler_params=_params("parallel", "parallel"), name="mm_merge")(y_sb, y_ca, w_sb, w_ca, proj, proj, *after)


def _merge_bwd(dm, gs, gc, a, b):
    ss, sc = jax.nn.sigmoid(gs), jax.nn.sigmoid(gc)
    return dm * ss, dm * sc, dm * a * ss * (1.0 - ss), dm * b * sc * (1.0 - sc)


def _mm_swiglu(h, w, tm, after=()):
    T, D = h.shape
    G2, _, nb = w.shape
    G = G2 // 2

    def body(h_ref, wg_ref, wu_ref, *rest):
        g_ref, u_ref, act_ref = rest[len(after):]
        hv = h_ref[...]
        gv = _dot(hv, wg_ref[...])
        uv = _dot(hv, wu_ref[...])
        g_ref[...] = gv.astype(BF16)
        u_ref[...] = uv.astype(BF16)
        act_ref[...] = (gv * jax.nn.sigmoid(gv) * uv).astype(BF16)

    out = pl.BlockSpec((tm, nb), lambda j, i: (i, j))
    kept = jax.ShapeDtypeStruct((T, G * nb), BF16)
    return pl.pallas_call(
        body, grid=(G, T // tm),
        in_specs=[pl.BlockSpec((tm, D), lambda j, i: (i, 0)), pl.BlockSpec((None, D, nb), lambda j, i: (j, 0, 0)),
                  pl.BlockSpec((None, D, nb), lambda j, i: (j + G, 0, 0))] + [ANY] * len(after),
        out_specs=[out, out, out], out_shape=[kept, kept, kept],
        compiler_params=_params("parallel", "parallel"), name="mm_ffn_in")(h, w, w, *after)


def _swiglu_bwd(dact, gate, up):
    gate, up = gate.astype(F32), up.astype(F32)
    s = jax.nn.sigmoid(gate)
    return dact * up * s * (1.0 + gate * (1.0 - s)), dact * gate * s


def _tail(zg, x3, pe, target, g_final):
    D = x3.shape[-1]
    gate = jax.nn.sigmoid(zg)
    x4 = x3 + gate * pe
    err = x4 * _rms(x4) * g_final - target
    part = 0.5 * jnp.sum(jnp.mean(err * err, axis=-1, keepdims=True), axis=0, keepdims=True)
    dx, dg = _norm_bwd_math(err * (1.0 / D), x4, g_final)
    return dx, dx * gate, dx * pe * gate * (1.0 - gate), part, dg


def _cast_bf16(x, name):
    R, C = x.shape
    rows = next(r for r in (ROWS, 128, 64, 32, 16) if R % r == 0)

    def body(x_ref, o_ref):
        o_ref[...] = x_ref[...].astype(BF16)

    spec = pl.BlockSpec((rows, C), lambda i: (i, 0))
    return pl.pallas_call(body, grid=(R // rows,), in_specs=[spec], out_specs=spec, out_shape=jax.ShapeDtypeStruct((R, C), BF16),
                          compiler_params=_params("parallel"), name=name)(x)


def _head_spec(T, col0, heads=1, single=False):
    return pl.BlockSpec((T, heads * HEAD_DIM), lambda h, *_: (0, col0 + h), pipeline_mode=pl.Buffered(1) if single else None)


SB_HEADS = 4


def _triangle(n, right):
    j = lax.broadcasted_iota(jnp.int32, (n, n), 0)
    s = lax.broadcasted_iota(jnp.int32, (n, n), 1)
    return jnp.where((j > s) if right else (j < s), 1.0, 0.0).astype(BF16)


def _lane_scan(x, tri):
    hi = x.astype(BF16)
    lo = (x - hi.astype(F32)).astype(BF16)
    return _dot(hi, tri) + _dot(lo, tri)


def _head_cols(ref, rows, hh):
    return ref[rows, hh * HEAD_DIM:(hh + 1) * HEAD_DIM]


def _sb_tile(qv, kk, past, c_lk, tri):
    z = _dot(qv, kk, NT) * SCALE
    sp = jnp.log(1.0 + jnp.exp(-jnp.abs(z)))
    ls_pos = jnp.minimum(z, 0.0) - sp
    lk = jnp.minimum(-z, 0.0) - sp
    if past is not None:
        lk = jnp.where(past, lk, 0.0)
    right = c_lk + _lane_scan(lk, tri)
    a = jnp.exp(ls_pos + right)
    if past is not None:
        a = jnp.where(past, a, 0.0)
    return ls_pos, a, right[:, 0:1] + lk[:, 0:1]


SB_Q = 512
SB_HEADS_BWD = 4


def _sb_mask(d):
    B, r = SB_BLOCK, SB_Q // SB_BLOCK
    return lax.broadcasted_iota(jnp.int32, (SB_Q, B), 1) + (r - 1 - d) * B < lax.broadcasted_iota(jnp.int32, (SB_Q, B), 0)


def _sb_rows(kb):
    return pl.ds(pl.multiple_of(kb * SB_BLOCK, SB_BLOCK), SB_BLOCK)


def _sb_fwd(proj, n_heads, after=()):
    T = proj.shape[0]
    B, Q, HP = SB_BLOCK, SB_Q, SB_HEADS
    r = Q // B
    assert n_heads % HP == 0 and T % Q == 0

    def body(q_ref, k_ref, v_ref, *rest):
        y_ref = rest[-1]
        qb = pl.program_id(1)
        tri = _triangle(B, right=True)
        qv = [_head_cols(q_ref, slice(None), hh).astype(BF16) for hh in range(HP)]

        def tile(kb, carry, past):
            out = []
            for hh in range(HP):
                acc, c_lk = carry[hh]
                kk = _head_cols(k_ref, _sb_rows(kb), hh).astype(BF16)
                vv = _head_cols(v_ref, _sb_rows(kb), hh).astype(BF16)
                _, a, c_lk = _sb_tile(qv[hh], kk, past, c_lk, tri)
                out.append((acc + _dot(a.astype(BF16), vv), c_lk))
            return tuple(out)

        carry = tuple((jnp.zeros((Q, HEAD_DIM), F32), jnp.zeros((Q, 1), F32)) for _ in range(HP))
        for d in range(r):
            carry = tile(r * qb + r - 1 - d, carry, _sb_mask(d))
        res = lax.fori_loop(0, r * qb, lambda i, c: tile(r * qb - 1 - i, c, None), carry)
        for hh in range(HP):
            y_ref[:, hh * HEAD_DIM:(hh + 1) * HEAD_DIM] = res[hh][0].astype(BF16)

    blk = pl.BlockSpec((Q, HP * HEAD_DIM), lambda h, i: (i, h))
    G = n_heads // HP
    return pl.pallas_call(
        body, grid=(G, T // Q),
        in_specs=[blk, _head_spec(T, G, HP), _head_spec(T, 2 * G, HP)] + [ANY] * len(after), out_specs=blk,
        out_shape=jax.ShapeDtypeStruct((T, n_heads * HEAD_DIM), BF16),
        compiler_params=_params("parallel", "arbitrary"), name="sb_fwd")(proj, proj, proj, *after)


def _sb_bwd(proj, dy, n_heads):
    T = proj.shape[0]
    B, Q, HP = SB_BLOCK, SB_Q, SB_HEADS_BWD
    r, nq = Q // B, T // Q

    def body(q_ref, k_ref, v_ref, dy_ref, dq_ref, dk_ref, dv_ref, g_s, sig_s, dk_s, dv_s):
        qb = pl.program_id(1)

        @pl.when(qb == 0)
        def _():
            dk_s[...] = jnp.zeros_like(dk_s)
            dv_s[...] = jnp.zeros_like(dv_s)

        tri_r = _triangle(B, right=True)
        tri_l = _triangle(B, right=False)
        qv = [_head_cols(q_ref, slice(None), hh).astype(BF16) for hh in range(HP)]
        dyb = [_head_cols(dy_ref, slice(None), hh).astype(BF16) for hh in range(HP)]

        def sweep(kb, carry, past):
            out = []
            for hh in range(HP):
                kk = _head_cols(k_ref, _sb_rows(kb), hh).astype(BF16)
                vv = _head_cols(v_ref, _sb_rows(kb), hh).astype(BF16)
                ls_pos, a, c_lk = _sb_tile(qv[hh], kk, past, carry[hh], tri_r)
                g_s[hh, kb] = _dot(dyb[hh], vv, NT) * a
                sig_s[hh, kb] = jnp.exp(ls_pos).astype(BF16)
                dv_s[_sb_rows(kb), hh * HEAD_DIM:(hh + 1) * HEAD_DIM] += _dot(a.astype(BF16), dyb[hh], TN)
                out.append(c_lk)
            return tuple(out)

        carry = tuple(jnp.zeros((Q, 1), F32) for _ in range(HP))
        for d in range(r):
            carry = sweep(r * qb + r - 1 - d, carry, _sb_mask(d))
        lax.fori_loop(0, r * qb, lambda i, c: sweep(r * qb - 1 - i, c, None), carry)

        def back(kb, carry, past):
            out = []
            for hh in range(HP):
                dq, c_g = carry[hh]
                kk = _head_cols(k_ref, _sb_rows(kb), hh).astype(BF16)
                g, sig = g_s[hh, kb], sig_s[hh, kb].astype(F32)
                left = c_g + _lane_scan(g, tri_l)
                dz = g * (1.0 - sig) - left * sig
                if past is not None:
                    dz = jnp.where(past, dz, 0.0)
                dz = (dz * SCALE).astype(BF16)
                dk_s[_sb_rows(kb), hh * HEAD_DIM:(hh + 1) * HEAD_DIM] += _dot(dz, qv[hh], TN)
                out.append((dq + _dot(dz, kk), left[:, B - 1:B] + g[:, B - 1:B]))
            return tuple(out)

        init = tuple((jnp.zeros((Q, HEAD_DIM), F32), jnp.zeros((Q, 1), F32)) for _ in range(HP))
        res = lax.fori_loop(0, r * qb, lambda kb, c: back(kb, c, None), init)
        for d in reversed(range(r)):
            res = back(r * qb + r - 1 - d, res, _sb_mask(d))
        for hh in range(HP):
            dq_ref[:, hh * HEAD_DIM:(hh + 1) * HEAD_DIM] = res[hh][0].astype(BF16)

        @pl.when(qb == nq - 1)
        def _():
            dk_ref[...] = dk_s[...].astype(BF16)
            dv_ref[...] = dv_s[...].astype(BF16)

    blk = pl.BlockSpec((Q, HP * HEAD_DIM), lambda h, i: (i, h))
    G = n_heads // HP
    full = _head_spec(T, 0, HP, single=True)
    shp = jax.ShapeDtypeStruct((T, n_heads * HEAD_DIM), BF16)
    return pl.pallas_call(
        body, grid=(G, nq),
        in_specs=[blk, _head_spec(T, G, HP, single=True), _head_spec(T, 2 * G, HP, single=True), blk], out_specs=[blk, full, full],
        out_shape=[shp, shp, shp],
        scratch_shapes=[pltpu.VMEM((HP, T // B, Q, B), F32), pltpu.VMEM((HP, T // B, Q, B), BF16)] + [pltpu.VMEM((T, HP * HEAD_DIM), F32)] * 2,
        compiler_params=_params("parallel", "arbitrary"), name="sb_bwd")(proj, proj, proj, dy)


DIAGS = PBAND + PAIR


def _diag_onehot():
    d = lax.broadcasted_iota(jnp.int32, (DIAGS, 2 * PAIR), 0)
    r = lax.broadcasted_iota(jnp.int32, (DIAGS, 2 * PAIR), 1)
    return jnp.where(jnp.clip(d - PAIR - PAD, -REL_CLIP, CHUNK - 1) + REL_CLIP == r, 1.0, 0.0)


def _bias_expand(rel_bias):
    H = rel_bias.shape[0]
    table = jnp.pad(rel_bias, ((0, 0), (0, 2 * PAIR - N_REL)))

    def body(rb_ref, o_ref):
        o_ref[...] = lax.dot_general(rb_ref[...], _diag_onehot(), NT, precision=lax.Precision.HIGHEST, preferred_element_type=F32)

    per_diag = pl.pallas_call(body, out_shape=jax.ShapeDtypeStruct((H, DIAGS), F32), name="bias_expand")(table)
    flat = jnp.tile(jnp.pad(per_diag, ((0, 0), (0, 1))), (1, PAIR))[:, :PAIR * DIAGS]
    return flat.reshape(H, PAIR, DIAGS)[:, :, PAIR:]


def _bias_reduce(dbias):
    H = dbias.shape[0]
    padded = jnp.pad(dbias, ((0, 0), (0, 1), (PAIR, 0))).reshape(H, -1)
    skewed = padded[:, :PAIR * (DIAGS + 1)].reshape(H, PAIR, DIAGS + 1)[:, :, :DIAGS]

    def body(s_ref, o_ref):
        per_diag = jnp.sum(s_ref[...], axis=0, keepdims=True)
        o_ref[...] = lax.dot_general(jnp.broadcast_to(per_diag, (8, DIAGS)), _diag_onehot(), NN, precision=lax.Precision.HIGHEST,
                                     preferred_element_type=F32)[0:1]

    return pl.pallas_call(
        body, grid=(H,), in_specs=[pl.BlockSpec((None, PAIR, DIAGS), lambda h: (h, 0, 0))],
        out_specs=pl.BlockSpec((None, 1, 2 * PAIR), lambda h: (h, 0, 0)),
        out_shape=jax.ShapeDtypeStruct((H, 1, 2 * PAIR), F32), compiler_params=_params("parallel"), name="bias_reduce")(skewed)[:, 0]


CA_HEADS = 2


def _ca_mask():
    i = lax.broadcasted_iota(jnp.int32, (CA_ROWS, CA_BAND), 0)
    j = lax.broadcasted_iota(jnp.int32, (CA_ROWS, CA_BAND), 1)
    qc, kc = i // CHUNK, j // CHUNK
    return j, (kc >= qc) & (kc <= qc + LEFT_CHUNKS)


def _ca_bias(pair_bias):
    rows = []
    for q in range(CA_PAIRS):
        parts = [jnp.zeros((PAIR, q * PAIR), F32)] * (q > 0) + [pair_bias] + [jnp.zeros((PAIR, (CA_PAIRS - 1 - q) * PAIR), F32)] * (q < CA_PAIRS - 1)
        rows.append(jnp.concatenate(parts, axis=1) if len(parts) > 1 else parts[0])
    return jnp.concatenate(rows, axis=0)


def _ca_weights(pr, qp, kb, bias, j, window):
    valid = window & (pr * CA_ROWS + j >= PAD)
    z = jnp.where(valid, _dot(qp, kb, NT) * SCALE + bias, NEG)
    e = jnp.exp(z - jnp.max(z, axis=1, keepdims=True))
    return e / jnp.sum(e, axis=1, keepdims=True)


def _ca_fill(k_ref, v_ref, kpad, vpad):
    T, W = k_ref.shape
    kpad[0:PAD, :] = jnp.zeros((PAD, W), BF16)
    vpad[0:PAD, :] = jnp.zeros((PAD, W), BF16)
    kpad[PAD:PAD + T, :] = k_ref[...].astype(BF16)
    vpad[PAD:PAD + T, :] = v_ref[...].astype(BF16)


def _ca_fwd(proj, bias, n_heads, col0):
    T = proj.shape[0]
    HP = CA_HEADS
    G = n_heads // HP
    assert n_heads % HP == 0 and col0 % HP == 0

    def body(q_ref, k_ref, v_ref, b_ref, y_ref, kpad, vpad):
        _ca_fill(k_ref, v_ref, kpad, vpad)
        j, window = _ca_mask()
        bias = [_ca_bias(b_ref[hh]) for hh in range(HP)]

        def step(pr, _):
            r0 = pl.multiple_of(pr * CA_ROWS, CA_ROWS)
            for hh in range(HP):
                qp = _head_cols(q_ref, pl.ds(r0, CA_ROWS), hh).astype(BF16)
                kb = _head_cols(kpad, pl.ds(r0, CA_BAND), hh)
                vb = _head_cols(vpad, pl.ds(r0, CA_BAND), hh)
                w = _ca_weights(pr, qp, kb, bias[hh], j, window)
                y_ref[pl.ds(r0, CA_ROWS), hh * HEAD_DIM:(hh + 1) * HEAD_DIM] = _dot(w.astype(BF16), vb).astype(BF16)
            return 0

        lax.fori_loop(0, T // CA_ROWS, step, 0)

    c = col0 // HP
    return pl.pallas_call(
        body, grid=(G,),
        in_specs=[_head_spec(T, c, HP), _head_spec(T, c + G, HP), _head_spec(T, c + 2 * G, HP),
                  pl.BlockSpec((HP, PAIR, PBAND), lambda h: (h, 0, 0))],
        out_specs=_head_spec(T, 0, HP), out_shape=jax.ShapeDtypeStruct((T, n_heads * HEAD_DIM), BF16),
        scratch_shapes=[pltpu.VMEM((PAD + T, HP * HEAD_DIM), BF16)] * 2,
        compiler_params=_params("parallel"), name="ca_fwd")(proj, proj, proj, bias)


def _ca_bwd(proj, bias, dy, n_heads, col0):
    T = proj.shape[0]
    HP = CA_HEADS
    G = n_heads // HP

    def body(q_ref, k_ref, v_ref, b_ref, dy_ref, dq_ref, dk_ref, dv_ref, db_ref, kpad, vpad, dkpad, dvpad):
        _ca_fill(k_ref, v_ref, kpad, vpad)
        dkpad[...] = jnp.zeros_like(dkpad)
        dvpad[...] = jnp.zeros_like(dvpad)
        db_ref[...] = jnp.zeros_like(db_ref)
        j, window = _ca_mask()
        bias = [_ca_bias(b_ref[hh]) for hh in range(HP)]

        def step(pr, _):
            r0 = pl.multiple_of(pr * CA_ROWS, CA_ROWS)
            for hh in range(HP):
                cols = slice(hh * HEAD_DIM, (hh + 1) * HEAD_DIM)
                qp = _head_cols(q_ref, pl.ds(r0, CA_ROWS), hh).astype(BF16)
                kb = _head_cols(kpad, pl.ds(r0, CA_BAND), hh)
                vb = _head_cols(vpad, pl.ds(r0, CA_BAND), hh)
                w = _ca_weights(pr, qp, kb, bias[hh], j, window)
                dyp = _head_cols(dy_ref, pl.ds(r0, CA_ROWS), hh).astype(BF16)
                dw = _dot(dyp, vb, NT)
                dz = w * (dw - jnp.sum(dw * w, axis=1, keepdims=True))
                db_ref[hh] += sum(dz[q * PAIR:(q + 1) * PAIR, q * PAIR:q * PAIR + PBAND] for q in range(CA_PAIRS))
                dzs = (dz * SCALE).astype(BF16)
                dq_ref[pl.ds(r0, CA_ROWS), cols] = _dot(dzs, kb).astype(BF16)
                dkpad[pl.ds(r0, CA_BAND), cols] += _dot(dzs, qp, TN)
                dvpad[pl.ds(r0, CA_BAND), cols] += _dot(w.astype(BF16), dyp, TN)
            return 0

        lax.fori_loop(0, T // CA_ROWS, step, 0)
        dk_ref[...] = dkpad[PAD:PAD + T, :].astype(BF16)
        dv_ref[...] = dvpad[PAD:PAD + T, :].astype(BF16)

    c = col0 // HP
    full = _head_spec(T, 0, HP)
    bspec = pl.BlockSpec((HP, PAIR, PBAND), lambda h: (h, 0, 0))
    shp = jax.ShapeDtypeStruct((T, n_heads * HEAD_DIM), BF16)
    return pl.pallas_call(
        body, grid=(G,),
        in_specs=[_head_spec(T, c, HP), _head_spec(T, c + G, HP), _head_spec(T, c + 2 * G, HP), bspec, full],
        out_specs=[full, full, full, bspec],
        out_shape=[shp, shp, shp, jax.ShapeDtypeStruct((n_heads, PAIR, PBAND), F32)],
        scratch_shapes=[pltpu.VMEM((PAD + T, HP * HEAD_DIM), BF16)] * 2 + [pltpu.VMEM((PAD + T, HP * HEAD_DIM), F32)] * 2,
        compiler_params=_params("parallel"), name="ca_bwd")(proj, proj, proj, bias, dy)


def _local_step(x, p, target, comm, g):
    T, D = x.shape
    H = g["rel_bias"].shape[0]
    W = H * HEAD_DIM
    nb_in = comm.shapes["w_in_a"][2]
    nb_ff = comm.shapes["w_ffn_in"][2]
    nb_o = comm.shapes["w_sb_out"][2]
    nb_p = comm.shapes["w_ple_in"][2]
    tm = min(T, 1024)
    tn = min(D, 1024)
    gate_col = 6 * W // D

    h1 = _norm_fwd(x, g["g_mix"], "norm1")
    bias = _bias_expand(g["rel_bias"])
    pb = _cast_bf16(p, "cast_p")
    comm.stage("norm1", h1, bias, pb)
    proj = _mm(h1, comm.weight("w_in_a", h1), mode="nn", tm=tm, tn=nb_in, tk=D // 2, out_dtype=F32, b_blocked=True,
               after=comm.pending(), name="mm_in_a")
    comm.stage("mm_in_a", proj)
    proj = _mm(h1, comm.weight("w_in_b", proj), mode="nn", tm=tm, tn=nb_in, tk=D // 2, out_dtype=F32, b_blocked=True, a_cols=(1, 1),
               res=proj, after=comm.pending(), name="mm_in")
    comm.stage("mm_in", proj)
    y_sb = _sb_fwd(proj, H, comm.pending())
    y_ca = _ca_fwd(proj, bias, H, 3 * H)
    w_sb_out = comm.weight("w_sb_out", (y_sb, y_ca))
    comm.stage("attention", y_sb, w_sb_out)
    a_sb, a_ca, merged = _mm_merge(y_sb, y_ca, w_sb_out, comm.weight("w_ca_out"), proj, gate_col, min(T, 512), tn, comm.pending())
    x2, h2 = _mm_fused(merged, comm.weight("w_mix_out"), [(x, 0), g["g_ffn"]], _residual_norm, [F32, BF16],
                       mode="nn", tm=min(T, 512), tn=D, tk=D, name="mm_mix")
    w_ffn_in = comm.weight("w_ffn_in", h2)
    comm.stage("mm_mix", w_ffn_in)
    gate, up, act = _mm_swiglu(h2, w_ffn_in, min(T, 512), comm.pending())
    F = act.shape[1]
    tkf = F // 2 if F % 256 == 0 else F
    x3 = _mm(act, comm.weight("w_ffn_out", act), mode="nn", tm=tm, tn=tn, tk=tkf, out_dtype=F32, res=x2, after=comm.pending(), name="mm_ffn_out")
    h3 = _norm_fwd(x3, g["g_ple"], "norm3")
    P = p.shape[1]
    pe = _mm(pb, comm.weight("w_ple_in"), mode="nn", tm=tm, tn=tn, tk=P, out_dtype=F32, b_blocked=True, name="mm_ple_in")
    dx4, dpe, dzg, loss, dg_final = _mm_fused(
        h3, comm.weight("w_ple_gate"), [(x3, 0), (pe, 0), (target, 0), g["g_final"]], _tail, [F32, BF16, BF16],
        mode="nn", tm=min(T, 256), tn=D, tk=D, sums=[(1, 128), (1, D)], name="mm_ple_gate")

    tw = min(D, 1024)
    DW = BF16
    comm.grad("w_ple_in", _mm(pb, dpe, mode="tn", tm=P, tn=nb_p, tk=T, out_dtype=DW, out_block=nb_p, name="mm_d_ple_in"))
    comm.grad("w_ple_gate", _mm(h3, dzg, mode="tn", tm=tw, tn=tn, tk=T, out_dtype=DW, name="mm_d_ple_gate"))
    dx3, dx3b, dg_ple = _mm_fused(dzg, comm.weight("w_ple_gate"), [(x3, 0), (dx4, 0), g["g_ple"]], _residual_norm_bwd, [F32, BF16],
                                  mode="nt", tm=min(T, 256), tn=D, tk=D, sums=[(1, D)], after=comm.pending(), name="mm_dh3")
    comm.grad("w_ffn_out", _mm(act, dx3b, mode="tn", tm=F // 4, tn=tn, tk=T, out_dtype=DW, name="mm_d_ffn_out"))
    dgate, dup = _mm_fused(dx3b, comm.weight("w_ffn_out"), [(gate, 0), (up, 0)], _swiglu_bwd, [BF16, BF16],
                           mode="nt", tm=min(T, 512), tn=nb_ff, tk=D, after=comm.pending(), b_outer=True, name="mm_dact")
    half = comm.shapes["w_ffn_in"][0] // 2
    d_ffn_in = _mm(h2, dgate, mode="tn", tm=tw, tn=nb_ff, tk=T, out_dtype=DW, out_block=nb_ff, o_count=2 * half, name="mm_d_ffn_in_gate")
    comm.grad("w_ffn_in", _mm(h2, dup, mode="tn", tm=tw, tn=nb_ff, tk=T, out_dtype=DW, out_block=nb_ff, o_first=half, o_count=2 * half,
                              into=d_ffn_in, name="mm_d_ffn_in"))
    dh2 = _mm(dgate, comm.weight("w_ffn_in"), mode="nt", tm=tm, tn=D, tk=nb_ff, out_dtype=F32, b_blocked=True, b_count=half,
              after=comm.pending(), name="mm_dh2_gate")
    dh2 = _mm(dup, comm.weight("w_ffn_in"), mode="nt", tm=min(T, 512), tn=D, tk=nb_ff, out_dtype=F32, b_blocked=True, b_first=half, b_count=half,
              res=dh2, name="mm_dh2")
    dx2, dx2b, dg_ffn = _norm_bwd(dh2, x2, g["g_ffn"], dx3, "norm2_bwd")
    per = D // tn
    da_sb, da_ca, dgate_sb, dgate_ca = _mm_fused(
        dx2b, comm.weight("w_mix_out"), [(proj, gate_col * per), (proj, (gate_col + 1) * per), (a_sb, 0), (a_ca, 0)], _merge_bwd, [BF16] * 4,
        mode="nt", tm=min(T, 512), tn=tn, tk=D, name="mm_dmerged")
    comm.grad("w_mix_out", _mm(merged, dx2b, mode="tn", tm=tw, tn=tn, tk=T, out_dtype=DW, name="mm_d_mix"))
    comm.grad("w_sb_out", _mm(y_sb, da_sb, mode="tn", tm=min(W, 512), tn=tn, tk=T, out_dtype=DW, out_block=nb_o, name="mm_d_sb_out"))
    comm.grad("w_ca_out", _mm(y_ca, da_ca, mode="tn", tm=min(W, 512), tn=tn, tk=T, out_dtype=DW, out_block=nb_o, name="mm_d_ca_out"))
    dy_sb = _mm(da_sb, comm.weight("w_sb_out"), mode="nt", tm=tm, tn=W, tk=tn, out_dtype=BF16, b_blocked=True, after=comm.pending(), name="mm_dy_sb")
    dy_ca = _mm(da_ca, comm.weight("w_ca_out"), mode="nt", tm=tm, tn=W, tk=tn, out_dtype=BF16, b_blocked=True, name="mm_dy_ca")
    dq_sb, dk_sb, dv_sb = _sb_bwd(proj, dy_sb, H)
    dq_ca, dk_ca, dv_ca, dbias = _ca_bwd(proj, bias, dy_ca, H, 3 * H)
    d_rel = _bias_reduce(dbias)[:, :N_REL]
    dproj = jnp.concatenate([dq_sb, dk_sb, dv_sb, dq_ca, dk_ca, dv_ca, dgate_sb, dgate_ca], axis=1)
    comm.grad("w_in", _mm(h1, dproj, mode="tn", tm=tw, tn=nb_in, tk=T, out_dtype=DW, out_block=nb_in, name="mm_d_in"))
    dh1 = _mm(dproj, comm.weight("w_in_a"), mode="nt", tm=tm, tn=D // 2, tk=nb_in, out_dtype=F32, b_blocked=True, out_cols=(2, 1, 0),
              after=comm.pending(), name="mm_dh1_a")
    comm.pair_done(dh1)
    dh1 = _mm(dproj, comm.weight("w_in_b"), mode="nt", tm=tm, tn=D // 2, tk=nb_in, out_dtype=F32, b_blocked=True, out_cols=(2, 1, 1),
              into=dh1, after=comm.pending(), name="mm_dh1")
    grad_x, _, dg_mix = _norm_bwd(dh1, x, g["g_mix"], dx2, "norm1_bwd")
    small = dict(g_mix=dg_mix, g_ffn=dg_ffn, g_ple=dg_ple, g_final=dg_final, rel_bias=d_rel)
    return loss, grad_x, small


def _position():
    x, y, c = lax.axis_index("x"), lax.axis_index("y"), lax.axis_index("c")
    return x, y, c


def _block_of(px, py, pc):
    return 4 * px + 2 * py + pc


def _flip(pos, k):
    x, y, c = pos
    return (1 - x if k & 4 else x, 1 - y if k & 2 else y, 1 - c if k & 1 else c)


HBM = pl.BlockSpec(memory_space=pltpu.HBM)
SEM = pl.BlockSpec(memory_space=pltpu.SEMAPHORE)
VMEM_SPEC = pl.BlockSpec(memory_space=pltpu.VMEM)
EFFECT = pltpu.SideEffectType.DATAFLOW_SIDE_EFFECTING
TOKEN = jax.ShapeDtypeStruct((8, 128), F32)


def _hbm(a):
    return pltpu.HBM(a.shape, a.dtype)


def _landing(shape, dtype):
    return pltpu.with_memory_space_constraint(lax.empty(shape, dtype), pltpu.HBM)


def _gather_start(lands, after, name, relay):
    n = len(lands)

    def body(*refs):
        ins = refs[:n]
        send, recv = refs[n + 1], refs[n + 2]
        token = refs[-1]
        x, y, c = _position()
        mine = _block_of(x, y, c)
        peers = [(x, y, 1 - c), (1 - x, y, c), (x, 1 - y, c), (1 - x, 1 - y, c)][:3 if relay else 4]
        for wi in range(n):
            for k, peer in enumerate(peers):
                pltpu.make_async_remote_copy(
                    src_ref=ins[wi].at[mine], dst_ref=ins[wi].at[mine], send_sem=send.at[4 * wi + k], recv_sem=recv.at[4 * wi + k],
                    device_id=peer, device_id_type=MESH).start()
        token[...] = jnp.zeros_like(token)

    outs = pl.pallas_call(
        body, name=name, in_specs=[HBM] * n + [ANY], out_specs=(SEM, SEM, *[HBM] * n, VMEM_SPEC),
        out_shape=(pltpu.SemaphoreType.DMA((4 * n,)), pltpu.SemaphoreType.DMA((4 * n,)), *[_hbm(a) for a in lands], TOKEN),
        input_output_aliases={i: 2 + i for i in range(n)},
        compiler_params=pltpu.CompilerParams(has_side_effects=EFFECT))(*[pltpu.with_memory_space_constraint(a, pltpu.HBM) for a in lands], after)
    return outs[0], outs[1], list(outs[2:2 + n]), outs[-1]


def _relays(ref, wi, send2, recv2, pos, received):
    x, y, c = pos
    half = ref.shape[1] // 2
    out = []
    for h, (origin, to) in enumerate((((1 - x, y, c), (x, 1 - y, c)), ((x, 1 - y, c), (1 - x, y, c)))):
        rows = ref.at[_block_of(1 - x, 1 - y, c) if received else _block_of(*origin), pl.ds(h * half, half)]
        out.append(pltpu.make_async_remote_copy(src_ref=rows, dst_ref=rows, send_sem=send2.at[2 * wi + h], recv_sem=recv2.at[2 * wi + h],
                                                device_id=to, device_id_type=MESH))
    return out


def _gather_forward(lands, send0, recv0, after, name, relay):
    n = len(lands)

    def body(*refs):
        ins = refs[:n]
        send0, recv0 = refs[n], refs[n + 1]
        send1, recv1, send2, recv2 = refs[n + 2 + len(after):n + 6 + len(after)]
        token = refs[-1]
        x, y, c = _position()
        chips = [(1 - x, y), (x, 1 - y), (1 - x, 1 - y)][:2 if relay else 3]
        for wi in range(n):
            for j, chip in enumerate(chips):
                rows = ins[wi].at[_block_of(*chip, c)]
                pltpu.make_async_remote_copy(
                    src_ref=rows, dst_ref=rows, send_sem=send0.at[4 * wi + 1 + j], recv_sem=recv0.at[4 * wi + 1 + j],
                    device_id=(*chip, c), device_id_type=MESH).wait_recv()
                pltpu.make_async_remote_copy(
                    src_ref=rows, dst_ref=rows, send_sem=send1.at[3 * wi + j], recv_sem=recv1.at[3 * wi + j],
                    device_id=(x, y, 1 - c), device_id_type=MESH).start()
            if relay:
                for sent in _relays(ins[wi], wi, send2, recv2, (x, y, c), False):
                    sent.start()
        token[...] = jnp.zeros_like(token)

    outs = pl.pallas_call(
        body, name=name, in_specs=[HBM] * n + [SEM, SEM] + [ANY] * len(after), out_specs=(SEM, SEM, SEM, SEM, *[HBM] * n, VMEM_SPEC),
        out_shape=(pltpu.SemaphoreType.DMA((3 * n,)), pltpu.SemaphoreType.DMA((3 * n,)), pltpu.SemaphoreType.DMA((2 * n,)),
                   pltpu.SemaphoreType.DMA((2 * n,)), *[_hbm(a) for a in lands], TOKEN),
        input_output_aliases={i: 4 + i for i in range(n)},
        compiler_params=pltpu.CompilerParams(has_side_effects=EFFECT))(*lands, send0, recv0, *after)
    return outs[:4], list(outs[4:4 + n]), outs[-1]


def _gather_far(lands, send1, recv1, send2, recv2, after, name):
    n = len(lands)

    def body(*refs):
        ins = refs[:n]
        send1, recv1, send2, recv2 = refs[n:n + 4]
        token = refs[-1]
        x, y, c = _position()
        for wi in range(n):
            for sent, got in zip(_relays(ins[wi], wi, send2, recv2, (x, y, c), False), _relays(ins[wi], wi, send2, recv2, (x, y, c), True)):
                sent.wait_send()
                got.wait_recv()
            far = ins[wi].at[_block_of(1 - x, 1 - y, c)]
            pltpu.make_async_remote_copy(src_ref=far, dst_ref=far, send_sem=send1.at[3 * wi + 2], recv_sem=recv1.at[3 * wi + 2],
                                         device_id=(x, y, 1 - c), device_id_type=MESH).start()
        token[...] = jnp.zeros_like(token)

    outs = pl.pallas_call(
        body, name=name, in_specs=[HBM] * n + [SEM] * 4 + [ANY] * len(after), out_specs=(*[HBM] * n, VMEM_SPEC),
        out_shape=(*[_hbm(a) for a in lands], TOKEN), input_output_aliases={i: i for i in range(n)},
        compiler_params=pltpu.CompilerParams(has_side_effects=EFFECT))(*lands, send1, recv1, send2, recv2, *after)
    return list(outs[:n]), outs[-1]


def _gather_wait(lands, send0, recv0, send1, recv1, send2, recv2, after, name, relay):
    n = len(lands)

    def body(*refs):
        ins = refs[:n]
        send0, recv0, send1, recv1, send2, recv2 = refs[n:n + 6]
        x, y, c = _position()
        mine = _block_of(x, y, c)
        sibling = (x, y, 1 - c)
        peers = [sibling, (1 - x, y, c), (x, 1 - y, c), (1 - x, 1 - y, c)][:3 if relay else 4]
        chips = [(1 - x, y), (x, 1 - y), (1 - x, 1 - y)]
        for wi in range(n):
            own = ins[wi].at[mine]
            for k, peer in enumerate(peers):
                pltpu.make_async_remote_copy(src_ref=own, dst_ref=own, send_sem=send0.at[4 * wi + k], recv_sem=recv0.at[4 * wi + k],
                                             device_id=peer, device_id_type=MESH).wait_send()
            theirs = ins[wi].at[_block_of(*sibling)]
            pltpu.make_async_remote_copy(src_ref=theirs, dst_ref=theirs, send_sem=send0.at[4 * wi], recv_sem=recv0.at[4 * wi],
                                         device_id=sibling, device_id_type=MESH).wait_recv()
            for j, chip in enumerate(chips):
                sent = ins[wi].at[_block_of(*chip, c)]
                got = ins[wi].at[_block_of(*chip, 1 - c)]
                pltpu.make_async_remote_copy(src_ref=sent, dst_ref=sent, send_sem=send1.at[3 * wi + j], recv_sem=recv1.at[3 * wi + j],
                                             device_id=sibling, device_id_type=MESH).wait_send()
                pltpu.make_async_remote_copy(src_ref=got, dst_ref=got, send_sem=send1.at[3 * wi + j], recv_sem=recv1.at[3 * wi + j],
                                             device_id=sibling, device_id_type=MESH).wait_recv()

    outs = pl.pallas_call(
        body, name=name, in_specs=[HBM] * n + [SEM] * 6 + [ANY], out_specs=tuple([HBM] * n),
        out_shape=tuple(_hbm(a) for a in lands), input_output_aliases={i: i for i in range(n)},
        compiler_params=pltpu.CompilerParams(has_side_effects=EFFECT))(*lands, send0, recv0, send1, recv1, send2, recv2, after)
    return list(outs)


def _plan_direct(me):
    return [(_block_of(*_flip(me, k)), k - 1, _flip(me, k)) for k in range(1, N_DEV)]


def _plan_sibling(me):
    x, y, c = me
    return [(_block_of(ci // 2, ci % 2, 1 - c), ci, (x, y, 1 - c)) for ci in range(4)]


def _plan_chips(me):
    x, y, c = me
    out = []
    for k in range(1, 4):
        px, py = (1 - x if k & 2 else x), (1 - y if k & 1 else y)
        out.append((2 * px + py, k - 1, (px, py, c)))
    return out


def _exchange_start(blocks, plan, name):
    n = len(blocks)
    slots = len(plan((0, 0, 0)))

    def body(*refs):
        srcs, lands = refs[:n], refs[n:2 * n]
        send, recv = refs[2 * n], refs[2 * n + 1]
        token = refs[-1]
        for wi in range(n):
            for block, slot, peer in plan(_position()):
                pltpu.make_async_remote_copy(
                    src_ref=srcs[wi].at[block], dst_ref=lands[wi].at[slot], send_sem=send.at[slots * wi + slot],
                    recv_sem=recv.at[slots * wi + slot], device_id=peer, device_id_type=MESH).start()
        token[...] = jnp.zeros_like(token)

    zones = [_landing((slots,) + b.shape[1:], b.dtype) for b in blocks]
    outs = pl.pallas_call(
        body, name=name, in_specs=[HBM] * (2 * n), out_specs=(SEM, SEM, *[HBM] * (2 * n), VMEM_SPEC),
        out_shape=(pltpu.SemaphoreType.DMA((slots * n,)), pltpu.SemaphoreType.DMA((slots * n,)), *[_hbm(a) for a in blocks],
                   *[_hbm(z) for z in zones], TOKEN),
        input_output_aliases={i: 2 + i for i in range(2 * n)},
        compiler_params=pltpu.CompilerParams(has_side_effects=EFFECT))(
            *[pltpu.with_memory_space_constraint(b, pltpu.HBM) for b in blocks], *zones)
    return outs[0], outs[1], list(outs[2:2 + n]), list(outs[2 + n:2 + 2 * n]), outs[-1]


def _exchange_wait(groups, plan, after, name):
    flat, counts = [], []
    for send, recv, blocks, zones in groups:
        flat += [*blocks, *zones, send, recv]
        counts.append(len(blocks))
    slots = len(plan((0, 0, 0)))

    def body(*refs):
        pos = 0
        for n in counts:
            srcs, lands = refs[pos:pos + n], refs[pos + n:pos + 2 * n]
            send, recv = refs[pos + 2 * n], refs[pos + 2 * n + 1]
            pos += 2 * n + 2
            for wi in range(n):
                for block, slot, peer in plan(_position()):
                    cp = pltpu.make_async_remote_copy(
                        src_ref=srcs[wi].at[block], dst_ref=lands[wi].at[slot], send_sem=send.at[slots * wi + slot],
                        recv_sem=recv.at[slots * wi + slot], device_id=peer, device_id_type=MESH)
                    cp.wait_send()
                    cp.wait_recv()

    in_specs, out_specs, out_shape, aliases = [], [], [], {}
    i = 0
    for n, (send, recv, blocks, zones) in zip(counts, groups):
        for a in (*blocks, *zones):
            aliases[i] = len(out_shape)
            in_specs.append(HBM)
            out_specs.append(HBM)
            out_shape.append(_hbm(a))
            i += 1
        in_specs += [SEM, SEM]
        i += 2
    outs = pl.pallas_call(
        body, name=name, in_specs=in_specs + [ANY], out_specs=tuple(out_specs), out_shape=tuple(out_shape),
        input_output_aliases=aliases, compiler_params=pltpu.CompilerParams(has_side_effects=EFFECT))(*flat, after)
    res, pos = [], 0
    for n in counts:
        res.append((list(outs[pos:pos + n]), list(outs[pos + n:pos + 2 * n])))
        pos += 2 * n
    return res


def _sibling_sum(blocks, zone, core, name):
    _, R, C = zone.shape
    rt = next(r for r in (R, R // 2, R // 4, 128, 64) if R % r == 0 and r % 16 == 0 and r * C <= 4 * 1024 * 1024)

    def body(core_ref, own_ref, z_ref, o_ref):
        o_ref[...] = (own_ref[...].astype(F32) + z_ref[...].astype(F32)).astype(o_ref.dtype)

    grid_spec = pltpu.PrefetchScalarGridSpec(
        num_scalar_prefetch=1, grid=(4, R // rt),
        in_specs=[pl.BlockSpec((None, rt, C), lambda ci, i, core_ref: (2 * ci + core_ref[0], i, 0)),
                  pl.BlockSpec((None, rt, C), lambda ci, i, core_ref: (ci, i, 0))],
        out_specs=pl.BlockSpec((None, rt, C), lambda ci, i, core_ref: (ci, i, 0)))
    return pl.pallas_call(body, grid_spec=grid_spec, out_shape=jax.ShapeDtypeStruct(zone.shape, zone.dtype),
                          compiler_params=_params("parallel", "parallel"), name=name)(core, blocks, zone)


def _adamw(w, g, m, v):
    m = ADAM_B1 * m + (1.0 - ADAM_B1) * g
    v = ADAM_B2 * v + (1.0 - ADAM_B2) * (g * g)
    m_hat = m / (1.0 - ADAM_B1 ** ADAM_STEP)
    v_hat = v / (1.0 - ADAM_B2 ** ADAM_STEP)
    delta = -ADAM_LR * (m_hat / (jnp.sqrt(v_hat) + ADAM_EPS) + ADAM_WD * w)
    return delta, m, v


def _reduce_adamw(blocks, zone, mine, w, m, v, name):
    R, C = w.shape
    rt = next(r for r in (256, 128, 64) if R % r == 0 and r * C <= 512 * 1024)

    def body(mine_ref, own_ref, z_ref, w_ref, m_ref, v_ref, g_out, d_out, m_out, v_out):
        g = own_ref[...].astype(F32)
        for s in range(zone.shape[0]):
            g = g + z_ref[s].astype(F32)
        delta, m2, v2 = _adamw(w_ref[...], g, m_ref[...], v_ref[...])
        g_out[...] = g
        d_out[...] = delta
        m_out[...] = m2
        v_out[...] = v2

    spec = pl.BlockSpec((rt, C), lambda i, mine_ref: (i, 0))
    grid_spec = pltpu.PrefetchScalarGridSpec(
        num_scalar_prefetch=1, grid=(R // rt,),
        in_specs=[pl.BlockSpec((None, rt, C), lambda i, mine_ref: (mine_ref[0], i, 0)),
                  pl.BlockSpec((zone.shape[0], rt, C), lambda i, mine_ref: (0, i, 0)), spec, spec, spec],
        out_specs=[spec] * 4)
    return pl.pallas_call(body, grid_spec=grid_spec, out_shape=[jax.ShapeDtypeStruct((R, C), F32)] * 4,
                          compiler_params=_params("parallel"), name=name)(mine, blocks, zone, w, m, v)


def _small_step(part, w, m, v, after):
    R, C = part.shape

    def body(part_ref, w_ref, m_ref, v_ref, *rest):
        g_out, d_out, m_out, v_out, gath, send, recv = rest[len(after):]
        me = _position()
        gath[_block_of(*me)] = part_ref[...]

        def copy(k, slot):
            return pltpu.make_async_remote_copy(
                src_ref=part_ref, dst_ref=gath.at[slot], send_sem=send.at[k - 1], recv_sem=recv.at[k - 1],
                device_id=_flip(me, k), device_id_type=MESH)

        sent = [copy(k, _block_of(*me)) for k in range(1, N_DEV)]
        for cp in sent:
            cp.start()
        for k in range(1, N_DEV):
            copy(k, _block_of(*_flip(me, k))).wait_recv()
        for cp in sent:
            cp.wait_send()
        g = gath[0]
        for s in range(1, N_DEV):
            g = g + gath[s]
        delta, m2, v2 = _adamw(w_ref[...], g, m_ref[...], v_ref[...])
        g_out[...] = g
        d_out[...] = delta
        m_out[...] = m2
        v_out[...] = v2

    vm = pl.BlockSpec(memory_space=pltpu.VMEM)
    return pl.pallas_call(
        body, in_specs=[vm] * 4 + [ANY] * len(after), out_specs=[vm] * 4, out_shape=[jax.ShapeDtypeStruct((R, C), F32)] * 4,
        scratch_shapes=[pltpu.VMEM((N_DEV, R, C), F32), pltpu.SemaphoreType.DMA((7,)), pltpu.SemaphoreType.DMA((7,))],
        name="small_step")(part, w, m, v, *after)


COLUMN_SHARDED = ("w_in", "w_sb_out", "w_ca_out", "w_ffn_in", "w_ple_in")
ROW_SHARDED = ("w_mix_out", "w_ffn_out", "w_ple_gate")
BIG = COLUMN_SHARDED + ROW_SHARDED
SMALL = ("g_mix", "g_ffn", "g_ple", "g_final", "rel_bias")
WEIGHTS = ("w_in", "w_sb_out", "w_ca_out", "w_mix_out", "rel_bias", "g_mix", "g_ffn", "g_ple", "g_final",
           "w_ffn_in", "w_ffn_out", "w_ple_in", "w_ple_gate")


def _pack_small(t, D):
    rows = [t[n].reshape(1, D) for n in SMALL[:4]]
    rb = t["rel_bias"].reshape(1, -1)
    rows.append(jnp.pad(rb, ((0, 0), (0, D - rb.shape[1]))))
    return jnp.concatenate(rows + [jnp.zeros((8 - len(rows), D), F32)], axis=0)


def _unpack_small(a, like):
    out = {n: a[i].reshape(like[n].shape) for i, n in enumerate(SMALL[:4])}
    out["rel_bias"] = a[4, :like["rel_bias"].size].reshape(like["rel_bias"].shape)
    return out


GATHER_GROUPS = (("w_in_a",), ("w_in_b",), ("w_sb_out", "w_ca_out", "w_mix_out"), ("w_ffn_in",), ("w_ffn_out", "w_ple_gate", "w_ple_in"))
FORWARD_AFTER = ("norm1", "mm_in_a", "mm_in", "attention", "mm_mix")
RELAYED = (False, False, True, True, True)
GRAD_GROUPS = (("w_ple_in", "w_ple_gate"), ("w_ffn_out",), ("w_ffn_in",), ("w_mix_out", "w_sb_out", "w_ca_out"), ("w_in",))


class _Exchange:
    def __init__(self, shards):
        me = _position()
        self.mine = _block_of(*me)
        self.chip = jnp.reshape(2 * me[0] + me[1], (1,)).astype(jnp.int32)
        self.core = jnp.reshape(me[2], (1,)).astype(jnp.int32)
        self.device = jnp.reshape(self.mine, (1,)).astype(jnp.int32)
        self.shapes = {n: ((N_DEV * s.shape[0], s.shape[1]) if n in ROW_SHARDED else (N_DEV,) + s.shape) for n, s in shards.items()}
        self.tokens = []
        self.ready = {}
        self.gathers = []
        for gi, names in enumerate(GATHER_GROUPS):
            lands = [lax.dynamic_update_slice(lax.empty((N_DEV,) + shards[n].shape, BF16), shards[n][None], (self.mine, 0, 0))
                     for n in names]
            behind = self.tokens[-1] if self.tokens else shards[names[0]]
            send0, recv0, lands, token = _gather_start(lands, behind, f"gather_start_{gi}", RELAYED[gi])
            self.tokens.append(token)
            self.gathers.append(dict(names=names, lands=lands, sems=(send0, recv0), token=token))
        self.grads = {}
        self.exchanges = []

    def pending(self):
        tokens, self.tokens = self.tokens, []
        return tokens

    def stage(self, tag, *made):
        gi = FORWARD_AFTER.index(tag)
        gth = self.gathers[gi]
        sems, lands, token = _gather_forward(gth["lands"], *gth["sems"], made + tuple(self.tokens), f"gather_forward_{gi}", RELAYED[gi])
        gth.update(lands=lands, sems=gth["sems"] + tuple(sems), token=token)
        self.tokens.append(token)

    def weight(self, name, after=None):
        if name not in self.ready:
            gi = next(i for i, names in enumerate(GATHER_GROUPS) if name in names)
            gth = self.gathers[gi]
            after = gth["token"] if after is None else after
            if RELAYED[gi]:
                gth["lands"], after = _gather_far(gth["lands"], *gth["sems"][2:], after if isinstance(after, tuple) else (after,), f"gather_far_{gi}")
            for n, a in zip(gth["names"], _gather_wait(gth["lands"], *gth["sems"], after, f"gather_wait_{gi}", RELAYED[gi])):
                self.ready[n] = a.reshape(self.shapes[n])
        return self.ready[name]

    def grad(self, name, blocks):
        self.grads[name] = blocks if name in COLUMN_SHARDED else blocks.reshape((N_DEV, -1, blocks.shape[-1]))
        names = next(names for names in GRAD_GROUPS if name in names)
        if not all(n in self.grads for n in names):
            return
        blocks = [self.grads[n] for n in names]
        if names == GRAD_GROUPS[-1]:
            send, recv, blocks, zones, token = _exchange_start(blocks, _plan_sibling, "pair_start_" + names[0])
            self.pair = (send, recv, blocks, zones)
        else:
            send, recv, blocks, zones, token = _exchange_start(blocks, _plan_direct, "exchange_start_" + names[0])
            self.exchanges.append(dict(names=names, state=(send, recv, blocks, zones), plan=_plan_direct, own=self.device))
        self.tokens.append(token)

    def pair_done(self, after):
        names = GRAD_GROUPS[-1]
        (blocks, zones), = _exchange_wait([self.pair], _plan_sibling, after, "pair_wait_" + names[0])
        blocks = [_sibling_sum(b, z, self.core, "pair_sum_" + n) for n, b, z in zip(names, blocks, zones)]
        send, recv, blocks, zones, token = _exchange_start(blocks, _plan_chips, "exchange_start_" + names[0])
        self.exchanges.append(dict(names=names, state=(send, recv, blocks, zones), plan=_plan_chips, own=self.chip))
        self.tokens.append(token)

    def collect(self, which, after, name):
        sel = [e for e in self.exchanges if GRAD_GROUPS.index(e["names"]) in which]
        out = {}
        for e, (blocks, zones) in zip(sel, _exchange_wait([e["state"] for e in sel], sel[0]["plan"], after, name)):
            out.update({n: (b, e["own"], z) for n, b, z in zip(e["names"], blocks, zones)})
        return out


def kernel(x, p, w_in, w_sb_out, w_ca_out, w_mix_out, rel_bias, g_mix, g_ffn, g_ple, g_final, w_ffn_in, w_ffn_out, w_ple_in, w_ple_gate, loss_target, m_w_in, m_w_sb_out, m_w_ca_out, m_w_mix_out, m_rel_bias, m_g_mix, m_g_ffn, m_g_ple, m_g_final, m_w_ffn_in, m_w_ffn_out, m_w_ple_in, m_w_ple_gate, v_w_in, v_w_sb_out, v_w_ca_out, v_w_mix_out, v_rel_bias, v_g_mix, v_g_ffn, v_g_ple, v_g_final, v_w_ffn_in, v_w_ffn_out, v_w_ple_in, v_w_ple_gate):
    wts = dict(w_in=w_in, w_sb_out=w_sb_out, w_ca_out=w_ca_out, w_mix_out=w_mix_out, rel_bias=rel_bias, g_mix=g_mix, g_ffn=g_ffn,
               g_ple=g_ple, g_final=g_final, w_ffn_in=w_ffn_in, w_ffn_out=w_ffn_out, w_ple_in=w_ple_in, w_ple_gate=w_ple_gate)
    mom = dict(w_in=m_w_in, w_sb_out=m_w_sb_out, w_ca_out=m_w_ca_out, w_mix_out=m_w_mix_out, rel_bias=m_rel_bias, g_mix=m_g_mix,
               g_ffn=m_g_ffn, g_ple=m_g_ple, g_final=m_g_final, w_ffn_in=m_w_ffn_in, w_ffn_out=m_w_ffn_out, w_ple_in=m_w_ple_in,
               w_ple_gate=m_w_ple_gate)
    var = dict(w_in=v_w_in, w_sb_out=v_w_sb_out, w_ca_out=v_w_ca_out, w_mix_out=v_w_mix_out, rel_bias=v_rel_bias, g_mix=v_g_mix,
               g_ffn=v_g_ffn, g_ple=v_g_ple, g_final=v_g_final, w_ffn_in=v_w_ffn_in, w_ffn_out=v_w_ffn_out, w_ple_in=v_w_ple_in,
               w_ple_gate=v_w_ple_gate)
    T, D = x.shape[1], x.shape[2]
    shard = {n: wts[n].reshape(wts[n].shape[-2:]) for n in BIG}
    bf = {n: _cast_bf16(shard[n], "cast_" + n) for n in BIG}
    half = bf["w_in"].shape[0] // 2
    bf["w_in_a"], bf["w_in_b"] = bf["w_in"][:half], bf.pop("w_in")[half:]
    comm = _Exchange(bf)
    g = dict(g_mix=g_mix.reshape(1, D), g_ffn=g_ffn.reshape(1, D), g_ple=g_ple.reshape(1, D), g_final=g_final.reshape(1, D),
             rel_bias=rel_bias.reshape(rel_bias.shape[-2:]))

    loss, grad_x, dsmall = _local_step(x.reshape(T, D), p.reshape(T, -1), loss_target.reshape(T, D), comm, g)
    loss = lax.psum(loss[0, 0], ("x", "y", "c"))

    grad, delta, new_m, new_v = {}, {}, {}, {}

    def update(parts):
        done = []
        for n, (blocks, own, zone) in parts.items():
            outs = _reduce_adamw(blocks, zone, own, shard[n], mom[n].reshape(shard[n].shape), var[n].reshape(shard[n].shape), "adamw_" + n)
            grad[n], delta[n], new_m[n], new_v[n] = [o.reshape(wts[n].shape) for o in outs]
            done.append(outs[0])
        return done

    done = update(comm.collect(range(len(GRAD_GROUPS) - 1), grad_x, "exchange_wait_rest"))
    outs = _small_step(_pack_small(dsmall, D), _pack_small(wts, D), _pack_small(mom, D), _pack_small(var, D), done)
    for dst, a in zip((grad, delta, new_m, new_v), outs):
        dst.update(_unpack_small(a, wts))
    update(comm.collect([len(GRAD_GROUPS) - 1], outs[0], "exchange_wait_w_in"))

    return (loss, grad_x.reshape(x.shape), *[grad[n] for n in WEIGHTS], *[delta[n] for n in WEIGHTS],
            *[new_m[n] for n in WEIGHTS], *[new_v[n] for n in WEIGHTS])
```
